```python
import jax
import jax.numpy as jnp
from jax import lax
import numpy as np

D_MODEL = 1024
BATCH = 16
SEQ = 4096
DEPTH = 1

CHUNK = 64
N_META = 16
Q_BLOCK = 128
FOX_HEADS = 8
FOX_HEAD_DIM = 64
FOX_WIDTH = FOX_HEADS * FOX_HEAD_DIM
HG_HEADS = 4
HG_KEY_DIM = 128
HG_VAL_DIM = 128
HG_KWIDTH = HG_HEADS * HG_KEY_DIM
HG_VWIDTH = HG_HEADS * HG_VAL_DIM
N_BRANCH = 2
D_FF = 2816
CONV_WIDTH = 3
EPS = 1e-6
SPLIT_SIZES = (FOX_WIDTH, FOX_WIDTH, FOX_WIDTH, FOX_HEADS,
               HG_KWIDTH, HG_KWIDTH, HG_VWIDTH, HG_VWIDTH,
               D_MODEL, D_MODEL)
IN_COLS = sum(SPLIT_SIZES)
SPLIT_POINTS = tuple(int(v) for v in np.cumsum(SPLIT_SIZES)[:-1])

kernel_name = 'hybrid_hgrn2_fox_gated_block'


def _rms(x, gain):
    xf = x.astype(jnp.float32)
    y = xf * lax.rsqrt(jnp.mean(xf * xf, axis=-1, keepdims=True) + EPS)
    return (y * gain.astype(jnp.float32)).astype(x.dtype)


def _fox_attention(q, k, v, f_logit):
    b, nh, l, dh = q.shape
    n_blk = -(-l // Q_BLOCK)
    lp = n_blk * Q_BLOCK
    pad4 = ((0, 0), (0, 0), (0, lp - l), (0, 0))
    qp = jnp.pad(q, pad4)
    kp = jnp.pad(k, pad4)
    vp = jnp.pad(v, pad4)
    logf = jax.nn.log_sigmoid(f_logit.astype(jnp.float32))
    cum = jnp.cumsum(jnp.pad(logf, ((0, 0), (0, 0), (0, lp - l))), axis=-1)
    scale = dh ** -0.5
    outs = []
    for i in range(n_blk):
        start, stop = i * Q_BLOCK, (i + 1) * Q_BLOCK
        qb = qp[:, :, start:stop]
        s = jnp.einsum('bhqd,bhkd->bhqk', qb, kp[:, :, :stop],
                       preferred_element_type=jnp.float32) * scale
        s = s + (cum[:, :, start:stop, None] - cum[:, :, None, :stop])
        causal = jnp.arange(stop)[None, :] <= jnp.arange(start, stop)[:, None]
        s = jnp.where(causal, s, -jnp.inf)
        p = jax.nn.softmax(s, axis=-1)
        outs.append(jnp.einsum('bhqk,bhkd->bhqd', p.astype(vp.dtype), vp[:, :, :stop]))
    out = jnp.concatenate(outs, axis=2)
    return out[:, :, :l]


def _hgrn2(q, log_f, k, v):
    b, l, nh, dk = q.shape
    dv = v.shape[-1]
    lead = (-N_META) % CHUNK
    tail = (-(l + lead)) % CHUNK
    n_chunk = (l + lead + tail) // CHUNK
    pad = ((0, 0), (lead, tail), (0, 0), (0, 0))

    def to_chunks(a):
        a = jnp.pad(a.astype(jnp.float32), pad)
        a = a.reshape(b, n_chunk, CHUNK, nh, a.shape[-1])
        return jnp.transpose(a, (2, 0, 1, 3, 4))

    qc, fc, kc, vc = to_chunks(q), to_chunks(log_f), to_chunks(k), to_chunks(v)

    def step(s, inp):
        q_t, f_t, k_t, v_t = inp
        s = jnp.exp(f_t)[..., None] * s + k_t[..., None] * v_t[..., None, :]
        return s, jnp.einsum('bnhk,bnhkv->bnhv', q_t, s)

    s0 = jnp.zeros((b, n_chunk, nh, dk, dv), jnp.float32)
    s_local, o_local = lax.scan(step, s0, (qc, fc, kc, vc))
    g = jnp.cumsum(fc, axis=0)

    def carry_step(s, inp):
        s_loc, g_tot = inp
        return jnp.exp(g_tot)[..., None] * s + s_loc, s

    _, s_prev = lax.scan(carry_step, jnp.zeros((b, nh, dk, dv), jnp.float32),
                         (jnp.moveaxis(s_local, 1, 0), jnp.moveaxis(g[-1], 1, 0)))
    s_prev = jnp.moveaxis(s_prev, 0, 1)
    o = o_local + jnp.einsum('cbnhk,bnhkv->cbnhv', qc * jnp.exp(g), s_prev)
    o = jnp.transpose(o, (1, 2, 0, 3, 4)).reshape(b, n_chunk * CHUNK, nh, dv)
    return o[:, lead:lead + l]


def _causal_dwconv(u, w, bias):
    c = u.shape[-1]
    y = lax.conv_general_dilated(u, w[:, None, :].astype(u.dtype), window_strides=(1,),
                                 padding=[(CONV_WIDTH - 1, 0)],
                                 dimension_numbers=('NWC', 'WIO', 'NWC'),
                                 feature_group_count=c)
    return y + bias.astype(u.dtype)


def _fwd_setup_inputs(seed: int = 0) -> dict:
    key = jax.random.key(seed)
    ks = jax.random.split(key, 18)
    f32 = jnp.float32

    def nrm(k, shape, scale):
        return jax.random.normal(k, shape, f32) * scale

    return {
        'x': nrm(ks[0], (BATCH, SEQ, D_MODEL), 1.0),
        'meta_tokens': nrm(ks[1], (N_META, D_MODEL), 1.0),
        'norm1_gain': 1.0 + nrm(ks[2], (DEPTH, D_MODEL), 0.02),
        'w_in': nrm(ks[3], (DEPTH, D_MODEL, IN_COLS), D_MODEL ** -0.5),
        'fox_b_f': 3.0 + nrm(ks[4], (DEPTH, FOX_HEADS), 0.5),
        'q_norm_gain': 1.0 + nrm(ks[5], (DEPTH, FOX_HEAD_DIM), 0.02),
        'k_norm_gain': 1.0 + nrm(ks[6], (DEPTH, FOX_HEAD_DIM), 0.02),
        'hg_lb_logits': nrm(ks[7], (DEPTH + 1, HG_KWIDTH), 0.1),
        'hg_out_gain': 1.0 + nrm(ks[8], (DEPTH, HG_VAL_DIM), 0.02),
        'w_branch_a': nrm(ks[9], (DEPTH, HG_VWIDTH, D_MODEL), HG_VWIDTH ** -0.5),
        'w_branch_b': nrm(ks[10], (DEPTH, FOX_WIDTH, D_MODEL), FOX_WIDTH ** -0.5),
        'w_out': nrm(ks[11], (DEPTH, D_MODEL, D_MODEL), D_MODEL ** -0.5),
        'norm2_gain': 1.0 + nrm(ks[12], (DEPTH, D_MODEL), 0.02),
        'w_up': nrm(ks[13], (DEPTH, D_MODEL, 2 * D_FF), D_MODEL ** -0.5),
        'conv_w': nrm(ks[14], (DEPTH, CONV_WIDTH, 2 * D_FF), CONV_WIDTH ** -0.5),
        'conv_b': nrm(ks[15], (DEPTH, 2 * D_FF), 0.02),
        'w_down': nrm(ks[16], (DEPTH, D_FF, D_MODEL), D_FF ** -0.5),
    }


def _fwd_reference(x, meta_tokens, norm1_gain, w_in, fox_b_f, q_norm_gain, k_norm_gain,
              hg_lb_logits, hg_out_gain, w_branch_a, w_branch_b, w_out, norm2_gain,
              w_up, conv_w, conv_b, w_down):
    b = x.shape[0]
    meta = jnp.broadcast_to(meta_tokens[None].astype(x.dtype), (b, N_META, D_MODEL))
    h = jnp.concatenate([meta, x], axis=1)
    l = h.shape[1]
    lower_bounds = jnp.cumsum(jax.nn.softmax(hg_lb_logits.astype(jnp.float32), axis=0), axis=0)

    for layer in range(DEPTH):
        xn = _rms(h, norm1_gain[layer])
        proj = xn @ w_in[layer]
        (fq, fk, fv, ff, hq, hf, hi, hg, gate_a, gate_b) = jnp.split(proj, SPLIT_POINTS, axis=-1)

        fq = _rms(fq.reshape(b, l, FOX_HEADS, FOX_HEAD_DIM), q_norm_gain[layer])
        fk = _rms(fk.reshape(b, l, FOX_HEADS, FOX_HEAD_DIM), k_norm_gain[layer])
        fv = fv.reshape(b, l, FOX_HEADS, FOX_HEAD_DIM)
        f_logit = jnp.transpose(ff + fox_b_f[layer].astype(ff.dtype), (0, 2, 1))
        o_fox = _fox_attention(jnp.transpose(fq, (0, 2, 1, 3)), jnp.transpose(fk, (0, 2, 1, 3)),
                               jnp.transpose(fv, (0, 2, 1, 3)), f_logit)
        o_fox = jnp.transpose(o_fox, (0, 2, 1, 3)).reshape(b, l, FOX_WIDTH)

        lb = lower_bounds[layer].reshape(HG_HEADS, HG_KEY_DIM)
        hf32 = hf.astype(jnp.float32).reshape(b, l, HG_HEADS, HG_KEY_DIM)
        log_f = jnp.log(lb + (1.0 - lb) * jax.nn.sigmoid(hf32))
        k_in = (1.0 - lb) * jax.nn.sigmoid(-hf32)
        o_hg = _hgrn2(hq.reshape(b, l, HG_HEADS, HG_KEY_DIM), log_f, k_in,
                      hi.reshape(b, l, HG_HEADS, HG_VAL_DIM)).astype(h.dtype)
        o_hg = _rms(o_hg, hg_out_gain[layer]) * jax.nn.silu(hg.reshape(b, l, HG_HEADS, HG_VAL_DIM))
        o_hg = o_hg.reshape(b, l, HG_VWIDTH)

        y_a = o_hg @ w_branch_a[layer]
        y_b = o_fox @ w_branch_b[layer]
        merged = jax.nn.sigmoid(gate_a) * y_a + jax.nn.sigmoid(gate_b) * y_b
        h = h + merged @ w_out[layer]

        hn = _rms(h, norm2_gain[layer])
        u = _causal_dwconv(hn @ w_up[layer], conv_w[layer], conv_b[layer])
        u_gate, u_val = jnp.split(u, 2, axis=-1)
        h = h + (jax.nn.silu(u_gate) * u_val) @ w_down[layer]

    return h[:, N_META:]


import jax as _jax
import jax.numpy as _jnp

TWIN_FORMAT = 'train_step'
FWD_PARAMS = ['x', 'meta_tokens', 'norm1_gain', 'w_in', 'fox_b_f', 'q_norm_gain', 'k_norm_gain', 'hg_lb_logits', 'hg_out_gain', 'w_branch_a', 'w_branch_b', 'w_out', 'norm2_gain', 'w_up', 'conv_w', 'conv_b', 'w_down']
TWIN_WEIGHTS = ['meta_tokens', 'norm1_gain', 'w_in', 'fox_b_f', 'q_norm_gain', 'k_norm_gain', 'hg_lb_logits', 'hg_out_gain', 'w_branch_a', 'w_branch_b', 'w_out', 'norm2_gain', 'w_up', 'conv_w', 'conv_b', 'w_down']
TWIN_DIFF_INPUT = 'x'
TWIN_INPUTS = ['x', 'meta_tokens', 'norm1_gain', 'w_in', 'fox_b_f', 'q_norm_gain', 'k_norm_gain', 'hg_lb_logits', 'hg_out_gain', 'w_branch_a', 'w_branch_b', 'w_out', 'norm2_gain', 'w_up', 'conv_w', 'conv_b', 'w_down', 'loss_target', 'm_meta_tokens', 'm_norm1_gain', 'm_w_in', 'm_fox_b_f', 'm_q_norm_gain', 'm_k_norm_gain', 'm_hg_lb_logits', 'm_hg_out_gain', 'm_w_branch_a', 'm_w_branch_b', 'm_w_out', 'm_norm2_gain', 'm_w_up', 'm_conv_w', 'm_conv_b', 'm_w_down', 'v_meta_tokens', 'v_norm1_gain', 'v_w_in', 'v_fox_b_f', 'v_q_norm_gain', 'v_k_norm_gain', 'v_hg_lb_logits', 'v_hg_out_gain', 'v_w_branch_a', 'v_w_branch_b', 'v_w_out', 'v_norm2_gain', 'v_w_up', 'v_conv_w', 'v_conv_b', 'v_w_down']
TWIN_OUTPUTS = ['loss', 'grad_x', 'grad_meta_tokens', 'grad_norm1_gain', 'grad_w_in', 'grad_fox_b_f', 'grad_q_norm_gain', 'grad_k_norm_gain', 'grad_hg_lb_logits', 'grad_hg_out_gain', 'grad_w_branch_a', 'grad_w_branch_b', 'grad_w_out', 'grad_norm2_gain', 'grad_w_up', 'grad_conv_w', 'grad_conv_b', 'grad_w_down', 'delta_meta_tokens', 'delta_norm1_gain', 'delta_w_in', 'delta_fox_b_f', 'delta_q_norm_gain', 'delta_k_norm_gain', 'delta_hg_lb_logits', 'delta_hg_out_gain', 'delta_w_branch_a', 'delta_w_branch_b', 'delta_w_out', 'delta_norm2_gain', 'delta_w_up', 'delta_conv_w', 'delta_conv_b', 'delta_w_down', 'new_m_meta_tokens', 'new_m_norm1_gain', 'new_m_w_in', 'new_m_fox_b_f', 'new_m_q_norm_gain', 'new_m_k_norm_gain', 'new_m_hg_lb_logits', 'new_m_hg_out_gain', 'new_m_w_branch_a', 'new_m_w_branch_b', 'new_m_w_out', 'new_m_norm2_gain', 'new_m_w_up', 'new_m_conv_w', 'new_m_conv_b', 'new_m_w_down', 'new_v_meta_tokens', 'new_v_norm1_gain', 'new_v_w_in', 'new_v_fox_b_f', 'new_v_q_norm_gain', 'new_v_k_norm_gain', 'new_v_hg_lb_logits', 'new_v_hg_out_gain', 'new_v_w_branch_a', 'new_v_w_branch_b', 'new_v_w_out', 'new_v_norm2_gain', 'new_v_w_up', 'new_v_conv_w', 'new_v_conv_b', 'new_v_w_down']
TWIN_LEAF_KINDS = {'loss': 'loss', 'grad_x': 'grad_x', 'grad_meta_tokens': 'grad_w', 'grad_norm1_gain': 'grad_w', 'grad_w_in': 'grad_w', 'grad_fox_b_f': 'grad_w', 'grad_q_norm_gain': 'grad_w', 'grad_k_norm_gain': 'grad_w', 'grad_hg_lb_logits': 'grad_w', 'grad_hg_out_gain': 'grad_w', 'grad_w_branch_a': 'grad_w', 'grad_w_branch_b': 'grad_w', 'grad_w_out': 'grad_w', 'grad_norm2_gain': 'grad_w', 'grad_w_up': 'grad_w', 'grad_conv_w': 'grad_w', 'grad_conv_b': 'grad_w', 'grad_w_down': 'grad_w', 'delta_meta_tokens': 'delta_w', 'delta_norm1_gain': 'delta_w', 'delta_w_in': 'delta_w', 'delta_fox_b_f': 'delta_w', 'delta_q_norm_gain': 'delta_w', 'delta_k_norm_gain': 'delta_w', 'delta_hg_lb_logits': 'delta_w', 'delta_hg_out_gain': 'delta_w', 'delta_w_branch_a': 'delta_w', 'delta_w_branch_b': 'delta_w', 'delta_w_out': 'delta_w', 'delta_norm2_gain': 'delta_w', 'delta_w_up': 'delta_w', 'delta_conv_w': 'delta_w', 'delta_conv_b': 'delta_w', 'delta_w_down': 'delta_w', 'new_m_meta_tokens': 'new_m', 'new_m_norm1_gain': 'new_m', 'new_m_w_in': 'new_m', 'new_m_fox_b_f': 'new_m', 'new_m_q_norm_gain': 'new_m', 'new_m_k_norm_gain': 'new_m', 'new_m_hg_lb_logits': 'new_m', 'new_m_hg_out_gain': 'new_m', 'new_m_w_branch_a': 'new_m', 'new_m_w_branch_b': 'new_m', 'new_m_w_out': 'new_m', 'new_m_norm2_gain': 'new_m', 'new_m_w_up': 'new_m', 'new_m_conv_w': 'new_m', 'new_m_conv_b': 'new_m', 'new_m_w_down': 'new_m', 'new_v_meta_tokens': 'new_v', 'new_v_norm1_gain': 'new_v', 'new_v_w_in': 'new_v', 'new_v_fox_b_f': 'new_v', 'new_v_q_norm_gain': 'new_v', 'new_v_k_norm_gain': 'new_v', 'new_v_hg_lb_logits': 'new_v', 'new_v_hg_out_gain': 'new_v', 'new_v_w_branch_a': 'new_v', 'new_v_w_branch_b': 'new_v', 'new_v_w_out': 'new_v', 'new_v_norm2_gain': 'new_v', 'new_v_w_up': 'new_v', 'new_v_conv_w': 'new_v', 'new_v_conv_b': 'new_v', 'new_v_w_down': 'new_v'}


def _forward(args):
    return _fwd_reference(*[args[k] for k in FWD_PARAMS])


def _output_shape():
    out = _jax.eval_shape(lambda: _forward(_fwd_setup_inputs(0)))
    return out.shape, out.dtype

N_MICROBATCH = 1
ADAM_LR = 0.001
ADAM_B1 = 0.9
ADAM_B2 = 0.999
ADAM_EPS = 1e-08
ADAM_WD = 0.01
ADAM_STEP = 10
PER_EXAMPLE_BATCH_AXIS = {'x': 0, 'loss_target': 0}
SHARED_INPUTS = []
_WEIGHT_DTYPES = {'meta_tokens': _jnp.float32, 'norm1_gain': _jnp.float32, 'w_in': _jnp.float32, 'fox_b_f': _jnp.float32, 'q_norm_gain': _jnp.float32, 'k_norm_gain': _jnp.float32, 'hg_lb_logits': _jnp.float32, 'hg_out_gain': _jnp.float32, 'w_branch_a': _jnp.float32, 'w_branch_b': _jnp.float32, 'w_out': _jnp.float32, 'norm2_gain': _jnp.float32, 'w_up': _jnp.float32, 'conv_w': _jnp.float32, 'conv_b': _jnp.float32, 'w_down': _jnp.float32}
MOMENT_SCALE = {'meta_tokens': 1.295628e-02, 'norm1_gain': 9.971846e+00, 'w_in': 2.432864e-01, 'fox_b_f': 8.250345e+01, 'q_norm_gain': 1.211417e+01, 'k_norm_gain': 1.220492e+01, 'hg_lb_logits': 2.195326e-01, 'hg_out_gain': 5.360727e+01, 'w_branch_a': 2.722544e-01, 'w_branch_b': 1.298179e-01, 'w_out': 2.958312e-01, 'norm2_gain': 5.047685e+01, 'w_up': 3.262914e-01, 'conv_w': 6.899825e+00, 'conv_b': 5.966554e+00, 'w_down': 4.780480e-01}


def _to_microbatches(a, axis):
    t = _jnp.moveaxis(a, axis, 0)
    t = t.reshape((N_MICROBATCH, t.shape[0] // N_MICROBATCH) + t.shape[1:])
    return _jnp.moveaxis(t, 1, axis + 1)


def setup_inputs(seed: int = 0) -> dict:
    inp = _fwd_setup_inputs(seed)
    key = _jax.random.fold_in(_jax.random.key(seed), 7919)
    shape, _ = _output_shape()
    out = dict(inp)
    out["loss_target"] = _jax.random.normal(_jax.random.fold_in(key, 0), shape, _jnp.float32)
    for i, name in enumerate(TWIN_WEIGHTS):
        w = inp[name].astype(_jnp.float32)
        if MOMENT_SCALE is None:
            s = _jnp.sqrt(_jnp.mean(_jnp.square(w)) + 1e-30)
        else:
            s = MOMENT_SCALE[name]
        km, kv = _jax.random.split(_jax.random.fold_in(key, i + 1))
        out[name] = w
        out["m_" + name] = s * _jax.random.normal(km, w.shape, _jnp.float32)
        out["v_" + name] = (s * s) * _jax.random.uniform(kv, w.shape, _jnp.float32, 0.5, 1.5)
    if N_MICROBATCH > 1:
        for name, axis in PER_EXAMPLE_BATCH_AXIS.items():
            out[name] = _to_microbatches(out[name], axis)
    return {'x': out['x'], 'meta_tokens': out['meta_tokens'], 'norm1_gain': out['norm1_gain'], 'w_in': out['w_in'], 'fox_b_f': out['fox_b_f'], 'q_norm_gain': out['q_norm_gain'], 'k_norm_gain': out['k_norm_gain'], 'hg_lb_logits': out['hg_lb_logits'], 'hg_out_gain': out['hg_out_gain'], 'w_branch_a': out['w_branch_a'], 'w_branch_b': out['w_branch_b'], 'w_out': out['w_out'], 'norm2_gain': out['norm2_gain'], 'w_up': out['w_up'], 'conv_w': out['conv_w'], 'conv_b': out['conv_b'], 'w_down': out['w_down'], 'loss_target': out['loss_target'], 'm_meta_tokens': out['m_meta_tokens'], 'm_norm1_gain': out['m_norm1_gain'], 'm_w_in': out['m_w_in'], 'm_fox_b_f': out['m_fox_b_f'], 'm_q_norm_gain': out['m_q_norm_gain'], 'm_k_norm_gain': out['m_k_norm_gain'], 'm_hg_lb_logits': out['m_hg_lb_logits'], 'm_hg_out_gain': out['m_hg_out_gain'], 'm_w_branch_a': out['m_w_branch_a'], 'm_w_branch_b': out['m_w_branch_b'], 'm_w_out': out['m_w_out'], 'm_norm2_gain': out['m_norm2_gain'], 'm_w_up': out['m_w_up'], 'm_conv_w': out['m_conv_w'], 'm_conv_b': out['m_conv_b'], 'm_w_down': out['m_w_down'], 'v_meta_tokens': out['v_meta_tokens'], 'v_norm1_gain': out['v_norm1_gain'], 'v_w_in': out['v_w_in'], 'v_fox_b_f': out['v_fox_b_f'], 'v_q_norm_gain': out['v_q_norm_gain'], 'v_k_norm_gain': out['v_k_norm_gain'], 'v_hg_lb_logits': out['v_hg_lb_logits'], 'v_hg_out_gain': out['v_hg_out_gain'], 'v_w_branch_a': out['v_w_branch_a'], 'v_w_branch_b': out['v_w_branch_b'], 'v_w_out': out['v_w_out'], 'v_norm2_gain': out['v_norm2_gain'], 'v_w_up': out['v_w_up'], 'v_conv_w': out['v_conv_w'], 'v_conv_b': out['v_conv_b'], 'v_w_down': out['v_w_down']}


def _loss(weights, diff, rest, loss_target):
    with _jax.named_scope("forward"):
        args = {**rest, TWIN_DIFF_INPUT: diff, **{k: w.astype(_WEIGHT_DTYPES[k]) for k, w in weights.items()}}
        y = _forward(args)
    with _jax.named_scope("loss_head"):
        err = _jnp.square(y.astype(_jnp.float32) - loss_target)
        return 0.5 * _jnp.sum(_jnp.mean(err, axis=-1)) if err.ndim else 0.5 * err


def _adamw(w, g, m, v):
    m = ADAM_B1 * m + (1.0 - ADAM_B1) * g
    v = ADAM_B2 * v + (1.0 - ADAM_B2) * _jnp.square(g)
    m_hat = m / (1.0 - ADAM_B1 ** ADAM_STEP)
    v_hat = v / (1.0 - ADAM_B2 ** ADAM_STEP)
    delta = -ADAM_LR * (m_hat / (_jnp.sqrt(v_hat) + ADAM_EPS) + ADAM_WD * w)
    return delta, m, v


def reference(x, meta_tokens, norm1_gain, w_in, fox_b_f, q_norm_gain, k_norm_gain, hg_lb_logits, hg_out_gain, w_branch_a, w_branch_b, w_out, norm2_gain, w_up, conv_w, conv_b, w_down, loss_target, m_meta_tokens, m_norm1_gain, m_w_in, m_fox_b_f, m_q_norm_gain, m_k_norm_gain, m_hg_lb_logits, m_hg_out_gain, m_w_branch_a, m_w_branch_b, m_w_out, m_norm2_gain, m_w_up, m_conv_w, m_conv_b, m_w_down, v_meta_tokens, v_norm1_gain, v_w_in, v_fox_b_f, v_q_norm_gain, v_k_norm_gain, v_hg_lb_logits, v_hg_out_gain, v_w_branch_a, v_w_branch_b, v_w_out, v_norm2_gain, v_w_up, v_conv_w, v_conv_b, v_w_down):
    given = dict(x=x, meta_tokens=meta_tokens, norm1_gain=norm1_gain, w_in=w_in, fox_b_f=fox_b_f, q_norm_gain=q_norm_gain, k_norm_gain=k_norm_gain, hg_lb_logits=hg_lb_logits, hg_out_gain=hg_out_gain, w_branch_a=w_branch_a, w_branch_b=w_branch_b, w_out=w_out, norm2_gain=norm2_gain, w_up=w_up, conv_w=conv_w, conv_b=conv_b, w_down=w_down, loss_target=loss_target, m_meta_tokens=m_meta_tokens, m_norm1_gain=m_norm1_gain, m_w_in=m_w_in, m_fox_b_f=m_fox_b_f, m_q_norm_gain=m_q_norm_gain, m_k_norm_gain=m_k_norm_gain, m_hg_lb_logits=m_hg_lb_logits, m_hg_out_gain=m_hg_out_gain, m_w_branch_a=m_w_branch_a, m_w_branch_b=m_w_branch_b, m_w_out=m_w_out, m_norm2_gain=m_norm2_gain, m_w_up=m_w_up, m_conv_w=m_conv_w, m_conv_b=m_conv_b, m_w_down=m_w_down, v_meta_tokens=v_meta_tokens, v_norm1_gain=v_norm1_gain, v_w_in=v_w_in, v_fox_b_f=v_fox_b_f, v_q_norm_gain=v_q_norm_gain, v_k_norm_gain=v_k_norm_gain, v_hg_lb_logits=v_hg_lb_logits, v_hg_out_gain=v_hg_out_gain, v_w_branch_a=v_w_branch_a, v_w_branch_b=v_w_branch_b, v_w_out=v_w_out, v_norm2_gain=v_norm2_gain, v_w_up=v_w_up, v_conv_w=v_conv_w, v_conv_b=v_conv_b, v_w_down=v_w_down)
    weights = {n: given[n] for n in TWIN_WEIGHTS}
    shared = {n: given[n] for n in SHARED_INPUTS}
    per_example = {n: given[n] for n in ['x']}
    grad_fn = _jax.value_and_grad(_loss, argnums=(0, 1))

    def one_microbatch(ex, loss_target):
        ex = dict(ex)
        diff = ex.pop(TWIN_DIFF_INPUT)
        return grad_fn(weights, diff, {**shared, **ex}, loss_target)

    if N_MICROBATCH == 1:
        loss, (grad_w, grad_x) = one_microbatch(per_example, given["loss_target"])
    else:
        def body(carry, xs):
            loss_sum, grad_sum = carry
            l_k, (gw_k, gx_k) = one_microbatch(xs[0], xs[1])
            with _jax.named_scope("update"):
                return (loss_sum + l_k, _jax.tree.map(_jnp.add, grad_sum, gw_k)), gx_k

        init = (_jnp.zeros((), _jnp.float32), _jax.tree.map(_jnp.zeros_like, weights))
        (loss, grad_w), grad_x = _jax.lax.scan(body, init, (per_example, given["loss_target"]))
    with _jax.named_scope("update"):
        delta_w, new_m, new_v = {}, {}, {}
        for n in TWIN_WEIGHTS:
            delta_w[n], new_m[n], new_v[n] = _adamw(weights[n], grad_w[n], given["m_" + n], given["v_" + n])
    return (loss, grad_x, *[grad_w[n] for n in TWIN_WEIGHTS], *[delta_w[n] for n in TWIN_WEIGHTS],
            *[new_m[n] for n in TWIN_WEIGHTS], *[new_v[n] for n in TWIN_WEIGHTS])
```

```python
import functools

import jax
import jax.numpy as jnp
import numpy as np
from jax import lax
from jax.experimental import pallas as pl
from jax.experimental.pallas import tpu as pltpu

F32 = jnp.float32
BF16 = jnp.bfloat16
MXU_DTYPE = BF16
HIGHEST = lax.Precision.HIGHEST

D_MODEL = 1024
N_META = 16
LEAD = 48
ROW0 = LEAD + N_META
FOX_HEADS, FOX_DIM, FOX_W = 8, 64, 512
HG_HEADS, HG_DIM, HG_W = 4, 128, 512
D_FF = 2816
FF2 = 2 * D_FF
EPS = 1e-6
SUB = 16
LANES = 128
NEG = -1e30

ADAM_LR, ADAM_B1, ADAM_B2, ADAM_EPS, ADAM_WD, ADAM_STEP = 0.001, 0.9, 0.999, 1e-08, 0.01, 10

VMEM_LIMIT = 56 * 1024 * 1024

C_GA, C_GB = 0, 1
C_FQ, C_FK, C_FV, C_HQ, C_HF, C_HI, C_HG = 4, 5, 6, 7, 8, 9, 10


def _params(n_axes=1):
    return pltpu.CompilerParams(dimension_semantics=("arbitrary",) * n_axes, vmem_limit_bytes=VMEM_LIMIT)


def _pick(n, cands):
    for c in cands:
        if n % c == 0:
            return c
    raise ValueError(f"no tile for {n} among {cands}")


def _rowwise(fn, rows, consts, outs, reds, *, n_rows, tile, name):
    assert n_rows % tile == 0
    rows = [r if isinstance(r, tuple) else (r, r.shape[1], 0) for r in rows]
    nr, nc, no = len(rows), len(consts), len(outs)

    def body(*refs):
        i = pl.program_id(0)
        ins = [r[...] for r in refs[:nr + nc]]
        res = fn(i, *ins)
        res = res if isinstance(res, (tuple, list)) else (res,)
        for ref, v in zip(refs[nr + nc:nr + nc + no], res[:no]):
            ref[...] = v.astype(ref.dtype)
        red_refs = refs[nr + nc + no:]
        if red_refs:
            @pl.when(i == 0)
            def _():
                for ref in red_refs:
                    ref[...] = jnp.zeros_like(ref)
            for ref, v in zip(red_refs, res[no:]):
                ref[...] += v.astype(F32)

    in_specs = [pl.BlockSpec((tile, w), functools.partial(lambda i, j: (i, j), j=j)) for (_, w, j) in rows]
    in_specs += [pl.BlockSpec(c.shape, functools.partial(lambda i, nd: (0,) * nd, nd=c.ndim)) for c in consts]
    out_specs = [pl.BlockSpec((tile, w), lambda i: (i, 0)) for (w, _) in outs]
    out_specs += [pl.BlockSpec(s, functools.partial(lambda i, nd: (0,) * nd, nd=len(s))) for s in reds]
    out_shape = [jax.ShapeDtypeStruct((n_rows, w), dt) for (w, dt) in outs]
    out_shape += [jax.ShapeDtypeStruct(s, F32) for s in reds]
    return pl.pallas_call(
        body, name=name, grid=(n_rows // tile,), in_specs=in_specs, out_specs=out_specs, out_shape=out_shape,
        compiler_params=_params(1),
    )(*[r[0] for r in rows], *consts)


def _matmul(a, b, *, trans_a=False, out_dtype=F32, name):
    if trans_a:
        k, m = a.shape
    else:
        m, k = a.shape
    n = b.shape[1]
    assert b.shape[0] == k
    if trans_a:
        tm = _pick(m, (1408, 1024, 512, 256, 128))
        tk = _pick(k, (1056, 768, 528, 384, 128))
    else:
        tm = _pick(m, (528, 384, 256, 128))
        tk = k if k <= 1024 else _pick(k, (1408, 1024, 512))
    tn = _pick(n, (1408, 1024, 512, 256, 128))
    nk = k // tk
    dims = (((0,), (0,)), ((), ())) if trans_a else (((1,), (0,)), ((), ()))

    def body(a_ref, b_ref, o_ref, acc_ref):
        part = lax.dot_general(a_ref[...], b_ref[...], dims, preferred_element_type=F32)
        if nk == 1:
            o_ref[...] = part.astype(o_ref.dtype)
        else:
            kk = pl.program_id(2)

            @pl.when(kk == 0)
            def _():
                acc_ref[...] = part

            @pl.when(kk > 0)
            def _():
                acc_ref[...] += part

            @pl.when(kk == nk - 1)
            def _():
                o_ref[...] = acc_ref[...].astype(o_ref.dtype)

    a_spec = pl.BlockSpec((tk, tm), lambda i, j, kk: (kk, i)) if trans_a else pl.BlockSpec((tm, tk), lambda i, j, kk: (i, kk))
    return pl.pallas_call(
        body, name=name, grid=(m // tm, n // tn, nk),
        in_specs=[a_spec, pl.BlockSpec((tk, tn), lambda i, j, kk: (kk, j))],
        out_specs=pl.BlockSpec((tm, tn), lambda i, j, kk: (i, j)),
        out_shape=jax.ShapeDtypeStruct((m, n), out_dtype),
        scratch_shapes=[pltpu.VMEM((tm, tn) if nk > 1 else (8, LANES), F32)],
        compiler_params=_params(3),
    )(a, b)


def _sigmoid(x):
    return 1.0 / (1.0 + jnp.exp(-x))


def _silu(x):
    return x * _sigmoid(x)


def _log_sigmoid(x):
    return jnp.minimum(x, 0.0) - jnp.log(1.0 + jnp.exp(-jnp.abs(x)))


def _rms(x, gain):
    return x * lax.rsqrt(jnp.mean(x * x, axis=-1, keepdims=True) + EPS) * gain


def _group_matrix(width, group):
    g = (np.arange(width)[:, None] // group == np.arange(LANES)[None, :]).astype(np.float32)
    return jnp.asarray(g), jnp.asarray(g.T.copy())


def _group_rms(x, gain, gmat, gmat_t, group):
    ms = jnp.dot(x * x, gmat, precision=HIGHEST, preferred_element_type=F32) * (1.0 / group)
    rstd = lax.rsqrt(ms + EPS)
    return x * jnp.dot(rstd, gmat_t, precision=HIGHEST, preferred_element_type=F32) * gain


class _Layout:
    def __init__(self, batch, seq):
        self.batch, self.seq = batch, seq
        self.l_real = N_META + seq
        self.lp = -(-(LEAD + self.l_real) // LANES) * LANES
        self.n = batch * self.lp
        self.tile = _pick(self.lp, (384, 128))

    def valid(self, i, tile):
        per = self.lp // tile
        r = lax.rem(i, per) * tile + lax.broadcasted_iota(jnp.int32, (tile, 1), 0)
        return (r >= LEAD) & (r < LEAD + self.l_real)


def _cumsum_rows(x, lay, *, reverse, name):
    t = LANES
    nt = lay.lp // t
    c = x.shape[1]

    def body(x_ref, o_ref, carry):
        j = pl.program_id(1)

        @pl.when(j == 0)
        def _():
            carry[...] = jnp.zeros_like(carry)

        r = lax.broadcasted_iota(jnp.int32, (t, t), 0)
        q = lax.broadcasted_iota(jnp.int32, (t, t), 1)
        tri = jnp.where((q >= r) if reverse else (q <= r), 1.0, 0.0).astype(F32)
        xs = x_ref[...]
        out = jnp.dot(tri, xs, precision=HIGHEST, preferred_element_type=F32) + carry[0:1, :]
        o_ref[...] = out
        carry[...] = jnp.broadcast_to(carry[0:1, :] + jnp.sum(xs, axis=0, keepdims=True), carry.shape)

    def idx(b, j):
        return (b * nt + (nt - 1 - j if reverse else j), 0)

    return pl.pallas_call(
        body, name=name, grid=(lay.batch, nt),
        in_specs=[pl.BlockSpec((t, c), idx)], out_specs=pl.BlockSpec((t, c), idx),
        out_shape=jax.ShapeDtypeStruct(x.shape, F32),
        scratch_shapes=[pltpu.VMEM((8, c), F32)],
        compiler_params=_params(2),
    )(x)


def _group_cumsum(x, tile, *, reverse):
    r = lax.rem(lax.broadcasted_iota(jnp.int32, (tile, 1), 0), SUB)
    s = 1
    while s < SUB:
        if reverse:
            x = x + jnp.where(r < SUB - s, pltpu.roll(x, tile - s, 0), 0.0)
        else:
            x = x + jnp.where(r >= s, pltpu.roll(x, s, 0), 0.0)
        s *= 2
    return x


def _fox_blocks(lay):
    bq = _pick(lay.lp, (384, 128))
    return bq, LANES


def _fox_scores(q, k, cum_t, cum_s, row0, col0, bq, bk):
    s = lax.dot_general(q, k, (((1,), (1,)), ((), ())), preferred_element_type=F32) * (FOX_DIM ** -0.5)
    s = s + (cum_t - cum_s)
    rows = row0 + lax.broadcasted_iota(jnp.int32, (bq, bk), 0)
    cols = col0 + lax.broadcasted_iota(jnp.int32, (bq, bk), 1)
    return jnp.where((cols <= rows) & (cols >= LEAD), s, NEG)


def _per_pair_cols(a, lay):
    return jnp.transpose(a.reshape(lay.n, FOX_HEADS // 2, 2), (1, 0, 2))


def _per_pair_rows(a, lay):
    return jnp.transpose(a.reshape(lay.batch, lay.lp, FOX_HEADS), (0, 2, 1)).reshape(lay.batch * FOX_HEADS, 1, lay.lp)


def _from_pair_cols(a, lay):
    return jnp.transpose(a, (1, 0, 2)).reshape(lay.n, FOX_HEADS)


def _fox_fwd(q, k, v, cum, cum_t, lay):
    bq, bk = _fox_blocks(lay)
    nq = lay.lp // bq
    pairs = FOX_HEADS // 2

    def body(q_ref, k_ref, v_ref, cum_ref, cumt_ref, o_ref, lse_ref):
        qi = pl.program_id(2)
        row0 = qi * bq
        rows = row0 + lax.broadcasted_iota(jnp.int32, (bq, 1), 0)
        valid = (rows >= LEAD) & (rows < LEAD + lay.l_real)
        nkb = (row0 + bq) // bk
        for hh in range(2):
            lanes = slice(hh * FOX_DIM, (hh + 1) * FOX_DIM)
            qh = q_ref[:, lanes]
            cum_col = cum_ref[0, :, hh:hh + 1]

            def step(kb, carry):
                m, l, acc = carry
                c0 = pl.multiple_of(kb * bk, bk)
                kblk = k_ref[pl.ds(c0, bk), lanes]
                vblk = v_ref[pl.ds(c0, bk), lanes]
                cum_row = cumt_ref[hh, :, pl.ds(c0, bk)]
                s = _fox_scores(qh, kblk, cum_col, cum_row, row0, c0, bq, bk)
                m_new = jnp.maximum(m, jnp.max(s, axis=1, keepdims=True))
                alpha = jnp.exp(m - m_new)
                p = jnp.exp(s - m_new)
                l = alpha * l + jnp.sum(p, axis=1, keepdims=True)
                acc = alpha * acc + jnp.dot(p.astype(vblk.dtype), vblk, preferred_element_type=F32)
                return m_new, l, acc

            m0 = jnp.full((bq, 1), NEG, F32)
            m, l, acc = lax.fori_loop(0, nkb, step, (m0, jnp.zeros((bq, 1), F32), jnp.zeros((bq, FOX_DIM), F32)))
            o_ref[:, lanes] = jnp.where(valid, acc / l, 0.0).astype(o_ref.dtype)
            lse_ref[0, :, hh:hh + 1] = m + jnp.log(l)

    return pl.pallas_call(
        body, name="fox_fwd", grid=(lay.batch, pairs, nq),
        in_specs=[
            pl.BlockSpec((bq, LANES), lambda b, p, i: (b * nq + i, p)),
            pl.BlockSpec((lay.lp, LANES), lambda b, p, i: (b, p)),
            pl.BlockSpec((lay.lp, LANES), lambda b, p, i: (b, p)),
            pl.BlockSpec((1, bq, 2), lambda b, p, i: (p, b * nq + i, 0)),
            pl.BlockSpec((2, 1, lay.lp), lambda b, p, i: (b * pairs + p, 0, 0)),
        ],
        out_specs=[
            pl.BlockSpec((bq, LANES), lambda b, p, i: (b * nq + i, p)),
            pl.BlockSpec((1, bq, 2), lambda b, p, i: (p, b * nq + i, 0)),
        ],
        out_shape=[jax.ShapeDtypeStruct((lay.n, FOX_W), MXU_DTYPE), jax.ShapeDtypeStruct((pairs, lay.n, 2), F32)],
        compiler_params=_params(3),
    )(q, k, v, cum, cum_t)


def _fox_bwd_q(q, k, v, do, cum, cum_t, lse, delta, lay):
    bq, bk = _fox_blocks(lay)
    nq = lay.lp // bq
    pairs = FOX_HEADS // 2

    def body(q_ref, k_ref, v_ref, do_ref, cum_ref, cumt_ref, lse_ref, dl_ref, dq_ref):
        qi = pl.program_id(2)
        row0 = qi * bq
        nkb = (row0 + bq) // bk
        for hh in range(2):
            lanes = slice(hh * FOX_DIM, (hh + 1) * FOX_DIM)
            qh = q_ref[:, lanes]
            doh = do_ref[:, lanes]
            cum_col = cum_ref[0, :, hh:hh + 1]
            dl_col = dl_ref[0, :, hh:hh + 1]
            lse_col = lse_ref[0, :, hh:hh + 1]

            def step(kb, dq):
                c0 = pl.multiple_of(kb * bk, bk)
                kblk = k_ref[pl.ds(c0, bk), lanes]
                vblk = v_ref[pl.ds(c0, bk), lanes]
                cum_row = cumt_ref[hh, :, pl.ds(c0, bk)]
                s = _fox_scores(qh, kblk, cum_col, cum_row, row0, c0, bq, bk)
                p = jnp.exp(s - lse_col)
                dp = lax.dot_general(doh, vblk, (((1,), (1,)), ((), ())), preferred_element_type=F32)
                ds = p * (dp - dl_col)
                return dq + jnp.dot(ds.astype(kblk.dtype), kblk, preferred_element_type=F32)

            dq = lax.fori_loop(0, nkb, step, jnp.zeros((bq, FOX_DIM), F32))
            dq_ref[:, lanes] = dq * (FOX_DIM ** -0.5)

    return pl.pallas_call(
        body, name="fox_bwd_q", grid=(lay.batch, pairs, nq),
        in_specs=[
            pl.BlockSpec((bq, LANES), lambda b, p, i: (b * nq + i, p)),
            pl.BlockSpec((lay.lp, LANES), lambda b, p, i: (b, p)),
            pl.BlockSpec((lay.lp, LANES), lambda b, p, i: (b, p)),
            pl.BlockSpec((bq, LANES), lambda b, p, i: (b * nq + i, p)),
            pl.BlockSpec((1, bq, 2), lambda b, p, i: (p, b * nq + i, 0)),
            pl.BlockSpec((2, 1, lay.lp), lambda b, p, i: (b * pairs + p, 0, 0)),
            pl.BlockSpec((1, bq, 2), lambda b, p, i: (p, b * nq + i, 0)),
            pl.BlockSpec((1, bq, 2), lambda b, p, i: (p, b * nq + i, 0)),
        ],
        out_specs=pl.BlockSpec((bq, LANES), lambda b, p, i: (b * nq + i, p)),
        out_shape=jax.ShapeDtypeStruct((lay.n, FOX_W), F32),
        compiler_params=_params(3),
    )(q, k, v, do, cum, cum_t, lse, delta)


def _fox_bwd_kv(q, k, v, do, cum, cum_t, lse_t, delta_t, lay):
    bk, bq = _fox_blocks(lay)
    nkb = lay.lp // bk
    nqb = lay.lp // bq
    pairs = FOX_HEADS // 2

    def body(q_ref, k_ref, v_ref, do_ref, cum_ref, cumt_ref, lset_ref, dlt_ref, dk_ref, dv_ref, dc_ref):
        ki = pl.program_id(2)
        key0 = ki * bk
        q_first = key0 // bq
        for hh in range(2):
            lanes = slice(hh * FOX_DIM, (hh + 1) * FOX_DIM)
            kh = k_ref[:, lanes]
            vh = v_ref[:, lanes]
            cum_key = cum_ref[0, :, hh:hh + 1]

            def step(qb, carry):
                dk, dv, dc = carry
                c0 = pl.multiple_of(qb * bq, LANES)
                qblk = q_ref[pl.ds(c0, bq), lanes]
                doblk = do_ref[pl.ds(c0, bq), lanes]
                cum_q = cumt_ref[hh, :, pl.ds(c0, bq)]
                lse_q = lset_ref[hh, :, pl.ds(c0, bq)]
                dl_q = dlt_ref[hh, :, pl.ds(c0, bq)]
                st = lax.dot_general(kh, qblk, (((1,), (1,)), ((), ())), preferred_element_type=F32) * (FOX_DIM ** -0.5)
                st = st + (cum_q - cum_key)
                keys = key0 + lax.broadcasted_iota(jnp.int32, (bk, bq), 0)
                qs = c0 + lax.broadcasted_iota(jnp.int32, (bk, bq), 1)
                st = jnp.where((keys <= qs) & (keys >= LEAD), st, NEG)
                pt = jnp.exp(st - lse_q)
                dv = dv + jnp.dot(pt.astype(doblk.dtype), doblk, preferred_element_type=F32)
                dpt = lax.dot_general(vh, doblk, (((1,), (1,)), ((), ())), preferred_element_type=F32)
                dst = pt * (dpt - dl_q)
                dk = dk + jnp.dot(dst.astype(qblk.dtype), qblk, preferred_element_type=F32)
                dc = dc - jnp.sum(dst, axis=1, keepdims=True)
                return dk, dv, dc

            z = jnp.zeros((bk, FOX_DIM), F32)
            dk, dv, dc = lax.fori_loop(q_first, nqb, step, (z, z, jnp.zeros((bk, 1), F32)))
            dk_ref[:, lanes] = dk * (FOX_DIM ** -0.5)
            dv_ref[:, lanes] = dv
            dc_ref[0, :, hh:hh + 1] = dc

    whole = lambda b, p, i: (b, p)
    tvec = lambda b, p, i: (b * pairs + p, 0, 0)
    return pl.pallas_call(
        body, name="fox_bwd_kv", grid=(lay.batch, pairs, nkb),
        in_specs=[
            pl.BlockSpec((lay.lp, LANES), whole),
            pl.BlockSpec((bk, LANES), lambda b, p, i: (b * nkb + i, p)),
            pl.BlockSpec((bk, LANES), lambda b, p, i: (b * nkb + i, p)),
            pl.BlockSpec((lay.lp, LANES), whole),
            pl.BlockSpec((1, bk, 2), lambda b, p, i: (p, b * nkb + i, 0)),
            pl.BlockSpec((2, 1, lay.lp), tvec),
            pl.BlockSpec((2, 1, lay.lp), tvec),
            pl.BlockSpec((2, 1, lay.lp), tvec),
        ],
        out_specs=[
            pl.BlockSpec((bk, LANES), lambda b, p, i: (b * nkb + i, p)),
            pl.BlockSpec((bk, LANES), lambda b, p, i: (b * nkb + i, p)),
            pl.BlockSpec((1, bk, 2), lambda b, p, i: (p, b * nkb + i, 0)),
        ],
        out_shape=[jax.ShapeDtypeStruct((lay.n, FOX_W), F32), jax.ShapeDtypeStruct((lay.n, FOX_W), F32),
                   jax.ShapeDtypeStruct((pairs, lay.n, 2), F32)],
        compiler_params=_params(3),
    )(q, k, v, do, cum, cum_t, lse_t, delta_t)


def _hg_tile(lay):
    return _pick(lay.lp, (384, 128))


def _hgrn_fwd(proj, kk, gl, lay):
    t = _hg_tile(lay)
    nt = lay.lp // t
    nsc = t // SUB

    def body(q_ref, k_ref, g_ref, v_ref, o_ref, st_ref, state, sub_rows):
        @pl.when(pl.program_id(1) == 0)
        def _():
            state[...] = jnp.zeros_like(state)

        rowi = lax.broadcasted_iota(jnp.int32, (SUB, 1), 0)

        def sub(sc, _):
            r0 = pl.multiple_of(sc * SUB, SUB)
            sub_rows[0] = k_ref[pl.ds(r0, SUB), :]
            sub_rows[1] = g_ref[pl.ds(r0, SUB), :]
            sub_rows[2] = v_ref[pl.ds(r0, SUB), :]
            for h in range(HG_HEADS):
                lanes = slice(h * HG_DIM, (h + 1) * HG_DIM)
                q16 = q_ref[pl.ds(r0, SUB), lanes]
                k16 = sub_rows[0, :, lanes]
                g16 = sub_rows[1, :, lanes]
                v16 = sub_rows[2, :, lanes]
                g_end = sub_rows[1, SUB - 1:SUB, lanes]
                s_prev = state[h]
                st_ref[sc, h] = s_prev
                o = lax.dot_general((q16 * jnp.exp(g16)).astype(MXU_DTYPE), s_prev.astype(MXU_DTYPE),
                                    (((1,), (1,)), ((), ())), preferred_element_type=F32)
                for s in range(SUB):
                    ks = sub_rows[0, s:s + 1, lanes]
                    gs = sub_rows[1, s:s + 1, lanes]
                    vs = sub_rows[2, s:s + 1, lanes]
                    w = q16 * jnp.exp(jnp.minimum(g16 - gs, 0.0)) * ks
                    a = jnp.where(rowi >= s, jnp.sum(w, axis=1, keepdims=True), 0.0)
                    o = o + a * vs
                o_ref[pl.ds(r0, SUB), lanes] = o
                kt = k16 * jnp.exp(g_end - g16)
                upd = lax.dot_general(v16.astype(MXU_DTYPE), kt.astype(MXU_DTYPE), (((0,), (0,)), ((), ())),
                                      preferred_element_type=F32)
                state[h] = jnp.exp(g_end) * s_prev + upd
            return 0

        lax.fori_loop(0, nsc, sub, 0)

    rows = lambda col: pl.BlockSpec((t, HG_W), functools.partial(lambda b, i, col: (b * nt + i, col), col=col))
    return pl.pallas_call(
        body, name="hgrn_fwd", grid=(lay.batch, nt),
        in_specs=[rows(C_HQ), rows(0), rows(0), rows(C_HI)],
        out_specs=[rows(0), pl.BlockSpec((nsc, HG_HEADS, HG_DIM, HG_DIM), lambda b, i: (b * nt + i, 0, 0, 0))],
        out_shape=[jax.ShapeDtypeStruct((lay.n, HG_W), F32),
                   jax.ShapeDtypeStruct((lay.n // SUB, HG_HEADS, HG_DIM, HG_DIM), F32)],
        scratch_shapes=[pltpu.VMEM((HG_HEADS, HG_DIM, HG_DIM), F32), pltpu.VMEM((3, SUB, HG_W), F32)],
        compiler_params=_params(2),
    )(proj, kk, gl, proj)


def _hgrn_bwd(proj, kk, gl, do, states, lay):
    t = _hg_tile(lay)
    nt = lay.lp // t
    nsc = t // SUB

    def body(q_ref, k_ref, g_ref, v_ref, do_ref, st_ref, dq_ref, dk_ref, dv_ref, dg_ref, dstate, sub_rows):
        @pl.when(pl.program_id(1) == 0)
        def _():
            dstate[...] = jnp.zeros_like(dstate)

        rowi = lax.broadcasted_iota(jnp.int32, (SUB, 1), 0)

        def sub(it, _):
            sc = nsc - 1 - it
            r0 = pl.multiple_of(sc * SUB, SUB)
            sub_rows[0] = k_ref[pl.ds(r0, SUB), :]
            sub_rows[1] = g_ref[pl.ds(r0, SUB), :]
            sub_rows[2] = v_ref[pl.ds(r0, SUB), :]
            for h in range(HG_HEADS):
                lanes = slice(h * HG_DIM, (h + 1) * HG_DIM)
                q16 = q_ref[pl.ds(r0, SUB), lanes]
                k16 = sub_rows[0, :, lanes]
                g16 = sub_rows[1, :, lanes]
                v16 = sub_rows[2, :, lanes]
                do16 = do_ref[pl.ds(r0, SUB), lanes]
                g_end = sub_rows[1, SUB - 1:SUB, lanes]
                s_prev = st_ref[sc, h]
                ds_end = dstate[h]
                eg = jnp.exp(g16)
                ekt = jnp.exp(g_end - g16)
                e_end = jnp.exp(g_end)
                qt = q16 * eg
                kt = k16 * ekt
                ds_mx = ds_end.astype(MXU_DTYPE)
                dv = lax.dot_general(kt.astype(MXU_DTYPE), ds_mx, (((1,), (1,)), ((), ())), preferred_element_type=F32)
                dkt = jnp.dot(v16.astype(MXU_DTYPE), ds_mx, preferred_element_type=F32)
                dk = dkt * ekt
                ktdkt = kt * dkt
                dg_end = jnp.sum(ktdkt, axis=0, keepdims=True) + jnp.sum(s_prev * ds_end, axis=0, keepdims=True) * e_end
                dg = jnp.where(rowi == SUB - 1, dg_end, 0.0) - ktdkt
                dqt = jnp.dot(do16.astype(MXU_DTYPE), s_prev.astype(MXU_DTYPE), preferred_element_type=F32)
                dq = dqt * eg
                dg = dg + qt * dqt
                dstate[h] = e_end * ds_end + lax.dot_general(do16.astype(MXU_DTYPE), qt.astype(MXU_DTYPE),
                                                             (((0,), (0,)), ((), ())), preferred_element_type=F32)
                for s in range(SUB):
                    ks = sub_rows[0, s:s + 1, lanes]
                    gs = sub_rows[1, s:s + 1, lanes]
                    vs = sub_rows[2, s:s + 1, lanes]
                    live = rowi >= s
                    e = jnp.where(live, jnp.exp(jnp.minimum(g16 - gs, 0.0)), 0.0)
                    qe = q16 * e
                    a = jnp.sum(qe * ks, axis=1, keepdims=True)
                    da = jnp.where(live, jnp.sum(do16 * vs, axis=1, keepdims=True), 0.0)
                    dv_row = jnp.sum(a * do16, axis=0, keepdims=True)
                    t1 = da * qe
                    dk_row = jnp.sum(t1, axis=0, keepdims=True)
                    dq = dq + da * (e * ks)
                    is_s = rowi == s
                    dv = dv + jnp.where(is_s, dv_row, 0.0)
                    dk = dk + jnp.where(is_s, dk_row, 0.0)
                    dg = dg + t1 * ks - jnp.where(is_s, ks * dk_row, 0.0)
                dq_ref[pl.ds(r0, SUB), lanes] = dq
                dk_ref[pl.ds(r0, SUB), lanes] = dk
                dv_ref[pl.ds(r0, SUB), lanes] = dv
                dg_ref[pl.ds(r0, SUB), lanes] = dg
            return 0

        lax.fori_loop(0, nsc, sub, 0)

    def rows(col):
        return pl.BlockSpec((t, HG_W), functools.partial(lambda b, i, col: (b * nt + nt - 1 - i, col), col=col))

    out = jax.ShapeDtypeStruct((lay.n, HG_W), F32)
    return pl.pallas_call(
        body, name="hgrn_bwd", grid=(lay.batch, nt),
        in_specs=[rows(C_HQ), rows(0), rows(0), rows(C_HI), rows(0),
                  pl.BlockSpec((nsc, HG_HEADS, HG_DIM, HG_DIM), lambda b, i: (b * nt + nt - 1 - i, 0, 0, 0))],
        out_specs=[rows(0)] * 4, out_shape=[out] * 4,
        scratch_shapes=[pltpu.VMEM((HG_HEADS, HG_DIM, HG_DIM), F32), pltpu.VMEM((3, SUB, HG_W), F32)],
        compiler_params=_params(2),
    )(proj, kk, gl, proj, do, states)


CONV_COLS = 256


def _shift_down(x, halo, tile, by):
    out = pltpu.roll(x, by, 0)
    rowi = lax.broadcasted_iota(jnp.int32, (tile, 1), 0)
    for r in range(by):
        out = jnp.where(rowi == r, halo[8 - by + r:8 - by + r + 1, :], out)
    return out


def _shift_up(x, halo, tile, by):
    out = pltpu.roll(x, tile - by, 0)
    rowi = lax.broadcasted_iota(jnp.int32, (tile, 1), 0)
    for r in range(by):
        out = jnp.where(rowi == tile - by + r, halo[r:r + 1, :], out)
    return out


def _conv_specs(lay, tile, with_next):
    ncb = D_FF // CONV_COLS
    nblk8 = lay.n // 8
    per8 = tile // 8

    def tile_spec(off):
        return pl.BlockSpec((tile, CONV_COLS), functools.partial(lambda i, j, off: (i, j + off), off=off))

    def prev_spec(off):
        return pl.BlockSpec((8, CONV_COLS), functools.partial(lambda i, j, off: (jnp.maximum(i * per8 - 1, 0), j + off), off=off))

    def next_spec(off):
        return pl.BlockSpec((8, CONV_COLS),
                            functools.partial(lambda i, j, off: (jnp.minimum((i + 1) * per8, nblk8 - 1), j + off), off=off))

    def w_spec(off):
        return pl.BlockSpec((3, CONV_COLS), functools.partial(lambda i, j, off: (0, j + off), off=off))

    def b_spec(off):
        return pl.BlockSpec((1, CONV_COLS), functools.partial(lambda i, j, off: (0, j + off), off=off))

    return ncb, tile_spec, (next_spec if with_next else prev_spec), w_spec, b_spec


def _conv3(x, halo, w, b, tile):
    return w[0:1, :] * _shift_down(x, halo, tile, 2) + w[1:2, :] * _shift_down(x, halo, tile, 1) + w[2:3, :] * x + b


def _conv_act_fwd(u, conv_w, conv_b, lay):
    tile = lay.tile
    ncb, tile_spec, prev_spec, w_spec, b_spec = _conv_specs(lay, tile, False)

    def body(ug, uv, pg, pv, wg, wv, bg, bv, o_ref):
        cg = _conv3(ug[...], pg, wg, bg[...], tile)
        cv = _conv3(uv[...], pv, wv, bv[...], tile)
        o_ref[...] = (_silu(cg) * cv).astype(o_ref.dtype)

    return pl.pallas_call(
        body, name="conv_act_fwd", grid=(lay.n // tile, ncb),
        in_specs=[tile_spec(0), tile_spec(ncb), prev_spec(0), prev_spec(ncb), w_spec(0), w_spec(ncb), b_spec(0), b_spec(ncb)],
        out_specs=pl.BlockSpec((tile, CONV_COLS), lambda i, j: (i, j)),
        out_shape=jax.ShapeDtypeStruct((lay.n, D_FF), MXU_DTYPE),
        compiler_params=_params(2),
    )(u, u, u, u, conv_w, conv_w, conv_b, conv_b)


def _conv_act_bwd(u, dact, conv_w, conv_b, lay):
    tile = lay.tile
    ncb, tile_spec, prev_spec, w_spec, b_spec = _conv_specs(lay, tile, False)

    def body(ug, uv, pg, pv, wg, wv, bg, bv, da_ref, dg_ref, dv_ref, gwg, gwv, gbg, gbv):
        @pl.when(pl.program_id(1) == 0)
        def _():
            for r in (gwg, gwv, gbg, gbv):
                r[...] = jnp.zeros_like(r)

        xg, xv = ug[...], uv[...]
        cg = _conv3(xg, pg, wg, bg[...], tile)
        cv = _conv3(xv, pv, wv, bv[...], tile)
        da = da_ref[...].astype(F32)
        sg = _sigmoid(cg)
        dcv = da * (cg * sg)
        dcg = da * cv * (sg * (1.0 + cg * (1.0 - sg)))
        dg_ref[...] = dcg
        dv_ref[...] = dcv
        for x, halo, dc, gw, gb in ((xg, pg, dcg, gwg, gbg), (xv, pv, dcv, gwv, gbv)):
            gw[0, 0:1, :] += jnp.sum(dc * _shift_down(x, halo, tile, 2), axis=0, keepdims=True)
            gw[0, 1:2, :] += jnp.sum(dc * _shift_down(x, halo, tile, 1), axis=0, keepdims=True)
            gw[0, 2:3, :] += jnp.sum(dc * x, axis=0, keepdims=True)
            gb[0] += jnp.sum(dc, axis=0, keepdims=True)

    swap = lambda spec: pl.BlockSpec(spec.block_shape, functools.partial(lambda j, i, f: f(i, j), f=spec.index_map))
    col = lambda j, i: (i, j)
    red_w = pl.BlockSpec((1, 3, CONV_COLS), lambda j, i: (j, 0, 0))
    red_b = pl.BlockSpec((1, 1, CONV_COLS), lambda j, i: (j, 0, 0))
    outs = pl.pallas_call(
        body, name="conv_act_bwd", grid=(ncb, lay.n // tile),
        in_specs=[swap(s) for s in (tile_spec(0), tile_spec(ncb), prev_spec(0), prev_spec(ncb), w_spec(0), w_spec(ncb),
                                    b_spec(0), b_spec(ncb))] + [pl.BlockSpec((tile, CONV_COLS), col)],
        out_specs=[pl.BlockSpec((tile, CONV_COLS), col), pl.BlockSpec((tile, CONV_COLS), col), red_w, red_w, red_b, red_b],
        out_shape=[jax.ShapeDtypeStruct((lay.n, D_FF), F32), jax.ShapeDtypeStruct((lay.n, D_FF), F32),
                   jax.ShapeDtypeStruct((ncb, 3, CONV_COLS), F32), jax.ShapeDtypeStruct((ncb, 3, CONV_COLS), F32),
                   jax.ShapeDtypeStruct((ncb, 1, CONV_COLS), F32), jax.ShapeDtypeStruct((ncb, 1, CONV_COLS), F32)],
        compiler_params=_params(2),
    )(u, u, u, u, conv_w, conv_w, conv_b, conv_b, dact)
    dcg, dcv, gwg, gwv, gbg, gbv = outs
    unblock = lambda g: jnp.transpose(g, (1, 0, 2)).reshape(g.shape[1], D_FF)
    g_w = jnp.concatenate([unblock(gwg), unblock(gwv)], axis=1)
    g_b = jnp.concatenate([unblock(gbg), unblock(gbv)], axis=1)
    return dcg, dcv, g_w, g_b


def _conv_input_bwd(dcg, dcv, conv_w, lay):
    tile = lay.tile
    ncb, tile_spec, next_spec, w_spec, _ = _conv_specs(lay, tile, True)

    def body(dg, dv, ng, nv, wg, wv, og, ov):
        valid = lay.valid(pl.program_id(0), tile)
        for d, halo, w, o in ((dg, ng, wg, og), (dv, nv, wv, ov)):
            x = d[...]
            du = w[2:3, :] * x + w[1:2, :] * _shift_up(x, halo, tile, 1) + w[0:1, :] * _shift_up(x, halo, tile, 2)
            o[...] = jnp.where(valid, du, 0.0).astype(o.dtype)

    out = pl.BlockSpec((tile, CONV_COLS), lambda i, j: (i, j))
    dug, duv = pl.pallas_call(
        body, name="conv_input_bwd", grid=(lay.n // tile, ncb),
        in_specs=[tile_spec(0), tile_spec(0), next_spec(0), next_spec(0), w_spec(0), w_spec(ncb)],
        out_specs=[out, out],
        out_shape=[jax.ShapeDtypeStruct((lay.n, D_FF), MXU_DTYPE)] * 2,
        compiler_params=_params(2),
    )(dcg, dcv, dcg, dcv, conv_w, conv_w)
    return jnp.concatenate([dug, duv], axis=1)


def _loss_head(h1, mlp, target, lay):
    t = 64
    per = lay.lp // t
    nreal = lay.seq // t
    first = ROW0 // t

    def body(h_ref, m_ref, t_ref, loss_ref, dy_ref, dyb_ref):
        b, j = pl.program_id(0), pl.program_id(1)

        @pl.when((b == 0) & (j == 0))
        def _():
            loss_ref[...] = jnp.zeros_like(loss_ref)

        real = (j >= first) & (j < first + nreal)
        err = jnp.where(real, h_ref[...] + m_ref[...] - t_ref[...], 0.0)
        dy = err * (1.0 / D_MODEL)
        dy_ref[...] = dy
        dyb_ref[...] = dy.astype(dyb_ref.dtype)
        loss_ref[...] += 0.5 * jnp.sum(err * dy)

    rows = pl.BlockSpec((t, D_MODEL), lambda b, j: (b * per + j, 0))
    tgt = pl.BlockSpec((t, D_MODEL), lambda b, j: (b * nreal + jnp.clip(j - first, 0, nreal - 1), 0))
    return pl.pallas_call(
        body, name="loss_head", grid=(lay.batch, per),
        in_specs=[rows, rows, tgt],
        out_specs=[pl.BlockSpec((8, LANES), lambda b, j: (0, 0)), rows, rows],
        out_shape=[jax.ShapeDtypeStruct((8, LANES), F32), jax.ShapeDtypeStruct((lay.n, D_MODEL), F32),
                   jax.ShapeDtypeStruct((lay.n, D_MODEL), MXU_DTYPE)],
        compiler_params=_params(2),
    )(h1, mlp, target)


def _fox_prep(fq, fk, ff, gq, gk, bf, gmat, gmat_t, valid):
    q = _group_rms(fq, gq, gmat, gmat_t, FOX_DIM)
    k = _group_rms(fk, gk, gmat, gmat_t, FOX_DIM)
    logf = jnp.where(valid, _log_sigmoid(ff + bf), 0.0)
    return q, k, logf


def _hg_prep(hf, l0, l1):
    mx = jnp.maximum(l0, l1)
    e0, e1 = jnp.exp(l0 - mx), jnp.exp(l1 - mx)
    lb = e0 / (e0 + e1)
    lf = jnp.log(lb + (1.0 - lb) * _sigmoid(hf))
    kk = (1.0 - lb) * _sigmoid(-hf)
    return lf, kk


def _hg_post(o, hg, gain, gmat, gmat_t):
    return _group_rms(o, jnp.tile(gain, (1, HG_HEADS)), gmat, gmat_t, HG_DIM) * _silu(hg)


def _gate(ga, gb, ya, yb):
    return _sigmoid(ga) * ya + _sigmoid(gb) * yb


def _local_step(x, target, w, lay):
    n, tile = lay.n, lay.tile
    rw = functools.partial(_rowwise, n_rows=n, tile=tile)
    mx = lambda a: a.astype(MXU_DTYPE)

    w_in = w["w_in"]
    fq, fk, fv, ffw, hq, hf, hi, hg, ga, gb = jnp.split(w_in, list(np.cumsum([512, 512, 512, 8, 512, 512, 512, 512, 1024])), axis=1)
    w_main = mx(jnp.concatenate([ga, gb, fq, fk, fv, hq, hf, hi, hg], axis=1))
    w_ff = mx(jnp.pad(ffw, ((0, 0), (0, LANES - FOX_HEADS))))
    w_a, w_b, w_out, w_up, w_down = mx(w["w_branch_a"]), mx(w["w_branch_b"]), mx(w["w_out"]), mx(w["w_up"]), mx(w["w_down"])
    conv_w, conv_b = w["conv_w"].astype(F32), w["conv_b"].astype(F32)
    g1, g2 = w["norm1_gain"], w["norm2_gain"]
    gq, gk = jnp.tile(w["q_norm_gain"], (1, FOX_HEADS)), jnp.tile(w["k_norm_gain"], (1, FOX_HEADS))
    bf = jnp.pad(w["fox_b_f"], ((0, 0), (0, LANES - FOX_HEADS)))
    lb_logits, hg_gain = w["hg_lb_logits"], w["hg_out_gain"]
    gm64, gm64_t = _group_matrix(FOX_W, FOX_DIM)
    gm128, gm128_t = _group_matrix(HG_W, HG_DIM)

    meta = jnp.broadcast_to(w["meta_tokens"].astype(F32)[None], (lay.batch, N_META, D_MODEL))
    h0 = jnp.concatenate([jnp.zeros((lay.batch, LEAD, D_MODEL), F32), meta, x,
                          jnp.zeros((lay.batch, lay.lp - LEAD - lay.l_real, D_MODEL), F32)], axis=1).reshape(n, D_MODEL)

    (xn,) = rw(lambda i, h, g: _rms(h, g), [h0], [g1], [(D_MODEL, MXU_DTYPE)], [], name="norm1")
    proj = _matmul(xn, w_main, name="proj_main")
    pff = _matmul(xn, w_ff, name="proj_ff")

    def fox_prep_fn(i, a, b_, v_, f_, gq_, gk_, bf_, m_, mt_):
        q_, k_, logf = _fox_prep(a, b_, f_, gq_, gk_, bf_, m_, mt_, lay.valid(i, tile))
        return q_, k_, v_, logf

    q, k, v, logf = rw(fox_prep_fn, [(proj, 512, C_FQ), (proj, 512, C_FK), (proj, 512, C_FV), pff], [gq, gk, bf, gm64, gm64_t],
                       [(512, MXU_DTYPE), (512, MXU_DTYPE), (512, MXU_DTYPE), (LANES, F32)], [], name="fox_prep")
    cum8 = _cumsum_rows(logf, lay, reverse=False, name="fox_cum")[:, :FOX_HEADS]
    cum, cum_t = _per_pair_cols(cum8, lay), _per_pair_rows(cum8, lay)
    o_fox, lse = _fox_fwd(q, k, v, cum, cum_t, lay)

    def hg_prep_fn(i, hf_, l0, l1):
        lf, kk_ = _hg_prep(hf_, l0, l1)
        return kk_, _group_cumsum(lf, tile, reverse=False)

    lb0, lb1 = lb_logits[0:1], lb_logits[1:2]
    kk, gl = rw(hg_prep_fn, [(proj, 512, C_HF)], [lb0, lb1], [(512, F32), (512, F32)], [], name="hg_prep")
    o_hg, states = _hgrn_fwd(proj, kk, gl, lay)
    (oh,) = rw(lambda i, o, g_, gain, m_, mt_: _hg_post(o, g_, gain, m_, mt_), [o_hg, (proj, 512, C_HG)],
               [hg_gain, gm128, gm128_t], [(512, MXU_DTYPE)], [], name="hg_post")
    ya = _matmul(oh, w_a, name="branch_a")
    yb = _matmul(o_fox, w_b, name="branch_b")
    pga, pgb = (proj, 1024, C_GA), (proj, 1024, C_GB)
    (merged,) = rw(lambda i, a, b_, c_, d_: _gate(a, b_, c_, d_), [pga, pgb, ya, yb], [], [(D_MODEL, MXU_DTYPE)], [], name="gate")
    mo = _matmul(merged, w_out, name="out_proj")
    h1, hn = rw(lambda i, h, m_, g: (h + m_, _rms(h + m_, g)), [h0, mo], [g2], [(D_MODEL, F32), (D_MODEL, MXU_DTYPE)], [],
                name="norm2")
    u = _matmul(hn, w_up, name="up_proj")
    act = _conv_act_fwd(u, conv_w, conv_b, lay)
    mlp = _matmul(act, w_down, name="down_proj")
    loss_blk, dy, dyb = _loss_head(h1, mlp, target.reshape(lay.batch * lay.seq, D_MODEL), lay)
    loss = loss_blk[0, 0]

    grads = {}
    dact = _matmul(dyb, w_down.T, out_dtype=MXU_DTYPE, name="down_bwd_x")
    grads["w_down"] = _matmul(act, dyb, trans_a=True, name="down_bwd_w")
    dcg, dcv, grads["conv_w"], grads["conv_b"] = _conv_act_bwd(u, dact, conv_w, conv_b, lay)
    du = _conv_input_bwd(dcg, dcv, conv_w, lay)
    dhn = _matmul(du, w_up.T, name="up_bwd_x")
    grads["w_up"] = _matmul(hn, du, trans_a=True, name="up_bwd_w")

    def norm2_bwd(i, h, d_, dy_, g):
        _, vjp = jax.vjp(_rms, h, g)
        dh, dg = vjp(d_)
        return dh + dy_, dh + dy_, dg

    dh1, dh1b, grads["norm2_gain"] = rw(norm2_bwd, [h1, dhn, dy], [g2], [(D_MODEL, F32), (D_MODEL, MXU_DTYPE)], [(1, D_MODEL)],
                                        name="norm2_bwd")
    dmerged = _matmul(dh1b, w_out.T, name="out_bwd_x")
    grads["w_out"] = _matmul(merged, dh1b, trans_a=True, name="out_bwd_w")

    def gate_bwd(i, a, b_, c_, d_, dm):
        _, vjp = jax.vjp(_gate, a, b_, c_, d_)
        return vjp(dm)

    dga, dgb, dya, dyb_ = rw(gate_bwd, [pga, pgb, ya, yb, dmerged], [], [(D_MODEL, MXU_DTYPE)] * 4, [], name="gate_bwd")
    doh = _matmul(dya, w_a.T, name="branch_a_bwd_x")
    grads["w_branch_a"] = _matmul(oh, dya, trans_a=True, name="branch_a_bwd_w")
    dofox = _matmul(dyb_, w_b.T, out_dtype=MXU_DTYPE, name="branch_b_bwd_x")
    grads["w_branch_b"] = _matmul(o_fox, dyb_, trans_a=True, name="branch_b_bwd_w")

    def hg_post_bwd(i, o, g_, d_, gain, m_, mt_):
        _, vjp = jax.vjp(lambda o__, g__, gain__: _hg_post(o__, g__, gain__, m_, mt_), o, g_, gain)
        return vjp(d_)

    do_hg, dhg, grads["hg_out_gain"] = rw(hg_post_bwd, [o_hg, (proj, 512, C_HG), doh], [hg_gain, gm128, gm128_t],
                                          [(512, F32), (512, MXU_DTYPE)], [(1, HG_DIM)], name="hg_post_bwd")
    dhq, dkk, dhi, dgl = _hgrn_bwd(proj, kk, gl, do_hg, states, lay)

    def hg_prep_bwd(i, hf_, dkk_, dgl_, l0, l1):
        _, vjp = jax.vjp(_hg_prep, hf_, l0, l1)
        return vjp((_group_cumsum(dgl_, tile, reverse=True), dkk_))

    dhf, g_lb0, g_lb1 = rw(hg_prep_bwd, [(proj, 512, C_HF), dkk, dgl], [lb0, lb1], [(512, MXU_DTYPE)], [(1, HG_W), (1, HG_W)],
                           name="hg_prep_bwd")
    grads["hg_lb_logits"] = jnp.concatenate([g_lb0, g_lb1], axis=0)

    def delta_fn(i, o, d_, m_):
        return jnp.dot(o.astype(F32) * d_.astype(F32), m_, precision=HIGHEST, preferred_element_type=F32)

    (delta8,) = rw(delta_fn, [o_fox, dofox], [gm64], [(LANES, F32)], [], name="fox_delta")
    delta8 = delta8[:, :FOX_HEADS]
    dq = _fox_bwd_q(q, k, v, dofox, cum, cum_t, lse, _per_pair_cols(delta8, lay), lay)
    dk, dv, dcum = _fox_bwd_kv(q, k, v, dofox, cum, cum_t, _per_pair_rows(_from_pair_cols(lse, lay), lay),
                               _per_pair_rows(delta8, lay), lay)
    dcum = jnp.pad(_from_pair_cols(dcum, lay), ((0, 0), (0, LANES - FOX_HEADS)))
    dlogf = _cumsum_rows(dcum, lay, reverse=True, name="fox_cum_bwd")

    def fox_prep_bwd(i, a, b_, f_, dq_, dk_, dl_, gq_, gk_, bf_, m_, mt_):
        valid = lay.valid(i, tile)
        _, vjp = jax.vjp(lambda a_, b__, f__, gq__, gk__, bf__: _fox_prep(a_, b__, f__, gq__, gk__, bf__, m_, mt_, valid),
                         a, b_, f_, gq_, gk_, bf_)
        return vjp((dq_, dk_, dl_))

    dfq, dfk, dff, g_gq, g_gk, g_bf = rw(
        fox_prep_bwd, [(proj, 512, C_FQ), (proj, 512, C_FK), pff, dq, dk, dlogf], [gq, gk, bf, gm64, gm64_t],
        [(512, MXU_DTYPE), (512, MXU_DTYPE), (LANES, MXU_DTYPE)], [(1, FOX_W), (1, FOX_W), (1, LANES)], name="fox_prep_bwd")
    grads["q_norm_gain"] = g_gq.reshape(FOX_HEADS, FOX_DIM).sum(0, keepdims=True)
    grads["k_norm_gain"] = g_gk.reshape(FOX_HEADS, FOX_DIM).sum(0, keepdims=True)
    grads["fox_b_f"] = g_bf[:, :FOX_HEADS]

    dproj = jnp.concatenate([dga, dgb, dfq, dfk, mx(dv), mx(dhq), dhf, mx(dhi), dhg], axis=1)
    dxn = _matmul(dproj, w_main.T, name="proj_bwd_x")
    dxn_ff = _matmul(dff, w_ff.T, name="proj_ff_bwd_x")
    g_main = _matmul(xn, dproj, trans_a=True, name="proj_bwd_w")
    g_ff = _matmul(xn, dff, trans_a=True, name="proj_ff_bwd_w")[:, :FOX_HEADS]
    p = jnp.split(g_main, list(np.cumsum([1024, 1024] + [512] * 6)), axis=1)
    grads["w_in"] = jnp.concatenate([p[2], p[3], p[4], g_ff, p[5], p[6], p[7], p[8], p[0], p[1]], axis=1)

    per = lay.lp // tile

    def norm1_bwd(i, h, d1, d2, dh1_, g):
        _, vjp = jax.vjp(_rms, h, g)
        dh, dg = vjp(d1 + d2)
        dh = dh + dh1_
        dmeta = jnp.where(lax.rem(i, per) == 0, dh[LEAD:LEAD + N_META, :], 0.0)
        return dh, dg, dmeta

    dh0, grads["norm1_gain"], grads["meta_tokens"] = rw(norm1_bwd, [h0, dxn, dxn_ff, dh1], [g1], [(D_MODEL, F32)],
                                                       [(1, D_MODEL), (N_META, D_MODEL)], name="norm1_bwd")
    grad_x = dh0.reshape(lay.batch, lay.lp, D_MODEL)[:, ROW0:ROW0 + lay.seq]
    return loss, grad_x, grads


MESH = pl.DeviceIdType.MESH
HBM_SPEC = pl.BlockSpec(memory_space=pltpu.HBM)
N_CHIPS = 4
WEIGHT_NAMES = ["meta_tokens", "norm1_gain", "w_in", "fox_b_f", "q_norm_gain", "k_norm_gain", "hg_lb_logits", "hg_out_gain",
                "w_branch_a", "w_branch_b", "w_out", "norm2_gain", "w_up", "conv_w", "conv_b", "w_down"]
COL_SHARDED = ("meta_tokens", "w_in", "w_branch_a", "w_branch_b", "w_up", "conv_w")
ROW_SHARDED = ("w_out", "w_down")
REPLICATED = ("norm1_gain", "fox_b_f", "q_norm_gain", "k_norm_gain", "hg_lb_logits", "hg_out_gain", "norm2_gain", "conv_b")
GATHER_BF16 = ("w_in", "w_branch_a", "w_branch_b", "w_out", "w_up", "w_down")
GATHER_F32 = ("conv_w", "meta_tokens")
PACK_ROWS = 32768
HALF_ROWS = PACK_ROWS // 2


def _position():
    return lax.axis_index("x"), lax.axis_index("y"), lax.axis_index("c")


def _other_chips(x, y):
    return [(1 - x, y), (x, 1 - y), (1 - x, 1 - y)]


def _all_gather8(mine):
    rows = mine.shape[0]

    def body(x_ref, out_ref, send_sems, recv_sems, local_sem):
        x, y, c = _position()
        me, sibling = (x, y, c), (x, y, 1 - c)
        chips = _other_chips(x, y)

        def blk(px, py, pc):
            return out_ref.at[4 * px + 2 * py + pc]

        def copy(k, block, to, src=None):
            return pltpu.make_async_remote_copy(
                src_ref=blk(*block) if src is None else src, dst_ref=blk(*block),
                send_sem=send_sems.at[k], recv_sem=recv_sems.at[k], device_id=to, device_id_type=MESH)

        own = pltpu.make_async_copy(x_ref, blk(*me), local_sem)
        own.start()
        first = [copy(0, me, sibling, src=x_ref)] + [copy(1 + j, me, (*chip, c), src=x_ref) for j, chip in enumerate(chips)]
        for cp in first:
            cp.start()
        passed = [copy(4 + j, (*chip, c), sibling) for j, chip in enumerate(chips)]
        for j, chip in enumerate(chips):
            copy(1 + j, (*chip, c), me).wait_recv()
            passed[j].start()
        copy(0, sibling, me).wait_recv()
        for j, chip in enumerate(chips):
            copy(4 + j, (*chip, 1 - c), me).wait_recv()
        for cp in first + passed:
            cp.wait_send()
        own.wait()

    return pl.pallas_call(
        body, name="gather_weights", out_shape=jax.ShapeDtypeStruct((8, rows, LANES), mine.dtype),
        in_specs=[HBM_SPEC], out_specs=HBM_SPEC,
        scratch_shapes=[pltpu.SemaphoreType.DMA((7,)), pltpu.SemaphoreType.DMA((7,)), pltpu.SemaphoreType.DMA],
    )(mine)


def _sibling_exchange(g):
    rows = g.shape[2]

    def body(g_ref, own_ref, got_ref, send_sems, recv_sems, local_sems):
        x, y, c = _position()
        sibling = (x, y, 1 - c)
        local = [pltpu.make_async_copy(g_ref.at[j, c], own_ref.at[j], local_sems.at[j]) for j in range(N_CHIPS)]
        remote = [pltpu.make_async_remote_copy(src_ref=g_ref.at[j, 1 - c], dst_ref=got_ref.at[j], send_sem=send_sems.at[j],
                                               recv_sem=recv_sems.at[j], device_id=sibling, device_id_type=MESH)
                  for j in range(N_CHIPS)]
        for cp in remote + local:
            cp.start()
        for cp in remote + local:
            cp.wait()

    out = jax.ShapeDtypeStruct((N_CHIPS, rows, LANES), g.dtype)
    return pl.pallas_call(
        body, name="reduce_sibling", out_shape=[out, out], in_specs=[HBM_SPEC], out_specs=[HBM_SPEC, HBM_SPEC],
        scratch_shapes=[pltpu.SemaphoreType.DMA((N_CHIPS,)), pltpu.SemaphoreType.DMA((N_CHIPS,)),
                        pltpu.SemaphoreType.DMA((N_CHIPS,))],
    )(g)


def _chip_exchange(part):
    rows = part.shape[1]

    def body(p_ref, got_ref, send_sems, recv_sems, local_sem):
        x, y, c = _position()
        mine = 2 * x + y
        chips = _other_chips(x, y)
        own = pltpu.make_async_copy(p_ref.at[mine], got_ref.at[mine], local_sem)
        own.start()
        sends = [pltpu.make_async_remote_copy(src_ref=p_ref.at[2 * cx + cy], dst_ref=got_ref.at[mine], send_sem=send_sems.at[j],
                                              recv_sem=recv_sems.at[j], device_id=(cx, cy, c), device_id_type=MESH)
                 for j, (cx, cy) in enumerate(chips)]
        for cp in sends:
            cp.start()
        for j, (cx, cy) in enumerate(chips):
            pltpu.make_async_remote_copy(src_ref=p_ref.at[mine], dst_ref=got_ref.at[2 * cx + cy], send_sem=send_sems.at[j],
                                         recv_sem=recv_sems.at[j], device_id=(cx, cy, c), device_id_type=MESH).wait_recv()
        for cp in sends:
            cp.wait_send()
        own.wait()

    return pl.pallas_call(
        body, name="reduce_chips", out_shape=jax.ShapeDtypeStruct((N_CHIPS, rows, LANES), part.dtype),
        in_specs=[HBM_SPEC], out_specs=HBM_SPEC,
        scratch_shapes=[pltpu.SemaphoreType.DMA((3,)), pltpu.SemaphoreType.DMA((3,)), pltpu.SemaphoreType.DMA],
    )(part)


def _sibling_gather(half):
    rows = half.shape[0]

    def body(h_ref, out_ref, send_sem, recv_sem, local_sem):
        x, y, c = _position()
        own = pltpu.make_async_copy(h_ref, out_ref.at[c], local_sem)
        own.start()
        send = pltpu.make_async_remote_copy(src_ref=h_ref, dst_ref=out_ref.at[c], send_sem=send_sem, recv_sem=recv_sem,
                                            device_id=(x, y, 1 - c), device_id_type=MESH)
        send.start()
        pltpu.make_async_remote_copy(src_ref=h_ref, dst_ref=out_ref.at[1 - c], send_sem=send_sem, recv_sem=recv_sem,
                                     device_id=(x, y, 1 - c), device_id_type=MESH).wait_recv()
        send.wait_send()
        own.wait()

    return pl.pallas_call(
        body, name="reduce_gather", out_shape=jax.ShapeDtypeStruct((2, rows, LANES), half.dtype),
        in_specs=[HBM_SPEC], out_specs=HBM_SPEC,
        scratch_shapes=[pltpu.SemaphoreType.DMA, pltpu.SemaphoreType.DMA, pltpu.SemaphoreType.DMA],
    )(half)


def _add_pairs(a, b):
    rows = a.shape[0]
    tile = _pick(rows, (2048, 1024, 512))

    def body(a_ref, b_ref, o_ref):
        o_ref[...] = a_ref[...] + b_ref[...]

    spec = pl.BlockSpec((tile, LANES), lambda i: (i, 0))
    return pl.pallas_call(body, name="reduce_add2", grid=(rows // tile,), in_specs=[spec, spec], out_specs=spec,
                          out_shape=jax.ShapeDtypeStruct(a.shape, a.dtype), compiler_params=_params(1))(a, b)


def _add_chips(got):
    rows = got.shape[1]
    tile = _pick(rows, (2048, 1024, 512))

    def body(g_ref, o_ref):
        o_ref[...] = ((g_ref[0] + g_ref[1]) + g_ref[2]) + g_ref[3]

    return pl.pallas_call(body, name="reduce_add4", grid=(rows // tile,),
                          in_specs=[pl.BlockSpec((N_CHIPS, tile, LANES), lambda i: (0, i, 0))],
                          out_specs=pl.BlockSpec((tile, LANES), lambda i: (i, 0)),
                          out_shape=jax.ShapeDtypeStruct((rows, LANES), got.dtype), compiler_params=_params(1))(got)


def _adamw(w, g, m, v):
    rows = w.shape[0]
    tile = _pick(rows, (2048, 1024, 512))
    c1 = 1.0 / (1.0 - ADAM_B1 ** ADAM_STEP)
    c2 = 1.0 / (1.0 - ADAM_B2 ** ADAM_STEP)

    def body(w_ref, g_ref, m_ref, v_ref, d_ref, mo_ref, vo_ref):
        g_ = g_ref[...]
        m_new = ADAM_B1 * m_ref[...] + (1.0 - ADAM_B1) * g_
        v_new = ADAM_B2 * v_ref[...] + (1.0 - ADAM_B2) * (g_ * g_)
        d_ref[...] = -ADAM_LR * ((m_new * c1) / (jnp.sqrt(v_new * c2) + ADAM_EPS) + ADAM_WD * w_ref[...])
        mo_ref[...] = m_new
        vo_ref[...] = v_new

    spec = pl.BlockSpec((tile, LANES), lambda i: (i, 0))
    out = jax.ShapeDtypeStruct(w.shape, F32)
    return pl.pallas_call(body, name="adamw", grid=(rows // tile,), in_specs=[spec] * 4, out_specs=[spec] * 3,
                          out_shape=[out] * 3, compiler_params=_params(1))(w, g, m, v)


def _shard_shape(name, full_shape):
    if name in COL_SHARDED:
        return full_shape[:-1] + (full_shape[-1] // N_CHIPS,)
    if name in ROW_SHARDED:
        return (full_shape[0] // N_CHIPS,) + full_shape[1:]
    return full_shape


def _to_rows(flat, rows):
    return jnp.pad(flat, (0, rows * LANES - flat.shape[0])).reshape(rows, LANES)


def _pack_local(tree):
    return _to_rows(jnp.concatenate([tree[n].astype(F32).reshape(-1) for n in WEIGHT_NAMES]), PACK_ROWS)


def _unpack_local(packed, shapes):
    flat, out, at = packed.reshape(-1), {}, 0
    for n in WEIGHT_NAMES:
        size = int(np.prod(shapes[n]))
        out[n] = flat[at:at + size].reshape(shapes[n])
        at += size
    return out


def _pack_by_chip(grads, full_shapes):
    pieces = []
    for n in WEIGHT_NAMES:
        g = grads[n].astype(F32).reshape(full_shapes[n])
        if n in COL_SHARDED:
            g2 = g.reshape(-1, N_CHIPS, g.shape[-1] // N_CHIPS)
            pieces.append(jnp.transpose(g2, (1, 0, 2)).reshape(N_CHIPS, -1))
        elif n in ROW_SHARDED:
            pieces.append(g.reshape(N_CHIPS, -1))
        else:
            pieces.append(jnp.broadcast_to(g.reshape(1, -1), (N_CHIPS, g.size)))
    flat = jnp.concatenate(pieces, axis=1)
    return jnp.pad(flat, ((0, 0), (0, PACK_ROWS * LANES - flat.shape[1]))).reshape(N_CHIPS, PACK_ROWS, LANES)


def _gather_weights(local, full_shapes):
    parts = [local[n].astype(BF16).reshape(-1) for n in GATHER_BF16]
    parts += [lax.bitcast_convert_type(local[n].astype(F32).reshape(-1), BF16).reshape(-1) for n in GATHER_F32]
    flat = jnp.concatenate(parts)
    rows = -(-flat.shape[0] // (2 * 16 * LANES)) * 16
    both = _to_rows(flat, 2 * rows).reshape(2, rows, LANES)
    mine = lax.dynamic_index_in_dim(both, lax.axis_index("c"), axis=0, keepdims=False)
    got = _all_gather8(mine).reshape(N_CHIPS, -1)
    out, at = {}, 0
    for n in GATHER_BF16 + GATHER_F32:
        shard = _shard_shape(n, full_shapes[n])
        size = int(np.prod(shard)) * (2 if n in GATHER_F32 else 1)
        piece = got[:, at:at + size]
        at += size
        if n in GATHER_F32:
            piece = lax.bitcast_convert_type(piece.reshape(N_CHIPS, -1, 2), F32)
        piece = piece.reshape((N_CHIPS,) + shard)
        if n in COL_SHARDED:
            piece = jnp.moveaxis(piece, 0, -2).reshape(full_shapes[n])
        else:
            piece = piece.reshape(full_shapes[n])
        out[n] = piece
    return out


def kernel(x, meta_tokens, norm1_gain, w_in, fox_b_f, q_norm_gain, k_norm_gain, hg_lb_logits, hg_out_gain, w_branch_a, w_branch_b, w_out, norm2_gain, w_up, conv_w, conv_b, w_down, loss_target, m_meta_tokens, m_norm1_gain, m_w_in, m_fox_b_f, m_q_norm_gain, m_k_norm_gain, m_hg_lb_logits, m_hg_out_gain, m_w_branch_a, m_w_branch_b, m_w_out, m_norm2_gain, m_w_up, m_conv_w, m_conv_b, m_w_down, v_meta_tokens, v_norm1_gain, v_w_in, v_fox_b_f, v_q_norm_gain, v_k_norm_gain, v_hg_lb_logits, v_hg_out_gain, v_w_branch_a, v_w_branch_b, v_w_out, v_norm2_gain, v_w_up, v_conv_w, v_conv_b, v_w_down):
    w_loc = dict(zip(WEIGHT_NAMES, (meta_tokens, norm1_gain, w_in, fox_b_f, q_norm_gain, k_norm_gain, hg_lb_logits, hg_out_gain,
                                    w_branch_a, w_branch_b, w_out, norm2_gain, w_up, conv_w, conv_b, w_down)))
    m_loc = dict(zip(WEIGHT_NAMES, (m_meta_tokens, m_norm1_gain, m_w_in, m_fox_b_f, m_q_norm_gain, m_k_norm_gain, m_hg_lb_logits,
                                    m_hg_out_gain, m_w_branch_a, m_w_branch_b, m_w_out, m_norm2_gain, m_w_up, m_conv_w, m_conv_b,
                                    m_w_down)))
    v_loc = dict(zip(WEIGHT_NAMES, (v_meta_tokens, v_norm1_gain, v_w_in, v_fox_b_f, v_q_norm_gain, v_k_norm_gain, v_hg_lb_logits,
                                    v_hg_out_gain, v_w_branch_a, v_w_branch_b, v_w_out, v_norm2_gain, v_w_up, v_conv_w, v_conv_b,
                                    v_w_down)))
    local_shapes = {n: tuple(w_loc[n].shape) for n in WEIGHT_NAMES}
    squeeze = lambda s: s[1:] if len(s) == 3 else s
    full_shapes = {}
    for n in WEIGHT_NAMES:
        s = squeeze(local_shapes[n])
        if n in COL_SHARDED:
            s = s[:-1] + (s[-1] * N_CHIPS,)
        elif n in ROW_SHARDED:
            s = (s[0] * N_CHIPS,) + s[1:]
        full_shapes[n] = s

    weights = {n: w_loc[n].reshape(full_shapes[n]) for n in REPLICATED}
    weights.update(_gather_weights(w_loc, full_shapes))

    lay = _Layout(x.shape[0], x.shape[1])
    loss, grad_x, grads = _local_step(x, loss_target, weights, lay)
    loss = lax.psum(loss, ("x", "y", "c"))

    by_chip = _pack_by_chip(grads, full_shapes).reshape(N_CHIPS, 2, HALF_ROWS, LANES)
    own, got = _sibling_exchange(by_chip)
    part = _add_pairs(own.reshape(-1, LANES), got.reshape(-1, LANES)).reshape(N_CHIPS, HALF_ROWS, LANES)
    half = _add_chips(_chip_exchange(part))
    g_packed = _sibling_gather(half).reshape(PACK_ROWS, LANES)

    delta, new_m, new_v = _adamw(_pack_local(w_loc), g_packed, _pack_local(m_loc), _pack_local(v_loc))
    outs = [_unpack_local(a, local_shapes) for a in (g_packed, delta, new_m, new_v)]
    return (loss, grad_x, *[o[n] for o in outs for n in WEIGHT_NAMES])
```

```python
import functools

import jax
import jax.numpy as jnp
import numpy as np
from jax import lax
from jax.experimental import pallas as pl
from jax.experimental.pallas import tpu as pltpu

F32 = jnp.float32
BF16 = jnp.bfloat16
MXU_DTYPE = BF16
HIGHEST = lax.Precision.HIGHEST

D_MODEL = 1024
N_META = 16
LEAD = 48
ROW0 = LEAD + N_META
FOX_HEADS, FOX_DIM, FOX_W = 8, 64, 512
HG_HEADS, HG_DIM, HG_W = 4, 128, 512
D_FF = 2816
FF2 = 2 * D_FF
EPS = 1e-6
SUB = 16
LANES = 128
NEG = -1e30

ADAM_LR, ADAM_B1, ADAM_B2, ADAM_EPS, ADAM_WD, ADAM_STEP = 0.001, 0.9, 0.999, 1e-08, 0.01, 10

VMEM_LIMIT = 56 * 1024 * 1024

C_GA, C_GB = 0, 1
C_FQ, C_FK, C_FV, C_HQ, C_HF, C_HI, C_HG = 4, 5, 6, 7, 8, 9, 10


def _params(n_axes=1):
    return pltpu.CompilerParams(dimension_semantics=("arbitrary",) * n_axes, vmem_limit_bytes=VMEM_LIMIT)


def _pick(n, cands):
    for c in cands:
        if n % c == 0:
            return c
    raise ValueError(f"no tile for {n} among {cands}")


def _rowwise(fn, rows, consts, outs, reds, *, n_rows, tile, name):
    assert n_rows % tile == 0
    rows = [r if isinstance(r, tuple) else (r, r.shape[1], 0) for r in rows]
    nr, nc, no = len(rows), len(consts), len(outs)

    def body(*refs):
        i = pl.program_id(0)
        ins = [r[...] for r in refs[:nr + nc]]
        res = fn(i, *ins)
        res = res if isinstance(res, (tuple, list)) else (res,)
        for ref, v in zip(refs[nr + nc:nr + nc + no], res[:no]):
            ref[...] = v.astype(ref.dtype)
        red_refs = refs[nr + nc + no:]
        if red_refs:
            @pl.when(i == 0)
            def _():
                for ref in red_refs:
                    ref[...] = jnp.zeros_like(ref)
            for ref, v in zip(red_refs, res[no:]):
                ref[...] += v.astype(F32)

    in_specs = [pl.BlockSpec((tile, w), functools.partial(lambda i, j: (i, j), j=j)) for (_, w, j) in rows]
    in_specs += [pl.BlockSpec(c.shape, functools.partial(lambda i, nd: (0,) * nd, nd=c.ndim)) for c in consts]
    out_specs = [pl.BlockSpec((tile, w), lambda i: (i, 0)) for (w, _) in outs]
    out_specs += [pl.BlockSpec(s, functools.partial(lambda i, nd: (0,) * nd, nd=len(s))) for s in reds]
    out_shape = [jax.ShapeDtypeStruct((n_rows, w), dt) for (w, dt) in outs]
    out_shape += [jax.ShapeDtypeStruct(s, F32) for s in reds]
    return pl.pallas_call(
        body, name=name, grid=(n_rows // tile,), in_specs=in_specs, out_specs=out_specs, out_shape=out_shape,
        compiler_params=_params(1),
    )(*[r[0] for r in rows], *consts)


def _matmul(a, b, *, trans_a=False, out_dtype=F32, name):
    if trans_a:
        k, m = a.shape
    else:
        m, k = a.shape
    n = b.shape[1]
    assert b.shape[0] == k
    if trans_a:
        tm = _pick(m, (1408, 1024, 512, 256, 128))
        tk = _pick(k, (1088, 1024, 512))
    else:
        tm = _pick(m, (512, 256, 128))
        tk = k if k <= 1024 else _pick(k, (1408, 1024, 512))
    tn = _pick(n, (1408, 1024, 512, 256, 128))
    nk = k // tk
    dims = (((0,), (0,)), ((), ())) if trans_a else (((1,), (0,)), ((), ()))

    def body(a_ref, b_ref, o_ref, acc_ref):
        part = lax.dot_general(a_ref[...], b_ref[...], dims, preferred_element_type=F32)
        if nk == 1:
            o_ref[...] = part.astype(o_ref.dtype)
        else:
            kk = pl.program_id(2)

            @pl.when(kk == 0)
            def _():
                acc_ref[...] = part

            @pl.when(kk > 0)
            def _():
                acc_ref[...] += part

            @pl.when(kk == nk - 1)
            def _():
                o_ref[...] = acc_ref[...].astype(o_ref.dtype)

    a_spec = pl.BlockSpec((tk, tm), lambda i, j, kk: (kk, i)) if trans_a else pl.BlockSpec((tm, tk), lambda i, j, kk: (i, kk))
    return pl.pallas_call(
        body, name=name, grid=(m // tm, n // tn, nk),
        in_specs=[a_spec, pl.BlockSpec((tk, tn), lambda i, j, kk: (kk, j))],
        out_specs=pl.BlockSpec((tm, tn), lambda i, j, kk: (i, j)),
        out_shape=jax.ShapeDtypeStruct((m, n), out_dtype),
        scratch_shapes=[pltpu.VMEM((tm, tn) if nk > 1 else (8, LANES), F32)],
        compiler_params=_params(3),
    )(a, b)


def _sigmoid(x):
    return 1.0 / (1.0 + jnp.exp(-x))


def _silu(x):
    return x * _sigmoid(x)


def _log_sigmoid(x):
    return jnp.minimum(x, 0.0) - jnp.log(1.0 + jnp.exp(-jnp.abs(x)))


def _rms(x, gain):
    return x * lax.rsqrt(jnp.mean(x * x, axis=-1, keepdims=True) + EPS) * gain


def _group_matrix(width, group):
    g = (np.arange(width)[:, None] // group == np.arange(LANES)[None, :]).astype(np.float32)
    return jnp.asarray(g), jnp.asarray(g.T.copy())


def _group_rms(x, gain, gmat, gmat_t, group):
    ms = jnp.dot(x * x, gmat, precision=HIGHEST, preferred_element_type=F32) * (1.0 / group)
    rstd = lax.rsqrt(ms + EPS)
    return x * jnp.dot(rstd, gmat_t, precision=HIGHEST, preferred_element_type=F32) * gain


class _Layout:
    def __init__(self, batch, seq):
        self.batch, self.seq = batch, seq
        self.l_real = N_META + seq
        self.lp = -(-(LEAD + self.l_real) // 256) * 256
        self.n = batch * self.lp
        self.tile = _pick(self.lp, (512, 256))

    def valid(self, i, tile):
        per = self.lp // tile
        r = lax.rem(i, per) * tile + lax.broadcasted_iota(jnp.int32, (tile, 1), 0)
        return (r >= LEAD) & (r < LEAD + self.l_real)


def _cumsum_rows(x, lay, *, reverse, name):
    t = LANES
    nt = lay.lp // t
    c = x.shape[1]

    def body(x_ref, o_ref, carry):
        j = pl.program_id(1)

        @pl.when(j == 0)
        def _():
            carry[...] = jnp.zeros_like(carry)

        r = lax.broadcasted_iota(jnp.int32, (t, t), 0)
        q = lax.broadcasted_iota(jnp.int32, (t, t), 1)
        tri = jnp.where((q >= r) if reverse else (q <= r), 1.0, 0.0).astype(F32)
        xs = x_ref[...]
        out = jnp.dot(tri, xs, precision=HIGHEST, preferred_element_type=F32) + carry[0:1, :]
        o_ref[...] = out
        carry[...] = jnp.broadcast_to(carry[0:1, :] + jnp.sum(xs, axis=0, keepdims=True), carry.shape)

    def idx(b, j):
        return (b * nt + (nt - 1 - j if reverse else j), 0)

    return pl.pallas_call(
        body, name=name, grid=(lay.batch, nt),
        in_specs=[pl.BlockSpec((t, c), idx)], out_specs=pl.BlockSpec((t, c), idx),
        out_shape=jax.ShapeDtypeStruct(x.shape, F32),
        scratch_shapes=[pltpu.VMEM((8, c), F32)],
        compiler_params=_params(2),
    )(x)


def _group_cumsum(x, tile, *, reverse):
    r = lax.rem(lax.broadcasted_iota(jnp.int32, (tile, 1), 0), SUB)
    s = 1
    while s < SUB:
        if reverse:
            x = x + jnp.where(r < SUB - s, pltpu.roll(x, tile - s, 0), 0.0)
        else:
            x = x + jnp.where(r >= s, pltpu.roll(x, s, 0), 0.0)
        s *= 2
    return x


def _fox_blocks(lay):
    bq = _pick(lay.lp, (384, 128))
    return bq, LANES


def _fox_scores(q, k, cum_t, cum_s, row0, col0, bq, bk):
    s = lax.dot_general(q, k, (((1,), (1,)), ((), ())), preferred_element_type=F32) * (FOX_DIM ** -0.5)
    s = s + (cum_t - cum_s)
    rows = row0 + lax.broadcasted_iota(jnp.int32, (bq, bk), 0)
    cols = col0 + lax.broadcasted_iota(jnp.int32, (bq, bk), 1)
    return jnp.where((cols <= rows) & (cols >= LEAD), s, NEG)


def _per_pair_cols(a, lay):
    return jnp.transpose(a.reshape(lay.n, FOX_HEADS // 2, 2), (1, 0, 2))


def _per_pair_rows(a, lay):
    return jnp.transpose(a.reshape(lay.batch, lay.lp, FOX_HEADS), (0, 2, 1)).reshape(lay.batch * FOX_HEADS, 1, lay.lp)


def _from_pair_cols(a, lay):
    return jnp.transpose(a, (1, 0, 2)).reshape(lay.n, FOX_HEADS)


def _fox_fwd(q, k, v, cum, cum_t, lay):
    bq, bk = _fox_blocks(lay)
    nq = lay.lp // bq
    pairs = FOX_HEADS // 2

    def body(q_ref, k_ref, v_ref, cum_ref, cumt_ref, o_ref, lse_ref):
        qi = pl.program_id(2)
        row0 = qi * bq
        rows = row0 + lax.broadcasted_iota(jnp.int32, (bq, 1), 0)
        valid = (rows >= LEAD) & (rows < LEAD + lay.l_real)
        nkb = (row0 + bq) // bk
        for hh in range(2):
            lanes = slice(hh * FOX_DIM, (hh + 1) * FOX_DIM)
            qh = q_ref[:, lanes]
            cum_col = cum_ref[0, :, hh:hh + 1]

            def step(kb, carry):
                m, l, acc = carry
                c0 = pl.multiple_of(kb * bk, bk)
                kblk = k_ref[pl.ds(c0, bk), lanes]
                vblk = v_ref[pl.ds(c0, bk), lanes]
                cum_row = cumt_ref[hh, :, pl.ds(c0, bk)]
                s = _fox_scores(qh, kblk, cum_col, cum_row, row0, c0, bq, bk)
                m_new = jnp.maximum(m, jnp.max(s, axis=1, keepdims=True))
                alpha = jnp.exp(m - m_new)
                p = jnp.exp(s - m_new)
                l = alpha * l + jnp.sum(p, axis=1, keepdims=True)
                acc = alpha * acc + jnp.dot(p.astype(vblk.dtype), vblk, preferred_element_type=F32)
                return m_new, l, acc

            m0 = jnp.full((bq, 1), NEG, F32)
            m, l, acc = lax.fori_loop(0, nkb, step, (m0, jnp.zeros((bq, 1), F32), jnp.zeros((bq, FOX_DIM), F32)))
            o_ref[:, lanes] = jnp.where(valid, acc / l, 0.0).astype(o_ref.dtype)
            lse_ref[0, :, hh:hh + 1] = m + jnp.log(l)

    return pl.pallas_call(
        body, name="fox_fwd", grid=(lay.batch, pairs, nq),
        in_specs=[
            pl.BlockSpec((bq, LANES), lambda b, p, i: (b * nq + i, p)),
            pl.BlockSpec((lay.lp, LANES), lambda b, p, i: (b, p)),
            pl.BlockSpec((lay.lp, LANES), lambda b, p, i: (b, p)),
            pl.BlockSpec((1, bq, 2), lambda b, p, i: (p, b * nq + i, 0)),
            pl.BlockSpec((2, 1, lay.lp), lambda b, p, i: (b * pairs + p, 0, 0)),
        ],
        out_specs=[
            pl.BlockSpec((bq, LANES), lambda b, p, i: (b * nq + i, p)),
            pl.BlockSpec((1, bq, 2), lambda b, p, i: (p, b * nq + i, 0)),
        ],
        out_shape=[jax.ShapeDtypeStruct((lay.n, FOX_W), MXU_DTYPE), jax.ShapeDtypeStruct((pairs, lay.n, 2), F32)],
        compiler_params=_params(3),
    )(q, k, v, cum, cum_t)


def _fox_bwd_q(q, k, v, do, cum, cum_t, lse, delta, lay):
    bq, bk = _fox_blocks(lay)
    nq = lay.lp // bq
    pairs = FOX_HEADS // 2

    def body(q_ref, k_ref, v_ref, do_ref, cum_ref, cumt_ref, lse_ref, dl_ref, dq_ref):
        qi = pl.program_id(2)
        row0 = qi * bq
        nkb = (row0 + bq) // bk
        for hh in range(2):
            lanes = slice(hh * FOX_DIM, (hh + 1) * FOX_DIM)
            qh = q_ref[:, lanes]
            doh = do_ref[:, lanes]
            cum_col = cum_ref[0, :, hh:hh + 1]
            dl_col = dl_ref[0, :, hh:hh + 1]
            lse_col = lse_ref[0, :, hh:hh + 1]

            def step(kb, dq):
                c0 = pl.multiple_of(kb * bk, bk)
                kblk = k_ref[pl.ds(c0, bk), lanes]
                vblk = v_ref[pl.ds(c0, bk), lanes]
                cum_row = cumt_ref[hh, :, pl.ds(c0, bk)]
                s = _fox_scores(qh, kblk, cum_col, cum_row, row0, c0, bq, bk)
                p = jnp.exp(s - lse_col)
                dp = lax.dot_general(doh, vblk, (((1,), (1,)), ((), ())), preferred_element_type=F32)
                ds = p * (dp - dl_col)
                return dq + jnp.dot(ds.astype(kblk.dtype), kblk, preferred_element_type=F32)

            dq = lax.fori_loop(0, nkb, step, jnp.zeros((bq, FOX_DIM), F32))
            dq_ref[:, lanes] = dq * (FOX_DIM ** -0.5)

    return pl.pallas_call(
        body, name="fox_bwd_q", grid=(lay.batch, pairs, nq),
        in_specs=[
            pl.BlockSpec((bq, LANES), lambda b, p, i: (b * nq + i, p)),
            pl.BlockSpec((lay.lp, LANES), lambda b, p, i: (b, p)),
            pl.BlockSpec((lay.lp, LANES), lambda b, p, i: (b, p)),
            pl.BlockSpec((bq, LANES), lambda b, p, i: (b * nq + i, p)),
            pl.BlockSpec((1, bq, 2), lambda b, p, i: (p, b * nq + i, 0)),
            pl.BlockSpec((2, 1, lay.lp), lambda b, p, i: (b * pairs + p, 0, 0)),
            pl.BlockSpec((1, bq, 2), lambda b, p, i: (p, b * nq + i, 0)),
            pl.BlockSpec((1, bq, 2), lambda b, p, i: (p, b * nq + i, 0)),
        ],
        out_specs=pl.BlockSpec((bq, LANES), lambda b, p, i: (b * nq + i, p)),
        out_shape=jax.ShapeDtypeStruct((lay.n, FOX_W), F32),
        compiler_params=_params(3),
    )(q, k, v, do, cum, cum_t, lse, delta)


def _fox_bwd_kv(q, k, v, do, cum, cum_t, lse_t, delta_t, lay):
    bk, bq = _fox_blocks(lay)
    nkb = lay.lp // bk
    nqb = lay.lp // bq
    pairs = FOX_HEADS // 2

    def body(q_ref, k_ref, v_ref, do_ref, cum_ref, cumt_ref, lset_ref, dlt_ref, dk_ref, dv_ref, dc_ref):
        ki = pl.program_id(2)
        key0 = ki * bk
        q_first = key0 // bq
        for hh in range(2):
            lanes = slice(hh * FOX_DIM, (hh + 1) * FOX_DIM)
            kh = k_ref[:, lanes]
            vh = v_ref[:, lanes]
            cum_key = cum_ref[0, :, hh:hh + 1]

            def step(qb, carry):
                dk, dv, dc = carry
                c0 = pl.multiple_of(qb * bq, LANES)
                qblk = q_ref[pl.ds(c0, bq), lanes]
                doblk = do_ref[pl.ds(c0, bq), lanes]
                cum_q = cumt_ref[hh, :, pl.ds(c0, bq)]
                lse_q = lset_ref[hh, :, pl.ds(c0, bq)]
                dl_q = dlt_ref[hh, :, pl.ds(c0, bq)]
                st = lax.dot_general(kh, qblk, (((1,), (1,)), ((), ())), preferred_element_type=F32) * (FOX_DIM ** -0.5)
                st = st + (cum_q - cum_key)
                keys = key0 + lax.broadcasted_iota(jnp.int32, (bk, bq), 0)
                qs = c0 + lax.broadcasted_iota(jnp.int32, (bk, bq), 1)
                st = jnp.where((keys <= qs) & (keys >= LEAD), st, NEG)
                pt = jnp.exp(st - lse_q)
                dv = dv + jnp.dot(pt.astype(doblk.dtype), doblk, preferred_element_type=F32)
                dpt = lax.dot_general(vh, doblk, (((1,), (1,)), ((), ())), preferred_element_type=F32)
                dst = pt * (dpt - dl_q)
                dk = dk + jnp.dot(dst.astype(qblk.dtype), qblk, preferred_element_type=F32)
                dc = dc - jnp.sum(dst, axis=1, keepdims=True)
                return dk, dv, dc

            z = jnp.zeros((bk, FOX_DIM), F32)
            dk, dv, dc = lax.fori_loop(q_first, nqb, step, (z, z, jnp.zeros((bk, 1), F32)))
            dk_ref[:, lanes] = dk * (FOX_DIM ** -0.5)
            dv_ref[:, lanes] = dv
            dc_ref[0, :, hh:hh + 1] = dc

    whole = lambda b, p, i: (b, p)
    tvec = lambda b, p, i: (b * pairs + p, 0, 0)
    return pl.pallas_call(
        body, name="fox_bwd_kv", grid=(lay.batch, pairs, nkb),
        in_specs=[
            pl.BlockSpec((lay.lp, LANES), whole),
            pl.BlockSpec((bk, LANES), lambda b, p, i: (b * nkb + i, p)),
            pl.BlockSpec((bk, LANES), lambda b, p, i: (b * nkb + i, p)),
            pl.BlockSpec((lay.lp, LANES), whole),
            pl.BlockSpec((1, bk, 2), lambda b, p, i: (p, b * nkb + i, 0)),
            pl.BlockSpec((2, 1, lay.lp), tvec),
            pl.BlockSpec((2, 1, lay.lp), tvec),
            pl.BlockSpec((2, 1, lay.lp), tvec),
        ],
        out_specs=[
            pl.BlockSpec((bk, LANES), lambda b, p, i: (b * nkb + i, p)),
            pl.BlockSpec((bk, LANES), lambda b, p, i: (b * nkb + i, p)),
            pl.BlockSpec((1, bk, 2), lambda b, p, i: (p, b * nkb + i, 0)),
        ],
        out_shape=[jax.ShapeDtypeStruct((lay.n, FOX_W), F32), jax.ShapeDtypeStruct((lay.n, FOX_W), F32),
                   jax.ShapeDtypeStruct((pairs, lay.n, 2), F32)],
        compiler_params=_params(3),
    )(q, k, v, do, cum, cum_t, lse_t, delta_t)


AUG = 128
FOX_BK = 256
FOX_BQ = 256
FOX_SCALE = FOX_DIM ** -0.5


def _aug_matrices():
    e1 = np.zeros((FOX_W, FOX_HEADS * AUG), np.float32)
    e2 = np.zeros((LANES, FOX_HEADS * AUG), np.float32)
    ones = np.zeros((1, FOX_HEADS * AUG), np.float32)
    for h in range(FOX_HEADS):
        for d in range(FOX_DIM):
            e1[h * FOX_DIM + d, h * AUG + d] = 1.0
        for j in range(3):
            e2[j * FOX_HEADS + h, h * AUG + FOX_DIM + j] = 1.0
            ones[0, h * AUG + FOX_DIM + j] = 1.0
    return jnp.asarray(e1, MXU_DTYPE), jnp.asarray(e2, MXU_DTYPE), jnp.asarray(ones)


def _fox_augment(q, k, cum, key_ok, e1, e2, ones):
    dt = q.dtype
    c = jnp.where(key_ok, -cum, NEG)
    hi = c.astype(dt)
    r1 = c - hi.astype(F32)
    mid = r1.astype(dt)
    lo = (r1 - mid.astype(F32)).astype(dt)
    lane = lax.broadcasted_iota(jnp.int32, c.shape, 1)
    shift = lambda a, by: pltpu.roll(a.astype(F32), by, 1)
    parts = jnp.where(lane < FOX_HEADS, hi.astype(F32),
                      jnp.where(lane < 2 * FOX_HEADS, shift(mid, FOX_HEADS),
                                jnp.where(lane < 3 * FOX_HEADS, shift(lo, 2 * FOX_HEADS), 0.0))).astype(dt)
    qs = (q.astype(F32) * FOX_SCALE).astype(dt)
    q_aug = jnp.dot(qs, e1, preferred_element_type=F32) + ones
    k_aug = jnp.dot(k, e1, preferred_element_type=F32) + jnp.dot(parts, e2, preferred_element_type=F32)
    return q_aug.astype(dt), k_aug.astype(dt)


def _fox_tile(k_blk, q_blk, k0, q0, masked):
    st = lax.dot_general(k_blk, q_blk, (((1,), (1,)), ((), ())), preferred_element_type=F32)
    if masked:
        keys = k0 + lax.broadcasted_iota(jnp.int32, st.shape, 0)
        qs = q0 + lax.broadcasted_iota(jnp.int32, st.shape, 1)
        st = jnp.where(keys <= qs, st, NEG)
    return st


def _fox_fwd_t(q_aug, k_aug, v_t, lay):
    bk, bq = FOX_BK, FOX_BQ
    nq = lay.lp // bq
    pairs = FOX_HEADS // 2

    def body(q_ref, k_ref, vt_ref, ot_ref, lse_ref):
        heads = [(slice(hh * AUG, (hh + 1) * AUG), slice(hh * FOX_DIM, (hh + 1) * FOX_DIM)) for hh in range(2)]

        def q_loop(qb, _):
            q0 = pl.multiple_of(qb * bq, bq)
            q_blks = [q_ref[pl.ds(q0, bq), lanes] for lanes, _ in heads]

            def k_step(kb, carries, masked):
                k0 = pl.multiple_of(kb * bk, bk)
                out = []
                for (lanes, vrows), q_blk, (m, l, acc) in zip(heads, q_blks, carries):
                    st = _fox_tile(k_ref[pl.ds(k0, bk), lanes], q_blk, k0, q0, masked)
                    m_new = jnp.maximum(m, jnp.max(st, axis=0, keepdims=True))
                    alpha = jnp.exp(m - m_new)
                    p = jnp.exp(st - m_new)
                    l = alpha * l + jnp.sum(p, axis=0, keepdims=True)
                    acc = alpha * acc + jnp.dot(vt_ref[vrows, pl.ds(k0, bk)], p.astype(vt_ref.dtype),
                                                preferred_element_type=F32)
                    out.append((m_new, l, acc))
                return tuple(out)

            init = (jnp.full((1, bq), NEG, F32), jnp.zeros((1, bq), F32), jnp.zeros((FOX_DIM, bq), F32))
            carries = lax.fori_loop(0, qb, lambda kb, c: k_step(kb, c, False), (init, init))
            carries = k_step(qb, carries, True)
            qs = q0 + lax.broadcasted_iota(jnp.int32, (1, bq), 1)
            ok = (qs >= LEAD) & (qs < LEAD + lay.l_real)
            for hh, ((_, vrows), (m, l, acc)) in enumerate(zip(heads, carries)):
                ot_ref[vrows, pl.ds(q0, bq)] = jnp.where(ok, acc / l, 0.0).astype(ot_ref.dtype)
                lse_ref[hh, :, pl.ds(q0, bq)] = m + jnp.log(l)
            return 0

        lax.fori_loop(0, nq, q_loop, 0)

    aug = pl.BlockSpec((lay.lp, 2 * AUG), lambda b, p: (b, p))
    tr = pl.BlockSpec((2 * FOX_DIM, lay.lp), lambda b, p: (p, b))
    return pl.pallas_call(
        body, name="fox_fwd", grid=(lay.batch, pairs),
        in_specs=[aug, aug, tr],
        out_specs=[tr, pl.BlockSpec((2, 1, lay.lp), lambda b, p: (b * pairs + p, 0, 0))],
        out_shape=[jax.ShapeDtypeStruct((FOX_W, lay.n), MXU_DTYPE),
                   jax.ShapeDtypeStruct((lay.batch * FOX_HEADS, 1, lay.lp), F32)],
        compiler_params=_params(2),
    )(q_aug, k_aug, v_t)


KT_ROWS = FOX_DIM + 16


def _fox_bwd_t(q_aug, k_aug, v, do, k_t, o_t, do_t, lse, lay):
    bk, bq = FOX_BK, FOX_BQ
    nq, nk = lay.lp // bq, lay.lp // bk
    pairs = FOX_HEADS // 2

    def body(q_ref, k_ref, v_ref, do_ref, kt_ref, ot_ref, dot_ref, lse_ref, dqt_ref, dk_ref, dv_ref, delta):
        dqt_ref[...] = jnp.zeros_like(dqt_ref)
        dk_ref[...] = jnp.zeros_like(dk_ref)
        dv_ref[...] = jnp.zeros_like(dv_ref)
        heads = [(hh, slice(hh * AUG, (hh + 1) * AUG), slice(hh * FOX_DIM, (hh + 1) * FOX_DIM),
                  slice(hh * KT_ROWS, (hh + 1) * KT_ROWS)) for hh in range(2)]

        def delta_loop(qb, _):
            q0 = pl.multiple_of(qb * bq, bq)
            for hh, _, cols, _ in heads:
                prod = ot_ref[cols, pl.ds(q0, bq)].astype(F32) * dot_ref[cols, pl.ds(q0, bq)].astype(F32)
                delta[hh, :, pl.ds(q0, bq)] = jnp.sum(prod, axis=0, keepdims=True)
            return 0

        lax.fori_loop(0, nq, delta_loop, 0)

        def k_loop(kb, _):
            k0 = pl.multiple_of(kb * bk, bk)

            def q_step(qb, masked):
                q0 = pl.multiple_of(qb * bq, bq)
                for hh, lanes, cols, trows in heads:
                    k_blk = k_ref[pl.ds(k0, bk), lanes]
                    q_blk = q_ref[pl.ds(q0, bq), lanes]
                    do_blk = do_ref[pl.ds(q0, bq), cols]
                    st = _fox_tile(k_blk, q_blk, k0, q0, masked)
                    pt = jnp.exp(st - lse_ref[hh, :, pl.ds(q0, bq)])
                    dpt = lax.dot_general(v_ref[pl.ds(k0, bk), cols], do_blk, (((1,), (1,)), ((), ())),
                                          preferred_element_type=F32)
                    dst = (pt * (dpt - delta[hh, :, pl.ds(q0, bq)])).astype(q_blk.dtype)
                    dv_ref[pl.ds(k0, bk), cols] += jnp.dot(pt.astype(do_blk.dtype), do_blk, preferred_element_type=F32)
                    dk_ref[pl.ds(k0, bk), lanes] += jnp.dot(dst, q_blk, preferred_element_type=F32)
                    dqt_ref[trows, pl.ds(q0, bq)] += jnp.dot(kt_ref[trows, pl.ds(k0, bk)], dst, preferred_element_type=F32)

            q_step(kb, True)

            def rest(qb, _):
                q_step(qb, False)
                return 0

            lax.fori_loop(kb + 1, nq, rest, 0)
            return 0

        lax.fori_loop(0, nk, k_loop, 0)

    aug = pl.BlockSpec((lay.lp, 2 * AUG), lambda b, p: (b, p))
    rows = pl.BlockSpec((lay.lp, 2 * FOX_DIM), lambda b, p: (b, p))
    tr = pl.BlockSpec((2 * FOX_DIM, lay.lp), lambda b, p: (p, b))
    tr_k = pl.BlockSpec((2 * KT_ROWS, lay.lp), lambda b, p: (p, b))
    return pl.pallas_call(
        body, name="fox_bwd", grid=(lay.batch, pairs),
        in_specs=[aug, aug, rows, rows, tr_k, tr, tr, pl.BlockSpec((2, 1, lay.lp), lambda b, p: (b * pairs + p, 0, 0))],
        out_specs=[tr_k, aug, rows],
        out_shape=[jax.ShapeDtypeStruct((FOX_HEADS * KT_ROWS, lay.n), F32), jax.ShapeDtypeStruct((lay.n, FOX_HEADS * AUG), F32),
                   jax.ShapeDtypeStruct((lay.n, FOX_W), F32)],
        scratch_shapes=[pltpu.VMEM((2, 1, lay.lp), F32)],
        compiler_params=_params(2),
    )(q_aug, k_aug, v, do, k_t, o_t, do_t, lse)


def _hg_tile(lay):
    return lay.tile


def _hgrn_fwd(proj, kk, gl, lay):
    t = _hg_tile(lay)
    nt = lay.lp // t
    nsc = t // SUB

    def body(q_ref, k_ref, g_ref, v_ref, o_ref, st_ref, state, sub_rows):
        @pl.when(pl.program_id(1) == 0)
        def _():
            state[...] = jnp.zeros_like(state)

        rowi = lax.broadcasted_iota(jnp.int32, (SUB, 1), 0)

        def sub(sc, _):
            r0 = pl.multiple_of(sc * SUB, SUB)
            sub_rows[0] = k_ref[pl.ds(r0, SUB), :]
            sub_rows[1] = g_ref[pl.ds(r0, SUB), :]
            sub_rows[2] = v_ref[pl.ds(r0, SUB), :]
            for h in range(HG_HEADS):
                lanes = slice(h * HG_DIM, (h + 1) * HG_DIM)
                q16 = q_ref[pl.ds(r0, SUB), lanes]
                k16 = sub_rows[0, :, lanes]
                g16 = sub_rows[1, :, lanes]
                v16 = sub_rows[2, :, lanes]
                g_end = sub_rows[1, SUB - 1:SUB, lanes]
                s_prev = state[h]
                st_ref[sc, h] = s_prev
                o = lax.dot_general((q16 * jnp.exp(g16)).astype(MXU_DTYPE), s_prev.astype(MXU_DTYPE),
                                    (((1,), (1,)), ((), ())), preferred_element_type=F32)
                for s in range(SUB):
                    ks = sub_rows[0, s:s + 1, lanes]
                    gs = sub_rows[1, s:s + 1, lanes]
                    vs = sub_rows[2, s:s + 1, lanes]
                    w = q16 * jnp.exp(jnp.minimum(g16 - gs, 0.0)) * ks
                    a = jnp.where(rowi >= s, jnp.sum(w, axis=1, keepdims=True), 0.0)
                    o = o + a * vs
                o_ref[pl.ds(r0, SUB), lanes] = o
                kt = k16 * jnp.exp(g_end - g16)
                upd = lax.dot_general(v16.astype(MXU_DTYPE), kt.astype(MXU_DTYPE), (((0,), (0,)), ((), ())),
                                      preferred_element_type=F32)
                state[h] = jnp.exp(g_end) * s_prev + upd
            return 0

        lax.fori_loop(0, nsc, sub, 0)

    rows = lambda col: pl.BlockSpec((t, HG_W), functools.partial(lambda b, i, col: (b * nt + i, col), col=col))
    return pl.pallas_call(
        body, name="hgrn_fwd", grid=(lay.batch, nt),
        in_specs=[rows(C_HQ), rows(0), rows(0), rows(C_HI)],
        out_specs=[rows(0), pl.BlockSpec((nsc, HG_HEADS, HG_DIM, HG_DIM), lambda b, i: (b * nt + i, 0, 0, 0))],
        out_shape=[jax.ShapeDtypeStruct((lay.n, HG_W), F32),
                   jax.ShapeDtypeStruct((lay.n // SUB, HG_HEADS, HG_DIM, HG_DIM), F32)],
        scratch_shapes=[pltpu.VMEM((HG_HEADS, HG_DIM, HG_DIM), F32), pltpu.VMEM((3, SUB, HG_W), F32)],
        compiler_params=_params(2),
    )(proj, kk, gl, proj)


def _hgrn_bwd(proj, kk, gl, do, states, lay):
    t = _hg_tile(lay)
    nt = lay.lp // t
    nsc = t // SUB

    def body(q_ref, k_ref, g_ref, v_ref, do_ref, st_ref, dq_ref, dk_ref, dv_ref, dg_ref, dstate, sub_rows):
        @pl.when(pl.program_id(1) == 0)
        def _():
            dstate[...] = jnp.zeros_like(dstate)

        rowi = lax.broadcasted_iota(jnp.int32, (SUB, 1), 0)

        def sub(it, _):
            sc = nsc - 1 - it
            r0 = pl.multiple_of(sc * SUB, SUB)
            sub_rows[0] = k_ref[pl.ds(r0, SUB), :]
            sub_rows[1] = g_ref[pl.ds(r0, SUB), :]
            sub_rows[2] = v_ref[pl.ds(r0, SUB), :]
            for h in range(HG_HEADS):
                lanes = slice(h * HG_DIM, (h + 1) * HG_DIM)
                q16 = q_ref[pl.ds(r0, SUB), lanes]
                k16 = sub_rows[0, :, lanes]
                g16 = sub_rows[1, :, lanes]
                v16 = sub_rows[2, :, lanes]
                do16 = do_ref[pl.ds(r0, SUB), lanes]
                g_end = sub_rows[1, SUB - 1:SUB, lanes]
                s_prev = st_ref[sc, h]
                ds_end = dstate[h]
                eg = jnp.exp(g16)
                ekt = jnp.exp(g_end - g16)
                e_end = jnp.exp(g_end)
                qt = q16 * eg
                kt = k16 * ekt
                ds_mx = ds_end.astype(MXU_DTYPE)
                dv = lax.dot_general(kt.astype(MXU_DTYPE), ds_mx, (((1,), (1,)), ((), ())), preferred_element_type=F32)
                dkt = jnp.dot(v16.astype(MXU_DTYPE), ds_mx, preferred_element_type=F32)
                dk = dkt * ekt
                ktdkt = kt * dkt
                dg_end = jnp.sum(ktdkt, axis=0, keepdims=True) + jnp.sum(s_prev * ds_end, axis=0, keepdims=True) * e_end
                dg = jnp.where(rowi == SUB - 1, dg_end, 0.0) - ktdkt
                dqt = jnp.dot(do16.astype(MXU_DTYPE), s_prev.astype(MXU_DTYPE), preferred_element_type=F32)
                dq = dqt * eg
                dg = dg + qt * dqt
                dstate[h] = e_end * ds_end + lax.dot_general(do16.astype(MXU_DTYPE), qt.astype(MXU_DTYPE),
                                                             (((0,), (0,)), ((), ())), preferred_element_type=F32)
                for s in range(SUB):
                    ks = sub_rows[0, s:s + 1, lanes]
                    gs = sub_rows[1, s:s + 1, lanes]
                    vs = sub_rows[2, s:s + 1, lanes]
                    live = rowi >= s
                    e = jnp.where(live, jnp.exp(jnp.minimum(g16 - gs, 0.0)), 0.0)
                    qe = q16 * e
                    a = jnp.sum(qe * ks, axis=1, keepdims=True)
                    da = jnp.where(live, jnp.sum(do16 * vs, axis=1, keepdims=True), 0.0)
                    dv_row = jnp.sum(a * do16, axis=0, keepdims=True)
                    t1 = da * qe
                    dk_row = jnp.sum(t1, axis=0, keepdims=True)
                    dq = dq + da * (e * ks)
                    is_s = rowi == s
                    dv = dv + jnp.where(is_s, dv_row, 0.0)
                    dk = dk + jnp.where(is_s, dk_row, 0.0)
                    dg = dg + t1 * ks - jnp.where(is_s, ks * dk_row, 0.0)
                dq_ref[pl.ds(r0, SUB), lanes] = dq
                dk_ref[pl.ds(r0, SUB), lanes] = dk
                dv_ref[pl.ds(r0, SUB), lanes] = dv
                dg_ref[pl.ds(r0, SUB), lanes] = dg
            return 0

        lax.fori_loop(0, nsc, sub, 0)

    def rows(col):
        return pl.BlockSpec((t, HG_W), functools.partial(lambda b, i, col: (b * nt + nt - 1 - i, col), col=col))

    out = jax.ShapeDtypeStruct((lay.n, HG_W), F32)
    return pl.pallas_call(
        body, name="hgrn_bwd", grid=(lay.batch, nt),
        in_specs=[rows(C_HQ), rows(0), rows(0), rows(C_HI), rows(0),
                  pl.BlockSpec((nsc, HG_HEADS, HG_DIM, HG_DIM), lambda b, i: (b * nt + nt - 1 - i, 0, 0, 0))],
        out_specs=[rows(0)] * 4, out_shape=[out] * 4,
        scratch_shapes=[pltpu.VMEM((HG_HEADS, HG_DIM, HG_DIM), F32), pltpu.VMEM((3, SUB, HG_W), F32)],
        compiler_params=_params(2),
    )(proj, kk, gl, proj, do, states)


CONV_COLS = 256


def _shift_down(x, halo, tile, by):
    out = pltpu.roll(x, by, 0)
    rowi = lax.broadcasted_iota(jnp.int32, (tile, 1), 0)
    for r in range(by):
        out = jnp.where(rowi == r, halo[8 - by + r:8 - by + r + 1, :], out)
    return out


def _shift_up(x, halo, tile, by):
    out = pltpu.roll(x, tile - by, 0)
    rowi = lax.broadcasted_iota(jnp.int32, (tile, 1), 0)
    for r in range(by):
        out = jnp.where(rowi == tile - by + r, halo[r:r + 1, :], out)
    return out


def _conv_specs(lay, tile, with_next):
    ncb = D_FF // CONV_COLS
    nblk8 = lay.n // 8
    per8 = tile // 8

    def tile_spec(off):
        return pl.BlockSpec((tile, CONV_COLS), functools.partial(lambda i, j, off: (i, j + off), off=off))

    def prev_spec(off):
        return pl.BlockSpec((8, CONV_COLS), functools.partial(lambda i, j, off: (jnp.maximum(i * per8 - 1, 0), j + off), off=off))

    def next_spec(off):
        return pl.BlockSpec((8, CONV_COLS),
                            functools.partial(lambda i, j, off: (jnp.minimum((i + 1) * per8, nblk8 - 1), j + off), off=off))

    def w_spec(off):
        return pl.BlockSpec((3, CONV_COLS), functools.partial(lambda i, j, off: (0, j + off), off=off))

    def b_spec(off):
        return pl.BlockSpec((1, CONV_COLS), functools.partial(lambda i, j, off: (0, j + off), off=off))

    return ncb, tile_spec, (next_spec if with_next else prev_spec), w_spec, b_spec


def _conv3(x, halo, w, b, tile):
    return w[0:1, :] * _shift_down(x, halo, tile, 2) + w[1:2, :] * _shift_down(x, halo, tile, 1) + w[2:3, :] * x + b


def _conv_act_fwd(u, conv_w, conv_b, lay):
    tile = lay.tile
    ncb, tile_spec, prev_spec, w_spec, b_spec = _conv_specs(lay, tile, False)

    def body(ug, uv, pg, pv, wg, wv, bg, bv, o_ref):
        cg = _conv3(ug[...], pg, wg, bg[...], tile)
        cv = _conv3(uv[...], pv, wv, bv[...], tile)
        o_ref[...] = (_silu(cg) * cv).astype(o_ref.dtype)

    return pl.pallas_call(
        body, name="conv_act_fwd", grid=(lay.n // tile, ncb),
        in_specs=[tile_spec(0), tile_spec(ncb), prev_spec(0), prev_spec(ncb), w_spec(0), w_spec(ncb), b_spec(0), b_spec(ncb)],
        out_specs=pl.BlockSpec((tile, CONV_COLS), lambda i, j: (i, j)),
        out_shape=jax.ShapeDtypeStruct((lay.n, D_FF), MXU_DTYPE),
        compiler_params=_params(2),
    )(u, u, u, u, conv_w, conv_w, conv_b, conv_b)


def _conv_act_bwd(u, dact, conv_w, conv_b, lay):
    tile = lay.tile
    ncb, tile_spec, prev_spec, w_spec, b_spec = _conv_specs(lay, tile, False)

    def body(ug, uv, pg, pv, wg, wv, bg, bv, da_ref, dg_ref, dv_ref, gwg, gwv, gbg, gbv):
        @pl.when(pl.program_id(1) == 0)
        def _():
            for r in (gwg, gwv, gbg, gbv):
                r[...] = jnp.zeros_like(r)

        xg, xv = ug[...], uv[...]
        cg = _conv3(xg, pg, wg, bg[...], tile)
        cv = _conv3(xv, pv, wv, bv[...], tile)
        da = da_ref[...].astype(F32)
        sg = _sigmoid(cg)
        dcv = da * (cg * sg)
        dcg = da * cv * (sg * (1.0 + cg * (1.0 - sg)))
        dg_ref[...] = dcg
        dv_ref[...] = dcv
        for x, halo, dc, gw, gb in ((xg, pg, dcg, gwg, gbg), (xv, pv, dcv, gwv, gbv)):
            gw[0, 0:1, :] += jnp.sum(dc * _shift_down(x, halo, tile, 2), axis=0, keepdims=True)
            gw[0, 1:2, :] += jnp.sum(dc * _shift_down(x, halo, tile, 1), axis=0, keepdims=True)
            gw[0, 2:3, :] += jnp.sum(dc * x, axis=0, keepdims=True)
            gb[0] += jnp.sum(dc, axis=0, keepdims=True)

    swap = lambda spec: pl.BlockSpec(spec.block_shape, functools.partial(lambda j, i, f: f(i, j), f=spec.index_map))
    col = lambda j, i: (i, j)
    red_w = pl.BlockSpec((1, 3, CONV_COLS), lambda j, i: (j, 0, 0))
    red_b = pl.BlockSpec((1, 1, CONV_COLS), lambda j, i: (j, 0, 0))
    outs = pl.pallas_call(
        body, name="conv_act_bwd", grid=(ncb, lay.n // tile),
        in_specs=[swap(s) for s in (tile_spec(0), tile_spec(ncb), prev_spec(0), prev_spec(ncb), w_spec(0), w_spec(ncb),
                                    b_spec(0), b_spec(ncb))] + [pl.BlockSpec((tile, CONV_COLS), col)],
        out_specs=[pl.BlockSpec((tile, CONV_COLS), col), pl.BlockSpec((tile, CONV_COLS), col), red_w, red_w, red_b, red_b],
        out_shape=[jax.ShapeDtypeStruct((lay.n, D_FF), F32), jax.ShapeDtypeStruct((lay.n, D_FF), F32),
                   jax.ShapeDtypeStruct((ncb, 3, CONV_COLS), F32), jax.ShapeDtypeStruct((ncb, 3, CONV_COLS), F32),
                   jax.ShapeDtypeStruct((ncb, 1, CONV_COLS), F32), jax.ShapeDtypeStruct((ncb, 1, CONV_COLS), F32)],
        compiler_params=_params(2),
    )(u, u, u, u, conv_w, conv_w, conv_b, conv_b, dact)
    dcg, dcv, gwg, gwv, gbg, gbv = outs
    unblock = lambda g: jnp.transpose(g, (1, 0, 2)).reshape(g.shape[1], D_FF)
    g_w = jnp.concatenate([unblock(gwg), unblock(gwv)], axis=1)
    g_b = jnp.concatenate([unblock(gbg), unblock(gbv)], axis=1)
    return dcg, dcv, g_w, g_b


def _conv_input_bwd(dcg, dcv, conv_w, lay):
    tile = lay.tile
    ncb, tile_spec, next_spec, w_spec, _ = _conv_specs(lay, tile, True)

    def body(dg, dv, ng, nv, wg, wv, og, ov):
        valid = lay.valid(pl.program_id(0), tile)
        for d, halo, w, o in ((dg, ng, wg, og), (dv, nv, wv, ov)):
            x = d[...]
            du = w[2:3, :] * x + w[1:2, :] * _shift_up(x, halo, tile, 1) + w[0:1, :] * _shift_up(x, halo, tile, 2)
            o[...] = jnp.where(valid, du, 0.0).astype(o.dtype)

    out = pl.BlockSpec((tile, CONV_COLS), lambda i, j: (i, j))
    dug, duv = pl.pallas_call(
        body, name="conv_input_bwd", grid=(lay.n // tile, ncb),
        in_specs=[tile_spec(0), tile_spec(0), next_spec(0), next_spec(0), w_spec(0), w_spec(ncb)],
        out_specs=[out, out],
        out_shape=[jax.ShapeDtypeStruct((lay.n, D_FF), MXU_DTYPE)] * 2,
        compiler_params=_params(2),
    )(dcg, dcv, dcg, dcv, conv_w, conv_w)
    return jnp.concatenate([dug, duv], axis=1)


def _loss_head(h1, mlp, target, lay):
    t = 64
    per = lay.lp // t
    nreal = lay.seq // t
    first = ROW0 // t

    def body(h_ref, m_ref, t_ref, loss_ref, dy_ref, dyb_ref):
        b, j = pl.program_id(0), pl.program_id(1)

        @pl.when((b == 0) & (j == 0))
        def _():
            loss_ref[...] = jnp.zeros_like(loss_ref)

        real = (j >= first) & (j < first + nreal)
        err = jnp.where(real, h_ref[...] + m_ref[...] - t_ref[...], 0.0)
        dy = err * (1.0 / D_MODEL)
        dy_ref[...] = dy
        dyb_ref[...] = dy.astype(dyb_ref.dtype)
        loss_ref[...] += 0.5 * jnp.sum(err * dy)

    rows = pl.BlockSpec((t, D_MODEL), lambda b, j: (b * per + j, 0))
    tgt = pl.BlockSpec((t, D_MODEL), lambda b, j: (b * nreal + jnp.clip(j - first, 0, nreal - 1), 0))
    return pl.pallas_call(
        body, name="loss_head", grid=(lay.batch, per),
        in_specs=[rows, rows, tgt],
        out_specs=[pl.BlockSpec((8, LANES), lambda b, j: (0, 0)), rows, rows],
        out_shape=[jax.ShapeDtypeStruct((8, LANES), F32), jax.ShapeDtypeStruct((lay.n, D_MODEL), F32),
                   jax.ShapeDtypeStruct((lay.n, D_MODEL), MXU_DTYPE)],
        compiler_params=_params(2),
    )(h1, mlp, target)


def _fox_prep(fq, fk, ff, gq, gk, bf, gmat, gmat_t, valid):
    q = _group_rms(fq, gq, gmat, gmat_t, FOX_DIM)
    k = _group_rms(fk, gk, gmat, gmat_t, FOX_DIM)
    logf = jnp.where(valid, _log_sigmoid(ff + bf), 0.0)
    return q, k, logf


def _hg_prep(hf, l0, l1):
    mx = jnp.maximum(l0, l1)
    e0, e1 = jnp.exp(l0 - mx), jnp.exp(l1 - mx)
    lb = e0 / (e0 + e1)
    lf = jnp.log(lb + (1.0 - lb) * _sigmoid(hf))
    kk = (1.0 - lb) * _sigmoid(-hf)
    return lf, kk


def _hg_post(o, hg, gain, gmat, gmat_t):
    return _group_rms(o, jnp.tile(gain, (1, HG_HEADS)), gmat, gmat_t, HG_DIM) * _silu(hg)


def _gate(ga, gb, ya, yb):
    return _sigmoid(ga) * ya + _sigmoid(gb) * yb


def _local_step(x, target, w, lay):
    n, tile = lay.n, lay.tile
    rw = functools.partial(_rowwise, n_rows=n, tile=tile)
    mx = lambda a: a.astype(MXU_DTYPE)

    w_in = w["w_in"]
    fq, fk, fv, ffw, hq, hf, hi, hg, ga, gb = jnp.split(w_in, list(np.cumsum([512, 512, 512, 8, 512, 512, 512, 512, 1024])), axis=1)
    w_main = mx(jnp.concatenate([ga, gb, fq, fk, fv, hq, hf, hi, hg], axis=1))
    w_ff = mx(jnp.pad(ffw, ((0, 0), (0, LANES - FOX_HEADS))))
    w_a, w_b, w_out, w_up, w_down = mx(w["w_branch_a"]), mx(w["w_branch_b"]), mx(w["w_out"]), mx(w["w_up"]), mx(w["w_down"])
    conv_w, conv_b = w["conv_w"].astype(F32), w["conv_b"].astype(F32)
    g1, g2 = w["norm1_gain"], w["norm2_gain"]
    gq, gk = jnp.tile(w["q_norm_gain"], (1, FOX_HEADS)), jnp.tile(w["k_norm_gain"], (1, FOX_HEADS))
    bf = jnp.pad(w["fox_b_f"], ((0, 0), (0, LANES - FOX_HEADS)))
    lb_logits, hg_gain = w["hg_lb_logits"], w["hg_out_gain"]
    gm64, gm64_t = _group_matrix(FOX_W, FOX_DIM)
    gm128, gm128_t = _group_matrix(HG_W, HG_DIM)

    meta = jnp.broadcast_to(w["meta_tokens"].astype(F32)[None], (lay.batch, N_META, D_MODEL))
    h0 = jnp.concatenate([jnp.zeros((lay.batch, LEAD, D_MODEL), F32), meta, x,
                          jnp.zeros((lay.batch, lay.lp - LEAD - lay.l_real, D_MODEL), F32)], axis=1).reshape(n, D_MODEL)

    (xn,) = rw(lambda i, h, g: _rms(h, g), [h0], [g1], [(D_MODEL, MXU_DTYPE)], [], name="norm1")
    proj = _matmul(xn, w_main, name="proj_main")
    pff = _matmul(xn, w_ff, name="proj_ff")

    def fox_prep_fn(i, a, b_, v_, f_, gq_, gk_, bf_, m_, mt_):
        q_, k_, logf = _fox_prep(a, b_, f_, gq_, gk_, bf_, m_, mt_, lay.valid(i, tile))
        return q_, k_, v_, logf

    q, k, v, logf = rw(fox_prep_fn, [(proj, 512, C_FQ), (proj, 512, C_FK), (proj, 512, C_FV), pff], [gq, gk, bf, gm64, gm64_t],
                       [(512, MXU_DTYPE), (512, MXU_DTYPE), (512, MXU_DTYPE), (LANES, F32)], [], name="fox_prep")
    cum = _cumsum_rows(logf, lay, reverse=False, name="fox_cum")
    e1, e2, aug_ones = _aug_matrices()
    q_aug, k_aug = rw(lambda i, q_, k_, c_, e1_, e2_, on_: _fox_augment(q_, k_, c_, lay.valid(i, tile), e1_, e2_, on_),
                      [q, k, cum], [e1, e2, aug_ones], [(FOX_HEADS * AUG, MXU_DTYPE)] * 2, [], name="fox_aug")
    o_t, lse = _fox_fwd_t(q_aug, k_aug, v.T, lay)

    def hg_prep_fn(i, hf_, l0, l1):
        lf, kk_ = _hg_prep(hf_, l0, l1)
        return kk_, _group_cumsum(lf, tile, reverse=False)

    lb0, lb1 = lb_logits[0:1], lb_logits[1:2]
    kk, gl = rw(hg_prep_fn, [(proj, 512, C_HF)], [lb0, lb1], [(512, F32), (512, F32)], [], name="hg_prep")
    o_hg, states = _hgrn_fwd(proj, kk, gl, lay)
    (oh,) = rw(lambda i, o, g_, gain, m_, mt_: _hg_post(o, g_, gain, m_, mt_), [o_hg, (proj, 512, C_HG)],
               [hg_gain, gm128, gm128_t], [(512, MXU_DTYPE)], [], name="hg_post")
    ya = _matmul(oh, w_a, name="branch_a")
    yb = _matmul(o_t, w_b, trans_a=True, name="branch_b")
    pga, pgb = (proj, 1024, C_GA), (proj, 1024, C_GB)
    (merged,) = rw(lambda i, a, b_, c_, d_: _gate(a, b_, c_, d_), [pga, pgb, ya, yb], [], [(D_MODEL, MXU_DTYPE)], [], name="gate")
    mo = _matmul(merged, w_out, name="out_proj")
    h1, hn = rw(lambda i, h, m_, g: (h + m_, _rms(h + m_, g)), [h0, mo], [g2], [(D_MODEL, F32), (D_MODEL, MXU_DTYPE)], [],
                name="norm2")
    u = _matmul(hn, w_up, name="up_proj")
    act = _conv_act_fwd(u, conv_w, conv_b, lay)
    mlp = _matmul(act, w_down, name="down_proj")
    loss_blk, dy, dyb = _loss_head(h1, mlp, target.reshape(lay.batch * lay.seq, D_MODEL), lay)
    loss = loss_blk[0, 0]

    grads = {}
    dact = _matmul(dyb, w_down.T, out_dtype=MXU_DTYPE, name="down_bwd_x")
    grads["w_down"] = _matmul(act, dyb, trans_a=True, name="down_bwd_w")
    dcg, dcv, grads["conv_w"], grads["conv_b"] = _conv_act_bwd(u, dact, conv_w, conv_b, lay)
    du = _conv_input_bwd(dcg, dcv, conv_w, lay)
    dhn = _matmul(du, w_up.T, name="up_bwd_x")
    grads["w_up"] = _matmul(hn, du, trans_a=True, name="up_bwd_w")

    def norm2_bwd(i, h, d_, dy_, g):
        _, vjp = jax.vjp(_rms, h, g)
        dh, dg = vjp(d_)
        return dh + dy_, dh + dy_, dg

    dh1, dh1b, grads["norm2_gain"] = rw(norm2_bwd, [h1, dhn, dy], [g2], [(D_MODEL, F32), (D_MODEL, MXU_DTYPE)], [(1, D_MODEL)],
                                        name="norm2_bwd")
    dmerged = _matmul(dh1b, w_out.T, name="out_bwd_x")
    grads["w_out"] = _matmul(merged, dh1b, trans_a=True, name="out_bwd_w")

    def gate_bwd(i, a, b_, c_, d_, dm):
        _, vjp = jax.vjp(_gate, a, b_, c_, d_)
        return vjp(dm)

    dga, dgb, dya, dyb_ = rw(gate_bwd, [pga, pgb, ya, yb, dmerged], [], [(D_MODEL, MXU_DTYPE)] * 4, [], name="gate_bwd")
    doh = _matmul(dya, w_a.T, name="branch_a_bwd_x")
    grads["w_branch_a"] = _matmul(oh, dya, trans_a=True, name="branch_a_bwd_w")
    dofox = _matmul(dyb_, w_b.T, out_dtype=MXU_DTYPE, name="branch_b_bwd_x")
    grads["w_branch_b"] = _matmul(o_t, dyb_, name="branch_b_bwd_w")

    def hg_post_bwd(i, o, g_, d_, gain, m_, mt_):
        _, vjp = jax.vjp(lambda o__, g__, gain__: _hg_post(o__, g__, gain__, m_, mt_), o, g_, gain)
        return vjp(d_)

    do_hg, dhg, grads["hg_out_gain"] = rw(hg_post_bwd, [o_hg, (proj, 512, C_HG), doh], [hg_gain, gm128, gm128_t],
                                          [(512, F32), (512, MXU_DTYPE)], [(1, HG_DIM)], name="hg_post_bwd")
    dhq, dkk, dhi, dgl = _hgrn_bwd(proj, kk, gl, do_hg, states, lay)

    def hg_prep_bwd(i, hf_, dkk_, dgl_, l0, l1):
        _, vjp = jax.vjp(_hg_prep, hf_, l0, l1)
        return vjp((_group_cumsum(dgl_, tile, reverse=True), dkk_))

    dhf, g_lb0, g_lb1 = rw(hg_prep_bwd, [(proj, 512, C_HF), dkk, dgl], [lb0, lb1], [(512, MXU_DTYPE)], [(1, HG_W), (1, HG_W)],
                           name="hg_prep_bwd")
    grads["hg_lb_logits"] = jnp.concatenate([g_lb0, g_lb1], axis=0)

    k_t = (k.astype(F32) * FOX_SCALE).astype(MXU_DTYPE).T.reshape(FOX_HEADS, FOX_DIM, n)
    k_t = jnp.concatenate([k_t, jnp.ones((FOX_HEADS, KT_ROWS - FOX_DIM, n), MXU_DTYPE)], axis=1).reshape(FOX_HEADS * KT_ROWS, n)
    dq_t, dk_aug, dv = _fox_bwd_t(q_aug, k_aug, v, dofox, k_t, o_t, dofox.T, lse, lay)
    dq_t = dq_t.reshape(FOX_HEADS, KT_ROWS, n)
    dq = dq_t[:, :FOX_DIM].reshape(FOX_W, n).T
    dk_aug = dk_aug.reshape(n, FOX_HEADS, AUG)
    dk = dk_aug[:, :, :FOX_DIM].reshape(n, FOX_W)
    dcum = jnp.pad(dq_t[:, FOX_DIM].T - dk_aug[:, :, FOX_DIM], ((0, 0), (0, LANES - FOX_HEADS)))
    dlogf = _cumsum_rows(dcum, lay, reverse=True, name="fox_cum_bwd")

    def fox_prep_bwd(i, a, b_, f_, dq_, dk_, dl_, gq_, gk_, bf_, m_, mt_):
        valid = lay.valid(i, tile)
        _, vjp = jax.vjp(lambda a_, b__, f__, gq__, gk__, bf__: _fox_prep(a_, b__, f__, gq__, gk__, bf__, m_, mt_, valid),
                         a, b_, f_, gq_, gk_, bf_)
        return vjp((dq_, dk_, dl_))

    dfq, dfk, dff, g_gq, g_gk, g_bf = rw(
        fox_prep_bwd, [(proj, 512, C_FQ), (proj, 512, C_FK), pff, dq, dk, dlogf], [gq, gk, bf, gm64, gm64_t],
        [(512, MXU_DTYPE), (512, MXU_DTYPE), (LANES, MXU_DTYPE)], [(1, FOX_W), (1, FOX_W), (1, LANES)], name="fox_prep_bwd")
    grads["q_norm_gain"] = g_gq.reshape(FOX_HEADS, FOX_DIM).sum(0, keepdims=True)
    grads["k_norm_gain"] = g_gk.reshape(FOX_HEADS, FOX_DIM).sum(0, keepdims=True)
    grads["fox_b_f"] = g_bf[:, :FOX_HEADS]

    dproj = jnp.concatenate([dga, dgb, dfq, dfk, mx(dv), mx(dhq), dhf, mx(dhi), dhg], axis=1)
    dxn = _matmul(dproj, w_main.T, name="proj_bwd_x")
    dxn_ff = _matmul(dff, w_ff.T, name="proj_ff_bwd_x")
    g_main = _matmul(xn, dproj, trans_a=True, name="proj_bwd_w")
    g_ff = _matmul(xn, dff, trans_a=True, name="proj_ff_bwd_w")[:, :FOX_HEADS]
    p = jnp.split(g_main, list(np.cumsum([1024, 1024] + [512] * 6)), axis=1)
    grads["w_in"] = jnp.concatenate([p[2], p[3], p[4], g_ff, p[5], p[6], p[7], p[8], p[0], p[1]], axis=1)

    per = lay.lp // tile

    def norm1_bwd(i, h, d1, d2, dh1_, g):
        _, vjp = jax.vjp(_rms, h, g)
        dh, dg = vjp(d1 + d2)
        dh = dh + dh1_
        dmeta = jnp.where(lax.rem(i, per) == 0, dh[LEAD:LEAD + N_META, :], 0.0)
        return dh, dg, dmeta

    dh0, grads["norm1_gain"], grads["meta_tokens"] = rw(norm1_bwd, [h0, dxn, dxn_ff, dh1], [g1], [(D_MODEL, F32)],
                                                       [(1, D_MODEL), (N_META, D_MODEL)], name="norm1_bwd")
    grad_x = dh0.reshape(lay.batch, lay.lp, D_MODEL)[:, ROW0:ROW0 + lay.seq]
    return loss, grad_x, grads


MESH = pl.DeviceIdType.MESH
HBM_SPEC = pl.BlockSpec(memory_space=pltpu.HBM)
N_CHIPS = 4
WEIGHT_NAMES = ["meta_tokens", "norm1_gain", "w_in", "fox_b_f", "q_norm_gain", "k_norm_gain", "hg_lb_logits", "hg_out_gain",
                "w_branch_a", "w_branch_b", "w_out", "norm2_gain", "w_up", "conv_w", "conv_b", "w_down"]
COL_SHARDED = ("meta_tokens", "w_in", "w_branch_a", "w_branch_b", "w_up", "conv_w")
ROW_SHARDED = ("w_out", "w_down")
REPLICATED = ("norm1_gain", "fox_b_f", "q_norm_gain", "k_norm_gain", "hg_lb_logits", "hg_out_gain", "norm2_gain", "conv_b")
GATHER_BF16 = ("w_in", "w_branch_a", "w_branch_b", "w_out", "w_up", "w_down")
GATHER_F32 = ("conv_w", "meta_tokens")
PACK_ROWS = 32768
HALF_ROWS = PACK_ROWS // 2


def _position():
    return lax.axis_index("x"), lax.axis_index("y"), lax.axis_index("c")


def _other_chips(x, y):
    return [(1 - x, y), (x, 1 - y), (1 - x, 1 - y)]


def _all_gather8(mine):
    rows = mine.shape[0]

    def body(x_ref, out_ref, send_sems, recv_sems, local_sem):
        x, y, c = _position()
        me, sibling = (x, y, c), (x, y, 1 - c)
        chips = _other_chips(x, y)

        def blk(px, py, pc):
            return out_ref.at[4 * px + 2 * py + pc]

        def copy(k, block, to, src=None):
            return pltpu.make_async_remote_copy(
                src_ref=blk(*block) if src is None else src, dst_ref=blk(*block),
                send_sem=send_sems.at[k], recv_sem=recv_sems.at[k], device_id=to, device_id_type=MESH)

        own = pltpu.make_async_copy(x_ref, blk(*me), local_sem)
        own.start()
        first = [copy(0, me, sibling, src=x_ref)] + [copy(1 + j, me, (*chip, c), src=x_ref) for j, chip in enumerate(chips)]
        for cp in first:
            cp.start()
        passed = [copy(4 + j, (*chip, c), sibling) for j, chip in enumerate(chips)]
        for j, chip in enumerate(chips):
            copy(1 + j, (*chip, c), me).wait_recv()
            passed[j].start()
        copy(0, sibling, me).wait_recv()
        for j, chip in enumerate(chips):
            copy(4 + j, (*chip, 1 - c), me).wait_recv()
        for cp in first + passed:
            cp.wait_send()
        own.wait()

    return pl.pallas_call(
        body, name="gather_weights", out_shape=jax.ShapeDtypeStruct((8, rows, LANES), mine.dtype),
        in_specs=[HBM_SPEC], out_specs=HBM_SPEC,
        scratch_shapes=[pltpu.SemaphoreType.DMA((7,)), pltpu.SemaphoreType.DMA((7,)), pltpu.SemaphoreType.DMA],
    )(mine)


def _sibling_exchange(g):
    rows = g.shape[2]

    def body(g_ref, own_ref, got_ref, send_sems, recv_sems, local_sems):
        x, y, c = _position()
        sibling = (x, y, 1 - c)
        local = [pltpu.make_async_copy(g_ref.at[j, c], own_ref.at[j], local_sems.at[j]) for j in range(N_CHIPS)]
        remote = [pltpu.make_async_remote_copy(src_ref=g_ref.at[j, 1 - c], dst_ref=got_ref.at[j], send_sem=send_sems.at[j],
                                               recv_sem=recv_sems.at[j], device_id=sibling, device_id_type=MESH)
                  for j in range(N_CHIPS)]
        for cp in remote + local:
            cp.start()
        for cp in remote + local:
            cp.wait()

    out = jax.ShapeDtypeStruct((N_CHIPS, rows, LANES), g.dtype)
    return pl.pallas_call(
        body, name="reduce_sibling", out_shape=[out, out], in_specs=[HBM_SPEC], out_specs=[HBM_SPEC, HBM_SPEC],
        scratch_shapes=[pltpu.SemaphoreType.DMA((N_CHIPS,)), pltpu.SemaphoreType.DMA((N_CHIPS,)),
                        pltpu.SemaphoreType.DMA((N_CHIPS,))],
    )(g)


def _chip_exchange(part):
    rows = part.shape[1]

    def body(p_ref, got_ref, send_sems, recv_sems, local_sem):
        x, y, c = _position()
        mine = 2 * x + y
        chips = _other_chips(x, y)
        own = pltpu.make_async_copy(p_ref.at[mine], got_ref.at[mine], local_sem)
        own.start()
        sends = [pltpu.make_async_remote_copy(src_ref=p_ref.at[2 * cx + cy], dst_ref=got_ref.at[mine], send_sem=send_sems.at[j],
                                              recv_sem=recv_sems.at[j], device_id=(cx, cy, c), device_id_type=MESH)
                 for j, (cx, cy) in enumerate(chips)]
        for cp in sends:
            cp.start()
        for j, (cx, cy) in enumerate(chips):
            pltpu.make_async_remote_copy(src_ref=p_ref.at[mine], dst_ref=got_ref.at[2 * cx + cy], send_sem=send_sems.at[j],
                                         recv_sem=recv_sems.at[j], device_id=(cx, cy, c), device_id_type=MESH).wait_recv()
        for cp in sends:
            cp.wait_send()
        own.wait()

    return pl.pallas_call(
        body, name="reduce_chips", out_shape=jax.ShapeDtypeStruct((N_CHIPS, rows, LANES), part.dtype),
        in_specs=[HBM_SPEC], out_specs=HBM_SPEC,
        scratch_shapes=[pltpu.SemaphoreType.DMA((3,)), pltpu.SemaphoreType.DMA((3,)), pltpu.SemaphoreType.DMA],
    )(part)


def _sibling_gather(half):
    rows = half.shape[0]

    def body(h_ref, out_ref, send_sem, recv_sem, local_sem):
        x, y, c = _position()
        own = pltpu.make_async_copy(h_ref, out_ref.at[c], local_sem)
        own.start()
        send = pltpu.make_async_remote_copy(src_ref=h_ref, dst_ref=out_ref.at[c], send_sem=send_sem, recv_sem=recv_sem,
                                            device_id=(x, y, 1 - c), device_id_type=MESH)
        send.start()
        pltpu.make_async_remote_copy(src_ref=h_ref, dst_ref=out_ref.at[1 - c], send_sem=send_sem, recv_sem=recv_sem,
                                     device_id=(x, y, 1 - c), device_id_type=MESH).wait_recv()
        send.wait_send()
        own.wait()

    return pl.pallas_call(
        body, name="reduce_gather", out_shape=jax.ShapeDtypeStruct((2, rows, LANES), half.dtype),
        in_specs=[HBM_SPEC], out_specs=HBM_SPEC,
        scratch_shapes=[pltpu.SemaphoreType.DMA, pltpu.SemaphoreType.DMA, pltpu.SemaphoreType.DMA],
    )(half)


def _add_pairs(a, b):
    rows = a.shape[0]
    tile = _pick(rows, (2048, 1024, 512))

    def body(a_ref, b_ref, o_ref):
        o_ref[...] = a_ref[...] + b_ref[...]

    spec = pl.BlockSpec((tile, LANES), lambda i: (i, 0))
    return pl.pallas_call(body, name="reduce_add2", grid=(rows // tile,), in_specs=[spec, spec], out_specs=spec,
                          out_shape=jax.ShapeDtypeStruct(a.shape, a.dtype), compiler_params=_params(1))(a, b)


def _add_chips(got):
    rows = got.shape[1]
    tile = _pick(rows, (2048, 1024, 512))

    def body(g_ref, o_ref):
        o_ref[...] = ((g_ref[0] + g_ref[1]) + g_ref[2]) + g_ref[3]

    return pl.pallas_call(body, name="reduce_add4", grid=(rows // tile,),
                          in_specs=[pl.BlockSpec((N_CHIPS, tile, LANES), lambda i: (0, i, 0))],
                          out_specs=pl.BlockSpec((tile, LANES), lambda i: (i, 0)),
                          out_shape=jax.ShapeDtypeStruct((rows, LANES), got.dtype), compiler_params=_params(1))(got)


def _adamw(w, g, m, v):
    rows = w.shape[0]
    tile = _pick(rows, (2048, 1024, 512))
    c1 = 1.0 / (1.0 - ADAM_B1 ** ADAM_STEP)
    c2 = 1.0 / (1.0 - ADAM_B2 ** ADAM_STEP)

    def body(w_ref, g_ref, m_ref, v_ref, d_ref, mo_ref, vo_ref):
        g_ = g_ref[...]
        m_new = ADAM_B1 * m_ref[...] + (1.0 - ADAM_B1) * g_
        v_new = ADAM_B2 * v_ref[...] + (1.0 - ADAM_B2) * (g_ * g_)
        d_ref[...] = -ADAM_LR * ((m_new * c1) / (jnp.sqrt(v_new * c2) + ADAM_EPS) + ADAM_WD * w_ref[...])
        mo_ref[...] = m_new
        vo_ref[...] = v_new

    spec = pl.BlockSpec((tile, LANES), lambda i: (i, 0))
    out = jax.ShapeDtypeStruct(w.shape, F32)
    return pl.pallas_call(body, name="adamw", grid=(rows // tile,), in_specs=[spec] * 4, out_specs=[spec] * 3,
                          out_shape=[out] * 3, compiler_params=_params(1))(w, g, m, v)


def _shard_shape(name, full_shape):
    if name in COL_SHARDED:
        return full_shape[:-1] + (full_shape[-1] // N_CHIPS,)
    if name in ROW_SHARDED:
        return (full_shape[0] // N_CHIPS,) + full_shape[1:]
    return full_shape


def _to_rows(flat, rows):
    return jnp.pad(flat, (0, rows * LANES - flat.shape[0])).reshape(rows, LANES)


def _pack_local(tree):
    return _to_rows(jnp.concatenate([tree[n].astype(F32).reshape(-1) for n in WEIGHT_NAMES]), PACK_ROWS)


def _unpack_local(packed, shapes):
    flat, out, at = packed.reshape(-1), {}, 0
    for n in WEIGHT_NAMES:
        size = int(np.prod(shapes[n]))
        out[n] = flat[at:at + size].reshape(shapes[n])
        at += size
    return out


def _pack_by_chip(grads, full_shapes):
    pieces = []
    for n in WEIGHT_NAMES:
        g = grads[n].astype(F32).reshape(full_shapes[n])
        if n in COL_SHARDED:
            g2 = g.reshape(-1, N_CHIPS, g.shape[-1] // N_CHIPS)
            pieces.append(jnp.transpose(g2, (1, 0, 2)).reshape(N_CHIPS, -1))
        elif n in ROW_SHARDED:
            pieces.append(g.reshape(N_CHIPS, -1))
        else:
            pieces.append(jnp.broadcast_to(g.reshape(1, -1), (N_CHIPS, g.size)))
    flat = jnp.concatenate(pieces, axis=1)
    return jnp.pad(flat, ((0, 0), (0, PACK_ROWS * LANES - flat.shape[1]))).reshape(N_CHIPS, PACK_ROWS, LANES)


def _gather_weights(local, full_shapes):
    parts = [local[n].astype(BF16).reshape(-1) for n in GATHER_BF16]
    parts += [lax.bitcast_convert_type(local[n].astype(F32).reshape(-1), BF16).reshape(-1) for n in GATHER_F32]
    flat = jnp.concatenate(parts)
    rows = -(-flat.shape[0] // (2 * 16 * LANES)) * 16
    both = _to_rows(flat, 2 * rows).reshape(2, rows, LANES)
    mine = lax.dynamic_index_in_dim(both, lax.axis_index("c"), axis=0, keepdims=False)
    got = _all_gather8(mine).reshape(N_CHIPS, -1)
    out, at = {}, 0
    for n in GATHER_BF16 + GATHER_F32:
        shard = _shard_shape(n, full_shapes[n])
        size = int(np.prod(shard)) * (2 if n in GATHER_F32 else 1)
        piece = got[:, at:at + size]
        at += size
        if n in GATHER_F32:
            piece = lax.bitcast_convert_type(piece.reshape(N_CHIPS, -1, 2), F32)
        piece = piece.reshape((N_CHIPS,) + shard)
        if n in COL_SHARDED:
            piece = jnp.moveaxis(piece, 0, -2).reshape(full_shapes[n])
        else:
            piece = piece.reshape(full_shapes[n])
        out[n] = piece
    return out


def kernel(x, meta_tokens, norm1_gain, w_in, fox_b_f, q_norm_gain, k_norm_gain, hg_lb_logits, hg_out_gain, w_branch_a, w_branch_b, w_out, norm2_gain, w_up, conv_w, conv_b, w_down, loss_target, m_meta_tokens, m_norm1_gain, m_w_in, m_fox_b_f, m_q_norm_gain, m_k_norm_gain, m_hg_lb_logits, m_hg_out_gain, m_w_branch_a, m_w_branch_b, m_w_out, m_norm2_gain, m_w_up, m_conv_w, m_conv_b, m_w_down, v_meta_tokens, v_norm1_gain, v_w_in, v_fox_b_f, v_q_norm_gain, v_k_norm_gain, v_hg_lb_logits, v_hg_out_gain, v_w_branch_a, v_w_branch_b, v_w_out, v_norm2_gain, v_w_up, v_conv_w, v_conv_b, v_w_down):
    w_loc = dict(zip(WEIGHT_NAMES, (meta_tokens, norm1_gain, w_in, fox_b_f, q_norm_gain, k_norm_gain, hg_lb_logits, hg_out_gain,
                                    w_branch_a, w_branch_b, w_out, norm2_gain, w_up, conv_w, conv_b, w_down)))
    m_loc = dict(zip(WEIGHT_NAMES, (m_meta_tokens, m_norm1_gain, m_w_in, m_fox_b_f, m_q_norm_gain, m_k_norm_gain, m_hg_lb_logits,
                                    m_hg_out_gain, m_w_branch_a, m_w_branch_b, m_w_out, m_norm2_gain, m_w_up, m_conv_w, m_conv_b,
                                    m_w_down)))
    v_loc = dict(zip(WEIGHT_NAMES, (v_meta_tokens, v_norm1_gain, v_w_in, v_fox_b_f, v_q_norm_gain, v_k_norm_gain, v_hg_lb_logits,
                                    v_hg_out_gain, v_w_branch_a, v_w_branch_b, v_w_out, v_norm2_gain, v_w_up, v_conv_w, v_conv_b,
                                    v_w_down)))
    local_shapes = {n: tuple(w_loc[n].shape) for n in WEIGHT_NAMES}
    squeeze = lambda s: s[1:] if len(s) == 3 else s
    full_shapes = {}
    for n in WEIGHT_NAMES:
        s = squeeze(local_shapes[n])
        if n in COL_SHARDED:
            s = s[:-1] + (s[-1] * N_CHIPS,)
        elif n in ROW_SHARDED:
            s = (s[0] * N_CHIPS,) + s[1:]
        full_shapes[n] = s

    weights = {n: w_loc[n].reshape(full_shapes[n]) for n in REPLICATED}
    weights.update(_gather_weights(w_loc, full_shapes))

    lay = _Layout(x.shape[0], x.shape[1])
    loss, grad_x, grads = _local_step(x, loss_target, weights, lay)
    loss = lax.psum(loss, ("x", "y", "c"))

    by_chip = _pack_by_chip(grads, full_shapes).reshape(N_CHIPS, 2, HALF_ROWS, LANES)
    own, got = _sibling_exchange(by_chip)
    part = _add_pairs(own.reshape(-1, LANES), got.reshape(-1, LANES)).reshape(N_CHIPS, HALF_ROWS, LANES)
    half = _add_chips(_chip_exchange(part))
    g_packed = _sibling_gather(half).reshape(PACK_ROWS, LANES)

    delta, new_m, new_v = _adamw(_pack_local(w_loc), g_packed, _pack_local(m_loc), _pack_local(v_loc))
    outs = [_unpack_local(a, local_shapes) for a in (g_packed, delta, new_m, new_v)]
    return (loss, grad_x, *[o[n] for o in outs for n in WEIGHT_NAMES])
```

```python
import functools

import jax
import jax.numpy as jnp
import numpy as np
from jax import lax
from jax.experimental import pallas as pl
from jax.experimental.pallas import tpu as pltpu

F32 = jnp.float32
BF16 = jnp.bfloat16
MXU_DTYPE = BF16
HIGHEST = lax.Precision.HIGHEST

D_MODEL = 1024
N_META = 16
LEAD = 48
ROW0 = LEAD + N_META
FOX_HEADS, FOX_DIM, FOX_W = 8, 64, 512
HG_HEADS, HG_DIM, HG_W = 4, 128, 512
D_FF = 2816
FF2 = 2 * D_FF
EPS = 1e-6
SUB = 16
LANES = 128
N_CHIPS = 4
NEG = -1e30

ADAM_LR, ADAM_B1, ADAM_B2, ADAM_EPS, ADAM_WD, ADAM_STEP = 0.001, 0.9, 0.999, 1e-08, 0.01, 10

VMEM_LIMIT = 56 * 1024 * 1024

C_GA, C_GB = 0, 1
C_FQ, C_FK, C_FV, C_HQ, C_HF, C_HI, C_HG = 4, 5, 6, 7, 8, 9, 10


def _params(n_axes=1):
    return pltpu.CompilerParams(dimension_semantics=("arbitrary",) * n_axes, vmem_limit_bytes=VMEM_LIMIT)


def _pick(n, cands):
    for c in cands:
        if n % c == 0:
            return c
    raise ValueError(f"no tile for {n} among {cands}")


def _rowwise(fn, rows, consts, outs, reds, *, n_rows, tile, name):
    assert n_rows % tile == 0
    rows = [r if isinstance(r, tuple) else (r, r.shape[1], 0) for r in rows]
    nr, nc, no = len(rows), len(consts), len(outs)

    def body(*refs):
        i = pl.program_id(0)
        ins = [r[...] for r in refs[:nr + nc]]
        res = fn(i, *ins)
        res = res if isinstance(res, (tuple, list)) else (res,)
        for ref, v in zip(refs[nr + nc:nr + nc + no], res[:no]):
            ref[...] = v.astype(ref.dtype)
        red_refs = refs[nr + nc + no:]
        if red_refs:
            @pl.when(i == 0)
            def _():
                for ref in red_refs:
                    ref[...] = jnp.zeros_like(ref)
            for ref, v in zip(red_refs, res[no:]):
                ref[...] += v.astype(F32)

    in_specs = [pl.BlockSpec((tile, w), functools.partial(lambda i, j: (i, j), j=j)) for (_, w, j) in rows]
    in_specs += [pl.BlockSpec(c.shape, functools.partial(lambda i, nd: (0,) * nd, nd=c.ndim)) for c in consts]
    out_specs = [pl.BlockSpec((tile, w), lambda i: (i, 0)) for (w, _) in outs]
    out_specs += [pl.BlockSpec(s, functools.partial(lambda i, nd: (0,) * nd, nd=len(s))) for s in reds]
    out_shape = [jax.ShapeDtypeStruct((n_rows, w), dt) for (w, dt) in outs]
    out_shape += [jax.ShapeDtypeStruct(s, F32) for s in reds]
    return pl.pallas_call(
        body, name=name, grid=(n_rows // tile,), in_specs=in_specs, out_specs=out_specs, out_shape=out_shape,
        compiler_params=_params(1),
    )(*[r[0] for r in rows], *consts)


def _matmul(a, b, *, trans_a=False, trans_b=False, out_dtype=F32, by_chip=False, name):
    if trans_a:
        k, m = a.shape
    else:
        m, k = a.shape
    n = b.shape[0] if trans_b else b.shape[1]
    assert (b.shape[1] if trans_b else b.shape[0]) == k
    if trans_a:
        tm = _pick(m, (1408, 1024, 512, 256, 128))
        tk = _pick(k, (1088, 1024, 512))
    else:
        tm = _pick(m, (512, 256, 128))
        tk = k if k <= 1024 else _pick(k, (1408, 1024, 512))
    tn = n // N_CHIPS if by_chip else _pick(n, (1408, 1024, 512, 256, 128))
    nk = k // tk
    dims = (((0 if trans_a else 1,), (1 if trans_b else 0,)), ((), ()))

    def body(a_ref, b_ref, o_ref, acc_ref):
        out = o_ref.at[0] if by_chip else o_ref
        part = lax.dot_general(a_ref[...], b_ref[...], dims, preferred_element_type=F32)
        if nk == 1:
            out[...] = part.astype(out.dtype)
        else:
            kk = pl.program_id(2)

            @pl.when(kk == 0)
            def _():
                acc_ref[...] = part

            @pl.when(kk > 0)
            def _():
                acc_ref[...] += part

            @pl.when(kk == nk - 1)
            def _():
                out[...] = acc_ref[...].astype(out.dtype)

    a_spec = pl.BlockSpec((tk, tm), lambda i, j, kk: (kk, i)) if trans_a else pl.BlockSpec((tm, tk), lambda i, j, kk: (i, kk))
    b_spec = pl.BlockSpec((tn, tk), lambda i, j, kk: (j, kk)) if trans_b else pl.BlockSpec((tk, tn), lambda i, j, kk: (kk, j))
    if by_chip:
        out_spec, out_shape = pl.BlockSpec((1, tm, tn), lambda i, j, kk: (j, i, 0)), (N_CHIPS, m, tn)
    else:
        out_spec, out_shape = pl.BlockSpec((tm, tn), lambda i, j, kk: (i, j)), (m, n)
    return pl.pallas_call(
        body, name=name, grid=(m // tm, n // tn, nk), in_specs=[a_spec, b_spec], out_specs=out_spec,
        out_shape=jax.ShapeDtypeStruct(out_shape, out_dtype),
        scratch_shapes=[pltpu.VMEM((tm, tn) if nk > 1 else (8, LANES), F32)],
        compiler_params=_params(3),
    )(a, b)


def _sigmoid(x):
    return 1.0 / (1.0 + jnp.exp(-x))


def _silu(x):
    return x * _sigmoid(x)


def _log_sigmoid(x):
    return jnp.minimum(x, 0.0) - jnp.log(1.0 + jnp.exp(-jnp.abs(x)))


def _rms(x, gain):
    return x * lax.rsqrt(jnp.mean(x * x, axis=-1, keepdims=True) + EPS) * gain


def _group_matrix(width, group):
    g = (np.arange(width)[:, None] // group == np.arange(LANES)[None, :]).astype(np.float32)
    return jnp.asarray(g), jnp.asarray(g.T.copy())


def _group_rms(x, gain, gmat, gmat_t, group):
    ms = jnp.dot(x * x, gmat, precision=HIGHEST, preferred_element_type=F32) * (1.0 / group)
    rstd = lax.rsqrt(ms + EPS)
    return x * jnp.dot(rstd, gmat_t, precision=HIGHEST, preferred_element_type=F32) * gain


class _Layout:
    def __init__(self, batch, seq):
        self.batch, self.seq = batch, seq
        self.l_real = N_META + seq
        self.lp = -(-(LEAD + self.l_real) // 256) * 256
        self.n = batch * self.lp
        self.tile = _pick(self.lp, (512, 256))

    def valid(self, i, tile):
        per = self.lp // tile
        r = lax.rem(i, per) * tile + lax.broadcasted_iota(jnp.int32, (tile, 1), 0)
        return (r >= LEAD) & (r < LEAD + self.l_real)


def _cumsum_rows(x, lay, *, reverse, name):
    t = LANES
    nt = lay.lp // t
    c = x.shape[1]

    def body(x_ref, o_ref, carry):
        j = pl.program_id(1)

        @pl.when(j == 0)
        def _():
            carry[...] = jnp.zeros_like(carry)

        r = lax.broadcasted_iota(jnp.int32, (t, t), 0)
        q = lax.broadcasted_iota(jnp.int32, (t, t), 1)
        tri = jnp.where((q >= r) if reverse else (q <= r), 1.0, 0.0).astype(F32)
        xs = x_ref[...]
        out = jnp.dot(tri, xs, precision=HIGHEST, preferred_element_type=F32) + carry[0:1, :]
        o_ref[...] = out
        carry[...] = jnp.broadcast_to(carry[0:1, :] + jnp.sum(xs, axis=0, keepdims=True), carry.shape)

    def idx(b, j):
        return (b * nt + (nt - 1 - j if reverse else j), 0)

    return pl.pallas_call(
        body, name=name, grid=(lay.batch, nt),
        in_specs=[pl.BlockSpec((t, c), idx)], out_specs=pl.BlockSpec((t, c), idx),
        out_shape=jax.ShapeDtypeStruct(x.shape, F32),
        scratch_shapes=[pltpu.VMEM((8, c), F32)],
        compiler_params=_params(2),
    )(x)


def _group_cumsum(x, tile, *, reverse):
    r = lax.rem(lax.broadcasted_iota(jnp.int32, (tile, 1), 0), SUB)
    s = 1
    while s < SUB:
        if reverse:
            x = x + jnp.where(r < SUB - s, pltpu.roll(x, tile - s, 0), 0.0)
        else:
            x = x + jnp.where(r >= s, pltpu.roll(x, s, 0), 0.0)
        s *= 2
    return x


AUG = 128
FOX_BK = 256
FOX_BQ = 256
FOX_SCALE = FOX_DIM ** -0.5
KT_ROWS = FOX_DIM + 16


def _aug_matrices():
    e1 = np.zeros((FOX_W, FOX_HEADS * AUG), np.float32)
    e2 = np.zeros((LANES, FOX_HEADS * AUG), np.float32)
    ones = np.zeros((1, FOX_HEADS * AUG), np.float32)
    for h in range(FOX_HEADS):
        for d in range(FOX_DIM):
            e1[h * FOX_DIM + d, h * AUG + d] = 1.0
        for j in range(3):
            e2[j * FOX_HEADS + h, h * AUG + FOX_DIM + j] = 1.0
            ones[0, h * AUG + FOX_DIM + j] = 1.0
    return jnp.asarray(e1, MXU_DTYPE), jnp.asarray(e2, MXU_DTYPE), jnp.asarray(ones)


def _fox_augment(q, k, cum, key_ok, e1, e2, ones):
    dt = q.dtype
    c = jnp.where(key_ok, -cum, NEG)
    hi = c.astype(dt)
    r1 = c - hi.astype(F32)
    mid = r1.astype(dt)
    lo = (r1 - mid.astype(F32)).astype(dt)
    lane = lax.broadcasted_iota(jnp.int32, c.shape, 1)
    shift = lambda a, by: pltpu.roll(a.astype(F32), by, 1)
    parts = jnp.where(lane < FOX_HEADS, hi.astype(F32),
                      jnp.where(lane < 2 * FOX_HEADS, shift(mid, FOX_HEADS),
                                jnp.where(lane < 3 * FOX_HEADS, shift(lo, 2 * FOX_HEADS), 0.0))).astype(dt)
    qs = (q.astype(F32) * FOX_SCALE).astype(dt)
    q_aug = jnp.dot(qs, e1, preferred_element_type=F32) + ones
    k_aug = jnp.dot(k, e1, preferred_element_type=F32) + jnp.dot(parts, e2, preferred_element_type=F32)
    return q_aug.astype(dt), k_aug.astype(dt)


def _fox_tile(k_blk, q_blk, k0, q0, masked):
    st = lax.dot_general(k_blk, q_blk, (((1,), (1,)), ((), ())), preferred_element_type=F32)
    if masked:
        keys = k0 + lax.broadcasted_iota(jnp.int32, st.shape, 0)
        qs = q0 + lax.broadcasted_iota(jnp.int32, st.shape, 1)
        st = jnp.where(keys <= qs, st, NEG)
    return st


def _fox_fwd_t(q_aug, k_aug, v_t, lay):
    bk, bq = FOX_BK, FOX_BQ
    nq = lay.lp // bq
    pairs = FOX_HEADS // 2

    def body(q_ref, k_ref, vt_ref, ot_ref, lse_ref):
        heads = [(slice(hh * AUG, (hh + 1) * AUG), slice(hh * FOX_DIM, (hh + 1) * FOX_DIM)) for hh in range(2)]

        def q_loop(qb, _):
            q0 = pl.multiple_of(qb * bq, bq)
            q_blks = [q_ref[pl.ds(q0, bq), lanes] for lanes, _ in heads]

            def k_step(kb, carries, masked):
                k0 = pl.multiple_of(kb * bk, bk)
                out = []
                for (lanes, vrows), q_blk, (m, l, acc) in zip(heads, q_blks, carries):
                    st = _fox_tile(k_ref[pl.ds(k0, bk), lanes], q_blk, k0, q0, masked)
                    m_new = jnp.maximum(m, jnp.max(st, axis=0, keepdims=True))
                    alpha = jnp.exp(m - m_new)
                    p = jnp.exp(st - m_new)
                    l = alpha * l + jnp.sum(p, axis=0, keepdims=True)
                    acc = alpha * acc + jnp.dot(vt_ref[vrows, pl.ds(k0, bk)], p.astype(vt_ref.dtype),
                                                preferred_element_type=F32)
                    out.append((m_new, l, acc))
                return tuple(out)

            init = (jnp.full((1, bq), NEG, F32), jnp.zeros((1, bq), F32), jnp.zeros((FOX_DIM, bq), F32))
            carries = lax.fori_loop(0, qb, lambda kb, c: k_step(kb, c, False), (init, init))
            carries = k_step(qb, carries, True)
            qs = q0 + lax.broadcasted_iota(jnp.int32, (1, bq), 1)
            ok = (qs >= LEAD) & (qs < LEAD + lay.l_real)
            for hh, ((_, vrows), (m, l, acc)) in enumerate(zip(heads, carries)):
                ot_ref[vrows, pl.ds(q0, bq)] = jnp.where(ok, acc / l, 0.0).astype(ot_ref.dtype)
                lse_ref[hh, :, pl.ds(q0, bq)] = m + jnp.log(l)
            return 0

        lax.fori_loop(0, nq, q_loop, 0)

    aug = pl.BlockSpec((lay.lp, 2 * AUG), lambda b, p: (b, p))
    tr = pl.BlockSpec((2 * FOX_DIM, lay.lp), lambda b, p: (p, b))
    return pl.pallas_call(
        body, name="fox_fwd", grid=(lay.batch, pairs),
        in_specs=[aug, aug, tr],
        out_specs=[tr, pl.BlockSpec((2, 1, lay.lp), lambda b, p: (b * pairs + p, 0, 0))],
        out_shape=[jax.ShapeDtypeStruct((FOX_W, lay.n), MXU_DTYPE),
                   jax.ShapeDtypeStruct((lay.batch * FOX_HEADS, 1, lay.lp), F32)],
        compiler_params=_params(2),
    )(q_aug, k_aug, v_t)


def _fox_bwd_t(q_aug, k_aug, v, do, k_t, o_t, do_t, lse, lay):
    bk, bq = FOX_BK, FOX_BQ
    nq, nk = lay.lp // bq, lay.lp // bk
    pairs = FOX_HEADS // 2

    def body(q_ref, k_ref, v_ref, do_ref, kt_ref, ot_ref, dot_ref, lse_ref, dqt_ref, dk_ref, dv_ref, delta):
        dqt_ref[...] = jnp.zeros_like(dqt_ref)
        dk_ref[...] = jnp.zeros_like(dk_ref)
        dv_ref[...] = jnp.zeros_like(dv_ref)
        heads = [(hh, slice(hh * AUG, (hh + 1) * AUG), slice(hh * FOX_DIM, (hh + 1) * FOX_DIM),
                  slice(hh * KT_ROWS, (hh + 1) * KT_ROWS)) for hh in range(2)]

        def delta_loop(qb, _):
            q0 = pl.multiple_of(qb * bq, bq)
            for hh, _, cols, _ in heads:
                prod = ot_ref[cols, pl.ds(q0, bq)].astype(F32) * dot_ref[cols, pl.ds(q0, bq)].astype(F32)
                delta[hh, :, pl.ds(q0, bq)] = jnp.sum(prod, axis=0, keepdims=True)
            return 0

        lax.fori_loop(0, nq, delta_loop, 0)

        def k_loop(kb, _):
            k0 = pl.multiple_of(kb * bk, bk)

            def q_step(qb, masked):
                q0 = pl.multiple_of(qb * bq, bq)
                for hh, lanes, cols, trows in heads:
                    k_blk = k_ref[pl.ds(k0, bk), lanes]
                    q_blk = q_ref[pl.ds(q0, bq), lanes]
                    do_blk = do_ref[pl.ds(q0, bq), cols]
                    st = _fox_tile(k_blk, q_blk, k0, q0, masked)
                    pt = jnp.exp(st - lse_ref[hh, :, pl.ds(q0, bq)])
                    dpt = lax.dot_general(v_ref[pl.ds(k0, bk), cols], do_blk, (((1,), (1,)), ((), ())),
                                          preferred_element_type=F32)
                    dst = (pt * (dpt - delta[hh, :, pl.ds(q0, bq)])).astype(q_blk.dtype)
                    dv_ref[pl.ds(k0, bk), cols] += jnp.dot(pt.astype(do_blk.dtype), do_blk, preferred_element_type=F32)
                    dk_ref[pl.ds(k0, bk), lanes] += jnp.dot(dst, q_blk, preferred_element_type=F32)
                    dqt_ref[trows, pl.ds(q0, bq)] += jnp.dot(kt_ref[trows, pl.ds(k0, bk)], dst, preferred_element_type=F32)

            q_step(kb, True)

            def rest(qb, _):
                q_step(qb, False)
                return 0

            lax.fori_loop(kb + 1, nq, rest, 0)
            return 0

        lax.fori_loop(0, nk, k_loop, 0)

    aug = pl.BlockSpec((lay.lp, 2 * AUG), lambda b, p: (b, p))
    rows = pl.BlockSpec((lay.lp, 2 * FOX_DIM), lambda b, p: (b, p))
    tr = pl.BlockSpec((2 * FOX_DIM, lay.lp), lambda b, p: (p, b))
    tr_k = pl.BlockSpec((2 * KT_ROWS, lay.lp), lambda b, p: (p, b))
    return pl.pallas_call(
        body, name="fox_bwd", grid=(lay.batch, pairs),
        in_specs=[aug, aug, rows, rows, tr_k, tr, tr, pl.BlockSpec((2, 1, lay.lp), lambda b, p: (b * pairs + p, 0, 0))],
        out_specs=[tr_k, aug, rows],
        out_shape=[jax.ShapeDtypeStruct((FOX_HEADS * KT_ROWS, lay.n), F32), jax.ShapeDtypeStruct((lay.n, FOX_HEADS * AUG), F32),
                   jax.ShapeDtypeStruct((lay.n, FOX_W), F32)],
        scratch_shapes=[pltpu.VMEM((2, 1, lay.lp), F32)],
        compiler_params=_params(2),
    )(q_aug, k_aug, v, do, k_t, o_t, do_t, lse)


def _hgrn_fwd(proj, kk, gl, lay):
    t = lay.tile
    nt = lay.lp // t
    nsc = t // SUB

    def body(q_ref, k_ref, g_ref, v_ref, o_ref, st_ref, state, sub_rows):
        @pl.when(pl.program_id(1) == 0)
        def _():
            state[...] = jnp.zeros_like(state)

        rowi = lax.broadcasted_iota(jnp.int32, (SUB, 1), 0)

        def sub(sc, _):
            r0 = pl.multiple_of(sc * SUB, SUB)
            sub_rows[0] = k_ref[pl.ds(r0, SUB), :]
            sub_rows[1] = g_ref[pl.ds(r0, SUB), :]
            sub_rows[2] = v_ref[pl.ds(r0, SUB), :]
            for h in range(HG_HEADS):
                lanes = slice(h * HG_DIM, (h + 1) * HG_DIM)
                q16 = q_ref[pl.ds(r0, SUB), lanes]
                k16 = sub_rows[0, :, lanes]
                g16 = sub_rows[1, :, lanes]
                v16 = sub_rows[2, :, lanes]
                g_end = sub_rows[1, SUB - 1:SUB, lanes]
                s_prev = state[h]
                st_ref[sc, h] = s_prev
                o = lax.dot_general((q16 * jnp.exp(g16)).astype(MXU_DTYPE), s_prev.astype(MXU_DTYPE),
                                    (((1,), (1,)), ((), ())), preferred_element_type=F32)
                for s in range(SUB):
                    ks = sub_rows[0, s:s + 1, lanes]
                    gs = sub_rows[1, s:s + 1, lanes]
                    vs = sub_rows[2, s:s + 1, lanes]
                    w = q16 * jnp.exp(jnp.minimum(g16 - gs, 0.0)) * ks
                    a = jnp.where(rowi >= s, jnp.sum(w, axis=1, keepdims=True), 0.0)
                    o = o + a * vs
                o_ref[pl.ds(r0, SUB), lanes] = o
                kt = k16 * jnp.exp(g_end - g16)
                upd = lax.dot_general(v16.astype(MXU_DTYPE), kt.astype(MXU_DTYPE), (((0,), (0,)), ((), ())),
                                      preferred_element_type=F32)
                state[h] = jnp.exp(g_end) * s_prev + upd
            return 0

        lax.fori_loop(0, nsc, sub, 0)

    rows = lambda col: pl.BlockSpec((t, HG_W), functools.partial(lambda b, i, col: (b * nt + i, col), col=col))
    return pl.pallas_call(
        body, name="hgrn_fwd", grid=(lay.batch, nt),
        in_specs=[rows(C_HQ), rows(0), rows(0), rows(C_HI)],
        out_specs=[rows(0), pl.BlockSpec((nsc, HG_HEADS, HG_DIM, HG_DIM), lambda b, i: (b * nt + i, 0, 0, 0))],
        out_shape=[jax.ShapeDtypeStruct((lay.n, HG_W), F32),
                   jax.ShapeDtypeStruct((lay.n // SUB, HG_HEADS, HG_DIM, HG_DIM), F32)],
        scratch_shapes=[pltpu.VMEM((HG_HEADS, HG_DIM, HG_DIM), F32), pltpu.VMEM((3, SUB, HG_W), F32)],
        compiler_params=_params(2),
    )(proj, kk, gl, proj)


def _hgrn_bwd(proj, kk, gl, do, states, lay):
    t = lay.tile
    nt = lay.lp // t
    nsc = t // SUB

    def body(q_ref, k_ref, g_ref, v_ref, do_ref, st_ref, dq_ref, dk_ref, dv_ref, dg_ref, dstate, sub_rows):
        @pl.when(pl.program_id(1) == 0)
        def _():
            dstate[...] = jnp.zeros_like(dstate)

        rowi = lax.broadcasted_iota(jnp.int32, (SUB, 1), 0)

        def sub(it, _):
            sc = nsc - 1 - it
            r0 = pl.multiple_of(sc * SUB, SUB)
            sub_rows[0] = k_ref[pl.ds(r0, SUB), :]
            sub_rows[1] = g_ref[pl.ds(r0, SUB), :]
            sub_rows[2] = v_ref[pl.ds(r0, SUB), :]
            for h in range(HG_HEADS):
                lanes = slice(h * HG_DIM, (h + 1) * HG_DIM)
                q16 = q_ref[pl.ds(r0, SUB), lanes]
                k16 = sub_rows[0, :, lanes]
                g16 = sub_rows[1, :, lanes]
                v16 = sub_rows[2, :, lanes]
                do16 = do_ref[pl.ds(r0, SUB), lanes]
                g_end = sub_rows[1, SUB - 1:SUB, lanes]
                s_prev = st_ref[sc, h]
                ds_end = dstate[h]
                eg = jnp.exp(g16)
                ekt = jnp.exp(g_end - g16)
                e_end = jnp.exp(g_end)
                qt = q16 * eg
                kt = k16 * ekt
                ds_mx = ds_end.astype(MXU_DTYPE)
                dv = lax.dot_general(kt.astype(MXU_DTYPE), ds_mx, (((1,), (1,)), ((), ())), preferred_element_type=F32)
                dkt = jnp.dot(v16.astype(MXU_DTYPE), ds_mx, preferred_element_type=F32)
                dk = dkt * ekt
                ktdkt = kt * dkt
                dg_end = jnp.sum(ktdkt, axis=0, keepdims=True) + jnp.sum(s_prev * ds_end, axis=0, keepdims=True) * e_end
                dg = jnp.where(rowi == SUB - 1, dg_end, 0.0) - ktdkt
                dqt = jnp.dot(do16.astype(MXU_DTYPE), s_prev.astype(MXU_DTYPE), preferred_element_type=F32)
                dq = dqt * eg
                dg = dg + qt * dqt
                dstate[h] = e_end * ds_end + lax.dot_general(do16.astype(MXU_DTYPE), qt.astype(MXU_DTYPE),
                                                             (((0,), (0,)), ((), ())), preferred_element_type=F32)
                for s in range(SUB):
                    ks = sub_rows[0, s:s + 1, lanes]
                    gs = sub_rows[1, s:s + 1, lanes]
                    vs = sub_rows[2, s:s + 1, lanes]
                    live = rowi >= s
                    e = jnp.where(live, jnp.exp(jnp.minimum(g16 - gs, 0.0)), 0.0)
                    qe = q16 * e
                    a = jnp.sum(qe * ks, axis=1, keepdims=True)
                    da = jnp.where(live, jnp.sum(do16 * vs, axis=1, keepdims=True), 0.0)
                    dv_row = jnp.sum(a * do16, axis=0, keepdims=True)
                    t1 = da * qe
                    dk_row = jnp.sum(t1, axis=0, keepdims=True)
                    dq = dq + da * (e * ks)
                    is_s = rowi == s
                    dv = dv + jnp.where(is_s, dv_row, 0.0)
                    dk = dk + jnp.where(is_s, dk_row, 0.0)
                    dg = dg + t1 * ks - jnp.where(is_s, ks * dk_row, 0.0)
                dq_ref[pl.ds(r0, SUB), lanes] = dq
                dk_ref[pl.ds(r0, SUB), lanes] = dk
                dv_ref[pl.ds(r0, SUB), lanes] = dv
                dg_ref[pl.ds(r0, SUB), lanes] = dg
            return 0

        lax.fori_loop(0, nsc, sub, 0)

    def rows(col):
        return pl.BlockSpec((t, HG_W), functools.partial(lambda b, i, col: (b * nt + nt - 1 - i, col), col=col))

    out = jax.ShapeDtypeStruct((lay.n, HG_W), F32)
    return pl.pallas_call(
        body, name="hgrn_bwd", grid=(lay.batch, nt),
        in_specs=[rows(C_HQ), rows(0), rows(0), rows(C_HI), rows(0),
                  pl.BlockSpec((nsc, HG_HEADS, HG_DIM, HG_DIM), lambda b, i: (b * nt + nt - 1 - i, 0, 0, 0))],
        out_specs=[rows(0)] * 4, out_shape=[out] * 4,
        scratch_shapes=[pltpu.VMEM((HG_HEADS, HG_DIM, HG_DIM), F32), pltpu.VMEM((3, SUB, HG_W), F32)],
        compiler_params=_params(2),
    )(proj, kk, gl, proj, do, states)


CONV_COLS = 256


def _shift_down(x, halo, tile, by):
    out = pltpu.roll(x, by, 0)
    rowi = lax.broadcasted_iota(jnp.int32, (tile, 1), 0)
    for r in range(by):
        out = jnp.where(rowi == r, halo[8 - by + r:8 - by + r + 1, :], out)
    return out


def _shift_up(x, halo, tile, by):
    out = pltpu.roll(x, tile - by, 0)
    rowi = lax.broadcasted_iota(jnp.int32, (tile, 1), 0)
    for r in range(by):
        out = jnp.where(rowi == tile - by + r, halo[r:r + 1, :], out)
    return out


def _conv_specs(tile):
    ncb = D_FF // CONV_COLS
    per8 = tile // 8

    def tile_spec(off):
        return pl.BlockSpec((tile, CONV_COLS), functools.partial(lambda i, j, off: (i, j + off), off=off))

    def prev_spec(off):
        return pl.BlockSpec((8, CONV_COLS), functools.partial(lambda i, j, off: (jnp.maximum(i * per8 - 1, 0), j + off), off=off))

    def w_spec(off):
        return pl.BlockSpec((3, CONV_COLS), functools.partial(lambda i, j, off: (0, j + off), off=off))

    def b_spec(off):
        return pl.BlockSpec((1, CONV_COLS), functools.partial(lambda i, j, off: (0, j + off), off=off))

    return ncb, tile_spec, prev_spec, w_spec, b_spec


def _conv3(x, halo, w, b, tile):
    return w[0:1, :] * _shift_down(x, halo, tile, 2) + w[1:2, :] * _shift_down(x, halo, tile, 1) + w[2:3, :] * x + b


def _conv_act_fwd(u, conv_w, conv_b, lay):
    tile = lay.tile
    ncb, tile_spec, prev_spec, w_spec, b_spec = _conv_specs(tile)

    def body(ug, uv, pg, pv, wg, wv, bg, bv, o_ref):
        cg = _conv3(ug[...], pg, wg, bg[...], tile)
        cv = _conv3(uv[...], pv, wv, bv[...], tile)
        o_ref[...] = (_silu(cg) * cv).astype(o_ref.dtype)

    return pl.pallas_call(
        body, name="conv_act_fwd", grid=(lay.n // tile, ncb),
        in_specs=[tile_spec(0), tile_spec(ncb), prev_spec(0), prev_spec(ncb), w_spec(0), w_spec(ncb), b_spec(0), b_spec(ncb)],
        out_specs=pl.BlockSpec((tile, CONV_COLS), lambda i, j: (i, j)),
        out_shape=jax.ShapeDtypeStruct((lay.n, D_FF), MXU_DTYPE),
        compiler_params=_params(2),
    )(u, u, u, u, conv_w, conv_w, conv_b, conv_b)


def _conv_act_bwd(u, dact, conv_w, conv_b, lay):
    tile = lay.tile
    ncb, tile_spec, prev_spec, w_spec, b_spec = _conv_specs(tile)

    def body(ug, uv, pg, pv, wg, wv, bg, bv, da_ref, dg_ref, dv_ref, gwg, gwv, gbg, gbv):
        @pl.when(pl.program_id(1) == 0)
        def _():
            for r in (gwg, gwv, gbg, gbv):
                r[...] = jnp.zeros_like(r)

        xg, xv = ug[...], uv[...]
        cg = _conv3(xg, pg, wg, bg[...], tile)
        cv = _conv3(xv, pv, wv, bv[...], tile)
        da = da_ref[...].astype(F32)
        sg = _sigmoid(cg)
        dcv = da * (cg * sg)
        dcg = da * cv * (sg * (1.0 + cg * (1.0 - sg)))
        dg_ref[...] = dcg
        dv_ref[...] = dcv
        for x, halo, dc, gw, gb in ((xg, pg, dcg, gwg, gbg), (xv, pv, dcv, gwv, gbv)):
            gw[0, 0:1, :] += jnp.sum(dc * _shift_down(x, halo, tile, 2), axis=0, keepdims=True)
            gw[0, 1:2, :] += jnp.sum(dc * _shift_down(x, halo, tile, 1), axis=0, keepdims=True)
            gw[0, 2:3, :] += jnp.sum(dc * x, axis=0, keepdims=True)
            gb[0] += jnp.sum(dc, axis=0, keepdims=True)

    swap = lambda spec: pl.BlockSpec(spec.block_shape, functools.partial(lambda j, i, f: f(i, j), f=spec.index_map))
    col = lambda j, i: (i, j)
    red_w = pl.BlockSpec((1, 3, CONV_COLS), lambda j, i: (j, 0, 0))
    red_b = pl.BlockSpec((1, 1, CONV_COLS), lambda j, i: (j, 0, 0))
    outs = pl.pallas_call(
        body, name="conv_act_bwd", grid=(ncb, lay.n // tile),
        in_specs=[swap(s) for s in (tile_spec(0), tile_spec(ncb), prev_spec(0), prev_spec(ncb), w_spec(0), w_spec(ncb),
                                    b_spec(0), b_spec(ncb))] + [pl.BlockSpec((tile, CONV_COLS), col)],
        out_specs=[pl.BlockSpec((tile, CONV_COLS), col), pl.BlockSpec((tile, CONV_COLS), col), red_w, red_w, red_b, red_b],
        out_shape=[jax.ShapeDtypeStruct((lay.n, D_FF), F32), jax.ShapeDtypeStruct((lay.n, D_FF), F32),
                   jax.ShapeDtypeStruct((ncb, 3, CONV_COLS), F32), jax.ShapeDtypeStruct((ncb, 3, CONV_COLS), F32),
                   jax.ShapeDtypeStruct((ncb, 1, CONV_COLS), F32), jax.ShapeDtypeStruct((ncb, 1, CONV_COLS), F32)],
        compiler_params=_params(2),
    )(u, u, u, u, conv_w, conv_w, conv_b, conv_b, dact)
    dcg, dcv, gwg, gwv, gbg, gbv = outs
    unblock = lambda g: jnp.transpose(g, (1, 0, 2)).reshape(g.shape[1], D_FF)
    g_w = jnp.concatenate([unblock(gwg), unblock(gwv)], axis=1)
    g_b = jnp.concatenate([unblock(gbg), unblock(gbv)], axis=1)
    return dcg, dcv, g_w, g_b


def _conv_input_bwd(dcg, dcv, conv_w, lay):
    tile = lay.tile
    ncb = D_FF // CONV_COLS
    nblk8 = lay.n // 8
    per8 = tile // 8

    def body(dg, dv, ng, nv, w, o):
        valid = lay.valid(pl.program_id(0), tile)

        def emit(d, halo):
            x = d[...]
            du = w[2:3, :] * x + w[1:2, :] * _shift_up(x, halo, tile, 1) + w[0:1, :] * _shift_up(x, halo, tile, 2)
            o[...] = jnp.where(valid, du, 0.0).astype(o.dtype)

        @pl.when(pl.program_id(1) < ncb)
        def _():
            emit(dg, ng)

        @pl.when(pl.program_id(1) >= ncb)
        def _():
            emit(dv, nv)

    gate_col = lambda j: jnp.minimum(j, ncb - 1)
    val_col = lambda j: jnp.maximum(j - ncb, 0)
    nxt = lambda i: jnp.minimum((i + 1) * per8, nblk8 - 1)
    return pl.pallas_call(
        body, name="conv_input_bwd", grid=(lay.n // tile, 2 * ncb),
        in_specs=[pl.BlockSpec((tile, CONV_COLS), lambda i, j: (i, gate_col(j))),
                  pl.BlockSpec((tile, CONV_COLS), lambda i, j: (i, val_col(j))),
                  pl.BlockSpec((8, CONV_COLS), lambda i, j: (nxt(i), gate_col(j))),
                  pl.BlockSpec((8, CONV_COLS), lambda i, j: (nxt(i), val_col(j))),
                  pl.BlockSpec((3, CONV_COLS), lambda i, j: (0, j))],
        out_specs=pl.BlockSpec((tile, CONV_COLS), lambda i, j: (i, j)),
        out_shape=jax.ShapeDtypeStruct((lay.n, FF2), MXU_DTYPE),
        compiler_params=_params(2),
    )(dcg, dcv, dcg, dcv, conv_w)


def _loss_head(h1, mlp, target, lay):
    t = 64
    per = lay.lp // t
    nreal = lay.seq // t
    first = ROW0 // t

    def body(h_ref, m_ref, t_ref, loss_ref, dy_ref, dyb_ref):
        b, j = pl.program_id(0), pl.program_id(1)

        @pl.when((b == 0) & (j == 0))
        def _():
            loss_ref[...] = jnp.zeros_like(loss_ref)

        real = (j >= first) & (j < first + nreal)
        err = jnp.where(real, h_ref[...] + m_ref[...] - t_ref[...], 0.0)
        dy = err * (1.0 / D_MODEL)
        dy_ref[...] = dy
        dyb_ref[...] = dy.astype(dyb_ref.dtype)
        loss_ref[...] += 0.5 * jnp.sum(err * dy)

    rows = pl.BlockSpec((t, D_MODEL), lambda b, j: (b * per + j, 0))
    tgt = pl.BlockSpec((t, D_MODEL), lambda b, j: (b * nreal + jnp.clip(j - first, 0, nreal - 1), 0))
    return pl.pallas_call(
        body, name="loss_head", grid=(lay.batch, per),
        in_specs=[rows, rows, tgt],
        out_specs=[pl.BlockSpec((8, LANES), lambda b, j: (0, 0)), rows, rows],
        out_shape=[jax.ShapeDtypeStruct((8, LANES), F32), jax.ShapeDtypeStruct((lay.n, D_MODEL), F32),
                   jax.ShapeDtypeStruct((lay.n, D_MODEL), MXU_DTYPE)],
        compiler_params=_params(2),
    )(h1, mlp, target)


def _fox_prep(fq, fk, ff, gq, gk, bf, gmat, gmat_t, valid):
    q = _group_rms(fq, gq, gmat, gmat_t, FOX_DIM)
    k = _group_rms(fk, gk, gmat, gmat_t, FOX_DIM)
    logf = jnp.where(valid, _log_sigmoid(ff + bf), 0.0)
    return q, k, logf


def _hg_prep(hf, l0, l1):
    mx = jnp.maximum(l0, l1)
    e0, e1 = jnp.exp(l0 - mx), jnp.exp(l1 - mx)
    lb = e0 / (e0 + e1)
    lf = jnp.log(lb + (1.0 - lb) * _sigmoid(hf))
    kk = (1.0 - lb) * _sigmoid(-hf)
    return lf, kk


def _hg_post(o, hg, gain, gmat, gmat_t):
    return _group_rms(o, jnp.tile(gain, (1, HG_HEADS)), gmat, gmat_t, HG_DIM) * _silu(hg)


def _gate(ga, gb, ya, yb):
    return _sigmoid(ga) * ya + _sigmoid(gb) * yb


def _by_chip(g):
    return jnp.transpose(g.reshape(g.shape[0], N_CHIPS, g.shape[1] // N_CHIPS), (1, 0, 2))


def _from_chips(a):
    return jnp.transpose(a, (1, 0, 2)).reshape(a.shape[1], N_CHIPS * a.shape[2])


def _local_step(x, target, w, lay):
    n, tile = lay.n, lay.tile
    rw = functools.partial(_rowwise, n_rows=n, tile=tile)
    mx = lambda a: a.astype(MXU_DTYPE)

    w_in = w["w_in"]
    fq, fk, fv, ffw, hq, hf, hi, hg, ga, gb = jnp.split(w_in, list(np.cumsum([512, 512, 512, 8, 512, 512, 512, 512, 1024])), axis=1)
    w_main = mx(jnp.concatenate([ga, gb, fq, fk, fv, hq, hf, hi, hg], axis=1))
    w_ff = mx(jnp.pad(ffw, ((0, 0), (0, LANES - FOX_HEADS))))
    w_a, w_b, w_out, w_up, w_down = mx(w["w_branch_a"]), mx(w["w_branch_b"]), mx(w["w_out"]), mx(w["w_up"]), mx(w["w_down"])
    conv_w, conv_b = w["conv_w"].astype(F32), w["conv_b"].astype(F32)
    g1, g2 = w["norm1_gain"], w["norm2_gain"]
    gq, gk = jnp.tile(w["q_norm_gain"], (1, FOX_HEADS)), jnp.tile(w["k_norm_gain"], (1, FOX_HEADS))
    bf = jnp.pad(w["fox_b_f"], ((0, 0), (0, LANES - FOX_HEADS)))
    lb_logits, hg_gain = w["hg_lb_logits"], w["hg_out_gain"]
    gm64, gm64_t = _group_matrix(FOX_W, FOX_DIM)
    gm128, gm128_t = _group_matrix(HG_W, HG_DIM)

    meta = jnp.broadcast_to(w["meta_tokens"].astype(F32)[None], (lay.batch, N_META, D_MODEL))
    h0 = jnp.concatenate([jnp.zeros((lay.batch, LEAD, D_MODEL), F32), meta, x,
                          jnp.zeros((lay.batch, lay.lp - LEAD - lay.l_real, D_MODEL), F32)], axis=1).reshape(n, D_MODEL)

    (xn,) = rw(lambda i, h, g: _rms(h, g), [h0], [g1], [(D_MODEL, MXU_DTYPE)], [], name="norm1")
    proj = _matmul(xn, w_main, name="proj_main")
    pff = _matmul(xn, w_ff, name="proj_ff")

    def fox_prep_fn(i, a, b_, v_, f_, gq_, gk_, bf_, m_, mt_):
        q_, k_, logf = _fox_prep(a, b_, f_, gq_, gk_, bf_, m_, mt_, lay.valid(i, tile))
        return q_, k_, v_, logf

    q, k, v, logf = rw(fox_prep_fn, [(proj, 512, C_FQ), (proj, 512, C_FK), (proj, 512, C_FV), pff], [gq, gk, bf, gm64, gm64_t],
                       [(512, MXU_DTYPE), (512, MXU_DTYPE), (512, MXU_DTYPE), (LANES, F32)], [], name="fox_prep")
    cum = _cumsum_rows(logf, lay, reverse=False, name="fox_cum")
    e1, e2, aug_ones = _aug_matrices()
    q_aug, k_aug = rw(lambda i, q_, k_, c_, e1_, e2_, on_: _fox_augment(q_, k_, c_, lay.valid(i, tile), e1_, e2_, on_),
                      [q, k, cum], [e1, e2, aug_ones], [(FOX_HEADS * AUG, MXU_DTYPE)] * 2, [], name="fox_aug")
    o_t, lse = _fox_fwd_t(q_aug, k_aug, v.T, lay)

    def hg_prep_fn(i, hf_, l0, l1):
        lf, kk_ = _hg_prep(hf_, l0, l1)
        return kk_, _group_cumsum(lf, tile, reverse=False)

    lb0, lb1 = lb_logits[0:1], lb_logits[1:2]
    kk, gl = rw(hg_prep_fn, [(proj, 512, C_HF)], [lb0, lb1], [(512, F32), (512, F32)], [], name="hg_prep")
    o_hg, states = _hgrn_fwd(proj, kk, gl, lay)
    (oh,) = rw(lambda i, o, g_, gain, m_, mt_: _hg_post(o, g_, gain, m_, mt_), [o_hg, (proj, 512, C_HG)],
               [hg_gain, gm128, gm128_t], [(512, MXU_DTYPE)], [], name="hg_post")
    ya = _matmul(oh, w_a, name="branch_a")
    yb = _matmul(o_t, w_b, trans_a=True, name="branch_b")
    pga, pgb = (proj, 1024, C_GA), (proj, 1024, C_GB)
    (merged,) = rw(lambda i, a, b_, c_, d_: _gate(a, b_, c_, d_), [pga, pgb, ya, yb], [], [(D_MODEL, MXU_DTYPE)], [], name="gate")
    mo = _matmul(merged, w_out, name="out_proj")
    h1, hn = rw(lambda i, h, m_, g: (h + m_, _rms(h + m_, g)), [h0, mo], [g2], [(D_MODEL, F32), (D_MODEL, MXU_DTYPE)], [],
                name="norm2")
    u = _matmul(hn, w_up, name="up_proj")
    act = _conv_act_fwd(u, conv_w, conv_b, lay)
    mlp = _matmul(act, w_down, name="down_proj")
    loss_blk, dy, dyb = _loss_head(h1, mlp, target.reshape(lay.batch * lay.seq, D_MODEL), lay)
    loss = loss_blk[0, 0]

    grads = {}
    dact = _matmul(dyb, w_down, trans_b=True, out_dtype=MXU_DTYPE, name="down_bwd_x")
    grads["w_down"] = _matmul(act, dyb, trans_a=True, name="down_bwd_w").reshape(N_CHIPS, D_FF // N_CHIPS, D_MODEL)
    dcg, dcv, grads["conv_w"], grads["conv_b"] = _conv_act_bwd(u, dact, conv_w, conv_b, lay)
    du = _conv_input_bwd(dcg, dcv, conv_w, lay)
    dhn = _matmul(du, w_up, trans_b=True, name="up_bwd_x")
    grads["w_up"] = _matmul(hn, du, trans_a=True, by_chip=True, name="up_bwd_w")

    def norm2_bwd(i, h, d_, dy_, g):
        _, vjp = jax.vjp(_rms, h, g)
        dh, dg = vjp(d_)
        return dh + dy_, dh + dy_, dg

    dh1, dh1b, grads["norm2_gain"] = rw(norm2_bwd, [h1, dhn, dy], [g2], [(D_MODEL, F32), (D_MODEL, MXU_DTYPE)], [(1, D_MODEL)],
                                        name="norm2_bwd")
    dmerged = _matmul(dh1b, w_out, trans_b=True, name="out_bwd_x")
    grads["w_out"] = _matmul(merged, dh1b, trans_a=True, name="out_bwd_w").reshape(N_CHIPS, D_MODEL // N_CHIPS, D_MODEL)

    def gate_bwd(i, a, b_, c_, d_, dm):
        _, vjp = jax.vjp(_gate, a, b_, c_, d_)
        return vjp(dm)

    dga, dgb, dya, dyb_ = rw(gate_bwd, [pga, pgb, ya, yb, dmerged], [], [(D_MODEL, MXU_DTYPE)] * 4, [], name="gate_bwd")
    doh = _matmul(dya, w_a, trans_b=True, name="branch_a_bwd_x")
    grads["w_branch_a"] = _matmul(oh, dya, trans_a=True, by_chip=True, name="branch_a_bwd_w")
    dofox = _matmul(dyb_, w_b, trans_b=True, out_dtype=MXU_DTYPE, name="branch_b_bwd_x")
    grads["w_branch_b"] = _matmul(o_t, dyb_, by_chip=True, name="branch_b_bwd_w")

    def hg_post_bwd(i, o, g_, d_, gain, m_, mt_):
        _, vjp = jax.vjp(lambda o__, g__, gain__: _hg_post(o__, g__, gain__, m_, mt_), o, g_, gain)
        return vjp(d_)

    do_hg, dhg, grads["hg_out_gain"] = rw(hg_post_bwd, [o_hg, (proj, 512, C_HG), doh], [hg_gain, gm128, gm128_t],
                                          [(512, F32), (512, MXU_DTYPE)], [(1, HG_DIM)], name="hg_post_bwd")
    dhq, dkk, dhi, dgl = _hgrn_bwd(proj, kk, gl, do_hg, states, lay)

    def hg_prep_bwd(i, hf_, dkk_, dgl_, l0, l1):
        _, vjp = jax.vjp(_hg_prep, hf_, l0, l1)
        return vjp((_group_cumsum(dgl_, tile, reverse=True), dkk_))

    dhf, g_lb0, g_lb1 = rw(hg_prep_bwd, [(proj, 512, C_HF), dkk, dgl], [lb0, lb1], [(512, MXU_DTYPE)], [(1, HG_W), (1, HG_W)],
                           name="hg_prep_bwd")
    grads["hg_lb_logits"] = jnp.concatenate([g_lb0, g_lb1], axis=0)

    k_t = (k.astype(F32) * FOX_SCALE).astype(MXU_DTYPE).T.reshape(FOX_HEADS, FOX_DIM, n)
    k_t = jnp.concatenate([k_t, jnp.ones((FOX_HEADS, KT_ROWS - FOX_DIM, n), MXU_DTYPE)], axis=1).reshape(FOX_HEADS * KT_ROWS, n)
    dq_t, dk_aug, dv = _fox_bwd_t(q_aug, k_aug, v, dofox, k_t, o_t, dofox.T, lse, lay)
    dq_t = dq_t.reshape(FOX_HEADS, KT_ROWS, n)
    dq = dq_t[:, :FOX_DIM].reshape(FOX_W, n).T
    dk_aug = dk_aug.reshape(n, FOX_HEADS, AUG)
    dk = dk_aug[:, :, :FOX_DIM].reshape(n, FOX_W)
    dcum = jnp.pad(dq_t[:, FOX_DIM].T - dk_aug[:, :, FOX_DIM], ((0, 0), (0, LANES - FOX_HEADS)))
    dlogf = _cumsum_rows(dcum, lay, reverse=True, name="fox_cum_bwd")

    def fox_prep_bwd(i, a, b_, f_, dq_, dk_, dl_, gq_, gk_, bf_, m_, mt_):
        valid = lay.valid(i, tile)
        _, vjp = jax.vjp(lambda a_, b__, f__, gq__, gk__, bf__: _fox_prep(a_, b__, f__, gq__, gk__, bf__, m_, mt_, valid),
                         a, b_, f_, gq_, gk_, bf_)
        return vjp((dq_, dk_, dl_))

    dfq, dfk, dff, g_gq, g_gk, g_bf = rw(
        fox_prep_bwd, [(proj, 512, C_FQ), (proj, 512, C_FK), pff, dq, dk, dlogf], [gq, gk, bf, gm64, gm64_t],
        [(512, MXU_DTYPE), (512, MXU_DTYPE), (LANES, MXU_DTYPE)], [(1, FOX_W), (1, FOX_W), (1, LANES)], name="fox_prep_bwd")
    grads["q_norm_gain"] = g_gq.reshape(FOX_HEADS, FOX_DIM).sum(0, keepdims=True)
    grads["k_norm_gain"] = g_gk.reshape(FOX_HEADS, FOX_DIM).sum(0, keepdims=True)
    grads["fox_b_f"] = g_bf[:, :FOX_HEADS]

    dproj = jnp.concatenate([dga, dgb, dfq, dfk, mx(dv), mx(dhq), dhf, mx(dhi), dhg], axis=1)
    dxn = _matmul(dproj, w_main, trans_b=True, name="proj_bwd_x")
    dxn_ff = _matmul(dff, w_ff, trans_b=True, name="proj_ff_bwd_x")
    g_main = _matmul(xn, dproj, trans_a=True, name="proj_bwd_w")
    g_ff = _matmul(xn, dff, trans_a=True, name="proj_ff_bwd_w")[:, :FOX_HEADS]
    p = jnp.split(g_main, list(np.cumsum([1024, 1024] + [512] * 6)), axis=1)
    grads["w_in"] = _by_chip(jnp.concatenate([p[2], p[3], p[4], g_ff, p[5], p[6], p[7], p[8], p[0], p[1]], axis=1))

    per = lay.lp // tile

    def norm1_bwd(i, h, d1, d2, dh1_, g):
        _, vjp = jax.vjp(_rms, h, g)
        dh, dg = vjp(d1 + d2)
        dh = dh + dh1_
        dmeta = jnp.where(lax.rem(i, per) == 0, dh[LEAD:LEAD + N_META, :], 0.0)
        return dh, dg, dmeta

    dh0, grads["norm1_gain"], grads["meta_tokens"] = rw(norm1_bwd, [h0, dxn, dxn_ff, dh1], [g1], [(D_MODEL, F32)],
                                                       [(1, D_MODEL), (N_META, D_MODEL)], name="norm1_bwd")
    grad_x = dh0.reshape(lay.batch, lay.lp, D_MODEL)[:, ROW0:ROW0 + lay.seq]
    return loss, grad_x, grads


MESH = pl.DeviceIdType.MESH
HBM_SPEC = pl.BlockSpec(memory_space=pltpu.HBM)
WEIGHT_NAMES = ["meta_tokens", "norm1_gain", "w_in", "fox_b_f", "q_norm_gain", "k_norm_gain", "hg_lb_logits", "hg_out_gain",
                "w_branch_a", "w_branch_b", "w_out", "norm2_gain", "w_up", "conv_w", "conv_b", "w_down"]
BIG = ("w_in", "w_branch_a", "w_branch_b", "w_out", "w_up", "w_down")
BIG_COL_SHARDED = ("w_in", "w_branch_a", "w_branch_b", "w_up")
SMALL = tuple(n for n in WEIGHT_NAMES if n not in BIG)
SMALL_SHARDED = ("meta_tokens", "conv_w")
SMALL_ROWS = 144
GATHER_SMALL_ROWS = 80


def _position():
    return lax.axis_index("x"), lax.axis_index("y"), lax.axis_index("c")


def _other_chips(x, y):
    return [(1 - x, y), (x, 1 - y), (1 - x, 1 - y)]


def _scalar(v):
    return jnp.reshape(v, (1,)).astype(jnp.int32)


def _row_tile(rows, cols):
    width = -(-cols // LANES) * LANES * 4
    best = 8
    for d in range(8, rows + 1, 8):
        if rows % d == 0 and d * width <= (1 << 20):
            best = d
    return best


def _gather_shards(shards):
    na = len(shards)
    halves = [s.shape[0] // 2 for s in shards]

    def body(*refs):
        xs, outs, send_sems, recv_sems = refs[:na], refs[na:2 * na], refs[2 * na], refs[2 * na + 1]
        x, y, c = _position()
        sibling = (x, y, 1 - c)
        chips = _other_chips(x, y)

        def copy(a, k, block, to, src=None):
            dst = outs[a].at[4 * block[0] + 2 * block[1] + block[2]]
            return pltpu.make_async_remote_copy(src_ref=dst if src is None else src, dst_ref=dst, send_sem=send_sems.at[a, k],
                                                recv_sem=recv_sems.at[a, k], device_id=to, device_id_type=MESH)

        first = []
        for a in range(na):
            mine = xs[a].at[pl.ds(pl.multiple_of(c * halves[a], 8), halves[a]), :]
            first += [copy(a, j, (x, y, c), (*chip, c), src=mine) for j, chip in enumerate(chips)]
        for cp in first:
            cp.start()
        passed = []
        for j, chip in enumerate(chips):
            for a in range(na):
                copy(a, j, (*chip, c), (x, y, c)).wait_recv()
                cp = copy(a, 3 + j, (*chip, c), sibling)
                cp.start()
                passed.append(cp)
        for a in range(na):
            for j, chip in enumerate(chips):
                copy(a, 3 + j, (*chip, 1 - c), (x, y, c)).wait_recv()
        for cp in first + passed:
            cp.wait_send()

    return pl.pallas_call(
        body, name="gather_weights",
        out_shape=[jax.ShapeDtypeStruct((8, h, s.shape[1]), s.dtype) for h, s in zip(halves, shards)],
        in_specs=[HBM_SPEC] * na, out_specs=[HBM_SPEC] * na,
        scratch_shapes=[pltpu.SemaphoreType.DMA((na, 6)), pltpu.SemaphoreType.DMA((na, 6))],
    )(*shards)


def _sibling_exchange(gs):
    na = len(gs)
    halves = [g.shape[1] // 2 for g in gs]

    def body(*refs):
        srcs, gots, send_sems, recv_sems = refs[:na], refs[na:2 * na], refs[2 * na], refs[2 * na + 1]
        x, y, c = _position()
        copies = [pltpu.make_async_remote_copy(
            src_ref=srcs[a].at[:, pl.ds(pl.multiple_of((1 - c) * halves[a], 8), halves[a]), :], dst_ref=gots[a],
            send_sem=send_sems.at[a], recv_sem=recv_sems.at[a], device_id=(x, y, 1 - c), device_id_type=MESH) for a in range(na)]
        for cp in copies:
            cp.start()
        for cp in copies:
            cp.wait()

    return pl.pallas_call(
        body, name="reduce_sibling",
        out_shape=[jax.ShapeDtypeStruct((N_CHIPS, h, g.shape[2]), g.dtype) for h, g in zip(halves, gs)],
        in_specs=[HBM_SPEC] * na, out_specs=[HBM_SPEC] * na,
        scratch_shapes=[pltpu.SemaphoreType.DMA((na,)), pltpu.SemaphoreType.DMA((na,))],
    )(*gs)


def _chip_exchange(parts):
    na = len(parts)

    def body(*refs):
        srcs, gots, send_sems, recv_sems = refs[:na], refs[na:2 * na], refs[2 * na], refs[2 * na + 1]
        x, y, c = _position()
        mine = 2 * x + y
        chips = _other_chips(x, y)

        def copy(a, j):
            cx, cy = chips[j]
            return pltpu.make_async_remote_copy(src_ref=srcs[a].at[2 * cx + cy], dst_ref=gots[a].at[mine],
                                                send_sem=send_sems.at[a, j], recv_sem=recv_sems.at[a, j],
                                                device_id=(cx, cy, c), device_id_type=MESH)

        def arrival(a, j):
            cx, cy = chips[j]
            return pltpu.make_async_remote_copy(src_ref=srcs[a].at[mine], dst_ref=gots[a].at[2 * cx + cy],
                                                send_sem=send_sems.at[a, j], recv_sem=recv_sems.at[a, j],
                                                device_id=(cx, cy, c), device_id_type=MESH)

        sends = [copy(a, j) for a in range(na) for j in range(3)]
        for cp in sends:
            cp.start()
        for a in range(na):
            for j in range(3):
                arrival(a, j).wait_recv()
        for cp in sends:
            cp.wait_send()

    return pl.pallas_call(
        body, name="reduce_chips", out_shape=[jax.ShapeDtypeStruct(p.shape, p.dtype) for p in parts],
        in_specs=[HBM_SPEC] * na, out_specs=[HBM_SPEC] * na,
        scratch_shapes=[pltpu.SemaphoreType.DMA((na, 3)), pltpu.SemaphoreType.DMA((na, 3))],
    )(*parts)


def _sibling_send(halves):
    na = len(halves)

    def body(*refs):
        srcs, gots, send_sems, recv_sems = refs[:na], refs[na:2 * na], refs[2 * na], refs[2 * na + 1]
        x, y, c = _position()
        copies = [pltpu.make_async_remote_copy(src_ref=srcs[a], dst_ref=gots[a], send_sem=send_sems.at[a], recv_sem=recv_sems.at[a],
                                               device_id=(x, y, 1 - c), device_id_type=MESH) for a in range(na)]
        for cp in copies:
            cp.start()
        for cp in copies:
            cp.wait()

    return pl.pallas_call(
        body, name="reduce_gather", out_shape=[jax.ShapeDtypeStruct(h.shape, h.dtype) for h in halves],
        in_specs=[HBM_SPEC] * na, out_specs=[HBM_SPEC] * na,
        scratch_shapes=[pltpu.SemaphoreType.DMA((na,)), pltpu.SemaphoreType.DMA((na,))],
    )(*halves)


def _add_own_half(g, got, c, name):
    _, r, cols = g.shape
    r2 = r // 2
    tr = _row_tile(r2, cols)
    nrt = r2 // tr

    def body(c_ref, g_ref, got_ref, o_ref):
        o_ref[...] = g_ref[...] + got_ref[...]

    blk = (1, tr, cols)
    return pl.pallas_call(
        body, name=name,
        grid_spec=pltpu.PrefetchScalarGridSpec(
            num_scalar_prefetch=1, grid=(N_CHIPS, nrt),
            in_specs=[pl.BlockSpec(blk, lambda j, i, c_: (j, c_[0] * nrt + i, 0)), pl.BlockSpec(blk, lambda j, i, c_: (j, i, 0))],
            out_specs=pl.BlockSpec(blk, lambda j, i, c_: (j, i, 0))),
        out_shape=jax.ShapeDtypeStruct((N_CHIPS, r2, cols), F32), compiler_params=_params(2),
    )(c, g, got)


def _add_chips(part, got, mine, name):
    _, r2, cols = part.shape
    tr = _row_tile(r2, cols)

    def body(m_ref, p_ref, g0, g1, g2, g3, o_ref):
        t = [jnp.where(m_ref[0] == k, p_ref[0], g[0]) for k, g in enumerate((g0, g1, g2, g3))]
        o_ref[...] = ((t[0] + t[1]) + t[2]) + t[3]

    blk = (1, tr, cols)
    others = [pl.BlockSpec(blk, functools.partial(lambda i, m, k: (jnp.where(m[0] == k, (k + 1) % N_CHIPS, k), i, 0), k=k))
              for k in range(N_CHIPS)]
    return pl.pallas_call(
        body, name=name,
        grid_spec=pltpu.PrefetchScalarGridSpec(
            num_scalar_prefetch=1, grid=(r2 // tr,),
            in_specs=[pl.BlockSpec(blk, lambda i, m: (m[0], i, 0))] + others,
            out_specs=pl.BlockSpec((tr, cols), lambda i, m: (i, 0))),
        out_shape=jax.ShapeDtypeStruct((r2, cols), F32), compiler_params=_params(1),
    )(mine, part, got, got, got, got)


def _adamw(w, own, other, m, v, c, name):
    r, cols = w.shape
    r2 = r // 2
    tr = _row_tile(r2, cols)
    nrt = r2 // tr
    c1 = 1.0 / (1.0 - ADAM_B1 ** ADAM_STEP)
    c2 = 1.0 / (1.0 - ADAM_B2 ** ADAM_STEP)

    def body(c_ref, w_ref, own_ref, other_ref, m_ref, v_ref, g_out, d_out, m_out, v_out):
        g_ = jnp.where(pl.program_id(0) == c_ref[0], own_ref[...], other_ref[...])
        m_new = ADAM_B1 * m_ref[...] + (1.0 - ADAM_B1) * g_
        v_new = ADAM_B2 * v_ref[...] + (1.0 - ADAM_B2) * (g_ * g_)
        g_out[...] = g_
        d_out[...] = -ADAM_LR * ((m_new * c1) / (jnp.sqrt(v_new * c2) + ADAM_EPS) + ADAM_WD * w_ref[...])
        m_out[...] = m_new
        v_out[...] = v_new

    full = pl.BlockSpec((tr, cols), lambda h, i, c_: (h * nrt + i, 0))
    half = pl.BlockSpec((tr, cols), lambda h, i, c_: (i, 0))
    out = jax.ShapeDtypeStruct((r, cols), F32)
    return pl.pallas_call(
        body, name=name,
        grid_spec=pltpu.PrefetchScalarGridSpec(num_scalar_prefetch=1, grid=(2, nrt), in_specs=[full, half, half, full, full],
                                               out_specs=[full] * 4),
        out_shape=[out] * 4, compiler_params=_params(2),
    )(c, w, own, other, m, v)


def _to_rows(flat, rows):
    return jnp.pad(flat, (0, rows * LANES - flat.shape[0])).reshape(rows, LANES)


def _pack_small(tree):
    return _to_rows(jnp.concatenate([tree[n].astype(F32).reshape(-1) for n in SMALL]), SMALL_ROWS)


def _unpack_small(packed, shapes):
    flat, out, at = packed.reshape(-1), {}, 0
    for n in SMALL:
        size = int(np.prod(shapes[n]))
        out[n] = flat[at:at + size].reshape(shapes[n])
        at += size
    return out


def _pack_small_by_chip(grads):
    pieces = []
    for n in SMALL:
        g = grads[n].astype(F32)
        if n in SMALL_SHARDED:
            pieces.append(_by_chip(g).reshape(N_CHIPS, -1))
        else:
            pieces.append(jnp.broadcast_to(g.reshape(1, -1), (N_CHIPS, g.size)))
    flat = jnp.concatenate(pieces, axis=1)
    return jnp.pad(flat, ((0, 0), (0, SMALL_ROWS * LANES - flat.shape[1]))).reshape(N_CHIPS, SMALL_ROWS, LANES)


def _gather_weights(local):
    x, y, _ = _position()
    mine = 2 * x + y
    shards = [local[n].reshape(local[n].shape[-2:]).astype(BF16) for n in BIG]
    shards.append(_to_rows(jnp.concatenate([local[n].astype(F32).reshape(-1) for n in SMALL_SHARDED]), GATHER_SMALL_ROWS))
    got = _gather_shards(shards)
    full = []
    for s, g in zip(shards, got):
        g = g.reshape((N_CHIPS,) + s.shape)
        full.append(lax.dynamic_update_slice(g, s[None], (mine, 0, 0)))
    out = {}
    for n, f in zip(BIG, full):
        out[n] = _from_chips(f) if n in BIG_COL_SHARDED else f.reshape(N_CHIPS * f.shape[1], f.shape[2])
    flat, at = full[-1].reshape(N_CHIPS, -1), 0
    for n in SMALL_SHARDED:
        shape = local[n].shape[-2:]
        size = int(np.prod(shape))
        out[n] = _from_chips(flat[:, at:at + size].reshape((N_CHIPS,) + shape))
        at += size
    return out


def kernel(x, meta_tokens, norm1_gain, w_in, fox_b_f, q_norm_gain, k_norm_gain, hg_lb_logits, hg_out_gain, w_branch_a, w_branch_b, w_out, norm2_gain, w_up, conv_w, conv_b, w_down, loss_target, m_meta_tokens, m_norm1_gain, m_w_in, m_fox_b_f, m_q_norm_gain, m_k_norm_gain, m_hg_lb_logits, m_hg_out_gain, m_w_branch_a, m_w_branch_b, m_w_out, m_norm2_gain, m_w_up, m_conv_w, m_conv_b, m_w_down, v_meta_tokens, v_norm1_gain, v_w_in, v_fox_b_f, v_q_norm_gain, v_k_norm_gain, v_hg_lb_logits, v_hg_out_gain, v_w_branch_a, v_w_branch_b, v_w_out, v_norm2_gain, v_w_up, v_conv_w, v_conv_b, v_w_down):
    w_loc = dict(zip(WEIGHT_NAMES, (meta_tokens, norm1_gain, w_in, fox_b_f, q_norm_gain, k_norm_gain, hg_lb_logits, hg_out_gain,
                                    w_branch_a, w_branch_b, w_out, norm2_gain, w_up, conv_w, conv_b, w_down)))
    m_loc = dict(zip(WEIGHT_NAMES, (m_meta_tokens, m_norm1_gain, m_w_in, m_fox_b_f, m_q_norm_gain, m_k_norm_gain, m_hg_lb_logits,
                                    m_hg_out_gain, m_w_branch_a, m_w_branch_b, m_w_out, m_norm2_gain, m_w_up, m_conv_w, m_conv_b,
                                    m_w_down)))
    v_loc = dict(zip(WEIGHT_NAMES, (v_meta_tokens, v_norm1_gain, v_w_in, v_fox_b_f, v_q_norm_gain, v_k_norm_gain, v_hg_lb_logits,
                                    v_hg_out_gain, v_w_branch_a, v_w_branch_b, v_w_out, v_norm2_gain, v_w_up, v_conv_w, v_conv_b,
                                    v_w_down)))
    local_shapes = {n: tuple(w_loc[n].shape) for n in WEIGHT_NAMES}
    px, py, pc = _position()
    c, mine = _scalar(pc), _scalar(2 * px + py)

    weights = {n: w_loc[n].reshape(w_loc[n].shape[-2:]) for n in SMALL if n not in SMALL_SHARDED}
    weights.update(_gather_weights(w_loc))

    lay = _Layout(x.shape[0], x.shape[1])
    loss, grad_x, grads = _local_step(x, loss_target, weights, lay)
    loss = lax.psum(loss, ("x", "y", "c"))

    names = list(BIG) + ["small"]
    by_chip = [grads[n] for n in BIG] + [_pack_small_by_chip(grads)]
    from_sibling = _sibling_exchange(by_chip)
    parts = [_add_own_half(g, s, c, name=f"reduce_add2_{n}") for n, g, s in zip(names, by_chip, from_sibling)]
    from_chips = _chip_exchange(parts)
    own = [_add_chips(p, g, mine, name=f"reduce_add4_{n}") for n, p, g in zip(names, parts, from_chips)]
    other = _sibling_send(own)

    two_d = lambda t: [t[n].reshape(t[n].shape[-2:]) for n in BIG] + [_pack_small(t)]
    results = [_adamw(w_, o_, t_, m_, v_, c, name=f"adamw_{n}")
               for n, w_, o_, t_, m_, v_ in zip(names, two_d(w_loc), own, other, two_d(m_loc), two_d(v_loc))]
    outs = []
    for kind in range(4):
        tree = {n: results[i][kind].reshape(local_shapes[n]) for i, n in enumerate(BIG)}
        tree.update(_unpack_small(results[-1][kind], local_shapes))
        outs += [tree[n] for n in WEIGHT_NAMES]
    return (loss, grad_x, *outs)
```

```python
import functools

import jax
import jax.numpy as jnp
import numpy as np
from jax import lax
from jax.experimental import pallas as pl
from jax.experimental.pallas import tpu as pltpu

F32 = jnp.float32
BF16 = jnp.bfloat16
MXU_DTYPE = BF16
HIGHEST = lax.Precision.HIGHEST

D_MODEL = 1024
N_META = 16
LEAD = 48
ROW0 = LEAD + N_META
FOX_HEADS, FOX_DIM, FOX_W = 8, 64, 512
HG_HEADS, HG_DIM, HG_W = 4, 128, 512
D_FF = 2816
FF2 = 2 * D_FF
EPS = 1e-6
SUB = 16
LANES = 128
N_CHIPS = 4
NEG = -1e30

ADAM_LR, ADAM_B1, ADAM_B2, ADAM_EPS, ADAM_WD, ADAM_STEP = 0.001, 0.9, 0.999, 1e-08, 0.01, 10

VMEM_LIMIT = 56 * 1024 * 1024

C_GA, C_GB = 0, 1
C_FQ, C_FK, C_FV, C_HQ, C_HF, C_HI, C_HG = 4, 5, 6, 7, 8, 9, 10


def _params(n_axes=1):
    return pltpu.CompilerParams(dimension_semantics=("arbitrary",) * n_axes, vmem_limit_bytes=VMEM_LIMIT)


def _pick(n, cands):
    for c in cands:
        if n % c == 0:
            return c
    raise ValueError(f"no tile for {n} among {cands}")


def _rowwise(fn, rows, consts, outs, reds, *, n_rows, tile, name):
    assert n_rows % tile == 0
    rows = [r if isinstance(r, tuple) else (r, r.shape[1], 0) for r in rows]
    nr, nc, no = len(rows), len(consts), len(outs)

    def body(*refs):
        i = pl.program_id(0)
        ins = [r[...] for r in refs[:nr + nc]]
        res = fn(i, *ins)
        res = res if isinstance(res, (tuple, list)) else (res,)
        for ref, v in zip(refs[nr + nc:nr + nc + no], res[:no]):
            ref[...] = v.astype(ref.dtype)
        red_refs = refs[nr + nc + no:]
        if red_refs:
            @pl.when(i == 0)
            def _():
                for ref in red_refs:
                    ref[...] = jnp.zeros_like(ref)
            for ref, v in zip(red_refs, res[no:]):
                ref[...] += v.astype(F32)

    in_specs = [pl.BlockSpec((tile, w), functools.partial(lambda i, j: (i, j), j=j)) for (_, w, j) in rows]
    in_specs += [pl.BlockSpec(c.shape, functools.partial(lambda i, nd: (0,) * nd, nd=c.ndim)) for c in consts]
    out_specs = [pl.BlockSpec((tile, w), lambda i: (i, 0)) for (w, _) in outs]
    out_specs += [pl.BlockSpec(s, functools.partial(lambda i, nd: (0,) * nd, nd=len(s))) for s in reds]
    out_shape = [jax.ShapeDtypeStruct((n_rows, w), dt) for (w, dt) in outs]
    out_shape += [jax.ShapeDtypeStruct(s, F32) for s in reds]
    return pl.pallas_call(
        body, name=name, grid=(n_rows // tile,), in_specs=in_specs, out_specs=out_specs, out_shape=out_shape,
        compiler_params=_params(1),
    )(*[r[0] for r in rows], *consts)


def _matmul(a, b, *, trans_a=False, trans_b=False, out_dtype=F32, by_chip=False, name):
    if trans_a:
        k, m = a.shape
    else:
        m, k = a.shape
    n = b.shape[0] if trans_b else b.shape[1]
    assert (b.shape[1] if trans_b else b.shape[0]) == k
    if trans_a:
        tm = _pick(m, (1408, 1024, 512, 256, 128))
        tk = _pick(k, (1088, 1024, 768, 512, 256))
    else:
        tm = _pick(m, (1088, 512, 256, 128))
        tk = k if k <= 1024 else _pick(k, (1408, 1024, 512))
    tn = n // N_CHIPS if by_chip else _pick(n, (1408, 1024, 512, 256, 128))
    nk = k // tk
    dims = (((0 if trans_a else 1,), (1 if trans_b else 0,)), ((), ()))

    def body(a_ref, b_ref, o_ref, acc_ref):
        out = o_ref.at[0] if by_chip else o_ref
        part = lax.dot_general(a_ref[...], b_ref[...], dims, preferred_element_type=F32)
        if nk == 1:
            out[...] = part.astype(out.dtype)
        else:
            kk = pl.program_id(2)

            @pl.when(kk == 0)
            def _():
                acc_ref[...] = part

            @pl.when(kk > 0)
            def _():
                acc_ref[...] += part

            @pl.when(kk == nk - 1)
            def _():
                out[...] = acc_ref[...].astype(out.dtype)

    a_spec = pl.BlockSpec((tk, tm), lambda i, j, kk: (kk, i)) if trans_a else pl.BlockSpec((tm, tk), lambda i, j, kk: (i, kk))
    b_spec = pl.BlockSpec((tn, tk), lambda i, j, kk: (j, kk)) if trans_b else pl.BlockSpec((tk, tn), lambda i, j, kk: (kk, j))
    if by_chip:
        out_spec, out_shape = pl.BlockSpec((1, tm, tn), lambda i, j, kk: (j, i, 0)), (N_CHIPS, m, tn)
    else:
        out_spec, out_shape = pl.BlockSpec((tm, tn), lambda i, j, kk: (i, j)), (m, n)
    return pl.pallas_call(
        body, name=name, grid=(m // tm, n // tn, nk), in_specs=[a_spec, b_spec], out_specs=out_spec,
        out_shape=jax.ShapeDtypeStruct(out_shape, out_dtype),
        scratch_shapes=[pltpu.VMEM((tm, tn) if nk > 1 else (8, LANES), F32)],
        compiler_params=_params(3),
    )(a, b)


def _sigmoid(x):
    return 1.0 / (1.0 + jnp.exp(-x))


def _silu(x):
    return x * _sigmoid(x)


def _log_sigmoid(x):
    return jnp.minimum(x, 0.0) - jnp.log(1.0 + jnp.exp(-jnp.abs(x)))


def _rms(x, gain):
    return x * lax.rsqrt(jnp.mean(x * x, axis=-1, keepdims=True) + EPS) * gain


def _group_matrix(width, group):
    g = (np.arange(width)[:, None] // group == np.arange(LANES)[None, :]).astype(np.float32)
    return jnp.asarray(g, MXU_DTYPE), jnp.asarray(g.T.copy(), MXU_DTYPE)


def _split_dot(x, mat):
    dt = mat.dtype
    hi = x.astype(dt)
    r1 = x - hi.astype(F32)
    mid = r1.astype(dt)
    lo = (r1 - mid.astype(F32)).astype(dt)
    dot = lambda a: jnp.dot(a, mat, preferred_element_type=F32)
    return dot(hi) + dot(mid) + dot(lo)


@jax.custom_vjp
def _group_sum(x, gmat, gmat_t):
    return _split_dot(x, gmat)


@jax.custom_vjp
def _group_spread(s, gmat, gmat_t):
    return _split_dot(s, gmat_t)


_group_sum.defvjp(lambda x, g, gt: (_split_dot(x, g), (g, gt)),
                  lambda res, ct: (_group_spread(ct, *res), jnp.zeros_like(res[0]), jnp.zeros_like(res[1])))
_group_spread.defvjp(lambda s, g, gt: (_split_dot(s, gt), (g, gt)),
                     lambda res, ct: (_group_sum(ct, *res), jnp.zeros_like(res[0]), jnp.zeros_like(res[1])))


def _group_rms(x, gain, gmat, gmat_t, group):
    rstd = lax.rsqrt(_group_sum(x * x, gmat, gmat_t) * (1.0 / group) + EPS)
    return x * _group_spread(rstd, gmat, gmat_t) * gain


def _head_rms(x, gain):
    outs = []
    for h in range(x.shape[1] // LANES):
        xs = x[:, h * LANES:(h + 1) * LANES]
        outs.append(xs * lax.rsqrt(jnp.mean(xs * xs, axis=-1, keepdims=True) + EPS) * gain)
    return jnp.concatenate(outs, axis=1)


class _Layout:
    def __init__(self, batch, seq):
        self.batch, self.seq = batch, seq
        self.l_real = N_META + seq
        self.lp = -(-(LEAD + self.l_real) // 256) * 256
        self.n = batch * self.lp
        self.tile = _pick(self.lp, (512, 256))

    def valid(self, i, tile):
        per = self.lp // tile
        r = lax.rem(i, per) * tile + lax.broadcasted_iota(jnp.int32, (tile, 1), 0)
        return (r >= LEAD) & (r < LEAD + self.l_real)


def _cumsum_rows(x, lay, *, reverse, name):
    t = LANES
    nt = lay.lp // t
    c = x.shape[1]

    def body(x_ref, o_ref, carry):
        j = pl.program_id(1)

        @pl.when(j == 0)
        def _():
            carry[...] = jnp.zeros_like(carry)

        r = lax.broadcasted_iota(jnp.int32, (t, t), 0)
        q = lax.broadcasted_iota(jnp.int32, (t, t), 1)
        tri = jnp.where((q >= r) if reverse else (q <= r), 1.0, 0.0).astype(F32)
        xs = x_ref[...]
        out = jnp.dot(tri, xs, precision=HIGHEST, preferred_element_type=F32) + carry[0:1, :]
        o_ref[...] = out
        carry[...] = jnp.broadcast_to(carry[0:1, :] + jnp.sum(xs, axis=0, keepdims=True), carry.shape)

    def idx(b, j):
        return (b * nt + (nt - 1 - j if reverse else j), 0)

    return pl.pallas_call(
        body, name=name, grid=(lay.batch, nt),
        in_specs=[pl.BlockSpec((t, c), idx)], out_specs=pl.BlockSpec((t, c), idx),
        out_shape=jax.ShapeDtypeStruct(x.shape, F32),
        scratch_shapes=[pltpu.VMEM((8, c), F32)],
        compiler_params=_params(2),
    )(x)


def _group_cumsum(x, tile, *, reverse):
    r = lax.rem(lax.broadcasted_iota(jnp.int32, (tile, 1), 0), SUB)
    s = 1
    while s < SUB:
        if reverse:
            x = x + jnp.where(r < SUB - s, pltpu.roll(x, tile - s, 0), 0.0)
        else:
            x = x + jnp.where(r >= s, pltpu.roll(x, s, 0), 0.0)
        s *= 2
    return x


AUG = 128
FOX_BK = 256
FOX_BQ = 256
FOX_SCALE = FOX_DIM ** -0.5
KT_ROWS = FOX_DIM + 16


def _aug_matrices():
    e1 = np.zeros((FOX_W, FOX_HEADS * AUG), np.float32)
    e2 = np.zeros((LANES, FOX_HEADS * AUG), np.float32)
    ones = np.zeros((1, FOX_HEADS * AUG), np.float32)
    for h in range(FOX_HEADS):
        for d in range(FOX_DIM):
            e1[h * FOX_DIM + d, h * AUG + d] = 1.0
        for j in range(3):
            e2[j * FOX_HEADS + h, h * AUG + FOX_DIM + j] = 1.0
            ones[0, h * AUG + FOX_DIM + j] = 1.0
    return jnp.asarray(e1, MXU_DTYPE), jnp.asarray(e2, MXU_DTYPE), jnp.asarray(ones)


def _fox_augment(q, k, cum, key_ok, e1, e2, ones):
    dt = q.dtype
    c = jnp.where(key_ok, -cum, NEG)
    hi = c.astype(dt)
    r1 = c - hi.astype(F32)
    mid = r1.astype(dt)
    lo = (r1 - mid.astype(F32)).astype(dt)
    lane = lax.broadcasted_iota(jnp.int32, c.shape, 1)
    shift = lambda a, by: pltpu.roll(a.astype(F32), by, 1)
    parts = jnp.where(lane < FOX_HEADS, hi.astype(F32),
                      jnp.where(lane < 2 * FOX_HEADS, shift(mid, FOX_HEADS),
                                jnp.where(lane < 3 * FOX_HEADS, shift(lo, 2 * FOX_HEADS), 0.0))).astype(dt)
    qs = (q.astype(F32) * FOX_SCALE).astype(dt)
    q_aug = jnp.dot(qs, e1, preferred_element_type=F32) + ones
    k_aug = jnp.dot(k, e1, preferred_element_type=F32) + jnp.dot(parts, e2, preferred_element_type=F32)
    return q_aug.astype(dt), k_aug.astype(dt)


def _fox_tile(k_blk, q_blk, k0, q0, masked):
    st = lax.dot_general(k_blk, q_blk, (((1,), (1,)), ((), ())), preferred_element_type=F32)
    if masked:
        keys = k0 + lax.broadcasted_iota(jnp.int32, st.shape, 0)
        qs = q0 + lax.broadcasted_iota(jnp.int32, st.shape, 1)
        st = jnp.where(keys <= qs, st, NEG)
    return st


def _fox_fwd_t(q_aug, k_aug, v_t, lay):
    bk, bq = FOX_BK, FOX_BQ
    nq = lay.lp // bq
    pairs = FOX_HEADS // 2

    def body(q_ref, k_ref, vt_ref, ot_ref, lse_ref, zeros_ref):
        heads = [(slice(hh * AUG, (hh + 1) * AUG), slice(hh * FOX_DIM, (hh + 1) * FOX_DIM)) for hh in range(2)]
        zeros_ref[...] = jnp.zeros_like(zeros_ref)

        def q_loop(qb, _):
            q0 = pl.multiple_of(qb * bq, bq)
            q_blks = [q_ref[pl.ds(q0, bq), lanes] for lanes, _ in heads]

            def scores(kb, h):
                k0 = pl.multiple_of(kb * bk, bk)
                return _fox_tile(k_ref[pl.ds(k0, bk), heads[h][0]], q_blks[h], k0, q0, False)

            def consume(kb, h, state, masked):
                m, l, acc, pend, st = state
                k0 = pl.multiple_of(kb * bk, bk)
                if masked:
                    keys = k0 + lax.broadcasted_iota(jnp.int32, st.shape, 0)
                    qs_ = q0 + lax.broadcasted_iota(jnp.int32, st.shape, 1)
                    st = jnp.where(keys <= qs_, st, NEG)
                m_new = jnp.maximum(m, jnp.max(st, axis=0, keepdims=True))
                alpha = jnp.exp(m - m_new)
                p = jnp.exp(st - m_new)
                l = alpha * l + jnp.sum(p, axis=0, keepdims=True)
                acc = alpha * (acc + pend)
                pend = jnp.dot(vt_ref[heads[h][1], pl.ds(k0, bk)], p.astype(vt_ref.dtype), preferred_element_type=F32)
                return m_new, l, acc, pend

            def k_step(kb, states):
                nxt = [scores(kb + 1, h) for h in range(2)]
                return tuple(consume(kb, h, states[h], False) + (nxt[h],) for h in range(2))

            states = tuple((jnp.full((1, bq), NEG, F32), jnp.zeros((1, bq), F32), zeros_ref[...], zeros_ref[...], scores(0, h))
                           for h in range(2))
            states = lax.fori_loop(0, qb, k_step, states)
            qs = q0 + lax.broadcasted_iota(jnp.int32, (1, bq), 1)
            ok = (qs >= LEAD) & (qs < LEAD + lay.l_real)
            for hh in range(2):
                m, l, acc, pend = consume(qb, hh, states[hh], True)
                ot_ref[heads[hh][1], pl.ds(q0, bq)] = jnp.where(ok, (acc + pend) / l, 0.0).astype(ot_ref.dtype)
                lse_ref[hh, :, pl.ds(q0, bq)] = m + jnp.log(l)
            return 0

        lax.fori_loop(0, nq, q_loop, 0)

    aug = pl.BlockSpec((lay.lp, 2 * AUG), lambda b, p: (b, p))
    tr = pl.BlockSpec((2 * FOX_DIM, lay.lp), lambda b, p: (p, b))
    return pl.pallas_call(
        body, name="fox_fwd", grid=(lay.batch, pairs),
        in_specs=[aug, aug, tr],
        out_specs=[tr, pl.BlockSpec((2, 1, lay.lp), lambda b, p: (b * pairs + p, 0, 0))],
        out_shape=[jax.ShapeDtypeStruct((FOX_W, lay.n), MXU_DTYPE),
                   jax.ShapeDtypeStruct((lay.batch * FOX_HEADS, 1, lay.lp), F32)],
        scratch_shapes=[pltpu.VMEM((FOX_DIM, bq), F32)],
        compiler_params=_params(2),
    )(q_aug, k_aug, v_t)


def _fox_bwd_t(q_aug, k_aug, v, do, k_t, o_t, do_t, lse, lay):
    bk, bq = FOX_BK, FOX_BQ
    nq, nk = lay.lp // bq, lay.lp // bk
    pairs = FOX_HEADS // 2

    def body(q_ref, k_ref, v_ref, do_ref, kt_ref, ot_ref, dot_ref, lse_ref, dqt_ref, dk_ref, dv_ref, delta):
        dqt_ref[...] = jnp.zeros_like(dqt_ref)
        dk_ref[...] = jnp.zeros_like(dk_ref)
        dv_ref[...] = jnp.zeros_like(dv_ref)
        heads = [(hh, slice(hh * AUG, (hh + 1) * AUG), slice(hh * FOX_DIM, (hh + 1) * FOX_DIM),
                  slice(hh * KT_ROWS, (hh + 1) * KT_ROWS)) for hh in range(2)]

        def delta_loop(qb, _):
            q0 = pl.multiple_of(qb * bq, bq)
            for hh, _, cols, _ in heads:
                prod = ot_ref[cols, pl.ds(q0, bq)].astype(F32) * dot_ref[cols, pl.ds(q0, bq)].astype(F32)
                delta[hh, :, pl.ds(q0, bq)] = jnp.sum(prod, axis=0, keepdims=True)
            return 0

        lax.fori_loop(0, nq, delta_loop, 0)

        def k_loop(kb, _):
            k0 = pl.multiple_of(kb * bk, bk)

            def products(qb, h):
                q0 = pl.multiple_of(qb * bq, bq)
                _, lanes, cols, _ = heads[h]
                st = _fox_tile(k_ref[pl.ds(k0, bk), lanes], q_ref[pl.ds(q0, bq), lanes], k0, q0, False)
                dpt = lax.dot_general(v_ref[pl.ds(k0, bk), cols], do_ref[pl.ds(q0, bq), cols], (((1,), (1,)), ((), ())),
                                      preferred_element_type=F32)
                return st, dpt

            def consume(qb, h, st, dpt, masked):
                q0 = pl.multiple_of(qb * bq, bq)
                hh, lanes, cols, trows = heads[h]
                if masked:
                    keys = k0 + lax.broadcasted_iota(jnp.int32, st.shape, 0)
                    qs = q0 + lax.broadcasted_iota(jnp.int32, st.shape, 1)
                    st = jnp.where(keys <= qs, st, NEG)
                q_blk = q_ref[pl.ds(q0, bq), lanes]
                do_blk = do_ref[pl.ds(q0, bq), cols]
                pt = jnp.exp(st - lse_ref[hh, :, pl.ds(q0, bq)])
                dst = (pt * (dpt - delta[hh, :, pl.ds(q0, bq)])).astype(q_blk.dtype)
                dv_ref[pl.ds(k0, bk), cols] += jnp.dot(pt.astype(do_blk.dtype), do_blk, preferred_element_type=F32)
                dk_ref[pl.ds(k0, bk), lanes] += jnp.dot(dst, q_blk, preferred_element_type=F32)
                dqt_ref[trows, pl.ds(q0, bq)] += jnp.dot(kt_ref[trows, pl.ds(k0, bk)], dst, preferred_element_type=F32)

            after = lambda qb: jnp.minimum(qb + 1, nq - 1)
            cur = [products(kb, h) for h in range(2)]
            nxt = tuple(products(after(kb), h) for h in range(2))
            for h in range(2):
                consume(kb, h, *cur[h], True)

            def rest(qb, held):
                new = tuple(products(after(qb), h) for h in range(2))
                for h in range(2):
                    consume(qb, h, *held[h], False)
                return new

            lax.fori_loop(kb + 1, nq, rest, nxt)
            return 0

        lax.fori_loop(0, nk, k_loop, 0)

    aug = pl.BlockSpec((lay.lp, 2 * AUG), lambda b, p: (b, p))
    rows = pl.BlockSpec((lay.lp, 2 * FOX_DIM), lambda b, p: (b, p))
    tr = pl.BlockSpec((2 * FOX_DIM, lay.lp), lambda b, p: (p, b))
    tr_k = pl.BlockSpec((2 * KT_ROWS, lay.lp), lambda b, p: (p, b))
    return pl.pallas_call(
        body, name="fox_bwd", grid=(lay.batch, pairs),
        in_specs=[aug, aug, rows, rows, tr_k, tr, tr, pl.BlockSpec((2, 1, lay.lp), lambda b, p: (b * pairs + p, 0, 0))],
        out_specs=[tr_k, aug, rows],
        out_shape=[jax.ShapeDtypeStruct((FOX_HEADS * KT_ROWS, lay.n), F32), jax.ShapeDtypeStruct((lay.n, FOX_HEADS * AUG), F32),
                   jax.ShapeDtypeStruct((lay.n, FOX_W), F32)],
        scratch_shapes=[pltpu.VMEM((2, 1, lay.lp), F32)],
        compiler_params=_params(2),
    )(q_aug, k_aug, v, do, k_t, o_t, do_t, lse)


def _hgrn_fwd(proj, kk, gl, lay):
    t = lay.tile
    nt = lay.lp // t
    nsc = t // SUB

    def body(q_ref, k_ref, g_ref, v_ref, o_ref, st_ref, state, sub_rows):
        @pl.when(pl.program_id(1) == 0)
        def _():
            state[...] = jnp.zeros_like(state)

        rowi = lax.broadcasted_iota(jnp.int32, (SUB, 1), 0)

        def sub(sc, _):
            r0 = pl.multiple_of(sc * SUB, SUB)
            sub_rows[0] = k_ref[pl.ds(r0, SUB), :]
            sub_rows[1] = g_ref[pl.ds(r0, SUB), :]
            sub_rows[2] = v_ref[pl.ds(r0, SUB), :]
            for h in range(HG_HEADS):
                lanes = slice(h * HG_DIM, (h + 1) * HG_DIM)
                q16 = q_ref[pl.ds(r0, SUB), lanes]
                k16 = sub_rows[0, :, lanes]
                g16 = sub_rows[1, :, lanes]
                v16 = sub_rows[2, :, lanes]
                g_end = sub_rows[1, SUB - 1:SUB, lanes]
                s_prev = state[h]
                st_ref[sc, h] = s_prev
                o = lax.dot_general((q16 * jnp.exp(g16)).astype(MXU_DTYPE), s_prev.astype(MXU_DTYPE),
                                    (((1,), (1,)), ((), ())), preferred_element_type=F32)
                for s in range(SUB):
                    ks = sub_rows[0, s:s + 1, lanes]
                    gs = sub_rows[1, s:s + 1, lanes]
                    vs = sub_rows[2, s:s + 1, lanes]
                    w = q16 * jnp.exp(jnp.minimum(g16 - gs, 0.0)) * ks
                    a = jnp.where(rowi >= s, jnp.sum(w, axis=1, keepdims=True), 0.0)
                    o = o + a * vs
                o_ref[pl.ds(r0, SUB), lanes] = o
                kt = k16 * jnp.exp(g_end - g16)
                upd = lax.dot_general(v16.astype(MXU_DTYPE), kt.astype(MXU_DTYPE), (((0,), (0,)), ((), ())),
                                      preferred_element_type=F32)
                state[h] = jnp.exp(g_end) * s_prev + upd
            return 0

        lax.fori_loop(0, nsc, sub, 0)

    rows = lambda col: pl.BlockSpec((t, HG_W), functools.partial(lambda b, i, col: (b * nt + i, col), col=col))
    return pl.pallas_call(
        body, name="hgrn_fwd", grid=(lay.batch, nt),
        in_specs=[rows(C_HQ), rows(0), rows(0), rows(C_HI)],
        out_specs=[rows(0), pl.BlockSpec((nsc, HG_HEADS, HG_DIM, HG_DIM), lambda b, i: (b * nt + i, 0, 0, 0))],
        out_shape=[jax.ShapeDtypeStruct((lay.n, HG_W), F32),
                   jax.ShapeDtypeStruct((lay.n // SUB, HG_HEADS, HG_DIM, HG_DIM), F32)],
        scratch_shapes=[pltpu.VMEM((HG_HEADS, HG_DIM, HG_DIM), F32), pltpu.VMEM((3, SUB, HG_W), F32)],
        compiler_params=_params(2),
    )(proj, kk, gl, proj)


def _hgrn_bwd(proj, kk, gl, do, states, lay):
    t = lay.tile
    nt = lay.lp // t
    nsc = t // SUB

    def body(q_ref, k_ref, g_ref, v_ref, do_ref, st_ref, dq_ref, dk_ref, dv_ref, dg_ref, dstate, sub_rows):
        @pl.when(pl.program_id(1) == 0)
        def _():
            dstate[...] = jnp.zeros_like(dstate)

        rowi = lax.broadcasted_iota(jnp.int32, (SUB, 1), 0)

        def sub(it, _):
            sc = nsc - 1 - it
            r0 = pl.multiple_of(sc * SUB, SUB)
            sub_rows[0] = k_ref[pl.ds(r0, SUB), :]
            sub_rows[1] = g_ref[pl.ds(r0, SUB), :]
            sub_rows[2] = v_ref[pl.ds(r0, SUB), :]
            for h in range(HG_HEADS):
                lanes = slice(h * HG_DIM, (h + 1) * HG_DIM)
                q16 = q_ref[pl.ds(r0, SUB), lanes]
                k16 = sub_rows[0, :, lanes]
                g16 = sub_rows[1, :, lanes]
                v16 = sub_rows[2, :, lanes]
                do16 = do_ref[pl.ds(r0, SUB), lanes]
                g_end = sub_rows[1, SUB - 1:SUB, lanes]
                s_prev = st_ref[sc, h]
                ds_end = dstate[h]
                eg = jnp.exp(g16)
                ekt = jnp.exp(g_end - g16)
                e_end = jnp.exp(g_end)
                qt = q16 * eg
                kt = k16 * ekt
                ds_mx = ds_end.astype(MXU_DTYPE)
                dv = lax.dot_general(kt.astype(MXU_DTYPE), ds_mx, (((1,), (1,)), ((), ())), preferred_element_type=F32)
                dkt = jnp.dot(v16.astype(MXU_DTYPE), ds_mx, preferred_element_type=F32)
                dk = dkt * ekt
                ktdkt = kt * dkt
                dg_end = jnp.sum(ktdkt, axis=0, keepdims=True) + jnp.sum(s_prev * ds_end, axis=0, keepdims=True) * e_end
                dg = jnp.where(rowi == SUB - 1, dg_end, 0.0) - ktdkt
                dqt = jnp.dot(do16.astype(MXU_DTYPE), s_prev.astype(MXU_DTYPE), preferred_element_type=F32)
                dq = dqt * eg
                dg = dg + qt * dqt
                dstate[h] = e_end * ds_end + lax.dot_general(do16.astype(MXU_DTYPE), qt.astype(MXU_DTYPE),
                                                             (((0,), (0,)), ((), ())), preferred_element_type=F32)
                for s in range(SUB):
                    ks = sub_rows[0, s:s + 1, lanes]
                    gs = sub_rows[1, s:s + 1, lanes]
                    vs = sub_rows[2, s:s + 1, lanes]
                    live = rowi >= s
                    e = jnp.where(live, jnp.exp(jnp.minimum(g16 - gs, 0.0)), 0.0)
                    qe = q16 * e
                    a = jnp.sum(qe * ks, axis=1, keepdims=True)
                    da = jnp.where(live, jnp.sum(do16 * vs, axis=1, keepdims=True), 0.0)
                    dv_row = jnp.sum(a * do16, axis=0, keepdims=True)
                    t1 = da * qe
                    dk_row = jnp.sum(t1, axis=0, keepdims=True)
                    dq = dq + da * (e * ks)
                    is_s = rowi == s
                    dv = dv + jnp.where(is_s, dv_row, 0.0)
                    dk = dk + jnp.where(is_s, dk_row, 0.0)
                    dg = dg + t1 * ks - jnp.where(is_s, ks * dk_row, 0.0)
                dq_ref[pl.ds(r0, SUB), lanes] = dq
                dk_ref[pl.ds(r0, SUB), lanes] = dk
                dv_ref[pl.ds(r0, SUB), lanes] = dv
                dg_ref[pl.ds(r0, SUB), lanes] = dg
            return 0

        lax.fori_loop(0, nsc, sub, 0)

    def rows(col):
        return pl.BlockSpec((t, HG_W), functools.partial(lambda b, i, col: (b * nt + nt - 1 - i, col), col=col))

    out = jax.ShapeDtypeStruct((lay.n, HG_W), F32)
    return pl.pallas_call(
        body, name="hgrn_bwd", grid=(lay.batch, nt),
        in_specs=[rows(C_HQ), rows(0), rows(0), rows(C_HI), rows(0),
                  pl.BlockSpec((nsc, HG_HEADS, HG_DIM, HG_DIM), lambda b, i: (b * nt + nt - 1 - i, 0, 0, 0))],
        out_specs=[rows(0)] * 4, out_shape=[out] * 4,
        scratch_shapes=[pltpu.VMEM((HG_HEADS, HG_DIM, HG_DIM), F32), pltpu.VMEM((3, SUB, HG_W), F32)],
        compiler_params=_params(2),
    )(proj, kk, gl, proj, do, states)


CONV_COLS = 256


def _shift_down(x, halo, tile, by):
    out = pltpu.roll(x, by, 0)
    rowi = lax.broadcasted_iota(jnp.int32, (tile, 1), 0)
    for r in range(by):
        out = jnp.where(rowi == r, halo[8 - by + r:8 - by + r + 1, :], out)
    return out


def _shift_up(x, halo, tile, by):
    out = pltpu.roll(x, tile - by, 0)
    rowi = lax.broadcasted_iota(jnp.int32, (tile, 1), 0)
    for r in range(by):
        out = jnp.where(rowi == tile - by + r, halo[r:r + 1, :], out)
    return out


def _conv_specs(tile):
    ncb = D_FF // CONV_COLS
    per8 = tile // 8

    def tile_spec(off):
        return pl.BlockSpec((tile, CONV_COLS), functools.partial(lambda i, j, off: (i, j + off), off=off))

    def prev_spec(off):
        return pl.BlockSpec((8, CONV_COLS), functools.partial(lambda i, j, off: (jnp.maximum(i * per8 - 1, 0), j + off), off=off))

    def w_spec(off):
        return pl.BlockSpec((3, CONV_COLS), functools.partial(lambda i, j, off: (0, j + off), off=off))

    def b_spec(off):
        return pl.BlockSpec((1, CONV_COLS), functools.partial(lambda i, j, off: (0, j + off), off=off))

    return ncb, tile_spec, prev_spec, w_spec, b_spec


def _conv3(x, halo, w, b, tile):
    return w[0:1, :] * _shift_down(x, halo, tile, 2) + w[1:2, :] * _shift_down(x, halo, tile, 1) + w[2:3, :] * x + b


def _conv_act_fwd(u, conv_w, conv_b, lay):
    tile = lay.tile
    ncb, tile_spec, prev_spec, w_spec, b_spec = _conv_specs(tile)

    def body(ug, uv, pg, pv, wg, wv, bg, bv, o_ref):
        cg = _conv3(ug[...], pg, wg, bg[...], tile)
        cv = _conv3(uv[...], pv, wv, bv[...], tile)
        o_ref[...] = (_silu(cg) * cv).astype(o_ref.dtype)

    return pl.pallas_call(
        body, name="conv_act_fwd", grid=(lay.n // tile, ncb),
        in_specs=[tile_spec(0), tile_spec(ncb), prev_spec(0), prev_spec(ncb), w_spec(0), w_spec(ncb), b_spec(0), b_spec(ncb)],
        out_specs=pl.BlockSpec((tile, CONV_COLS), lambda i, j: (i, j)),
        out_shape=jax.ShapeDtypeStruct((lay.n, D_FF), MXU_DTYPE),
        compiler_params=_params(2),
    )(u, u, u, u, conv_w, conv_w, conv_b, conv_b)


def _conv_act_bwd(u, dact, conv_w, conv_b, lay):
    tile = lay.tile
    ncb, tile_spec, prev_spec, w_spec, b_spec = _conv_specs(tile)

    def body(ug, uv, pg, pv, wg, wv, bg, bv, da_ref, dg_ref, dv_ref, gwg, gwv, gbg, gbv):
        @pl.when(pl.program_id(1) == 0)
        def _():
            for r in (gwg, gwv, gbg, gbv):
                r[...] = jnp.zeros_like(r)

        xg, xv = ug[...], uv[...]
        cg = _conv3(xg, pg, wg, bg[...], tile)
        cv = _conv3(xv, pv, wv, bv[...], tile)
        da = da_ref[...].astype(F32)
        sg = _sigmoid(cg)
        dcv = da * (cg * sg)
        dcg = da * cv * (sg * (1.0 + cg * (1.0 - sg)))
        dg_ref[...] = dcg
        dv_ref[...] = dcv
        for x, halo, dc, gw, gb in ((xg, pg, dcg, gwg, gbg), (xv, pv, dcv, gwv, gbv)):
            gw[0, 0:1, :] += jnp.sum(dc * _shift_down(x, halo, tile, 2), axis=0, keepdims=True)
            gw[0, 1:2, :] += jnp.sum(dc * _shift_down(x, halo, tile, 1), axis=0, keepdims=True)
            gw[0, 2:3, :] += jnp.sum(dc * x, axis=0, keepdims=True)
            gb[0] += jnp.sum(dc, axis=0, keepdims=True)

    swap = lambda spec: pl.BlockSpec(spec.block_shape, functools.partial(lambda j, i, f: f(i, j), f=spec.index_map))
    col = lambda j, i: (i, j)
    red_w = pl.BlockSpec((1, 3, CONV_COLS), lambda j, i: (j, 0, 0))
    red_b = pl.BlockSpec((1, 1, CONV_COLS), lambda j, i: (j, 0, 0))
    outs = pl.pallas_call(
        body, name="conv_act_bwd", grid=(ncb, lay.n // tile),
        in_specs=[swap(s) for s in (tile_spec(0), tile_spec(ncb), prev_spec(0), prev_spec(ncb), w_spec(0), w_spec(ncb),
                                    b_spec(0), b_spec(ncb))] + [pl.BlockSpec((tile, CONV_COLS), col)],
        out_specs=[pl.BlockSpec((tile, CONV_COLS), col), pl.BlockSpec((tile, CONV_COLS), col), red_w, red_w, red_b, red_b],
        out_shape=[jax.ShapeDtypeStruct((lay.n, D_FF), F32), jax.ShapeDtypeStruct((lay.n, D_FF), F32),
                   jax.ShapeDtypeStruct((ncb, 3, CONV_COLS), F32), jax.ShapeDtypeStruct((ncb, 3, CONV_COLS), F32),
                   jax.ShapeDtypeStruct((ncb, 1, CONV_COLS), F32), jax.ShapeDtypeStruct((ncb, 1, CONV_COLS), F32)],
        compiler_params=_params(2),
    )(u, u, u, u, conv_w, conv_w, conv_b, conv_b, dact)
    dcg, dcv, gwg, gwv, gbg, gbv = outs
    unblock = lambda g: jnp.transpose(g, (1, 0, 2)).reshape(g.shape[1], D_FF)
    g_w = jnp.concatenate([unblock(gwg), unblock(gwv)], axis=1)
    g_b = jnp.concatenate([unblock(gbg), unblock(gbv)], axis=1)
    return dcg, dcv, g_w, g_b


def _conv_input_bwd(dcg, dcv, conv_w, lay):
    tile = lay.tile
    ncb = D_FF // CONV_COLS
    nblk8 = lay.n // 8
    per8 = tile // 8
    nxt = lambda i: jnp.minimum((i + 1) * per8, nblk8 - 1)

    def half(dc, off, into, name):
        def body(*refs):
            d, halo, w, o = refs[0], refs[1], refs[2], refs[-1]
            x = d[...]
            du = w[2:3, :] * x + w[1:2, :] * _shift_up(x, halo, tile, 1) + w[0:1, :] * _shift_up(x, halo, tile, 2)
            o[...] = jnp.where(lay.valid(pl.program_id(0), tile), du, 0.0).astype(o.dtype)

        in_specs = [pl.BlockSpec((tile, CONV_COLS), lambda i, j: (i, j)),
                    pl.BlockSpec((8, CONV_COLS), lambda i, j: (nxt(i), j)),
                    pl.BlockSpec((3, CONV_COLS), lambda i, j: (0, j + off))]
        args = [dc, dc, conv_w]
        if into is not None:
            in_specs.append(pl.BlockSpec(memory_space=pltpu.HBM))
            args.append(into)
        return pl.pallas_call(
            body, name=name, grid=(lay.n // tile, ncb), in_specs=in_specs,
            out_specs=pl.BlockSpec((tile, CONV_COLS), lambda i, j: (i, j + off)),
            out_shape=jax.ShapeDtypeStruct((lay.n, FF2), MXU_DTYPE),
            input_output_aliases={} if into is None else {3: 0},
            compiler_params=_params(2),
        )(*args)

    return half(dcv, ncb, half(dcg, 0, None, "conv_input_bwd_gate"), "conv_input_bwd_value")


def _loss_head(h1, mlp, target, lay):
    t = 64
    per = lay.lp // t
    nreal = lay.seq // t
    first = ROW0 // t

    def body(h_ref, m_ref, t_ref, loss_ref, dy_ref, dyb_ref):
        b, j = pl.program_id(0), pl.program_id(1)

        @pl.when((b == 0) & (j == 0))
        def _():
            loss_ref[...] = jnp.zeros_like(loss_ref)

        real = (j >= first) & (j < first + nreal)
        err = jnp.where(real, h_ref[...] + m_ref[...] - t_ref[...], 0.0)
        dy = err * (1.0 / D_MODEL)
        dy_ref[...] = dy
        dyb_ref[...] = dy.astype(dyb_ref.dtype)
        loss_ref[...] += 0.5 * jnp.sum(err * dy)

    rows = pl.BlockSpec((t, D_MODEL), lambda b, j: (b * per + j, 0))
    tgt = pl.BlockSpec((t, D_MODEL), lambda b, j: (b * nreal + jnp.clip(j - first, 0, nreal - 1), 0))
    return pl.pallas_call(
        body, name="loss_head", grid=(lay.batch, per),
        in_specs=[rows, rows, tgt],
        out_specs=[pl.BlockSpec((8, LANES), lambda b, j: (0, 0)), rows, rows],
        out_shape=[jax.ShapeDtypeStruct((8, LANES), F32), jax.ShapeDtypeStruct((lay.n, D_MODEL), F32),
                   jax.ShapeDtypeStruct((lay.n, D_MODEL), MXU_DTYPE)],
        compiler_params=_params(2),
    )(h1, mlp, target)


def _fox_prep(fq, fk, ff, gq, gk, bf, gmat, gmat_t, valid):
    q = _group_rms(fq, gq, gmat, gmat_t, FOX_DIM)
    k = _group_rms(fk, gk, gmat, gmat_t, FOX_DIM)
    logf = jnp.where(valid, _log_sigmoid(ff + bf), 0.0)
    return q, k, logf


def _hg_prep(hf, l0, l1):
    mx = jnp.maximum(l0, l1)
    e0, e1 = jnp.exp(l0 - mx), jnp.exp(l1 - mx)
    lb = e0 / (e0 + e1)
    lf = jnp.log(lb + (1.0 - lb) * _sigmoid(hf))
    kk = (1.0 - lb) * _sigmoid(-hf)
    return lf, kk


def _hg_post(o, hg, gain):
    return _head_rms(o, gain) * _silu(hg)


def _gate(ga, gb, ya, yb):
    return _sigmoid(ga) * ya + _sigmoid(gb) * yb


def _by_chip(g):
    return jnp.transpose(g.reshape(g.shape[0], N_CHIPS, g.shape[1] // N_CHIPS), (1, 0, 2))


def _from_chips(a):
    return jnp.transpose(a, (1, 0, 2)).reshape(a.shape[1], N_CHIPS * a.shape[2])


def _local_step(x, target, w, lay):
    n, tile = lay.n, lay.tile
    rw = functools.partial(_rowwise, n_rows=n, tile=tile)
    mx = lambda a: a.astype(MXU_DTYPE)

    w_in = w["w_in"]
    fq, fk, fv, ffw, hq, hf, hi, hg, ga, gb = jnp.split(w_in, list(np.cumsum([512, 512, 512, 8, 512, 512, 512, 512, 1024])), axis=1)
    w_main = mx(jnp.concatenate([ga, gb, fq, fk, fv, hq, hf, hi, hg], axis=1))
    w_ff = mx(jnp.pad(ffw, ((0, 0), (0, LANES - FOX_HEADS))))
    w_a, w_b, w_out, w_up, w_down = mx(w["w_branch_a"]), mx(w["w_branch_b"]), mx(w["w_out"]), mx(w["w_up"]), mx(w["w_down"])
    conv_w, conv_b = w["conv_w"].astype(F32), w["conv_b"].astype(F32)
    g1, g2 = w["norm1_gain"], w["norm2_gain"]
    gq, gk = jnp.tile(w["q_norm_gain"], (1, FOX_HEADS)), jnp.tile(w["k_norm_gain"], (1, FOX_HEADS))
    bf = jnp.pad(w["fox_b_f"], ((0, 0), (0, LANES - FOX_HEADS)))
    lb_logits, hg_gain = w["hg_lb_logits"], w["hg_out_gain"]
    gm64, gm64_t = _group_matrix(FOX_W, FOX_DIM)

    meta = jnp.broadcast_to(w["meta_tokens"].astype(F32)[None], (lay.batch, N_META, D_MODEL))
    h0 = jnp.concatenate([jnp.zeros((lay.batch, LEAD, D_MODEL), F32), meta, x,
                          jnp.zeros((lay.batch, lay.lp - LEAD - lay.l_real, D_MODEL), F32)], axis=1).reshape(n, D_MODEL)

    (xn,) = rw(lambda i, h, g: _rms(h, g), [h0], [g1], [(D_MODEL, MXU_DTYPE)], [], name="norm1")
    proj = _matmul(xn, w_main, name="proj_main")
    pff = _matmul(xn, w_ff, name="proj_ff")

    def fox_prep_fn(i, a, b_, v_, f_, gq_, gk_, bf_, m_, mt_):
        q_, k_, logf = _fox_prep(a, b_, f_, gq_, gk_, bf_, m_, mt_, lay.valid(i, tile))
        return q_, k_, v_, logf

    q, k, v, logf = rw(fox_prep_fn, [(proj, 512, C_FQ), (proj, 512, C_FK), (proj, 512, C_FV), pff], [gq, gk, bf, gm64, gm64_t],
                       [(512, MXU_DTYPE), (512, MXU_DTYPE), (512, MXU_DTYPE), (LANES, F32)], [], name="fox_prep")
    cum = _cumsum_rows(logf, lay, reverse=False, name="fox_cum")
    e1, e2, aug_ones = _aug_matrices()
    q_aug, k_aug = rw(lambda i, q_, k_, c_, e1_, e2_, on_: _fox_augment(q_, k_, c_, lay.valid(i, tile), e1_, e2_, on_),
                      [q, k, cum], [e1, e2, aug_ones], [(FOX_HEADS * AUG, MXU_DTYPE)] * 2, [], name="fox_aug")
    o_t, lse = _fox_fwd_t(q_aug, k_aug, v.T, lay)

    def hg_prep_fn(i, hf_, l0, l1):
        lf, kk_ = _hg_prep(hf_, l0, l1)
        return kk_, _group_cumsum(lf, tile, reverse=False)

    lb0, lb1 = lb_logits[0:1], lb_logits[1:2]
    kk, gl = rw(hg_prep_fn, [(proj, 512, C_HF)], [lb0, lb1], [(512, F32), (512, F32)], [], name="hg_prep")
    o_hg, states = _hgrn_fwd(proj, kk, gl, lay)
    (oh,) = rw(lambda i, o, g_, gain: _hg_post(o, g_, gain), [o_hg, (proj, 512, C_HG)], [hg_gain], [(512, MXU_DTYPE)], [],
               name="hg_post")
    ya = _matmul(oh, w_a, name="branch_a")
    yb = _matmul(o_t, w_b, trans_a=True, name="branch_b")
    pga, pgb = (proj, 1024, C_GA), (proj, 1024, C_GB)
    (merged,) = rw(lambda i, a, b_, c_, d_: _gate(a, b_, c_, d_), [pga, pgb, ya, yb], [], [(D_MODEL, MXU_DTYPE)], [], name="gate")
    mo = _matmul(merged, w_out, name="out_proj")
    h1, hn = rw(lambda i, h, m_, g: (h + m_, _rms(h + m_, g)), [h0, mo], [g2], [(D_MODEL, F32), (D_MODEL, MXU_DTYPE)], [],
                name="norm2")
    u = _matmul(hn, w_up, name="up_proj")
    act = _conv_act_fwd(u, conv_w, conv_b, lay)
    mlp = _matmul(act, w_down, name="down_proj")
    loss_blk, dy, dyb = _loss_head(h1, mlp, target.reshape(lay.batch * lay.seq, D_MODEL), lay)
    loss = loss_blk[0, 0]

    grads = {}
    dact = _matmul(dyb, w_down, trans_b=True, out_dtype=MXU_DTYPE, name="down_bwd_x")
    grads["w_down"] = _matmul(act, dyb, trans_a=True, name="down_bwd_w").reshape(N_CHIPS, D_FF // N_CHIPS, D_MODEL)
    dcg, dcv, grads["conv_w"], grads["conv_b"] = _conv_act_bwd(u, dact, conv_w, conv_b, lay)
    du = _conv_input_bwd(dcg, dcv, conv_w, lay)
    dhn = _matmul(du, w_up, trans_b=True, name="up_bwd_x")
    grads["w_up"] = _matmul(hn, du, trans_a=True, by_chip=True, name="up_bwd_w")

    def norm2_bwd(i, h, d_, dy_, g):
        _, vjp = jax.vjp(_rms, h, g)
        dh, dg = vjp(d_)
        return dh + dy_, dh + dy_, dg

    dh1, dh1b, grads["norm2_gain"] = rw(norm2_bwd, [h1, dhn, dy], [g2], [(D_MODEL, F32), (D_MODEL, MXU_DTYPE)], [(1, D_MODEL)],
                                        name="norm2_bwd")
    dmerged = _matmul(dh1b, w_out, trans_b=True, name="out_bwd_x")
    grads["w_out"] = _matmul(merged, dh1b, trans_a=True, name="out_bwd_w").reshape(N_CHIPS, D_MODEL // N_CHIPS, D_MODEL)

    def gate_bwd(i, a, b_, c_, d_, dm):
        _, vjp = jax.vjp(_gate, a, b_, c_, d_)
        return vjp(dm)

    dga, dgb, dya, dyb_ = rw(gate_bwd, [pga, pgb, ya, yb, dmerged], [], [(D_MODEL, MXU_DTYPE)] * 4, [], name="gate_bwd")
    doh = _matmul(dya, w_a, trans_b=True, name="branch_a_bwd_x")
    grads["w_branch_a"] = _matmul(oh, dya, trans_a=True, by_chip=True, name="branch_a_bwd_w")
    dofox = _matmul(dyb_, w_b, trans_b=True, out_dtype=MXU_DTYPE, name="branch_b_bwd_x")
    grads["w_branch_b"] = _matmul(o_t, dyb_, by_chip=True, name="branch_b_bwd_w")

    def hg_post_bwd(i, o, g_, d_, gain):
        _, vjp = jax.vjp(_hg_post, o, g_, gain)
        return vjp(d_)

    do_hg, dhg, grads["hg_out_gain"] = rw(hg_post_bwd, [o_hg, (proj, 512, C_HG), doh], [hg_gain], [(512, F32), (512, MXU_DTYPE)],
                                          [(1, HG_DIM)], name="hg_post_bwd")
    dhq, dkk, dhi, dgl = _hgrn_bwd(proj, kk, gl, do_hg, states, lay)

    def hg_prep_bwd(i, hf_, dkk_, dgl_, l0, l1):
        _, vjp = jax.vjp(_hg_prep, hf_, l0, l1)
        return vjp((_group_cumsum(dgl_, tile, reverse=True), dkk_))

    dhf, g_lb0, g_lb1 = rw(hg_prep_bwd, [(proj, 512, C_HF), dkk, dgl], [lb0, lb1], [(512, MXU_DTYPE)], [(1, HG_W), (1, HG_W)],
                           name="hg_prep_bwd")
    grads["hg_lb_logits"] = jnp.concatenate([g_lb0, g_lb1], axis=0)

    k_t = (k.astype(F32) * FOX_SCALE).astype(MXU_DTYPE).T.reshape(FOX_HEADS, FOX_DIM, n)
    k_t = jnp.concatenate([k_t, jnp.ones((FOX_HEADS, KT_ROWS - FOX_DIM, n), MXU_DTYPE)], axis=1).reshape(FOX_HEADS * KT_ROWS, n)
    dq_t, dk_aug, dv = _fox_bwd_t(q_aug, k_aug, v, dofox, k_t, o_t, dofox.T, lse, lay)
    dq_t = dq_t.reshape(FOX_HEADS, KT_ROWS, n)
    dq = dq_t[:, :FOX_DIM].reshape(FOX_W, n).T
    dk_aug = dk_aug.reshape(n, FOX_HEADS, AUG)
    dk = dk_aug[:, :, :FOX_DIM].reshape(n, FOX_W)
    dcum = jnp.pad(dq_t[:, FOX_DIM].T - dk_aug[:, :, FOX_DIM], ((0, 0), (0, LANES - FOX_HEADS)))
    dlogf = _cumsum_rows(dcum, lay, reverse=True, name="fox_cum_bwd")

    def fox_prep_bwd(i, a, b_, f_, dq_, dk_, dl_, gq_, gk_, bf_, m_, mt_):
        valid = lay.valid(i, tile)
        _, vjp = jax.vjp(lambda a_, b__, f__, gq__, gk__, bf__: _fox_prep(a_, b__, f__, gq__, gk__, bf__, m_, mt_, valid),
                         a, b_, f_, gq_, gk_, bf_)
        return vjp((dq_, dk_, dl_))

    dfq, dfk, dff, g_gq, g_gk, g_bf = rw(
        fox_prep_bwd, [(proj, 512, C_FQ), (proj, 512, C_FK), pff, dq, dk, dlogf], [gq, gk, bf, gm64, gm64_t],
        [(512, MXU_DTYPE), (512, MXU_DTYPE), (LANES, MXU_DTYPE)], [(1, FOX_W), (1, FOX_W), (1, LANES)], name="fox_prep_bwd")
    grads["q_norm_gain"] = g_gq.reshape(FOX_HEADS, FOX_DIM).sum(0, keepdims=True)
    grads["k_norm_gain"] = g_gk.reshape(FOX_HEADS, FOX_DIM).sum(0, keepdims=True)
    grads["fox_b_f"] = g_bf[:, :FOX_HEADS]

    dproj = jnp.concatenate([dga, dgb, dfq, dfk, mx(dv), mx(dhq), dhf, mx(dhi), dhg], axis=1)
    dxn = _matmul(dproj, w_main, trans_b=True, name="proj_bwd_x")
    dxn_ff = _matmul(dff, w_ff, trans_b=True, name="proj_ff_bwd_x")
    g_main = _matmul(xn, dproj, trans_a=True, name="proj_bwd_w")
    g_ff = _matmul(xn, dff, trans_a=True, name="proj_ff_bwd_w")[:, :FOX_HEADS]
    p = jnp.split(g_main, list(np.cumsum([1024, 1024] + [512] * 6)), axis=1)
    grads["w_in"] = _by_chip(jnp.concatenate([p[2], p[3], p[4], g_ff, p[5], p[6], p[7], p[8], p[0], p[1]], axis=1))

    per = lay.lp // tile

    def norm1_bwd(i, h, d1, d2, dh1_, g):
        _, vjp = jax.vjp(_rms, h, g)
        dh, dg = vjp(d1 + d2)
        dh = dh + dh1_
        dmeta = jnp.where(lax.rem(i, per) == 0, dh[LEAD:LEAD + N_META, :], 0.0)
        return dh, dg, dmeta

    dh0, grads["norm1_gain"], grads["meta_tokens"] = rw(norm1_bwd, [h0, dxn, dxn_ff, dh1], [g1], [(D_MODEL, F32)],
                                                       [(1, D_MODEL), (N_META, D_MODEL)], name="norm1_bwd")
    grad_x = dh0.reshape(lay.batch, lay.lp, D_MODEL)[:, ROW0:ROW0 + lay.seq]
    return loss, grad_x, grads


MESH = pl.DeviceIdType.MESH
HBM_SPEC = pl.BlockSpec(memory_space=pltpu.HBM)
WEIGHT_NAMES = ["meta_tokens", "norm1_gain", "w_in", "fox_b_f", "q_norm_gain", "k_norm_gain", "hg_lb_logits", "hg_out_gain",
                "w_branch_a", "w_branch_b", "w_out", "norm2_gain", "w_up", "conv_w", "conv_b", "w_down"]
BIG = ("w_in", "w_branch_a", "w_branch_b", "w_out", "w_up", "w_down")
BIG_COL_SHARDED = ("w_in", "w_branch_a", "w_branch_b", "w_up")
SMALL = tuple(n for n in WEIGHT_NAMES if n not in BIG)
SMALL_SHARDED = ("meta_tokens", "conv_w")
SMALL_ROWS = 144
GATHER_SMALL_ROWS = 80


def _position():
    return lax.axis_index("x"), lax.axis_index("y"), lax.axis_index("c")


def _other_chips(x, y):
    return [(1 - x, y), (x, 1 - y), (1 - x, 1 - y)]


def _scalar(v):
    return jnp.reshape(v, (1,)).astype(jnp.int32)


def _row_tile(rows, cols):
    width = -(-cols // LANES) * LANES * 4
    best = 8
    for d in range(8, rows + 1, 8):
        if rows % d == 0 and d * width <= (1 << 20):
            best = d
    return best


def _gather_shards(shards):
    na = len(shards)
    halves = [s.shape[0] // 2 for s in shards]

    def body(*refs):
        xs, outs, send_sems, recv_sems = refs[:na], refs[na:2 * na], refs[2 * na], refs[2 * na + 1]
        x, y, c = _position()
        sibling = (x, y, 1 - c)
        chips = _other_chips(x, y)

        def copy(a, k, block, to, src=None):
            dst = outs[a].at[4 * block[0] + 2 * block[1] + block[2]]
            return pltpu.make_async_remote_copy(src_ref=dst if src is None else src, dst_ref=dst, send_sem=send_sems.at[a, k],
                                                recv_sem=recv_sems.at[a, k], device_id=to, device_id_type=MESH)

        first = []
        for a in range(na):
            mine = xs[a].at[pl.ds(pl.multiple_of(c * halves[a], 8), halves[a]), :]
            first += [copy(a, j, (x, y, c), (*chip, c), src=mine) for j, chip in enumerate(chips)]
        for cp in first:
            cp.start()
        passed = []
        for j, chip in enumerate(chips):
            for a in range(na):
                copy(a, j, (*chip, c), (x, y, c)).wait_recv()
                cp = copy(a, 3 + j, (*chip, c), sibling)
                cp.start()
                passed.append(cp)
        for a in range(na):
            for j, chip in enumerate(chips):
                copy(a, 3 + j, (*chip, 1 - c), (x, y, c)).wait_recv()
        for cp in first + passed:
            cp.wait_send()

    return pl.pallas_call(
        body, name="gather_weights",
        out_shape=[jax.ShapeDtypeStruct((8, h, s.shape[1]), s.dtype) for h, s in zip(halves, shards)],
        in_specs=[HBM_SPEC] * na, out_specs=[HBM_SPEC] * na,
        scratch_shapes=[pltpu.SemaphoreType.DMA((na, 6)), pltpu.SemaphoreType.DMA((na, 6))],
    )(*shards)


def _sibling_exchange(gs):
    na = len(gs)
    halves = [g.shape[1] // 2 for g in gs]

    def body(*refs):
        srcs, gots, send_sems, recv_sems = refs[:na], refs[na:2 * na], refs[2 * na], refs[2 * na + 1]
        x, y, c = _position()
        copies = [pltpu.make_async_remote_copy(
            src_ref=srcs[a].at[:, pl.ds(pl.multiple_of((1 - c) * halves[a], 8), halves[a]), :], dst_ref=gots[a],
            send_sem=send_sems.at[a], recv_sem=recv_sems.at[a], device_id=(x, y, 1 - c), device_id_type=MESH) for a in range(na)]
        for cp in copies:
            cp.start()
        for cp in copies:
            cp.wait()

    return pl.pallas_call(
        body, name="reduce_sibling",
        out_shape=[jax.ShapeDtypeStruct((N_CHIPS, h, g.shape[2]), g.dtype) for h, g in zip(halves, gs)],
        in_specs=[HBM_SPEC] * na, out_specs=[HBM_SPEC] * na,
        scratch_shapes=[pltpu.SemaphoreType.DMA((na,)), pltpu.SemaphoreType.DMA((na,))],
    )(*gs)


def _chip_exchange(parts):
    na = len(parts)

    def body(*refs):
        srcs, gots, send_sems, recv_sems = refs[:na], refs[na:2 * na], refs[2 * na], refs[2 * na + 1]
        x, y, c = _position()
        mine = 2 * x + y
        chips = _other_chips(x, y)

        def copy(a, j):
            cx, cy = chips[j]
            return pltpu.make_async_remote_copy(src_ref=srcs[a].at[2 * cx + cy], dst_ref=gots[a].at[mine],
                                                send_sem=send_sems.at[a, j], recv_sem=recv_sems.at[a, j],
                                                device_id=(cx, cy, c), device_id_type=MESH)

        def arrival(a, j):
            cx, cy = chips[j]
            return pltpu.make_async_remote_copy(src_ref=srcs[a].at[mine], dst_ref=gots[a].at[2 * cx + cy],
                                                send_sem=send_sems.at[a, j], recv_sem=recv_sems.at[a, j],
                                                device_id=(cx, cy, c), device_id_type=MESH)

        sends = [copy(a, j) for a in range(na) for j in range(3)]
        for cp in sends:
            cp.start()
        for a in range(na):
            for j in range(3):
                arrival(a, j).wait_recv()
        for cp in sends:
            cp.wait_send()

    return pl.pallas_call(
        body, name="reduce_chips", out_shape=[jax.ShapeDtypeStruct(p.shape, p.dtype) for p in parts],
        in_specs=[HBM_SPEC] * na, out_specs=[HBM_SPEC] * na,
        scratch_shapes=[pltpu.SemaphoreType.DMA((na, 3)), pltpu.SemaphoreType.DMA((na, 3))],
    )(*parts)


def _sibling_send(halves):
    na = len(halves)

    def body(*refs):
        srcs, gots, send_sems, recv_sems = refs[:na], refs[na:2 * na], refs[2 * na], refs[2 * na + 1]
        x, y, c = _position()
        copies = [pltpu.make_async_remote_copy(src_ref=srcs[a], dst_ref=gots[a], send_sem=send_sems.at[a], recv_sem=recv_sems.at[a],
                                               device_id=(x, y, 1 - c), device_id_type=MESH) for a in range(na)]
        for cp in copies:
            cp.start()
        for cp in copies:
            cp.wait()

    return pl.pallas_call(
        body, name="reduce_gather", out_shape=[jax.ShapeDtypeStruct(h.shape, h.dtype) for h in halves],
        in_specs=[HBM_SPEC] * na, out_specs=[HBM_SPEC] * na,
        scratch_shapes=[pltpu.SemaphoreType.DMA((na,)), pltpu.SemaphoreType.DMA((na,))],
    )(*halves)


def _add_own_half(g, got, c, name):
    _, r, cols = g.shape
    r2 = r // 2
    tr = _row_tile(r2, cols)
    nrt = r2 // tr

    def body(c_ref, g_ref, got_ref, o_ref):
        o_ref[...] = g_ref[...] + got_ref[...]

    blk = (1, tr, cols)
    return pl.pallas_call(
        body, name=name,
        grid_spec=pltpu.PrefetchScalarGridSpec(
            num_scalar_prefetch=1, grid=(N_CHIPS, nrt),
            in_specs=[pl.BlockSpec(blk, lambda j, i, c_: (j, c_[0] * nrt + i, 0)), pl.BlockSpec(blk, lambda j, i, c_: (j, i, 0))],
            out_specs=pl.BlockSpec(blk, lambda j, i, c_: (j, i, 0))),
        out_shape=jax.ShapeDtypeStruct((N_CHIPS, r2, cols), F32), compiler_params=_params(2),
    )(c, g, got)


def _add_chips(part, got, mine, name):
    _, r2, cols = part.shape
    tr = _row_tile(r2, cols)

    def body(m_ref, p_ref, g0, g1, g2, g3, o_ref):
        t = [jnp.where(m_ref[0] == k, p_ref[0], g[0]) for k, g in enumerate((g0, g1, g2, g3))]
        o_ref[...] = ((t[0] + t[1]) + t[2]) + t[3]

    blk = (1, tr, cols)
    others = [pl.BlockSpec(blk, functools.partial(lambda i, m, k: (jnp.where(m[0] == k, (k + 1) % N_CHIPS, k), i, 0), k=k))
              for k in range(N_CHIPS)]
    return pl.pallas_call(
        body, name=name,
        grid_spec=pltpu.PrefetchScalarGridSpec(
            num_scalar_prefetch=1, grid=(r2 // tr,),
            in_specs=[pl.BlockSpec(blk, lambda i, m: (m[0], i, 0))] + others,
            out_specs=pl.BlockSpec((tr, cols), lambda i, m: (i, 0))),
        out_shape=jax.ShapeDtypeStruct((r2, cols), F32), compiler_params=_params(1),
    )(mine, part, got, got, got, got)


def _adamw(w, own, other, m, v, c, name):
    r, cols = w.shape
    r2 = r // 2
    tr = _row_tile(r2, cols)
    nrt = r2 // tr
    c1 = 1.0 / (1.0 - ADAM_B1 ** ADAM_STEP)
    c2 = 1.0 / (1.0 - ADAM_B2 ** ADAM_STEP)

    def body(c_ref, w_ref, own_ref, other_ref, m_ref, v_ref, g_out, d_out, m_out, v_out):
        g_ = jnp.where(pl.program_id(0) == c_ref[0], own_ref[...], other_ref[...])
        m_new = ADAM_B1 * m_ref[...] + (1.0 - ADAM_B1) * g_
        v_new = ADAM_B2 * v_ref[...] + (1.0 - ADAM_B2) * (g_ * g_)
        g_out[...] = g_
        d_out[...] = -ADAM_LR * ((m_new * c1) / (jnp.sqrt(v_new * c2) + ADAM_EPS) + ADAM_WD * w_ref[...])
        m_out[...] = m_new
        v_out[...] = v_new

    full = pl.BlockSpec((tr, cols), lambda h, i, c_: (h * nrt + i, 0))
    half = pl.BlockSpec((tr, cols), lambda h, i, c_: (i, 0))
    out = jax.ShapeDtypeStruct((r, cols), F32)
    return pl.pallas_call(
        body, name=name,
        grid_spec=pltpu.PrefetchScalarGridSpec(num_scalar_prefetch=1, grid=(2, nrt), in_specs=[full, half, half, full, full],
                                               out_specs=[full] * 4),
        out_shape=[out] * 4, compiler_params=_params(2),
    )(c, w, own, other, m, v)


def _to_rows(flat, rows):
    return jnp.pad(flat, (0, rows * LANES - flat.shape[0])).reshape(rows, LANES)


def _pack_small(tree):
    return _to_rows(jnp.concatenate([tree[n].astype(F32).reshape(-1) for n in SMALL]), SMALL_ROWS)


def _unpack_small(packed, shapes):
    flat, out, at = packed.reshape(-1), {}, 0
    for n in SMALL:
        size = int(np.prod(shapes[n]))
        out[n] = flat[at:at + size].reshape(shapes[n])
        at += size
    return out


def _pack_small_by_chip(grads):
    pieces = []
    for n in SMALL:
        g = grads[n].astype(F32)
        if n in SMALL_SHARDED:
            pieces.append(_by_chip(g).reshape(N_CHIPS, -1))
        else:
            pieces.append(jnp.broadcast_to(g.reshape(1, -1), (N_CHIPS, g.size)))
    flat = jnp.concatenate(pieces, axis=1)
    return jnp.pad(flat, ((0, 0), (0, SMALL_ROWS * LANES - flat.shape[1]))).reshape(N_CHIPS, SMALL_ROWS, LANES)


def _gather_weights(local):
    x, y, _ = _position()
    mine = 2 * x + y
    shards = [local[n].reshape(local[n].shape[-2:]).astype(BF16) for n in BIG]
    shards.append(_to_rows(jnp.concatenate([local[n].astype(F32).reshape(-1) for n in SMALL_SHARDED]), GATHER_SMALL_ROWS))
    got = _gather_shards(shards)
    full = []
    for s, g in zip(shards, got):
        g = g.reshape((N_CHIPS,) + s.shape)
        full.append(lax.dynamic_update_slice(g, s[None], (mine, 0, 0)))
    out = {}
    for n, f in zip(BIG, full):
        out[n] = _from_chips(f) if n in BIG_COL_SHARDED else f.reshape(N_CHIPS * f.shape[1], f.shape[2])
    flat, at = full[-1].reshape(N_CHIPS, -1), 0
    for n in SMALL_SHARDED:
        shape = local[n].shape[-2:]
        size = int(np.prod(shape))
        out[n] = _from_chips(flat[:, at:at + size].reshape((N_CHIPS,) + shape))
        at += size
    return out


def kernel(x, meta_tokens, norm1_gain, w_in, fox_b_f, q_norm_gain, k_norm_gain, hg_lb_logits, hg_out_gain, w_branch_a, w_branch_b, w_out, norm2_gain, w_up, conv_w, conv_b, w_down, loss_target, m_meta_tokens, m_norm1_gain, m_w_in, m_fox_b_f, m_q_norm_gain, m_k_norm_gain, m_hg_lb_logits, m_hg_out_gain, m_w_branch_a, m_w_branch_b, m_w_out, m_norm2_gain, m_w_up, m_conv_w, m_conv_b, m_w_down, v_meta_tokens, v_norm1_gain, v_w_in, v_fox_b_f, v_q_norm_gain, v_k_norm_gain, v_hg_lb_logits, v_hg_out_gain, v_w_branch_a, v_w_branch_b, v_w_out, v_norm2_gain, v_w_up, v_conv_w, v_conv_b, v_w_down):
    w_loc = dict(zip(WEIGHT_NAMES, (meta_tokens, norm1_gain, w_in, fox_b_f, q_norm_gain, k_norm_gain, hg_lb_logits, hg_out_gain,
                                    w_branch_a, w_branch_b, w_out, norm2_gain, w_up, conv_w, conv_b, w_down)))
    m_loc = dict(zip(WEIGHT_NAMES, (m_meta_tokens, m_norm1_gain, m_w_in, m_fox_b_f, m_q_norm_gain, m_k_norm_gain, m_hg_lb_logits,
                                    m_hg_out_gain, m_w_branch_a, m_w_branch_b, m_w_out, m_norm2_gain, m_w_up, m_conv_w, m_conv_b,
                                    m_w_down)))
    v_loc = dict(zip(WEIGHT_NAMES, (v_meta_tokens, v_norm1_gain, v_w_in, v_fox_b_f, v_q_norm_gain, v_k_norm_gain, v_hg_lb_logits,
                                    v_hg_out_gain, v_w_branch_a, v_w_branch_b, v_w_out, v_norm2_gain, v_w_up, v_conv_w, v_conv_b,
                                    v_w_down)))
    local_shapes = {n: tuple(w_loc[n].shape) for n in WEIGHT_NAMES}
    px, py, pc = _position()
    c, mine = _scalar(pc), _scalar(2 * px + py)

    weights = {n: w_loc[n].reshape(w_loc[n].shape[-2:]) for n in SMALL if n not in SMALL_SHARDED}
    weights.update(_gather_weights(w_loc))

    lay = _Layout(x.shape[0], x.shape[1])
    loss, grad_x, grads = _local_step(x, loss_target, weights, lay)
    loss = lax.psum(loss, ("x", "y", "c"))

    names = list(BIG) + ["small"]
    by_chip = [grads[n] for n in BIG] + [_pack_small_by_chip(grads)]
    from_sibling = _sibling_exchange(by_chip)
    parts = [_add_own_half(g, s, c, name=f"reduce_add2_{n}") for n, g, s in zip(names, by_chip, from_sibling)]
    from_chips = _chip_exchange(parts)
    own = [_add_chips(p, g, mine, name=f"reduce_add4_{n}") for n, p, g in zip(names, parts, from_chips)]
    other = _sibling_send(own)

    two_d = lambda t: [t[n].reshape(t[n].shape[-2:]) for n in BIG] + [_pack_small(t)]
    results = [_adamw(w_, o_, t_, m_, v_, c, name=f"adamw_{n}")
               for n, w_, o_, t_, m_, v_ in zip(names, two_d(w_loc), own, other, two_d(m_loc), two_d(v_loc))]
    outs = []
    for kind in range(4):
        tree = {n: results[i][kind].reshape(local_shapes[n]) for i, n in enumerate(BIG)}
        tree.update(_unpack_small(results[-1][kind], local_shapes))
        outs += [tree[n] for n in WEIGHT_NAMES]
    return (loss, grad_x, *outs)
```

```python
import functools

import jax
import jax.numpy as jnp
import numpy as np
from jax import lax
from jax.experimental import pallas as pl
from jax.experimental.pallas import tpu as pltpu

F32 = jnp.float32
BF16 = jnp.bfloat16
MXU_DTYPE = BF16
HIGHEST = lax.Precision.HIGHEST

D_MODEL = 1024
N_META = 16
LEAD = 48
ROW0 = LEAD + N_META
FOX_HEADS, FOX_DIM, FOX_W = 8, 64, 512
HG_HEADS, HG_DIM, HG_W = 4, 128, 512
D_FF = 2816
FF2 = 2 * D_FF
EPS = 1e-6
SUB = 16
LANES = 128
N_CHIPS = 4
NEG = -1e30

ADAM_LR, ADAM_B1, ADAM_B2, ADAM_EPS, ADAM_WD, ADAM_STEP = 0.001, 0.9, 0.999, 1e-08, 0.01, 10

VMEM_LIMIT = 56 * 1024 * 1024

C_GA, C_GB = 0, 1
C_FQ, C_FK, C_FV, C_HQ, C_HF, C_HI, C_HG = 4, 5, 6, 7, 8, 9, 10


def _params(n_axes=1):
    return pltpu.CompilerParams(dimension_semantics=("arbitrary",) * n_axes, vmem_limit_bytes=VMEM_LIMIT)


def _pick(n, cands):
    for c in cands:
        if n % c == 0:
            return c
    raise ValueError(f"no tile for {n} among {cands}")


def _rowwise(fn, rows, consts, outs, reds, *, n_rows, tile, name):
    assert n_rows % tile == 0
    rows = [r if isinstance(r, tuple) else (r, r.shape[1], 0) for r in rows]
    nr, nc, no = len(rows), len(consts), len(outs)

    def body(*refs):
        i = pl.program_id(0)
        ins = [r[...] for r in refs[:nr + nc]]
        res = fn(i, *ins)
        res = res if isinstance(res, (tuple, list)) else (res,)
        for ref, v in zip(refs[nr + nc:nr + nc + no], res[:no]):
            ref[...] = v.astype(ref.dtype)
        red_refs = refs[nr + nc + no:]
        if red_refs:
            @pl.when(i == 0)
            def _():
                for ref in red_refs:
                    ref[...] = jnp.zeros_like(ref)
            for ref, v in zip(red_refs, res[no:]):
                ref[...] += v.astype(F32)

    in_specs = [pl.BlockSpec((tile, w), functools.partial(lambda i, j: (i, j), j=j)) for (_, w, j) in rows]
    in_specs += [pl.BlockSpec(c.shape, functools.partial(lambda i, nd: (0,) * nd, nd=c.ndim)) for c in consts]
    out_specs = [pl.BlockSpec((tile, w), lambda i: (i, 0)) for (w, _) in outs]
    out_specs += [pl.BlockSpec(s, functools.partial(lambda i, nd: (0,) * nd, nd=len(s))) for s in reds]
    out_shape = [jax.ShapeDtypeStruct((n_rows, w), dt) for (w, dt) in outs]
    out_shape += [jax.ShapeDtypeStruct(s, F32) for s in reds]
    return pl.pallas_call(
        body, name=name, grid=(n_rows // tile,), in_specs=in_specs, out_specs=out_specs, out_shape=out_shape,
        compiler_params=_params(1),
    )(*[r[0] for r in rows], *consts)


def _matmul(a, b, *, trans_a=False, trans_b=False, out_dtype=F32, by_chip=False, name):
    if trans_a:
        k, m = a.shape
    else:
        m, k = a.shape
    n = b.shape[0] if trans_b else b.shape[1]
    assert (b.shape[1] if trans_b else b.shape[0]) == k
    if trans_a:
        tm = _pick(m, (1408, 1024, 512, 256, 128))
        tk = _pick(k, (1088, 1024, 768, 512, 256))
    else:
        tm = _pick(m, (1088, 512, 256, 128))
        tk = k if k <= 1024 else _pick(k, (1408, 1024, 512))
    tn = n // N_CHIPS if by_chip else _pick(n, (1408, 1024, 512, 256, 128))
    nk = k // tk
    dims = (((0 if trans_a else 1,), (1 if trans_b else 0,)), ((), ()))

    def body(a_ref, b_ref, o_ref, acc_ref):
        out = o_ref.at[0] if by_chip else o_ref
        part = lax.dot_general(a_ref[...], b_ref[...], dims, preferred_element_type=F32)
        if nk == 1:
            out[...] = part.astype(out.dtype)
        else:
            kk = pl.program_id(2)

            @pl.when(kk == 0)
            def _():
                acc_ref[...] = part

            @pl.when(kk > 0)
            def _():
                acc_ref[...] += part

            @pl.when(kk == nk - 1)
            def _():
                out[...] = acc_ref[...].astype(out.dtype)

    a_spec = pl.BlockSpec((tk, tm), lambda i, j, kk: (kk, i)) if trans_a else pl.BlockSpec((tm, tk), lambda i, j, kk: (i, kk))
    b_spec = pl.BlockSpec((tn, tk), lambda i, j, kk: (j, kk)) if trans_b else pl.BlockSpec((tk, tn), lambda i, j, kk: (kk, j))
    if by_chip:
        out_spec, out_shape = pl.BlockSpec((1, tm, tn), lambda i, j, kk: (j, i, 0)), (N_CHIPS, m, tn)
    else:
        out_spec, out_shape = pl.BlockSpec((tm, tn), lambda i, j, kk: (i, j)), (m, n)
    return pl.pallas_call(
        body, name=name, grid=(m // tm, n // tn, nk), in_specs=[a_spec, b_spec], out_specs=out_spec,
        out_shape=jax.ShapeDtypeStruct(out_shape, out_dtype),
        scratch_shapes=[pltpu.VMEM((tm, tn) if nk > 1 else (8, LANES), F32)],
        compiler_params=_params(3),
    )(a, b)


def _sigmoid(x):
    return 1.0 / (1.0 + jnp.exp(-x))


def _silu(x):
    return x * _sigmoid(x)


def _log_sigmoid(x):
    return jnp.minimum(x, 0.0) - jnp.log(1.0 + jnp.exp(-jnp.abs(x)))


def _rms(x, gain):
    return x * lax.rsqrt(jnp.mean(x * x, axis=-1, keepdims=True) + EPS) * gain


def _group_matrix(width, group):
    g = (np.arange(width)[:, None] // group == np.arange(LANES)[None, :]).astype(np.float32)
    return jnp.asarray(g, MXU_DTYPE), jnp.asarray(g.T.copy(), MXU_DTYPE)


def _split_dot(x, mat):
    dt = mat.dtype
    hi = x.astype(dt)
    r1 = x - hi.astype(F32)
    mid = r1.astype(dt)
    lo = (r1 - mid.astype(F32)).astype(dt)
    dot = lambda a: jnp.dot(a, mat, preferred_element_type=F32)
    return dot(hi) + dot(mid) + dot(lo)


@jax.custom_vjp
def _group_sum(x, gmat, gmat_t):
    return _split_dot(x, gmat)


@jax.custom_vjp
def _group_spread(s, gmat, gmat_t):
    return _split_dot(s, gmat_t)


_group_sum.defvjp(lambda x, g, gt: (_split_dot(x, g), (g, gt)),
                  lambda res, ct: (_group_spread(ct, *res), jnp.zeros_like(res[0]), jnp.zeros_like(res[1])))
_group_spread.defvjp(lambda s, g, gt: (_split_dot(s, gt), (g, gt)),
                     lambda res, ct: (_group_sum(ct, *res), jnp.zeros_like(res[0]), jnp.zeros_like(res[1])))


def _group_rms(x, gain, gmat, gmat_t, group):
    rstd = lax.rsqrt(_group_sum(x * x, gmat, gmat_t) * (1.0 / group) + EPS)
    return x * _group_spread(rstd, gmat, gmat_t) * gain


def _head_rms(x, gain):
    outs = []
    for h in range(x.shape[1] // LANES):
        xs = x[:, h * LANES:(h + 1) * LANES]
        outs.append(xs * lax.rsqrt(jnp.mean(xs * xs, axis=-1, keepdims=True) + EPS) * gain)
    return jnp.concatenate(outs, axis=1)


class _Layout:
    def __init__(self, batch, seq):
        self.batch, self.seq = batch, seq
        self.l_real = N_META + seq
        self.lp = -(-(LEAD + self.l_real) // 256) * 256
        self.n = batch * self.lp
        self.tile = _pick(self.lp, (512, 256))

    def valid(self, i, tile):
        per = self.lp // tile
        r = lax.rem(i, per) * tile + lax.broadcasted_iota(jnp.int32, (tile, 1), 0)
        return (r >= LEAD) & (r < LEAD + self.l_real)


def _cumsum_rows(x, lay, *, reverse, name):
    t = LANES
    nt = lay.lp // t
    c = x.shape[1]

    def body(x_ref, o_ref, carry):
        j = pl.program_id(1)

        @pl.when(j == 0)
        def _():
            carry[...] = jnp.zeros_like(carry)

        r = lax.broadcasted_iota(jnp.int32, (t, t), 0)
        q = lax.broadcasted_iota(jnp.int32, (t, t), 1)
        tri = jnp.where((q >= r) if reverse else (q <= r), 1.0, 0.0).astype(F32)
        xs = x_ref[...]
        out = jnp.dot(tri, xs, precision=HIGHEST, preferred_element_type=F32) + carry[0:1, :]
        o_ref[...] = out
        carry[...] = jnp.broadcast_to(carry[0:1, :] + jnp.sum(xs, axis=0, keepdims=True), carry.shape)

    def idx(b, j):
        return (b * nt + (nt - 1 - j if reverse else j), 0)

    return pl.pallas_call(
        body, name=name, grid=(lay.batch, nt),
        in_specs=[pl.BlockSpec((t, c), idx)], out_specs=pl.BlockSpec((t, c), idx),
        out_shape=jax.ShapeDtypeStruct(x.shape, F32),
        scratch_shapes=[pltpu.VMEM((8, c), F32)],
        compiler_params=_params(2),
    )(x)


def _group_cumsum(x, tile, *, reverse):
    r = lax.rem(lax.broadcasted_iota(jnp.int32, (tile, 1), 0), SUB)
    s = 1
    while s < SUB:
        if reverse:
            x = x + jnp.where(r < SUB - s, pltpu.roll(x, tile - s, 0), 0.0)
        else:
            x = x + jnp.where(r >= s, pltpu.roll(x, s, 0), 0.0)
        s *= 2
    return x


AUG = 128
FOX_BK = 256
FOX_BQ = 256
FOX_SCALE = FOX_DIM ** -0.5
KT_ROWS = FOX_DIM + 16


def _aug_matrices():
    e1 = np.zeros((FOX_W, FOX_HEADS * AUG), np.float32)
    e2 = np.zeros((LANES, FOX_HEADS * AUG), np.float32)
    ones = np.zeros((1, FOX_HEADS * AUG), np.float32)
    for h in range(FOX_HEADS):
        for d in range(FOX_DIM):
            e1[h * FOX_DIM + d, h * AUG + d] = 1.0
        for j in range(3):
            e2[j * FOX_HEADS + h, h * AUG + FOX_DIM + j] = 1.0
            ones[0, h * AUG + FOX_DIM + j] = 1.0
    return jnp.asarray(e1, MXU_DTYPE), jnp.asarray(e2, MXU_DTYPE), jnp.asarray(ones)


def _fox_augment(q, k, cum, key_ok, e1, e2, ones):
    dt = q.dtype
    c = jnp.where(key_ok, -cum, NEG)
    hi = c.astype(dt)
    r1 = c - hi.astype(F32)
    mid = r1.astype(dt)
    lo = (r1 - mid.astype(F32)).astype(dt)
    lane = lax.broadcasted_iota(jnp.int32, c.shape, 1)
    shift = lambda a, by: pltpu.roll(a.astype(F32), by, 1)
    parts = jnp.where(lane < FOX_HEADS, hi.astype(F32),
                      jnp.where(lane < 2 * FOX_HEADS, shift(mid, FOX_HEADS),
                                jnp.where(lane < 3 * FOX_HEADS, shift(lo, 2 * FOX_HEADS), 0.0))).astype(dt)
    qs = (q.astype(F32) * FOX_SCALE).astype(dt)
    q_aug = jnp.dot(qs, e1, preferred_element_type=F32) + ones
    k_aug = jnp.dot(k, e1, preferred_element_type=F32) + jnp.dot(parts, e2, preferred_element_type=F32)
    return q_aug.astype(dt), k_aug.astype(dt)


def _fox_tile(k_blk, q_blk, k0, q0, masked):
    st = lax.dot_general(k_blk, q_blk, (((1,), (1,)), ((), ())), preferred_element_type=F32)
    if masked:
        keys = k0 + lax.broadcasted_iota(jnp.int32, st.shape, 0)
        qs = q0 + lax.broadcasted_iota(jnp.int32, st.shape, 1)
        st = jnp.where(keys <= qs, st, NEG)
    return st


def _fox_fwd_t(q_aug, k_aug, v_t, lay):
    bk, bq = FOX_BK, FOX_BQ
    nq = lay.lp // bq
    pairs = FOX_HEADS // 2

    def body(q_ref, k_ref, vt_ref, ot_ref, lse_ref, zeros_ref):
        heads = [(slice(hh * AUG, (hh + 1) * AUG), slice(hh * FOX_DIM, (hh + 1) * FOX_DIM)) for hh in range(2)]
        zeros_ref[...] = jnp.zeros_like(zeros_ref)

        def q_loop(qb, _):
            q0 = pl.multiple_of(qb * bq, bq)
            q_blks = [q_ref[pl.ds(q0, bq), lanes] for lanes, _ in heads]

            def scores(kb, h):
                k0 = pl.multiple_of(kb * bk, bk)
                return _fox_tile(k_ref[pl.ds(k0, bk), heads[h][0]], q_blks[h], k0, q0, False)

            def consume(kb, h, state, masked):
                m, l, acc, pend, st = state
                k0 = pl.multiple_of(kb * bk, bk)
                if masked:
                    keys = k0 + lax.broadcasted_iota(jnp.int32, st.shape, 0)
                    qs_ = q0 + lax.broadcasted_iota(jnp.int32, st.shape, 1)
                    st = jnp.where(keys <= qs_, st, NEG)
                m_new = jnp.maximum(m, jnp.max(st, axis=0, keepdims=True))
                alpha = jnp.exp(m - m_new)
                p = jnp.exp(st - m_new)
                l = alpha * l + jnp.sum(p, axis=0, keepdims=True)
                acc = alpha * (acc + pend)
                pend = jnp.dot(vt_ref[heads[h][1], pl.ds(k0, bk)], p.astype(vt_ref.dtype), preferred_element_type=F32)
                return m_new, l, acc, pend

            def k_step(kb, states):
                nxt = [scores(kb + 1, h) for h in range(2)]
                return tuple(consume(kb, h, states[h], False) + (nxt[h],) for h in range(2))

            states = tuple((jnp.full((1, bq), NEG, F32), jnp.zeros((1, bq), F32), zeros_ref[...], zeros_ref[...], scores(0, h))
                           for h in range(2))
            states = lax.fori_loop(0, qb, k_step, states)
            qs = q0 + lax.broadcasted_iota(jnp.int32, (1, bq), 1)
            ok = (qs >= LEAD) & (qs < LEAD + lay.l_real)
            for hh in range(2):
                m, l, acc, pend = consume(qb, hh, states[hh], True)
                ot_ref[heads[hh][1], pl.ds(q0, bq)] = jnp.where(ok, (acc + pend) / l, 0.0).astype(ot_ref.dtype)
                lse_ref[hh, :, pl.ds(q0, bq)] = m + jnp.log(l)
            return 0

        lax.fori_loop(0, nq, q_loop, 0)

    aug = pl.BlockSpec((lay.lp, 2 * AUG), lambda b, p: (b, p))
    tr = pl.BlockSpec((2 * FOX_DIM, lay.lp), lambda b, p: (p, b))
    return pl.pallas_call(
        body, name="fox_fwd", grid=(lay.batch, pairs),
        in_specs=[aug, aug, tr],
        out_specs=[tr, pl.BlockSpec((2, 1, lay.lp), lambda b, p: (b * pairs + p, 0, 0))],
        out_shape=[jax.ShapeDtypeStruct((FOX_W, lay.n), MXU_DTYPE),
                   jax.ShapeDtypeStruct((lay.batch * FOX_HEADS, 1, lay.lp), F32)],
        scratch_shapes=[pltpu.VMEM((FOX_DIM, bq), F32)],
        compiler_params=_params(2),
    )(q_aug, k_aug, v_t)


def _fox_bwd_t(q_aug, k_aug, v, do, k_t, o_t, do_t, lse, lay):
    bk, bq = FOX_BK, FOX_BQ
    nq, nk = lay.lp // bq, lay.lp // bk
    pairs = FOX_HEADS // 2

    def body(q_ref, k_ref, v_ref, do_ref, kt_ref, ot_ref, dot_ref, lse_ref, dqt_ref, dk_ref, dv_ref, delta):
        dqt_ref[...] = jnp.zeros_like(dqt_ref)
        dk_ref[...] = jnp.zeros_like(dk_ref)
        dv_ref[...] = jnp.zeros_like(dv_ref)
        heads = [(hh, slice(hh * AUG, (hh + 1) * AUG), slice(hh * FOX_DIM, (hh + 1) * FOX_DIM),
                  slice(hh * KT_ROWS, (hh + 1) * KT_ROWS)) for hh in range(2)]

        def delta_loop(qb, _):
            q0 = pl.multiple_of(qb * bq, bq)
            for hh, _, cols, _ in heads:
                prod = ot_ref[cols, pl.ds(q0, bq)].astype(F32) * dot_ref[cols, pl.ds(q0, bq)].astype(F32)
                delta[hh, :, pl.ds(q0, bq)] = jnp.sum(prod, axis=0, keepdims=True)
            return 0

        lax.fori_loop(0, nq, delta_loop, 0)

        def k_loop(kb, _):
            k0 = pl.multiple_of(kb * bk, bk)

            def products(qb, h):
                q0 = pl.multiple_of(qb * bq, bq)
                _, lanes, cols, _ = heads[h]
                st = _fox_tile(k_ref[pl.ds(k0, bk), lanes], q_ref[pl.ds(q0, bq), lanes], k0, q0, False)
                dpt = lax.dot_general(v_ref[pl.ds(k0, bk), cols], do_ref[pl.ds(q0, bq), cols], (((1,), (1,)), ((), ())),
                                      preferred_element_type=F32)
                return st, dpt

            def consume(qb, h, st, dpt, masked):
                q0 = pl.multiple_of(qb * bq, bq)
                hh, lanes, cols, trows = heads[h]
                if masked:
                    keys = k0 + lax.broadcasted_iota(jnp.int32, st.shape, 0)
                    qs = q0 + lax.broadcasted_iota(jnp.int32, st.shape, 1)
                    st = jnp.where(keys <= qs, st, NEG)
                q_blk = q_ref[pl.ds(q0, bq), lanes]
                do_blk = do_ref[pl.ds(q0, bq), cols]
                pt = jnp.exp(st - lse_ref[hh, :, pl.ds(q0, bq)])
                dst = (pt * (dpt - delta[hh, :, pl.ds(q0, bq)])).astype(q_blk.dtype)
                dv_ref[pl.ds(k0, bk), cols] += jnp.dot(pt.astype(do_blk.dtype), do_blk, preferred_element_type=F32)
                dk_ref[pl.ds(k0, bk), lanes] += jnp.dot(dst, q_blk, preferred_element_type=F32)
                dqt_ref[trows, pl.ds(q0, bq)] += jnp.dot(kt_ref[trows, pl.ds(k0, bk)], dst, preferred_element_type=F32)

            after = lambda qb: jnp.minimum(qb + 1, nq - 1)
            cur = [products(kb, h) for h in range(2)]
            nxt = tuple(products(after(kb), h) for h in range(2))
            for h in range(2):
                consume(kb, h, *cur[h], True)

            def rest(qb, held):
                new = tuple(products(after(qb), h) for h in range(2))
                for h in range(2):
                    consume(qb, h, *held[h], False)
                return new

            lax.fori_loop(kb + 1, nq, rest, nxt)
            return 0

        lax.fori_loop(0, nk, k_loop, 0)

    aug = pl.BlockSpec((lay.lp, 2 * AUG), lambda b, p: (b, p))
    rows = pl.BlockSpec((lay.lp, 2 * FOX_DIM), lambda b, p: (b, p))
    tr = pl.BlockSpec((2 * FOX_DIM, lay.lp), lambda b, p: (p, b))
    tr_k = pl.BlockSpec((2 * KT_ROWS, lay.lp), lambda b, p: (p, b))
    return pl.pallas_call(
        body, name="fox_bwd", grid=(lay.batch, pairs),
        in_specs=[aug, aug, rows, rows, tr_k, tr, tr, pl.BlockSpec((2, 1, lay.lp), lambda b, p: (b * pairs + p, 0, 0))],
        out_specs=[tr_k, aug, rows],
        out_shape=[jax.ShapeDtypeStruct((FOX_HEADS * KT_ROWS, lay.n), F32), jax.ShapeDtypeStruct((lay.n, FOX_HEADS * AUG), F32),
                   jax.ShapeDtypeStruct((lay.n, FOX_W), F32)],
        scratch_shapes=[pltpu.VMEM((2, 1, lay.lp), F32)],
        compiler_params=_params(2),
    )(q_aug, k_aug, v, do, k_t, o_t, do_t, lse)


def _hgrn_fwd(proj, kk, gl, lay):
    t = lay.tile
    nt = lay.lp // t
    nsc = t // SUB

    def body(q_ref, k_ref, g_ref, v_ref, o_ref, st_ref, state, sub_rows):
        @pl.when(pl.program_id(1) == 0)
        def _():
            state[...] = jnp.zeros_like(state)

        rowi = lax.broadcasted_iota(jnp.int32, (SUB, 1), 0)

        def sub(sc, _):
            r0 = pl.multiple_of(sc * SUB, SUB)
            sub_rows[0] = k_ref[pl.ds(r0, SUB), :]
            sub_rows[1] = g_ref[pl.ds(r0, SUB), :]
            sub_rows[2] = v_ref[pl.ds(r0, SUB), :]
            for h in range(HG_HEADS):
                lanes = slice(h * HG_DIM, (h + 1) * HG_DIM)
                q16 = q_ref[pl.ds(r0, SUB), lanes]
                k16 = sub_rows[0, :, lanes]
                g16 = sub_rows[1, :, lanes]
                v16 = sub_rows[2, :, lanes]
                g_end = sub_rows[1, SUB - 1:SUB, lanes]
                s_prev = state[h]
                st_ref[sc, h] = s_prev
                o = lax.dot_general((q16 * jnp.exp(g16)).astype(MXU_DTYPE), s_prev.astype(MXU_DTYPE),
                                    (((1,), (1,)), ((), ())), preferred_element_type=F32)
                for s in range(SUB):
                    ks = sub_rows[0, s:s + 1, lanes]
                    gs = sub_rows[1, s:s + 1, lanes]
                    vs = sub_rows[2, s:s + 1, lanes]
                    w = q16 * jnp.exp(jnp.minimum(g16 - gs, 0.0)) * ks
                    a = jnp.where(rowi >= s, jnp.sum(w, axis=1, keepdims=True), 0.0)
                    o = o + a * vs
                o_ref[pl.ds(r0, SUB), lanes] = o
                kt = k16 * jnp.exp(g_end - g16)
                upd = lax.dot_general(v16.astype(MXU_DTYPE), kt.astype(MXU_DTYPE), (((0,), (0,)), ((), ())),
                                      preferred_element_type=F32)
                state[h] = jnp.exp(g_end) * s_prev + upd
            return 0

        lax.fori_loop(0, nsc, sub, 0)

    rows = lambda col: pl.BlockSpec((t, HG_W), functools.partial(lambda b, i, col: (b * nt + i, col), col=col))
    return pl.pallas_call(
        body, name="hgrn_fwd", grid=(lay.batch, nt),
        in_specs=[rows(C_HQ), rows(0), rows(0), rows(C_HI)],
        out_specs=[rows(0), pl.BlockSpec((nsc, HG_HEADS, HG_DIM, HG_DIM), lambda b, i: (b * nt + i, 0, 0, 0))],
        out_shape=[jax.ShapeDtypeStruct((lay.n, HG_W), F32),
                   jax.ShapeDtypeStruct((lay.n // SUB, HG_HEADS, HG_DIM, HG_DIM), F32)],
        scratch_shapes=[pltpu.VMEM((HG_HEADS, HG_DIM, HG_DIM), F32), pltpu.VMEM((3, SUB, HG_W), F32)],
        compiler_params=_params(2),
    )(proj, kk, gl, proj)


def _hgrn_bwd(proj, kk, gl, do, states, lay):
    t = lay.tile
    nt = lay.lp // t
    nsc = t // SUB

    def body(q_ref, k_ref, g_ref, v_ref, do_ref, st_ref, dq_ref, dk_ref, dv_ref, dg_ref, dstate, sub_rows):
        @pl.when(pl.program_id(1) == 0)
        def _():
            dstate[...] = jnp.zeros_like(dstate)

        rowi = lax.broadcasted_iota(jnp.int32, (SUB, 1), 0)

        def sub(it, _):
            sc = nsc - 1 - it
            r0 = pl.multiple_of(sc * SUB, SUB)
            sub_rows[0] = k_ref[pl.ds(r0, SUB), :]
            sub_rows[1] = g_ref[pl.ds(r0, SUB), :]
            sub_rows[2] = v_ref[pl.ds(r0, SUB), :]
            for h in range(HG_HEADS):
                lanes = slice(h * HG_DIM, (h + 1) * HG_DIM)
                q16 = q_ref[pl.ds(r0, SUB), lanes]
                k16 = sub_rows[0, :, lanes]
                g16 = sub_rows[1, :, lanes]
                v16 = sub_rows[2, :, lanes]
                do16 = do_ref[pl.ds(r0, SUB), lanes]
                g_end = sub_rows[1, SUB - 1:SUB, lanes]
                s_prev = st_ref[sc, h]
                ds_end = dstate[h]
                eg = jnp.exp(g16)
                ekt = jnp.exp(g_end - g16)
                e_end = jnp.exp(g_end)
                qt = q16 * eg
                kt = k16 * ekt
                ds_mx = ds_end.astype(MXU_DTYPE)
                dv = lax.dot_general(kt.astype(MXU_DTYPE), ds_mx, (((1,), (1,)), ((), ())), preferred_element_type=F32)
                dkt = jnp.dot(v16.astype(MXU_DTYPE), ds_mx, preferred_element_type=F32)
                dk = dkt * ekt
                ktdkt = kt * dkt
                dg_end = jnp.sum(ktdkt, axis=0, keepdims=True) + jnp.sum(s_prev * ds_end, axis=0, keepdims=True) * e_end
                dg = jnp.where(rowi == SUB - 1, dg_end, 0.0) - ktdkt
                dqt = jnp.dot(do16.astype(MXU_DTYPE), s_prev.astype(MXU_DTYPE), preferred_element_type=F32)
                dq = dqt * eg
                dg = dg + qt * dqt
                dstate[h] = e_end * ds_end + lax.dot_general(do16.astype(MXU_DTYPE), qt.astype(MXU_DTYPE),
                                                             (((0,), (0,)), ((), ())), preferred_element_type=F32)
                for s in range(SUB):
                    ks = sub_rows[0, s:s + 1, lanes]
                    gs = sub_rows[1, s:s + 1, lanes]
                    vs = sub_rows[2, s:s + 1, lanes]
                    live = rowi >= s
                    e = jnp.where(live, jnp.exp(jnp.minimum(g16 - gs, 0.0)), 0.0)
                    qe = q16 * e
                    a = jnp.sum(qe * ks, axis=1, keepdims=True)
                    da = jnp.where(live, jnp.sum(do16 * vs, axis=1, keepdims=True), 0.0)
                    dv_row = jnp.sum(a * do16, axis=0, keepdims=True)
                    t1 = da * qe
                    dk_row = jnp.sum(t1, axis=0, keepdims=True)
                    dq = dq + da * (e * ks)
                    is_s = rowi == s
                    dv = dv + jnp.where(is_s, dv_row, 0.0)
                    dk = dk + jnp.where(is_s, dk_row, 0.0)
                    dg = dg + t1 * ks - jnp.where(is_s, ks * dk_row, 0.0)
                dq_ref[pl.ds(r0, SUB), lanes] = dq
                dk_ref[pl.ds(r0, SUB), lanes] = dk
                dv_ref[pl.ds(r0, SUB), lanes] = dv
                dg_ref[pl.ds(r0, SUB), lanes] = dg
            return 0

        lax.fori_loop(0, nsc, sub, 0)

    def rows(col):
        return pl.BlockSpec((t, HG_W), functools.partial(lambda b, i, col: (b * nt + nt - 1 - i, col), col=col))

    out = jax.ShapeDtypeStruct((lay.n, HG_W), F32)
    return pl.pallas_call(
        body, name="hgrn_bwd", grid=(lay.batch, nt),
        in_specs=[rows(C_HQ), rows(0), rows(0), rows(C_HI), rows(0),
                  pl.BlockSpec((nsc, HG_HEADS, HG_DIM, HG_DIM), lambda b, i: (b * nt + nt - 1 - i, 0, 0, 0))],
        out_specs=[rows(0)] * 4, out_shape=[out] * 4,
        scratch_shapes=[pltpu.VMEM((HG_HEADS, HG_DIM, HG_DIM), F32), pltpu.VMEM((3, SUB, HG_W), F32)],
        compiler_params=_params(2),
    )(proj, kk, gl, proj, do, states)


CONV_COLS = 1408


def _shift_down(x, halo, tile, by):
    out = pltpu.roll(x, by, 0)
    rowi = lax.broadcasted_iota(jnp.int32, (tile, 1), 0)
    for r in range(by):
        out = jnp.where(rowi == r, halo[8 - by + r:8 - by + r + 1, :], out)
    return out


def _shift_up(x, halo, tile, by):
    out = pltpu.roll(x, tile - by, 0)
    rowi = lax.broadcasted_iota(jnp.int32, (tile, 1), 0)
    for r in range(by):
        out = jnp.where(rowi == tile - by + r, halo[r:r + 1, :], out)
    return out


def _conv_specs(tile):
    ncb = D_FF // CONV_COLS
    per8 = tile // 8

    def tile_spec(off):
        return pl.BlockSpec((tile, CONV_COLS), functools.partial(lambda i, j, off: (i, j + off), off=off))

    def prev_spec(off):
        return pl.BlockSpec((8, CONV_COLS), functools.partial(lambda i, j, off: (jnp.maximum(i * per8 - 1, 0), j + off), off=off))

    def w_spec(off):
        return pl.BlockSpec((3, CONV_COLS), functools.partial(lambda i, j, off: (0, j + off), off=off))

    def b_spec(off):
        return pl.BlockSpec((1, CONV_COLS), functools.partial(lambda i, j, off: (0, j + off), off=off))

    return ncb, tile_spec, prev_spec, w_spec, b_spec


def _conv3(x, halo, w, b, tile):
    return w[0:1, :] * _shift_down(x, halo, tile, 2) + w[1:2, :] * _shift_down(x, halo, tile, 1) + w[2:3, :] * x + b


def _conv_act_fwd(u, conv_w, conv_b, lay):
    tile = lay.tile
    ncb, tile_spec, prev_spec, w_spec, b_spec = _conv_specs(tile)

    def body(ug, uv, pg, pv, wg, wv, bg, bv, o_ref):
        cg = _conv3(ug[...], pg, wg, bg[...], tile)
        cv = _conv3(uv[...], pv, wv, bv[...], tile)
        o_ref[...] = (_silu(cg) * cv).astype(o_ref.dtype)

    return pl.pallas_call(
        body, name="conv_act_fwd", grid=(lay.n // tile, ncb),
        in_specs=[tile_spec(0), tile_spec(ncb), prev_spec(0), prev_spec(ncb), w_spec(0), w_spec(ncb), b_spec(0), b_spec(ncb)],
        out_specs=pl.BlockSpec((tile, CONV_COLS), lambda i, j: (i, j)),
        out_shape=jax.ShapeDtypeStruct((lay.n, D_FF), MXU_DTYPE),
        compiler_params=_params(2),
    )(u, u, u, u, conv_w, conv_w, conv_b, conv_b)


def _conv_act_bwd(u, dact, conv_w, conv_b, lay):
    tile = lay.tile
    ncb, tile_spec, prev_spec, w_spec, b_spec = _conv_specs(tile)

    def body(ug, uv, pg, pv, wg, wv, bg, bv, da_ref, dg_ref, dv_ref, gwg, gwv, gbg, gbv):
        @pl.when(pl.program_id(1) == 0)
        def _():
            for r in (gwg, gwv, gbg, gbv):
                r[...] = jnp.zeros_like(r)

        xg, xv = ug[...], uv[...]
        cg = _conv3(xg, pg, wg, bg[...], tile)
        cv = _conv3(xv, pv, wv, bv[...], tile)
        da = da_ref[...].astype(F32)
        sg = _sigmoid(cg)
        dcv = da * (cg * sg)
        dcg = da * cv * (sg * (1.0 + cg * (1.0 - sg)))
        dg_ref[...] = dcg
        dv_ref[...] = dcv
        for x, halo, dc, gw, gb in ((xg, pg, dcg, gwg, gbg), (xv, pv, dcv, gwv, gbv)):
            gw[0, 0:1, :] += jnp.sum(dc * _shift_down(x, halo, tile, 2), axis=0, keepdims=True)
            gw[0, 1:2, :] += jnp.sum(dc * _shift_down(x, halo, tile, 1), axis=0, keepdims=True)
            gw[0, 2:3, :] += jnp.sum(dc * x, axis=0, keepdims=True)
            gb[0] += jnp.sum(dc, axis=0, keepdims=True)

    swap = lambda spec: pl.BlockSpec(spec.block_shape, functools.partial(lambda j, i, f: f(i, j), f=spec.index_map))
    col = lambda j, i: (i, j)
    red_w = pl.BlockSpec((1, 3, CONV_COLS), lambda j, i: (j, 0, 0))
    red_b = pl.BlockSpec((1, 1, CONV_COLS), lambda j, i: (j, 0, 0))
    outs = pl.pallas_call(
        body, name="conv_act_bwd", grid=(ncb, lay.n // tile),
        in_specs=[swap(s) for s in (tile_spec(0), tile_spec(ncb), prev_spec(0), prev_spec(ncb), w_spec(0), w_spec(ncb),
                                    b_spec(0), b_spec(ncb))] + [pl.BlockSpec((tile, CONV_COLS), col)],
        out_specs=[pl.BlockSpec((tile, CONV_COLS), col), pl.BlockSpec((tile, CONV_COLS), col), red_w, red_w, red_b, red_b],
        out_shape=[jax.ShapeDtypeStruct((lay.n, D_FF), F32), jax.ShapeDtypeStruct((lay.n, D_FF), F32),
                   jax.ShapeDtypeStruct((ncb, 3, CONV_COLS), F32), jax.ShapeDtypeStruct((ncb, 3, CONV_COLS), F32),
                   jax.ShapeDtypeStruct((ncb, 1, CONV_COLS), F32), jax.ShapeDtypeStruct((ncb, 1, CONV_COLS), F32)],
        compiler_params=_params(2),
    )(u, u, u, u, conv_w, conv_w, conv_b, conv_b, dact)
    dcg, dcv, gwg, gwv, gbg, gbv = outs
    unblock = lambda g: jnp.transpose(g, (1, 0, 2)).reshape(g.shape[1], D_FF)
    g_w = jnp.concatenate([unblock(gwg), unblock(gwv)], axis=1)
    g_b = jnp.concatenate([unblock(gbg), unblock(gbv)], axis=1)
    return dcg, dcv, g_w, g_b


def _conv_input_bwd(dcg, dcv, conv_w, lay):
    tile = lay.tile
    ncb = D_FF // CONV_COLS
    nblk8 = lay.n // 8
    per8 = tile // 8
    nxt = lambda i: jnp.minimum((i + 1) * per8, nblk8 - 1)

    def half(dc, off, into, name):
        def body(*refs):
            d, halo, w, o = refs[0], refs[1], refs[2], refs[-1]
            x = d[...]
            du = w[2:3, :] * x + w[1:2, :] * _shift_up(x, halo, tile, 1) + w[0:1, :] * _shift_up(x, halo, tile, 2)
            o[...] = jnp.where(lay.valid(pl.program_id(0), tile), du, 0.0).astype(o.dtype)

        in_specs = [pl.BlockSpec((tile, CONV_COLS), lambda i, j: (i, j)),
                    pl.BlockSpec((8, CONV_COLS), lambda i, j: (nxt(i), j)),
                    pl.BlockSpec((3, CONV_COLS), lambda i, j: (0, j + off))]
        args = [dc, dc, conv_w]
        if into is not None:
            in_specs.append(pl.BlockSpec(memory_space=pltpu.HBM))
            args.append(into)
        return pl.pallas_call(
            body, name=name, grid=(lay.n // tile, ncb), in_specs=in_specs,
            out_specs=pl.BlockSpec((tile, CONV_COLS), lambda i, j: (i, j + off)),
            out_shape=jax.ShapeDtypeStruct((lay.n, FF2), MXU_DTYPE),
            input_output_aliases={} if into is None else {3: 0},
            compiler_params=_params(2),
        )(*args)

    return half(dcv, ncb, half(dcg, 0, None, "conv_input_bwd_gate"), "conv_input_bwd_value")


def _loss_head(h1, mlp, target, lay):
    t, sub = 256, ROW0
    per = lay.lp // t
    nsub = t // sub
    nreal = lay.seq // sub

    def body(h_ref, m_ref, *rest):
        t_refs, (loss_ref, dy_ref, dyb_ref) = rest[:nsub], rest[nsub:]
        b, j = pl.program_id(0), pl.program_id(1)

        @pl.when((b == 0) & (j == 0))
        def _():
            loss_ref[...] = jnp.zeros_like(loss_ref)

        rows_ = j * t + lax.broadcasted_iota(jnp.int32, (t, 1), 0)
        real = (rows_ >= ROW0) & (rows_ < ROW0 + lay.seq)
        tgt_ = jnp.concatenate([r[...] for r in t_refs], axis=0)
        err = jnp.where(real, h_ref[...] + m_ref[...] - tgt_, 0.0)
        dy = err * (1.0 / D_MODEL)
        dy_ref[...] = dy
        dyb_ref[...] = dy.astype(dyb_ref.dtype)
        loss_ref[...] += 0.5 * jnp.sum(err * dy)

    rows = pl.BlockSpec((t, D_MODEL), lambda b, j: (b * per + j, 0))
    tgt = [pl.BlockSpec((sub, D_MODEL), functools.partial(
        lambda b, j, r: (b * nreal + jnp.clip(j * nsub + r - 1, 0, nreal - 1), 0), r=r)) for r in range(nsub)]
    return pl.pallas_call(
        body, name="loss_head", grid=(lay.batch, per),
        in_specs=[rows, rows] + tgt,
        out_specs=[pl.BlockSpec((8, LANES), lambda b, j: (0, 0)), rows, rows],
        out_shape=[jax.ShapeDtypeStruct((8, LANES), F32), jax.ShapeDtypeStruct((lay.n, D_MODEL), F32),
                   jax.ShapeDtypeStruct((lay.n, D_MODEL), MXU_DTYPE)],
        compiler_params=_params(2),
    )(h1, mlp, *([target] * nsub))


def _fox_prep(fq, fk, ff, gq, gk, bf, gmat, gmat_t, valid):
    q = _group_rms(fq, gq, gmat, gmat_t, FOX_DIM)
    k = _group_rms(fk, gk, gmat, gmat_t, FOX_DIM)
    logf = jnp.where(valid, _log_sigmoid(ff + bf), 0.0)
    return q, k, logf


def _hg_prep(hf, l0, l1):
    mx = jnp.maximum(l0, l1)
    e0, e1 = jnp.exp(l0 - mx), jnp.exp(l1 - mx)
    lb = e0 / (e0 + e1)
    lf = jnp.log(lb + (1.0 - lb) * _sigmoid(hf))
    kk = (1.0 - lb) * _sigmoid(-hf)
    return lf, kk


def _hg_post(o, hg, gain):
    return _head_rms(o, gain) * _silu(hg)


def _gate(ga, gb, ya, yb):
    return _sigmoid(ga) * ya + _sigmoid(gb) * yb


def _by_chip(g):
    return jnp.transpose(g.reshape(g.shape[0], N_CHIPS, g.shape[1] // N_CHIPS), (1, 0, 2))


def _from_chips(a):
    return jnp.transpose(a, (1, 0, 2)).reshape(a.shape[1], N_CHIPS * a.shape[2])


def _local_step(x, target, w, lay):
    n, tile = lay.n, lay.tile
    rw = functools.partial(_rowwise, n_rows=n, tile=tile)
    mx = lambda a: a.astype(MXU_DTYPE)

    w_in = w["w_in"]
    fq, fk, fv, ffw, hq, hf, hi, hg, ga, gb = jnp.split(w_in, list(np.cumsum([512, 512, 512, 8, 512, 512, 512, 512, 1024])), axis=1)
    w_main = mx(jnp.concatenate([ga, gb, fq, fk, fv, hq, hf, hi, hg], axis=1))
    w_ff = mx(jnp.pad(ffw, ((0, 0), (0, LANES - FOX_HEADS))))
    w_a, w_b, w_out, w_up, w_down = mx(w["w_branch_a"]), mx(w["w_branch_b"]), mx(w["w_out"]), mx(w["w_up"]), mx(w["w_down"])
    conv_w, conv_b = w["conv_w"].astype(F32), w["conv_b"].astype(F32)
    g1, g2 = w["norm1_gain"], w["norm2_gain"]
    gq, gk = jnp.tile(w["q_norm_gain"], (1, FOX_HEADS)), jnp.tile(w["k_norm_gain"], (1, FOX_HEADS))
    bf = jnp.pad(w["fox_b_f"], ((0, 0), (0, LANES - FOX_HEADS)))
    lb_logits, hg_gain = w["hg_lb_logits"], w["hg_out_gain"]
    gm64, gm64_t = _group_matrix(FOX_W, FOX_DIM)

    meta = jnp.broadcast_to(w["meta_tokens"].astype(F32)[None], (lay.batch, N_META, D_MODEL))
    h0 = jnp.concatenate([jnp.zeros((lay.batch, LEAD, D_MODEL), F32), meta, x,
                          jnp.zeros((lay.batch, lay.lp - LEAD - lay.l_real, D_MODEL), F32)], axis=1).reshape(n, D_MODEL)

    (xn,) = rw(lambda i, h, g: _rms(h, g), [h0], [g1], [(D_MODEL, MXU_DTYPE)], [], name="norm1")
    proj = _matmul(xn, w_main, name="proj_main")
    pff = _matmul(xn, w_ff, name="proj_ff")

    def fox_prep_fn(i, a, b_, v_, f_, gq_, gk_, bf_, m_, mt_):
        q_, k_, logf = _fox_prep(a, b_, f_, gq_, gk_, bf_, m_, mt_, lay.valid(i, tile))
        return q_, k_, v_, logf

    q, k, v, logf = rw(fox_prep_fn, [(proj, 512, C_FQ), (proj, 512, C_FK), (proj, 512, C_FV), pff], [gq, gk, bf, gm64, gm64_t],
                       [(512, MXU_DTYPE), (512, MXU_DTYPE), (512, MXU_DTYPE), (LANES, F32)], [], name="fox_prep")
    cum = _cumsum_rows(logf, lay, reverse=False, name="fox_cum")
    e1, e2, aug_ones = _aug_matrices()
    q_aug, k_aug = rw(lambda i, q_, k_, c_, e1_, e2_, on_: _fox_augment(q_, k_, c_, lay.valid(i, tile), e1_, e2_, on_),
                      [q, k, cum], [e1, e2, aug_ones], [(FOX_HEADS * AUG, MXU_DTYPE)] * 2, [], name="fox_aug")
    o_t, lse = _fox_fwd_t(q_aug, k_aug, v.T, lay)

    def hg_prep_fn(i, hf_, l0, l1):
        lf, kk_ = _hg_prep(hf_, l0, l1)
        return kk_, _group_cumsum(lf, tile, reverse=False)

    lb0, lb1 = lb_logits[0:1], lb_logits[1:2]
    kk, gl = rw(hg_prep_fn, [(proj, 512, C_HF)], [lb0, lb1], [(512, F32), (512, F32)], [], name="hg_prep")
    o_hg, states = _hgrn_fwd(proj, kk, gl, lay)
    (oh,) = rw(lambda i, o, g_, gain: _hg_post(o, g_, gain), [o_hg, (proj, 512, C_HG)], [hg_gain], [(512, MXU_DTYPE)], [],
               name="hg_post")
    ya = _matmul(oh, w_a, name="branch_a")
    yb = _matmul(o_t, w_b, trans_a=True, name="branch_b")
    pga, pgb = (proj, 1024, C_GA), (proj, 1024, C_GB)
    (merged,) = rw(lambda i, a, b_, c_, d_: _gate(a, b_, c_, d_), [pga, pgb, ya, yb], [], [(D_MODEL, MXU_DTYPE)], [], name="gate")
    mo = _matmul(merged, w_out, name="out_proj")
    h1, hn = rw(lambda i, h, m_, g: (h + m_, _rms(h + m_, g)), [h0, mo], [g2], [(D_MODEL, F32), (D_MODEL, MXU_DTYPE)], [],
                name="norm2")
    u = _matmul(hn, w_up, name="up_proj")
    act = _conv_act_fwd(u, conv_w, conv_b, lay)
    mlp = _matmul(act, w_down, name="down_proj")
    loss_blk, dy, dyb = _loss_head(h1, mlp, target.reshape(lay.batch * lay.seq, D_MODEL), lay)
    loss = loss_blk[0, 0]

    grads = {}
    dact = _matmul(dyb, w_down, trans_b=True, out_dtype=MXU_DTYPE, name="down_bwd_x")
    grads["w_down"] = _matmul(act, dyb, trans_a=True, name="down_bwd_w").reshape(N_CHIPS, D_FF // N_CHIPS, D_MODEL)
    dcg, dcv, grads["conv_w"], grads["conv_b"] = _conv_act_bwd(u, dact, conv_w, conv_b, lay)
    du = _conv_input_bwd(dcg, dcv, conv_w, lay)
    dhn = _matmul(du, w_up, trans_b=True, name="up_bwd_x")
    grads["w_up"] = _matmul(hn, du, trans_a=True, by_chip=True, name="up_bwd_w")

    def norm2_bwd(i, h, d_, dy_, g):
        _, vjp = jax.vjp(_rms, h, g)
        dh, dg = vjp(d_)
        return dh + dy_, dh + dy_, dg

    dh1, dh1b, grads["norm2_gain"] = rw(norm2_bwd, [h1, dhn, dy], [g2], [(D_MODEL, F32), (D_MODEL, MXU_DTYPE)], [(1, D_MODEL)],
                                        name="norm2_bwd")
    dmerged = _matmul(dh1b, w_out, trans_b=True, name="out_bwd_x")
    grads["w_out"] = _matmul(merged, dh1b, trans_a=True, name="out_bwd_w").reshape(N_CHIPS, D_MODEL // N_CHIPS, D_MODEL)

    def gate_bwd(i, a, b_, c_, d_, dm):
        _, vjp = jax.vjp(_gate, a, b_, c_, d_)
        return vjp(dm)

    dga, dgb, dya, dyb_ = rw(gate_bwd, [pga, pgb, ya, yb, dmerged], [], [(D_MODEL, MXU_DTYPE)] * 4, [], name="gate_bwd")
    doh = _matmul(dya, w_a, trans_b=True, name="branch_a_bwd_x")
    grads["w_branch_a"] = _matmul(oh, dya, trans_a=True, by_chip=True, name="branch_a_bwd_w")
    dofox = _matmul(dyb_, w_b, trans_b=True, out_dtype=MXU_DTYPE, name="branch_b_bwd_x")
    grads["w_branch_b"] = _matmul(o_t, dyb_, by_chip=True, name="branch_b_bwd_w")

    def hg_post_bwd(i, o, g_, d_, gain):
        _, vjp = jax.vjp(_hg_post, o, g_, gain)
        return vjp(d_)

    do_hg, dhg, grads["hg_out_gain"] = rw(hg_post_bwd, [o_hg, (proj, 512, C_HG), doh], [hg_gain], [(512, F32), (512, MXU_DTYPE)],
                                          [(1, HG_DIM)], name="hg_post_bwd")
    dhq, dkk, dhi, dgl = _hgrn_bwd(proj, kk, gl, do_hg, states, lay)

    def hg_prep_bwd(i, hf_, dkk_, dgl_, l0, l1):
        _, vjp = jax.vjp(_hg_prep, hf_, l0, l1)
        return vjp((_group_cumsum(dgl_, tile, reverse=True), dkk_))

    dhf, g_lb0, g_lb1 = rw(hg_prep_bwd, [(proj, 512, C_HF), dkk, dgl], [lb0, lb1], [(512, MXU_DTYPE)], [(1, HG_W), (1, HG_W)],
                           name="hg_prep_bwd")
    grads["hg_lb_logits"] = jnp.concatenate([g_lb0, g_lb1], axis=0)

    k_t = (k.astype(F32) * FOX_SCALE).astype(MXU_DTYPE).T.reshape(FOX_HEADS, FOX_DIM, n)
    k_t = jnp.concatenate([k_t, jnp.ones((FOX_HEADS, KT_ROWS - FOX_DIM, n), MXU_DTYPE)], axis=1).reshape(FOX_HEADS * KT_ROWS, n)
    dq_t, dk_aug, dv = _fox_bwd_t(q_aug, k_aug, v, dofox, k_t, o_t, dofox.T, lse, lay)
    dq_t = dq_t.reshape(FOX_HEADS, KT_ROWS, n)
    dq = dq_t[:, :FOX_DIM].reshape(FOX_W, n).T
    dk_aug = dk_aug.reshape(n, FOX_HEADS, AUG)
    dk = dk_aug[:, :, :FOX_DIM].reshape(n, FOX_W)
    dcum = jnp.pad(dq_t[:, FOX_DIM].T - dk_aug[:, :, FOX_DIM], ((0, 0), (0, LANES - FOX_HEADS)))
    dlogf = _cumsum_rows(dcum, lay, reverse=True, name="fox_cum_bwd")

    def fox_prep_bwd(i, a, b_, f_, dq_, dk_, dl_, gq_, gk_, bf_, m_, mt_):
        valid = lay.valid(i, tile)
        _, vjp = jax.vjp(lambda a_, b__, f__, gq__, gk__, bf__: _fox_prep(a_, b__, f__, gq__, gk__, bf__, m_, mt_, valid),
                         a, b_, f_, gq_, gk_, bf_)
        return vjp((dq_, dk_, dl_))

    dfq, dfk, dff, g_gq, g_gk, g_bf = rw(
        fox_prep_bwd, [(proj, 512, C_FQ), (proj, 512, C_FK), pff, dq, dk, dlogf], [gq, gk, bf, gm64, gm64_t],
        [(512, MXU_DTYPE), (512, MXU_DTYPE), (LANES, MXU_DTYPE)], [(1, FOX_W), (1, FOX_W), (1, LANES)], name="fox_prep_bwd")
    grads["q_norm_gain"] = g_gq.reshape(FOX_HEADS, FOX_DIM).sum(0, keepdims=True)
    grads["k_norm_gain"] = g_gk.reshape(FOX_HEADS, FOX_DIM).sum(0, keepdims=True)
    grads["fox_b_f"] = g_bf[:, :FOX_HEADS]

    dproj = jnp.concatenate([dga, dgb, dfq, dfk, mx(dv), mx(dhq), dhf, mx(dhi), dhg], axis=1)
    dxn = _matmul(dproj, w_main, trans_b=True, name="proj_bwd_x")
    dxn_ff = _matmul(dff, w_ff, trans_b=True, name="proj_ff_bwd_x")
    g_main = _matmul(xn, dproj, trans_a=True, name="proj_bwd_w")
    g_ff = _matmul(xn, dff, trans_a=True, name="proj_ff_bwd_w")[:, :FOX_HEADS]
    p = jnp.split(g_main, list(np.cumsum([1024, 1024] + [512] * 6)), axis=1)
    grads["w_in"] = _by_chip(jnp.concatenate([p[2], p[3], p[4], g_ff, p[5], p[6], p[7], p[8], p[0], p[1]], axis=1))

    per = lay.lp // tile

    def norm1_bwd(i, h, d1, d2, dh1_, g):
        _, vjp = jax.vjp(_rms, h, g)
        dh, dg = vjp(d1 + d2)
        dh = dh + dh1_
        dmeta = jnp.where(lax.rem(i, per) == 0, dh[LEAD:LEAD + N_META, :], 0.0)
        return dh, dg, dmeta

    dh0, grads["norm1_gain"], grads["meta_tokens"] = rw(norm1_bwd, [h0, dxn, dxn_ff, dh1], [g1], [(D_MODEL, F32)],
                                                       [(1, D_MODEL), (N_META, D_MODEL)], name="norm1_bwd")
    grad_x = dh0.reshape(lay.batch, lay.lp, D_MODEL)[:, ROW0:ROW0 + lay.seq]
    return loss, grad_x, grads


MESH = pl.DeviceIdType.MESH
HBM_SPEC = pl.BlockSpec(memory_space=pltpu.HBM)
WEIGHT_NAMES = ["meta_tokens", "norm1_gain", "w_in", "fox_b_f", "q_norm_gain", "k_norm_gain", "hg_lb_logits", "hg_out_gain",
                "w_branch_a", "w_branch_b", "w_out", "norm2_gain", "w_up", "conv_w", "conv_b", "w_down"]
BIG = ("w_in", "w_branch_a", "w_branch_b", "w_out", "w_up", "w_down")
BIG_COL_SHARDED = ("w_in", "w_branch_a", "w_branch_b", "w_up")
SMALL = tuple(n for n in WEIGHT_NAMES if n not in BIG)
SMALL_SHARDED = ("meta_tokens", "conv_w")
SMALL_ROWS = 144
GATHER_SMALL_ROWS = 80


def _position():
    return lax.axis_index("x"), lax.axis_index("y"), lax.axis_index("c")


def _other_chips(x, y):
    return [(1 - x, y), (x, 1 - y), (1 - x, 1 - y)]


def _scalar(v):
    return jnp.reshape(v, (1,)).astype(jnp.int32)


def _row_tile(rows, cols):
    width = -(-cols // LANES) * LANES * 4
    best = 8
    for d in range(8, rows + 1, 8):
        if rows % d == 0 and d * width <= (1 << 20):
            best = d
    return best


def _gather_shards(shards):
    na = len(shards)
    halves = [s.shape[0] // 2 for s in shards]

    def body(*refs):
        xs, outs, send_sems, recv_sems, local_sems = refs[:na], refs[na:2 * na], refs[2 * na], refs[2 * na + 1], refs[2 * na + 2]
        x, y, c = _position()
        sibling = (x, y, 1 - c)
        chips = _other_chips(x, y)

        def copy(a, k, block, to, src=None):
            dst = outs[a].at[4 * block[0] + 2 * block[1] + block[2]]
            return pltpu.make_async_remote_copy(src_ref=dst if src is None else src, dst_ref=dst, send_sem=send_sems.at[a, k],
                                                recv_sem=recv_sems.at[a, k], device_id=to, device_id_type=MESH)

        first = []
        for a in range(na):
            mine = xs[a].at[pl.ds(pl.multiple_of(c * halves[a], 8), halves[a]), :]
            first += [copy(a, j, (x, y, c), (*chip, c), src=mine) for j, chip in enumerate(chips)]
        for cp in first:
            cp.start()
        own = [pltpu.make_async_copy(xs[a].at[pl.ds(h * halves[a], halves[a]), :], outs[a].at[4 * x + 2 * y + h],
                                     local_sems.at[a, h]) for a in range(na) for h in range(2)]
        for cp in own:
            cp.start()
        passed = []
        for j, chip in enumerate(chips):
            for a in range(na):
                copy(a, j, (*chip, c), (x, y, c)).wait_recv()
                cp = copy(a, 3 + j, (*chip, c), sibling)
                cp.start()
                passed.append(cp)
        for a in range(na):
            for j, chip in enumerate(chips):
                copy(a, 3 + j, (*chip, 1 - c), (x, y, c)).wait_recv()
        for cp in first + passed:
            cp.wait_send()
        for cp in own:
            cp.wait()

    return pl.pallas_call(
        body, name="gather_weights",
        out_shape=[jax.ShapeDtypeStruct((8, h, s.shape[1]), s.dtype) for h, s in zip(halves, shards)],
        in_specs=[HBM_SPEC] * na, out_specs=[HBM_SPEC] * na,
        scratch_shapes=[pltpu.SemaphoreType.DMA((na, 6)), pltpu.SemaphoreType.DMA((na, 6)), pltpu.SemaphoreType.DMA((na, 2))],
    )(*shards)


def _sibling_exchange(gs):
    na = len(gs)
    halves = [g.shape[1] // 2 for g in gs]

    def body(*refs):
        srcs, gots, send_sems, recv_sems = refs[:na], refs[na:2 * na], refs[2 * na], refs[2 * na + 1]
        x, y, c = _position()
        copies = [pltpu.make_async_remote_copy(
            src_ref=srcs[a].at[:, pl.ds(pl.multiple_of((1 - c) * halves[a], 8), halves[a]), :], dst_ref=gots[a],
            send_sem=send_sems.at[a], recv_sem=recv_sems.at[a], device_id=(x, y, 1 - c), device_id_type=MESH) for a in range(na)]
        for cp in copies:
            cp.start()
        for cp in copies:
            cp.wait()

    return pl.pallas_call(
        body, name="reduce_sibling",
        out_shape=[jax.ShapeDtypeStruct((N_CHIPS, h, g.shape[2]), g.dtype) for h, g in zip(halves, gs)],
        in_specs=[HBM_SPEC] * na, out_specs=[HBM_SPEC] * na,
        scratch_shapes=[pltpu.SemaphoreType.DMA((na,)), pltpu.SemaphoreType.DMA((na,))],
    )(*gs)


def _chip_exchange(parts):
    na = len(parts)

    def body(*refs):
        srcs, gots, send_sems, recv_sems = refs[:na], refs[na:2 * na], refs[2 * na], refs[2 * na + 1]
        x, y, c = _position()
        mine = 2 * x + y
        chips = _other_chips(x, y)

        def copy(a, j):
            cx, cy = chips[j]
            return pltpu.make_async_remote_copy(src_ref=srcs[a].at[2 * cx + cy], dst_ref=gots[a].at[mine],
                                                send_sem=send_sems.at[a, j], recv_sem=recv_sems.at[a, j],
                                                device_id=(cx, cy, c), device_id_type=MESH)

        def arrival(a, j):
            cx, cy = chips[j]
            return pltpu.make_async_remote_copy(src_ref=srcs[a].at[mine], dst_ref=gots[a].at[2 * cx + cy],
                                                send_sem=send_sems.at[a, j], recv_sem=recv_sems.at[a, j],
                                                device_id=(cx, cy, c), device_id_type=MESH)

        sends = [copy(a, j) for a in range(na) for j in range(3)]
        for cp in sends:
            cp.start()
        for a in range(na):
            for j in range(3):
                arrival(a, j).wait_recv()
        for cp in sends:
            cp.wait_send()

    return pl.pallas_call(
        body, name="reduce_chips", out_shape=[jax.ShapeDtypeStruct(p.shape, p.dtype) for p in parts],
        in_specs=[HBM_SPEC] * na, out_specs=[HBM_SPEC] * na,
        scratch_shapes=[pltpu.SemaphoreType.DMA((na, 3)), pltpu.SemaphoreType.DMA((na, 3))],
    )(*parts)


def _sibling_send(halves):
    na = len(halves)

    def body(*refs):
        srcs, gots, send_sems, recv_sems = refs[:na], refs[na:2 * na], refs[2 * na], refs[2 * na + 1]
        x, y, c = _position()
        copies = [pltpu.make_async_remote_copy(src_ref=srcs[a], dst_ref=gots[a], send_sem=send_sems.at[a], recv_sem=recv_sems.at[a],
                                               device_id=(x, y, 1 - c), device_id_type=MESH) for a in range(na)]
        for cp in copies:
            cp.start()
        for cp in copies:
            cp.wait()

    return pl.pallas_call(
        body, name="reduce_gather", out_shape=[jax.ShapeDtypeStruct(h.shape, h.dtype) for h in halves],
        in_specs=[HBM_SPEC] * na, out_specs=[HBM_SPEC] * na,
        scratch_shapes=[pltpu.SemaphoreType.DMA((na,)), pltpu.SemaphoreType.DMA((na,))],
    )(*halves)


def _add_own_half(g, got, c, name):
    _, r, cols = g.shape
    r2 = r // 2
    tr = _row_tile(r2, cols)
    nrt = r2 // tr

    def body(c_ref, g_ref, got_ref, o_ref):
        o_ref[...] = g_ref[...] + got_ref[...]

    blk = (1, tr, cols)
    return pl.pallas_call(
        body, name=name,
        grid_spec=pltpu.PrefetchScalarGridSpec(
            num_scalar_prefetch=1, grid=(N_CHIPS, nrt),
            in_specs=[pl.BlockSpec(blk, lambda j, i, c_: (j, c_[0] * nrt + i, 0)), pl.BlockSpec(blk, lambda j, i, c_: (j, i, 0))],
            out_specs=pl.BlockSpec(blk, lambda j, i, c_: (j, i, 0))),
        out_shape=jax.ShapeDtypeStruct((N_CHIPS, r2, cols), F32), compiler_params=_params(2),
    )(c, g, got)


def _add_chips(part, got, mine, name):
    _, r2, cols = part.shape
    tr = _row_tile(r2, cols)

    def body(m_ref, p_ref, g0, g1, g2, g3, o_ref):
        t = [jnp.where(m_ref[0] == k, p_ref[0], g[0]) for k, g in enumerate((g0, g1, g2, g3))]
        o_ref[...] = ((t[0] + t[1]) + t[2]) + t[3]

    blk = (1, tr, cols)
    others = [pl.BlockSpec(blk, functools.partial(lambda i, m, k: (jnp.where(m[0] == k, (k + 1) % N_CHIPS, k), i, 0), k=k))
              for k in range(N_CHIPS)]
    return pl.pallas_call(
        body, name=name,
        grid_spec=pltpu.PrefetchScalarGridSpec(
            num_scalar_prefetch=1, grid=(r2 // tr,),
            in_specs=[pl.BlockSpec(blk, lambda i, m: (m[0], i, 0))] + others,
            out_specs=pl.BlockSpec((tr, cols), lambda i, m: (i, 0))),
        out_shape=jax.ShapeDtypeStruct((r2, cols), F32), compiler_params=_params(1),
    )(mine, part, got, got, got, got)


def _adamw(w, own, other, m, v, c, name):
    r, cols = w.shape
    r2 = r // 2
    tr = _row_tile(r2, cols)
    nrt = r2 // tr
    c1 = 1.0 / (1.0 - ADAM_B1 ** ADAM_STEP)
    c2 = 1.0 / (1.0 - ADAM_B2 ** ADAM_STEP)

    def body(c_ref, w_ref, own_ref, other_ref, m_ref, v_ref, g_out, d_out, m_out, v_out):
        g_ = jnp.where(pl.program_id(0) == c_ref[0], own_ref[...], other_ref[...])
        m_new = ADAM_B1 * m_ref[...] + (1.0 - ADAM_B1) * g_
        v_new = ADAM_B2 * v_ref[...] + (1.0 - ADAM_B2) * (g_ * g_)
        g_out[...] = g_
        d_out[...] = -ADAM_LR * ((m_new * c1) / (jnp.sqrt(v_new * c2) + ADAM_EPS) + ADAM_WD * w_ref[...])
        m_out[...] = m_new
        v_out[...] = v_new

    full = pl.BlockSpec((tr, cols), lambda h, i, c_: (h * nrt + i, 0))
    half = pl.BlockSpec((tr, cols), lambda h, i, c_: (i, 0))
    out = jax.ShapeDtypeStruct((r, cols), F32)
    return pl.pallas_call(
        body, name=name,
        grid_spec=pltpu.PrefetchScalarGridSpec(num_scalar_prefetch=1, grid=(2, nrt), in_specs=[full, half, half, full, full],
                                               out_specs=[full] * 4),
        out_shape=[out] * 4, compiler_params=_params(2),
    )(c, w, own, other, m, v)


def _to_rows(flat, rows):
    return jnp.pad(flat, (0, rows * LANES - flat.shape[0])).reshape(rows, LANES)


def _pack_small(tree):
    return _to_rows(jnp.concatenate([tree[n].astype(F32).reshape(-1) for n in SMALL]), SMALL_ROWS)


def _unpack_small(packed, shapes):
    flat, out, at = packed.reshape(-1), {}, 0
    for n in SMALL:
        size = int(np.prod(shapes[n]))
        out[n] = flat[at:at + size].reshape(shapes[n])
        at += size
    return out


def _pack_small_by_chip(grads):
    pieces = []
    for n in SMALL:
        g = grads[n].astype(F32)
        if n in SMALL_SHARDED:
            pieces.append(_by_chip(g).reshape(N_CHIPS, -1))
        else:
            pieces.append(jnp.broadcast_to(g.reshape(1, -1), (N_CHIPS, g.size)))
    flat = jnp.concatenate(pieces, axis=1)
    return jnp.pad(flat, ((0, 0), (0, SMALL_ROWS * LANES - flat.shape[1]))).reshape(N_CHIPS, SMALL_ROWS, LANES)


def _gather_weights(local):
    shards = [local[n].reshape(local[n].shape[-2:]).astype(BF16) for n in BIG]
    shards.append(_to_rows(jnp.concatenate([local[n].astype(F32).reshape(-1) for n in SMALL_SHARDED]), GATHER_SMALL_ROWS))
    full = [g.reshape((N_CHIPS,) + s.shape) for s, g in zip(shards, _gather_shards(shards))]
    out = {}
    for n, f in zip(BIG, full):
        out[n] = _from_chips(f) if n in BIG_COL_SHARDED else f.reshape(N_CHIPS * f.shape[1], f.shape[2])
    flat, at = full[-1].reshape(N_CHIPS, -1), 0
    for n in SMALL_SHARDED:
        shape = local[n].shape[-2:]
        size = int(np.prod(shape))
        out[n] = _from_chips(flat[:, at:at + size].reshape((N_CHIPS,) + shape))
        at += size
    return out


def kernel(x, meta_tokens, norm1_gain, w_in, fox_b_f, q_norm_gain, k_norm_gain, hg_lb_logits, hg_out_gain, w_branch_a, w_branch_b, w_out, norm2_gain, w_up, conv_w, conv_b, w_down, loss_target, m_meta_tokens, m_norm1_gain, m_w_in, m_fox_b_f, m_q_norm_gain, m_k_norm_gain, m_hg_lb_logits, m_hg_out_gain, m_w_branch_a, m_w_branch_b, m_w_out, m_norm2_gain, m_w_up, m_conv_w, m_conv_b, m_w_down, v_meta_tokens, v_norm1_gain, v_w_in, v_fox_b_f, v_q_norm_gain, v_k_norm_gain, v_hg_lb_logits, v_hg_out_gain, v_w_branch_a, v_w_branch_b, v_w_out, v_norm2_gain, v_w_up, v_conv_w, v_conv_b, v_w_down):
    w_loc = dict(zip(WEIGHT_NAMES, (meta_tokens, norm1_gain, w_in, fox_b_f, q_norm_gain, k_norm_gain, hg_lb_logits, hg_out_gain,
                                    w_branch_a, w_branch_b, w_out, norm2_gain, w_up, conv_w, conv_b, w_down)))
    m_loc = dict(zip(WEIGHT_NAMES, (m_meta_tokens, m_norm1_gain, m_w_in, m_fox_b_f, m_q_norm_gain, m_k_norm_gain, m_hg_lb_logits,
                                    m_hg_out_gain, m_w_branch_a, m_w_branch_b, m_w_out, m_norm2_gain, m_w_up, m_conv_w, m_conv_b,
                                    m_w_down)))
    v_loc = dict(zip(WEIGHT_NAMES, (v_meta_tokens, v_norm1_gain, v_w_in, v_fox_b_f, v_q_norm_gain, v_k_norm_gain, v_hg_lb_logits,
                                    v_hg_out_gain, v_w_branch_a, v_w_branch_b, v_w_out, v_norm2_gain, v_w_up, v_conv_w, v_conv_b,
                                    v_w_down)))
    local_shapes = {n: tuple(w_loc[n].shape) for n in WEIGHT_NAMES}
    px, py, pc = _position()
    c, mine = _scalar(pc), _scalar(2 * px + py)

    weights = {n: w_loc[n].reshape(w_loc[n].shape[-2:]) for n in SMALL if n not in SMALL_SHARDED}
    weights.update(_gather_weights(w_loc))

    lay = _Layout(x.shape[0], x.shape[1])
    loss, grad_x, grads = _local_step(x, loss_target, weights, lay)
    loss = lax.psum(loss, ("x", "y", "c"))

    names = list(BIG) + ["small"]
    by_chip = [grads[n] for n in BIG] + [_pack_small_by_chip(grads)]
    from_sibling = _sibling_exchange(by_chip)
    parts = [_add_own_half(g, s, c, name=f"reduce_add2_{n}") for n, g, s in zip(names, by_chip, from_sibling)]
    from_chips = _chip_exchange(parts)
    own = [_add_chips(p, g, mine, name=f"reduce_add4_{n}") for n, p, g in zip(names, parts, from_chips)]
    other = _sibling_send(own)

    two_d = lambda t: [t[n].reshape(t[n].shape[-2:]) for n in BIG] + [_pack_small(t)]
    results = [_adamw(w_, o_, t_, m_, v_, c, name=f"adamw_{n}")
               for n, w_, o_, t_, m_, v_ in zip(names, two_d(w_loc), own, other, two_d(m_loc), two_d(v_loc))]
    outs = []
    for kind in range(4):
        tree = {n: results[i][kind].reshape(local_shapes[n]) for i, n in enumerate(BIG)}
        tree.update(_unpack_small(results[-1][kind], local_shapes))
        outs += [tree[n] for n in WEIGHT_NAMES]
    return (loss, grad_x, *outs)
```

```python
import functools

import jax
import jax.numpy as jnp
import numpy as np
from jax import lax
from jax.experimental import pallas as pl
from jax.experimental.pallas import tpu as pltpu

F32 = jnp.float32
BF16 = jnp.bfloat16
MXU_DTYPE = BF16
HIGHEST = lax.Precision.HIGHEST

D_MODEL = 1024
N_META = 16
LEAD = 48
ROW0 = LEAD + N_META
FOX_HEADS, FOX_DIM, FOX_W = 8, 64, 512
HG_HEADS, HG_DIM, HG_W = 4, 128, 512
D_FF = 2816
FF2 = 2 * D_FF
EPS = 1e-6
SUB = 16
LANES = 128
N_CHIPS = 4
NEG = -1e30

ADAM_LR, ADAM_B1, ADAM_B2, ADAM_EPS, ADAM_WD, ADAM_STEP = 0.001, 0.9, 0.999, 1e-08, 0.01, 10

VMEM_LIMIT = 56 * 1024 * 1024

C_GA, C_GB = 0, 1
C_FQ, C_FK, C_FV, C_HQ, C_HF, C_HI, C_HG = 4, 5, 6, 7, 8, 9, 10


def _params(n_axes=1):
    return pltpu.CompilerParams(dimension_semantics=("arbitrary",) * n_axes, vmem_limit_bytes=VMEM_LIMIT)


def _pick(n, cands):
    for c in cands:
        if n % c == 0:
            return c
    raise ValueError(f"no tile for {n} among {cands}")


def _rowwise(fn, rows, consts, outs, reds, *, n_rows, tile, name):
    assert n_rows % tile == 0
    rows = [r if isinstance(r, tuple) else (r, r.shape[1], 0) for r in rows]
    nr, nc, no = len(rows), len(consts), len(outs)

    def body(*refs):
        i = pl.program_id(0)
        ins = [r[...] for r in refs[:nr + nc]]
        res = fn(i, *ins)
        res = res if isinstance(res, (tuple, list)) else (res,)
        for ref, v in zip(refs[nr + nc:nr + nc + no], res[:no]):
            ref[...] = v.astype(ref.dtype)
        red_refs = refs[nr + nc + no:]
        if red_refs:
            @pl.when(i == 0)
            def _():
                for ref in red_refs:
                    ref[...] = jnp.zeros_like(ref)
            for ref, v in zip(red_refs, res[no:]):
                ref[...] += v.astype(F32)

    in_specs = [pl.BlockSpec((tile, w), functools.partial(lambda i, j: (i, j), j=j)) for (_, w, j) in rows]
    in_specs += [pl.BlockSpec(c.shape, functools.partial(lambda i, nd: (0,) * nd, nd=c.ndim)) for c in consts]
    out_specs = [pl.BlockSpec((tile, w), lambda i: (i, 0)) for (w, _) in outs]
    out_specs += [pl.BlockSpec(s, functools.partial(lambda i, nd: (0,) * nd, nd=len(s))) for s in reds]
    out_shape = [jax.ShapeDtypeStruct((n_rows, w), dt) for (w, dt) in outs]
    out_shape += [jax.ShapeDtypeStruct(s, F32) for s in reds]
    return pl.pallas_call(
        body, name=name, grid=(n_rows // tile,), in_specs=in_specs, out_specs=out_specs, out_shape=out_shape,
        compiler_params=_params(1),
    )(*[r[0] for r in rows], *consts)


def _matmul(a, b, *, trans_a=False, trans_b=False, out_dtype=F32, by_chip=False, name):
    if trans_a:
        k, m = a.shape
    else:
        m, k = a.shape
    n = b.shape[0] if trans_b else b.shape[1]
    assert (b.shape[1] if trans_b else b.shape[0]) == k
    if trans_a:
        tm = _pick(m, (1408, 1024, 512, 256, 128))
        tk = _pick(k, (1088, 1024, 768, 512, 256))
    else:
        tm = _pick(m, (1088, 512, 256, 128))
        tk = k if k <= 1024 else _pick(k, (1408, 1024, 512))
    tn = n // N_CHIPS if by_chip else _pick(n, (1408, 1024, 512, 256, 128))
    nk = k // tk
    dims = (((0 if trans_a else 1,), (1 if trans_b else 0,)), ((), ()))

    def body(a_ref, b_ref, o_ref, acc_ref):
        out = o_ref.at[0] if by_chip else o_ref
        part = lax.dot_general(a_ref[...], b_ref[...], dims, preferred_element_type=F32)
        if nk == 1:
            out[...] = part.astype(out.dtype)
        else:
            kk = pl.program_id(2)

            @pl.when(kk == 0)
            def _():
                acc_ref[...] = part

            @pl.when(kk > 0)
            def _():
                acc_ref[...] += part

            @pl.when(kk == nk - 1)
            def _():
                out[...] = acc_ref[...].astype(out.dtype)

    a_spec = pl.BlockSpec((tk, tm), lambda i, j, kk: (kk, i)) if trans_a else pl.BlockSpec((tm, tk), lambda i, j, kk: (i, kk))
    b_spec = pl.BlockSpec((tn, tk), lambda i, j, kk: (j, kk)) if trans_b else pl.BlockSpec((tk, tn), lambda i, j, kk: (kk, j))
    if by_chip:
        out_spec, out_shape = pl.BlockSpec((1, tm, tn), lambda i, j, kk: (j, i, 0)), (N_CHIPS, m, tn)
    else:
        out_spec, out_shape = pl.BlockSpec((tm, tn), lambda i, j, kk: (i, j)), (m, n)
    return pl.pallas_call(
        body, name=name, grid=(m // tm, n // tn, nk), in_specs=[a_spec, b_spec], out_specs=out_spec,
        out_shape=jax.ShapeDtypeStruct(out_shape, out_dtype),
        scratch_shapes=[pltpu.VMEM((tm, tn) if nk > 1 else (8, LANES), F32)],
        compiler_params=_params(3),
    )(a, b)


def _sigmoid(x):
    return 1.0 / (1.0 + jnp.exp(-x))


def _silu(x):
    return x * _sigmoid(x)


def _log_sigmoid(x):
    return jnp.minimum(x, 0.0) - jnp.log(1.0 + jnp.exp(-jnp.abs(x)))


def _rms(x, gain):
    return x * lax.rsqrt(jnp.mean(x * x, axis=-1, keepdims=True) + EPS) * gain


def _group_matrix(width, group):
    g = (np.arange(width)[:, None] // group == np.arange(LANES)[None, :]).astype(np.float32)
    return jnp.asarray(g, MXU_DTYPE), jnp.asarray(g.T.copy(), MXU_DTYPE)


def _split_dot(x, mat):
    dt = mat.dtype
    hi = x.astype(dt)
    r1 = x - hi.astype(F32)
    mid = r1.astype(dt)
    lo = (r1 - mid.astype(F32)).astype(dt)
    dot = lambda a: jnp.dot(a, mat, preferred_element_type=F32)
    return dot(hi) + dot(mid) + dot(lo)


@jax.custom_vjp
def _group_sum(x, gmat, gmat_t):
    return _split_dot(x, gmat)


@jax.custom_vjp
def _group_spread(s, gmat, gmat_t):
    return _split_dot(s, gmat_t)


_group_sum.defvjp(lambda x, g, gt: (_split_dot(x, g), (g, gt)),
                  lambda res, ct: (_group_spread(ct, *res), jnp.zeros_like(res[0]), jnp.zeros_like(res[1])))
_group_spread.defvjp(lambda s, g, gt: (_split_dot(s, gt), (g, gt)),
                     lambda res, ct: (_group_sum(ct, *res), jnp.zeros_like(res[0]), jnp.zeros_like(res[1])))


def _group_rms(x, gain, gmat, gmat_t, group):
    rstd = lax.rsqrt(_group_sum(x * x, gmat, gmat_t) * (1.0 / group) + EPS)
    return x * _group_spread(rstd, gmat, gmat_t) * gain


def _head_rms(x, gain):
    outs = []
    for h in range(x.shape[1] // LANES):
        xs = x[:, h * LANES:(h + 1) * LANES]
        outs.append(xs * lax.rsqrt(jnp.mean(xs * xs, axis=-1, keepdims=True) + EPS) * gain)
    return jnp.concatenate(outs, axis=1)


class _Layout:
    def __init__(self, batch, seq):
        self.batch, self.seq = batch, seq
        self.l_real = N_META + seq
        self.lp = -(-(LEAD + self.l_real) // 256) * 256
        self.n = batch * self.lp
        self.tile = _pick(self.lp, (512, 256))

    def valid(self, i, tile):
        per = self.lp // tile
        r = lax.rem(i, per) * tile + lax.broadcasted_iota(jnp.int32, (tile, 1), 0)
        return (r >= LEAD) & (r < LEAD + self.l_real)


def _cumsum_rows(x, lay, *, reverse, name):
    t = LANES
    nt = lay.lp // t
    c = x.shape[1]

    def body(x_ref, o_ref, carry):
        j = pl.program_id(1)

        @pl.when(j == 0)
        def _():
            carry[...] = jnp.zeros_like(carry)

        r = lax.broadcasted_iota(jnp.int32, (t, t), 0)
        q = lax.broadcasted_iota(jnp.int32, (t, t), 1)
        tri = jnp.where((q >= r) if reverse else (q <= r), 1.0, 0.0).astype(F32)
        xs = x_ref[...]
        out = jnp.dot(tri, xs, precision=HIGHEST, preferred_element_type=F32) + carry[0:1, :]
        o_ref[...] = out
        carry[...] = jnp.broadcast_to(carry[0:1, :] + jnp.sum(xs, axis=0, keepdims=True), carry.shape)

    def idx(b, j):
        return (b * nt + (nt - 1 - j if reverse else j), 0)

    return pl.pallas_call(
        body, name=name, grid=(lay.batch, nt),
        in_specs=[pl.BlockSpec((t, c), idx)], out_specs=pl.BlockSpec((t, c), idx),
        out_shape=jax.ShapeDtypeStruct(x.shape, F32),
        scratch_shapes=[pltpu.VMEM((8, c), F32)],
        compiler_params=_params(2),
    )(x)


def _group_cumsum(x, tile, *, reverse):
    r = lax.rem(lax.broadcasted_iota(jnp.int32, (tile, 1), 0), SUB)
    s = 1
    while s < SUB:
        if reverse:
            x = x + jnp.where(r < SUB - s, pltpu.roll(x, tile - s, 0), 0.0)
        else:
            x = x + jnp.where(r >= s, pltpu.roll(x, s, 0), 0.0)
        s *= 2
    return x


AUG = 128
FOX_BK = 256
FOX_BQ = 256
FOX_SCALE = FOX_DIM ** -0.5
KT_ROWS = FOX_DIM + 16


def _aug_matrices():
    e1 = np.zeros((FOX_W, FOX_HEADS * AUG), np.float32)
    e2 = np.zeros((LANES, FOX_HEADS * AUG), np.float32)
    ones = np.zeros((1, FOX_HEADS * AUG), np.float32)
    for h in range(FOX_HEADS):
        for d in range(FOX_DIM):
            e1[h * FOX_DIM + d, h * AUG + d] = 1.0
        for j in range(3):
            e2[j * FOX_HEADS + h, h * AUG + FOX_DIM + j] = 1.0
            ones[0, h * AUG + FOX_DIM + j] = 1.0
    return jnp.asarray(e1, MXU_DTYPE), jnp.asarray(e2, MXU_DTYPE), jnp.asarray(ones)


def _fox_augment(q, k, cum, key_ok, e1, e2, ones):
    dt = q.dtype
    c = jnp.where(key_ok, -cum, NEG)
    hi = c.astype(dt)
    r1 = c - hi.astype(F32)
    mid = r1.astype(dt)
    lo = (r1 - mid.astype(F32)).astype(dt)
    lane = lax.broadcasted_iota(jnp.int32, c.shape, 1)
    shift = lambda a, by: pltpu.roll(a.astype(F32), by, 1)
    parts = jnp.where(lane < FOX_HEADS, hi.astype(F32),
                      jnp.where(lane < 2 * FOX_HEADS, shift(mid, FOX_HEADS),
                                jnp.where(lane < 3 * FOX_HEADS, shift(lo, 2 * FOX_HEADS), 0.0))).astype(dt)
    qs = (q.astype(F32) * FOX_SCALE).astype(dt)
    q_aug = jnp.dot(qs, e1, preferred_element_type=F32) + ones
    k_aug = jnp.dot(k, e1, preferred_element_type=F32) + jnp.dot(parts, e2, preferred_element_type=F32)
    return q_aug.astype(dt), k_aug.astype(dt)


def _fox_tile(k_blk, q_blk, k0, q0, masked):
    st = lax.dot_general(k_blk, q_blk, (((1,), (1,)), ((), ())), preferred_element_type=F32)
    if masked:
        keys = k0 + lax.broadcasted_iota(jnp.int32, st.shape, 0)
        qs = q0 + lax.broadcasted_iota(jnp.int32, st.shape, 1)
        st = jnp.where(keys <= qs, st, NEG)
    return st


def _fox_fwd_t(q_aug, k_aug, v_t, lay):
    bk, bq = FOX_BK, FOX_BQ
    nq = lay.lp // bq
    pairs = FOX_HEADS // 2

    def body(q_ref, k_ref, vt_ref, ot_ref, lse_ref, zeros_ref):
        heads = [(slice(hh * AUG, (hh + 1) * AUG), slice(hh * FOX_DIM, (hh + 1) * FOX_DIM)) for hh in range(2)]
        zeros_ref[...] = jnp.zeros_like(zeros_ref)

        def q_loop(qb, _):
            q0 = pl.multiple_of(qb * bq, bq)
            q_blks = [q_ref[pl.ds(q0, bq), lanes] for lanes, _ in heads]

            def scores(kb, h):
                k0 = pl.multiple_of(kb * bk, bk)
                return _fox_tile(k_ref[pl.ds(k0, bk), heads[h][0]], q_blks[h], k0, q0, False)

            def consume(kb, h, state, masked):
                m, l, acc, pend, st = state
                k0 = pl.multiple_of(kb * bk, bk)
                if masked:
                    keys = k0 + lax.broadcasted_iota(jnp.int32, st.shape, 0)
                    qs_ = q0 + lax.broadcasted_iota(jnp.int32, st.shape, 1)
                    st = jnp.where(keys <= qs_, st, NEG)
                m_new = jnp.maximum(m, jnp.max(st, axis=0, keepdims=True))
                alpha = jnp.exp(m - m_new)
                p = jnp.exp(st - m_new)
                l = alpha * l + jnp.sum(p, axis=0, keepdims=True)
                acc = alpha * (acc + pend)
                pend = jnp.dot(vt_ref[heads[h][1], pl.ds(k0, bk)], p.astype(vt_ref.dtype), preferred_element_type=F32)
                return m_new, l, acc, pend

            def k_step(kb, states):
                nxt = [scores(kb + 1, h) for h in range(2)]
                return tuple(consume(kb, h, states[h], False) + (nxt[h],) for h in range(2))

            states = tuple((jnp.full((1, bq), NEG, F32), jnp.zeros((1, bq), F32), zeros_ref[...], zeros_ref[...], scores(0, h))
                           for h in range(2))
            states = lax.fori_loop(0, qb, k_step, states)
            qs = q0 + lax.broadcasted_iota(jnp.int32, (1, bq), 1)
            ok = (qs >= LEAD) & (qs < LEAD + lay.l_real)
            for hh in range(2):
                m, l, acc, pend = consume(qb, hh, states[hh], True)
                ot_ref[heads[hh][1], pl.ds(q0, bq)] = jnp.where(ok, (acc + pend) / l, 0.0).astype(ot_ref.dtype)
                lse_ref[hh, :, pl.ds(q0, bq)] = m + jnp.log(l)
            return 0

        lax.fori_loop(0, nq, q_loop, 0)

    aug = pl.BlockSpec((lay.lp, 2 * AUG), lambda b, p: (b, p))
    tr = pl.BlockSpec((2 * FOX_DIM, lay.lp), lambda b, p: (p, b))
    return pl.pallas_call(
        body, name="fox_fwd", grid=(lay.batch, pairs),
        in_specs=[aug, aug, tr],
        out_specs=[tr, pl.BlockSpec((2, 1, lay.lp), lambda b, p: (b * pairs + p, 0, 0))],
        out_shape=[jax.ShapeDtypeStruct((FOX_W, lay.n), MXU_DTYPE),
                   jax.ShapeDtypeStruct((lay.batch * FOX_HEADS, 1, lay.lp), F32)],
        scratch_shapes=[pltpu.VMEM((FOX_DIM, bq), F32)],
        compiler_params=_params(2),
    )(q_aug, k_aug, v_t)


def _fox_bwd_t(q_aug, k_aug, v, do, k_t, o_t, do_t, lse, lay):
    bk, bq = FOX_BK, FOX_BQ
    nq, nk = lay.lp // bq, lay.lp // bk
    pairs = FOX_HEADS // 2

    def body(q_ref, k_ref, v_ref, do_ref, kt_ref, ot_ref, dot_ref, lse_ref, dqt_ref, dk_ref, dv_ref, delta):
        dqt_ref[...] = jnp.zeros_like(dqt_ref)
        dk_ref[...] = jnp.zeros_like(dk_ref)
        dv_ref[...] = jnp.zeros_like(dv_ref)
        heads = [(hh, slice(hh * AUG, (hh + 1) * AUG), slice(hh * FOX_DIM, (hh + 1) * FOX_DIM),
                  slice(hh * KT_ROWS, (hh + 1) * KT_ROWS)) for hh in range(2)]

        def delta_loop(qb, _):
            q0 = pl.multiple_of(qb * bq, bq)
            for hh, _, cols, _ in heads:
                prod = ot_ref[cols, pl.ds(q0, bq)].astype(F32) * dot_ref[cols, pl.ds(q0, bq)].astype(F32)
                delta[hh, :, pl.ds(q0, bq)] = jnp.sum(prod, axis=0, keepdims=True)
            return 0

        lax.fori_loop(0, nq, delta_loop, 0)

        def k_loop(kb, _):
            k0 = pl.multiple_of(kb * bk, bk)

            def products(qb, h):
                q0 = pl.multiple_of(qb * bq, bq)
                _, lanes, cols, _ = heads[h]
                st = _fox_tile(k_ref[pl.ds(k0, bk), lanes], q_ref[pl.ds(q0, bq), lanes], k0, q0, False)
                dpt = lax.dot_general(v_ref[pl.ds(k0, bk), cols], do_ref[pl.ds(q0, bq), cols], (((1,), (1,)), ((), ())),
                                      preferred_element_type=F32)
                return st, dpt

            def consume(qb, h, st, dpt, masked):
                q0 = pl.multiple_of(qb * bq, bq)
                hh, lanes, cols, trows = heads[h]
                if masked:
                    keys = k0 + lax.broadcasted_iota(jnp.int32, st.shape, 0)
                    qs = q0 + lax.broadcasted_iota(jnp.int32, st.shape, 1)
                    st = jnp.where(keys <= qs, st, NEG)
                q_blk = q_ref[pl.ds(q0, bq), lanes]
                do_blk = do_ref[pl.ds(q0, bq), cols]
                pt = jnp.exp(st - lse_ref[hh, :, pl.ds(q0, bq)])
                dst = (pt * (dpt - delta[hh, :, pl.ds(q0, bq)])).astype(q_blk.dtype)
                dv_ref[pl.ds(k0, bk), cols] += jnp.dot(pt.astype(do_blk.dtype), do_blk, preferred_element_type=F32)
                dk_ref[pl.ds(k0, bk), lanes] += jnp.dot(dst, q_blk, preferred_element_type=F32)
                dqt_ref[trows, pl.ds(q0, bq)] += jnp.dot(kt_ref[trows, pl.ds(k0, bk)], dst, preferred_element_type=F32)

            after = lambda qb: jnp.minimum(qb + 1, nq - 1)
            cur = [products(kb, h) for h in range(2)]
            nxt = tuple(products(after(kb), h) for h in range(2))
            for h in range(2):
                consume(kb, h, *cur[h], True)

            def rest(qb, held):
                new = tuple(products(after(qb), h) for h in range(2))
                for h in range(2):
                    consume(qb, h, *held[h], False)
                return new

            lax.fori_loop(kb + 1, nq, rest, nxt)
            return 0

        lax.fori_loop(0, nk, k_loop, 0)

    aug = pl.BlockSpec((lay.lp, 2 * AUG), lambda b, p: (b, p))
    rows = pl.BlockSpec((lay.lp, 2 * FOX_DIM), lambda b, p: (b, p))
    tr = pl.BlockSpec((2 * FOX_DIM, lay.lp), lambda b, p: (p, b))
    tr_k = pl.BlockSpec((2 * KT_ROWS, lay.lp), lambda b, p: (p, b))
    return pl.pallas_call(
        body, name="fox_bwd", grid=(lay.batch, pairs),
        in_specs=[aug, aug, rows, rows, tr_k, tr, tr, pl.BlockSpec((2, 1, lay.lp), lambda b, p: (b * pairs + p, 0, 0))],
        out_specs=[tr_k, aug, rows],
        out_shape=[jax.ShapeDtypeStruct((FOX_HEADS * KT_ROWS, lay.n), F32), jax.ShapeDtypeStruct((lay.n, FOX_HEADS * AUG), F32),
                   jax.ShapeDtypeStruct((lay.n, FOX_W), F32)],
        scratch_shapes=[pltpu.VMEM((2, 1, lay.lp), F32)],
        compiler_params=_params(2),
    )(q_aug, k_aug, v, do, k_t, o_t, do_t, lse)


def _hgrn_fwd(proj, kk, gl, lay):
    t = lay.tile
    nt = lay.lp // t
    nsc = t // SUB

    def body(q_ref, k_ref, g_ref, v_ref, o_ref, st_ref, state, sub_rows):
        @pl.when(pl.program_id(1) == 0)
        def _():
            state[...] = jnp.zeros_like(state)

        rowi = lax.broadcasted_iota(jnp.int32, (SUB, 1), 0)

        def sub(sc, _):
            r0 = pl.multiple_of(sc * SUB, SUB)
            sub_rows[0] = k_ref[pl.ds(r0, SUB), :]
            sub_rows[1] = g_ref[pl.ds(r0, SUB), :]
            sub_rows[2] = v_ref[pl.ds(r0, SUB), :]
            for h in range(HG_HEADS):
                lanes = slice(h * HG_DIM, (h + 1) * HG_DIM)
                q16 = q_ref[pl.ds(r0, SUB), lanes]
                k16 = sub_rows[0, :, lanes]
                g16 = sub_rows[1, :, lanes]
                v16 = sub_rows[2, :, lanes]
                g_end = sub_rows[1, SUB - 1:SUB, lanes]
                s_prev = state[h]
                st_ref[sc, h] = s_prev
                o = lax.dot_general((q16 * jnp.exp(g16)).astype(MXU_DTYPE), s_prev.astype(MXU_DTYPE),
                                    (((1,), (1,)), ((), ())), preferred_element_type=F32)
                for s in range(SUB):
                    ks = sub_rows[0, s:s + 1, lanes]
                    gs = sub_rows[1, s:s + 1, lanes]
                    vs = sub_rows[2, s:s + 1, lanes]
                    w = q16 * jnp.exp(jnp.minimum(g16 - gs, 0.0)) * ks
                    a = jnp.where(rowi >= s, jnp.sum(w, axis=1, keepdims=True), 0.0)
                    o = o + a * vs
                o_ref[pl.ds(r0, SUB), lanes] = o
                kt = k16 * jnp.exp(g_end - g16)
                upd = lax.dot_general(v16.astype(MXU_DTYPE), kt.astype(MXU_DTYPE), (((0,), (0,)), ((), ())),
                                      preferred_element_type=F32)
                state[h] = jnp.exp(g_end) * s_prev + upd
            return 0

        lax.fori_loop(0, nsc, sub, 0)

    rows = lambda col: pl.BlockSpec((t, HG_W), functools.partial(lambda b, i, col: (b * nt + i, col), col=col))
    return pl.pallas_call(
        body, name="hgrn_fwd", grid=(lay.batch, nt),
        in_specs=[rows(C_HQ), rows(0), rows(0), rows(C_HI)],
        out_specs=[rows(0), pl.BlockSpec((nsc, HG_HEADS, HG_DIM, HG_DIM), lambda b, i: (b * nt + i, 0, 0, 0))],
        out_shape=[jax.ShapeDtypeStruct((lay.n, HG_W), F32),
                   jax.ShapeDtypeStruct((lay.n // SUB, HG_HEADS, HG_DIM, HG_DIM), F32)],
        scratch_shapes=[pltpu.VMEM((HG_HEADS, HG_DIM, HG_DIM), F32), pltpu.VMEM((3, SUB, HG_W), F32)],
        compiler_params=_params(2),
    )(proj, kk, gl, proj)


def _hgrn_bwd(proj, kk, gl, do, states, lay):
    t = lay.tile
    nt = lay.lp // t
    nsc = t // SUB

    def body(q_ref, k_ref, g_ref, v_ref, do_ref, st_ref, dq_ref, dk_ref, dv_ref, dg_ref, dstate, sub_rows, row_acc):
        @pl.when(pl.program_id(1) == 0)
        def _():
            dstate[...] = jnp.zeros_like(dstate)

        rowi = lax.broadcasted_iota(jnp.int32, (SUB, 1), 0)

        def sub(it, _):
            sc = nsc - 1 - it
            r0 = pl.multiple_of(sc * SUB, SUB)
            sub_rows[0] = k_ref[pl.ds(r0, SUB), :]
            sub_rows[1] = g_ref[pl.ds(r0, SUB), :]
            sub_rows[2] = v_ref[pl.ds(r0, SUB), :]
            for h in range(HG_HEADS):
                lanes = slice(h * HG_DIM, (h + 1) * HG_DIM)
                q16 = q_ref[pl.ds(r0, SUB), lanes]
                k16 = sub_rows[0, :, lanes]
                g16 = sub_rows[1, :, lanes]
                v16 = sub_rows[2, :, lanes]
                do16 = do_ref[pl.ds(r0, SUB), lanes]
                g_end = sub_rows[1, SUB - 1:SUB, lanes]
                s_prev = st_ref[sc, h]
                ds_end = dstate[h]
                eg = jnp.exp(g16)
                ekt = jnp.exp(g_end - g16)
                e_end = jnp.exp(g_end)
                qt = q16 * eg
                kt = k16 * ekt
                ds_mx = ds_end.astype(MXU_DTYPE)
                dv = lax.dot_general(kt.astype(MXU_DTYPE), ds_mx, (((1,), (1,)), ((), ())), preferred_element_type=F32)
                dkt = jnp.dot(v16.astype(MXU_DTYPE), ds_mx, preferred_element_type=F32)
                dk = dkt * ekt
                ktdkt = kt * dkt
                dg_end = jnp.sum(ktdkt, axis=0, keepdims=True) + jnp.sum(s_prev * ds_end, axis=0, keepdims=True) * e_end
                dg = jnp.where(rowi == SUB - 1, dg_end, 0.0) - ktdkt
                dqt = jnp.dot(do16.astype(MXU_DTYPE), s_prev.astype(MXU_DTYPE), preferred_element_type=F32)
                dq = dqt * eg
                dg = dg + qt * dqt
                dstate[h] = e_end * ds_end + lax.dot_general(do16.astype(MXU_DTYPE), qt.astype(MXU_DTYPE),
                                                             (((0,), (0,)), ((), ())), preferred_element_type=F32)
                for s in range(SUB):
                    ks = sub_rows[0, s:s + 1, lanes]
                    gs = sub_rows[1, s:s + 1, lanes]
                    vs = sub_rows[2, s:s + 1, lanes]
                    live = rowi >= s
                    e = jnp.where(live, jnp.exp(jnp.minimum(g16 - gs, 0.0)), 0.0)
                    qe = q16 * e
                    a = jnp.sum(qe * ks, axis=1, keepdims=True)
                    da = jnp.where(live, jnp.sum(do16 * vs, axis=1, keepdims=True), 0.0)
                    t1 = da * qe
                    dk_row = jnp.sum(t1, axis=0, keepdims=True)
                    dq = dq + da * (e * ks)
                    dg = dg + t1 * ks
                    row_acc[0, s:s + 1, :] = jnp.sum(a * do16, axis=0, keepdims=True)
                    row_acc[1, s:s + 1, :] = dk_row
                    row_acc[2, s:s + 1, :] = ks * dk_row
                dq_ref[pl.ds(r0, SUB), lanes] = dq
                dk_ref[pl.ds(r0, SUB), lanes] = dk + row_acc[1]
                dv_ref[pl.ds(r0, SUB), lanes] = dv + row_acc[0]
                dg_ref[pl.ds(r0, SUB), lanes] = dg - row_acc[2]
            return 0

        lax.fori_loop(0, nsc, sub, 0)

    def rows(col):
        return pl.BlockSpec((t, HG_W), functools.partial(lambda b, i, col: (b * nt + nt - 1 - i, col), col=col))

    out = jax.ShapeDtypeStruct((lay.n, HG_W), F32)
    return pl.pallas_call(
        body, name="hgrn_bwd", grid=(lay.batch, nt),
        in_specs=[rows(C_HQ), rows(0), rows(0), rows(C_HI), rows(0),
                  pl.BlockSpec((nsc, HG_HEADS, HG_DIM, HG_DIM), lambda b, i: (b * nt + nt - 1 - i, 0, 0, 0))],
        out_specs=[rows(0)] * 4, out_shape=[out] * 4,
        scratch_shapes=[pltpu.VMEM((HG_HEADS, HG_DIM, HG_DIM), F32), pltpu.VMEM((3, SUB, HG_W), F32),
                        pltpu.VMEM((3, SUB, HG_DIM), F32)],
        compiler_params=_params(2),
    )(proj, kk, gl, proj, do, states)


CONV_COLS = 1408


def _shift_down(x, halo, tile, by):
    out = pltpu.roll(x, by, 0)
    rowi = lax.broadcasted_iota(jnp.int32, (8, 1), 0)
    top = out[0:8]
    for r in range(by):
        top = jnp.where(rowi == r, halo[8 - by + r:8 - by + r + 1, :], top)
    return jnp.concatenate([top, out[8:]], axis=0)


def _shift_up(x, halo, tile, by):
    out = pltpu.roll(x, tile - by, 0)
    rowi = lax.broadcasted_iota(jnp.int32, (8, 1), 0)
    bottom = out[tile - 8:]
    for r in range(by):
        bottom = jnp.where(rowi == 8 - by + r, halo[r:r + 1, :], bottom)
    return jnp.concatenate([out[:tile - 8], bottom], axis=0)


def _conv_specs(tile):
    ncb = D_FF // CONV_COLS
    per8 = tile // 8

    def tile_spec(off):
        return pl.BlockSpec((tile, CONV_COLS), functools.partial(lambda i, j, off: (i, j + off), off=off))

    def prev_spec(off):
        return pl.BlockSpec((8, CONV_COLS), functools.partial(lambda i, j, off: (jnp.maximum(i * per8 - 1, 0), j + off), off=off))

    def w_spec(off):
        return pl.BlockSpec((3, CONV_COLS), functools.partial(lambda i, j, off: (0, j + off), off=off))

    def b_spec(off):
        return pl.BlockSpec((1, CONV_COLS), functools.partial(lambda i, j, off: (0, j + off), off=off))

    return ncb, tile_spec, prev_spec, w_spec, b_spec


def _conv3(x, halo, w, b, tile):
    return w[0:1, :] * _shift_down(x, halo, tile, 2) + w[1:2, :] * _shift_down(x, halo, tile, 1) + w[2:3, :] * x + b


def _conv_act_fwd(u, conv_w, conv_b, lay):
    tile = lay.tile
    ncb, tile_spec, prev_spec, w_spec, b_spec = _conv_specs(tile)

    def body(ug, uv, pg, pv, wg, wv, bg, bv, o_ref):
        cg = _conv3(ug[...], pg, wg, bg[...], tile)
        cv = _conv3(uv[...], pv, wv, bv[...], tile)
        o_ref[...] = (_silu(cg) * cv).astype(o_ref.dtype)

    return pl.pallas_call(
        body, name="conv_act_fwd", grid=(lay.n // tile, ncb),
        in_specs=[tile_spec(0), tile_spec(ncb), prev_spec(0), prev_spec(ncb), w_spec(0), w_spec(ncb), b_spec(0), b_spec(ncb)],
        out_specs=pl.BlockSpec((tile, CONV_COLS), lambda i, j: (i, j)),
        out_shape=jax.ShapeDtypeStruct((lay.n, D_FF), MXU_DTYPE),
        compiler_params=_params(2),
    )(u, u, u, u, conv_w, conv_w, conv_b, conv_b)


def _conv_act_bwd(u, dact, conv_w, conv_b, lay):
    tile = lay.tile
    ncb, tile_spec, prev_spec, w_spec, b_spec = _conv_specs(tile)

    def body(ug, uv, pg, pv, wg, wv, bg, bv, da_ref, dg_ref, dv_ref, gwg, gwv, gbg, gbv):
        @pl.when(pl.program_id(1) == 0)
        def _():
            for r in (gwg, gwv, gbg, gbv):
                r[...] = jnp.zeros_like(r)

        xg, xv = ug[...], uv[...]
        cg = _conv3(xg, pg, wg, bg[...], tile)
        cv = _conv3(xv, pv, wv, bv[...], tile)
        da = da_ref[...].astype(F32)
        sg = _sigmoid(cg)
        dcv = da * (cg * sg)
        dcg = da * cv * (sg * (1.0 + cg * (1.0 - sg)))
        dg_ref[...] = dcg
        dv_ref[...] = dcv
        for x, halo, dc, gw, gb in ((xg, pg, dcg, gwg, gbg), (xv, pv, dcv, gwv, gbv)):
            gw[0, 0:1, :] += jnp.sum(dc * _shift_down(x, halo, tile, 2), axis=0, keepdims=True)
            gw[0, 1:2, :] += jnp.sum(dc * _shift_down(x, halo, tile, 1), axis=0, keepdims=True)
            gw[0, 2:3, :] += jnp.sum(dc * x, axis=0, keepdims=True)
            gb[0] += jnp.sum(dc, axis=0, keepdims=True)

    swap = lambda spec: pl.BlockSpec(spec.block_shape, functools.partial(lambda j, i, f: f(i, j), f=spec.index_map))
    col = lambda j, i: (i, j)
    red_w = pl.BlockSpec((1, 3, CONV_COLS), lambda j, i: (j, 0, 0))
    red_b = pl.BlockSpec((1, 1, CONV_COLS), lambda j, i: (j, 0, 0))
    outs = pl.pallas_call(
        body, name="conv_act_bwd", grid=(ncb, lay.n // tile),
        in_specs=[swap(s) for s in (tile_spec(0), tile_spec(ncb), prev_spec(0), prev_spec(ncb), w_spec(0), w_spec(ncb),
                                    b_spec(0), b_spec(ncb))] + [pl.BlockSpec((tile, CONV_COLS), col)],
        out_specs=[pl.BlockSpec((tile, CONV_COLS), col), pl.BlockSpec((tile, CONV_COLS), col), red_w, red_w, red_b, red_b],
        out_shape=[jax.ShapeDtypeStruct((lay.n, D_FF), F32), jax.ShapeDtypeStruct((lay.n, D_FF), F32),
                   jax.ShapeDtypeStruct((ncb, 3, CONV_COLS), F32), jax.ShapeDtypeStruct((ncb, 3, CONV_COLS), F32),
                   jax.ShapeDtypeStruct((ncb, 1, CONV_COLS), F32), jax.ShapeDtypeStruct((ncb, 1, CONV_COLS), F32)],
        compiler_params=_params(2),
    )(u, u, u, u, conv_w, conv_w, conv_b, conv_b, dact)
    dcg, dcv, gwg, gwv, gbg, gbv = outs
    unblock = lambda g: jnp.transpose(g, (1, 0, 2)).reshape(g.shape[1], D_FF)
    g_w = jnp.concatenate([unblock(gwg), unblock(gwv)], axis=1)
    g_b = jnp.concatenate([unblock(gbg), unblock(gbv)], axis=1)
    return dcg, dcv, g_w, g_b


def _conv_input_bwd(dcg, dcv, conv_w, lay):
    tile = lay.tile
    ncb = D_FF // CONV_COLS
    nblk8 = lay.n // 8
    per8 = tile // 8
    nxt = lambda i: jnp.minimum((i + 1) * per8, nblk8 - 1)

    def half(dc, off, into, name):
        def body(*refs):
            d, halo, w, o = refs[0], refs[1], refs[2], refs[-1]
            x = d[...]
            du = w[2:3, :] * x + w[1:2, :] * _shift_up(x, halo, tile, 1) + w[0:1, :] * _shift_up(x, halo, tile, 2)
            o[...] = jnp.where(lay.valid(pl.program_id(0), tile), du, 0.0).astype(o.dtype)

        in_specs = [pl.BlockSpec((tile, CONV_COLS), lambda i, j: (i, j)),
                    pl.BlockSpec((8, CONV_COLS), lambda i, j: (nxt(i), j)),
                    pl.BlockSpec((3, CONV_COLS), lambda i, j: (0, j + off))]
        args = [dc, dc, conv_w]
        if into is not None:
            in_specs.append(pl.BlockSpec(memory_space=pltpu.HBM))
            args.append(into)
        return pl.pallas_call(
            body, name=name, grid=(lay.n // tile, ncb), in_specs=in_specs,
            out_specs=pl.BlockSpec((tile, CONV_COLS), lambda i, j: (i, j + off)),
            out_shape=jax.ShapeDtypeStruct((lay.n, FF2), MXU_DTYPE),
            input_output_aliases={} if into is None else {3: 0},
            compiler_params=_params(2),
        )(*args)

    return half(dcv, ncb, half(dcg, 0, None, "conv_input_bwd_gate"), "conv_input_bwd_value")


def _loss_head(h1, mlp, target, lay):
    t, sub = 256, ROW0
    per = lay.lp // t
    nsub = t // sub
    nreal = lay.seq // sub

    def body(h_ref, m_ref, *rest):
        t_refs, (loss_ref, dy_ref, dyb_ref) = rest[:nsub], rest[nsub:]
        b, j = pl.program_id(0), pl.program_id(1)

        @pl.when((b == 0) & (j == 0))
        def _():
            loss_ref[...] = jnp.zeros_like(loss_ref)

        rows_ = j * t + lax.broadcasted_iota(jnp.int32, (t, 1), 0)
        real = (rows_ >= ROW0) & (rows_ < ROW0 + lay.seq)
        tgt_ = jnp.concatenate([r[...] for r in t_refs], axis=0)
        err = jnp.where(real, h_ref[...] + m_ref[...] - tgt_, 0.0)
        dy = err * (1.0 / D_MODEL)
        dy_ref[...] = dy
        dyb_ref[...] = dy.astype(dyb_ref.dtype)
        loss_ref[...] += 0.5 * jnp.sum(err * dy)

    rows = pl.BlockSpec((t, D_MODEL), lambda b, j: (b * per + j, 0))
    tgt = [pl.BlockSpec((sub, D_MODEL), functools.partial(
        lambda b, j, r: (b * nreal + jnp.clip(j * nsub + r - 1, 0, nreal - 1), 0), r=r)) for r in range(nsub)]
    return pl.pallas_call(
        body, name="loss_head", grid=(lay.batch, per),
        in_specs=[rows, rows] + tgt,
        out_specs=[pl.BlockSpec((8, LANES), lambda b, j: (0, 0)), rows, rows],
        out_shape=[jax.ShapeDtypeStruct((8, LANES), F32), jax.ShapeDtypeStruct((lay.n, D_MODEL), F32),
                   jax.ShapeDtypeStruct((lay.n, D_MODEL), MXU_DTYPE)],
        compiler_params=_params(2),
    )(h1, mlp, *([target] * nsub))


def _fox_prep(fq, fk, ff, gq, gk, bf, gmat, gmat_t, valid):
    q = _group_rms(fq, gq, gmat, gmat_t, FOX_DIM)
    k = _group_rms(fk, gk, gmat, gmat_t, FOX_DIM)
    logf = jnp.where(valid, _log_sigmoid(ff + bf), 0.0)
    return q, k, logf


def _hg_prep(hf, l0, l1):
    mx = jnp.maximum(l0, l1)
    e0, e1 = jnp.exp(l0 - mx), jnp.exp(l1 - mx)
    lb = e0 / (e0 + e1)
    lf = jnp.log(lb + (1.0 - lb) * _sigmoid(hf))
    kk = (1.0 - lb) * _sigmoid(-hf)
    return lf, kk


def _hg_post(o, hg, gain):
    return _head_rms(o, gain) * _silu(hg)


def _gate(ga, gb, ya, yb):
    return _sigmoid(ga) * ya + _sigmoid(gb) * yb


def _by_chip(g):
    return jnp.transpose(g.reshape(g.shape[0], N_CHIPS, g.shape[1] // N_CHIPS), (1, 0, 2))


def _from_chips(a):
    return jnp.transpose(a, (1, 0, 2)).reshape(a.shape[1], N_CHIPS * a.shape[2])


def _local_step(x, target, w, lay):
    n, tile = lay.n, lay.tile
    rw = functools.partial(_rowwise, n_rows=n, tile=tile)
    mx = lambda a: a.astype(MXU_DTYPE)

    w_in = w["w_in"]
    fq, fk, fv, ffw, hq, hf, hi, hg, ga, gb = jnp.split(w_in, list(np.cumsum([512, 512, 512, 8, 512, 512, 512, 512, 1024])), axis=1)
    w_main = mx(jnp.concatenate([ga, gb, fq, fk, fv, hq, hf, hi, hg], axis=1))
    w_ff = mx(jnp.pad(ffw, ((0, 0), (0, LANES - FOX_HEADS))))
    w_a, w_b, w_out, w_up, w_down = mx(w["w_branch_a"]), mx(w["w_branch_b"]), mx(w["w_out"]), mx(w["w_up"]), mx(w["w_down"])
    conv_w, conv_b = w["conv_w"].astype(F32), w["conv_b"].astype(F32)
    g1, g2 = w["norm1_gain"], w["norm2_gain"]
    gq, gk = jnp.tile(w["q_norm_gain"], (1, FOX_HEADS)), jnp.tile(w["k_norm_gain"], (1, FOX_HEADS))
    bf = jnp.pad(w["fox_b_f"], ((0, 0), (0, LANES - FOX_HEADS)))
    lb_logits, hg_gain = w["hg_lb_logits"], w["hg_out_gain"]
    gm64, gm64_t = _group_matrix(FOX_W, FOX_DIM)

    meta = jnp.broadcast_to(w["meta_tokens"].astype(F32)[None], (lay.batch, N_META, D_MODEL))
    h0 = jnp.concatenate([jnp.zeros((lay.batch, LEAD, D_MODEL), F32), meta, x,
                          jnp.zeros((lay.batch, lay.lp - LEAD - lay.l_real, D_MODEL), F32)], axis=1).reshape(n, D_MODEL)

    (xn,) = rw(lambda i, h, g: _rms(h, g), [h0], [g1], [(D_MODEL, MXU_DTYPE)], [], name="norm1")
    proj = _matmul(xn, w_main, name="proj_main")
    pff = _matmul(xn, w_ff, name="proj_ff")

    def fox_prep_fn(i, a, b_, v_, f_, gq_, gk_, bf_, m_, mt_):
        q_, k_, logf = _fox_prep(a, b_, f_, gq_, gk_, bf_, m_, mt_, lay.valid(i, tile))
        return q_, k_, v_, logf

    q, k, v, logf = rw(fox_prep_fn, [(proj, 512, C_FQ), (proj, 512, C_FK), (proj, 512, C_FV), pff], [gq, gk, bf, gm64, gm64_t],
                       [(512, MXU_DTYPE), (512, MXU_DTYPE), (512, MXU_DTYPE), (LANES, F32)], [], name="fox_prep")
    cum = _cumsum_rows(logf, lay, reverse=False, name="fox_cum")
    e1, e2, aug_ones = _aug_matrices()
    q_aug, k_aug = rw(lambda i, q_, k_, c_, e1_, e2_, on_: _fox_augment(q_, k_, c_, lay.valid(i, tile), e1_, e2_, on_),
                      [q, k, cum], [e1, e2, aug_ones], [(FOX_HEADS * AUG, MXU_DTYPE)] * 2, [], name="fox_aug")
    o_t, lse = _fox_fwd_t(q_aug, k_aug, v.T, lay)

    def hg_prep_fn(i, hf_, l0, l1):
        lf, kk_ = _hg_prep(hf_, l0, l1)
        return kk_, _group_cumsum(lf, tile, reverse=False)

    lb0, lb1 = lb_logits[0:1], lb_logits[1:2]
    kk, gl = rw(hg_prep_fn, [(proj, 512, C_HF)], [lb0, lb1], [(512, F32), (512, F32)], [], name="hg_prep")
    o_hg, states = _hgrn_fwd(proj, kk, gl, lay)
    (oh,) = rw(lambda i, o, g_, gain: _hg_post(o, g_, gain), [o_hg, (proj, 512, C_HG)], [hg_gain], [(512, MXU_DTYPE)], [],
               name="hg_post")
    ya = _matmul(oh, w_a, name="branch_a")
    yb = _matmul(o_t, w_b, trans_a=True, name="branch_b")
    pga, pgb = (proj, 1024, C_GA), (proj, 1024, C_GB)
    (merged,) = rw(lambda i, a, b_, c_, d_: _gate(a, b_, c_, d_), [pga, pgb, ya, yb], [], [(D_MODEL, MXU_DTYPE)], [], name="gate")
    mo = _matmul(merged, w_out, name="out_proj")
    h1, hn = rw(lambda i, h, m_, g: (h + m_, _rms(h + m_, g)), [h0, mo], [g2], [(D_MODEL, F32), (D_MODEL, MXU_DTYPE)], [],
                name="norm2")
    u = _matmul(hn, w_up, name="up_proj")
    act = _conv_act_fwd(u, conv_w, conv_b, lay)
    mlp = _matmul(act, w_down, name="down_proj")
    loss_blk, dy, dyb = _loss_head(h1, mlp, target.reshape(lay.batch * lay.seq, D_MODEL), lay)
    loss = loss_blk[0, 0]

    grads = {}
    dact = _matmul(dyb, w_down, trans_b=True, out_dtype=MXU_DTYPE, name="down_bwd_x")
    grads["w_down"] = _matmul(act, dyb, trans_a=True, name="down_bwd_w").reshape(N_CHIPS, D_FF // N_CHIPS, D_MODEL)
    dcg, dcv, grads["conv_w"], grads["conv_b"] = _conv_act_bwd(u, dact, conv_w, conv_b, lay)
    du = _conv_input_bwd(dcg, dcv, conv_w, lay)
    dhn = _matmul(du, w_up, trans_b=True, name="up_bwd_x")
    grads["w_up"] = _matmul(hn, du, trans_a=True, by_chip=True, name="up_bwd_w")

    def norm2_bwd(i, h, d_, dy_, g):
        _, vjp = jax.vjp(_rms, h, g)
        dh, dg = vjp(d_)
        return dh + dy_, dh + dy_, dg

    dh1, dh1b, grads["norm2_gain"] = rw(norm2_bwd, [h1, dhn, dy], [g2], [(D_MODEL, F32), (D_MODEL, MXU_DTYPE)], [(1, D_MODEL)],
                                        name="norm2_bwd")
    dmerged = _matmul(dh1b, w_out, trans_b=True, name="out_bwd_x")
    grads["w_out"] = _matmul(merged, dh1b, trans_a=True, name="out_bwd_w").reshape(N_CHIPS, D_MODEL // N_CHIPS, D_MODEL)

    def gate_bwd(i, a, b_, c_, d_, dm):
        _, vjp = jax.vjp(_gate, a, b_, c_, d_)
        return vjp(dm)

    dga, dgb, dya, dyb_ = rw(gate_bwd, [pga, pgb, ya, yb, dmerged], [], [(D_MODEL, MXU_DTYPE)] * 4, [], name="gate_bwd")
    doh = _matmul(dya, w_a, trans_b=True, name="branch_a_bwd_x")
    grads["w_branch_a"] = _matmul(oh, dya, trans_a=True, by_chip=True, name="branch_a_bwd_w")
    dofox = _matmul(dyb_, w_b, trans_b=True, out_dtype=MXU_DTYPE, name="branch_b_bwd_x")
    grads["w_branch_b"] = _matmul(o_t, dyb_, by_chip=True, name="branch_b_bwd_w")

    def hg_post_bwd(i, o, g_, d_, gain):
        _, vjp = jax.vjp(_hg_post, o, g_, gain)
        return vjp(d_)

    do_hg, dhg, grads["hg_out_gain"] = rw(hg_post_bwd, [o_hg, (proj, 512, C_HG), doh], [hg_gain], [(512, F32), (512, MXU_DTYPE)],
                                          [(1, HG_DIM)], name="hg_post_bwd")
    dhq, dkk, dhi, dgl = _hgrn_bwd(proj, kk, gl, do_hg, states, lay)

    def hg_prep_bwd(i, hf_, dkk_, dgl_, l0, l1):
        _, vjp = jax.vjp(_hg_prep, hf_, l0, l1)
        return vjp((_group_cumsum(dgl_, tile, reverse=True), dkk_))

    dhf, g_lb0, g_lb1 = rw(hg_prep_bwd, [(proj, 512, C_HF), dkk, dgl], [lb0, lb1], [(512, MXU_DTYPE)], [(1, HG_W), (1, HG_W)],
                           name="hg_prep_bwd")
    grads["hg_lb_logits"] = jnp.concatenate([g_lb0, g_lb1], axis=0)

    k_t = (k.astype(F32) * FOX_SCALE).astype(MXU_DTYPE).T.reshape(FOX_HEADS, FOX_DIM, n)
    k_t = jnp.concatenate([k_t, jnp.ones((FOX_HEADS, KT_ROWS - FOX_DIM, n), MXU_DTYPE)], axis=1).reshape(FOX_HEADS * KT_ROWS, n)
    dq_t, dk_aug, dv = _fox_bwd_t(q_aug, k_aug, v, dofox, k_t, o_t, dofox.T, lse, lay)
    dq_t = dq_t.reshape(FOX_HEADS, KT_ROWS, n)
    dq = dq_t[:, :FOX_DIM].reshape(FOX_W, n).T
    dk_aug = dk_aug.reshape(n, FOX_HEADS, AUG)
    dk = dk_aug[:, :, :FOX_DIM].reshape(n, FOX_W)
    dcum = jnp.pad(dq_t[:, FOX_DIM].T - dk_aug[:, :, FOX_DIM], ((0, 0), (0, LANES - FOX_HEADS)))
    dlogf = _cumsum_rows(dcum, lay, reverse=True, name="fox_cum_bwd")

    def fox_prep_bwd(i, a, b_, f_, dq_, dk_, dl_, gq_, gk_, bf_, m_, mt_):
        valid = lay.valid(i, tile)
        _, vjp = jax.vjp(lambda a_, b__, f__, gq__, gk__, bf__: _fox_prep(a_, b__, f__, gq__, gk__, bf__, m_, mt_, valid),
                         a, b_, f_, gq_, gk_, bf_)
        return vjp((dq_, dk_, dl_))

    dfq, dfk, dff, g_gq, g_gk, g_bf = rw(
        fox_prep_bwd, [(proj, 512, C_FQ), (proj, 512, C_FK), pff, dq, dk, dlogf], [gq, gk, bf, gm64, gm64_t],
        [(512, MXU_DTYPE), (512, MXU_DTYPE), (LANES, MXU_DTYPE)], [(1, FOX_W), (1, FOX_W), (1, LANES)], name="fox_prep_bwd")
    grads["q_norm_gain"] = g_gq.reshape(FOX_HEADS, FOX_DIM).sum(0, keepdims=True)
    grads["k_norm_gain"] = g_gk.reshape(FOX_HEADS, FOX_DIM).sum(0, keepdims=True)
    grads["fox_b_f"] = g_bf[:, :FOX_HEADS]

    dproj = jnp.concatenate([dga, dgb, dfq, dfk, mx(dv), mx(dhq), dhf, mx(dhi), dhg], axis=1)
    dxn = _matmul(dproj, w_main, trans_b=True, name="proj_bwd_x")
    dxn_ff = _matmul(dff, w_ff, trans_b=True, name="proj_ff_bwd_x")
    g_main = _matmul(xn, dproj, trans_a=True, name="proj_bwd_w")
    g_ff = _matmul(xn, dff, trans_a=True, name="proj_ff_bwd_w")[:, :FOX_HEADS]
    p = jnp.split(g_main, list(np.cumsum([1024, 1024] + [512] * 6)), axis=1)
    grads["w_in"] = _by_chip(jnp.concatenate([p[2], p[3], p[4], g_ff, p[5], p[6], p[7], p[8], p[0], p[1]], axis=1))

    per = lay.lp // tile

    def norm1_bwd(i, h, d1, d2, dh1_, g):
        _, vjp = jax.vjp(_rms, h, g)
        dh, dg = vjp(d1 + d2)
        dh = dh + dh1_
        dmeta = jnp.where(lax.rem(i, per) == 0, dh[LEAD:LEAD + N_META, :], 0.0)
        return dh, dg, dmeta

    dh0, grads["norm1_gain"], grads["meta_tokens"] = rw(norm1_bwd, [h0, dxn, dxn_ff, dh1], [g1], [(D_MODEL, F32)],
                                                       [(1, D_MODEL), (N_META, D_MODEL)], name="norm1_bwd")
    grad_x = dh0.reshape(lay.batch, lay.lp, D_MODEL)[:, ROW0:ROW0 + lay.seq]
    return loss, grad_x, grads


MESH = pl.DeviceIdType.MESH
HBM_SPEC = pl.BlockSpec(memory_space=pltpu.HBM)
WEIGHT_NAMES = ["meta_tokens", "norm1_gain", "w_in", "fox_b_f", "q_norm_gain", "k_norm_gain", "hg_lb_logits", "hg_out_gain",
                "w_branch_a", "w_branch_b", "w_out", "norm2_gain", "w_up", "conv_w", "conv_b", "w_down"]
BIG = ("w_in", "w_branch_a", "w_branch_b", "w_out", "w_up", "w_down")
BIG_COL_SHARDED = ("w_in", "w_branch_a", "w_branch_b", "w_up")
SMALL = tuple(n for n in WEIGHT_NAMES if n not in BIG)
SMALL_SHARDED = ("meta_tokens", "conv_w")
SMALL_ROWS = 144
GATHER_SMALL_ROWS = 80


def _position():
    return lax.axis_index("x"), lax.axis_index("y"), lax.axis_index("c")


def _other_chips(x, y):
    return [(1 - x, y), (x, 1 - y), (1 - x, 1 - y)]


def _scalar(v):
    return jnp.reshape(v, (1,)).astype(jnp.int32)


def _row_tile(rows, cols):
    width = -(-cols // LANES) * LANES * 4
    best = 8
    for d in range(8, rows + 1, 8):
        if rows % d == 0 and d * width <= (1 << 20):
            best = d
    return best


def _gather_shards(shards):
    na = len(shards)
    halves = [s.shape[0] // 2 for s in shards]

    def body(*refs):
        xs, outs, send_sems, recv_sems = refs[:na], refs[na:2 * na], refs[2 * na], refs[2 * na + 1]
        x, y, c = _position()
        sibling = (x, y, 1 - c)
        chips = _other_chips(x, y)

        def copy(a, k, block, to, src=None):
            dst = outs[a].at[4 * block[0] + 2 * block[1] + block[2]]
            return pltpu.make_async_remote_copy(src_ref=dst if src is None else src, dst_ref=dst, send_sem=send_sems.at[a, k],
                                                recv_sem=recv_sems.at[a, k], device_id=to, device_id_type=MESH)

        first = []
        for a in range(na):
            mine = xs[a].at[pl.ds(pl.multiple_of(c * halves[a], 8), halves[a]), :]
            first += [copy(a, j, (x, y, c), (*chip, c), src=mine) for j, chip in enumerate(chips)]
        for cp in first:
            cp.start()
        passed = []
        for j, chip in enumerate(chips):
            for a in range(na):
                copy(a, j, (*chip, c), (x, y, c)).wait_recv()
                cp = copy(a, 3 + j, (*chip, c), sibling)
                cp.start()
                passed.append(cp)
        for a in range(na):
            for j, chip in enumerate(chips):
                copy(a, 3 + j, (*chip, 1 - c), (x, y, c)).wait_recv()
        for cp in first + passed:
            cp.wait_send()

    return pl.pallas_call(
        body, name="gather_weights",
        out_shape=[jax.ShapeDtypeStruct((8, h, s.shape[1]), s.dtype) for h, s in zip(halves, shards)],
        in_specs=[HBM_SPEC] * na, out_specs=[HBM_SPEC] * na,
        scratch_shapes=[pltpu.SemaphoreType.DMA((na, 6)), pltpu.SemaphoreType.DMA((na, 6))],
    )(*shards)


def _sibling_exchange(gs):
    na = len(gs)
    halves = [g.shape[1] // 2 for g in gs]

    def body(*refs):
        srcs, gots, send_sems, recv_sems = refs[:na], refs[na:2 * na], refs[2 * na], refs[2 * na + 1]
        x, y, c = _position()
        copies = [pltpu.make_async_remote_copy(
            src_ref=srcs[a].at[:, pl.ds(pl.multiple_of((1 - c) * halves[a], 8), halves[a]), :], dst_ref=gots[a],
            send_sem=send_sems.at[a], recv_sem=recv_sems.at[a], device_id=(x, y, 1 - c), device_id_type=MESH) for a in range(na)]
        for cp in copies:
            cp.start()
        for cp in copies:
            cp.wait()

    return pl.pallas_call(
        body, name="reduce_sibling",
        out_shape=[jax.ShapeDtypeStruct((N_CHIPS, h, g.shape[2]), g.dtype) for h, g in zip(halves, gs)],
        in_specs=[HBM_SPEC] * na, out_specs=[HBM_SPEC] * na,
        scratch_shapes=[pltpu.SemaphoreType.DMA((na,)), pltpu.SemaphoreType.DMA((na,))],
    )(*gs)


def _chip_exchange(parts):
    na = len(parts)

    def body(*refs):
        srcs, gots, send_sems, recv_sems = refs[:na], refs[na:2 * na], refs[2 * na], refs[2 * na + 1]
        x, y, c = _position()
        mine = 2 * x + y
        chips = _other_chips(x, y)

        def copy(a, j):
            cx, cy = chips[j]
            return pltpu.make_async_remote_copy(src_ref=srcs[a].at[2 * cx + cy], dst_ref=gots[a].at[mine],
                                                send_sem=send_sems.at[a, j], recv_sem=recv_sems.at[a, j],
                                                device_id=(cx, cy, c), device_id_type=MESH)

        def arrival(a, j):
            cx, cy = chips[j]
            return pltpu.make_async_remote_copy(src_ref=srcs[a].at[mine], dst_ref=gots[a].at[2 * cx + cy],
                                                send_sem=send_sems.at[a, j], recv_sem=recv_sems.at[a, j],
                                                device_id=(cx, cy, c), device_id_type=MESH)

        sends = [copy(a, j) for a in range(na) for j in range(3)]
        for cp in sends:
            cp.start()
        for a in range(na):
            for j in range(3):
                arrival(a, j).wait_recv()
        for cp in sends:
            cp.wait_send()

    return pl.pallas_call(
        body, name="reduce_chips", out_shape=[jax.ShapeDtypeStruct(p.shape, p.dtype) for p in parts],
        in_specs=[HBM_SPEC] * na, out_specs=[HBM_SPEC] * na,
        scratch_shapes=[pltpu.SemaphoreType.DMA((na, 3)), pltpu.SemaphoreType.DMA((na, 3))],
    )(*parts)


def _sibling_send(halves):
    na = len(halves)

    def body(*refs):
        srcs, gots, send_sems, recv_sems = refs[:na], refs[na:2 * na], refs[2 * na], refs[2 * na + 1]
        x, y, c = _position()
        copies = [pltpu.make_async_remote_copy(src_ref=srcs[a], dst_ref=gots[a], send_sem=send_sems.at[a], recv_sem=recv_sems.at[a],
                                               device_id=(x, y, 1 - c), device_id_type=MESH) for a in range(na)]
        for cp in copies:
            cp.start()
        for cp in copies:
            cp.wait()

    return pl.pallas_call(
        body, name="reduce_gather", out_shape=[jax.ShapeDtypeStruct(h.shape, h.dtype) for h in halves],
        in_specs=[HBM_SPEC] * na, out_specs=[HBM_SPEC] * na,
        scratch_shapes=[pltpu.SemaphoreType.DMA((na,)), pltpu.SemaphoreType.DMA((na,))],
    )(*halves)


def _add_own_half(g, got, c, dtype, name):
    _, r, cols = g.shape
    r2 = r // 2
    tr = _row_tile(r2, cols)
    nrt = r2 // tr

    def body(c_ref, g_ref, got_ref, o_ref):
        o_ref[...] = (g_ref[...] + got_ref[...]).astype(o_ref.dtype)

    blk = (1, tr, cols)
    return pl.pallas_call(
        body, name=name,
        grid_spec=pltpu.PrefetchScalarGridSpec(
            num_scalar_prefetch=1, grid=(N_CHIPS, nrt),
            in_specs=[pl.BlockSpec(blk, lambda j, i, c_: (j, c_[0] * nrt + i, 0)), pl.BlockSpec(blk, lambda j, i, c_: (j, i, 0))],
            out_specs=pl.BlockSpec(blk, lambda j, i, c_: (j, i, 0))),
        out_shape=jax.ShapeDtypeStruct((N_CHIPS, r2, cols), dtype), compiler_params=_params(2),
    )(c, g, got)


def _add_chips(part, got, mine, name):
    _, r2, cols = part.shape
    tr = _row_tile(r2, cols)

    def body(m_ref, p_ref, g0, g1, g2, g3, o_ref):
        t = [jnp.where(m_ref[0] == k, p_ref[0], g[0]).astype(F32) for k, g in enumerate((g0, g1, g2, g3))]
        o_ref[...] = ((t[0] + t[1]) + t[2]) + t[3]

    blk = (1, tr, cols)
    others = [pl.BlockSpec(blk, functools.partial(lambda i, m, k: (jnp.where(m[0] == k, (k + 1) % N_CHIPS, k), i, 0), k=k))
              for k in range(N_CHIPS)]
    return pl.pallas_call(
        body, name=name,
        grid_spec=pltpu.PrefetchScalarGridSpec(
            num_scalar_prefetch=1, grid=(r2 // tr,),
            in_specs=[pl.BlockSpec(blk, lambda i, m: (m[0], i, 0))] + others,
            out_specs=pl.BlockSpec((tr, cols), lambda i, m: (i, 0))),
        out_shape=jax.ShapeDtypeStruct((r2, cols), F32), compiler_params=_params(1),
    )(mine, part, got, got, got, got)


def _adamw(w, own, other, m, v, c, name):
    r, cols = w.shape
    r2 = r // 2
    tr = _row_tile(r2, cols)
    nrt = r2 // tr
    c1 = 1.0 / (1.0 - ADAM_B1 ** ADAM_STEP)
    c2 = 1.0 / (1.0 - ADAM_B2 ** ADAM_STEP)

    def body(c_ref, w_ref, own_ref, other_ref, m_ref, v_ref, g_out, d_out, m_out, v_out):
        g_ = jnp.where(pl.program_id(0) == c_ref[0], own_ref[...], other_ref[...])
        m_new = ADAM_B1 * m_ref[...] + (1.0 - ADAM_B1) * g_
        v_new = ADAM_B2 * v_ref[...] + (1.0 - ADAM_B2) * (g_ * g_)
        g_out[...] = g_
        d_out[...] = -ADAM_LR * ((m_new * c1) / (jnp.sqrt(v_new * c2) + ADAM_EPS) + ADAM_WD * w_ref[...])
        m_out[...] = m_new
        v_out[...] = v_new

    full = pl.BlockSpec((tr, cols), lambda h, i, c_: (h * nrt + i, 0))
    half = pl.BlockSpec((tr, cols), lambda h, i, c_: (i, 0))
    out = jax.ShapeDtypeStruct((r, cols), F32)
    return pl.pallas_call(
        body, name=name,
        grid_spec=pltpu.PrefetchScalarGridSpec(num_scalar_prefetch=1, grid=(2, nrt), in_specs=[full, half, half, full, full],
                                               out_specs=[full] * 4),
        out_shape=[out] * 4, compiler_params=_params(2),
    )(c, w, own, other, m, v)


def _to_rows(flat, rows):
    return jnp.pad(flat, (0, rows * LANES - flat.shape[0])).reshape(rows, LANES)


def _pack_small(tree):
    return _to_rows(jnp.concatenate([tree[n].astype(F32).reshape(-1) for n in SMALL]), SMALL_ROWS)


def _unpack_small(packed, shapes):
    flat, out, at = packed.reshape(-1), {}, 0
    for n in SMALL:
        size = int(np.prod(shapes[n]))
        out[n] = flat[at:at + size].reshape(shapes[n])
        at += size
    return out


def _pack_small_by_chip(grads):
    pieces = []
    for n in SMALL:
        g = grads[n].astype(F32)
        if n in SMALL_SHARDED:
            pieces.append(_by_chip(g).reshape(N_CHIPS, -1))
        else:
            pieces.append(jnp.broadcast_to(g.reshape(1, -1), (N_CHIPS, g.size)))
    flat = jnp.concatenate(pieces, axis=1)
    return jnp.pad(flat, ((0, 0), (0, SMALL_ROWS * LANES - flat.shape[1]))).reshape(N_CHIPS, SMALL_ROWS, LANES)


def _gather_weights(local):
    shards = [local[n].reshape(local[n].shape[-2:]).astype(BF16) for n in BIG]
    shards.append(_to_rows(jnp.concatenate([local[n].astype(F32).reshape(-1) for n in SMALL_SHARDED]), GATHER_SMALL_ROWS))
    x, y, _ = _position()
    is_mine = (lax.broadcasted_iota(jnp.int32, (N_CHIPS, 1, 1), 0) == 2 * x + y)
    full = [jnp.where(is_mine, s[None], g.reshape((N_CHIPS,) + s.shape)) for s, g in zip(shards, _gather_shards(shards))]
    out = {}
    for n, f in zip(BIG, full):
        out[n] = _from_chips(f) if n in BIG_COL_SHARDED else f.reshape(N_CHIPS * f.shape[1], f.shape[2])
    flat, at = full[-1].reshape(N_CHIPS, -1), 0
    for n in SMALL_SHARDED:
        shape = local[n].shape[-2:]
        size = int(np.prod(shape))
        out[n] = _from_chips(flat[:, at:at + size].reshape((N_CHIPS,) + shape))
        at += size
    return out


def kernel(x, meta_tokens, norm1_gain, w_in, fox_b_f, q_norm_gain, k_norm_gain, hg_lb_logits, hg_out_gain, w_branch_a, w_branch_b, w_out, norm2_gain, w_up, conv_w, conv_b, w_down, loss_target, m_meta_tokens, m_norm1_gain, m_w_in, m_fox_b_f, m_q_norm_gain, m_k_norm_gain, m_hg_lb_logits, m_hg_out_gain, m_w_branch_a, m_w_branch_b, m_w_out, m_norm2_gain, m_w_up, m_conv_w, m_conv_b, m_w_down, v_meta_tokens, v_norm1_gain, v_w_in, v_fox_b_f, v_q_norm_gain, v_k_norm_gain, v_hg_lb_logits, v_hg_out_gain, v_w_branch_a, v_w_branch_b, v_w_out, v_norm2_gain, v_w_up, v_conv_w, v_conv_b, v_w_down):
    w_loc = dict(zip(WEIGHT_NAMES, (meta_tokens, norm1_gain, w_in, fox_b_f, q_norm_gain, k_norm_gain, hg_lb_logits, hg_out_gain,
                                    w_branch_a, w_branch_b, w_out, norm2_gain, w_up, conv_w, conv_b, w_down)))
    m_loc = dict(zip(WEIGHT_NAMES, (m_meta_tokens, m_norm1_gain, m_w_in, m_fox_b_f, m_q_norm_gain, m_k_norm_gain, m_hg_lb_logits,
                                    m_hg_out_gain, m_w_branch_a, m_w_branch_b, m_w_out, m_norm2_gain, m_w_up, m_conv_w, m_conv_b,
                                    m_w_down)))
    v_loc = dict(zip(WEIGHT_NAMES, (v_meta_tokens, v_norm1_gain, v_w_in, v_fox_b_f, v_q_norm_gain, v_k_norm_gain, v_hg_lb_logits,
                                    v_hg_out_gain, v_w_branch_a, v_w_branch_b, v_w_out, v_norm2_gain, v_w_up, v_conv_w, v_conv_b,
                                    v_w_down)))
    local_shapes = {n: tuple(w_loc[n].shape) for n in WEIGHT_NAMES}
    px, py, pc = _position()
    c, mine = _scalar(pc), _scalar(2 * px + py)

    weights = {n: w_loc[n].reshape(w_loc[n].shape[-2:]) for n in SMALL if n not in SMALL_SHARDED}
    weights.update(_gather_weights(w_loc))

    lay = _Layout(x.shape[0], x.shape[1])
    loss, grad_x, grads = _local_step(x, loss_target, weights, lay)
    loss = lax.psum(loss, ("x", "y", "c"))

    names = list(BIG) + ["small"]
    by_chip = [grads[n] for n in BIG] + [_pack_small_by_chip(grads)]
    from_sibling = _sibling_exchange(by_chip)
    parts = [_add_own_half(g, s, c, F32 if n == "small" else BF16, name=f"reduce_add2_{n}")
             for n, g, s in zip(names, by_chip, from_sibling)]
    from_chips = _chip_exchange(parts)
    own = [_add_chips(p, g, mine, name=f"reduce_add4_{n}") for n, p, g in zip(names, parts, from_chips)]
    other = _sibling_send(own)

    two_d = lambda t: [t[n].reshape(t[n].shape[-2:]) for n in BIG] + [_pack_small(t)]
    results = [_adamw(w_, o_, t_, m_, v_, c, name=f"adamw_{n}")
               for n, w_, o_, t_, m_, v_ in zip(names, two_d(w_loc), own, other, two_d(m_loc), two_d(v_loc))]
    outs = []
    for kind in range(4):
        tree = {n: results[i][kind].reshape(local_shapes[n]) for i, n in enumerate(BIG)}
        tree.update(_unpack_small(results[-1][kind], local_shapes))
        outs += [tree[n] for n in WEIGHT_NAMES]
    return (loss, grad_x, *outs)
```

```python
import functools

import jax
import jax.numpy as jnp
import numpy as np
from jax import lax
from jax.experimental import pallas as pl
from jax.experimental.pallas import tpu as pltpu

F32 = jnp.float32
BF16 = jnp.bfloat16
MXU_DTYPE = BF16
HIGHEST = lax.Precision.HIGHEST

D_MODEL = 1024
N_META = 16
LEAD = 48
ROW0 = LEAD + N_META
FOX_HEADS, FOX_DIM, FOX_W = 8, 64, 512
HG_HEADS, HG_DIM, HG_W = 4, 128, 512
D_FF = 2816
FF2 = 2 * D_FF
EPS = 1e-6
SUB = 16
LANES = 128
N_CHIPS = 4
NEG = -1e30

ADAM_LR, ADAM_B1, ADAM_B2, ADAM_EPS, ADAM_WD, ADAM_STEP = 0.001, 0.9, 0.999, 1e-08, 0.01, 10

VMEM_LIMIT = 56 * 1024 * 1024

C_GA, C_GB = 0, 1
C_FQ, C_FK, C_FV, C_HQ, C_HI, C_HF, C_HG = 4, 5, 6, 7, 8, 9, 10
MAIN_COLS = 11 * 512


def _params(n_axes=1):
    return pltpu.CompilerParams(dimension_semantics=("arbitrary",) * n_axes, vmem_limit_bytes=VMEM_LIMIT)


def _pick(n, cands):
    for c in cands:
        if n % c == 0:
            return c
    raise ValueError(f"no tile for {n} among {cands}")


def _rowwise(fn, rows, consts, outs, reds, *, n_rows, tile, name, into=None):
    assert n_rows % tile == 0
    rows = [r if isinstance(r, tuple) else (r, r.shape[1], 0) for r in rows]
    nr, nc, no = len(rows), len(consts), len(outs)
    aliased = into is not None and not isinstance(into[0], int)
    n_in = nr + nc + (1 if aliased else 0)

    def body(*refs):
        i = pl.program_id(0)
        ins = [r[...] for r in refs[:nr + nc]]
        res = fn(i, *ins)
        res = res if isinstance(res, (tuple, list)) else (res,)
        for ref, v in zip(refs[n_in:n_in + no], res[:no]):
            ref[...] = v.astype(ref.dtype)
        red_refs = refs[n_in + no:]
        if red_refs:
            @pl.when(i == 0)
            def _():
                for ref in red_refs:
                    ref[...] = jnp.zeros_like(ref)
            for ref, v in zip(red_refs, res[no:]):
                ref[...] += v.astype(F32)

    in_specs = [pl.BlockSpec((tile, w), functools.partial(lambda i, j: (i, j), j=j)) for (_, w, j) in rows]
    in_specs += [pl.BlockSpec(c.shape, functools.partial(lambda i, nd: (0,) * nd, nd=c.ndim)) for c in consts]
    out_specs = [pl.BlockSpec((tile, w), lambda i: (i, 0)) for (w, _) in outs]
    out_specs += [pl.BlockSpec(s, functools.partial(lambda i, nd: (0,) * nd, nd=len(s))) for s in reds]
    out_shape = [jax.ShapeDtypeStruct((n_rows, w), dt) for (w, dt) in outs]
    out_shape += [jax.ShapeDtypeStruct(s, F32) for s in reds]
    args = [r[0] for r in rows] + list(consts)
    aliases = {}
    if into is not None:
        out_specs[0] = pl.BlockSpec((tile, outs[0][0]), functools.partial(lambda i, j: (i, j), j=into[1]))
        if aliased:
            in_specs.append(pl.BlockSpec(memory_space=pltpu.HBM))
            args.append(into[0])
            aliases = {n_in - 1: 0}
            out_shape[0] = jax.ShapeDtypeStruct(into[0].shape, into[0].dtype)
        else:
            out_shape[0] = jax.ShapeDtypeStruct((n_rows, into[0]), outs[0][1])
    return pl.pallas_call(
        body, name=name, grid=(n_rows // tile,), in_specs=in_specs, out_specs=out_specs, out_shape=out_shape,
        input_output_aliases=aliases, compiler_params=_params(1),
    )(*args)


def _matmul(a, b, *, trans_a=False, trans_b=False, out_dtype=F32, by_chip=False, name):
    if trans_a:
        k, m = a.shape
    else:
        m, k = a.shape
    n = b.shape[0] if trans_b else b.shape[1]
    assert (b.shape[1] if trans_b else b.shape[0]) == k
    if trans_a:
        tm = _pick(m, (1408, 1024, 512, 256, 128))
        tk = _pick(k, (1088, 1024, 768, 512, 256))
    else:
        tm = _pick(m, (1088, 512, 256, 128))
        tk = k if k <= 1024 else _pick(k, (1408, 1024, 512))
    nk = k // tk
    wide = (2816,) if nk == 1 and not trans_a else ()
    tn = n // N_CHIPS if by_chip else _pick(n, wide + (1408, 1024, 512, 256, 128))
    dims = (((0 if trans_a else 1,), (1 if trans_b else 0,)), ((), ()))

    def body(a_ref, b_ref, o_ref, acc_ref):
        out = o_ref.at[0] if by_chip else o_ref
        part = lax.dot_general(a_ref[...], b_ref[...], dims, preferred_element_type=F32)
        if nk == 1:
            out[...] = part.astype(out.dtype)
        else:
            kk = pl.program_id(2)

            @pl.when(kk == 0)
            def _():
                acc_ref[...] = part

            @pl.when(kk > 0)
            def _():
                acc_ref[...] += part

            @pl.when(kk == nk - 1)
            def _():
                out[...] = acc_ref[...].astype(out.dtype)

    a_spec = pl.BlockSpec((tk, tm), lambda i, j, kk: (kk, i)) if trans_a else pl.BlockSpec((tm, tk), lambda i, j, kk: (i, kk))
    b_spec = pl.BlockSpec((tn, tk), lambda i, j, kk: (j, kk)) if trans_b else pl.BlockSpec((tk, tn), lambda i, j, kk: (kk, j))
    if by_chip:
        out_spec, out_shape = pl.BlockSpec((1, tm, tn), lambda i, j, kk: (j, i, 0)), (N_CHIPS, m, tn)
    else:
        out_spec, out_shape = pl.BlockSpec((tm, tn), lambda i, j, kk: (i, j)), (m, n)
    return pl.pallas_call(
        body, name=name, grid=(m // tm, n // tn, nk), in_specs=[a_spec, b_spec], out_specs=out_spec,
        out_shape=jax.ShapeDtypeStruct(out_shape, out_dtype),
        scratch_shapes=[pltpu.VMEM((tm, tn) if nk > 1 else (8, LANES), F32)],
        compiler_params=_params(3),
    )(a, b)


def _sigmoid(x):
    return 1.0 / (1.0 + jnp.exp(-x))


def _silu(x):
    return x * _sigmoid(x)


def _log_sigmoid(x):
    return jnp.minimum(x, 0.0) - jnp.log(1.0 + jnp.exp(-jnp.abs(x)))


def _rms(x, gain):
    return x * lax.rsqrt(jnp.mean(x * x, axis=-1, keepdims=True) + EPS) * gain


def _group_matrix(width, group):
    g = (np.arange(width)[:, None] // group == np.arange(LANES)[None, :]).astype(np.float32)
    return jnp.asarray(g, MXU_DTYPE), jnp.asarray(g.T.copy(), MXU_DTYPE)


def _split_dot(x, mat):
    dt = mat.dtype
    hi = x.astype(dt)
    r1 = x - hi.astype(F32)
    mid = r1.astype(dt)
    lo = (r1 - mid.astype(F32)).astype(dt)
    dot = lambda a: jnp.dot(a, mat, preferred_element_type=F32)
    return dot(hi) + dot(mid) + dot(lo)


@jax.custom_vjp
def _group_sum(x, gmat, gmat_t):
    return _split_dot(x, gmat)


@jax.custom_vjp
def _group_spread(s, gmat, gmat_t):
    return _split_dot(s, gmat_t)


_group_sum.defvjp(lambda x, g, gt: (_split_dot(x, g), (g, gt)),
                  lambda res, ct: (_group_spread(ct, *res), jnp.zeros_like(res[0]), jnp.zeros_like(res[1])))
_group_spread.defvjp(lambda s, g, gt: (_split_dot(s, gt), (g, gt)),
                     lambda res, ct: (_group_sum(ct, *res), jnp.zeros_like(res[0]), jnp.zeros_like(res[1])))


def _group_rms(x, gain, gmat, gmat_t, group):
    rstd = lax.rsqrt(_group_sum(x * x, gmat, gmat_t) * (1.0 / group) + EPS)
    return x * _group_spread(rstd, gmat, gmat_t) * gain


def _head_rms(x, gain):
    outs = []
    for h in range(x.shape[1] // LANES):
        xs = x[:, h * LANES:(h + 1) * LANES]
        outs.append(xs * lax.rsqrt(jnp.mean(xs * xs, axis=-1, keepdims=True) + EPS) * gain)
    return jnp.concatenate(outs, axis=1)


class _Layout:
    def __init__(self, batch, seq):
        self.batch, self.seq = batch, seq
        self.l_real = N_META + seq
        self.lp = -(-(LEAD + self.l_real) // 256) * 256
        self.n = batch * self.lp
        self.tile = _pick(self.lp, (512, 256))

    def valid(self, i, tile):
        per = self.lp // tile
        r = lax.rem(i, per) * tile + lax.broadcasted_iota(jnp.int32, (tile, 1), 0)
        return (r >= LEAD) & (r < LEAD + self.l_real)


def _cumsum_rows(x, lay, *, reverse, name):
    t = LANES
    nt = lay.lp // t
    c = x.shape[1]

    def body(x_ref, o_ref, carry):
        j = pl.program_id(1)

        @pl.when(j == 0)
        def _():
            carry[...] = jnp.zeros_like(carry)

        r = lax.broadcasted_iota(jnp.int32, (t, t), 0)
        q = lax.broadcasted_iota(jnp.int32, (t, t), 1)
        tri = jnp.where((q >= r) if reverse else (q <= r), 1.0, 0.0).astype(F32)
        xs = x_ref[...]
        out = jnp.dot(tri, xs, precision=HIGHEST, preferred_element_type=F32) + carry[0:1, :]
        o_ref[...] = out
        carry[...] = jnp.broadcast_to(carry[0:1, :] + jnp.sum(xs, axis=0, keepdims=True), carry.shape)

    def idx(b, j):
        return (b * nt + (nt - 1 - j if reverse else j), 0)

    return pl.pallas_call(
        body, name=name, grid=(lay.batch, nt),
        in_specs=[pl.BlockSpec((t, c), idx)], out_specs=pl.BlockSpec((t, c), idx),
        out_shape=jax.ShapeDtypeStruct(x.shape, F32),
        scratch_shapes=[pltpu.VMEM((8, c), F32)],
        compiler_params=_params(2),
    )(x)


def _group_cumsum(x, tile, *, reverse):
    r = lax.rem(lax.broadcasted_iota(jnp.int32, (tile, 1), 0), SUB)
    s = 1
    while s < SUB:
        if reverse:
            x = x + jnp.where(r < SUB - s, pltpu.roll(x, tile - s, 0), 0.0)
        else:
            x = x + jnp.where(r >= s, pltpu.roll(x, s, 0), 0.0)
        s *= 2
    return x


AUG = 128
FOX_BK = 256
FOX_BQ = 256
FOX_SCALE = FOX_DIM ** -0.5
KT_ROWS = FOX_DIM + 16


def _aug_matrices():
    e1 = np.zeros((FOX_W, FOX_HEADS * AUG), np.float32)
    e2 = np.zeros((LANES, FOX_HEADS * AUG), np.float32)
    ones = np.zeros((1, FOX_HEADS * AUG), np.float32)
    for h in range(FOX_HEADS):
        for d in range(FOX_DIM):
            e1[h * FOX_DIM + d, h * AUG + d] = 1.0
        for j in range(3):
            e2[j * FOX_HEADS + h, h * AUG + FOX_DIM + j] = 1.0
            ones[0, h * AUG + FOX_DIM + j] = 1.0
    return jnp.asarray(e1, MXU_DTYPE), jnp.asarray(e2, MXU_DTYPE), jnp.asarray(ones)


def _fox_augment(q, k, cum, key_ok, e1, e2, ones):
    dt = q.dtype
    c = jnp.where(key_ok, -cum, NEG)
    hi = c.astype(dt)
    r1 = c - hi.astype(F32)
    mid = r1.astype(dt)
    lo = (r1 - mid.astype(F32)).astype(dt)
    lane = lax.broadcasted_iota(jnp.int32, c.shape, 1)
    shift = lambda a, by: pltpu.roll(a.astype(F32), by, 1)
    parts = jnp.where(lane < FOX_HEADS, hi.astype(F32),
                      jnp.where(lane < 2 * FOX_HEADS, shift(mid, FOX_HEADS),
                                jnp.where(lane < 3 * FOX_HEADS, shift(lo, 2 * FOX_HEADS), 0.0))).astype(dt)
    qs = (q.astype(F32) * FOX_SCALE).astype(dt)
    q_aug = jnp.dot(qs, e1, preferred_element_type=F32) + ones
    k_aug = jnp.dot(k, e1, preferred_element_type=F32) + jnp.dot(parts, e2, preferred_element_type=F32)
    return q_aug.astype(dt), k_aug.astype(dt)


def _fox_tile(k_blk, q_blk, k0, q0, masked):
    st = lax.dot_general(k_blk, q_blk, (((1,), (1,)), ((), ())), preferred_element_type=F32)
    if masked:
        keys = k0 + lax.broadcasted_iota(jnp.int32, st.shape, 0)
        qs = q0 + lax.broadcasted_iota(jnp.int32, st.shape, 1)
        st = jnp.where(keys <= qs, st, NEG)
    return st


def _fox_fwd_t(q_aug, k_aug, v_t, lay):
    bk, bq = FOX_BK, FOX_BQ
    nq = lay.lp // bq
    pairs = FOX_HEADS // 2

    def body(q_ref, k_ref, vt_ref, ot_ref, lse_ref, zeros_ref):
        heads = [(slice(hh * AUG, (hh + 1) * AUG), slice(hh * FOX_DIM, (hh + 1) * FOX_DIM)) for hh in range(2)]
        zeros_ref[...] = jnp.zeros_like(zeros_ref)

        def q_loop(qb, _):
            q0 = pl.multiple_of(qb * bq, bq)
            q_blks = [q_ref[pl.ds(q0, bq), lanes] for lanes, _ in heads]

            def scores(kb, h):
                k0 = pl.multiple_of(kb * bk, bk)
                return _fox_tile(k_ref[pl.ds(k0, bk), heads[h][0]], q_blks[h], k0, q0, False)

            def consume(kb, h, state, masked):
                m, l, acc, pend, st = state
                k0 = pl.multiple_of(kb * bk, bk)
                if masked:
                    keys = k0 + lax.broadcasted_iota(jnp.int32, st.shape, 0)
                    qs_ = q0 + lax.broadcasted_iota(jnp.int32, st.shape, 1)
                    st = jnp.where(keys <= qs_, st, NEG)
                m_new = jnp.maximum(m, jnp.max(st, axis=0, keepdims=True))
                alpha = jnp.exp(m - m_new)
                p = jnp.exp(st - m_new)
                l = alpha * l + jnp.sum(p, axis=0, keepdims=True)
                acc = alpha * (acc + pend)
                pend = jnp.dot(vt_ref[heads[h][1], pl.ds(k0, bk)], p.astype(vt_ref.dtype), preferred_element_type=F32)
                return m_new, l, acc, pend

            def k_step(kb, states):
                nxt = [scores(kb + 1, h) for h in range(2)]
                return tuple(consume(kb, h, states[h], False) + (nxt[h],) for h in range(2))

            states = tuple((jnp.full((1, bq), NEG, F32), jnp.zeros((1, bq), F32), zeros_ref[...], zeros_ref[...], scores(0, h))
                           for h in range(2))
            states = lax.fori_loop(0, qb, k_step, states)
            qs = q0 + lax.broadcasted_iota(jnp.int32, (1, bq), 1)
            ok = (qs >= LEAD) & (qs < LEAD + lay.l_real)
            for hh in range(2):
                m, l, acc, pend = consume(qb, hh, states[hh], True)
                ot_ref[heads[hh][1], pl.ds(q0, bq)] = jnp.where(ok, (acc + pend) / l, 0.0).astype(ot_ref.dtype)
                lse_ref[hh, :, pl.ds(q0, bq)] = m + jnp.log(l)
            return 0

        lax.fori_loop(0, nq, q_loop, 0)

    aug = pl.BlockSpec((lay.lp, 2 * AUG), lambda b, p: (b, p))
    tr = pl.BlockSpec((2 * FOX_DIM, lay.lp), lambda b, p: (p, b))
    return pl.pallas_call(
        body, name="fox_fwd", grid=(lay.batch, pairs),
        in_specs=[aug, aug, tr],
        out_specs=[tr, pl.BlockSpec((2, 1, lay.lp), lambda b, p: (b * pairs + p, 0, 0))],
        out_shape=[jax.ShapeDtypeStruct((FOX_W, lay.n), MXU_DTYPE),
                   jax.ShapeDtypeStruct((lay.batch * FOX_HEADS, 1, lay.lp), F32)],
        scratch_shapes=[pltpu.VMEM((FOX_DIM, bq), F32)],
        compiler_params=_params(2),
    )(q_aug, k_aug, v_t)


def _fox_bwd_t(q_aug, k_aug, v, do, k_t, o_t, do_t, lse, lay):
    bk, bq = FOX_BK, FOX_BQ
    nq, nk = lay.lp // bq, lay.lp // bk
    pairs = FOX_HEADS // 2

    def body(q_ref, k_ref, v_ref, do_ref, kt_ref, ot_ref, dot_ref, lse_ref, dqt_ref, dk_ref, dv_ref, delta):
        dqt_ref[...] = jnp.zeros_like(dqt_ref)
        dk_ref[...] = jnp.zeros_like(dk_ref)
        dv_ref[...] = jnp.zeros_like(dv_ref)
        heads = [(hh, slice(hh * AUG, (hh + 1) * AUG), slice(hh * FOX_DIM, (hh + 1) * FOX_DIM),
                  slice(hh * KT_ROWS, (hh + 1) * KT_ROWS)) for hh in range(2)]

        def delta_loop(qb, _):
            q0 = pl.multiple_of(qb * bq, bq)
            for hh, _, cols, _ in heads:
                prod = ot_ref[cols, pl.ds(q0, bq)].astype(F32) * dot_ref[cols, pl.ds(q0, bq)].astype(F32)
                delta[hh, :, pl.ds(q0, bq)] = jnp.sum(prod, axis=0, keepdims=True)
            return 0

        lax.fori_loop(0, nq, delta_loop, 0)

        def k_loop(kb, _):
            k0 = pl.multiple_of(kb * bk, bk)

            def products(qb, h):
                q0 = pl.multiple_of(qb * bq, bq)
                _, lanes, cols, _ = heads[h]
                st = _fox_tile(k_ref[pl.ds(k0, bk), lanes], q_ref[pl.ds(q0, bq), lanes], k0, q0, False)
                dpt = lax.dot_general(v_ref[pl.ds(k0, bk), cols], do_ref[pl.ds(q0, bq), cols], (((1,), (1,)), ((), ())),
                                      preferred_element_type=F32)
                return st, dpt

            def consume(qb, h, st, dpt, masked):
                q0 = pl.multiple_of(qb * bq, bq)
                hh, lanes, cols, trows = heads[h]
                if masked:
                    keys = k0 + lax.broadcasted_iota(jnp.int32, st.shape, 0)
                    qs = q0 + lax.broadcasted_iota(jnp.int32, st.shape, 1)
                    st = jnp.where(keys <= qs, st, NEG)
                q_blk = q_ref[pl.ds(q0, bq), lanes]
                do_blk = do_ref[pl.ds(q0, bq), cols]
                pt = jnp.exp(st - lse_ref[hh, :, pl.ds(q0, bq)])
                dst = (pt * (dpt - delta[hh, :, pl.ds(q0, bq)])).astype(q_blk.dtype)
                dv_ref[pl.ds(k0, bk), cols] += jnp.dot(pt.astype(do_blk.dtype), do_blk, preferred_element_type=F32)
                dk_ref[pl.ds(k0, bk), lanes] += jnp.dot(dst, q_blk, preferred_element_type=F32)
                dqt_ref[trows, pl.ds(q0, bq)] += jnp.dot(kt_ref[trows, pl.ds(k0, bk)], dst, preferred_element_type=F32)

            after = lambda qb: jnp.minimum(qb + 1, nq - 1)
            cur = [products(kb, h) for h in range(2)]
            nxt = tuple(products(after(kb), h) for h in range(2))
            for h in range(2):
                consume(kb, h, *cur[h], True)

            def rest(qb, held):
                new = tuple(products(after(qb), h) for h in range(2))
                for h in range(2):
                    consume(qb, h, *held[h], False)
                return new

            lax.fori_loop(kb + 1, nq, rest, nxt)
            return 0

        lax.fori_loop(0, nk, k_loop, 0)

    aug = pl.BlockSpec((lay.lp, 2 * AUG), lambda b, p: (b, p))
    rows = pl.BlockSpec((lay.lp, 2 * FOX_DIM), lambda b, p: (b, p))
    tr = pl.BlockSpec((2 * FOX_DIM, lay.lp), lambda b, p: (p, b))
    tr_k = pl.BlockSpec((2 * KT_ROWS, lay.lp), lambda b, p: (p, b))
    return pl.pallas_call(
        body, name="fox_bwd", grid=(lay.batch, pairs),
        in_specs=[aug, aug, rows, rows, tr_k, tr, tr, pl.BlockSpec((2, 1, lay.lp), lambda b, p: (b * pairs + p, 0, 0))],
        out_specs=[tr_k, aug, rows],
        out_shape=[jax.ShapeDtypeStruct((FOX_HEADS * KT_ROWS, lay.n), F32), jax.ShapeDtypeStruct((lay.n, FOX_HEADS * AUG), F32),
                   jax.ShapeDtypeStruct((lay.n, FOX_W), F32)],
        scratch_shapes=[pltpu.VMEM((2, 1, lay.lp), F32)],
        compiler_params=_params(2),
    )(q_aug, k_aug, v, do, k_t, o_t, do_t, lse)


def _hgrn_fwd(proj, kk, gl, lay):
    t = lay.tile
    nt = lay.lp // t
    nsc = t // SUB

    def body(q_ref, k_ref, g_ref, v_ref, o_ref, st_ref, state, sub_rows):
        @pl.when(pl.program_id(1) == 0)
        def _():
            state[...] = jnp.zeros_like(state)

        rowi = lax.broadcasted_iota(jnp.int32, (SUB, 1), 0)

        def sub(sc, _):
            r0 = pl.multiple_of(sc * SUB, SUB)
            sub_rows[0] = k_ref[pl.ds(r0, SUB), :]
            sub_rows[1] = g_ref[pl.ds(r0, SUB), :]
            sub_rows[2] = v_ref[pl.ds(r0, SUB), :]
            for h in range(HG_HEADS):
                lanes = slice(h * HG_DIM, (h + 1) * HG_DIM)
                q16 = q_ref[pl.ds(r0, SUB), lanes]
                k16 = sub_rows[0, :, lanes]
                g16 = sub_rows[1, :, lanes]
                v16 = sub_rows[2, :, lanes]
                g_end = sub_rows[1, SUB - 1:SUB, lanes]
                s_prev = state[h]
                st_ref[sc, h] = s_prev
                o = lax.dot_general((q16 * jnp.exp(g16)).astype(MXU_DTYPE), s_prev.astype(MXU_DTYPE),
                                    (((1,), (1,)), ((), ())), preferred_element_type=F32)
                for s in range(SUB):
                    ks = sub_rows[0, s:s + 1, lanes]
                    gs = sub_rows[1, s:s + 1, lanes]
                    vs = sub_rows[2, s:s + 1, lanes]
                    w = q16 * jnp.exp(jnp.minimum(g16 - gs, 0.0)) * ks
                    a = jnp.where(rowi >= s, jnp.sum(w, axis=1, keepdims=True), 0.0)
                    o = o + a * vs
                o_ref[pl.ds(r0, SUB), lanes] = o
                kt = k16 * jnp.exp(g_end - g16)
                upd = lax.dot_general(v16.astype(MXU_DTYPE), kt.astype(MXU_DTYPE), (((0,), (0,)), ((), ())),
                                      preferred_element_type=F32)
                state[h] = jnp.exp(g_end) * s_prev + upd
            return 0

        lax.fori_loop(0, nsc, sub, 0)

    rows = lambda col: pl.BlockSpec((t, HG_W), functools.partial(lambda b, i, col: (b * nt + i, col), col=col))
    return pl.pallas_call(
        body, name="hgrn_fwd", grid=(lay.batch, nt),
        in_specs=[rows(C_HQ), rows(0), rows(0), rows(C_HI)],
        out_specs=[rows(0), pl.BlockSpec((nsc, HG_HEADS, HG_DIM, HG_DIM), lambda b, i: (b * nt + i, 0, 0, 0))],
        out_shape=[jax.ShapeDtypeStruct((lay.n, HG_W), F32),
                   jax.ShapeDtypeStruct((lay.n // SUB, HG_HEADS, HG_DIM, HG_DIM), F32)],
        scratch_shapes=[pltpu.VMEM((HG_HEADS, HG_DIM, HG_DIM), F32), pltpu.VMEM((3, SUB, HG_W), F32)],
        compiler_params=_params(2),
    )(proj, kk, gl, proj)


def _hgrn_bwd(proj, kk, gl, do, states, lay):
    t = lay.tile
    nt = lay.lp // t
    nsc = t // SUB

    def body(q_ref, k_ref, g_ref, v_ref, do_ref, st_ref, dq_ref, dk_ref, dv_ref, dg_ref, dstate, sub_rows, row_acc):
        @pl.when(pl.program_id(1) == 0)
        def _():
            dstate[...] = jnp.zeros_like(dstate)

        rowi = lax.broadcasted_iota(jnp.int32, (SUB, 1), 0)

        def sub(it, _):
            sc = nsc - 1 - it
            r0 = pl.multiple_of(sc * SUB, SUB)
            sub_rows[0] = k_ref[pl.ds(r0, SUB), :]
            sub_rows[1] = g_ref[pl.ds(r0, SUB), :]
            sub_rows[2] = v_ref[pl.ds(r0, SUB), :]
            for h in range(HG_HEADS):
                lanes = slice(h * HG_DIM, (h + 1) * HG_DIM)
                q16 = q_ref[pl.ds(r0, SUB), lanes]
                k16 = sub_rows[0, :, lanes]
                g16 = sub_rows[1, :, lanes]
                v16 = sub_rows[2, :, lanes]
                do16 = do_ref[pl.ds(r0, SUB), lanes]
                g_end = sub_rows[1, SUB - 1:SUB, lanes]
                s_prev = st_ref[sc, h]
                ds_end = dstate[h]
                eg = jnp.exp(g16)
                ekt = jnp.exp(g_end - g16)
                e_end = jnp.exp(g_end)
                qt = q16 * eg
                kt = k16 * ekt
                ds_mx = ds_end.astype(MXU_DTYPE)
                dv = lax.dot_general(kt.astype(MXU_DTYPE), ds_mx, (((1,), (1,)), ((), ())), preferred_element_type=F32)
                dkt = jnp.dot(v16.astype(MXU_DTYPE), ds_mx, preferred_element_type=F32)
                dk = dkt * ekt
                ktdkt = kt * dkt
                dg_end = jnp.sum(ktdkt, axis=0, keepdims=True) + jnp.sum(s_prev * ds_end, axis=0, keepdims=True) * e_end
                dg = jnp.where(rowi == SUB - 1, dg_end, 0.0) - ktdkt
                dqt = jnp.dot(do16.astype(MXU_DTYPE), s_prev.astype(MXU_DTYPE), preferred_element_type=F32)
                dq = dqt * eg
                dg = dg + qt * dqt
                dstate[h] = e_end * ds_end + lax.dot_general(do16.astype(MXU_DTYPE), qt.astype(MXU_DTYPE),
                                                             (((0,), (0,)), ((), ())), preferred_element_type=F32)
                for s in range(SUB):
                    ks = sub_rows[0, s:s + 1, lanes]
                    gs = sub_rows[1, s:s + 1, lanes]
                    vs = sub_rows[2, s:s + 1, lanes]
                    live = rowi >= s
                    e = jnp.where(live, jnp.exp(jnp.minimum(g16 - gs, 0.0)), 0.0)
                    qe = q16 * e
                    a = jnp.sum(qe * ks, axis=1, keepdims=True)
                    da = jnp.where(live, jnp.sum(do16 * vs, axis=1, keepdims=True), 0.0)
                    t1 = da * qe
                    dk_row = jnp.sum(t1, axis=0, keepdims=True)
                    dq = dq + da * (e * ks)
                    dg = dg + t1 * ks
                    row_acc[0, s:s + 1, :] = jnp.sum(a * do16, axis=0, keepdims=True)
                    row_acc[1, s:s + 1, :] = dk_row
                    row_acc[2, s:s + 1, :] = ks * dk_row
                dq_ref[pl.ds(r0, SUB), lanes] = dq
                dk_ref[pl.ds(r0, SUB), lanes] = dk + row_acc[1]
                dv_ref[pl.ds(r0, SUB), lanes] = dv + row_acc[0]
                dg_ref[pl.ds(r0, SUB), lanes] = dg - row_acc[2]
            return 0

        lax.fori_loop(0, nsc, sub, 0)

    def rows(col):
        return pl.BlockSpec((t, HG_W), functools.partial(lambda b, i, col: (b * nt + nt - 1 - i, col), col=col))

    out = jax.ShapeDtypeStruct((lay.n, HG_W), F32)
    return pl.pallas_call(
        body, name="hgrn_bwd", grid=(lay.batch, nt),
        in_specs=[rows(C_HQ), rows(0), rows(0), rows(C_HI), rows(0),
                  pl.BlockSpec((nsc, HG_HEADS, HG_DIM, HG_DIM), lambda b, i: (b * nt + nt - 1 - i, 0, 0, 0))],
        out_specs=[rows(0)] * 4, out_shape=[out] * 4,
        scratch_shapes=[pltpu.VMEM((HG_HEADS, HG_DIM, HG_DIM), F32), pltpu.VMEM((3, SUB, HG_W), F32),
                        pltpu.VMEM((3, SUB, HG_DIM), F32)],
        compiler_params=_params(2),
    )(proj, kk, gl, proj, do, states)


CONV_COLS = 1408


def _shift_down(x, halo, tile, by):
    out = pltpu.roll(x, by, 0)
    rowi = lax.broadcasted_iota(jnp.int32, (8, 1), 0)
    top = out[0:8]
    for r in range(by):
        top = jnp.where(rowi == r, halo[8 - by + r:8 - by + r + 1, :], top)
    return jnp.concatenate([top, out[8:]], axis=0)


def _shift_up(x, halo, tile, by):
    out = pltpu.roll(x, tile - by, 0)
    rowi = lax.broadcasted_iota(jnp.int32, (8, 1), 0)
    bottom = out[tile - 8:]
    for r in range(by):
        bottom = jnp.where(rowi == 8 - by + r, halo[r:r + 1, :], bottom)
    return jnp.concatenate([out[:tile - 8], bottom], axis=0)


def _conv_specs(tile):
    ncb = D_FF // CONV_COLS
    per8 = tile // 8

    def tile_spec(off):
        return pl.BlockSpec((tile, CONV_COLS), functools.partial(lambda i, j, off: (i, j + off), off=off))

    def prev_spec(off):
        return pl.BlockSpec((8, CONV_COLS), functools.partial(lambda i, j, off: (jnp.maximum(i * per8 - 1, 0), j + off), off=off))

    def w_spec(off):
        return pl.BlockSpec((3, CONV_COLS), functools.partial(lambda i, j, off: (0, j + off), off=off))

    def b_spec(off):
        return pl.BlockSpec((1, CONV_COLS), functools.partial(lambda i, j, off: (0, j + off), off=off))

    return ncb, tile_spec, prev_spec, w_spec, b_spec


def _conv3(x, halo, w, b, tile):
    return w[0:1, :] * _shift_down(x, halo, tile, 2) + w[1:2, :] * _shift_down(x, halo, tile, 1) + w[2:3, :] * x + b


def _conv_act_fwd(u, conv_w, conv_b, lay):
    tile = lay.tile
    ncb, tile_spec, prev_spec, w_spec, b_spec = _conv_specs(tile)

    def body(ug, uv, pg, pv, wg, wv, bg, bv, o_ref):
        cg = _conv3(ug[...], pg, wg, bg[...], tile)
        cv = _conv3(uv[...], pv, wv, bv[...], tile)
        o_ref[...] = (_silu(cg) * cv).astype(o_ref.dtype)

    return pl.pallas_call(
        body, name="conv_act_fwd", grid=(lay.n // tile, ncb),
        in_specs=[tile_spec(0), tile_spec(ncb), prev_spec(0), prev_spec(ncb), w_spec(0), w_spec(ncb), b_spec(0), b_spec(ncb)],
        out_specs=pl.BlockSpec((tile, CONV_COLS), lambda i, j: (i, j)),
        out_shape=jax.ShapeDtypeStruct((lay.n, D_FF), MXU_DTYPE),
        compiler_params=_params(2),
    )(u, u, u, u, conv_w, conv_w, conv_b, conv_b)


def _conv_act_bwd(u, dact, conv_w, conv_b, lay):
    tile = lay.tile
    ncb, tile_spec, prev_spec, w_spec, b_spec = _conv_specs(tile)

    def body(ug, uv, pg, pv, wg, wv, bg, bv, da_ref, dg_ref, dv_ref, gwg, gwv, gbg, gbv):
        @pl.when(pl.program_id(1) == 0)
        def _():
            for r in (gwg, gwv, gbg, gbv):
                r[...] = jnp.zeros_like(r)

        xg, xv = ug[...], uv[...]
        cg = _conv3(xg, pg, wg, bg[...], tile)
        cv = _conv3(xv, pv, wv, bv[...], tile)
        da = da_ref[...].astype(F32)
        sg = _sigmoid(cg)
        dcv = da * (cg * sg)
        dcg = da * cv * (sg * (1.0 + cg * (1.0 - sg)))
        dg_ref[...] = dcg
        dv_ref[...] = dcv
        for x, halo, dc, gw, gb in ((xg, pg, dcg, gwg, gbg), (xv, pv, dcv, gwv, gbv)):
            gw[0, 0:1, :] += jnp.sum(dc * _shift_down(x, halo, tile, 2), axis=0, keepdims=True)
            gw[0, 1:2, :] += jnp.sum(dc * _shift_down(x, halo, tile, 1), axis=0, keepdims=True)
            gw[0, 2:3, :] += jnp.sum(dc * x, axis=0, keepdims=True)
            gb[0] += jnp.sum(dc, axis=0, keepdims=True)

    swap = lambda spec: pl.BlockSpec(spec.block_shape, functools.partial(lambda j, i, f: f(i, j), f=spec.index_map))
    col = lambda j, i: (i, j)
    red_w = pl.BlockSpec((1, 3, CONV_COLS), lambda j, i: (j, 0, 0))
    red_b = pl.BlockSpec((1, 1, CONV_COLS), lambda j, i: (j, 0, 0))
    outs = pl.pallas_call(
        body, name="conv_act_bwd", grid=(ncb, lay.n // tile),
        in_specs=[swap(s) for s in (tile_spec(0), tile_spec(ncb), prev_spec(0), prev_spec(ncb), w_spec(0), w_spec(ncb),
                                    b_spec(0), b_spec(ncb))] + [pl.BlockSpec((tile, CONV_COLS), col)],
        out_specs=[pl.BlockSpec((tile, CONV_COLS), col), pl.BlockSpec((tile, CONV_COLS), col), red_w, red_w, red_b, red_b],
        out_shape=[jax.ShapeDtypeStruct((lay.n, D_FF), F32), jax.ShapeDtypeStruct((lay.n, D_FF), F32),
                   jax.ShapeDtypeStruct((ncb, 3, CONV_COLS), F32), jax.ShapeDtypeStruct((ncb, 3, CONV_COLS), F32),
                   jax.ShapeDtypeStruct((ncb, 1, CONV_COLS), F32), jax.ShapeDtypeStruct((ncb, 1, CONV_COLS), F32)],
        compiler_params=_params(2),
    )(u, u, u, u, conv_w, conv_w, conv_b, conv_b, dact)
    dcg, dcv, gwg, gwv, gbg, gbv = outs
    unblock = lambda g: jnp.transpose(g, (1, 0, 2)).reshape(g.shape[1], D_FF)
    g_w = jnp.concatenate([unblock(gwg), unblock(gwv)], axis=1)
    g_b = jnp.concatenate([unblock(gbg), unblock(gbv)], axis=1)
    return dcg, dcv, g_w, g_b


def _conv_input_bwd(dcg, dcv, conv_w, lay):
    tile = lay.tile
    ncb = D_FF // CONV_COLS
    nblk8 = lay.n // 8
    per8 = tile // 8
    nxt = lambda i: jnp.minimum((i + 1) * per8, nblk8 - 1)

    def half(dc, off, into, name):
        def body(*refs):
            d, halo, w, o = refs[0], refs[1], refs[2], refs[-1]
            x = d[...]
            du = w[2:3, :] * x + w[1:2, :] * _shift_up(x, halo, tile, 1) + w[0:1, :] * _shift_up(x, halo, tile, 2)
            o[...] = jnp.where(lay.valid(pl.program_id(0), tile), du, 0.0).astype(o.dtype)

        in_specs = [pl.BlockSpec((tile, CONV_COLS), lambda i, j: (i, j)),
                    pl.BlockSpec((8, CONV_COLS), lambda i, j: (nxt(i), j)),
                    pl.BlockSpec((3, CONV_COLS), lambda i, j: (0, j + off))]
        args = [dc, dc, conv_w]
        if into is not None:
            in_specs.append(pl.BlockSpec(memory_space=pltpu.HBM))
            args.append(into)
        return pl.pallas_call(
            body, name=name, grid=(lay.n // tile, ncb), in_specs=in_specs,
            out_specs=pl.BlockSpec((tile, CONV_COLS), lambda i, j: (i, j + off)),
            out_shape=jax.ShapeDtypeStruct((lay.n, FF2), MXU_DTYPE),
            input_output_aliases={} if into is None else {3: 0},
            compiler_params=_params(2),
        )(*args)

    return half(dcv, ncb, half(dcg, 0, None, "conv_input_bwd_gate"), "conv_input_bwd_value")


def _loss_head(h1, mlp, target, lay):
    t, sub = 256, ROW0
    per = lay.lp // t
    nsub = t // sub
    nreal = lay.seq // sub

    def body(h_ref, m_ref, *rest):
        t_refs, (loss_ref, dy_ref, dyb_ref) = rest[:nsub], rest[nsub:]
        b, j = pl.program_id(0), pl.program_id(1)

        @pl.when((b == 0) & (j == 0))
        def _():
            loss_ref[...] = jnp.zeros_like(loss_ref)

        rows_ = j * t + lax.broadcasted_iota(jnp.int32, (t, 1), 0)
        real = (rows_ >= ROW0) & (rows_ < ROW0 + lay.seq)
        tgt_ = jnp.concatenate([r[...] for r in t_refs], axis=0)
        err = jnp.where(real, h_ref[...] + m_ref[...] - tgt_, 0.0)
        dy = err * (1.0 / D_MODEL)
        dy_ref[...] = dy
        dyb_ref[...] = dy.astype(dyb_ref.dtype)
        loss_ref[...] += 0.5 * jnp.sum(err * dy)

    rows = pl.BlockSpec((t, D_MODEL), lambda b, j: (b * per + j, 0))
    tgt = [pl.BlockSpec((sub, D_MODEL), functools.partial(
        lambda b, j, r: (b * nreal + jnp.clip(j * nsub + r - 1, 0, nreal - 1), 0), r=r)) for r in range(nsub)]
    return pl.pallas_call(
        body, name="loss_head", grid=(lay.batch, per),
        in_specs=[rows, rows] + tgt,
        out_specs=[pl.BlockSpec((8, LANES), lambda b, j: (0, 0)), rows, rows],
        out_shape=[jax.ShapeDtypeStruct((8, LANES), F32), jax.ShapeDtypeStruct((lay.n, D_MODEL), F32),
                   jax.ShapeDtypeStruct((lay.n, D_MODEL), MXU_DTYPE)],
        compiler_params=_params(2),
    )(h1, mlp, *([target] * nsub))


def _fox_prep(fq, fk, ff, gq, gk, bf, gmat, gmat_t, valid):
    q = _group_rms(fq, gq, gmat, gmat_t, FOX_DIM)
    k = _group_rms(fk, gk, gmat, gmat_t, FOX_DIM)
    logf = jnp.where(valid, _log_sigmoid(ff + bf), 0.0)
    return q, k, logf


def _hg_prep(hf, l0, l1):
    mx = jnp.maximum(l0, l1)
    e0, e1 = jnp.exp(l0 - mx), jnp.exp(l1 - mx)
    lb = e0 / (e0 + e1)
    lf = jnp.log(lb + (1.0 - lb) * _sigmoid(hf))
    kk = (1.0 - lb) * _sigmoid(-hf)
    return lf, kk


def _hg_post(o, hg, gain):
    return _head_rms(o, gain) * _silu(hg)


def _gate(ga, gb, ya, yb):
    return _sigmoid(ga) * ya + _sigmoid(gb) * yb


def _by_chip(g):
    return jnp.transpose(g.reshape(g.shape[0], N_CHIPS, g.shape[1] // N_CHIPS), (1, 0, 2))


def _from_chips(a):
    return jnp.transpose(a, (1, 0, 2)).reshape(a.shape[1], N_CHIPS * a.shape[2])


def _local_step(x, target, w, lay):
    n, tile = lay.n, lay.tile
    rw = functools.partial(_rowwise, n_rows=n, tile=tile)
    mx = lambda a: a.astype(MXU_DTYPE)

    w_in = w["w_in"]
    fq, fk, fv, ffw, hq, hf, hi, hg, ga, gb = jnp.split(w_in, list(np.cumsum([512, 512, 512, 8, 512, 512, 512, 512, 1024])), axis=1)
    w_main = mx(jnp.concatenate([ga, gb, fq, fk, fv, hq, hi, hf, hg], axis=1))
    w_ff = mx(jnp.pad(ffw, ((0, 0), (0, LANES - FOX_HEADS))))
    w_a, w_b, w_out, w_up, w_down = mx(w["w_branch_a"]), mx(w["w_branch_b"]), mx(w["w_out"]), mx(w["w_up"]), mx(w["w_down"])
    conv_w, conv_b = w["conv_w"].astype(F32), w["conv_b"].astype(F32)
    g1, g2 = w["norm1_gain"], w["norm2_gain"]
    gq, gk = jnp.tile(w["q_norm_gain"], (1, FOX_HEADS)), jnp.tile(w["k_norm_gain"], (1, FOX_HEADS))
    bf = jnp.pad(w["fox_b_f"], ((0, 0), (0, LANES - FOX_HEADS)))
    lb_logits, hg_gain = w["hg_lb_logits"], w["hg_out_gain"]
    gm64, gm64_t = _group_matrix(FOX_W, FOX_DIM)

    meta = jnp.broadcast_to(w["meta_tokens"].astype(F32)[None], (lay.batch, N_META, D_MODEL))
    h0 = jnp.concatenate([jnp.zeros((lay.batch, LEAD, D_MODEL), F32), meta, x,
                          jnp.zeros((lay.batch, lay.lp - LEAD - lay.l_real, D_MODEL), F32)], axis=1).reshape(n, D_MODEL)

    (xn,) = rw(lambda i, h, g: _rms(h, g), [h0], [g1], [(D_MODEL, MXU_DTYPE)], [], name="norm1")
    proj = _matmul(xn, w_main, name="proj_main")
    pff = _matmul(xn, w_ff, name="proj_ff")

    def fox_prep_fn(i, a, b_, v_, f_, gq_, gk_, bf_, m_, mt_):
        q_, k_, logf = _fox_prep(a, b_, f_, gq_, gk_, bf_, m_, mt_, lay.valid(i, tile))
        return q_, k_, v_, logf

    q, k, v, logf = rw(fox_prep_fn, [(proj, 512, C_FQ), (proj, 512, C_FK), (proj, 512, C_FV), pff], [gq, gk, bf, gm64, gm64_t],
                       [(512, MXU_DTYPE), (512, MXU_DTYPE), (512, MXU_DTYPE), (LANES, F32)], [], name="fox_prep")
    cum = _cumsum_rows(logf, lay, reverse=False, name="fox_cum")
    e1, e2, aug_ones = _aug_matrices()
    q_aug, k_aug = rw(lambda i, q_, k_, c_, e1_, e2_, on_: _fox_augment(q_, k_, c_, lay.valid(i, tile), e1_, e2_, on_),
                      [q, k, cum], [e1, e2, aug_ones], [(FOX_HEADS * AUG, MXU_DTYPE)] * 2, [], name="fox_aug")
    o_t, lse = _fox_fwd_t(q_aug, k_aug, v.T, lay)

    def hg_prep_fn(i, hf_, l0, l1):
        lf, kk_ = _hg_prep(hf_, l0, l1)
        return kk_, _group_cumsum(lf, tile, reverse=False)

    lb0, lb1 = lb_logits[0:1], lb_logits[1:2]
    kk, gl = rw(hg_prep_fn, [(proj, 512, C_HF)], [lb0, lb1], [(512, F32), (512, F32)], [], name="hg_prep")
    o_hg, states = _hgrn_fwd(proj, kk, gl, lay)
    (oh,) = rw(lambda i, o, g_, gain: _hg_post(o, g_, gain), [o_hg, (proj, 512, C_HG)], [hg_gain], [(512, MXU_DTYPE)], [],
               name="hg_post")
    ya = _matmul(oh, w_a, out_dtype=MXU_DTYPE, name="branch_a")
    yb = _matmul(o_t, w_b, trans_a=True, out_dtype=MXU_DTYPE, name="branch_b")
    pga, pgb = (proj, 1024, C_GA), (proj, 1024, C_GB)
    gate_fn = lambda a, b_, c_, d_: _gate(a, b_, c_.astype(F32), d_.astype(F32))
    (merged,) = rw(lambda i, a, b_, c_, d_: gate_fn(a, b_, c_, d_), [pga, pgb, ya, yb], [], [(D_MODEL, MXU_DTYPE)], [], name="gate")
    mo = _matmul(merged, w_out, name="out_proj")
    h1, hn = rw(lambda i, h, m_, g: (h + m_, _rms(h + m_, g)), [h0, mo], [g2], [(D_MODEL, F32), (D_MODEL, MXU_DTYPE)], [],
                name="norm2")
    u = _matmul(hn, w_up, name="up_proj")
    act = _conv_act_fwd(u, conv_w, conv_b, lay)
    mlp = _matmul(act, w_down, name="down_proj")
    loss_blk, dy, dyb = _loss_head(h1, mlp, target.reshape(lay.batch * lay.seq, D_MODEL), lay)
    loss = loss_blk[0, 0]

    grads = {}
    dact = _matmul(dyb, w_down, trans_b=True, out_dtype=MXU_DTYPE, name="down_bwd_x")
    grads["w_down"] = _matmul(act, dyb, trans_a=True, name="down_bwd_w").reshape(N_CHIPS, D_FF // N_CHIPS, D_MODEL)
    dcg, dcv, grads["conv_w"], grads["conv_b"] = _conv_act_bwd(u, dact, conv_w, conv_b, lay)
    du = _conv_input_bwd(dcg, dcv, conv_w, lay)
    dhn = _matmul(du, w_up, trans_b=True, name="up_bwd_x")
    grads["w_up"] = _matmul(hn, du, trans_a=True, by_chip=True, name="up_bwd_w")

    def norm2_bwd(i, h, d_, dy_, g):
        _, vjp = jax.vjp(_rms, h, g)
        dh, dg = vjp(d_)
        return dh + dy_, dh + dy_, dg

    dh1, dh1b, grads["norm2_gain"] = rw(norm2_bwd, [h1, dhn, dy], [g2], [(D_MODEL, F32), (D_MODEL, MXU_DTYPE)], [(1, D_MODEL)],
                                        name="norm2_bwd")
    dmerged = _matmul(dh1b, w_out, trans_b=True, out_dtype=MXU_DTYPE, name="out_bwd_x")
    grads["w_out"] = _matmul(merged, dh1b, trans_a=True, name="out_bwd_w").reshape(N_CHIPS, D_MODEL // N_CHIPS, D_MODEL)

    def gate_bwd(i, a, b_, c_, d_, dm):
        _, vjp = jax.vjp(gate_fn, a, b_, c_, d_)
        da, db, dc, dd = vjp(dm.astype(F32))
        return jnp.concatenate([da, db], axis=1), dc, dd

    dproj, dya, dyb_ = rw(gate_bwd, [pga, pgb, ya, yb, dmerged], [], [(2 * D_MODEL, MXU_DTYPE)] + [(D_MODEL, MXU_DTYPE)] * 2, [],
                          name="gate_bwd", into=(MAIN_COLS, 0))
    doh = _matmul(dya, w_a, trans_b=True, out_dtype=MXU_DTYPE, name="branch_a_bwd_x")
    grads["w_branch_a"] = _matmul(oh, dya, trans_a=True, by_chip=True, name="branch_a_bwd_w")
    dofox = _matmul(dyb_, w_b, trans_b=True, out_dtype=MXU_DTYPE, name="branch_b_bwd_x")
    grads["w_branch_b"] = _matmul(o_t, dyb_, by_chip=True, name="branch_b_bwd_w")

    def hg_post_bwd(i, o, g_, d_, gain):
        _, vjp = jax.vjp(_hg_post, o, g_, gain)
        do_, dg_, dgain = vjp(d_.astype(F32))
        return dg_, do_, dgain

    dproj, do_hg, grads["hg_out_gain"] = rw(hg_post_bwd, [o_hg, (proj, 512, C_HG), doh], [hg_gain],
                                            [(512, MXU_DTYPE), (512, F32)], [(1, HG_DIM)], name="hg_post_bwd", into=(dproj, C_HG))
    dhq, dkk, dhi, dgl = _hgrn_bwd(proj, kk, gl, do_hg, states, lay)

    def hg_prep_bwd(i, hf_, dkk_, dgl_, l0, l1):
        _, vjp = jax.vjp(_hg_prep, hf_, l0, l1)
        return vjp((_group_cumsum(dgl_, tile, reverse=True), dkk_))

    dproj, g_lb0, g_lb1 = rw(hg_prep_bwd, [(proj, 512, C_HF), dkk, dgl], [lb0, lb1], [(512, MXU_DTYPE)], [(1, HG_W), (1, HG_W)],
                             name="hg_prep_bwd", into=(dproj, C_HF))
    grads["hg_lb_logits"] = jnp.concatenate([g_lb0, g_lb1], axis=0)

    k_t = (k.astype(F32) * FOX_SCALE).astype(MXU_DTYPE).T.reshape(FOX_HEADS, FOX_DIM, n)
    k_t = jnp.concatenate([k_t, jnp.ones((FOX_HEADS, KT_ROWS - FOX_DIM, n), MXU_DTYPE)], axis=1).reshape(FOX_HEADS * KT_ROWS, n)
    dq_t, dk_aug, dv = _fox_bwd_t(q_aug, k_aug, v, dofox, k_t, o_t, dofox.T, lse, lay)
    dq_t = dq_t.reshape(FOX_HEADS, KT_ROWS, n)
    dq = dq_t[:, :FOX_DIM].reshape(FOX_W, n).T
    dk_aug = dk_aug.reshape(n, FOX_HEADS, AUG)
    dk = dk_aug[:, :, :FOX_DIM].reshape(n, FOX_W)
    dcum = jnp.pad(dq_t[:, FOX_DIM].T - dk_aug[:, :, FOX_DIM], ((0, 0), (0, LANES - FOX_HEADS)))
    dlogf = _cumsum_rows(dcum, lay, reverse=True, name="fox_cum_bwd")

    def fox_prep_bwd(i, a, b_, f_, dq_, dk_, dl_, gq_, gk_, bf_, m_, mt_):
        valid = lay.valid(i, tile)
        _, vjp = jax.vjp(lambda a_, b__, f__, gq__, gk__, bf__: _fox_prep(a_, b__, f__, gq__, gk__, bf__, m_, mt_, valid),
                         a, b_, f_, gq_, gk_, bf_)
        da, db, df, dgq, dgk, dbf = vjp((dq_, dk_, dl_))
        return jnp.concatenate([da, db], axis=1), df, dgq, dgk, dbf

    dproj, dff, g_gq, g_gk, g_bf = rw(
        fox_prep_bwd, [(proj, 512, C_FQ), (proj, 512, C_FK), pff, dq, dk, dlogf], [gq, gk, bf, gm64, gm64_t],
        [(2 * FOX_W, MXU_DTYPE), (LANES, MXU_DTYPE)], [(1, FOX_W), (1, FOX_W), (1, LANES)], name="fox_prep_bwd",
        into=(dproj, C_FQ // 2))
    grads["q_norm_gain"] = g_gq.reshape(FOX_HEADS, FOX_DIM).sum(0, keepdims=True)
    grads["k_norm_gain"] = g_gk.reshape(FOX_HEADS, FOX_DIM).sum(0, keepdims=True)
    grads["fox_b_f"] = g_bf[:, :FOX_HEADS]

    (dproj,) = rw(lambda i, a, b_, c_: jnp.concatenate([a, b_, c_], axis=1), [dv, dhq, dhi], [], [(3 * 512, MXU_DTYPE)], [],
                  name="dproj_cast", into=(dproj, C_FV // 3))
    dxn = _matmul(dproj, w_main, trans_b=True, name="proj_bwd_x")
    dxn_ff = _matmul(dff, w_ff, trans_b=True, name="proj_ff_bwd_x")
    g_main = _matmul(xn, dproj, trans_a=True, name="proj_bwd_w")
    g_ff = _matmul(xn, dff, trans_a=True, name="proj_ff_bwd_w")[:, :FOX_HEADS]
    p = jnp.split(g_main, list(np.cumsum([1024, 1024] + [512] * 6)), axis=1)
    grads["w_in"] = _by_chip(jnp.concatenate([p[2], p[3], p[4], g_ff, p[5], p[7], p[6], p[8], p[0], p[1]], axis=1))

    per = lay.lp // tile

    def norm1_bwd(i, h, d1, d2, dh1_, g):
        _, vjp = jax.vjp(_rms, h, g)
        dh, dg = vjp(d1 + d2)
        dh = dh + dh1_
        dmeta = jnp.where(lax.rem(i, per) == 0, dh[LEAD:LEAD + N_META, :], 0.0)
        return dh, dg, dmeta

    dh0, grads["norm1_gain"], grads["meta_tokens"] = rw(norm1_bwd, [h0, dxn, dxn_ff, dh1], [g1], [(D_MODEL, F32)],
                                                       [(1, D_MODEL), (N_META, D_MODEL)], name="norm1_bwd")
    grad_x = dh0.reshape(lay.batch, lay.lp, D_MODEL)[:, ROW0:ROW0 + lay.seq]
    return loss, grad_x, grads


MESH = pl.DeviceIdType.MESH
HBM_SPEC = pl.BlockSpec(memory_space=pltpu.HBM)
WEIGHT_NAMES = ["meta_tokens", "norm1_gain", "w_in", "fox_b_f", "q_norm_gain", "k_norm_gain", "hg_lb_logits", "hg_out_gain",
                "w_branch_a", "w_branch_b", "w_out", "norm2_gain", "w_up", "conv_w", "conv_b", "w_down"]
BIG = ("w_in", "w_branch_a", "w_branch_b", "w_out", "w_up", "w_down")
BIG_COL_SHARDED = ("w_in", "w_branch_a", "w_branch_b", "w_up")
SMALL = tuple(n for n in WEIGHT_NAMES if n not in BIG)
SMALL_SHARDED = ("meta_tokens", "conv_w")
SMALL_ROWS = 144
GATHER_SMALL_ROWS = 80


def _position():
    return lax.axis_index("x"), lax.axis_index("y"), lax.axis_index("c")


def _other_chips(x, y):
    return [(1 - x, y), (x, 1 - y), (1 - x, 1 - y)]


def _scalar(v):
    return jnp.reshape(v, (1,)).astype(jnp.int32)


def _row_tile(rows, cols):
    width = -(-cols // LANES) * LANES * 4
    best = 8
    for d in range(8, rows + 1, 8):
        if rows % d == 0 and d * width <= (1 << 20):
            best = d
    return best


def _gather_shards(shards):
    na = len(shards)
    halves = [s.shape[0] // 2 for s in shards]

    def body(*refs):
        xs, outs, send_sems, recv_sems = refs[:na], refs[na:2 * na], refs[2 * na], refs[2 * na + 1]
        x, y, c = _position()
        sibling = (x, y, 1 - c)
        chips = _other_chips(x, y)

        def copy(a, k, block, to, src=None):
            dst = outs[a].at[4 * block[0] + 2 * block[1] + block[2]]
            return pltpu.make_async_remote_copy(src_ref=dst if src is None else src, dst_ref=dst, send_sem=send_sems.at[a, k],
                                                recv_sem=recv_sems.at[a, k], device_id=to, device_id_type=MESH)

        first = []
        for a in range(na):
            mine = xs[a].at[pl.ds(pl.multiple_of(c * halves[a], 8), halves[a]), :]
            first += [copy(a, j, (x, y, c), (*chip, c), src=mine) for j, chip in enumerate(chips)]
        for cp in first:
            cp.start()
        passed = []
        for j, chip in enumerate(chips):
            for a in range(na):
                copy(a, j, (*chip, c), (x, y, c)).wait_recv()
                cp = copy(a, 3 + j, (*chip, c), sibling)
                cp.start()
                passed.append(cp)
        for a in range(na):
            for j, chip in enumerate(chips):
                copy(a, 3 + j, (*chip, 1 - c), (x, y, c)).wait_recv()
        for cp in first + passed:
            cp.wait_send()

    return pl.pallas_call(
        body, name="gather_weights",
        out_shape=[jax.ShapeDtypeStruct((8, h, s.shape[1]), s.dtype) for h, s in zip(halves, shards)],
        in_specs=[HBM_SPEC] * na, out_specs=[HBM_SPEC] * na,
        scratch_shapes=[pltpu.SemaphoreType.DMA((na, 6)), pltpu.SemaphoreType.DMA((na, 6))],
    )(*shards)


def _sibling_exchange(gs):
    na = len(gs)
    halves = [g.shape[1] // 2 for g in gs]

    def body(*refs):
        srcs, gots, send_sems, recv_sems = refs[:na], refs[na:2 * na], refs[2 * na], refs[2 * na + 1]
        x, y, c = _position()
        copies = [pltpu.make_async_remote_copy(
            src_ref=srcs[a].at[:, pl.ds(pl.multiple_of((1 - c) * halves[a], 8), halves[a]), :], dst_ref=gots[a],
            send_sem=send_sems.at[a], recv_sem=recv_sems.at[a], device_id=(x, y, 1 - c), device_id_type=MESH) for a in range(na)]
        for cp in copies:
            cp.start()
        for cp in copies:
            cp.wait()

    return pl.pallas_call(
        body, name="reduce_sibling",
        out_shape=[jax.ShapeDtypeStruct((N_CHIPS, h, g.shape[2]), g.dtype) for h, g in zip(halves, gs)],
        in_specs=[HBM_SPEC] * na, out_specs=[HBM_SPEC] * na,
        scratch_shapes=[pltpu.SemaphoreType.DMA((na,)), pltpu.SemaphoreType.DMA((na,))],
    )(*gs)


def _chip_exchange(parts):
    na = len(parts)

    def body(*refs):
        srcs, gots, send_sems, recv_sems = refs[:na], refs[na:2 * na], refs[2 * na], refs[2 * na + 1]
        x, y, c = _position()
        mine = 2 * x + y
        chips = _other_chips(x, y)

        def copy(a, j):
            cx, cy = chips[j]
            return pltpu.make_async_remote_copy(src_ref=srcs[a].at[2 * cx + cy], dst_ref=gots[a].at[mine],
                                                send_sem=send_sems.at[a, j], recv_sem=recv_sems.at[a, j],
                                                device_id=(cx, cy, c), device_id_type=MESH)

        def arrival(a, j):
            cx, cy = chips[j]
            return pltpu.make_async_remote_copy(src_ref=srcs[a].at[mine], dst_ref=gots[a].at[2 * cx + cy],
                                                send_sem=send_sems.at[a, j], recv_sem=recv_sems.at[a, j],
                                                device_id=(cx, cy, c), device_id_type=MESH)

        sends = [copy(a, j) for a in range(na) for j in range(3)]
        for cp in sends:
            cp.start()
        for a in range(na):
            for j in range(3):
                arrival(a, j).wait_recv()
        for cp in sends:
            cp.wait_send()

    return pl.pallas_call(
        body, name="reduce_chips", out_shape=[jax.ShapeDtypeStruct(p.shape, p.dtype) for p in parts],
        in_specs=[HBM_SPEC] * na, out_specs=[HBM_SPEC] * na,
        scratch_shapes=[pltpu.SemaphoreType.DMA((na, 3)), pltpu.SemaphoreType.DMA((na, 3))],
    )(*parts)


def _sibling_send(halves):
    na = len(halves)

    def body(*refs):
        srcs, gots, send_sems, recv_sems = refs[:na], refs[na:2 * na], refs[2 * na], refs[2 * na + 1]
        x, y, c = _position()
        copies = [pltpu.make_async_remote_copy(src_ref=srcs[a], dst_ref=gots[a], send_sem=send_sems.at[a], recv_sem=recv_sems.at[a],
                                               device_id=(x, y, 1 - c), device_id_type=MESH) for a in range(na)]
        for cp in copies:
            cp.start()
        for cp in copies:
            cp.wait()

    return pl.pallas_call(
        body, name="reduce_gather", out_shape=[jax.ShapeDtypeStruct(h.shape, h.dtype) for h in halves],
        in_specs=[HBM_SPEC] * na, out_specs=[HBM_SPEC] * na,
        scratch_shapes=[pltpu.SemaphoreType.DMA((na,)), pltpu.SemaphoreType.DMA((na,))],
    )(*halves)


def _add_own_half(g, got, c, dtype, name):
    _, r, cols = g.shape
    r2 = r // 2
    tr = _row_tile(r2, cols)
    nrt = r2 // tr

    def body(c_ref, g_ref, got_ref, o_ref):
        o_ref[...] = (g_ref[...] + got_ref[...]).astype(o_ref.dtype)

    blk = (1, tr, cols)
    return pl.pallas_call(
        body, name=name,
        grid_spec=pltpu.PrefetchScalarGridSpec(
            num_scalar_prefetch=1, grid=(N_CHIPS, nrt),
            in_specs=[pl.BlockSpec(blk, lambda j, i, c_: (j, c_[0] * nrt + i, 0)), pl.BlockSpec(blk, lambda j, i, c_: (j, i, 0))],
            out_specs=pl.BlockSpec(blk, lambda j, i, c_: (j, i, 0))),
        out_shape=jax.ShapeDtypeStruct((N_CHIPS, r2, cols), dtype), compiler_params=_params(2),
    )(c, g, got)


def _add_chips(part, got, mine, name):
    _, r2, cols = part.shape
    tr = _row_tile(r2, cols)

    def body(m_ref, p_ref, g0, g1, g2, g3, o_ref):
        t = [jnp.where(m_ref[0] == k, p_ref[0], g[0]).astype(F32) for k, g in enumerate((g0, g1, g2, g3))]
        o_ref[...] = ((t[0] + t[1]) + t[2]) + t[3]

    blk = (1, tr, cols)
    others = [pl.BlockSpec(blk, functools.partial(lambda i, m, k: (jnp.where(m[0] == k, (k + 1) % N_CHIPS, k), i, 0), k=k))
              for k in range(N_CHIPS)]
    return pl.pallas_call(
        body, name=name,
        grid_spec=pltpu.PrefetchScalarGridSpec(
            num_scalar_prefetch=1, grid=(r2 // tr,),
            in_specs=[pl.BlockSpec(blk, lambda i, m: (m[0], i, 0))] + others,
            out_specs=pl.BlockSpec((tr, cols), lambda i, m: (i, 0))),
        out_shape=jax.ShapeDtypeStruct((r2, cols), F32), compiler_params=_params(1),
    )(mine, part, got, got, got, got)


def _adamw(w, own, other, m, v, c, name):
    r, cols = w.shape
    r2 = r // 2
    tr = _row_tile(r2, cols)
    nrt = r2 // tr
    c1 = 1.0 / (1.0 - ADAM_B1 ** ADAM_STEP)
    c2 = 1.0 / (1.0 - ADAM_B2 ** ADAM_STEP)

    def body(c_ref, w_ref, own_ref, other_ref, m_ref, v_ref, g_out, d_out, m_out, v_out):
        g_ = jnp.where(pl.program_id(0) == c_ref[0], own_ref[...], other_ref[...])
        m_new = ADAM_B1 * m_ref[...] + (1.0 - ADAM_B1) * g_
        v_new = ADAM_B2 * v_ref[...] + (1.0 - ADAM_B2) * (g_ * g_)
        g_out[...] = g_
        d_out[...] = -ADAM_LR * ((m_new * c1) / (jnp.sqrt(v_new * c2) + ADAM_EPS) + ADAM_WD * w_ref[...])
        m_out[...] = m_new
        v_out[...] = v_new

    full = pl.BlockSpec((tr, cols), lambda h, i, c_: (h * nrt + i, 0))
    half = pl.BlockSpec((tr, cols), lambda h, i, c_: (i, 0))
    out = jax.ShapeDtypeStruct((r, cols), F32)
    return pl.pallas_call(
        body, name=name,
        grid_spec=pltpu.PrefetchScalarGridSpec(num_scalar_prefetch=1, grid=(2, nrt), in_specs=[full, half, half, full, full],
                                               out_specs=[full] * 4),
        out_shape=[out] * 4, compiler_params=_params(2),
    )(c, w, own, other, m, v)


def _to_rows(flat, rows):
    return jnp.pad(flat, (0, rows * LANES - flat.shape[0])).reshape(rows, LANES)


def _pack_small(tree):
    return _to_rows(jnp.concatenate([tree[n].astype(F32).reshape(-1) for n in SMALL]), SMALL_ROWS)


def _unpack_small(packed, shapes):
    flat, out, at = packed.reshape(-1), {}, 0
    for n in SMALL:
        size = int(np.prod(shapes[n]))
        out[n] = flat[at:at + size].reshape(shapes[n])
        at += size
    return out


def _pack_small_by_chip(grads):
    pieces = []
    for n in SMALL:
        g = grads[n].astype(F32)
        if n in SMALL_SHARDED:
            pieces.append(_by_chip(g).reshape(N_CHIPS, -1))
        else:
            pieces.append(jnp.broadcast_to(g.reshape(1, -1), (N_CHIPS, g.size)))
    flat = jnp.concatenate(pieces, axis=1)
    return jnp.pad(flat, ((0, 0), (0, SMALL_ROWS * LANES - flat.shape[1]))).reshape(N_CHIPS, SMALL_ROWS, LANES)


def _gather_weights(local):
    shards = [local[n].reshape(local[n].shape[-2:]).astype(BF16) for n in BIG]
    shards.append(_to_rows(jnp.concatenate([local[n].astype(F32).reshape(-1) for n in SMALL_SHARDED]), GATHER_SMALL_ROWS))
    x, y, _ = _position()
    is_mine = (lax.broadcasted_iota(jnp.int32, (N_CHIPS, 1, 1), 0) == 2 * x + y)
    full = [jnp.where(is_mine, s[None], g.reshape((N_CHIPS,) + s.shape)) for s, g in zip(shards, _gather_shards(shards))]
    out = {}
    for n, f in zip(BIG, full):
        out[n] = _from_chips(f) if n in BIG_COL_SHARDED else f.reshape(N_CHIPS * f.shape[1], f.shape[2])
    flat, at = full[-1].reshape(N_CHIPS, -1), 0
    for n in SMALL_SHARDED:
        shape = local[n].shape[-2:]
        size = int(np.prod(shape))
        out[n] = _from_chips(flat[:, at:at + size].reshape((N_CHIPS,) + shape))
        at += size
    return out


def kernel(x, meta_tokens, norm1_gain, w_in, fox_b_f, q_norm_gain, k_norm_gain, hg_lb_logits, hg_out_gain, w_branch_a, w_branch_b, w_out, norm2_gain, w_up, conv_w, conv_b, w_down, loss_target, m_meta_tokens, m_norm1_gain, m_w_in, m_fox_b_f, m_q_norm_gain, m_k_norm_gain, m_hg_lb_logits, m_hg_out_gain, m_w_branch_a, m_w_branch_b, m_w_out, m_norm2_gain, m_w_up, m_conv_w, m_conv_b, m_w_down, v_meta_tokens, v_norm1_gain, v_w_in, v_fox_b_f, v_q_norm_gain, v_k_norm_gain, v_hg_lb_logits, v_hg_out_gain, v_w_branch_a, v_w_branch_b, v_w_out, v_norm2_gain, v_w_up, v_conv_w, v_conv_b, v_w_down):
    w_loc = dict(zip(WEIGHT_NAMES, (meta_tokens, norm1_gain, w_in, fox_b_f, q_norm_gain, k_norm_gain, hg_lb_logits, hg_out_gain,
                                    w_branch_a, w_branch_b, w_out, norm2_gain, w_up, conv_w, conv_b, w_down)))
    m_loc = dict(zip(WEIGHT_NAMES, (m_meta_tokens, m_norm1_gain, m_w_in, m_fox_b_f, m_q_norm_gain, m_k_norm_gain, m_hg_lb_logits,
                                    m_hg_out_gain, m_w_branch_a, m_w_branch_b, m_w_out, m_norm2_gain, m_w_up, m_conv_w, m_conv_b,
                                    m_w_down)))
    v_loc = dict(zip(WEIGHT_NAMES, (v_meta_tokens, v_norm1_gain, v_w_in, v_fox_b_f, v_q_norm_gain, v_k_norm_gain, v_hg_lb_logits,
                                    v_hg_out_gain, v_w_branch_a, v_w_branch_b, v_w_out, v_norm2_gain, v_w_up, v_conv_w, v_conv_b,
                                    v_w_down)))
    local_shapes = {n: tuple(w_loc[n].shape) for n in WEIGHT_NAMES}
    px, py, pc = _position()
    c, mine = _scalar(pc), _scalar(2 * px + py)

    weights = {n: w_loc[n].reshape(w_loc[n].shape[-2:]) for n in SMALL if n not in SMALL_SHARDED}
    weights.update(_gather_weights(w_loc))

    lay = _Layout(x.shape[0], x.shape[1])
    loss, grad_x, grads = _local_step(x, loss_target, weights, lay)
    loss = lax.psum(loss, ("x", "y", "c"))

    names = list(BIG) + ["small"]
    by_chip = [grads[n] for n in BIG] + [_pack_small_by_chip(grads)]
    from_sibling = _sibling_exchange(by_chip)
    parts = [_add_own_half(g, s, c, F32 if n == "small" else BF16, name=f"reduce_add2_{n}")
             for n, g, s in zip(names, by_chip, from_sibling)]
    from_chips = _chip_exchange(parts)
    own = [_add_chips(p, g, mine, name=f"reduce_add4_{n}") for n, p, g in zip(names, parts, from_chips)]
    other = _sibling_send(own)

    two_d = lambda t: [t[n].reshape(t[n].shape[-2:]) for n in BIG] + [_pack_small(t)]
    results = [_adamw(w_, o_, t_, m_, v_, c, name=f"adamw_{n}")
               for n, w_, o_, t_, m_, v_ in zip(names, two_d(w_loc), own, other, two_d(m_loc), two_d(v_loc))]
    outs = []
    for kind in range(4):
        tree = {n: results[i][kind].reshape(local_shapes[n]) for i, n in enumerate(BIG)}
        tree.update(_unpack_small(results[-1][kind], local_shapes))
        outs += [tree[n] for n in WEIGHT_NAMES]
    return (loss, grad_x, *outs)
```

```python
import functools

import jax
import jax.numpy as jnp
import numpy as np
from jax import lax
from jax.experimental import pallas as pl
from jax.experimental.pallas import tpu as pltpu

F32 = jnp.float32
BF16 = jnp.bfloat16
MXU_DTYPE = BF16
HIGHEST = lax.Precision.HIGHEST

D_MODEL = 1024
N_META = 16
LEAD = 48
ROW0 = LEAD + N_META
FOX_HEADS, FOX_DIM, FOX_W = 8, 64, 512
HG_HEADS, HG_DIM, HG_W = 4, 128, 512
D_FF = 2816
FF2 = 2 * D_FF
EPS = 1e-6
SUB = 16
LANES = 128
N_CHIPS = 4
NEG = -1e30

ADAM_LR, ADAM_B1, ADAM_B2, ADAM_EPS, ADAM_WD, ADAM_STEP = 0.001, 0.9, 0.999, 1e-08, 0.01, 10

VMEM_LIMIT = 56 * 1024 * 1024

C_GA, C_GB = 0, 1
C_FQ, C_FK, C_FV, C_HQ, C_HI, C_HF, C_HG = 4, 5, 6, 7, 8, 9, 10
MAIN_COLS = 11 * 512


def _params(n_axes=1):
    return pltpu.CompilerParams(dimension_semantics=("arbitrary",) * n_axes, vmem_limit_bytes=VMEM_LIMIT)


def _pick(n, cands):
    for c in cands:
        if n % c == 0:
            return c
    raise ValueError(f"no tile for {n} among {cands}")


def _rowwise(fn, rows, consts, outs, reds, *, n_rows, tile, name, into=None):
    assert n_rows % tile == 0
    rows = [r if isinstance(r, tuple) else (r, r.shape[1], 0) for r in rows]
    nr, nc, no = len(rows), len(consts), len(outs)
    aliased = into is not None and not isinstance(into[0], int)
    n_in = nr + nc + (1 if aliased else 0)

    def body(*refs):
        i = pl.program_id(0)
        ins = [r[...] for r in refs[:nr + nc]]
        res = fn(i, *ins)
        res = res if isinstance(res, (tuple, list)) else (res,)
        for ref, v in zip(refs[n_in:n_in + no], res[:no]):
            ref[...] = v.astype(ref.dtype)
        red_refs = refs[n_in + no:]
        if red_refs:
            @pl.when(i == 0)
            def _():
                for ref in red_refs:
                    ref[...] = jnp.zeros_like(ref)
            for ref, v in zip(red_refs, res[no:]):
                ref[...] += v.astype(F32)

    in_specs = [pl.BlockSpec((tile, w), functools.partial(lambda i, j: (i, j), j=j)) for (_, w, j) in rows]
    in_specs += [pl.BlockSpec(c.shape, functools.partial(lambda i, nd: (0,) * nd, nd=c.ndim)) for c in consts]
    out_specs = [pl.BlockSpec((tile, w), lambda i: (i, 0)) for (w, _) in outs]
    out_specs += [pl.BlockSpec(s, functools.partial(lambda i, nd: (0,) * nd, nd=len(s))) for s in reds]
    out_shape = [jax.ShapeDtypeStruct((n_rows, w), dt) for (w, dt) in outs]
    out_shape += [jax.ShapeDtypeStruct(s, F32) for s in reds]
    args = [r[0] for r in rows] + list(consts)
    aliases = {}
    if into is not None:
        out_specs[0] = pl.BlockSpec((tile, outs[0][0]), functools.partial(lambda i, j: (i, j), j=into[1]))
        if aliased:
            in_specs.append(pl.BlockSpec(memory_space=pltpu.HBM))
            args.append(into[0])
            aliases = {n_in - 1: 0}
            out_shape[0] = jax.ShapeDtypeStruct(into[0].shape, into[0].dtype)
        else:
            out_shape[0] = jax.ShapeDtypeStruct((n_rows, into[0]), outs[0][1])
    return pl.pallas_call(
        body, name=name, grid=(n_rows // tile,), in_specs=in_specs, out_specs=out_specs, out_shape=out_shape,
        input_output_aliases=aliases, compiler_params=_params(1),
    )(*args)


def _matmul(a, b, *, trans_a=False, trans_b=False, out_dtype=F32, by_chip=False, name):
    if trans_a:
        k, m = a.shape
    else:
        m, k = a.shape
    n = b.shape[0] if trans_b else b.shape[1]
    assert (b.shape[1] if trans_b else b.shape[0]) == k
    if trans_a:
        tm = _pick(m, (1408, 1024, 512, 256, 128))
        tk = _pick(k, (1088, 1024, 768, 512, 256))
    else:
        tm = _pick(m, (1088, 512, 256, 128))
        tk = k if k <= 1024 else _pick(k, (1408, 1024, 512))
    nk = k // tk
    wide = (2816,) if nk == 1 and not trans_a else ()
    tn = n // N_CHIPS if by_chip else _pick(n, wide + (1408, 1024, 512, 256, 128))
    dims = (((0 if trans_a else 1,), (1 if trans_b else 0,)), ((), ()))

    def body(a_ref, b_ref, o_ref, acc_ref):
        out = o_ref.at[0] if by_chip else o_ref
        part = lax.dot_general(a_ref[...], b_ref[...], dims, preferred_element_type=F32)
        if nk == 1:
            out[...] = part.astype(out.dtype)
        else:
            kk = pl.program_id(2)

            @pl.when(kk == 0)
            def _():
                acc_ref[...] = part

            @pl.when(kk > 0)
            def _():
                acc_ref[...] += part

            @pl.when(kk == nk - 1)
            def _():
                out[...] = acc_ref[...].astype(out.dtype)

    a_spec = pl.BlockSpec((tk, tm), lambda i, j, kk: (kk, i)) if trans_a else pl.BlockSpec((tm, tk), lambda i, j, kk: (i, kk))
    b_spec = pl.BlockSpec((tn, tk), lambda i, j, kk: (j, kk)) if trans_b else pl.BlockSpec((tk, tn), lambda i, j, kk: (kk, j))
    if by_chip:
        out_spec, out_shape = pl.BlockSpec((1, tm, tn), lambda i, j, kk: (j, i, 0)), (N_CHIPS, m, tn)
    else:
        out_spec, out_shape = pl.BlockSpec((tm, tn), lambda i, j, kk: (i, j)), (m, n)
    return pl.pallas_call(
        body, name=name, grid=(m // tm, n // tn, nk), in_specs=[a_spec, b_spec], out_specs=out_spec,
        out_shape=jax.ShapeDtypeStruct(out_shape, out_dtype),
        scratch_shapes=[pltpu.VMEM((tm, tn) if nk > 1 else (8, LANES), F32)],
        compiler_params=_params(3),
    )(a, b)


def _sigmoid(x):
    return 1.0 / (1.0 + jnp.exp(-x))


def _silu(x):
    return x * _sigmoid(x)


def _log_sigmoid(x):
    return jnp.minimum(x, 0.0) - jnp.log(1.0 + jnp.exp(-jnp.abs(x)))


def _rms(x, gain):
    return x * lax.rsqrt(jnp.mean(x * x, axis=-1, keepdims=True) + EPS) * gain


def _group_matrix(width, group):
    g = (np.arange(width)[:, None] // group == np.arange(LANES)[None, :]).astype(np.float32)
    return jnp.asarray(g, MXU_DTYPE), jnp.asarray(g.T.copy(), MXU_DTYPE)


def _split_dot(x, mat):
    dt = mat.dtype
    hi = x.astype(dt)
    r1 = x - hi.astype(F32)
    mid = r1.astype(dt)
    lo = (r1 - mid.astype(F32)).astype(dt)
    dot = lambda a: jnp.dot(a, mat, preferred_element_type=F32)
    return dot(hi) + dot(mid) + dot(lo)


@jax.custom_vjp
def _group_sum(x, gmat, gmat_t):
    return _split_dot(x, gmat)


@jax.custom_vjp
def _group_spread(s, gmat, gmat_t):
    return _split_dot(s, gmat_t)


_group_sum.defvjp(lambda x, g, gt: (_split_dot(x, g), (g, gt)),
                  lambda res, ct: (_group_spread(ct, *res), jnp.zeros_like(res[0]), jnp.zeros_like(res[1])))
_group_spread.defvjp(lambda s, g, gt: (_split_dot(s, gt), (g, gt)),
                     lambda res, ct: (_group_sum(ct, *res), jnp.zeros_like(res[0]), jnp.zeros_like(res[1])))


def _group_rms(x, gain, gmat, gmat_t, group):
    rstd = lax.rsqrt(_group_sum(x * x, gmat, gmat_t) * (1.0 / group) + EPS)
    return x * _group_spread(rstd, gmat, gmat_t) * gain


def _head_rms(x, gain):
    outs = []
    for h in range(x.shape[1] // LANES):
        xs = x[:, h * LANES:(h + 1) * LANES]
        outs.append(xs * lax.rsqrt(jnp.mean(xs * xs, axis=-1, keepdims=True) + EPS) * gain)
    return jnp.concatenate(outs, axis=1)


class _Layout:
    def __init__(self, batch, seq):
        self.batch, self.seq = batch, seq
        self.l_real = N_META + seq
        self.lp = -(-(LEAD + self.l_real) // 256) * 256
        self.n = batch * self.lp
        self.tile = _pick(self.lp, (512, 256))

    def valid(self, i, tile):
        per = self.lp // tile
        r = lax.rem(i, per) * tile + lax.broadcasted_iota(jnp.int32, (tile, 1), 0)
        return (r >= LEAD) & (r < LEAD + self.l_real)


def _cumsum_rows(x, lay, *, reverse, name):
    t = LANES
    nt = lay.lp // t
    c = x.shape[1]

    def body(x_ref, o_ref, carry):
        j = pl.program_id(1)

        @pl.when(j == 0)
        def _():
            carry[...] = jnp.zeros_like(carry)

        r = lax.broadcasted_iota(jnp.int32, (t, t), 0)
        q = lax.broadcasted_iota(jnp.int32, (t, t), 1)
        tri = jnp.where((q >= r) if reverse else (q <= r), 1.0, 0.0).astype(F32)
        xs = x_ref[...]
        out = jnp.dot(tri, xs, precision=HIGHEST, preferred_element_type=F32) + carry[0:1, :]
        o_ref[...] = out
        carry[...] = jnp.broadcast_to(carry[0:1, :] + jnp.sum(xs, axis=0, keepdims=True), carry.shape)

    def idx(b, j):
        return (b * nt + (nt - 1 - j if reverse else j), 0)

    return pl.pallas_call(
        body, name=name, grid=(lay.batch, nt),
        in_specs=[pl.BlockSpec((t, c), idx)], out_specs=pl.BlockSpec((t, c), idx),
        out_shape=jax.ShapeDtypeStruct(x.shape, F32),
        scratch_shapes=[pltpu.VMEM((8, c), F32)],
        compiler_params=_params(2),
    )(x)


def _group_cumsum(x, tile, *, reverse):
    r = lax.rem(lax.broadcasted_iota(jnp.int32, (tile, 1), 0), SUB)
    s = 1
    while s < SUB:
        if reverse:
            x = x + jnp.where(r < SUB - s, pltpu.roll(x, tile - s, 0), 0.0)
        else:
            x = x + jnp.where(r >= s, pltpu.roll(x, s, 0), 0.0)
        s *= 2
    return x


AUG = 128
FOX_BK = 256
FOX_BQ = 256
FOX_SCALE = FOX_DIM ** -0.5
KT_ROWS = FOX_DIM + 16


def _aug_matrices():
    e1 = np.zeros((FOX_W, FOX_HEADS * AUG), np.float32)
    e2 = np.zeros((LANES, FOX_HEADS * AUG), np.float32)
    ones = np.zeros((1, FOX_HEADS * AUG), np.float32)
    for h in range(FOX_HEADS):
        for d in range(FOX_DIM):
            e1[h * FOX_DIM + d, h * AUG + d] = 1.0
        for j in range(3):
            e2[j * FOX_HEADS + h, h * AUG + FOX_DIM + j] = 1.0
            ones[0, h * AUG + FOX_DIM + j] = 1.0
    return jnp.asarray(e1, MXU_DTYPE), jnp.asarray(e2, MXU_DTYPE), jnp.asarray(ones)


def _fox_augment(q, k, cum, key_ok, e1, e2, ones):
    dt = q.dtype
    c = jnp.where(key_ok, -cum, NEG)
    hi = c.astype(dt)
    r1 = c - hi.astype(F32)
    mid = r1.astype(dt)
    lo = (r1 - mid.astype(F32)).astype(dt)
    lane = lax.broadcasted_iota(jnp.int32, c.shape, 1)
    shift = lambda a, by: pltpu.roll(a.astype(F32), by, 1)
    parts = jnp.where(lane < FOX_HEADS, hi.astype(F32),
                      jnp.where(lane < 2 * FOX_HEADS, shift(mid, FOX_HEADS),
                                jnp.where(lane < 3 * FOX_HEADS, shift(lo, 2 * FOX_HEADS), 0.0))).astype(dt)
    qs = (q.astype(F32) * FOX_SCALE).astype(dt)
    q_aug = jnp.dot(qs, e1, preferred_element_type=F32) + ones
    k_aug = jnp.dot(k, e1, preferred_element_type=F32) + jnp.dot(parts, e2, preferred_element_type=F32)
    return q_aug.astype(dt), k_aug.astype(dt)


def _fox_tile(k_blk, q_blk, k0, q0, masked):
    st = lax.dot_general(k_blk, q_blk, (((1,), (1,)), ((), ())), preferred_element_type=F32)
    if masked:
        keys = k0 + lax.broadcasted_iota(jnp.int32, st.shape, 0)
        qs = q0 + lax.broadcasted_iota(jnp.int32, st.shape, 1)
        st = jnp.where(keys <= qs, st, NEG)
    return st


def _fox_fwd_t(q_aug, k_aug, v_t, shards, lay):
    bk, bq = FOX_BK, FOX_BQ
    nq = lay.lp // bq
    pairs = FOX_HEADS // 2
    ng = len(shards)
    steps = lay.batch * pairs

    def body(*refs):
        q_ref, k_ref, vt_ref = refs[:3]
        ot_ref, lse_ref = refs[3 + ng:5 + ng]
        zeros_ref = refs[5 + 2 * ng]
        gather = _Gather(refs[3:3 + ng], refs[5 + ng:5 + 2 * ng], refs[6 + 2 * ng], refs[7 + 2 * ng])
        step = pl.program_id(0) * pairs + pl.program_id(1)
        pl.when(step == 0)(gather.start)
        pl.when(step == steps // 2)(gather.relay)
        heads = [(slice(hh * AUG, (hh + 1) * AUG), slice(hh * FOX_DIM, (hh + 1) * FOX_DIM)) for hh in range(2)]
        zeros_ref[...] = jnp.zeros_like(zeros_ref)

        def q_loop(qb, _):
            q0 = pl.multiple_of(qb * bq, bq)
            q_blks = [q_ref[pl.ds(q0, bq), lanes] for lanes, _ in heads]

            def scores(kb, h):
                k0 = pl.multiple_of(kb * bk, bk)
                return _fox_tile(k_ref[pl.ds(k0, bk), heads[h][0]], q_blks[h], k0, q0, False)

            def consume(kb, h, state, masked):
                m, l, acc, pend, st = state
                k0 = pl.multiple_of(kb * bk, bk)
                if masked:
                    keys = k0 + lax.broadcasted_iota(jnp.int32, st.shape, 0)
                    qs_ = q0 + lax.broadcasted_iota(jnp.int32, st.shape, 1)
                    st = jnp.where(keys <= qs_, st, NEG)
                m_new = jnp.maximum(m, jnp.max(st, axis=0, keepdims=True))
                alpha = jnp.exp(m - m_new)
                p = jnp.exp(st - m_new)
                l = alpha * l + jnp.sum(p, axis=0, keepdims=True)
                acc = alpha * (acc + pend)
                pend = jnp.dot(vt_ref[heads[h][1], pl.ds(k0, bk)], p.astype(vt_ref.dtype), preferred_element_type=F32)
                return m_new, l, acc, pend

            def k_step(kb, states):
                nxt = [scores(kb + 1, h) for h in range(2)]
                return tuple(consume(kb, h, states[h], False) + (nxt[h],) for h in range(2))

            states = tuple((jnp.full((1, bq), NEG, F32), jnp.zeros((1, bq), F32), zeros_ref[...], zeros_ref[...], scores(0, h))
                           for h in range(2))
            states = lax.fori_loop(0, qb, k_step, states)
            qs = q0 + lax.broadcasted_iota(jnp.int32, (1, bq), 1)
            ok = (qs >= LEAD) & (qs < LEAD + lay.l_real)
            for hh in range(2):
                m, l, acc, pend = consume(qb, hh, states[hh], True)
                ot_ref[heads[hh][1], pl.ds(q0, bq)] = jnp.where(ok, (acc + pend) / l, 0.0).astype(ot_ref.dtype)
                lse_ref[hh, :, pl.ds(q0, bq)] = m + jnp.log(l)
            return 0

        lax.fori_loop(0, nq, q_loop, 0)
        pl.when(step == steps - 1)(gather.finish)

    aug = pl.BlockSpec((lay.lp, 2 * AUG), lambda b, p: (b, p))
    tr = pl.BlockSpec((2 * FOX_DIM, lay.lp), lambda b, p: (p, b))
    outs = pl.pallas_call(
        body, name="fox_fwd", grid=(lay.batch, pairs),
        in_specs=[aug, aug, tr] + [HBM_SPEC] * ng,
        out_specs=[tr, pl.BlockSpec((2, 1, lay.lp), lambda b, p: (b * pairs + p, 0, 0))] + [HBM_SPEC] * ng,
        out_shape=[jax.ShapeDtypeStruct((FOX_W, lay.n), MXU_DTYPE),
                   jax.ShapeDtypeStruct((lay.batch * FOX_HEADS, 1, lay.lp), F32)] + _Gather.out_shapes(shards),
        scratch_shapes=[pltpu.VMEM((FOX_DIM, bq), F32)] + _Gather.semaphores(ng),
        compiler_params=_params(2),
    )(q_aug, k_aug, v_t, *shards)
    return outs[0], outs[1], outs[2:]


def _fox_bwd_t(q_aug, k_aug, v, do, k_t, o_t, do_t, lse, parts, lay):
    bk, bq = FOX_BK, FOX_BQ
    nq, nk = lay.lp // bq, lay.lp // bk
    pairs = FOX_HEADS // 2
    ne = len(parts)

    def body(*refs):
        q_ref, k_ref, v_ref, do_ref, kt_ref, ot_ref, dot_ref, lse_ref = refs[:8]
        dqt_ref, dk_ref, dv_ref = refs[8 + ne:11 + ne]
        delta = refs[11 + 2 * ne]
        exchange = _ChipExchange(refs[8:8 + ne], refs[11 + ne:11 + 2 * ne], refs[12 + 2 * ne], refs[13 + 2 * ne])
        step = pl.program_id(0) * pairs + pl.program_id(1)
        pl.when(step == 0)(exchange.start)
        dqt_ref[...] = jnp.zeros_like(dqt_ref)
        dk_ref[...] = jnp.zeros_like(dk_ref)
        dv_ref[...] = jnp.zeros_like(dv_ref)
        heads = [(hh, slice(hh * AUG, (hh + 1) * AUG), slice(hh * FOX_DIM, (hh + 1) * FOX_DIM),
                  slice(hh * KT_ROWS, (hh + 1) * KT_ROWS)) for hh in range(2)]

        def delta_loop(qb, _):
            q0 = pl.multiple_of(qb * bq, bq)
            for hh, _, cols, _ in heads:
                prod = ot_ref[cols, pl.ds(q0, bq)].astype(F32) * dot_ref[cols, pl.ds(q0, bq)].astype(F32)
                delta[hh, :, pl.ds(q0, bq)] = jnp.sum(prod, axis=0, keepdims=True)
            return 0

        lax.fori_loop(0, nq, delta_loop, 0)

        def k_loop(kb, _):
            k0 = pl.multiple_of(kb * bk, bk)

            def products(qb, h):
                q0 = pl.multiple_of(qb * bq, bq)
                _, lanes, cols, _ = heads[h]
                st = _fox_tile(k_ref[pl.ds(k0, bk), lanes], q_ref[pl.ds(q0, bq), lanes], k0, q0, False)
                dpt = lax.dot_general(v_ref[pl.ds(k0, bk), cols], do_ref[pl.ds(q0, bq), cols], (((1,), (1,)), ((), ())),
                                      preferred_element_type=F32)
                return st, dpt

            def consume(qb, h, st, dpt, masked):
                q0 = pl.multiple_of(qb * bq, bq)
                hh, lanes, cols, trows = heads[h]
                if masked:
                    keys = k0 + lax.broadcasted_iota(jnp.int32, st.shape, 0)
                    qs = q0 + lax.broadcasted_iota(jnp.int32, st.shape, 1)
                    st = jnp.where(keys <= qs, st, NEG)
                q_blk = q_ref[pl.ds(q0, bq), lanes]
                do_blk = do_ref[pl.ds(q0, bq), cols]
                pt = jnp.exp(st - lse_ref[hh, :, pl.ds(q0, bq)])
                dst = (pt * (dpt - delta[hh, :, pl.ds(q0, bq)])).astype(q_blk.dtype)
                dv_ref[pl.ds(k0, bk), cols] += jnp.dot(pt.astype(do_blk.dtype), do_blk, preferred_element_type=F32)
                dk_ref[pl.ds(k0, bk), lanes] += jnp.dot(dst, q_blk, preferred_element_type=F32)
                dqt_ref[trows, pl.ds(q0, bq)] += jnp.dot(kt_ref[trows, pl.ds(k0, bk)], dst, preferred_element_type=F32)

            after = lambda qb: jnp.minimum(qb + 1, nq - 1)
            cur = [products(kb, h) for h in range(2)]
            nxt = tuple(products(after(kb), h) for h in range(2))
            for h in range(2):
                consume(kb, h, *cur[h], True)

            def rest(qb, held):
                new = tuple(products(after(qb), h) for h in range(2))
                for h in range(2):
                    consume(qb, h, *held[h], False)
                return new

            lax.fori_loop(kb + 1, nq, rest, nxt)
            return 0

        lax.fori_loop(0, nk, k_loop, 0)
        pl.when(step == lay.batch * pairs - 1)(exchange.finish)

    aug = pl.BlockSpec((lay.lp, 2 * AUG), lambda b, p: (b, p))
    rows = pl.BlockSpec((lay.lp, 2 * FOX_DIM), lambda b, p: (b, p))
    tr = pl.BlockSpec((2 * FOX_DIM, lay.lp), lambda b, p: (p, b))
    tr_k = pl.BlockSpec((2 * KT_ROWS, lay.lp), lambda b, p: (p, b))
    outs = pl.pallas_call(
        body, name="fox_bwd", grid=(lay.batch, pairs),
        in_specs=[aug, aug, rows, rows, tr_k, tr, tr, pl.BlockSpec((2, 1, lay.lp), lambda b, p: (b * pairs + p, 0, 0))]
        + [HBM_SPEC] * ne,
        out_specs=[tr_k, aug, rows] + [HBM_SPEC] * ne,
        out_shape=[jax.ShapeDtypeStruct((FOX_HEADS * KT_ROWS, lay.n), F32), jax.ShapeDtypeStruct((lay.n, FOX_HEADS * AUG), F32),
                   jax.ShapeDtypeStruct((lay.n, FOX_W), F32)] + [jax.ShapeDtypeStruct(p.shape, p.dtype) for p in parts],
        scratch_shapes=[pltpu.VMEM((2, 1, lay.lp), F32)] + _ChipExchange.semaphores(ne),
        compiler_params=_params(2),
    )(q_aug, k_aug, v, do, k_t, o_t, do_t, lse, *parts)
    return outs[0], outs[1], outs[2], outs[3:]


def _hgrn_fwd(proj, kk, gl, lay):
    t = lay.tile
    nt = lay.lp // t
    nsc = t // SUB

    def body(q_ref, k_ref, g_ref, v_ref, o_ref, st_ref, state, sub_rows):
        @pl.when(pl.program_id(1) == 0)
        def _():
            state[...] = jnp.zeros_like(state)

        rowi = lax.broadcasted_iota(jnp.int32, (SUB, 1), 0)

        def sub(sc, _):
            r0 = pl.multiple_of(sc * SUB, SUB)
            sub_rows[0] = k_ref[pl.ds(r0, SUB), :]
            sub_rows[1] = g_ref[pl.ds(r0, SUB), :]
            sub_rows[2] = v_ref[pl.ds(r0, SUB), :]
            for h in range(HG_HEADS):
                lanes = slice(h * HG_DIM, (h + 1) * HG_DIM)
                q16 = q_ref[pl.ds(r0, SUB), lanes]
                k16 = sub_rows[0, :, lanes]
                g16 = sub_rows[1, :, lanes]
                v16 = sub_rows[2, :, lanes]
                g_end = sub_rows[1, SUB - 1:SUB, lanes]
                s_prev = state[h]
                st_ref[sc, h] = s_prev
                o = lax.dot_general((q16 * jnp.exp(g16)).astype(MXU_DTYPE), s_prev.astype(MXU_DTYPE),
                                    (((1,), (1,)), ((), ())), preferred_element_type=F32)
                for s in range(SUB):
                    ks = sub_rows[0, s:s + 1, lanes]
                    gs = sub_rows[1, s:s + 1, lanes]
                    vs = sub_rows[2, s:s + 1, lanes]
                    w = q16 * jnp.exp(jnp.minimum(g16 - gs, 0.0)) * ks
                    a = jnp.where(rowi >= s, jnp.sum(w, axis=1, keepdims=True), 0.0)
                    o = o + a * vs
                o_ref[pl.ds(r0, SUB), lanes] = o
                kt = k16 * jnp.exp(g_end - g16)
                upd = lax.dot_general(v16.astype(MXU_DTYPE), kt.astype(MXU_DTYPE), (((0,), (0,)), ((), ())),
                                      preferred_element_type=F32)
                state[h] = jnp.exp(g_end) * s_prev + upd
            return 0

        lax.fori_loop(0, nsc, sub, 0)

    rows = lambda col: pl.BlockSpec((t, HG_W), functools.partial(lambda b, i, col: (b * nt + i, col), col=col))
    return pl.pallas_call(
        body, name="hgrn_fwd", grid=(lay.batch, nt),
        in_specs=[rows(C_HQ), rows(0), rows(0), rows(C_HI)],
        out_specs=[rows(0), pl.BlockSpec((nsc, HG_HEADS, HG_DIM, HG_DIM), lambda b, i: (b * nt + i, 0, 0, 0))],
        out_shape=[jax.ShapeDtypeStruct((lay.n, HG_W), F32),
                   jax.ShapeDtypeStruct((lay.n // SUB, HG_HEADS, HG_DIM, HG_DIM), F32)],
        scratch_shapes=[pltpu.VMEM((HG_HEADS, HG_DIM, HG_DIM), F32), pltpu.VMEM((3, SUB, HG_W), F32)],
        compiler_params=_params(2),
    )(proj, kk, gl, proj)


def _hgrn_bwd(proj, kk, gl, do, states, lay):
    t = lay.tile
    nt = lay.lp // t
    nsc = t // SUB

    def body(q_ref, k_ref, g_ref, v_ref, do_ref, st_ref, dq_ref, dk_ref, dv_ref, dg_ref, dstate, sub_rows, row_acc):
        @pl.when(pl.program_id(1) == 0)
        def _():
            dstate[...] = jnp.zeros_like(dstate)

        rowi = lax.broadcasted_iota(jnp.int32, (SUB, 1), 0)

        def sub(it, _):
            sc = nsc - 1 - it
            r0 = pl.multiple_of(sc * SUB, SUB)
            sub_rows[0] = k_ref[pl.ds(r0, SUB), :]
            sub_rows[1] = g_ref[pl.ds(r0, SUB), :]
            sub_rows[2] = v_ref[pl.ds(r0, SUB), :]
            for h in range(HG_HEADS):
                lanes = slice(h * HG_DIM, (h + 1) * HG_DIM)
                q16 = q_ref[pl.ds(r0, SUB), lanes]
                k16 = sub_rows[0, :, lanes]
                g16 = sub_rows[1, :, lanes]
                v16 = sub_rows[2, :, lanes]
                do16 = do_ref[pl.ds(r0, SUB), lanes]
                g_end = sub_rows[1, SUB - 1:SUB, lanes]
                s_prev = st_ref[sc, h]
                ds_end = dstate[h]
                eg = jnp.exp(g16)
                ekt = jnp.exp(g_end - g16)
                e_end = jnp.exp(g_end)
                qt = q16 * eg
                kt = k16 * ekt
                ds_mx = ds_end.astype(MXU_DTYPE)
                dv = lax.dot_general(kt.astype(MXU_DTYPE), ds_mx, (((1,), (1,)), ((), ())), preferred_element_type=F32)
                dkt = jnp.dot(v16.astype(MXU_DTYPE), ds_mx, preferred_element_type=F32)
                dk = dkt * ekt
                ktdkt = kt * dkt
                dg_end = jnp.sum(ktdkt, axis=0, keepdims=True) + jnp.sum(s_prev * ds_end, axis=0, keepdims=True) * e_end
                dg = jnp.where(rowi == SUB - 1, dg_end, 0.0) - ktdkt
                dqt = jnp.dot(do16.astype(MXU_DTYPE), s_prev.astype(MXU_DTYPE), preferred_element_type=F32)
                dq = dqt * eg
                dg = dg + qt * dqt
                dstate[h] = e_end * ds_end + lax.dot_general(do16.astype(MXU_DTYPE), qt.astype(MXU_DTYPE),
                                                             (((0,), (0,)), ((), ())), preferred_element_type=F32)
                for s in range(SUB):
                    ks = sub_rows[0, s:s + 1, lanes]
                    gs = sub_rows[1, s:s + 1, lanes]
                    vs = sub_rows[2, s:s + 1, lanes]
                    live = rowi >= s
                    e = jnp.where(live, jnp.exp(jnp.minimum(g16 - gs, 0.0)), 0.0)
                    qe = q16 * e
                    a = jnp.sum(qe * ks, axis=1, keepdims=True)
                    da = jnp.where(live, jnp.sum(do16 * vs, axis=1, keepdims=True), 0.0)
                    t1 = da * qe
                    dk_row = jnp.sum(t1, axis=0, keepdims=True)
                    dq = dq + da * (e * ks)
                    dg = dg + t1 * ks
                    row_acc[0, s:s + 1, :] = jnp.sum(a * do16, axis=0, keepdims=True)
                    row_acc[1, s:s + 1, :] = dk_row
                    row_acc[2, s:s + 1, :] = ks * dk_row
                dq_ref[pl.ds(r0, SUB), lanes] = dq
                dk_ref[pl.ds(r0, SUB), lanes] = dk + row_acc[1]
                dv_ref[pl.ds(r0, SUB), lanes] = dv + row_acc[0]
                dg_ref[pl.ds(r0, SUB), lanes] = dg - row_acc[2]
            return 0

        lax.fori_loop(0, nsc, sub, 0)

    def rows(col):
        return pl.BlockSpec((t, HG_W), functools.partial(lambda b, i, col: (b * nt + nt - 1 - i, col), col=col))

    out = jax.ShapeDtypeStruct((lay.n, HG_W), F32)
    return pl.pallas_call(
        body, name="hgrn_bwd", grid=(lay.batch, nt),
        in_specs=[rows(C_HQ), rows(0), rows(0), rows(C_HI), rows(0),
                  pl.BlockSpec((nsc, HG_HEADS, HG_DIM, HG_DIM), lambda b, i: (b * nt + nt - 1 - i, 0, 0, 0))],
        out_specs=[rows(0)] * 4, out_shape=[out] * 4,
        scratch_shapes=[pltpu.VMEM((HG_HEADS, HG_DIM, HG_DIM), F32), pltpu.VMEM((3, SUB, HG_W), F32),
                        pltpu.VMEM((3, SUB, HG_DIM), F32)],
        compiler_params=_params(2),
    )(proj, kk, gl, proj, do, states)


CONV_COLS = 1408


def _shift_down(x, halo, tile, by):
    out = pltpu.roll(x, by, 0)
    rowi = lax.broadcasted_iota(jnp.int32, (8, 1), 0)
    top = out[0:8]
    for r in range(by):
        top = jnp.where(rowi == r, halo[8 - by + r:8 - by + r + 1, :], top)
    return jnp.concatenate([top, out[8:]], axis=0)


def _shift_up(x, halo, tile, by):
    out = pltpu.roll(x, tile - by, 0)
    rowi = lax.broadcasted_iota(jnp.int32, (8, 1), 0)
    bottom = out[tile - 8:]
    for r in range(by):
        bottom = jnp.where(rowi == 8 - by + r, halo[r:r + 1, :], bottom)
    return jnp.concatenate([out[:tile - 8], bottom], axis=0)


def _conv_specs(tile):
    ncb = D_FF // CONV_COLS
    per8 = tile // 8

    def tile_spec(off):
        return pl.BlockSpec((tile, CONV_COLS), functools.partial(lambda i, j, off: (i, j + off), off=off))

    def prev_spec(off):
        return pl.BlockSpec((8, CONV_COLS), functools.partial(lambda i, j, off: (jnp.maximum(i * per8 - 1, 0), j + off), off=off))

    def w_spec(off):
        return pl.BlockSpec((3, CONV_COLS), functools.partial(lambda i, j, off: (0, j + off), off=off))

    def b_spec(off):
        return pl.BlockSpec((1, CONV_COLS), functools.partial(lambda i, j, off: (0, j + off), off=off))

    return ncb, tile_spec, prev_spec, w_spec, b_spec


def _conv3(x, halo, w, b, tile):
    return w[0:1, :] * _shift_down(x, halo, tile, 2) + w[1:2, :] * _shift_down(x, halo, tile, 1) + w[2:3, :] * x + b


def _conv_act_fwd(u, conv_w, conv_b, lay):
    tile = lay.tile
    ncb, tile_spec, prev_spec, w_spec, b_spec = _conv_specs(tile)

    def body(ug, uv, pg, pv, wg, wv, bg, bv, o_ref):
        cg = _conv3(ug[...], pg, wg, bg[...], tile)
        cv = _conv3(uv[...], pv, wv, bv[...], tile)
        o_ref[...] = (_silu(cg) * cv).astype(o_ref.dtype)

    return pl.pallas_call(
        body, name="conv_act_fwd", grid=(lay.n // tile, ncb),
        in_specs=[tile_spec(0), tile_spec(ncb), prev_spec(0), prev_spec(ncb), w_spec(0), w_spec(ncb), b_spec(0), b_spec(ncb)],
        out_specs=pl.BlockSpec((tile, CONV_COLS), lambda i, j: (i, j)),
        out_shape=jax.ShapeDtypeStruct((lay.n, D_FF), MXU_DTYPE),
        compiler_params=_params(2),
    )(u, u, u, u, conv_w, conv_w, conv_b, conv_b)


def _conv_act_bwd(u, dact, conv_w, conv_b, lay):
    tile = lay.tile
    ncb, tile_spec, prev_spec, w_spec, b_spec = _conv_specs(tile)

    def body(ug, uv, pg, pv, wg, wv, bg, bv, da_ref, dg_ref, dv_ref, gwg, gwv, gbg, gbv):
        @pl.when(pl.program_id(1) == 0)
        def _():
            for r in (gwg, gwv, gbg, gbv):
                r[...] = jnp.zeros_like(r)

        xg, xv = ug[...], uv[...]
        cg = _conv3(xg, pg, wg, bg[...], tile)
        cv = _conv3(xv, pv, wv, bv[...], tile)
        da = da_ref[...].astype(F32)
        sg = _sigmoid(cg)
        dcv = da * (cg * sg)
        dcg = da * cv * (sg * (1.0 + cg * (1.0 - sg)))
        dg_ref[...] = dcg
        dv_ref[...] = dcv
        for x, halo, dc, gw, gb in ((xg, pg, dcg, gwg, gbg), (xv, pv, dcv, gwv, gbv)):
            gw[0, 0:1, :] += jnp.sum(dc * _shift_down(x, halo, tile, 2), axis=0, keepdims=True)
            gw[0, 1:2, :] += jnp.sum(dc * _shift_down(x, halo, tile, 1), axis=0, keepdims=True)
            gw[0, 2:3, :] += jnp.sum(dc * x, axis=0, keepdims=True)
            gb[0] += jnp.sum(dc, axis=0, keepdims=True)

    swap = lambda spec: pl.BlockSpec(spec.block_shape, functools.partial(lambda j, i, f: f(i, j), f=spec.index_map))
    col = lambda j, i: (i, j)
    red_w = pl.BlockSpec((1, 3, CONV_COLS), lambda j, i: (j, 0, 0))
    red_b = pl.BlockSpec((1, 1, CONV_COLS), lambda j, i: (j, 0, 0))
    outs = pl.pallas_call(
        body, name="conv_act_bwd", grid=(ncb, lay.n // tile),
        in_specs=[swap(s) for s in (tile_spec(0), tile_spec(ncb), prev_spec(0), prev_spec(ncb), w_spec(0), w_spec(ncb),
                                    b_spec(0), b_spec(ncb))] + [pl.BlockSpec((tile, CONV_COLS), col)],
        out_specs=[pl.BlockSpec((tile, CONV_COLS), col), pl.BlockSpec((tile, CONV_COLS), col), red_w, red_w, red_b, red_b],
        out_shape=[jax.ShapeDtypeStruct((lay.n, D_FF), F32), jax.ShapeDtypeStruct((lay.n, D_FF), F32),
                   jax.ShapeDtypeStruct((ncb, 3, CONV_COLS), F32), jax.ShapeDtypeStruct((ncb, 3, CONV_COLS), F32),
                   jax.ShapeDtypeStruct((ncb, 1, CONV_COLS), F32), jax.ShapeDtypeStruct((ncb, 1, CONV_COLS), F32)],
        compiler_params=_params(2),
    )(u, u, u, u, conv_w, conv_w, conv_b, conv_b, dact)
    dcg, dcv, gwg, gwv, gbg, gbv = outs
    unblock = lambda g: jnp.transpose(g, (1, 0, 2)).reshape(g.shape[1], D_FF)
    g_w = jnp.concatenate([unblock(gwg), unblock(gwv)], axis=1)
    g_b = jnp.concatenate([unblock(gbg), unblock(gbv)], axis=1)
    return dcg, dcv, g_w, g_b


def _conv_input_bwd(dcg, dcv, conv_w, lay):
    tile = lay.tile
    ncb = D_FF // CONV_COLS
    nblk8 = lay.n // 8
    per8 = tile // 8
    nxt = lambda i: jnp.minimum((i + 1) * per8, nblk8 - 1)

    def half(dc, off, into, name):
        def body(*refs):
            d, halo, w, o = refs[0], refs[1], refs[2], refs[-1]
            x = d[...]
            du = w[2:3, :] * x + w[1:2, :] * _shift_up(x, halo, tile, 1) + w[0:1, :] * _shift_up(x, halo, tile, 2)
            o[...] = jnp.where(lay.valid(pl.program_id(0), tile), du, 0.0).astype(o.dtype)

        in_specs = [pl.BlockSpec((tile, CONV_COLS), lambda i, j: (i, j)),
                    pl.BlockSpec((8, CONV_COLS), lambda i, j: (nxt(i), j)),
                    pl.BlockSpec((3, CONV_COLS), lambda i, j: (0, j + off))]
        args = [dc, dc, conv_w]
        if into is not None:
            in_specs.append(pl.BlockSpec(memory_space=pltpu.HBM))
            args.append(into)
        return pl.pallas_call(
            body, name=name, grid=(lay.n // tile, ncb), in_specs=in_specs,
            out_specs=pl.BlockSpec((tile, CONV_COLS), lambda i, j: (i, j + off)),
            out_shape=jax.ShapeDtypeStruct((lay.n, FF2), MXU_DTYPE),
            input_output_aliases={} if into is None else {3: 0},
            compiler_params=_params(2),
        )(*args)

    return half(dcv, ncb, half(dcg, 0, None, "conv_input_bwd_gate"), "conv_input_bwd_value")


def _loss_head(h1, mlp, target, lay):
    t, sub = 256, ROW0
    per = lay.lp // t
    nsub = t // sub
    nreal = lay.seq // sub

    def body(h_ref, m_ref, *rest):
        t_refs, (loss_ref, dy_ref, dyb_ref) = rest[:nsub], rest[nsub:]
        b, j = pl.program_id(0), pl.program_id(1)

        @pl.when((b == 0) & (j == 0))
        def _():
            loss_ref[...] = jnp.zeros_like(loss_ref)

        rows_ = j * t + lax.broadcasted_iota(jnp.int32, (t, 1), 0)
        real = (rows_ >= ROW0) & (rows_ < ROW0 + lay.seq)
        tgt_ = jnp.concatenate([r[...] for r in t_refs], axis=0)
        err = jnp.where(real, h_ref[...] + m_ref[...] - tgt_, 0.0)
        dy = err * (1.0 / D_MODEL)
        dy_ref[...] = dy
        dyb_ref[...] = dy.astype(dyb_ref.dtype)
        loss_ref[...] += 0.5 * jnp.sum(err * dy)

    rows = pl.BlockSpec((t, D_MODEL), lambda b, j: (b * per + j, 0))
    tgt = [pl.BlockSpec((sub, D_MODEL), functools.partial(
        lambda b, j, r: (b * nreal + jnp.clip(j * nsub + r - 1, 0, nreal - 1), 0), r=r)) for r in range(nsub)]
    return pl.pallas_call(
        body, name="loss_head", grid=(lay.batch, per),
        in_specs=[rows, rows] + tgt,
        out_specs=[pl.BlockSpec((8, LANES), lambda b, j: (0, 0)), rows, rows],
        out_shape=[jax.ShapeDtypeStruct((8, LANES), F32), jax.ShapeDtypeStruct((lay.n, D_MODEL), F32),
                   jax.ShapeDtypeStruct((lay.n, D_MODEL), MXU_DTYPE)],
        compiler_params=_params(2),
    )(h1, mlp, *([target] * nsub))


def _fox_prep(fq, fk, ff, gq, gk, bf, gmat, gmat_t, valid):
    q = _group_rms(fq, gq, gmat, gmat_t, FOX_DIM)
    k = _group_rms(fk, gk, gmat, gmat_t, FOX_DIM)
    logf = jnp.where(valid, _log_sigmoid(ff + bf), 0.0)
    return q, k, logf


def _hg_prep(hf, l0, l1):
    mx = jnp.maximum(l0, l1)
    e0, e1 = jnp.exp(l0 - mx), jnp.exp(l1 - mx)
    lb = e0 / (e0 + e1)
    lf = jnp.log(lb + (1.0 - lb) * _sigmoid(hf))
    kk = (1.0 - lb) * _sigmoid(-hf)
    return lf, kk


def _hg_post(o, hg, gain):
    return _head_rms(o, gain) * _silu(hg)


def _gate(ga, gb, ya, yb):
    return _sigmoid(ga) * ya + _sigmoid(gb) * yb


def _by_chip(g):
    return jnp.transpose(g.reshape(g.shape[0], N_CHIPS, g.shape[1] // N_CHIPS), (1, 0, 2))


def _from_chips(a):
    return jnp.transpose(a, (1, 0, 2)).reshape(a.shape[1], N_CHIPS * a.shape[2])


def _local_step(x, target, w, late_shards, c, mine, lay):
    n, tile = lay.n, lay.tile
    rw = functools.partial(_rowwise, n_rows=n, tile=tile)
    mx = lambda a: a.astype(MXU_DTYPE)

    w_in = w["w_in"]
    fq, fk, fv, ffw, hq, hf, hi, hg, ga, gb = jnp.split(w_in, list(np.cumsum([512, 512, 512, 8, 512, 512, 512, 512, 1024])), axis=1)
    w_main = mx(jnp.concatenate([ga, gb, fq, fk, fv, hq, hi, hf, hg], axis=1))
    w_ff = mx(jnp.pad(ffw, ((0, 0), (0, LANES - FOX_HEADS))))
    conv_w, conv_b = w["conv_w"].astype(F32), w["conv_b"].astype(F32)
    g1, g2 = w["norm1_gain"], w["norm2_gain"]
    gq, gk = jnp.tile(w["q_norm_gain"], (1, FOX_HEADS)), jnp.tile(w["k_norm_gain"], (1, FOX_HEADS))
    bf = jnp.pad(w["fox_b_f"], ((0, 0), (0, LANES - FOX_HEADS)))
    lb_logits, hg_gain = w["hg_lb_logits"], w["hg_out_gain"]
    gm64, gm64_t = _group_matrix(FOX_W, FOX_DIM)

    meta = jnp.broadcast_to(w["meta_tokens"].astype(F32)[None], (lay.batch, N_META, D_MODEL))
    h0 = jnp.concatenate([jnp.zeros((lay.batch, LEAD, D_MODEL), F32), meta, x,
                          jnp.zeros((lay.batch, lay.lp - LEAD - lay.l_real, D_MODEL), F32)], axis=1).reshape(n, D_MODEL)

    (xn,) = rw(lambda i, h, g: _rms(h, g), [h0], [g1], [(D_MODEL, MXU_DTYPE)], [], name="norm1")
    proj = _matmul(xn, w_main, name="proj_main")
    pff = _matmul(xn, w_ff, name="proj_ff")

    def fox_prep_fn(i, a, b_, v_, f_, gq_, gk_, bf_, m_, mt_):
        q_, k_, logf = _fox_prep(a, b_, f_, gq_, gk_, bf_, m_, mt_, lay.valid(i, tile))
        return q_, k_, v_, logf

    q, k, v, logf = rw(fox_prep_fn, [(proj, 512, C_FQ), (proj, 512, C_FK), (proj, 512, C_FV), pff], [gq, gk, bf, gm64, gm64_t],
                       [(512, MXU_DTYPE), (512, MXU_DTYPE), (512, MXU_DTYPE), (LANES, F32)], [], name="fox_prep")
    cum = _cumsum_rows(logf, lay, reverse=False, name="fox_cum")
    e1, e2, aug_ones = _aug_matrices()
    q_aug, k_aug = rw(lambda i, q_, k_, c_, e1_, e2_, on_: _fox_augment(q_, k_, c_, lay.valid(i, tile), e1_, e2_, on_),
                      [q, k, cum], [e1, e2, aug_ones], [(FOX_HEADS * AUG, MXU_DTYPE)] * 2, [], name="fox_aug")
    o_t, lse, late_slabs = _fox_fwd_t(q_aug, k_aug, v.T, late_shards, lay)
    w_a, w_b, w_out, w_up, w_down = [mx(a) for a in _assemble(LATE, late_shards, late_slabs)]

    def hg_prep_fn(i, hf_, l0, l1):
        lf, kk_ = _hg_prep(hf_, l0, l1)
        return kk_, _group_cumsum(lf, tile, reverse=False)

    lb0, lb1 = lb_logits[0:1], lb_logits[1:2]
    kk, gl = rw(hg_prep_fn, [(proj, 512, C_HF)], [lb0, lb1], [(512, F32), (512, F32)], [], name="hg_prep")
    o_hg, states = _hgrn_fwd(proj, kk, gl, lay)
    (oh,) = rw(lambda i, o, g_, gain: _hg_post(o, g_, gain), [o_hg, (proj, 512, C_HG)], [hg_gain], [(512, MXU_DTYPE)], [],
               name="hg_post")
    ya = _matmul(oh, w_a, out_dtype=MXU_DTYPE, name="branch_a")
    yb = _matmul(o_t, w_b, trans_a=True, out_dtype=MXU_DTYPE, name="branch_b")
    pga, pgb = (proj, 1024, C_GA), (proj, 1024, C_GB)
    gate_fn = lambda a, b_, c_, d_: _gate(a, b_, c_.astype(F32), d_.astype(F32))
    (merged,) = rw(lambda i, a, b_, c_, d_: gate_fn(a, b_, c_, d_), [pga, pgb, ya, yb], [], [(D_MODEL, MXU_DTYPE)], [], name="gate")
    mo = _matmul(merged, w_out, name="out_proj")
    h1, hn = rw(lambda i, h, m_, g: (h + m_, _rms(h + m_, g)), [h0, mo], [g2], [(D_MODEL, F32), (D_MODEL, MXU_DTYPE)], [],
                name="norm2")
    u = _matmul(hn, w_up, name="up_proj")
    act = _conv_act_fwd(u, conv_w, conv_b, lay)
    mlp = _matmul(act, w_down, name="down_proj")
    loss_blk, dy, dyb = _loss_head(h1, mlp, target.reshape(lay.batch * lay.seq, D_MODEL), lay)
    loss = loss_blk[0, 0]

    grads = {}
    dact = _matmul(dyb, w_down, trans_b=True, out_dtype=MXU_DTYPE, name="down_bwd_x")
    grads["w_down"] = _matmul(act, dyb, trans_a=True, name="down_bwd_w").reshape(N_CHIPS, D_FF // N_CHIPS, D_MODEL)
    dcg, dcv, grads["conv_w"], grads["conv_b"] = _conv_act_bwd(u, dact, conv_w, conv_b, lay)
    du = _conv_input_bwd(dcg, dcv, conv_w, lay)
    dhn = _matmul(du, w_up, trans_b=True, name="up_bwd_x")
    grads["w_up"] = _matmul(hn, du, trans_a=True, by_chip=True, name="up_bwd_w")

    def norm2_bwd(i, h, d_, dy_, g):
        _, vjp = jax.vjp(_rms, h, g)
        dh, dg = vjp(d_)
        return dh + dy_, dh + dy_, dg

    dh1, dh1b, grads["norm2_gain"] = rw(norm2_bwd, [h1, dhn, dy], [g2], [(D_MODEL, F32), (D_MODEL, MXU_DTYPE)], [(1, D_MODEL)],
                                        name="norm2_bwd")
    dmerged = _matmul(dh1b, w_out, trans_b=True, out_dtype=MXU_DTYPE, name="out_bwd_x")
    grads["w_out"] = _matmul(merged, dh1b, trans_a=True, name="out_bwd_w").reshape(N_CHIPS, D_MODEL // N_CHIPS, D_MODEL)

    def gate_bwd(i, a, b_, c_, d_, dm):
        _, vjp = jax.vjp(gate_fn, a, b_, c_, d_)
        da, db, dc, dd = vjp(dm.astype(F32))
        return jnp.concatenate([da, db], axis=1), dc, dd

    dproj, dya, dyb_ = rw(gate_bwd, [pga, pgb, ya, yb, dmerged], [], [(2 * D_MODEL, MXU_DTYPE)] + [(D_MODEL, MXU_DTYPE)] * 2, [],
                          name="gate_bwd", into=(MAIN_COLS, 0))
    doh = _matmul(dya, w_a, trans_b=True, out_dtype=MXU_DTYPE, name="branch_a_bwd_x")
    grads["w_branch_a"] = _matmul(oh, dya, trans_a=True, by_chip=True, name="branch_a_bwd_w")
    dofox = _matmul(dyb_, w_b, trans_b=True, out_dtype=MXU_DTYPE, name="branch_b_bwd_x")
    grads["w_branch_b"] = _matmul(o_t, dyb_, by_chip=True, name="branch_b_bwd_w")

    def hg_post_bwd(i, o, g_, d_, gain):
        _, vjp = jax.vjp(_hg_post, o, g_, gain)
        do_, dg_, dgain = vjp(d_.astype(F32))
        return dg_, do_, dgain

    dproj, do_hg, grads["hg_out_gain"] = rw(hg_post_bwd, [o_hg, (proj, 512, C_HG), doh], [hg_gain],
                                            [(512, MXU_DTYPE), (512, F32)], [(1, HG_DIM)], name="hg_post_bwd", into=(dproj, C_HG))
    dhq, dkk, dhi, dgl = _hgrn_bwd(proj, kk, gl, do_hg, states, lay)

    def hg_prep_bwd(i, hf_, dkk_, dgl_, l0, l1):
        _, vjp = jax.vjp(_hg_prep, hf_, l0, l1)
        return vjp((_group_cumsum(dgl_, tile, reverse=True), dkk_))

    dproj, g_lb0, g_lb1 = rw(hg_prep_bwd, [(proj, 512, C_HF), dkk, dgl], [lb0, lb1], [(512, MXU_DTYPE)], [(1, HG_W), (1, HG_W)],
                             name="hg_prep_bwd", into=(dproj, C_HF))
    grads["hg_lb_logits"] = jnp.concatenate([g_lb0, g_lb1], axis=0)

    k_t = (k.astype(F32) * FOX_SCALE).astype(MXU_DTYPE).T.reshape(FOX_HEADS, FOX_DIM, n)
    k_t = jnp.concatenate([k_t, jnp.ones((FOX_HEADS, KT_ROWS - FOX_DIM, n), MXU_DTYPE)], axis=1).reshape(FOX_HEADS * KT_ROWS, n)
    late_grads = [grads.pop(n_) for n_ in LATE]
    parts = [_add_own_half(g_, s_, c, BF16, name=f"reduce_add2_{n_}")
             for n_, g_, s_ in zip(LATE, late_grads, _sibling_exchange(late_grads, "reduce_sibling_late"))]
    dq_t, dk_aug, dv, from_chips = _fox_bwd_t(q_aug, k_aug, v, dofox, k_t, o_t, dofox.T, lse, parts, lay)
    reduced = [_add_chips(p_, g_, mine, name=f"reduce_add4_{n_}") for n_, p_, g_ in zip(LATE, parts, from_chips)]
    dq_t = dq_t.reshape(FOX_HEADS, KT_ROWS, n)
    dq = dq_t[:, :FOX_DIM].reshape(FOX_W, n).T
    dk_aug = dk_aug.reshape(n, FOX_HEADS, AUG)
    dk = dk_aug[:, :, :FOX_DIM].reshape(n, FOX_W)
    dcum = jnp.pad(dq_t[:, FOX_DIM].T - dk_aug[:, :, FOX_DIM], ((0, 0), (0, LANES - FOX_HEADS)))
    dlogf = _cumsum_rows(dcum, lay, reverse=True, name="fox_cum_bwd")

    def fox_prep_bwd(i, a, b_, f_, dq_, dk_, dl_, gq_, gk_, bf_, m_, mt_):
        valid = lay.valid(i, tile)
        _, vjp = jax.vjp(lambda a_, b__, f__, gq__, gk__, bf__: _fox_prep(a_, b__, f__, gq__, gk__, bf__, m_, mt_, valid),
                         a, b_, f_, gq_, gk_, bf_)
        da, db, df, dgq, dgk, dbf = vjp((dq_, dk_, dl_))
        return jnp.concatenate([da, db], axis=1), df, dgq, dgk, dbf

    dproj, dff, g_gq, g_gk, g_bf = rw(
        fox_prep_bwd, [(proj, 512, C_FQ), (proj, 512, C_FK), pff, dq, dk, dlogf], [gq, gk, bf, gm64, gm64_t],
        [(2 * FOX_W, MXU_DTYPE), (LANES, MXU_DTYPE)], [(1, FOX_W), (1, FOX_W), (1, LANES)], name="fox_prep_bwd",
        into=(dproj, C_FQ // 2))
    grads["q_norm_gain"] = g_gq.reshape(FOX_HEADS, FOX_DIM).sum(0, keepdims=True)
    grads["k_norm_gain"] = g_gk.reshape(FOX_HEADS, FOX_DIM).sum(0, keepdims=True)
    grads["fox_b_f"] = g_bf[:, :FOX_HEADS]

    (dproj,) = rw(lambda i, a, b_, c_: jnp.concatenate([a, b_, c_], axis=1), [dv, dhq, dhi], [], [(3 * 512, MXU_DTYPE)], [],
                  name="dproj_cast", into=(dproj, C_FV // 3))
    dxn = _matmul(dproj, w_main, trans_b=True, name="proj_bwd_x")
    dxn_ff = _matmul(dff, w_ff, trans_b=True, name="proj_ff_bwd_x")
    g_main = _matmul(xn, dproj, trans_a=True, name="proj_bwd_w")
    g_ff = _matmul(xn, dff, trans_a=True, name="proj_ff_bwd_w")[:, :FOX_HEADS]
    p = jnp.split(g_main, list(np.cumsum([1024, 1024] + [512] * 6)), axis=1)
    grads["w_in"] = _by_chip(jnp.concatenate([p[2], p[3], p[4], g_ff, p[5], p[7], p[6], p[8], p[0], p[1]], axis=1))

    per = lay.lp // tile

    def norm1_bwd(i, h, d1, d2, dh1_, g):
        _, vjp = jax.vjp(_rms, h, g)
        dh, dg = vjp(d1 + d2)
        dh = dh + dh1_
        dmeta = jnp.where(lax.rem(i, per) == 0, dh[LEAD:LEAD + N_META, :], 0.0)
        return dh, dg, dmeta

    dh0, grads["norm1_gain"], grads["meta_tokens"] = rw(norm1_bwd, [h0, dxn, dxn_ff, dh1], [g1], [(D_MODEL, F32)],
                                                       [(1, D_MODEL), (N_META, D_MODEL)], name="norm1_bwd")
    grad_x = dh0.reshape(lay.batch, lay.lp, D_MODEL)[:, ROW0:ROW0 + lay.seq]
    return loss, grad_x, grads, reduced


MESH = pl.DeviceIdType.MESH
HBM_SPEC = pl.BlockSpec(memory_space=pltpu.HBM)
WEIGHT_NAMES = ["meta_tokens", "norm1_gain", "w_in", "fox_b_f", "q_norm_gain", "k_norm_gain", "hg_lb_logits", "hg_out_gain",
                "w_branch_a", "w_branch_b", "w_out", "norm2_gain", "w_up", "conv_w", "conv_b", "w_down"]
BIG = ("w_in", "w_branch_a", "w_branch_b", "w_out", "w_up", "w_down")
BIG_COL_SHARDED = ("w_in", "w_branch_a", "w_branch_b", "w_up")
LATE = BIG[1:]
SMALL = tuple(n for n in WEIGHT_NAMES if n not in BIG)
SMALL_SHARDED = ("meta_tokens", "conv_w")
SMALL_ROWS = 144
GATHER_SMALL_ROWS = 80


def _position():
    return lax.axis_index("x"), lax.axis_index("y"), lax.axis_index("c")


def _other_chips(x, y):
    return [(1 - x, y), (x, 1 - y), (1 - x, 1 - y)]


def _scalar(v):
    return jnp.reshape(v, (1,)).astype(jnp.int32)


def _row_tile(rows, cols):
    width = -(-cols // LANES) * LANES * 4
    best = 8
    for d in range(8, rows + 1, 8):
        if rows % d == 0 and d * width <= (1 << 20):
            best = d
    return best


class _Gather:
    def __init__(self, xs, outs, send_sems, recv_sems):
        self.xs, self.outs, self.send_sems, self.recv_sems = xs, outs, send_sems, recv_sems
        self.na = len(xs)
        self.x, self.y, self.c = _position()
        self.me, self.sibling = (self.x, self.y, self.c), (self.x, self.y, 1 - self.c)
        self.chips = _other_chips(self.x, self.y)

    def _copy(self, a, k, block, to, own=False):
        dst = self.outs[a].at[4 * block[0] + 2 * block[1] + block[2]]
        src = dst
        if own:
            half = self.xs[a].shape[0] // 2
            src = self.xs[a].at[pl.ds(pl.multiple_of(self.c * half, 8), half), :]
        return pltpu.make_async_remote_copy(src_ref=src, dst_ref=dst, send_sem=self.send_sems.at[a, k],
                                            recv_sem=self.recv_sems.at[a, k], device_id=to, device_id_type=MESH)

    def _firsts(self):
        return [self._copy(a, j, self.me, (*chip, self.c), own=True) for a in range(self.na) for j, chip in enumerate(self.chips)]

    def _relays(self):
        return [self._copy(a, 3 + j, (*chip, self.c), self.sibling) for j, chip in enumerate(self.chips) for a in range(self.na)]

    def start(self):
        for cp in self._firsts():
            cp.start()

    def relay(self):
        for j, chip in enumerate(self.chips):
            for a in range(self.na):
                self._copy(a, j, (*chip, self.c), self.me).wait_recv()
                self._copy(a, 3 + j, (*chip, self.c), self.sibling).start()

    def finish(self):
        for a in range(self.na):
            for j, chip in enumerate(self.chips):
                self._copy(a, 3 + j, (*chip, 1 - self.c), self.me).wait_recv()
        for cp in self._firsts() + self._relays():
            cp.wait_send()

    @staticmethod
    def out_shapes(shards):
        return [jax.ShapeDtypeStruct((8, s.shape[0] // 2, s.shape[1]), s.dtype) for s in shards]

    @staticmethod
    def semaphores(na):
        return [pltpu.SemaphoreType.DMA((na, 6)), pltpu.SemaphoreType.DMA((na, 6))]


def _gather_shards(shards):
    na = len(shards)

    def body(*refs):
        g = _Gather(refs[:na], refs[na:2 * na], refs[2 * na], refs[2 * na + 1])
        g.start()
        g.relay()
        g.finish()

    return pl.pallas_call(
        body, name="gather_weights", out_shape=_Gather.out_shapes(shards),
        in_specs=[HBM_SPEC] * na, out_specs=[HBM_SPEC] * na, scratch_shapes=_Gather.semaphores(na),
    )(*shards)


def _sibling_exchange(gs, name):
    na = len(gs)
    halves = [g.shape[1] // 2 for g in gs]

    def body(*refs):
        srcs, gots, send_sems, recv_sems = refs[:na], refs[na:2 * na], refs[2 * na], refs[2 * na + 1]
        x, y, c = _position()
        copies = [pltpu.make_async_remote_copy(
            src_ref=srcs[a].at[:, pl.ds(pl.multiple_of((1 - c) * halves[a], 8), halves[a]), :], dst_ref=gots[a],
            send_sem=send_sems.at[a], recv_sem=recv_sems.at[a], device_id=(x, y, 1 - c), device_id_type=MESH) for a in range(na)]
        for cp in copies:
            cp.start()
        for cp in copies:
            cp.wait()

    return pl.pallas_call(
        body, name=name,
        out_shape=[jax.ShapeDtypeStruct((N_CHIPS, h, g.shape[2]), g.dtype) for h, g in zip(halves, gs)],
        in_specs=[HBM_SPEC] * na, out_specs=[HBM_SPEC] * na,
        scratch_shapes=[pltpu.SemaphoreType.DMA((na,)), pltpu.SemaphoreType.DMA((na,))],
    )(*gs)


class _ChipExchange:
    def __init__(self, srcs, gots, send_sems, recv_sems):
        self.srcs, self.gots, self.send_sems, self.recv_sems = srcs, gots, send_sems, recv_sems
        self.na = len(srcs)
        x, y, self.c = _position()
        self.mine = 2 * x + y
        self.chips = _other_chips(x, y)

    def _copy(self, a, j, arriving):
        cx, cy = self.chips[j]
        theirs = 2 * cx + cy
        src = self.srcs[a].at[self.mine if arriving else theirs]
        dst = self.gots[a].at[theirs if arriving else self.mine]
        return pltpu.make_async_remote_copy(src_ref=src, dst_ref=dst, send_sem=self.send_sems.at[a, j],
                                            recv_sem=self.recv_sems.at[a, j], device_id=(cx, cy, self.c), device_id_type=MESH)

    def start(self):
        for a in range(self.na):
            for j in range(3):
                self._copy(a, j, False).start()

    def finish(self):
        for a in range(self.na):
            for j in range(3):
                self._copy(a, j, True).wait_recv()
        for a in range(self.na):
            for j in range(3):
                self._copy(a, j, False).wait_send()

    @staticmethod
    def semaphores(na):
        return [pltpu.SemaphoreType.DMA((na, 3)), pltpu.SemaphoreType.DMA((na, 3))]


def _chip_exchange(parts):
    na = len(parts)

    def body(*refs):
        ex = _ChipExchange(refs[:na], refs[na:2 * na], refs[2 * na], refs[2 * na + 1])
        ex.start()
        ex.finish()

    return pl.pallas_call(
        body, name="reduce_chips", out_shape=[jax.ShapeDtypeStruct(p.shape, p.dtype) for p in parts],
        in_specs=[HBM_SPEC] * na, out_specs=[HBM_SPEC] * na, scratch_shapes=_ChipExchange.semaphores(na),
    )(*parts)


def _sibling_send(halves):
    na = len(halves)

    def body(*refs):
        srcs, gots, send_sems, recv_sems = refs[:na], refs[na:2 * na], refs[2 * na], refs[2 * na + 1]
        x, y, c = _position()
        copies = [pltpu.make_async_remote_copy(src_ref=srcs[a], dst_ref=gots[a], send_sem=send_sems.at[a], recv_sem=recv_sems.at[a],
                                               device_id=(x, y, 1 - c), device_id_type=MESH) for a in range(na)]
        for cp in copies:
            cp.start()
        for cp in copies:
            cp.wait()

    return pl.pallas_call(
        body, name="reduce_gather", out_shape=[jax.ShapeDtypeStruct(h.shape, h.dtype) for h in halves],
        in_specs=[HBM_SPEC] * na, out_specs=[HBM_SPEC] * na,
        scratch_shapes=[pltpu.SemaphoreType.DMA((na,)), pltpu.SemaphoreType.DMA((na,))],
    )(*halves)


def _add_own_half(g, got, c, dtype, name):
    _, r, cols = g.shape
    r2 = r // 2
    tr = _row_tile(r2, cols)
    nrt = r2 // tr

    def body(c_ref, g_ref, got_ref, o_ref):
        o_ref[...] = (g_ref[...] + got_ref[...]).astype(o_ref.dtype)

    blk = (1, tr, cols)
    return pl.pallas_call(
        body, name=name,
        grid_spec=pltpu.PrefetchScalarGridSpec(
            num_scalar_prefetch=1, grid=(N_CHIPS, nrt),
            in_specs=[pl.BlockSpec(blk, lambda j, i, c_: (j, c_[0] * nrt + i, 0)), pl.BlockSpec(blk, lambda j, i, c_: (j, i, 0))],
            out_specs=pl.BlockSpec(blk, lambda j, i, c_: (j, i, 0))),
        out_shape=jax.ShapeDtypeStruct((N_CHIPS, r2, cols), dtype), compiler_params=_params(2),
    )(c, g, got)


def _add_chips(part, got, mine, name):
    _, r2, cols = part.shape
    tr = _row_tile(r2, cols)

    def body(m_ref, p_ref, g0, g1, g2, g3, o_ref):
        t = [jnp.where(m_ref[0] == k, p_ref[0], g[0]).astype(F32) for k, g in enumerate((g0, g1, g2, g3))]
        o_ref[...] = ((t[0] + t[1]) + t[2]) + t[3]

    blk = (1, tr, cols)
    others = [pl.BlockSpec(blk, functools.partial(lambda i, m, k: (jnp.where(m[0] == k, (k + 1) % N_CHIPS, k), i, 0), k=k))
              for k in range(N_CHIPS)]
    return pl.pallas_call(
        body, name=name,
        grid_spec=pltpu.PrefetchScalarGridSpec(
            num_scalar_prefetch=1, grid=(r2 // tr,),
            in_specs=[pl.BlockSpec(blk, lambda i, m: (m[0], i, 0))] + others,
            out_specs=pl.BlockSpec((tr, cols), lambda i, m: (i, 0))),
        out_shape=jax.ShapeDtypeStruct((r2, cols), F32), compiler_params=_params(1),
    )(mine, part, got, got, got, got)


def _adamw(w, own, other, m, v, c, name):
    r, cols = w.shape
    r2 = r // 2
    tr = _row_tile(r2, cols)
    nrt = r2 // tr
    c1 = 1.0 / (1.0 - ADAM_B1 ** ADAM_STEP)
    c2 = 1.0 / (1.0 - ADAM_B2 ** ADAM_STEP)

    def body(c_ref, w_ref, own_ref, other_ref, m_ref, v_ref, g_out, d_out, m_out, v_out):
        g_ = jnp.where(pl.program_id(0) == c_ref[0], own_ref[...], other_ref[...])
        m_new = ADAM_B1 * m_ref[...] + (1.0 - ADAM_B1) * g_
        v_new = ADAM_B2 * v_ref[...] + (1.0 - ADAM_B2) * (g_ * g_)
        g_out[...] = g_
        d_out[...] = -ADAM_LR * ((m_new * c1) / (jnp.sqrt(v_new * c2) + ADAM_EPS) + ADAM_WD * w_ref[...])
        m_out[...] = m_new
        v_out[...] = v_new

    full = pl.BlockSpec((tr, cols), lambda h, i, c_: (h * nrt + i, 0))
    half = pl.BlockSpec((tr, cols), lambda h, i, c_: (i, 0))
    out = jax.ShapeDtypeStruct((r, cols), F32)
    return pl.pallas_call(
        body, name=name,
        grid_spec=pltpu.PrefetchScalarGridSpec(num_scalar_prefetch=1, grid=(2, nrt), in_specs=[full, half, half, full, full],
                                               out_specs=[full] * 4),
        out_shape=[out] * 4, compiler_params=_params(2),
    )(c, w, own, other, m, v)


def _to_rows(flat, rows):
    return jnp.pad(flat, (0, rows * LANES - flat.shape[0])).reshape(rows, LANES)


def _pack_small(tree):
    return _to_rows(jnp.concatenate([tree[n].astype(F32).reshape(-1) for n in SMALL]), SMALL_ROWS)


def _unpack_small(packed, shapes):
    flat, out, at = packed.reshape(-1), {}, 0
    for n in SMALL:
        size = int(np.prod(shapes[n]))
        out[n] = flat[at:at + size].reshape(shapes[n])
        at += size
    return out


def _pack_small_by_chip(grads):
    pieces = []
    for n in SMALL:
        g = grads[n].astype(F32)
        if n in SMALL_SHARDED:
            pieces.append(_by_chip(g).reshape(N_CHIPS, -1))
        else:
            pieces.append(jnp.broadcast_to(g.reshape(1, -1), (N_CHIPS, g.size)))
    flat = jnp.concatenate(pieces, axis=1)
    return jnp.pad(flat, ((0, 0), (0, SMALL_ROWS * LANES - flat.shape[1]))).reshape(N_CHIPS, SMALL_ROWS, LANES)


def _bf16_shard(local, n):
    return local[n].reshape(local[n].shape[-2:]).astype(BF16)


def _all_chips(shard, slabs):
    x, y, _ = _position()
    is_mine = (lax.broadcasted_iota(jnp.int32, (N_CHIPS, 1, 1), 0) == 2 * x + y)
    return jnp.where(is_mine, shard[None], slabs.reshape((N_CHIPS,) + shard.shape))


def _assemble(names, shards, slabs):
    full = [_all_chips(s, g) for s, g in zip(shards, slabs)]
    return [_from_chips(f) if n in BIG_COL_SHARDED else f.reshape(N_CHIPS * f.shape[1], f.shape[2]) for n, f in zip(names, full)]


def _gather_first(local):
    shards = [_bf16_shard(local, "w_in"),
              _to_rows(jnp.concatenate([local[n].astype(F32).reshape(-1) for n in SMALL_SHARDED]), GATHER_SMALL_ROWS)]
    slabs = _gather_shards(shards)
    out = {"w_in": _assemble(["w_in"], shards[:1], slabs[:1])[0]}
    flat, at = _all_chips(shards[1], slabs[1]).reshape(N_CHIPS, -1), 0
    for n in SMALL_SHARDED:
        shape = local[n].shape[-2:]
        size = int(np.prod(shape))
        out[n] = _from_chips(flat[:, at:at + size].reshape((N_CHIPS,) + shape))
        at += size
    return out


def kernel(x, meta_tokens, norm1_gain, w_in, fox_b_f, q_norm_gain, k_norm_gain, hg_lb_logits, hg_out_gain, w_branch_a, w_branch_b, w_out, norm2_gain, w_up, conv_w, conv_b, w_down, loss_target, m_meta_tokens, m_norm1_gain, m_w_in, m_fox_b_f, m_q_norm_gain, m_k_norm_gain, m_hg_lb_logits, m_hg_out_gain, m_w_branch_a, m_w_branch_b, m_w_out, m_norm2_gain, m_w_up, m_conv_w, m_conv_b, m_w_down, v_meta_tokens, v_norm1_gain, v_w_in, v_fox_b_f, v_q_norm_gain, v_k_norm_gain, v_hg_lb_logits, v_hg_out_gain, v_w_branch_a, v_w_branch_b, v_w_out, v_norm2_gain, v_w_up, v_conv_w, v_conv_b, v_w_down):
    w_loc = dict(zip(WEIGHT_NAMES, (meta_tokens, norm1_gain, w_in, fox_b_f, q_norm_gain, k_norm_gain, hg_lb_logits, hg_out_gain,
                                    w_branch_a, w_branch_b, w_out, norm2_gain, w_up, conv_w, conv_b, w_down)))
    m_loc = dict(zip(WEIGHT_NAMES, (m_meta_tokens, m_norm1_gain, m_w_in, m_fox_b_f, m_q_norm_gain, m_k_norm_gain, m_hg_lb_logits,
                                    m_hg_out_gain, m_w_branch_a, m_w_branch_b, m_w_out, m_norm2_gain, m_w_up, m_conv_w, m_conv_b,
                                    m_w_down)))
    v_loc = dict(zip(WEIGHT_NAMES, (v_meta_tokens, v_norm1_gain, v_w_in, v_fox_b_f, v_q_norm_gain, v_k_norm_gain, v_hg_lb_logits,
                                    v_hg_out_gain, v_w_branch_a, v_w_branch_b, v_w_out, v_norm2_gain, v_w_up, v_conv_w, v_conv_b,
                                    v_w_down)))
    local_shapes = {n: tuple(w_loc[n].shape) for n in WEIGHT_NAMES}
    px, py, pc = _position()
    c, mine = _scalar(pc), _scalar(2 * px + py)

    weights = {n: w_loc[n].reshape(w_loc[n].shape[-2:]) for n in SMALL if n not in SMALL_SHARDED}
    weights.update(_gather_first(w_loc))

    lay = _Layout(x.shape[0], x.shape[1])
    loss, grad_x, grads, reduced_late = _local_step(x, loss_target, weights, [_bf16_shard(w_loc, n) for n in LATE], c, mine, lay)
    loss = lax.psum(loss, ("x", "y", "c"))

    names = ["w_in", "small"]
    by_chip = [grads["w_in"], _pack_small_by_chip(grads)]
    from_sibling = _sibling_exchange(by_chip, "reduce_sibling")
    parts = [_add_own_half(g, s, c, F32 if n == "small" else BF16, name=f"reduce_add2_{n}")
             for n, g, s in zip(names, by_chip, from_sibling)]
    from_chips = _chip_exchange(parts)
    own = [_add_chips(p, g, mine, name=f"reduce_add4_{n}") for n, p, g in zip(names, parts, from_chips)]
    names = list(BIG) + ["small"]
    own = [own[0]] + reduced_late + [own[1]]
    other = _sibling_send(own)

    two_d = lambda t: [t[n].reshape(t[n].shape[-2:]) for n in BIG] + [_pack_small(t)]
    results = [_adamw(w_, o_, t_, m_, v_, c, name=f"adamw_{n}")
               for n, w_, o_, t_, m_, v_ in zip(names, two_d(w_loc), own, other, two_d(m_loc), two_d(v_loc))]
    outs = []
    for kind in range(4):
        tree = {n: results[i][kind].reshape(local_shapes[n]) for i, n in enumerate(BIG)}
        tree.update(_unpack_small(results[-1][kind], local_shapes))
        outs += [tree[n] for n in WEIGHT_NAMES]
    return (loss, grad_x, *outs)
```

```python
import functools

import jax
import jax.numpy as jnp
import numpy as np
from jax import lax
from jax.experimental import pallas as pl
from jax.experimental.pallas import tpu as pltpu

F32 = jnp.float32
BF16 = jnp.bfloat16
MXU_DTYPE = BF16
HIGHEST = lax.Precision.HIGHEST

D_MODEL = 1024
N_META = 16
LEAD = 48
ROW0 = LEAD + N_META
FOX_HEADS, FOX_DIM, FOX_W = 8, 64, 512
HG_HEADS, HG_DIM, HG_W = 4, 128, 512
D_FF = 2816
FF2 = 2 * D_FF
EPS = 1e-6
SUB = 16
LANES = 128
N_CHIPS = 4
NEG = -1e30

ADAM_LR, ADAM_B1, ADAM_B2, ADAM_EPS, ADAM_WD, ADAM_STEP = 0.001, 0.9, 0.999, 1e-08, 0.01, 10

VMEM_LIMIT = 56 * 1024 * 1024

C_GA, C_GB = 0, 1
C_FQ, C_FK, C_FV, C_HQ, C_HI, C_HF, C_HG = 4, 5, 6, 7, 8, 9, 10
MAIN_COLS = 11 * 512


def _params(n_axes=1):
    return pltpu.CompilerParams(dimension_semantics=("arbitrary",) * n_axes, vmem_limit_bytes=VMEM_LIMIT)


def _pick(n, cands):
    for c in cands:
        if n % c == 0:
            return c
    raise ValueError(f"no tile for {n} among {cands}")


def _rowwise(fn, rows, consts, outs, reds, *, n_rows, tile, name, into=None):
    assert n_rows % tile == 0
    rows = [r if isinstance(r, tuple) else (r, r.shape[1], 0) for r in rows]
    nr, nc, no = len(rows), len(consts), len(outs)
    aliased = into is not None and not isinstance(into[0], int)
    n_in = nr + nc + (1 if aliased else 0)

    def body(*refs):
        i = pl.program_id(0)
        ins = [r[...] for r in refs[:nr + nc]]
        res = fn(i, *ins)
        res = res if isinstance(res, (tuple, list)) else (res,)
        for ref, v in zip(refs[n_in:n_in + no], res[:no]):
            ref[...] = v.astype(ref.dtype)
        red_refs = refs[n_in + no:]
        if red_refs:
            @pl.when(i == 0)
            def _():
                for ref in red_refs:
                    ref[...] = jnp.zeros_like(ref)
            for ref, v in zip(red_refs, res[no:]):
                ref[...] += v.astype(F32)

    in_specs = [pl.BlockSpec((tile, w), functools.partial(lambda i, j: (i, j), j=j)) for (_, w, j) in rows]
    in_specs += [pl.BlockSpec(c.shape, functools.partial(lambda i, nd: (0,) * nd, nd=c.ndim)) for c in consts]
    out_specs = [pl.BlockSpec((tile, w), lambda i: (i, 0)) for (w, _) in outs]
    out_specs += [pl.BlockSpec(s, functools.partial(lambda i, nd: (0,) * nd, nd=len(s))) for s in reds]
    out_shape = [jax.ShapeDtypeStruct((n_rows, w), dt) for (w, dt) in outs]
    out_shape += [jax.ShapeDtypeStruct(s, F32) for s in reds]
    args = [r[0] for r in rows] + list(consts)
    aliases = {}
    if into is not None:
        out_specs[0] = pl.BlockSpec((tile, outs[0][0]), functools.partial(lambda i, j: (i, j), j=into[1]))
        if aliased:
            in_specs.append(pl.BlockSpec(memory_space=pltpu.HBM))
            args.append(into[0])
            aliases = {n_in - 1: 0}
            out_shape[0] = jax.ShapeDtypeStruct(into[0].shape, into[0].dtype)
        else:
            out_shape[0] = jax.ShapeDtypeStruct((n_rows, into[0]), outs[0][1])
    return pl.pallas_call(
        body, name=name, grid=(n_rows // tile,), in_specs=in_specs, out_specs=out_specs, out_shape=out_shape,
        input_output_aliases=aliases, compiler_params=_params(1),
    )(*args)


def _matmul(a, b, *, trans_a=False, trans_b=False, out_dtype=F32, by_chip=False, name):
    if trans_a:
        k, m = a.shape
    else:
        m, k = a.shape
    n = b.shape[0] if trans_b else b.shape[1]
    assert (b.shape[1] if trans_b else b.shape[0]) == k
    if trans_a:
        tm = _pick(m, (1408, 1024, 512, 256, 128))
        tk = _pick(k, (2176, 1088, 1024, 768, 512, 256))
    else:
        tm = _pick(m, (1088, 512, 256, 128))
        tk = k if k <= 1024 else _pick(k, (1408, 1024, 512))
    nk = k // tk
    wide = (2816,) if nk == 1 and not trans_a else ()
    tn = n // N_CHIPS if by_chip else _pick(n, wide + (1408, 1024, 512, 256, 128))
    dims = (((0 if trans_a else 1,), (1 if trans_b else 0,)), ((), ()))

    def body(a_ref, b_ref, o_ref, acc_ref):
        out = o_ref.at[0] if by_chip else o_ref
        part = lax.dot_general(a_ref[...], b_ref[...], dims, preferred_element_type=F32)
        if nk == 1:
            out[...] = part.astype(out.dtype)
        else:
            kk = pl.program_id(2)

            @pl.when(kk == 0)
            def _():
                acc_ref[...] = part

            @pl.when(kk > 0)
            def _():
                acc_ref[...] += part

            @pl.when(kk == nk - 1)
            def _():
                out[...] = acc_ref[...].astype(out.dtype)

    a_spec = pl.BlockSpec((tk, tm), lambda i, j, kk: (kk, i)) if trans_a else pl.BlockSpec((tm, tk), lambda i, j, kk: (i, kk))
    b_spec = pl.BlockSpec((tn, tk), lambda i, j, kk: (j, kk)) if trans_b else pl.BlockSpec((tk, tn), lambda i, j, kk: (kk, j))
    if by_chip:
        out_spec, out_shape = pl.BlockSpec((1, tm, tn), lambda i, j, kk: (j, i, 0)), (N_CHIPS, m, tn)
    else:
        out_spec, out_shape = pl.BlockSpec((tm, tn), lambda i, j, kk: (i, j)), (m, n)
    return pl.pallas_call(
        body, name=name, grid=(m // tm, n // tn, nk), in_specs=[a_spec, b_spec], out_specs=out_spec,
        out_shape=jax.ShapeDtypeStruct(out_shape, out_dtype),
        scratch_shapes=[pltpu.VMEM((tm, tn) if nk > 1 else (8, LANES), F32)],
        compiler_params=_params(3),
    )(a, b)


def _sigmoid(x):
    return 1.0 / (1.0 + jnp.exp(-x))


def _silu(x):
    return x * _sigmoid(x)


def _log_sigmoid(x):
    return jnp.minimum(x, 0.0) - jnp.log(1.0 + jnp.exp(-jnp.abs(x)))


def _rms(x, gain):
    return x * lax.rsqrt(jnp.mean(x * x, axis=-1, keepdims=True) + EPS) * gain


def _group_matrix(width, group):
    g = (np.arange(width)[:, None] // group == np.arange(LANES)[None, :]).astype(np.float32)
    return jnp.asarray(g, MXU_DTYPE), jnp.asarray(g.T.copy(), MXU_DTYPE)


def _split_dot(x, mat):
    dt = mat.dtype
    hi = x.astype(dt)
    r1 = x - hi.astype(F32)
    mid = r1.astype(dt)
    lo = (r1 - mid.astype(F32)).astype(dt)
    dot = lambda a: jnp.dot(a, mat, preferred_element_type=F32)
    return dot(hi) + dot(mid) + dot(lo)


@jax.custom_vjp
def _group_sum(x, gmat, gmat_t):
    return _split_dot(x, gmat)


@jax.custom_vjp
def _group_spread(s, gmat, gmat_t):
    return _split_dot(s, gmat_t)


_group_sum.defvjp(lambda x, g, gt: (_split_dot(x, g), (g, gt)),
                  lambda res, ct: (_group_spread(ct, *res), jnp.zeros_like(res[0]), jnp.zeros_like(res[1])))
_group_spread.defvjp(lambda s, g, gt: (_split_dot(s, gt), (g, gt)),
                     lambda res, ct: (_group_sum(ct, *res), jnp.zeros_like(res[0]), jnp.zeros_like(res[1])))


def _group_rms(x, gain, gmat, gmat_t, group):
    rstd = lax.rsqrt(_group_sum(x * x, gmat, gmat_t) * (1.0 / group) + EPS)
    return x * _group_spread(rstd, gmat, gmat_t) * gain


def _head_rms(x, gain):
    outs = []
    for h in range(x.shape[1] // LANES):
        xs = x[:, h * LANES:(h + 1) * LANES]
        outs.append(xs * lax.rsqrt(jnp.mean(xs * xs, axis=-1, keepdims=True) + EPS) * gain)
    return jnp.concatenate(outs, axis=1)


class _Layout:
    def __init__(self, batch, seq):
        self.batch, self.seq = batch, seq
        self.l_real = N_META + seq
        self.lp = -(-(LEAD + self.l_real) // 256) * 256
        self.n = batch * self.lp
        self.tile = _pick(self.lp, (512, 256))

    def valid(self, i, tile):
        per = self.lp // tile
        r = lax.rem(i, per) * tile + lax.broadcasted_iota(jnp.int32, (tile, 1), 0)
        return (r >= LEAD) & (r < LEAD + self.l_real)


def _cumsum_rows(x, lay, *, reverse, name):
    t = lay.tile
    nt = lay.lp // t
    c = x.shape[1]

    def body(x_ref, o_ref, carry):
        j = pl.program_id(1)

        @pl.when(j == 0)
        def _():
            carry[...] = jnp.zeros_like(carry)

        r = lax.broadcasted_iota(jnp.int32, (t, t), 0)
        q = lax.broadcasted_iota(jnp.int32, (t, t), 1)
        tri = jnp.where((q >= r) if reverse else (q <= r), 1.0, 0.0).astype(F32)
        xs = x_ref[...]
        out = jnp.dot(tri, xs, precision=HIGHEST, preferred_element_type=F32) + carry[0:1, :]
        o_ref[...] = out
        carry[...] = jnp.broadcast_to(carry[0:1, :] + jnp.sum(xs, axis=0, keepdims=True), carry.shape)

    def idx(b, j):
        return (b * nt + (nt - 1 - j if reverse else j), 0)

    return pl.pallas_call(
        body, name=name, grid=(lay.batch, nt),
        in_specs=[pl.BlockSpec((t, c), idx)], out_specs=pl.BlockSpec((t, c), idx),
        out_shape=jax.ShapeDtypeStruct(x.shape, F32),
        scratch_shapes=[pltpu.VMEM((8, c), F32)],
        compiler_params=_params(2),
    )(x)


def _group_cumsum(x, tile, *, reverse):
    r = lax.rem(lax.broadcasted_iota(jnp.int32, (tile, 1), 0), SUB)
    s = 1
    while s < SUB:
        if reverse:
            x = x + jnp.where(r < SUB - s, pltpu.roll(x, tile - s, 0), 0.0)
        else:
            x = x + jnp.where(r >= s, pltpu.roll(x, s, 0), 0.0)
        s *= 2
    return x


AUG = 128
FOX_BK = 256
FOX_BQ = 256
FOX_SCALE = FOX_DIM ** -0.5
KT_ROWS = FOX_DIM + 16


def _aug_matrices():
    e1 = np.zeros((FOX_W, FOX_HEADS * AUG), np.float32)
    e2 = np.zeros((LANES, FOX_HEADS * AUG), np.float32)
    ones = np.zeros((1, FOX_HEADS * AUG), np.float32)
    for h in range(FOX_HEADS):
        for d in range(FOX_DIM):
            e1[h * FOX_DIM + d, h * AUG + d] = 1.0
        for j in range(3):
            e2[j * FOX_HEADS + h, h * AUG + FOX_DIM + j] = 1.0
            ones[0, h * AUG + FOX_DIM + j] = 1.0
    return jnp.asarray(e1, MXU_DTYPE), jnp.asarray(e2, MXU_DTYPE), jnp.asarray(ones)


def _fox_augment(q, k, cum, key_ok, e1, e2, ones):
    dt = q.dtype
    c = jnp.where(key_ok, -cum, NEG)
    hi = c.astype(dt)
    r1 = c - hi.astype(F32)
    mid = r1.astype(dt)
    lo = (r1 - mid.astype(F32)).astype(dt)
    lane = lax.broadcasted_iota(jnp.int32, c.shape, 1)
    shift = lambda a, by: pltpu.roll(a.astype(F32), by, 1)
    parts = jnp.where(lane < FOX_HEADS, hi.astype(F32),
                      jnp.where(lane < 2 * FOX_HEADS, shift(mid, FOX_HEADS),
                                jnp.where(lane < 3 * FOX_HEADS, shift(lo, 2 * FOX_HEADS), 0.0))).astype(dt)
    qs = (q.astype(F32) * FOX_SCALE).astype(dt)
    q_aug = jnp.dot(qs, e1, preferred_element_type=F32) + ones
    k_aug = jnp.dot(k, e1, preferred_element_type=F32) + jnp.dot(parts, e2, preferred_element_type=F32)
    return q_aug.astype(dt), k_aug.astype(dt)


def _fox_tile(k_blk, q_blk, k0, q0, masked):
    st = lax.dot_general(k_blk, q_blk, (((1,), (1,)), ((), ())), preferred_element_type=F32)
    if masked:
        keys = k0 + lax.broadcasted_iota(jnp.int32, st.shape, 0)
        qs = q0 + lax.broadcasted_iota(jnp.int32, st.shape, 1)
        st = jnp.where(keys <= qs, st, NEG)
    return st


def _fox_fwd_t(q_aug, k_aug, v_t, shards, lay):
    bk, bq = FOX_BK, FOX_BQ
    nq = lay.lp // bq
    pairs = FOX_HEADS // 2
    ng = len(shards)
    steps = lay.batch * pairs

    def body(*refs):
        q_ref, k_ref, vt_ref = refs[:3]
        ot_ref, lse_ref = refs[3 + ng:5 + ng]
        zeros_ref = refs[5 + 2 * ng]
        gather = _Gather(refs[3:3 + ng], refs[5 + ng:5 + 2 * ng], refs[6 + 2 * ng], refs[7 + 2 * ng])
        step = pl.program_id(0) * pairs + pl.program_id(1)
        pl.when(step == 0)(gather.start)
        pl.when(step == steps // 2)(gather.relay)
        heads = [(slice(hh * AUG, (hh + 1) * AUG), slice(hh * FOX_DIM, (hh + 1) * FOX_DIM)) for hh in range(2)]
        zeros_ref[...] = jnp.zeros_like(zeros_ref)

        def q_loop(qb, _):
            q0 = pl.multiple_of(qb * bq, bq)
            q_blks = [q_ref[pl.ds(q0, bq), lanes] for lanes, _ in heads]

            def scores(kb, h):
                k0 = pl.multiple_of(kb * bk, bk)
                return _fox_tile(k_ref[pl.ds(k0, bk), heads[h][0]], q_blks[h], k0, q0, False)

            def consume(kb, h, state, masked):
                m, l, acc, pend, st = state
                k0 = pl.multiple_of(kb * bk, bk)
                if masked:
                    keys = k0 + lax.broadcasted_iota(jnp.int32, st.shape, 0)
                    qs_ = q0 + lax.broadcasted_iota(jnp.int32, st.shape, 1)
                    st = jnp.where(keys <= qs_, st, NEG)
                m_new = jnp.maximum(m, jnp.max(st, axis=0, keepdims=True))
                alpha = jnp.exp(m - m_new)
                p = jnp.exp(st - m_new)
                l = alpha * l + jnp.sum(p, axis=0, keepdims=True)
                acc = alpha * (acc + pend)
                pend = jnp.dot(vt_ref[heads[h][1], pl.ds(k0, bk)], p.astype(vt_ref.dtype), preferred_element_type=F32)
                return m_new, l, acc, pend

            def k_step(kb, states):
                nxt = [scores(kb + 1, h) for h in range(2)]
                return tuple(consume(kb, h, states[h], False) + (nxt[h],) for h in range(2))

            states = tuple((jnp.full((1, bq), NEG, F32), jnp.zeros((1, bq), F32), zeros_ref[...], zeros_ref[...], scores(0, h))
                           for h in range(2))
            states = lax.fori_loop(0, qb, k_step, states)
            qs = q0 + lax.broadcasted_iota(jnp.int32, (1, bq), 1)
            ok = (qs >= LEAD) & (qs < LEAD + lay.l_real)
            for hh in range(2):
                m, l, acc, pend = consume(qb, hh, states[hh], True)
                ot_ref[heads[hh][1], pl.ds(q0, bq)] = jnp.where(ok, (acc + pend) / l, 0.0).astype(ot_ref.dtype)
                lse_ref[hh, :, pl.ds(q0, bq)] = m + jnp.log(l)
            return 0

        lax.fori_loop(0, nq, q_loop, 0)
        pl.when(step == steps - 1)(gather.finish)

    aug = pl.BlockSpec((lay.lp, 2 * AUG), lambda b, p: (b, p))
    tr = pl.BlockSpec((2 * FOX_DIM, lay.lp), lambda b, p: (p, b))
    outs = pl.pallas_call(
        body, name="fox_fwd", grid=(lay.batch, pairs),
        in_specs=[aug, aug, tr] + [HBM_SPEC] * ng,
        out_specs=[tr, pl.BlockSpec((2, 1, lay.lp), lambda b, p: (b * pairs + p, 0, 0))] + [HBM_SPEC] * ng,
        out_shape=[jax.ShapeDtypeStruct((FOX_W, lay.n), MXU_DTYPE),
                   jax.ShapeDtypeStruct((lay.batch * FOX_HEADS, 1, lay.lp), F32)] + _Gather.out_shapes(shards),
        scratch_shapes=[pltpu.VMEM((FOX_DIM, bq), F32)] + _Gather.semaphores(ng),
        compiler_params=_params(2),
    )(q_aug, k_aug, v_t, *shards)
    return outs[0], outs[1], outs[2:]


def _fox_bwd_t(q_aug, k_aug, v, do, k_t, o_t, do_t, lse, parts, lay):
    bk, bq = FOX_BK, FOX_BQ
    nq, nk = lay.lp // bq, lay.lp // bk
    pairs = FOX_HEADS // 2
    ne = len(parts)

    def body(*refs):
        q_ref, k_ref, v_ref, do_ref, kt_ref, ot_ref, dot_ref, lse_ref = refs[:8]
        dqt_ref, dk_ref, dv_ref = refs[8 + ne:11 + ne]
        delta = refs[11 + 2 * ne]
        exchange = _ChipExchange(refs[8:8 + ne], refs[11 + ne:11 + 2 * ne], refs[12 + 2 * ne], refs[13 + 2 * ne])
        step = pl.program_id(0) * pairs + pl.program_id(1)
        pl.when(step == 0)(exchange.start)
        dqt_ref[...] = jnp.zeros_like(dqt_ref)
        dk_ref[...] = jnp.zeros_like(dk_ref)
        dv_ref[...] = jnp.zeros_like(dv_ref)
        heads = [(hh, slice(hh * AUG, (hh + 1) * AUG), slice(hh * FOX_DIM, (hh + 1) * FOX_DIM),
                  slice(hh * KT_ROWS, (hh + 1) * KT_ROWS)) for hh in range(2)]

        def delta_loop(qb, _):
            q0 = pl.multiple_of(qb * bq, bq)
            for hh, _, cols, _ in heads:
                prod = ot_ref[cols, pl.ds(q0, bq)].astype(F32) * dot_ref[cols, pl.ds(q0, bq)].astype(F32)
                delta[hh, :, pl.ds(q0, bq)] = jnp.sum(prod, axis=0, keepdims=True)
            return 0

        lax.fori_loop(0, nq, delta_loop, 0)

        def k_loop(kb, _):
            k0 = pl.multiple_of(kb * bk, bk)

            def products(qb, h):
                q0 = pl.multiple_of(qb * bq, bq)
                _, lanes, cols, _ = heads[h]
                st = _fox_tile(k_ref[pl.ds(k0, bk), lanes], q_ref[pl.ds(q0, bq), lanes], k0, q0, False)
                dpt = lax.dot_general(v_ref[pl.ds(k0, bk), cols], do_ref[pl.ds(q0, bq), cols], (((1,), (1,)), ((), ())),
                                      preferred_element_type=F32)
                return st, dpt

            def consume(qb, h, st, dpt, masked):
                q0 = pl.multiple_of(qb * bq, bq)
                hh, lanes, cols, trows = heads[h]
                if masked:
                    keys = k0 + lax.broadcasted_iota(jnp.int32, st.shape, 0)
                    qs = q0 + lax.broadcasted_iota(jnp.int32, st.shape, 1)
                    st = jnp.where(keys <= qs, st, NEG)
                q_blk = q_ref[pl.ds(q0, bq), lanes]
                do_blk = do_ref[pl.ds(q0, bq), cols]
                pt = jnp.exp(st - lse_ref[hh, :, pl.ds(q0, bq)])
                dst = (pt * (dpt - delta[hh, :, pl.ds(q0, bq)])).astype(q_blk.dtype)
                dv_ref[pl.ds(k0, bk), cols] += jnp.dot(pt.astype(do_blk.dtype), do_blk, preferred_element_type=F32)
                dk_ref[pl.ds(k0, bk), lanes] += jnp.dot(dst, q_blk, preferred_element_type=F32)
                dqt_ref[trows, pl.ds(q0, bq)] += jnp.dot(kt_ref[trows, pl.ds(k0, bk)], dst, preferred_element_type=F32)

            after = lambda qb: jnp.minimum(qb + 1, nq - 1)
            cur = [products(kb, h) for h in range(2)]
            nxt = tuple(products(after(kb), h) for h in range(2))
            for h in range(2):
                consume(kb, h, *cur[h], True)

            def rest(qb, held):
                new = tuple(products(after(qb), h) for h in range(2))
                for h in range(2):
                    consume(qb, h, *held[h], False)
                return new

            lax.fori_loop(kb + 1, nq, rest, nxt)
            return 0

        lax.fori_loop(0, nk, k_loop, 0)
        pl.when(step == lay.batch * pairs - 1)(exchange.finish)

    aug = pl.BlockSpec((lay.lp, 2 * AUG), lambda b, p: (b, p))
    rows = pl.BlockSpec((lay.lp, 2 * FOX_DIM), lambda b, p: (b, p))
    tr = pl.BlockSpec((2 * FOX_DIM, lay.lp), lambda b, p: (p, b))
    tr_k = pl.BlockSpec((2 * KT_ROWS, lay.lp), lambda b, p: (p, b))
    outs = pl.pallas_call(
        body, name="fox_bwd", grid=(lay.batch, pairs),
        in_specs=[aug, aug, rows, rows, tr_k, tr, tr, pl.BlockSpec((2, 1, lay.lp), lambda b, p: (b * pairs + p, 0, 0))]
        + [HBM_SPEC] * ne,
        out_specs=[tr_k, aug, rows] + [HBM_SPEC] * ne,
        out_shape=[jax.ShapeDtypeStruct((FOX_HEADS * KT_ROWS, lay.n), F32), jax.ShapeDtypeStruct((lay.n, FOX_HEADS * AUG), F32),
                   jax.ShapeDtypeStruct((lay.n, FOX_W), F32)] + [jax.ShapeDtypeStruct(p.shape, p.dtype) for p in parts],
        scratch_shapes=[pltpu.VMEM((2, 1, lay.lp), F32)] + _ChipExchange.semaphores(ne),
        compiler_params=_params(2),
    )(q_aug, k_aug, v, do, k_t, o_t, do_t, lse, *parts)
    return outs[0], outs[1], outs[2], outs[3:]


def _hgrn_fwd(proj, kk, gl, lay):
    t = lay.tile
    nt = lay.lp // t
    nsc = t // SUB

    def body(q_ref, k_ref, g_ref, v_ref, o_ref, st_ref, state, sub_rows):
        @pl.when(pl.program_id(1) == 0)
        def _():
            state[...] = jnp.zeros_like(state)

        rowi = lax.broadcasted_iota(jnp.int32, (SUB, 1), 0)

        def sub(sc, _):
            r0 = pl.multiple_of(sc * SUB, SUB)
            sub_rows[0] = k_ref[pl.ds(r0, SUB), :]
            sub_rows[1] = g_ref[pl.ds(r0, SUB), :]
            sub_rows[2] = v_ref[pl.ds(r0, SUB), :]
            for h in range(HG_HEADS):
                lanes = slice(h * HG_DIM, (h + 1) * HG_DIM)
                q16 = q_ref[pl.ds(r0, SUB), lanes]
                k16 = sub_rows[0, :, lanes]
                g16 = sub_rows[1, :, lanes]
                v16 = sub_rows[2, :, lanes]
                g_end = sub_rows[1, SUB - 1:SUB, lanes]
                s_prev = state[h]
                st_ref[sc, h] = s_prev
                o = lax.dot_general((q16 * jnp.exp(g16)).astype(MXU_DTYPE), s_prev.astype(MXU_DTYPE),
                                    (((1,), (1,)), ((), ())), preferred_element_type=F32)
                for s in range(SUB):
                    ks = sub_rows[0, s:s + 1, lanes]
                    gs = sub_rows[1, s:s + 1, lanes]
                    vs = sub_rows[2, s:s + 1, lanes]
                    w = q16 * jnp.exp(jnp.minimum(g16 - gs, 0.0)) * ks
                    a = jnp.where(rowi >= s, jnp.sum(w, axis=1, keepdims=True), 0.0)
                    o = o + a * vs
                o_ref[pl.ds(r0, SUB), lanes] = o
                kt = k16 * jnp.exp(g_end - g16)
                upd = lax.dot_general(v16.astype(MXU_DTYPE), kt.astype(MXU_DTYPE), (((0,), (0,)), ((), ())),
                                      preferred_element_type=F32)
                state[h] = jnp.exp(g_end) * s_prev + upd
            return 0

        lax.fori_loop(0, nsc, sub, 0)

    rows = lambda col: pl.BlockSpec((t, HG_W), functools.partial(lambda b, i, col: (b * nt + i, col), col=col))
    return pl.pallas_call(
        body, name="hgrn_fwd", grid=(lay.batch, nt),
        in_specs=[rows(C_HQ), rows(0), rows(0), rows(C_HI)],
        out_specs=[rows(0), pl.BlockSpec((nsc, HG_HEADS, HG_DIM, HG_DIM), lambda b, i: (b * nt + i, 0, 0, 0))],
        out_shape=[jax.ShapeDtypeStruct((lay.n, HG_W), F32),
                   jax.ShapeDtypeStruct((lay.n // SUB, HG_HEADS, HG_DIM, HG_DIM), F32)],
        scratch_shapes=[pltpu.VMEM((HG_HEADS, HG_DIM, HG_DIM), F32), pltpu.VMEM((3, SUB, HG_W), F32)],
        compiler_params=_params(2),
    )(proj, kk, gl, proj)


def _hgrn_bwd(proj, kk, gl, do, states, lay):
    t = lay.tile
    nt = lay.lp // t
    nsc = t // SUB

    def body(q_ref, k_ref, g_ref, v_ref, do_ref, st_ref, dq_ref, dk_ref, dv_ref, dg_ref, dstate, sub_rows, row_acc):
        @pl.when(pl.program_id(1) == 0)
        def _():
            dstate[...] = jnp.zeros_like(dstate)

        rowi = lax.broadcasted_iota(jnp.int32, (SUB, 1), 0)

        def sub(it, _):
            sc = nsc - 1 - it
            r0 = pl.multiple_of(sc * SUB, SUB)
            sub_rows[0] = k_ref[pl.ds(r0, SUB), :]
            sub_rows[1] = g_ref[pl.ds(r0, SUB), :]
            sub_rows[2] = v_ref[pl.ds(r0, SUB), :]
            for h in range(HG_HEADS):
                lanes = slice(h * HG_DIM, (h + 1) * HG_DIM)
                q16 = q_ref[pl.ds(r0, SUB), lanes]
                k16 = sub_rows[0, :, lanes]
                g16 = sub_rows[1, :, lanes]
                v16 = sub_rows[2, :, lanes]
                do16 = do_ref[pl.ds(r0, SUB), lanes]
                g_end = sub_rows[1, SUB - 1:SUB, lanes]
                s_prev = st_ref[sc, h]
                ds_end = dstate[h]
                eg = jnp.exp(g16)
                ekt = jnp.exp(g_end - g16)
                e_end = jnp.exp(g_end)
                qt = q16 * eg
                kt = k16 * ekt
                ds_mx = ds_end.astype(MXU_DTYPE)
                dv = lax.dot_general(kt.astype(MXU_DTYPE), ds_mx, (((1,), (1,)), ((), ())), preferred_element_type=F32)
                dkt = jnp.dot(v16.astype(MXU_DTYPE), ds_mx, preferred_element_type=F32)
                dk = dkt * ekt
                ktdkt = kt * dkt
                dg_end = jnp.sum(ktdkt, axis=0, keepdims=True) + jnp.sum(s_prev * ds_end, axis=0, keepdims=True) * e_end
                dg = jnp.where(rowi == SUB - 1, dg_end, 0.0) - ktdkt
                dqt = jnp.dot(do16.astype(MXU_DTYPE), s_prev.astype(MXU_DTYPE), preferred_element_type=F32)
                dq = dqt * eg
                dg = dg + qt * dqt
                dstate[h] = e_end * ds_end + lax.dot_general(do16.astype(MXU_DTYPE), qt.astype(MXU_DTYPE),
                                                             (((0,), (0,)), ((), ())), preferred_element_type=F32)
                for s in range(SUB):
                    ks = sub_rows[0, s:s + 1, lanes]
                    gs = sub_rows[1, s:s + 1, lanes]
                    vs = sub_rows[2, s:s + 1, lanes]
                    live = rowi >= s
                    e = jnp.where(live, jnp.exp(jnp.minimum(g16 - gs, 0.0)), 0.0)
                    qe = q16 * e
                    a = jnp.sum(qe * ks, axis=1, keepdims=True)
                    da = jnp.where(live, jnp.sum(do16 * vs, axis=1, keepdims=True), 0.0)
                    t1 = da * qe
                    dk_row = jnp.sum(t1, axis=0, keepdims=True)
                    dq = dq + da * (e * ks)
                    dg = dg + t1 * ks
                    row_acc[0, s:s + 1, :] = jnp.sum(a * do16, axis=0, keepdims=True)
                    row_acc[1, s:s + 1, :] = dk_row
                    row_acc[2, s:s + 1, :] = ks * dk_row
                dq_ref[pl.ds(r0, SUB), lanes] = dq
                dk_ref[pl.ds(r0, SUB), lanes] = dk + row_acc[1]
                dv_ref[pl.ds(r0, SUB), lanes] = dv + row_acc[0]
                dg_ref[pl.ds(r0, SUB), lanes] = dg - row_acc[2]
            return 0

        lax.fori_loop(0, nsc, sub, 0)

    def rows(col):
        return pl.BlockSpec((t, HG_W), functools.partial(lambda b, i, col: (b * nt + nt - 1 - i, col), col=col))

    out = jax.ShapeDtypeStruct((lay.n, HG_W), F32)
    return pl.pallas_call(
        body, name="hgrn_bwd", grid=(lay.batch, nt),
        in_specs=[rows(C_HQ), rows(0), rows(0), rows(C_HI), rows(0),
                  pl.BlockSpec((nsc, HG_HEADS, HG_DIM, HG_DIM), lambda b, i: (b * nt + nt - 1 - i, 0, 0, 0))],
        out_specs=[rows(0)] * 4, out_shape=[out] * 4,
        scratch_shapes=[pltpu.VMEM((HG_HEADS, HG_DIM, HG_DIM), F32), pltpu.VMEM((3, SUB, HG_W), F32),
                        pltpu.VMEM((3, SUB, HG_DIM), F32)],
        compiler_params=_params(2),
    )(proj, kk, gl, proj, do, states)


CONV_COLS = 1408


def _shift_down(x, halo, tile, by):
    out = pltpu.roll(x, by, 0)
    rowi = lax.broadcasted_iota(jnp.int32, (8, 1), 0)
    top = out[0:8]
    for r in range(by):
        top = jnp.where(rowi == r, halo[8 - by + r:8 - by + r + 1, :], top)
    return jnp.concatenate([top, out[8:]], axis=0)


def _shift_up(x, halo, tile, by):
    out = pltpu.roll(x, tile - by, 0)
    rowi = lax.broadcasted_iota(jnp.int32, (8, 1), 0)
    bottom = out[tile - 8:]
    for r in range(by):
        bottom = jnp.where(rowi == 8 - by + r, halo[r:r + 1, :], bottom)
    return jnp.concatenate([out[:tile - 8], bottom], axis=0)


def _conv_specs(tile):
    ncb = D_FF // CONV_COLS
    per8 = tile // 8

    def tile_spec(off):
        return pl.BlockSpec((tile, CONV_COLS), functools.partial(lambda i, j, off: (i, j + off), off=off))

    def prev_spec(off):
        return pl.BlockSpec((8, CONV_COLS), functools.partial(lambda i, j, off: (jnp.maximum(i * per8 - 1, 0), j + off), off=off))

    def w_spec(off):
        return pl.BlockSpec((3, CONV_COLS), functools.partial(lambda i, j, off: (0, j + off), off=off))

    def b_spec(off):
        return pl.BlockSpec((1, CONV_COLS), functools.partial(lambda i, j, off: (0, j + off), off=off))

    return ncb, tile_spec, prev_spec, w_spec, b_spec


def _conv3(x, halo, w, b, tile):
    return w[0:1, :] * _shift_down(x, halo, tile, 2) + w[1:2, :] * _shift_down(x, halo, tile, 1) + w[2:3, :] * x + b


def _conv_act_fwd(u, conv_w, conv_b, lay):
    tile = lay.tile
    ncb, tile_spec, prev_spec, w_spec, b_spec = _conv_specs(tile)

    def body(ug, uv, pg, pv, wg, wv, bg, bv, o_ref):
        cg = _conv3(ug[...], pg, wg, bg[...], tile)
        cv = _conv3(uv[...], pv, wv, bv[...], tile)
        o_ref[...] = (_silu(cg) * cv).astype(o_ref.dtype)

    return pl.pallas_call(
        body, name="conv_act_fwd", grid=(lay.n // tile, ncb),
        in_specs=[tile_spec(0), tile_spec(ncb), prev_spec(0), prev_spec(ncb), w_spec(0), w_spec(ncb), b_spec(0), b_spec(ncb)],
        out_specs=pl.BlockSpec((tile, CONV_COLS), lambda i, j: (i, j)),
        out_shape=jax.ShapeDtypeStruct((lay.n, D_FF), MXU_DTYPE),
        compiler_params=_params(2),
    )(u, u, u, u, conv_w, conv_w, conv_b, conv_b)


def _conv_act_bwd(u, dact, conv_w, conv_b, lay):
    tile = lay.tile
    ncb, tile_spec, prev_spec, w_spec, b_spec = _conv_specs(tile)

    def body(ug, uv, pg, pv, wg, wv, bg, bv, da_ref, dg_ref, dv_ref, gwg, gwv, gbg, gbv):
        @pl.when(pl.program_id(1) == 0)
        def _():
            for r in (gwg, gwv, gbg, gbv):
                r[...] = jnp.zeros_like(r)

        xg, xv = ug[...], uv[...]
        cg = _conv3(xg, pg, wg, bg[...], tile)
        cv = _conv3(xv, pv, wv, bv[...], tile)
        da = da_ref[...].astype(F32)
        sg = _sigmoid(cg)
        dcv = da * (cg * sg)
        dcg = da * cv * (sg * (1.0 + cg * (1.0 - sg)))
        dg_ref[...] = dcg
        dv_ref[...] = dcv
        for x, halo, dc, gw, gb in ((xg, pg, dcg, gwg, gbg), (xv, pv, dcv, gwv, gbv)):
            gw[0, 0:1, :] += jnp.sum(dc * _shift_down(x, halo, tile, 2), axis=0, keepdims=True)
            gw[0, 1:2, :] += jnp.sum(dc * _shift_down(x, halo, tile, 1), axis=0, keepdims=True)
            gw[0, 2:3, :] += jnp.sum(dc * x, axis=0, keepdims=True)
            gb[0] += jnp.sum(dc, axis=0, keepdims=True)

    swap = lambda spec: pl.BlockSpec(spec.block_shape, functools.partial(lambda j, i, f: f(i, j), f=spec.index_map))
    col = lambda j, i: (i, j)
    red_w = pl.BlockSpec((1, 3, CONV_COLS), lambda j, i: (j, 0, 0))
    red_b = pl.BlockSpec((1, 1, CONV_COLS), lambda j, i: (j, 0, 0))
    outs = pl.pallas_call(
        body, name="conv_act_bwd", grid=(ncb, lay.n // tile),
        in_specs=[swap(s) for s in (tile_spec(0), tile_spec(ncb), prev_spec(0), prev_spec(ncb), w_spec(0), w_spec(ncb),
                                    b_spec(0), b_spec(ncb))] + [pl.BlockSpec((tile, CONV_COLS), col)],
        out_specs=[pl.BlockSpec((tile, CONV_COLS), col), pl.BlockSpec((tile, CONV_COLS), col), red_w, red_w, red_b, red_b],
        out_shape=[jax.ShapeDtypeStruct((lay.n, D_FF), F32), jax.ShapeDtypeStruct((lay.n, D_FF), F32),
                   jax.ShapeDtypeStruct((ncb, 3, CONV_COLS), F32), jax.ShapeDtypeStruct((ncb, 3, CONV_COLS), F32),
                   jax.ShapeDtypeStruct((ncb, 1, CONV_COLS), F32), jax.ShapeDtypeStruct((ncb, 1, CONV_COLS), F32)],
        compiler_params=_params(2),
    )(u, u, u, u, conv_w, conv_w, conv_b, conv_b, dact)
    dcg, dcv, gwg, gwv, gbg, gbv = outs
    unblock = lambda g: jnp.transpose(g, (1, 0, 2)).reshape(g.shape[1], D_FF)
    g_w = jnp.concatenate([unblock(gwg), unblock(gwv)], axis=1)
    g_b = jnp.concatenate([unblock(gbg), unblock(gbv)], axis=1)
    return dcg, dcv, g_w, g_b


def _conv_input_bwd(dcg, dcv, conv_w, lay):
    tile = lay.tile
    ncb = D_FF // CONV_COLS
    nblk8 = lay.n // 8
    per8 = tile // 8
    nxt = lambda i: jnp.minimum((i + 1) * per8, nblk8 - 1)

    def half(dc, off, into, name):
        def body(*refs):
            d, halo, w, o = refs[0], refs[1], refs[2], refs[-1]
            x = d[...]
            du = w[2:3, :] * x + w[1:2, :] * _shift_up(x, halo, tile, 1) + w[0:1, :] * _shift_up(x, halo, tile, 2)
            o[...] = jnp.where(lay.valid(pl.program_id(0), tile), du, 0.0).astype(o.dtype)

        in_specs = [pl.BlockSpec((tile, CONV_COLS), lambda i, j: (i, j)),
                    pl.BlockSpec((8, CONV_COLS), lambda i, j: (nxt(i), j)),
                    pl.BlockSpec((3, CONV_COLS), lambda i, j: (0, j + off))]
        args = [dc, dc, conv_w]
        if into is not None:
            in_specs.append(pl.BlockSpec(memory_space=pltpu.HBM))
            args.append(into)
        return pl.pallas_call(
            body, name=name, grid=(lay.n // tile, ncb), in_specs=in_specs,
            out_specs=pl.BlockSpec((tile, CONV_COLS), lambda i, j: (i, j + off)),
            out_shape=jax.ShapeDtypeStruct((lay.n, FF2), MXU_DTYPE),
            input_output_aliases={} if into is None else {3: 0},
            compiler_params=_params(2),
        )(*args)

    return half(dcv, ncb, half(dcg, 0, None, "conv_input_bwd_gate"), "conv_input_bwd_value")


def _loss_head(h1, mlp, target, lay):
    t, sub = 256, ROW0
    per = lay.lp // t
    nsub = t // sub
    nreal = lay.seq // sub

    def body(h_ref, m_ref, *rest):
        t_refs, (loss_ref, dy_ref, dyb_ref) = rest[:nsub], rest[nsub:]
        b, j = pl.program_id(0), pl.program_id(1)

        @pl.when((b == 0) & (j == 0))
        def _():
            loss_ref[...] = jnp.zeros_like(loss_ref)

        rows_ = j * t + lax.broadcasted_iota(jnp.int32, (t, 1), 0)
        real = (rows_ >= ROW0) & (rows_ < ROW0 + lay.seq)
        tgt_ = jnp.concatenate([r[...] for r in t_refs], axis=0)
        err = jnp.where(real, h_ref[...] + m_ref[...] - tgt_, 0.0)
        dy = err * (1.0 / D_MODEL)
        dy_ref[...] = dy
        dyb_ref[...] = dy.astype(dyb_ref.dtype)
        loss_ref[...] += 0.5 * jnp.sum(err * dy)

    rows = pl.BlockSpec((t, D_MODEL), lambda b, j: (b * per + j, 0))
    tgt = [pl.BlockSpec((sub, D_MODEL), functools.partial(
        lambda b, j, r: (b * nreal + jnp.clip(j * nsub + r - 1, 0, nreal - 1), 0), r=r)) for r in range(nsub)]
    return pl.pallas_call(
        body, name="loss_head", grid=(lay.batch, per),
        in_specs=[rows, rows] + tgt,
        out_specs=[pl.BlockSpec((8, LANES), lambda b, j: (0, 0)), rows, rows],
        out_shape=[jax.ShapeDtypeStruct((8, LANES), F32), jax.ShapeDtypeStruct((lay.n, D_MODEL), F32),
                   jax.ShapeDtypeStruct((lay.n, D_MODEL), MXU_DTYPE)],
        compiler_params=_params(2),
    )(h1, mlp, *([target] * nsub))


def _fox_prep(fq, fk, ff, gq, gk, bf, gmat, gmat_t, valid):
    q = _group_rms(fq, gq, gmat, gmat_t, FOX_DIM)
    k = _group_rms(fk, gk, gmat, gmat_t, FOX_DIM)
    logf = jnp.where(valid, _log_sigmoid(ff + bf), 0.0)
    return q, k, logf


def _hg_prep(hf, l0, l1):
    mx = jnp.maximum(l0, l1)
    e0, e1 = jnp.exp(l0 - mx), jnp.exp(l1 - mx)
    lb = e0 / (e0 + e1)
    lf = jnp.log(lb + (1.0 - lb) * _sigmoid(hf))
    kk = (1.0 - lb) * _sigmoid(-hf)
    return lf, kk


def _hg_post(o, hg, gain):
    return _head_rms(o, gain) * _silu(hg)


def _gate(ga, gb, ya, yb):
    return _sigmoid(ga) * ya + _sigmoid(gb) * yb


def _by_chip(g):
    return jnp.transpose(g.reshape(g.shape[0], N_CHIPS, g.shape[1] // N_CHIPS), (1, 0, 2))


def _from_chips(a):
    return jnp.transpose(a, (1, 0, 2)).reshape(a.shape[1], N_CHIPS * a.shape[2])


def _local_step(x, target, w, late_shards, c, mine, lay):
    n, tile = lay.n, lay.tile
    rw = functools.partial(_rowwise, n_rows=n, tile=tile)
    mx = lambda a: a.astype(MXU_DTYPE)

    w_in = w["w_in"]
    fq, fk, fv, ffw, hq, hf, hi, hg, ga, gb = jnp.split(w_in, list(np.cumsum([512, 512, 512, 8, 512, 512, 512, 512, 1024])), axis=1)
    w_main = mx(jnp.concatenate([ga, gb, fq, fk, fv, hq, hi, hf, hg], axis=1))
    w_ff = mx(jnp.pad(ffw, ((0, 0), (0, LANES - FOX_HEADS))))
    conv_w, conv_b = w["conv_w"].astype(F32), w["conv_b"].astype(F32)
    g1, g2 = w["norm1_gain"], w["norm2_gain"]
    gq, gk = jnp.tile(w["q_norm_gain"], (1, FOX_HEADS)), jnp.tile(w["k_norm_gain"], (1, FOX_HEADS))
    bf = jnp.pad(w["fox_b_f"], ((0, 0), (0, LANES - FOX_HEADS)))
    lb_logits, hg_gain = w["hg_lb_logits"], w["hg_out_gain"]
    gm64, gm64_t = _group_matrix(FOX_W, FOX_DIM)

    meta = jnp.broadcast_to(w["meta_tokens"].astype(F32)[None], (lay.batch, N_META, D_MODEL))
    h0 = jnp.concatenate([jnp.zeros((lay.batch, LEAD, D_MODEL), F32), meta, x,
                          jnp.zeros((lay.batch, lay.lp - LEAD - lay.l_real, D_MODEL), F32)], axis=1).reshape(n, D_MODEL)

    (xn,) = rw(lambda i, h, g: _rms(h, g), [h0], [g1], [(D_MODEL, MXU_DTYPE)], [], name="norm1")
    proj = _matmul(xn, w_main, name="proj_main")
    pff = _matmul(xn, w_ff, name="proj_ff")

    def fox_prep_fn(i, a, b_, v_, f_, gq_, gk_, bf_, m_, mt_):
        q_, k_, logf = _fox_prep(a, b_, f_, gq_, gk_, bf_, m_, mt_, lay.valid(i, tile))
        return q_, k_, v_, logf

    q, k, v, logf = rw(fox_prep_fn, [(proj, 512, C_FQ), (proj, 512, C_FK), (proj, 512, C_FV), pff], [gq, gk, bf, gm64, gm64_t],
                       [(512, MXU_DTYPE), (512, MXU_DTYPE), (512, MXU_DTYPE), (LANES, F32)], [], name="fox_prep")
    cum = _cumsum_rows(logf, lay, reverse=False, name="fox_cum")
    e1, e2, aug_ones = _aug_matrices()
    q_aug, k_aug = rw(lambda i, q_, k_, c_, e1_, e2_, on_: _fox_augment(q_, k_, c_, lay.valid(i, tile), e1_, e2_, on_),
                      [q, k, cum], [e1, e2, aug_ones], [(FOX_HEADS * AUG, MXU_DTYPE)] * 2, [], name="fox_aug")
    o_t, lse, late_slabs = _fox_fwd_t(q_aug, k_aug, v.T, late_shards, lay)
    w_a, w_b, w_out, w_up, w_down = [mx(a) for a in _assemble(LATE, late_shards, late_slabs)]

    def hg_prep_fn(i, hf_, l0, l1):
        lf, kk_ = _hg_prep(hf_, l0, l1)
        return kk_, _group_cumsum(lf, tile, reverse=False)

    lb0, lb1 = lb_logits[0:1], lb_logits[1:2]
    kk, gl = rw(hg_prep_fn, [(proj, 512, C_HF)], [lb0, lb1], [(512, F32), (512, F32)], [], name="hg_prep")
    o_hg, states = _hgrn_fwd(proj, kk, gl, lay)
    (oh,) = rw(lambda i, o, g_, gain: _hg_post(o, g_, gain), [o_hg, (proj, 512, C_HG)], [hg_gain], [(512, MXU_DTYPE)], [],
               name="hg_post")
    ya = _matmul(oh, w_a, out_dtype=MXU_DTYPE, name="branch_a")
    yb = _matmul(o_t, w_b, trans_a=True, out_dtype=MXU_DTYPE, name="branch_b")
    pga, pgb = (proj, 1024, C_GA), (proj, 1024, C_GB)
    gate_fn = lambda a, b_, c_, d_: _gate(a, b_, c_.astype(F32), d_.astype(F32))
    (merged,) = rw(lambda i, a, b_, c_, d_: gate_fn(a, b_, c_, d_), [pga, pgb, ya, yb], [], [(D_MODEL, MXU_DTYPE)], [], name="gate")
    mo = _matmul(merged, w_out, name="out_proj")
    h1, hn = rw(lambda i, h, m_, g: (h + m_, _rms(h + m_, g)), [h0, mo], [g2], [(D_MODEL, F32), (D_MODEL, MXU_DTYPE)], [],
                name="norm2")
    u = _matmul(hn, w_up, name="up_proj")
    act = _conv_act_fwd(u, conv_w, conv_b, lay)
    mlp = _matmul(act, w_down, name="down_proj")
    loss_blk, dy, dyb = _loss_head(h1, mlp, target.reshape(lay.batch * lay.seq, D_MODEL), lay)
    loss = loss_blk[0, 0]

    grads = {}
    dact = _matmul(dyb, w_down, trans_b=True, out_dtype=MXU_DTYPE, name="down_bwd_x")
    grads["w_down"] = _matmul(act, dyb, trans_a=True, name="down_bwd_w").reshape(N_CHIPS, D_FF // N_CHIPS, D_MODEL)
    dcg, dcv, grads["conv_w"], grads["conv_b"] = _conv_act_bwd(u, dact, conv_w, conv_b, lay)
    du = _conv_input_bwd(dcg, dcv, conv_w, lay)
    dhn = _matmul(du, w_up, trans_b=True, name="up_bwd_x")
    grads["w_up"] = _matmul(hn, du, trans_a=True, by_chip=True, name="up_bwd_w")

    def norm2_bwd(i, h, d_, dy_, g):
        _, vjp = jax.vjp(_rms, h, g)
        dh, dg = vjp(d_)
        return dh + dy_, dh + dy_, dg

    dh1, dh1b, grads["norm2_gain"] = rw(norm2_bwd, [h1, dhn, dy], [g2], [(D_MODEL, F32), (D_MODEL, MXU_DTYPE)], [(1, D_MODEL)],
                                        name="norm2_bwd")
    dmerged = _matmul(dh1b, w_out, trans_b=True, out_dtype=MXU_DTYPE, name="out_bwd_x")
    grads["w_out"] = _matmul(merged, dh1b, trans_a=True, name="out_bwd_w").reshape(N_CHIPS, D_MODEL // N_CHIPS, D_MODEL)

    def gate_bwd(i, a, b_, c_, d_, dm):
        _, vjp = jax.vjp(gate_fn, a, b_, c_, d_)
        da, db, dc, dd = vjp(dm.astype(F32))
        return jnp.concatenate([da, db], axis=1), dc, dd

    dproj, dya, dyb_ = rw(gate_bwd, [pga, pgb, ya, yb, dmerged], [], [(2 * D_MODEL, MXU_DTYPE)] + [(D_MODEL, MXU_DTYPE)] * 2, [],
                          name="gate_bwd", into=(MAIN_COLS, 0))
    doh = _matmul(dya, w_a, trans_b=True, out_dtype=MXU_DTYPE, name="branch_a_bwd_x")
    grads["w_branch_a"] = _matmul(oh, dya, trans_a=True, by_chip=True, name="branch_a_bwd_w")
    dofox = _matmul(dyb_, w_b, trans_b=True, out_dtype=MXU_DTYPE, name="branch_b_bwd_x")
    grads["w_branch_b"] = _matmul(o_t, dyb_, by_chip=True, name="branch_b_bwd_w")

    def hg_post_bwd(i, o, g_, d_, gain):
        _, vjp = jax.vjp(_hg_post, o, g_, gain)
        do_, dg_, dgain = vjp(d_.astype(F32))
        return dg_, do_, dgain

    dproj, do_hg, grads["hg_out_gain"] = rw(hg_post_bwd, [o_hg, (proj, 512, C_HG), doh], [hg_gain],
                                            [(512, MXU_DTYPE), (512, F32)], [(1, HG_DIM)], name="hg_post_bwd", into=(dproj, C_HG))
    dhq, dkk, dhi, dgl = _hgrn_bwd(proj, kk, gl, do_hg, states, lay)

    def hg_prep_bwd(i, hf_, dkk_, dgl_, l0, l1):
        _, vjp = jax.vjp(_hg_prep, hf_, l0, l1)
        return vjp((_group_cumsum(dgl_, tile, reverse=True), dkk_))

    dproj, g_lb0, g_lb1 = rw(hg_prep_bwd, [(proj, 512, C_HF), dkk, dgl], [lb0, lb1], [(512, MXU_DTYPE)], [(1, HG_W), (1, HG_W)],
                             name="hg_prep_bwd", into=(dproj, C_HF))
    grads["hg_lb_logits"] = jnp.concatenate([g_lb0, g_lb1], axis=0)

    k_t = (k.astype(F32) * FOX_SCALE).astype(MXU_DTYPE).T.reshape(FOX_HEADS, FOX_DIM, n)
    k_t = jnp.concatenate([k_t, jnp.ones((FOX_HEADS, KT_ROWS - FOX_DIM, n), MXU_DTYPE)], axis=1).reshape(FOX_HEADS * KT_ROWS, n)
    late_grads = [grads.pop(n_) for n_ in LATE]
    parts = [_add_own_half(g_, s_, c, BF16, name=f"reduce_add2_{n_}")
             for n_, g_, s_ in zip(LATE, late_grads, _sibling_exchange(late_grads, "reduce_sibling_late"))]
    dq_t, dk_aug, dv, from_chips = _fox_bwd_t(q_aug, k_aug, v, dofox, k_t, o_t, dofox.T, lse, parts, lay)
    reduced = [_add_chips(p_, g_, mine, name=f"reduce_add4_{n_}") for n_, p_, g_ in zip(LATE, parts, from_chips)]
    dq_t = dq_t.reshape(FOX_HEADS, KT_ROWS, n)
    dq = dq_t[:, :FOX_DIM].reshape(FOX_W, n).T
    dk_aug = dk_aug.reshape(n, FOX_HEADS, AUG)
    dk = dk_aug[:, :, :FOX_DIM].reshape(n, FOX_W)
    dcum = jnp.pad(dq_t[:, FOX_DIM].T - dk_aug[:, :, FOX_DIM], ((0, 0), (0, LANES - FOX_HEADS)))
    dlogf = _cumsum_rows(dcum, lay, reverse=True, name="fox_cum_bwd")

    def fox_prep_bwd(i, a, b_, f_, dq_, dk_, dl_, gq_, gk_, bf_, m_, mt_):
        valid = lay.valid(i, tile)
        _, vjp = jax.vjp(lambda a_, b__, f__, gq__, gk__, bf__: _fox_prep(a_, b__, f__, gq__, gk__, bf__, m_, mt_, valid),
                         a, b_, f_, gq_, gk_, bf_)
        da, db, df, dgq, dgk, dbf = vjp((dq_, dk_, dl_))
        return jnp.concatenate([da, db], axis=1), df, dgq, dgk, dbf

    dproj, dff, g_gq, g_gk, g_bf = rw(
        fox_prep_bwd, [(proj, 512, C_FQ), (proj, 512, C_FK), pff, dq, dk, dlogf], [gq, gk, bf, gm64, gm64_t],
        [(2 * FOX_W, MXU_DTYPE), (LANES, MXU_DTYPE)], [(1, FOX_W), (1, FOX_W), (1, LANES)], name="fox_prep_bwd",
        into=(dproj, C_FQ // 2))
    grads["q_norm_gain"] = g_gq.reshape(FOX_HEADS, FOX_DIM).sum(0, keepdims=True)
    grads["k_norm_gain"] = g_gk.reshape(FOX_HEADS, FOX_DIM).sum(0, keepdims=True)
    grads["fox_b_f"] = g_bf[:, :FOX_HEADS]

    (dproj,) = rw(lambda i, a, b_, c_: jnp.concatenate([a, b_, c_], axis=1), [dv, dhq, dhi], [], [(3 * 512, MXU_DTYPE)], [],
                  name="dproj_cast", into=(dproj, C_FV // 3))
    dxn = _matmul(dproj, w_main, trans_b=True, name="proj_bwd_x")
    dxn_ff = _matmul(dff, w_ff, trans_b=True, name="proj_ff_bwd_x")
    g_main = _matmul(xn, dproj, trans_a=True, name="proj_bwd_w")
    g_ff = _matmul(xn, dff, trans_a=True, name="proj_ff_bwd_w")[:, :FOX_HEADS]
    p = jnp.split(g_main, list(np.cumsum([1024, 1024] + [512] * 6)), axis=1)
    grads["w_in"] = _by_chip(jnp.concatenate([p[2], p[3], p[4], g_ff, p[5], p[7], p[6], p[8], p[0], p[1]], axis=1))

    per = lay.lp // tile

    def norm1_bwd(i, h, d1, d2, dh1_, g):
        _, vjp = jax.vjp(_rms, h, g)
        dh, dg = vjp(d1 + d2)
        dh = dh + dh1_
        dmeta = jnp.where(lax.rem(i, per) == 0, dh[LEAD:LEAD + N_META, :], 0.0)
        return dh, dg, dmeta

    dh0, grads["norm1_gain"], grads["meta_tokens"] = rw(norm1_bwd, [h0, dxn, dxn_ff, dh1], [g1], [(D_MODEL, F32)],
                                                       [(1, D_MODEL), (N_META, D_MODEL)], name="norm1_bwd")
    grad_x = dh0.reshape(lay.batch, lay.lp, D_MODEL)[:, ROW0:ROW0 + lay.seq]
    return loss, grad_x, grads, reduced


MESH = pl.DeviceIdType.MESH
HBM_SPEC = pl.BlockSpec(memory_space=pltpu.HBM)
WEIGHT_NAMES = ["meta_tokens", "norm1_gain", "w_in", "fox_b_f", "q_norm_gain", "k_norm_gain", "hg_lb_logits", "hg_out_gain",
                "w_branch_a", "w_branch_b", "w_out", "norm2_gain", "w_up", "conv_w", "conv_b", "w_down"]
BIG = ("w_in", "w_branch_a", "w_branch_b", "w_out", "w_up", "w_down")
BIG_COL_SHARDED = ("w_in", "w_branch_a", "w_branch_b", "w_up")
LATE = BIG[1:]
SMALL = tuple(n for n in WEIGHT_NAMES if n not in BIG)
SMALL_SHARDED = ("meta_tokens", "conv_w")
SMALL_ROWS = 144
GATHER_SMALL_ROWS = 80


def _position():
    return lax.axis_index("x"), lax.axis_index("y"), lax.axis_index("c")


def _other_chips(x, y):
    return [(1 - x, y), (x, 1 - y), (1 - x, 1 - y)]


def _scalar(v):
    return jnp.reshape(v, (1,)).astype(jnp.int32)


def _row_tile(rows, cols):
    width = -(-cols // LANES) * LANES * 4
    best = 8
    for d in range(8, rows + 1, 8):
        if rows % d == 0 and d * width <= (1 << 20):
            best = d
    return best


class _Gather:
    def __init__(self, xs, outs, send_sems, recv_sems):
        self.xs, self.outs, self.send_sems, self.recv_sems = xs, outs, send_sems, recv_sems
        self.na = len(xs)
        self.x, self.y, self.c = _position()
        self.me, self.sibling = (self.x, self.y, self.c), (self.x, self.y, 1 - self.c)
        self.chips = _other_chips(self.x, self.y)

    def _copy(self, a, k, block, to, own=False):
        dst = self.outs[a].at[4 * block[0] + 2 * block[1] + block[2]]
        src = dst
        if own:
            half = self.xs[a].shape[0] // 2
            src = self.xs[a].at[pl.ds(pl.multiple_of(self.c * half, 8), half), :]
        return pltpu.make_async_remote_copy(src_ref=src, dst_ref=dst, send_sem=self.send_sems.at[a, k],
                                            recv_sem=self.recv_sems.at[a, k], device_id=to, device_id_type=MESH)

    def _firsts(self):
        return [self._copy(a, j, self.me, (*chip, self.c), own=True) for a in range(self.na) for j, chip in enumerate(self.chips)]

    def _relays(self):
        return [self._copy(a, 3 + j, (*chip, self.c), self.sibling) for j, chip in enumerate(self.chips) for a in range(self.na)]

    def start(self):
        for cp in self._firsts():
            cp.start()

    def relay(self):
        for j, chip in enumerate(self.chips):
            for a in range(self.na):
                self._copy(a, j, (*chip, self.c), self.me).wait_recv()
                self._copy(a, 3 + j, (*chip, self.c), self.sibling).start()

    def finish(self):
        for a in range(self.na):
            for j, chip in enumerate(self.chips):
                self._copy(a, 3 + j, (*chip, 1 - self.c), self.me).wait_recv()
        for cp in self._firsts() + self._relays():
            cp.wait_send()

    @staticmethod
    def out_shapes(shards):
        return [jax.ShapeDtypeStruct((8, s.shape[0] // 2, s.shape[1]), s.dtype) for s in shards]

    @staticmethod
    def semaphores(na):
        return [pltpu.SemaphoreType.DMA((na, 6)), pltpu.SemaphoreType.DMA((na, 6))]


def _gather_shards(shards):
    na = len(shards)

    def body(*refs):
        g = _Gather(refs[:na], refs[na:2 * na], refs[2 * na], refs[2 * na + 1])
        g.start()
        g.relay()
        g.finish()

    return pl.pallas_call(
        body, name="gather_weights", out_shape=_Gather.out_shapes(shards),
        in_specs=[HBM_SPEC] * na, out_specs=[HBM_SPEC] * na, scratch_shapes=_Gather.semaphores(na),
    )(*shards)


def _sibling_exchange(gs, name):
    na = len(gs)
    halves = [g.shape[1] // 2 for g in gs]

    def body(*refs):
        srcs, gots, send_sems, recv_sems = refs[:na], refs[na:2 * na], refs[2 * na], refs[2 * na + 1]
        x, y, c = _position()
        copies = [pltpu.make_async_remote_copy(
            src_ref=srcs[a].at[:, pl.ds(pl.multiple_of((1 - c) * halves[a], 8), halves[a]), :], dst_ref=gots[a],
            send_sem=send_sems.at[a], recv_sem=recv_sems.at[a], device_id=(x, y, 1 - c), device_id_type=MESH) for a in range(na)]
        for cp in copies:
            cp.start()
        for cp in copies:
            cp.wait()

    return pl.pallas_call(
        body, name=name,
        out_shape=[jax.ShapeDtypeStruct((N_CHIPS, h, g.shape[2]), g.dtype) for h, g in zip(halves, gs)],
        in_specs=[HBM_SPEC] * na, out_specs=[HBM_SPEC] * na,
        scratch_shapes=[pltpu.SemaphoreType.DMA((na,)), pltpu.SemaphoreType.DMA((na,))],
    )(*gs)


class _ChipExchange:
    def __init__(self, srcs, gots, send_sems, recv_sems):
        self.srcs, self.gots, self.send_sems, self.recv_sems = srcs, gots, send_sems, recv_sems
        self.na = len(srcs)
        x, y, self.c = _position()
        self.mine = 2 * x + y
        self.chips = _other_chips(x, y)

    def _copy(self, a, j, arriving):
        cx, cy = self.chips[j]
        theirs = 2 * cx + cy
        src = self.srcs[a].at[self.mine if arriving else theirs]
        dst = self.gots[a].at[theirs if arriving else self.mine]
        return pltpu.make_async_remote_copy(src_ref=src, dst_ref=dst, send_sem=self.send_sems.at[a, j],
                                            recv_sem=self.recv_sems.at[a, j], device_id=(cx, cy, self.c), device_id_type=MESH)

    def start(self):
        for a in range(self.na):
            for j in range(3):
                self._copy(a, j, False).start()

    def finish(self):
        for a in range(self.na):
            for j in range(3):
                self._copy(a, j, True).wait_recv()
        for a in range(self.na):
            for j in range(3):
                self._copy(a, j, False).wait_send()

    @staticmethod
    def semaphores(na):
        return [pltpu.SemaphoreType.DMA((na, 3)), pltpu.SemaphoreType.DMA((na, 3))]


def _chip_exchange(parts):
    na = len(parts)

    def body(*refs):
        ex = _ChipExchange(refs[:na], refs[na:2 * na], refs[2 * na], refs[2 * na + 1])
        ex.start()
        ex.finish()

    return pl.pallas_call(
        body, name="reduce_chips", out_shape=[jax.ShapeDtypeStruct(p.shape, p.dtype) for p in parts],
        in_specs=[HBM_SPEC] * na, out_specs=[HBM_SPEC] * na, scratch_shapes=_ChipExchange.semaphores(na),
    )(*parts)


def _sibling_send(halves):
    na = len(halves)

    def body(*refs):
        srcs, gots, send_sems, recv_sems = refs[:na], refs[na:2 * na], refs[2 * na], refs[2 * na + 1]
        x, y, c = _position()
        copies = [pltpu.make_async_remote_copy(src_ref=srcs[a], dst_ref=gots[a], send_sem=send_sems.at[a], recv_sem=recv_sems.at[a],
                                               device_id=(x, y, 1 - c), device_id_type=MESH) for a in range(na)]
        for cp in copies:
            cp.start()
        for cp in copies:
            cp.wait()

    return pl.pallas_call(
        body, name="reduce_gather", out_shape=[jax.ShapeDtypeStruct(h.shape, h.dtype) for h in halves],
        in_specs=[HBM_SPEC] * na, out_specs=[HBM_SPEC] * na,
        scratch_shapes=[pltpu.SemaphoreType.DMA((na,)), pltpu.SemaphoreType.DMA((na,))],
    )(*halves)


def _add_own_half(g, got, c, dtype, name):
    _, r, cols = g.shape
    r2 = r // 2
    tr = _row_tile(r2, cols)
    nrt = r2 // tr

    def body(c_ref, g_ref, got_ref, o_ref):
        o_ref[...] = (g_ref[...] + got_ref[...]).astype(o_ref.dtype)

    blk = (1, tr, cols)
    return pl.pallas_call(
        body, name=name,
        grid_spec=pltpu.PrefetchScalarGridSpec(
            num_scalar_prefetch=1, grid=(N_CHIPS, nrt),
            in_specs=[pl.BlockSpec(blk, lambda j, i, c_: (j, c_[0] * nrt + i, 0)), pl.BlockSpec(blk, lambda j, i, c_: (j, i, 0))],
            out_specs=pl.BlockSpec(blk, lambda j, i, c_: (j, i, 0))),
        out_shape=jax.ShapeDtypeStruct((N_CHIPS, r2, cols), dtype), compiler_params=_params(2),
    )(c, g, got)


def _add_chips(part, got, mine, name):
    _, r2, cols = part.shape
    tr = _row_tile(r2, cols)

    def body(m_ref, p_ref, g0, g1, g2, g3, o_ref):
        t = [jnp.where(m_ref[0] == k, p_ref[0], g[0]).astype(F32) for k, g in enumerate((g0, g1, g2, g3))]
        o_ref[...] = ((t[0] + t[1]) + t[2]) + t[3]

    blk = (1, tr, cols)
    others = [pl.BlockSpec(blk, functools.partial(lambda i, m, k: (jnp.where(m[0] == k, (k + 1) % N_CHIPS, k), i, 0), k=k))
              for k in range(N_CHIPS)]
    return pl.pallas_call(
        body, name=name,
        grid_spec=pltpu.PrefetchScalarGridSpec(
            num_scalar_prefetch=1, grid=(r2 // tr,),
            in_specs=[pl.BlockSpec(blk, lambda i, m: (m[0], i, 0))] + others,
            out_specs=pl.BlockSpec((tr, cols), lambda i, m: (i, 0))),
        out_shape=jax.ShapeDtypeStruct((r2, cols), F32), compiler_params=_params(1),
    )(mine, part, got, got, got, got)


def _adamw(w, own, other, m, v, c, name):
    r, cols = w.shape
    r2 = r // 2
    tr = _row_tile(r2, cols)
    nrt = r2 // tr
    c1 = 1.0 / (1.0 - ADAM_B1 ** ADAM_STEP)
    c2 = 1.0 / (1.0 - ADAM_B2 ** ADAM_STEP)

    def body(c_ref, w_ref, own_ref, other_ref, m_ref, v_ref, g_out, d_out, m_out, v_out):
        g_ = jnp.where(pl.program_id(0) == c_ref[0], own_ref[...], other_ref[...])
        m_new = ADAM_B1 * m_ref[...] + (1.0 - ADAM_B1) * g_
        v_new = ADAM_B2 * v_ref[...] + (1.0 - ADAM_B2) * (g_ * g_)
        g_out[...] = g_
        d_out[...] = -ADAM_LR * ((m_new * c1) / (jnp.sqrt(v_new * c2) + ADAM_EPS) + ADAM_WD * w_ref[...])
        m_out[...] = m_new
        v_out[...] = v_new

    full = pl.BlockSpec((tr, cols), lambda h, i, c_: (h * nrt + i, 0))
    half = pl.BlockSpec((tr, cols), lambda h, i, c_: (i, 0))
    out = jax.ShapeDtypeStruct((r, cols), F32)
    return pl.pallas_call(
        body, name=name,
        grid_spec=pltpu.PrefetchScalarGridSpec(num_scalar_prefetch=1, grid=(2, nrt), in_specs=[full, half, half, full, full],
                                               out_specs=[full] * 4),
        out_shape=[out] * 4, compiler_params=_params(2),
    )(c, w, own, other, m, v)


def _to_rows(flat, rows):
    return jnp.pad(flat, (0, rows * LANES - flat.shape[0])).reshape(rows, LANES)


def _pack_small(tree):
    return _to_rows(jnp.concatenate([tree[n].astype(F32).reshape(-1) for n in SMALL]), SMALL_ROWS)


def _unpack_small(packed, shapes):
    flat, out, at = packed.reshape(-1), {}, 0
    for n in SMALL:
        size = int(np.prod(shapes[n]))
        out[n] = flat[at:at + size].reshape(shapes[n])
        at += size
    return out


def _pack_small_by_chip(grads):
    pieces = []
    for n in SMALL:
        g = grads[n].astype(F32)
        if n in SMALL_SHARDED:
            pieces.append(_by_chip(g).reshape(N_CHIPS, -1))
        else:
            pieces.append(jnp.broadcast_to(g.reshape(1, -1), (N_CHIPS, g.size)))
    flat = jnp.concatenate(pieces, axis=1)
    return jnp.pad(flat, ((0, 0), (0, SMALL_ROWS * LANES - flat.shape[1]))).reshape(N_CHIPS, SMALL_ROWS, LANES)


def _bf16_shard(local, n):
    return local[n].reshape(local[n].shape[-2:]).astype(BF16)


def _all_chips(shard, slabs):
    x, y, _ = _position()
    is_mine = (lax.broadcasted_iota(jnp.int32, (N_CHIPS, 1, 1), 0) == 2 * x + y)
    return jnp.where(is_mine, shard[None], slabs.reshape((N_CHIPS,) + shard.shape))


def _assemble(names, shards, slabs):
    full = [_all_chips(s, g) for s, g in zip(shards, slabs)]
    return [_from_chips(f) if n in BIG_COL_SHARDED else f.reshape(N_CHIPS * f.shape[1], f.shape[2]) for n, f in zip(names, full)]


def _gather_first(local):
    shards = [_bf16_shard(local, "w_in"),
              _to_rows(jnp.concatenate([local[n].astype(F32).reshape(-1) for n in SMALL_SHARDED]), GATHER_SMALL_ROWS)]
    slabs = _gather_shards(shards)
    out = {"w_in": _assemble(["w_in"], shards[:1], slabs[:1])[0]}
    flat, at = _all_chips(shards[1], slabs[1]).reshape(N_CHIPS, -1), 0
    for n in SMALL_SHARDED:
        shape = local[n].shape[-2:]
        size = int(np.prod(shape))
        out[n] = _from_chips(flat[:, at:at + size].reshape((N_CHIPS,) + shape))
        at += size
    return out


def kernel(x, meta_tokens, norm1_gain, w_in, fox_b_f, q_norm_gain, k_norm_gain, hg_lb_logits, hg_out_gain, w_branch_a, w_branch_b, w_out, norm2_gain, w_up, conv_w, conv_b, w_down, loss_target, m_meta_tokens, m_norm1_gain, m_w_in, m_fox_b_f, m_q_norm_gain, m_k_norm_gain, m_hg_lb_logits, m_hg_out_gain, m_w_branch_a, m_w_branch_b, m_w_out, m_norm2_gain, m_w_up, m_conv_w, m_conv_b, m_w_down, v_meta_tokens, v_norm1_gain, v_w_in, v_fox_b_f, v_q_norm_gain, v_k_norm_gain, v_hg_lb_logits, v_hg_out_gain, v_w_branch_a, v_w_branch_b, v_w_out, v_norm2_gain, v_w_up, v_conv_w, v_conv_b, v_w_down):
    w_loc = dict(zip(WEIGHT_NAMES, (meta_tokens, norm1_gain, w_in, fox_b_f, q_norm_gain, k_norm_gain, hg_lb_logits, hg_out_gain,
                                    w_branch_a, w_branch_b, w_out, norm2_gain, w_up, conv_w, conv_b, w_down)))
    m_loc = dict(zip(WEIGHT_NAMES, (m_meta_tokens, m_norm1_gain, m_w_in, m_fox_b_f, m_q_norm_gain, m_k_norm_gain, m_hg_lb_logits,
                                    m_hg_out_gain, m_w_branch_a, m_w_branch_b, m_w_out, m_norm2_gain, m_w_up, m_conv_w, m_conv_b,
                                    m_w_down)))
    v_loc = dict(zip(WEIGHT_NAMES, (v_meta_tokens, v_norm1_gain, v_w_in, v_fox_b_f, v_q_norm_gain, v_k_norm_gain, v_hg_lb_logits,
                                    v_hg_out_gain, v_w_branch_a, v_w_branch_b, v_w_out, v_norm2_gain, v_w_up, v_conv_w, v_conv_b,
                                    v_w_down)))
    local_shapes = {n: tuple(w_loc[n].shape) for n in WEIGHT_NAMES}
    px, py, pc = _position()
    c, mine = _scalar(pc), _scalar(2 * px + py)

    weights = {n: w_loc[n].reshape(w_loc[n].shape[-2:]) for n in SMALL if n not in SMALL_SHARDED}
    weights.update(_gather_first(w_loc))

    lay = _Layout(x.shape[0], x.shape[1])
    loss, grad_x, grads, reduced_late = _local_step(x, loss_target, weights, [_bf16_shard(w_loc, n) for n in LATE], c, mine, lay)
    loss = lax.psum(loss, ("x", "y", "c"))

    names = ["w_in", "small"]
    by_chip = [grads["w_in"], _pack_small_by_chip(grads)]
    from_sibling = _sibling_exchange(by_chip, "reduce_sibling")
    parts = [_add_own_half(g, s, c, F32 if n == "small" else BF16, name=f"reduce_add2_{n}")
             for n, g, s in zip(names, by_chip, from_sibling)]
    from_chips = _chip_exchange(parts)
    own = [_add_chips(p, g, mine, name=f"reduce_add4_{n}") for n, p, g in zip(names, parts, from_chips)]
    names = list(BIG) + ["small"]
    own = [own[0]] + reduced_late + [own[1]]
    other = _sibling_send(own)

    two_d = lambda t: [t[n].reshape(t[n].shape[-2:]) for n in BIG] + [_pack_small(t)]
    results = [_adamw(w_, o_, t_, m_, v_, c, name=f"adamw_{n}")
               for n, w_, o_, t_, m_, v_ in zip(names, two_d(w_loc), own, other, two_d(m_loc), two_d(v_loc))]
    outs = []
    for kind in range(4):
        tree = {n: results[i][kind].reshape(local_shapes[n]) for i, n in enumerate(BIG)}
        tree.update(_unpack_small(results[-1][kind], local_shapes))
        outs += [tree[n] for n in WEIGHT_NAMES]
    return (loss, grad_x, *outs)
```

```python
import functools

import jax
import jax.numpy as jnp
import numpy as np
from jax import lax
from jax.experimental import pallas as pl
from jax.experimental.pallas import tpu as pltpu

F32 = jnp.float32
BF16 = jnp.bfloat16
MXU_DTYPE = BF16
HIGHEST = lax.Precision.HIGHEST

D_MODEL = 1024
N_META = 16
LEAD = 48
ROW0 = LEAD + N_META
FOX_HEADS, FOX_DIM, FOX_W = 8, 64, 512
HG_HEADS, HG_DIM, HG_W = 4, 128, 512
D_FF = 2816
FF2 = 2 * D_FF
EPS = 1e-6
SUB = 16
LANES = 128
N_CHIPS = 4
NEG = -1e30

ADAM_LR, ADAM_B1, ADAM_B2, ADAM_EPS, ADAM_WD, ADAM_STEP = 0.001, 0.9, 0.999, 1e-08, 0.01, 10

VMEM_LIMIT = 56 * 1024 * 1024

C_GA, C_GB = 0, 1
C_FQ, C_FK, C_FV, C_HQ, C_HI, C_HF, C_HG = 4, 5, 6, 7, 8, 9, 10
MAIN_COLS = 11 * 512


def _params(n_axes=1):
    return pltpu.CompilerParams(dimension_semantics=("arbitrary",) * n_axes, vmem_limit_bytes=VMEM_LIMIT)


def _pick(n, cands):
    for c in cands:
        if n % c == 0:
            return c
    raise ValueError(f"no tile for {n} among {cands}")


def _rowwise(fn, rows, consts, outs, reds, *, n_rows, tile, name, into=None):
    assert n_rows % tile == 0
    rows = [r if isinstance(r, tuple) else (r, r.shape[1], 0) for r in rows]
    nr, nc, no = len(rows), len(consts), len(outs)
    aliased = into is not None and not isinstance(into[0], int)
    n_in = nr + nc + (1 if aliased else 0)

    def body(*refs):
        i = pl.program_id(0)
        ins = [r[...] for r in refs[:nr + nc]]
        res = fn(i, *ins)
        res = res if isinstance(res, (tuple, list)) else (res,)
        for ref, v in zip(refs[n_in:n_in + no], res[:no]):
            ref[...] = v.astype(ref.dtype)
        red_refs = refs[n_in + no:]
        if red_refs:
            @pl.when(i == 0)
            def _():
                for ref in red_refs:
                    ref[...] = jnp.zeros_like(ref)
            for ref, v in zip(red_refs, res[no:]):
                ref[...] += v.astype(F32)

    in_specs = [pl.BlockSpec((tile, w), functools.partial(lambda i, j: (i, j), j=j)) for (_, w, j) in rows]
    in_specs += [pl.BlockSpec(c.shape, functools.partial(lambda i, nd: (0,) * nd, nd=c.ndim)) for c in consts]
    out_specs = [pl.BlockSpec((tile, w), lambda i: (i, 0)) for (w, _) in outs]
    out_specs += [pl.BlockSpec(s, functools.partial(lambda i, nd: (0,) * nd, nd=len(s))) for s in reds]
    out_shape = [jax.ShapeDtypeStruct((n_rows, w), dt) for (w, dt) in outs]
    out_shape += [jax.ShapeDtypeStruct(s, F32) for s in reds]
    args = [r[0] for r in rows] + list(consts)
    aliases = {}
    if into is not None:
        out_specs[0] = pl.BlockSpec((tile, outs[0][0]), functools.partial(lambda i, j: (i, j), j=into[1]))
        if aliased:
            in_specs.append(pl.BlockSpec(memory_space=pltpu.HBM))
            args.append(into[0])
            aliases = {n_in - 1: 0}
            out_shape[0] = jax.ShapeDtypeStruct(into[0].shape, into[0].dtype)
        else:
            out_shape[0] = jax.ShapeDtypeStruct((n_rows, into[0]), outs[0][1])
    return pl.pallas_call(
        body, name=name, grid=(n_rows // tile,), in_specs=in_specs, out_specs=out_specs, out_shape=out_shape,
        input_output_aliases=aliases, compiler_params=_params(1),
    )(*args)


def _matmul(a, b, *, trans_a=False, trans_b=False, out_dtype=F32, by_chip=False, name):
    if trans_a:
        k, m = a.shape
    else:
        m, k = a.shape
    n = b.shape[0] if trans_b else b.shape[1]
    assert (b.shape[1] if trans_b else b.shape[0]) == k
    if trans_a:
        tm = _pick(m, (1408, 1024, 512, 256, 128))
        tk = _pick(k, (2176, 1088, 1024, 768, 512, 256))
    else:
        tm = _pick(m, (1088, 512, 256, 128))
        tk = k if k <= 1024 else _pick(k, (2176, 1408, 1024, 512))
    nk = k // tk
    wide = (2816,) if nk == 1 and not trans_a else ()
    tn = n // N_CHIPS if by_chip else _pick(n, wide + (1408, 1024, 512, 256, 128))
    dims = (((0 if trans_a else 1,), (1 if trans_b else 0,)), ((), ()))

    def body(a_ref, b_ref, o_ref, acc_ref):
        out = o_ref.at[0] if by_chip else o_ref
        part = lax.dot_general(a_ref[...], b_ref[...], dims, preferred_element_type=F32)
        if nk == 1:
            out[...] = part.astype(out.dtype)
        else:
            kk = pl.program_id(2)

            @pl.when(kk == 0)
            def _():
                acc_ref[...] = part

            @pl.when(kk > 0)
            def _():
                acc_ref[...] += part

            @pl.when(kk == nk - 1)
            def _():
                out[...] = acc_ref[...].astype(out.dtype)

    a_spec = pl.BlockSpec((tk, tm), lambda i, j, kk: (kk, i)) if trans_a else pl.BlockSpec((tm, tk), lambda i, j, kk: (i, kk))
    b_spec = pl.BlockSpec((tn, tk), lambda i, j, kk: (j, kk)) if trans_b else pl.BlockSpec((tk, tn), lambda i, j, kk: (kk, j))
    if by_chip:
        out_spec, out_shape = pl.BlockSpec((1, tm, tn), lambda i, j, kk: (j, i, 0)), (N_CHIPS, m, tn)
    else:
        out_spec, out_shape = pl.BlockSpec((tm, tn), lambda i, j, kk: (i, j)), (m, n)
    return pl.pallas_call(
        body, name=name, grid=(m // tm, n // tn, nk), in_specs=[a_spec, b_spec], out_specs=out_spec,
        out_shape=jax.ShapeDtypeStruct(out_shape, out_dtype),
        scratch_shapes=[pltpu.VMEM((tm, tn) if nk > 1 else (8, LANES), F32)],
        compiler_params=_params(3),
    )(a, b)


def _sigmoid(x):
    return 1.0 / (1.0 + jnp.exp(-x))


def _silu(x):
    return x * _sigmoid(x)


def _log_sigmoid(x):
    return jnp.minimum(x, 0.0) - jnp.log(1.0 + jnp.exp(-jnp.abs(x)))


def _rms(x, gain):
    return x * lax.rsqrt(jnp.mean(x * x, axis=-1, keepdims=True) + EPS) * gain


def _group_matrix(width, group):
    g = (np.arange(width)[:, None] // group == np.arange(LANES)[None, :]).astype(np.float32)
    return jnp.asarray(g, MXU_DTYPE), jnp.asarray(g.T.copy(), MXU_DTYPE)


def _split_dot(x, mat):
    dt = mat.dtype
    hi = x.astype(dt)
    r1 = x - hi.astype(F32)
    mid = r1.astype(dt)
    lo = (r1 - mid.astype(F32)).astype(dt)
    dot = lambda a: jnp.dot(a, mat, preferred_element_type=F32)
    return dot(hi) + dot(mid) + dot(lo)


@jax.custom_vjp
def _group_sum(x, gmat, gmat_t):
    return _split_dot(x, gmat)


@jax.custom_vjp
def _group_spread(s, gmat, gmat_t):
    return _split_dot(s, gmat_t)


_group_sum.defvjp(lambda x, g, gt: (_split_dot(x, g), (g, gt)),
                  lambda res, ct: (_group_spread(ct, *res), jnp.zeros_like(res[0]), jnp.zeros_like(res[1])))
_group_spread.defvjp(lambda s, g, gt: (_split_dot(s, gt), (g, gt)),
                     lambda res, ct: (_group_sum(ct, *res), jnp.zeros_like(res[0]), jnp.zeros_like(res[1])))


def _group_rms(x, gain, gmat, gmat_t, group):
    rstd = lax.rsqrt(_group_sum(x * x, gmat, gmat_t) * (1.0 / group) + EPS)
    return x * _group_spread(rstd, gmat, gmat_t) * gain


def _head_rms(x, gain):
    outs = []
    for h in range(x.shape[1] // LANES):
        xs = x[:, h * LANES:(h + 1) * LANES]
        outs.append(xs * lax.rsqrt(jnp.mean(xs * xs, axis=-1, keepdims=True) + EPS) * gain)
    return jnp.concatenate(outs, axis=1)


class _Layout:
    def __init__(self, batch, seq):
        self.batch, self.seq = batch, seq
        self.l_real = N_META + seq
        self.lp = -(-(LEAD + self.l_real) // 256) * 256
        self.n = batch * self.lp
        self.tile = _pick(self.lp, (512, 256))

    def valid(self, i, tile):
        per = self.lp // tile
        r = lax.rem(i, per) * tile + lax.broadcasted_iota(jnp.int32, (tile, 1), 0)
        return (r >= LEAD) & (r < LEAD + self.l_real)


def _cumsum_rows(x, lay, *, reverse, name):
    t = lay.tile
    nt = lay.lp // t
    c = x.shape[1]

    def body(x_ref, o_ref, carry):
        j = pl.program_id(1)

        @pl.when(j == 0)
        def _():
            carry[...] = jnp.zeros_like(carry)

        r = lax.broadcasted_iota(jnp.int32, (t, t), 0)
        q = lax.broadcasted_iota(jnp.int32, (t, t), 1)
        tri = jnp.where((q >= r) if reverse else (q <= r), 1.0, 0.0).astype(F32)
        xs = x_ref[...]
        out = jnp.dot(tri, xs, precision=HIGHEST, preferred_element_type=F32) + carry[0:1, :]
        o_ref[...] = out
        carry[...] = jnp.broadcast_to(carry[0:1, :] + jnp.sum(xs, axis=0, keepdims=True), carry.shape)

    def idx(b, j):
        return (b * nt + (nt - 1 - j if reverse else j), 0)

    return pl.pallas_call(
        body, name=name, grid=(lay.batch, nt),
        in_specs=[pl.BlockSpec((t, c), idx)], out_specs=pl.BlockSpec((t, c), idx),
        out_shape=jax.ShapeDtypeStruct(x.shape, F32),
        scratch_shapes=[pltpu.VMEM((8, c), F32)],
        compiler_params=_params(2),
    )(x)


def _group_cumsum(x, tile, *, reverse):
    r = lax.rem(lax.broadcasted_iota(jnp.int32, (tile, 1), 0), SUB)
    s = 1
    while s < SUB:
        if reverse:
            x = x + jnp.where(r < SUB - s, pltpu.roll(x, tile - s, 0), 0.0)
        else:
            x = x + jnp.where(r >= s, pltpu.roll(x, s, 0), 0.0)
        s *= 2
    return x


AUG = 128
FOX_BK = 256
FOX_BQ = 256
FOX_SCALE = FOX_DIM ** -0.5
KT_ROWS = FOX_DIM + 16


def _aug_matrices():
    e1 = np.zeros((FOX_W, FOX_HEADS * AUG), np.float32)
    e2 = np.zeros((LANES, FOX_HEADS * AUG), np.float32)
    ones = np.zeros((1, FOX_HEADS * AUG), np.float32)
    for h in range(FOX_HEADS):
        for d in range(FOX_DIM):
            e1[h * FOX_DIM + d, h * AUG + d] = 1.0
        for j in range(3):
            e2[j * FOX_HEADS + h, h * AUG + FOX_DIM + j] = 1.0
            ones[0, h * AUG + FOX_DIM + j] = 1.0
    return jnp.asarray(e1, MXU_DTYPE), jnp.asarray(e2, MXU_DTYPE), jnp.asarray(ones)


def _fox_augment(q, k, cum, key_ok, e1, e2, ones):
    dt = q.dtype
    c = jnp.where(key_ok, -cum, NEG)
    hi = c.astype(dt)
    r1 = c - hi.astype(F32)
    mid = r1.astype(dt)
    lo = (r1 - mid.astype(F32)).astype(dt)
    lane = lax.broadcasted_iota(jnp.int32, c.shape, 1)
    shift = lambda a, by: pltpu.roll(a.astype(F32), by, 1)
    parts = jnp.where(lane < FOX_HEADS, hi.astype(F32),
                      jnp.where(lane < 2 * FOX_HEADS, shift(mid, FOX_HEADS),
                                jnp.where(lane < 3 * FOX_HEADS, shift(lo, 2 * FOX_HEADS), 0.0))).astype(dt)
    qs = (q.astype(F32) * FOX_SCALE).astype(dt)
    q_aug = jnp.dot(qs, e1, preferred_element_type=F32) + ones
    k_aug = jnp.dot(k, e1, preferred_element_type=F32) + jnp.dot(parts, e2, preferred_element_type=F32)
    return q_aug.astype(dt), k_aug.astype(dt)


def _fox_tile(k_blk, q_blk, k0, q0, masked):
    st = lax.dot_general(k_blk, q_blk, (((1,), (1,)), ((), ())), preferred_element_type=F32)
    if masked:
        keys = k0 + lax.broadcasted_iota(jnp.int32, st.shape, 0)
        qs = q0 + lax.broadcasted_iota(jnp.int32, st.shape, 1)
        st = jnp.where(keys <= qs, st, NEG)
    return st


def _fox_fwd_t(q_aug, k_aug, v_t, shards, lay):
    bk, bq = FOX_BK, FOX_BQ
    nq = lay.lp // bq
    pairs = FOX_HEADS // 2
    ng = len(shards)
    steps = lay.batch * pairs

    def body(*refs):
        q_ref, k_ref, vt_ref = refs[:3]
        ot_ref, lse_ref = refs[3 + ng:5 + ng]
        zeros_ref = refs[5 + 2 * ng]
        gather = _Gather(refs[3:3 + ng], refs[5 + ng:5 + 2 * ng], refs[6 + 2 * ng], refs[7 + 2 * ng])
        step = pl.program_id(0) * pairs + pl.program_id(1)
        pl.when(step == 0)(gather.start)
        pl.when(step == steps // 2)(gather.relay)
        heads = [(slice(hh * AUG, (hh + 1) * AUG), slice(hh * FOX_DIM, (hh + 1) * FOX_DIM)) for hh in range(2)]
        zeros_ref[...] = jnp.zeros_like(zeros_ref)

        def q_loop(qb, _):
            q0 = pl.multiple_of(qb * bq, bq)
            q_blks = [q_ref[pl.ds(q0, bq), lanes] for lanes, _ in heads]

            def scores(kb, h):
                k0 = pl.multiple_of(kb * bk, bk)
                return _fox_tile(k_ref[pl.ds(k0, bk), heads[h][0]], q_blks[h], k0, q0, False)

            def consume(kb, h, state, masked):
                m, l, acc, pend, st = state
                k0 = pl.multiple_of(kb * bk, bk)
                if masked:
                    keys = k0 + lax.broadcasted_iota(jnp.int32, st.shape, 0)
                    qs_ = q0 + lax.broadcasted_iota(jnp.int32, st.shape, 1)
                    st = jnp.where(keys <= qs_, st, NEG)
                m_new = jnp.maximum(m, jnp.max(st, axis=0, keepdims=True))
                alpha = jnp.exp(m - m_new)
                p = jnp.exp(st - m_new)
                l = alpha * l + jnp.sum(p, axis=0, keepdims=True)
                acc = alpha * (acc + pend)
                pend = jnp.dot(vt_ref[heads[h][1], pl.ds(k0, bk)], p.astype(vt_ref.dtype), preferred_element_type=F32)
                return m_new, l, acc, pend

            def k_step(kb, states):
                nxt = [scores(kb + 1, h) for h in range(2)]
                return tuple(consume(kb, h, states[h], False) + (nxt[h],) for h in range(2))

            states = tuple((jnp.full((1, bq), NEG, F32), jnp.zeros((1, bq), F32), zeros_ref[...], zeros_ref[...], scores(0, h))
                           for h in range(2))
            states = lax.fori_loop(0, qb, k_step, states)
            qs = q0 + lax.broadcasted_iota(jnp.int32, (1, bq), 1)
            ok = (qs >= LEAD) & (qs < LEAD + lay.l_real)
            for hh in range(2):
                m, l, acc, pend = consume(qb, hh, states[hh], True)
                ot_ref[heads[hh][1], pl.ds(q0, bq)] = jnp.where(ok, (acc + pend) / l, 0.0).astype(ot_ref.dtype)
                lse_ref[hh, :, pl.ds(q0, bq)] = m + jnp.log(l)
            return 0

        lax.fori_loop(0, nq, q_loop, 0)
        pl.when(step == steps - 1)(gather.finish)

    aug = pl.BlockSpec((lay.lp, 2 * AUG), lambda b, p: (b, p))
    tr = pl.BlockSpec((2 * FOX_DIM, lay.lp), lambda b, p: (p, b))
    outs = pl.pallas_call(
        body, name="fox_fwd", grid=(lay.batch, pairs),
        in_specs=[aug, aug, tr] + [HBM_SPEC] * ng,
        out_specs=[tr, pl.BlockSpec((2, 1, lay.lp), lambda b, p: (b * pairs + p, 0, 0))] + [HBM_SPEC] * ng,
        out_shape=[jax.ShapeDtypeStruct((FOX_W, lay.n), MXU_DTYPE),
                   jax.ShapeDtypeStruct((lay.batch * FOX_HEADS, 1, lay.lp), F32)] + _Gather.out_shapes(shards),
        scratch_shapes=[pltpu.VMEM((FOX_DIM, bq), F32)] + _Gather.semaphores(ng),
        compiler_params=_params(2),
    )(q_aug, k_aug, v_t, *shards)
    return outs[0], outs[1], outs[2:]


def _fox_bwd_t(q_aug, k_aug, v, do, k_t, o_t, do_t, lse, parts, lay):
    bk, bq = FOX_BK, FOX_BQ
    nq, nk = lay.lp // bq, lay.lp // bk
    pairs = FOX_HEADS // 2
    ne = len(parts)

    def body(*refs):
        q_ref, k_ref, v_ref, do_ref, kt_ref, ot_ref, dot_ref, lse_ref = refs[:8]
        dqt_ref, dk_ref, dv_ref = refs[8 + ne:11 + ne]
        delta = refs[11 + 2 * ne]
        exchange = _ChipExchange(refs[8:8 + ne], refs[11 + ne:11 + 2 * ne], refs[12 + 2 * ne], refs[13 + 2 * ne])
        step = pl.program_id(0) * pairs + pl.program_id(1)
        pl.when(step == 0)(exchange.start)
        dqt_ref[...] = jnp.zeros_like(dqt_ref)
        dk_ref[...] = jnp.zeros_like(dk_ref)
        dv_ref[...] = jnp.zeros_like(dv_ref)
        heads = [(hh, slice(hh * AUG, (hh + 1) * AUG), slice(hh * FOX_DIM, (hh + 1) * FOX_DIM),
                  slice(hh * KT_ROWS, (hh + 1) * KT_ROWS)) for hh in range(2)]

        def delta_loop(qb, _):
            q0 = pl.multiple_of(qb * bq, bq)
            for hh, _, cols, _ in heads:
                prod = ot_ref[cols, pl.ds(q0, bq)].astype(F32) * dot_ref[cols, pl.ds(q0, bq)].astype(F32)
                delta[hh, :, pl.ds(q0, bq)] = jnp.sum(prod, axis=0, keepdims=True)
            return 0

        lax.fori_loop(0, nq, delta_loop, 0)

        def k_loop(kb, _):
            k0 = pl.multiple_of(kb * bk, bk)

            def products(qb, h):
                q0 = pl.multiple_of(qb * bq, bq)
                _, lanes, cols, _ = heads[h]
                st = _fox_tile(k_ref[pl.ds(k0, bk), lanes], q_ref[pl.ds(q0, bq), lanes], k0, q0, False)
                dpt = lax.dot_general(v_ref[pl.ds(k0, bk), cols], do_ref[pl.ds(q0, bq), cols], (((1,), (1,)), ((), ())),
                                      preferred_element_type=F32)
                return st, dpt

            def consume(qb, h, st, dpt, masked):
                q0 = pl.multiple_of(qb * bq, bq)
                hh, lanes, cols, trows = heads[h]
                if masked:
                    keys = k0 + lax.broadcasted_iota(jnp.int32, st.shape, 0)
                    qs = q0 + lax.broadcasted_iota(jnp.int32, st.shape, 1)
                    st = jnp.where(keys <= qs, st, NEG)
                q_blk = q_ref[pl.ds(q0, bq), lanes]
                do_blk = do_ref[pl.ds(q0, bq), cols]
                pt = jnp.exp(st - lse_ref[hh, :, pl.ds(q0, bq)])
                dst = (pt * (dpt - delta[hh, :, pl.ds(q0, bq)])).astype(q_blk.dtype)
                dv_ref[pl.ds(k0, bk), cols] += jnp.dot(pt.astype(do_blk.dtype), do_blk, preferred_element_type=F32)
                dk_ref[pl.ds(k0, bk), lanes] += jnp.dot(dst, q_blk, preferred_element_type=F32)
                dqt_ref[trows, pl.ds(q0, bq)] += jnp.dot(kt_ref[trows, pl.ds(k0, bk)], dst, preferred_element_type=F32)

            after = lambda qb: jnp.minimum(qb + 1, nq - 1)
            cur = [products(kb, h) for h in range(2)]
            nxt = tuple(products(after(kb), h) for h in range(2))
            for h in range(2):
                consume(kb, h, *cur[h], True)

            def rest(qb, held):
                new = tuple(products(after(qb), h) for h in range(2))
                for h in range(2):
                    consume(qb, h, *held[h], False)
                return new

            lax.fori_loop(kb + 1, nq, rest, nxt)
            return 0

        lax.fori_loop(0, nk, k_loop, 0)
        pl.when(step == lay.batch * pairs - 1)(exchange.finish)

    aug = pl.BlockSpec((lay.lp, 2 * AUG), lambda b, p: (b, p))
    rows = pl.BlockSpec((lay.lp, 2 * FOX_DIM), lambda b, p: (b, p))
    tr = pl.BlockSpec((2 * FOX_DIM, lay.lp), lambda b, p: (p, b))
    tr_k = pl.BlockSpec((2 * KT_ROWS, lay.lp), lambda b, p: (p, b))
    outs = pl.pallas_call(
        body, name="fox_bwd", grid=(lay.batch, pairs),
        in_specs=[aug, aug, rows, rows, tr_k, tr, tr, pl.BlockSpec((2, 1, lay.lp), lambda b, p: (b * pairs + p, 0, 0))]
        + [HBM_SPEC] * ne,
        out_specs=[tr_k, aug, rows] + [HBM_SPEC] * ne,
        out_shape=[jax.ShapeDtypeStruct((FOX_HEADS * KT_ROWS, lay.n), F32), jax.ShapeDtypeStruct((lay.n, FOX_HEADS * AUG), F32),
                   jax.ShapeDtypeStruct((lay.n, FOX_W), F32)] + [jax.ShapeDtypeStruct(p.shape, p.dtype) for p in parts],
        scratch_shapes=[pltpu.VMEM((2, 1, lay.lp), F32)] + _ChipExchange.semaphores(ne),
        compiler_params=_params(2),
    )(q_aug, k_aug, v, do, k_t, o_t, do_t, lse, *parts)
    return outs[0], outs[1], outs[2], outs[3:]


def _hgrn_fwd(proj, kk, gl, lay):
    t = lay.tile
    nt = lay.lp // t
    nsc = t // SUB

    def body(q_ref, k_ref, g_ref, v_ref, o_ref, st_ref, state, sub_rows):
        @pl.when(pl.program_id(1) == 0)
        def _():
            state[...] = jnp.zeros_like(state)

        rowi = lax.broadcasted_iota(jnp.int32, (SUB, 1), 0)

        def sub(sc, _):
            r0 = pl.multiple_of(sc * SUB, SUB)
            sub_rows[0] = k_ref[pl.ds(r0, SUB), :]
            sub_rows[1] = g_ref[pl.ds(r0, SUB), :]
            sub_rows[2] = v_ref[pl.ds(r0, SUB), :]
            for h in range(HG_HEADS):
                lanes = slice(h * HG_DIM, (h + 1) * HG_DIM)
                q16 = q_ref[pl.ds(r0, SUB), lanes]
                k16 = sub_rows[0, :, lanes]
                g16 = sub_rows[1, :, lanes]
                v16 = sub_rows[2, :, lanes]
                g_end = sub_rows[1, SUB - 1:SUB, lanes]
                s_prev = state[h]
                st_ref[sc, h] = s_prev
                o = lax.dot_general((q16 * jnp.exp(g16)).astype(MXU_DTYPE), s_prev.astype(MXU_DTYPE),
                                    (((1,), (1,)), ((), ())), preferred_element_type=F32)
                for s in range(SUB):
                    ks = sub_rows[0, s:s + 1, lanes]
                    gs = sub_rows[1, s:s + 1, lanes]
                    vs = sub_rows[2, s:s + 1, lanes]
                    w = q16 * jnp.exp(jnp.minimum(g16 - gs, 0.0)) * ks
                    a = jnp.where(rowi >= s, jnp.sum(w, axis=1, keepdims=True), 0.0)
                    o = o + a * vs
                o_ref[pl.ds(r0, SUB), lanes] = o
                kt = k16 * jnp.exp(g_end - g16)
                upd = lax.dot_general(v16.astype(MXU_DTYPE), kt.astype(MXU_DTYPE), (((0,), (0,)), ((), ())),
                                      preferred_element_type=F32)
                state[h] = jnp.exp(g_end) * s_prev + upd
            return 0

        lax.fori_loop(0, nsc, sub, 0)

    rows = lambda col: pl.BlockSpec((t, HG_W), functools.partial(lambda b, i, col: (b * nt + i, col), col=col))
    return pl.pallas_call(
        body, name="hgrn_fwd", grid=(lay.batch, nt),
        in_specs=[rows(C_HQ), rows(0), rows(0), rows(C_HI)],
        out_specs=[rows(0), pl.BlockSpec((nsc, HG_HEADS, HG_DIM, HG_DIM), lambda b, i: (b * nt + i, 0, 0, 0))],
        out_shape=[jax.ShapeDtypeStruct((lay.n, HG_W), F32),
                   jax.ShapeDtypeStruct((lay.n // SUB, HG_HEADS, HG_DIM, HG_DIM), F32)],
        scratch_shapes=[pltpu.VMEM((HG_HEADS, HG_DIM, HG_DIM), F32), pltpu.VMEM((3, SUB, HG_W), F32)],
        compiler_params=_params(2),
    )(proj, kk, gl, proj)


def _hgrn_bwd(proj, kk, gl, do, states, lay):
    t = lay.tile
    nt = lay.lp // t
    nsc = t // SUB

    def body(q_ref, k_ref, g_ref, v_ref, do_ref, st_ref, dq_ref, dk_ref, dv_ref, dg_ref, dstate, sub_rows, row_acc):
        @pl.when(pl.program_id(1) == 0)
        def _():
            dstate[...] = jnp.zeros_like(dstate)

        rowi = lax.broadcasted_iota(jnp.int32, (SUB, 1), 0)

        def sub(it, _):
            sc = nsc - 1 - it
            r0 = pl.multiple_of(sc * SUB, SUB)
            sub_rows[0] = k_ref[pl.ds(r0, SUB), :]
            sub_rows[1] = g_ref[pl.ds(r0, SUB), :]
            sub_rows[2] = v_ref[pl.ds(r0, SUB), :]
            for h in range(HG_HEADS):
                lanes = slice(h * HG_DIM, (h + 1) * HG_DIM)
                q16 = q_ref[pl.ds(r0, SUB), lanes]
                k16 = sub_rows[0, :, lanes]
                g16 = sub_rows[1, :, lanes]
                v16 = sub_rows[2, :, lanes]
                do16 = do_ref[pl.ds(r0, SUB), lanes]
                g_end = sub_rows[1, SUB - 1:SUB, lanes]
                s_prev = st_ref[sc, h]
                ds_end = dstate[h]
                eg = jnp.exp(g16)
                ekt = jnp.exp(g_end - g16)
                e_end = jnp.exp(g_end)
                qt = q16 * eg
                kt = k16 * ekt
                ds_mx = ds_end.astype(MXU_DTYPE)
                dv = lax.dot_general(kt.astype(MXU_DTYPE), ds_mx, (((1,), (1,)), ((), ())), preferred_element_type=F32)
                dkt = jnp.dot(v16.astype(MXU_DTYPE), ds_mx, preferred_element_type=F32)
                dk = dkt * ekt
                ktdkt = kt * dkt
                dg_end = jnp.sum(ktdkt, axis=0, keepdims=True) + jnp.sum(s_prev * ds_end, axis=0, keepdims=True) * e_end
                dg = jnp.where(rowi == SUB - 1, dg_end, 0.0) - ktdkt
                dqt = jnp.dot(do16.astype(MXU_DTYPE), s_prev.astype(MXU_DTYPE), preferred_element_type=F32)
                dq = dqt * eg
                dg = dg + qt * dqt
                dstate[h] = e_end * ds_end + lax.dot_general(do16.astype(MXU_DTYPE), qt.astype(MXU_DTYPE),
                                                             (((0,), (0,)), ((), ())), preferred_element_type=F32)
                for s in range(SUB):
                    ks = sub_rows[0, s:s + 1, lanes]
                    gs = sub_rows[1, s:s + 1, lanes]
                    vs = sub_rows[2, s:s + 1, lanes]
                    live = rowi >= s
                    e = jnp.where(live, jnp.exp(jnp.minimum(g16 - gs, 0.0)), 0.0)
                    qe = q16 * e
                    a = jnp.sum(qe * ks, axis=1, keepdims=True)
                    da = jnp.where(live, jnp.sum(do16 * vs, axis=1, keepdims=True), 0.0)
                    t1 = da * qe
                    dk_row = jnp.sum(t1, axis=0, keepdims=True)
                    dq = dq + da * (e * ks)
                    dg = dg + t1 * ks
                    row_acc[0, s:s + 1, :] = jnp.sum(a * do16, axis=0, keepdims=True)
                    row_acc[1, s:s + 1, :] = dk_row
                    row_acc[2, s:s + 1, :] = ks * dk_row
                dq_ref[pl.ds(r0, SUB), lanes] = dq
                dk_ref[pl.ds(r0, SUB), lanes] = dk + row_acc[1]
                dv_ref[pl.ds(r0, SUB), lanes] = dv + row_acc[0]
                dg_ref[pl.ds(r0, SUB), lanes] = dg - row_acc[2]
            return 0

        lax.fori_loop(0, nsc, sub, 0)

    def rows(col):
        return pl.BlockSpec((t, HG_W), functools.partial(lambda b, i, col: (b * nt + nt - 1 - i, col), col=col))

    out = jax.ShapeDtypeStruct((lay.n, HG_W), F32)
    return pl.pallas_call(
        body, name="hgrn_bwd", grid=(lay.batch, nt),
        in_specs=[rows(C_HQ), rows(0), rows(0), rows(C_HI), rows(0),
                  pl.BlockSpec((nsc, HG_HEADS, HG_DIM, HG_DIM), lambda b, i: (b * nt + nt - 1 - i, 0, 0, 0))],
        out_specs=[rows(0)] * 4, out_shape=[out] * 4,
        scratch_shapes=[pltpu.VMEM((HG_HEADS, HG_DIM, HG_DIM), F32), pltpu.VMEM((3, SUB, HG_W), F32),
                        pltpu.VMEM((3, SUB, HG_DIM), F32)],
        compiler_params=_params(2),
    )(proj, kk, gl, proj, do, states)


CONV_COLS = 1408


def _shift_down(x, halo, tile, by):
    out = pltpu.roll(x, by, 0)
    rowi = lax.broadcasted_iota(jnp.int32, (8, 1), 0)
    top = out[0:8]
    for r in range(by):
        top = jnp.where(rowi == r, halo[8 - by + r:8 - by + r + 1, :], top)
    return jnp.concatenate([top, out[8:]], axis=0)


def _shift_up(x, halo, tile, by):
    out = pltpu.roll(x, tile - by, 0)
    rowi = lax.broadcasted_iota(jnp.int32, (8, 1), 0)
    bottom = out[tile - 8:]
    for r in range(by):
        bottom = jnp.where(rowi == 8 - by + r, halo[r:r + 1, :], bottom)
    return jnp.concatenate([out[:tile - 8], bottom], axis=0)


def _conv_specs(tile):
    ncb = D_FF // CONV_COLS
    per8 = tile // 8

    def tile_spec(off):
        return pl.BlockSpec((tile, CONV_COLS), functools.partial(lambda i, j, off: (i, j + off), off=off))

    def prev_spec(off):
        return pl.BlockSpec((8, CONV_COLS), functools.partial(lambda i, j, off: (jnp.maximum(i * per8 - 1, 0), j + off), off=off))

    def w_spec(off):
        return pl.BlockSpec((3, CONV_COLS), functools.partial(lambda i, j, off: (0, j + off), off=off))

    def b_spec(off):
        return pl.BlockSpec((1, CONV_COLS), functools.partial(lambda i, j, off: (0, j + off), off=off))

    return ncb, tile_spec, prev_spec, w_spec, b_spec


def _conv3(x, halo, w, b, tile):
    return w[0:1, :] * _shift_down(x, halo, tile, 2) + w[1:2, :] * _shift_down(x, halo, tile, 1) + w[2:3, :] * x + b


def _conv_act_fwd(u, conv_w, conv_b, lay):
    tile = lay.tile
    ncb, tile_spec, prev_spec, w_spec, b_spec = _conv_specs(tile)

    def body(ug, uv, pg, pv, wg, wv, bg, bv, o_ref):
        cg = _conv3(ug[...], pg, wg, bg[...], tile)
        cv = _conv3(uv[...], pv, wv, bv[...], tile)
        o_ref[...] = (_silu(cg) * cv).astype(o_ref.dtype)

    return pl.pallas_call(
        body, name="conv_act_fwd", grid=(lay.n // tile, ncb),
        in_specs=[tile_spec(0), tile_spec(ncb), prev_spec(0), prev_spec(ncb), w_spec(0), w_spec(ncb), b_spec(0), b_spec(ncb)],
        out_specs=pl.BlockSpec((tile, CONV_COLS), lambda i, j: (i, j)),
        out_shape=jax.ShapeDtypeStruct((lay.n, D_FF), MXU_DTYPE),
        compiler_params=_params(2),
    )(u, u, u, u, conv_w, conv_w, conv_b, conv_b)


def _conv_act_bwd(u, dact, conv_w, conv_b, lay):
    tile = lay.tile
    ncb, tile_spec, prev_spec, w_spec, b_spec = _conv_specs(tile)

    def body(ug, uv, pg, pv, wg, wv, bg, bv, da_ref, dg_ref, dv_ref, gwg, gwv, gbg, gbv):
        @pl.when(pl.program_id(1) == 0)
        def _():
            for r in (gwg, gwv, gbg, gbv):
                r[...] = jnp.zeros_like(r)

        xg, xv = ug[...], uv[...]
        cg = _conv3(xg, pg, wg, bg[...], tile)
        cv = _conv3(xv, pv, wv, bv[...], tile)
        da = da_ref[...].astype(F32)
        sg = _sigmoid(cg)
        dcv = da * (cg * sg)
        dcg = da * cv * (sg * (1.0 + cg * (1.0 - sg)))
        dg_ref[...] = dcg
        dv_ref[...] = dcv
        for x, halo, dc, gw, gb in ((xg, pg, dcg, gwg, gbg), (xv, pv, dcv, gwv, gbv)):
            gw[0, 0:1, :] += jnp.sum(dc * _shift_down(x, halo, tile, 2), axis=0, keepdims=True)
            gw[0, 1:2, :] += jnp.sum(dc * _shift_down(x, halo, tile, 1), axis=0, keepdims=True)
            gw[0, 2:3, :] += jnp.sum(dc * x, axis=0, keepdims=True)
            gb[0] += jnp.sum(dc, axis=0, keepdims=True)

    swap = lambda spec: pl.BlockSpec(spec.block_shape, functools.partial(lambda j, i, f: f(i, j), f=spec.index_map))
    col = lambda j, i: (i, j)
    red_w = pl.BlockSpec((1, 3, CONV_COLS), lambda j, i: (j, 0, 0))
    red_b = pl.BlockSpec((1, 1, CONV_COLS), lambda j, i: (j, 0, 0))
    outs = pl.pallas_call(
        body, name="conv_act_bwd", grid=(ncb, lay.n // tile),
        in_specs=[swap(s) for s in (tile_spec(0), tile_spec(ncb), prev_spec(0), prev_spec(ncb), w_spec(0), w_spec(ncb),
                                    b_spec(0), b_spec(ncb))] + [pl.BlockSpec((tile, CONV_COLS), col)],
        out_specs=[pl.BlockSpec((tile, CONV_COLS), col), pl.BlockSpec((tile, CONV_COLS), col), red_w, red_w, red_b, red_b],
        out_shape=[jax.ShapeDtypeStruct((lay.n, D_FF), F32), jax.ShapeDtypeStruct((lay.n, D_FF), F32),
                   jax.ShapeDtypeStruct((ncb, 3, CONV_COLS), F32), jax.ShapeDtypeStruct((ncb, 3, CONV_COLS), F32),
                   jax.ShapeDtypeStruct((ncb, 1, CONV_COLS), F32), jax.ShapeDtypeStruct((ncb, 1, CONV_COLS), F32)],
        compiler_params=_params(2),
    )(u, u, u, u, conv_w, conv_w, conv_b, conv_b, dact)
    dcg, dcv, gwg, gwv, gbg, gbv = outs
    unblock = lambda g: jnp.transpose(g, (1, 0, 2)).reshape(g.shape[1], D_FF)
    g_w = jnp.concatenate([unblock(gwg), unblock(gwv)], axis=1)
    g_b = jnp.concatenate([unblock(gbg), unblock(gbv)], axis=1)
    return dcg, dcv, g_w, g_b


def _conv_input_bwd(dcg, dcv, conv_w, lay):
    tile = lay.tile
    ncb = D_FF // CONV_COLS
    nblk8 = lay.n // 8
    per8 = tile // 8
    nxt = lambda i: jnp.minimum((i + 1) * per8, nblk8 - 1)

    def half(dc, off, into, name):
        def body(*refs):
            d, halo, w, o = refs[0], refs[1], refs[2], refs[-1]
            x = d[...]
            du = w[2:3, :] * x + w[1:2, :] * _shift_up(x, halo, tile, 1) + w[0:1, :] * _shift_up(x, halo, tile, 2)
            o[...] = jnp.where(lay.valid(pl.program_id(0), tile), du, 0.0).astype(o.dtype)

        in_specs = [pl.BlockSpec((tile, CONV_COLS), lambda i, j: (i, j)),
                    pl.BlockSpec((8, CONV_COLS), lambda i, j: (nxt(i), j)),
                    pl.BlockSpec((3, CONV_COLS), lambda i, j: (0, j + off))]
        args = [dc, dc, conv_w]
        if into is not None:
            in_specs.append(pl.BlockSpec(memory_space=pltpu.HBM))
            args.append(into)
        return pl.pallas_call(
            body, name=name, grid=(lay.n // tile, ncb), in_specs=in_specs,
            out_specs=pl.BlockSpec((tile, CONV_COLS), lambda i, j: (i, j + off)),
            out_shape=jax.ShapeDtypeStruct((lay.n, FF2), MXU_DTYPE),
            input_output_aliases={} if into is None else {3: 0},
            compiler_params=_params(2),
        )(*args)

    return half(dcv, ncb, half(dcg, 0, None, "conv_input_bwd_gate"), "conv_input_bwd_value")


def _loss_head(h1, mlp, target, lay):
    t, sub = 256, ROW0
    per = lay.lp // t
    nsub = t // sub
    nreal = lay.seq // sub

    def body(h_ref, m_ref, *rest):
        t_refs, (loss_ref, dy_ref, dyb_ref) = rest[:nsub], rest[nsub:]
        b, j = pl.program_id(0), pl.program_id(1)

        @pl.when((b == 0) & (j == 0))
        def _():
            loss_ref[...] = jnp.zeros_like(loss_ref)

        rows_ = j * t + lax.broadcasted_iota(jnp.int32, (t, 1), 0)
        real = (rows_ >= ROW0) & (rows_ < ROW0 + lay.seq)
        tgt_ = jnp.concatenate([r[...] for r in t_refs], axis=0)
        err = jnp.where(real, h_ref[...] + m_ref[...] - tgt_, 0.0)
        dy = err * (1.0 / D_MODEL)
        dy_ref[...] = dy
        dyb_ref[...] = dy.astype(dyb_ref.dtype)
        loss_ref[...] += 0.5 * jnp.sum(err * dy)

    rows = pl.BlockSpec((t, D_MODEL), lambda b, j: (b * per + j, 0))
    tgt = [pl.BlockSpec((sub, D_MODEL), functools.partial(
        lambda b, j, r: (b * nreal + jnp.clip(j * nsub + r - 1, 0, nreal - 1), 0), r=r)) for r in range(nsub)]
    return pl.pallas_call(
        body, name="loss_head", grid=(lay.batch, per),
        in_specs=[rows, rows] + tgt,
        out_specs=[pl.BlockSpec((8, LANES), lambda b, j: (0, 0)), rows, rows],
        out_shape=[jax.ShapeDtypeStruct((8, LANES), F32), jax.ShapeDtypeStruct((lay.n, D_MODEL), F32),
                   jax.ShapeDtypeStruct((lay.n, D_MODEL), MXU_DTYPE)],
        compiler_params=_params(2),
    )(h1, mlp, *([target] * nsub))


def _fox_prep(fq, fk, ff, gq, gk, bf, gmat, gmat_t, valid):
    q = _group_rms(fq, gq, gmat, gmat_t, FOX_DIM)
    k = _group_rms(fk, gk, gmat, gmat_t, FOX_DIM)
    logf = jnp.where(valid, _log_sigmoid(ff + bf), 0.0)
    return q, k, logf


def _hg_prep(hf, l0, l1):
    mx = jnp.maximum(l0, l1)
    e0, e1 = jnp.exp(l0 - mx), jnp.exp(l1 - mx)
    lb = e0 / (e0 + e1)
    lf = jnp.log(lb + (1.0 - lb) * _sigmoid(hf))
    kk = (1.0 - lb) * _sigmoid(-hf)
    return lf, kk


def _hg_post(o, hg, gain):
    return _head_rms(o, gain) * _silu(hg)


def _gate(ga, gb, ya, yb):
    return _sigmoid(ga) * ya + _sigmoid(gb) * yb


def _by_chip(g):
    return jnp.transpose(g.reshape(g.shape[0], N_CHIPS, g.shape[1] // N_CHIPS), (1, 0, 2))


def _from_chips(a):
    return jnp.transpose(a, (1, 0, 2)).reshape(a.shape[1], N_CHIPS * a.shape[2])


def _local_step(x, target, w, late_shards, c, mine, lay):
    n, tile = lay.n, lay.tile
    rw = functools.partial(_rowwise, n_rows=n, tile=tile)
    mx = lambda a: a.astype(MXU_DTYPE)

    w_in = w["w_in"]
    fq, fk, fv, ffw, hq, hf, hi, hg, ga, gb = jnp.split(w_in, list(np.cumsum([512, 512, 512, 8, 512, 512, 512, 512, 1024])), axis=1)
    w_main = mx(jnp.concatenate([ga, gb, fq, fk, fv, hq, hi, hf, hg], axis=1))
    w_ff = mx(jnp.pad(ffw, ((0, 0), (0, LANES - FOX_HEADS))))
    conv_w, conv_b = w["conv_w"].astype(F32), w["conv_b"].astype(F32)
    g1, g2 = w["norm1_gain"], w["norm2_gain"]
    gq, gk = jnp.tile(w["q_norm_gain"], (1, FOX_HEADS)), jnp.tile(w["k_norm_gain"], (1, FOX_HEADS))
    bf = jnp.pad(w["fox_b_f"], ((0, 0), (0, LANES - FOX_HEADS)))
    lb_logits, hg_gain = w["hg_lb_logits"], w["hg_out_gain"]
    gm64, gm64_t = _group_matrix(FOX_W, FOX_DIM)

    meta = jnp.broadcast_to(w["meta_tokens"].astype(F32)[None], (lay.batch, N_META, D_MODEL))
    h0 = jnp.concatenate([jnp.zeros((lay.batch, LEAD, D_MODEL), F32), meta, x,
                          jnp.zeros((lay.batch, lay.lp - LEAD - lay.l_real, D_MODEL), F32)], axis=1).reshape(n, D_MODEL)

    (xn,) = rw(lambda i, h, g: _rms(h, g), [h0], [g1], [(D_MODEL, MXU_DTYPE)], [], name="norm1")
    proj = _matmul(xn, w_main, name="proj_main")
    pff = _matmul(xn, w_ff, name="proj_ff")

    def fox_prep_fn(i, a, b_, v_, f_, gq_, gk_, bf_, m_, mt_):
        q_, k_, logf = _fox_prep(a, b_, f_, gq_, gk_, bf_, m_, mt_, lay.valid(i, tile))
        return q_, k_, v_, logf

    q, k, v, logf = rw(fox_prep_fn, [(proj, 512, C_FQ), (proj, 512, C_FK), (proj, 512, C_FV), pff], [gq, gk, bf, gm64, gm64_t],
                       [(512, MXU_DTYPE), (512, MXU_DTYPE), (512, MXU_DTYPE), (LANES, F32)], [], name="fox_prep")
    cum = _cumsum_rows(logf, lay, reverse=False, name="fox_cum")
    e1, e2, aug_ones = _aug_matrices()
    q_aug, k_aug = rw(lambda i, q_, k_, c_, e1_, e2_, on_: _fox_augment(q_, k_, c_, lay.valid(i, tile), e1_, e2_, on_),
                      [q, k, cum], [e1, e2, aug_ones], [(FOX_HEADS * AUG, MXU_DTYPE)] * 2, [], name="fox_aug")
    o_t, lse, late_slabs = _fox_fwd_t(q_aug, k_aug, v.T, late_shards, lay)
    w_a, w_b, w_out, w_up, w_down = [mx(a) for a in _assemble(LATE, late_shards, late_slabs)]

    def hg_prep_fn(i, hf_, l0, l1):
        lf, kk_ = _hg_prep(hf_, l0, l1)
        return kk_, _group_cumsum(lf, tile, reverse=False)

    lb0, lb1 = lb_logits[0:1], lb_logits[1:2]
    kk, gl = rw(hg_prep_fn, [(proj, 512, C_HF)], [lb0, lb1], [(512, F32), (512, F32)], [], name="hg_prep")
    o_hg, states = _hgrn_fwd(proj, kk, gl, lay)
    (oh,) = rw(lambda i, o, g_, gain: _hg_post(o, g_, gain), [o_hg, (proj, 512, C_HG)], [hg_gain], [(512, MXU_DTYPE)], [],
               name="hg_post")
    ya = _matmul(oh, w_a, out_dtype=MXU_DTYPE, name="branch_a")
    yb = _matmul(o_t, w_b, trans_a=True, out_dtype=MXU_DTYPE, name="branch_b")
    pga, pgb = (proj, 1024, C_GA), (proj, 1024, C_GB)
    gate_fn = lambda a, b_, c_, d_: _gate(a, b_, c_.astype(F32), d_.astype(F32))
    (merged,) = rw(lambda i, a, b_, c_, d_: gate_fn(a, b_, c_, d_), [pga, pgb, ya, yb], [], [(D_MODEL, MXU_DTYPE)], [], name="gate")
    mo = _matmul(merged, w_out, name="out_proj")
    h1, hn = rw(lambda i, h, m_, g: (h + m_, _rms(h + m_, g)), [h0, mo], [g2], [(D_MODEL, F32), (D_MODEL, MXU_DTYPE)], [],
                name="norm2")
    u = _matmul(hn, w_up, name="up_proj")
    act = _conv_act_fwd(u, conv_w, conv_b, lay)
    mlp = _matmul(act, w_down, name="down_proj")
    loss_blk, dy, dyb = _loss_head(h1, mlp, target.reshape(lay.batch * lay.seq, D_MODEL), lay)
    loss = loss_blk[0, 0]

    grads = {}
    dact = _matmul(dyb, w_down, trans_b=True, out_dtype=MXU_DTYPE, name="down_bwd_x")
    grads["w_down"] = _matmul(act, dyb, trans_a=True, name="down_bwd_w").reshape(N_CHIPS, D_FF // N_CHIPS, D_MODEL)
    dcg, dcv, grads["conv_w"], grads["conv_b"] = _conv_act_bwd(u, dact, conv_w, conv_b, lay)
    du = _conv_input_bwd(dcg, dcv, conv_w, lay)
    dhn = _matmul(du, w_up, trans_b=True, name="up_bwd_x")
    grads["w_up"] = _matmul(hn, du, trans_a=True, by_chip=True, name="up_bwd_w")

    def norm2_bwd(i, h, d_, dy_, g):
        _, vjp = jax.vjp(_rms, h, g)
        dh, dg = vjp(d_)
        return dh + dy_, dh + dy_, dg

    dh1, dh1b, grads["norm2_gain"] = rw(norm2_bwd, [h1, dhn, dy], [g2], [(D_MODEL, F32), (D_MODEL, MXU_DTYPE)], [(1, D_MODEL)],
                                        name="norm2_bwd")
    dmerged = _matmul(dh1b, w_out, trans_b=True, out_dtype=MXU_DTYPE, name="out_bwd_x")
    grads["w_out"] = _matmul(merged, dh1b, trans_a=True, name="out_bwd_w").reshape(N_CHIPS, D_MODEL // N_CHIPS, D_MODEL)

    def gate_bwd(i, a, b_, c_, d_, dm):
        _, vjp = jax.vjp(gate_fn, a, b_, c_, d_)
        da, db, dc, dd = vjp(dm.astype(F32))
        return jnp.concatenate([da, db], axis=1), dc, dd

    dproj, dya, dyb_ = rw(gate_bwd, [pga, pgb, ya, yb, dmerged], [], [(2 * D_MODEL, MXU_DTYPE)] + [(D_MODEL, MXU_DTYPE)] * 2, [],
                          name="gate_bwd", into=(MAIN_COLS, 0))
    doh = _matmul(dya, w_a, trans_b=True, out_dtype=MXU_DTYPE, name="branch_a_bwd_x")
    grads["w_branch_a"] = _matmul(oh, dya, trans_a=True, by_chip=True, name="branch_a_bwd_w")
    dofox = _matmul(dyb_, w_b, trans_b=True, out_dtype=MXU_DTYPE, name="branch_b_bwd_x")
    grads["w_branch_b"] = _matmul(o_t, dyb_, by_chip=True, name="branch_b_bwd_w")

    def hg_post_bwd(i, o, g_, d_, gain):
        _, vjp = jax.vjp(_hg_post, o, g_, gain)
        do_, dg_, dgain = vjp(d_.astype(F32))
        return dg_, do_, dgain

    dproj, do_hg, grads["hg_out_gain"] = rw(hg_post_bwd, [o_hg, (proj, 512, C_HG), doh], [hg_gain],
                                            [(512, MXU_DTYPE), (512, F32)], [(1, HG_DIM)], name="hg_post_bwd", into=(dproj, C_HG))
    dhq, dkk, dhi, dgl = _hgrn_bwd(proj, kk, gl, do_hg, states, lay)

    def hg_prep_bwd(i, hf_, dkk_, dgl_, l0, l1):
        _, vjp = jax.vjp(_hg_prep, hf_, l0, l1)
        return vjp((_group_cumsum(dgl_, tile, reverse=True), dkk_))

    dproj, g_lb0, g_lb1 = rw(hg_prep_bwd, [(proj, 512, C_HF), dkk, dgl], [lb0, lb1], [(512, MXU_DTYPE)], [(1, HG_W), (1, HG_W)],
                             name="hg_prep_bwd", into=(dproj, C_HF))
    grads["hg_lb_logits"] = jnp.concatenate([g_lb0, g_lb1], axis=0)

    k_t = (k.astype(F32) * FOX_SCALE).astype(MXU_DTYPE).T.reshape(FOX_HEADS, FOX_DIM, n)
    k_t = jnp.concatenate([k_t, jnp.ones((FOX_HEADS, KT_ROWS - FOX_DIM, n), MXU_DTYPE)], axis=1).reshape(FOX_HEADS * KT_ROWS, n)
    late_grads = [grads.pop(n_) for n_ in LATE]
    parts = [_add_own_half(g_, s_, c, BF16, name=f"reduce_add2_{n_}")
             for n_, g_, s_ in zip(LATE, late_grads, _sibling_exchange(late_grads, "reduce_sibling_late"))]
    dq_t, dk_aug, dv, from_chips = _fox_bwd_t(q_aug, k_aug, v, dofox, k_t, o_t, dofox.T, lse, parts, lay)
    reduced = [_add_chips(p_, g_, mine, name=f"reduce_add4_{n_}") for n_, p_, g_ in zip(LATE, parts, from_chips)]
    dq_t = dq_t.reshape(FOX_HEADS, KT_ROWS, n)
    dq = dq_t[:, :FOX_DIM].reshape(FOX_W, n).T
    dk_aug = dk_aug.reshape(n, FOX_HEADS, AUG)
    dk = dk_aug[:, :, :FOX_DIM].reshape(n, FOX_W)
    dcum = jnp.pad(dq_t[:, FOX_DIM].T - dk_aug[:, :, FOX_DIM], ((0, 0), (0, LANES - FOX_HEADS)))
    dlogf = _cumsum_rows(dcum, lay, reverse=True, name="fox_cum_bwd")

    def fox_prep_bwd(i, a, b_, f_, dq_, dk_, dl_, gq_, gk_, bf_, m_, mt_):
        valid = lay.valid(i, tile)
        _, vjp = jax.vjp(lambda a_, b__, f__, gq__, gk__, bf__: _fox_prep(a_, b__, f__, gq__, gk__, bf__, m_, mt_, valid),
                         a, b_, f_, gq_, gk_, bf_)
        da, db, df, dgq, dgk, dbf = vjp((dq_, dk_, dl_))
        return jnp.concatenate([da, db], axis=1), df, dgq, dgk, dbf

    dproj, dff, g_gq, g_gk, g_bf = rw(
        fox_prep_bwd, [(proj, 512, C_FQ), (proj, 512, C_FK), pff, dq, dk, dlogf], [gq, gk, bf, gm64, gm64_t],
        [(2 * FOX_W, MXU_DTYPE), (LANES, MXU_DTYPE)], [(1, FOX_W), (1, FOX_W), (1, LANES)], name="fox_prep_bwd",
        into=(dproj, C_FQ // 2))
    grads["q_norm_gain"] = g_gq.reshape(FOX_HEADS, FOX_DIM).sum(0, keepdims=True)
    grads["k_norm_gain"] = g_gk.reshape(FOX_HEADS, FOX_DIM).sum(0, keepdims=True)
    grads["fox_b_f"] = g_bf[:, :FOX_HEADS]

    (dproj,) = rw(lambda i, a, b_, c_: jnp.concatenate([a, b_, c_], axis=1), [dv, dhq, dhi], [], [(3 * 512, MXU_DTYPE)], [],
                  name="dproj_cast", into=(dproj, C_FV // 3))
    dxn = _matmul(dproj, w_main, trans_b=True, name="proj_bwd_x")
    dxn_ff = _matmul(dff, w_ff, trans_b=True, name="proj_ff_bwd_x")
    g_main = _matmul(xn, dproj, trans_a=True, name="proj_bwd_w")
    g_ff = _matmul(xn, dff, trans_a=True, name="proj_ff_bwd_w")[:, :FOX_HEADS]
    p = jnp.split(g_main, list(np.cumsum([1024, 1024] + [512] * 6)), axis=1)
    grads["w_in"] = _by_chip(jnp.concatenate([p[2], p[3], p[4], g_ff, p[5], p[7], p[6], p[8], p[0], p[1]], axis=1))

    per = lay.lp // tile

    def norm1_bwd(i, h, d1, d2, dh1_, g):
        _, vjp = jax.vjp(_rms, h, g)
        dh, dg = vjp(d1 + d2)
        dh = dh + dh1_
        dmeta = jnp.where(lax.rem(i, per) == 0, dh[LEAD:LEAD + N_META, :], 0.0)
        return dh, dg, dmeta

    dh0, grads["norm1_gain"], grads["meta_tokens"] = rw(norm1_bwd, [h0, dxn, dxn_ff, dh1], [g1], [(D_MODEL, F32)],
                                                       [(1, D_MODEL), (N_META, D_MODEL)], name="norm1_bwd")
    grad_x = dh0.reshape(lay.batch, lay.lp, D_MODEL)[:, ROW0:ROW0 + lay.seq]
    return loss, grad_x, grads, reduced


MESH = pl.DeviceIdType.MESH
HBM_SPEC = pl.BlockSpec(memory_space=pltpu.HBM)
WEIGHT_NAMES = ["meta_tokens", "norm1_gain", "w_in", "fox_b_f", "q_norm_gain", "k_norm_gain", "hg_lb_logits", "hg_out_gain",
                "w_branch_a", "w_branch_b", "w_out", "norm2_gain", "w_up", "conv_w", "conv_b", "w_down"]
BIG = ("w_in", "w_branch_a", "w_branch_b", "w_out", "w_up", "w_down")
BIG_COL_SHARDED = ("w_in", "w_branch_a", "w_branch_b", "w_up")
LATE = BIG[1:]
SMALL = tuple(n for n in WEIGHT_NAMES if n not in BIG)
SMALL_SHARDED = ("meta_tokens", "conv_w")
SMALL_ROWS = 144
GATHER_SMALL_ROWS = 80


def _position():
    return lax.axis_index("x"), lax.axis_index("y"), lax.axis_index("c")


def _other_chips(x, y):
    return [(1 - x, y), (x, 1 - y), (1 - x, 1 - y)]


def _scalar(v):
    return jnp.reshape(v, (1,)).astype(jnp.int32)


def _row_tile(rows, cols):
    width = -(-cols // LANES) * LANES * 4
    best = 8
    for d in range(8, rows + 1, 8):
        if rows % d == 0 and d * width <= (1 << 20):
            best = d
    return best


class _Gather:
    def __init__(self, xs, outs, send_sems, recv_sems):
        self.xs, self.outs, self.send_sems, self.recv_sems = xs, outs, send_sems, recv_sems
        self.na = len(xs)
        self.x, self.y, self.c = _position()
        self.me, self.sibling = (self.x, self.y, self.c), (self.x, self.y, 1 - self.c)
        self.chips = _other_chips(self.x, self.y)

    def _copy(self, a, k, block, to, own=False):
        dst = self.outs[a].at[4 * block[0] + 2 * block[1] + block[2]]
        src = dst
        if own:
            half = self.xs[a].shape[0] // 2
            src = self.xs[a].at[pl.ds(pl.multiple_of(self.c * half, 8), half), :]
        return pltpu.make_async_remote_copy(src_ref=src, dst_ref=dst, send_sem=self.send_sems.at[a, k],
                                            recv_sem=self.recv_sems.at[a, k], device_id=to, device_id_type=MESH)

    def _firsts(self):
        return [self._copy(a, j, self.me, (*chip, self.c), own=True) for a in range(self.na) for j, chip in enumerate(self.chips)]

    def _relays(self):
        return [self._copy(a, 3 + j, (*chip, self.c), self.sibling) for j, chip in enumerate(self.chips) for a in range(self.na)]

    def start(self):
        for cp in self._firsts():
            cp.start()

    def relay(self):
        for j, chip in enumerate(self.chips):
            for a in range(self.na):
                self._copy(a, j, (*chip, self.c), self.me).wait_recv()
                self._copy(a, 3 + j, (*chip, self.c), self.sibling).start()

    def finish(self):
        for a in range(self.na):
            for j, chip in enumerate(self.chips):
                self._copy(a, 3 + j, (*chip, 1 - self.c), self.me).wait_recv()
        for cp in self._firsts() + self._relays():
            cp.wait_send()

    @staticmethod
    def out_shapes(shards):
        return [jax.ShapeDtypeStruct((8, s.shape[0] // 2, s.shape[1]), s.dtype) for s in shards]

    @staticmethod
    def semaphores(na):
        return [pltpu.SemaphoreType.DMA((na, 6)), pltpu.SemaphoreType.DMA((na, 6))]


def _gather_shards(shards):
    na = len(shards)

    def body(*refs):
        g = _Gather(refs[:na], refs[na:2 * na], refs[2 * na], refs[2 * na + 1])
        g.start()
        g.relay()
        g.finish()

    return pl.pallas_call(
        body, name="gather_weights", out_shape=_Gather.out_shapes(shards),
        in_specs=[HBM_SPEC] * na, out_specs=[HBM_SPEC] * na, scratch_shapes=_Gather.semaphores(na),
    )(*shards)


def _sibling_exchange(gs, name):
    na = len(gs)
    halves = [g.shape[1] // 2 for g in gs]

    def body(*refs):
        srcs, gots, send_sems, recv_sems = refs[:na], refs[na:2 * na], refs[2 * na], refs[2 * na + 1]
        x, y, c = _position()
        copies = [pltpu.make_async_remote_copy(
            src_ref=srcs[a].at[:, pl.ds(pl.multiple_of((1 - c) * halves[a], 8), halves[a]), :], dst_ref=gots[a],
            send_sem=send_sems.at[a], recv_sem=recv_sems.at[a], device_id=(x, y, 1 - c), device_id_type=MESH) for a in range(na)]
        for cp in copies:
            cp.start()
        for cp in copies:
            cp.wait()

    return pl.pallas_call(
        body, name=name,
        out_shape=[jax.ShapeDtypeStruct((N_CHIPS, h, g.shape[2]), g.dtype) for h, g in zip(halves, gs)],
        in_specs=[HBM_SPEC] * na, out_specs=[HBM_SPEC] * na,
        scratch_shapes=[pltpu.SemaphoreType.DMA((na,)), pltpu.SemaphoreType.DMA((na,))],
    )(*gs)


class _ChipExchange:
    def __init__(self, srcs, gots, send_sems, recv_sems):
        self.srcs, self.gots, self.send_sems, self.recv_sems = srcs, gots, send_sems, recv_sems
        self.na = len(srcs)
        x, y, self.c = _position()
        self.mine = 2 * x + y
        self.chips = _other_chips(x, y)

    def _copy(self, a, j, arriving):
        cx, cy = self.chips[j]
        theirs = 2 * cx + cy
        src = self.srcs[a].at[self.mine if arriving else theirs]
        dst = self.gots[a].at[theirs if arriving else self.mine]
        return pltpu.make_async_remote_copy(src_ref=src, dst_ref=dst, send_sem=self.send_sems.at[a, j],
                                            recv_sem=self.recv_sems.at[a, j], device_id=(cx, cy, self.c), device_id_type=MESH)

    def start(self):
        for a in range(self.na):
            for j in range(3):
                self._copy(a, j, False).start()

    def finish(self):
        for a in range(self.na):
            for j in range(3):
                self._copy(a, j, True).wait_recv()
        for a in range(self.na):
            for j in range(3):
                self._copy(a, j, False).wait_send()

    @staticmethod
    def semaphores(na):
        return [pltpu.SemaphoreType.DMA((na, 3)), pltpu.SemaphoreType.DMA((na, 3))]


def _chip_exchange(parts):
    na = len(parts)

    def body(*refs):
        ex = _ChipExchange(refs[:na], refs[na:2 * na], refs[2 * na], refs[2 * na + 1])
        ex.start()
        ex.finish()

    return pl.pallas_call(
        body, name="reduce_chips", out_shape=[jax.ShapeDtypeStruct(p.shape, p.dtype) for p in parts],
        in_specs=[HBM_SPEC] * na, out_specs=[HBM_SPEC] * na, scratch_shapes=_ChipExchange.semaphores(na),
    )(*parts)


def _sibling_send(halves):
    na = len(halves)

    def body(*refs):
        srcs, gots, send_sems, recv_sems = refs[:na], refs[na:2 * na], refs[2 * na], refs[2 * na + 1]
        x, y, c = _position()
        copies = [pltpu.make_async_remote_copy(src_ref=srcs[a], dst_ref=gots[a], send_sem=send_sems.at[a], recv_sem=recv_sems.at[a],
                                               device_id=(x, y, 1 - c), device_id_type=MESH) for a in range(na)]
        for cp in copies:
            cp.start()
        for cp in copies:
            cp.wait()

    return pl.pallas_call(
        body, name="reduce_gather", out_shape=[jax.ShapeDtypeStruct(h.shape, h.dtype) for h in halves],
        in_specs=[HBM_SPEC] * na, out_specs=[HBM_SPEC] * na,
        scratch_shapes=[pltpu.SemaphoreType.DMA((na,)), pltpu.SemaphoreType.DMA((na,))],
    )(*halves)


def _add_own_half(g, got, c, dtype, name):
    _, r, cols = g.shape
    r2 = r // 2
    tr = _row_tile(r2, cols)
    nrt = r2 // tr

    def body(c_ref, g_ref, got_ref, o_ref):
        o_ref[...] = (g_ref[...] + got_ref[...]).astype(o_ref.dtype)

    blk = (1, tr, cols)
    return pl.pallas_call(
        body, name=name,
        grid_spec=pltpu.PrefetchScalarGridSpec(
            num_scalar_prefetch=1, grid=(N_CHIPS, nrt),
            in_specs=[pl.BlockSpec(blk, lambda j, i, c_: (j, c_[0] * nrt + i, 0)), pl.BlockSpec(blk, lambda j, i, c_: (j, i, 0))],
            out_specs=pl.BlockSpec(blk, lambda j, i, c_: (j, i, 0))),
        out_shape=jax.ShapeDtypeStruct((N_CHIPS, r2, cols), dtype), compiler_params=_params(2),
    )(c, g, got)


def _add_chips(part, got, mine, name):
    _, r2, cols = part.shape
    tr = _row_tile(r2, cols)

    def body(m_ref, p_ref, g0, g1, g2, g3, o_ref):
        t = [jnp.where(m_ref[0] == k, p_ref[0], g[0]).astype(F32) for k, g in enumerate((g0, g1, g2, g3))]
        o_ref[...] = ((t[0] + t[1]) + t[2]) + t[3]

    blk = (1, tr, cols)
    others = [pl.BlockSpec(blk, functools.partial(lambda i, m, k: (jnp.where(m[0] == k, (k + 1) % N_CHIPS, k), i, 0), k=k))
              for k in range(N_CHIPS)]
    return pl.pallas_call(
        body, name=name,
        grid_spec=pltpu.PrefetchScalarGridSpec(
            num_scalar_prefetch=1, grid=(r2 // tr,),
            in_specs=[pl.BlockSpec(blk, lambda i, m: (m[0], i, 0))] + others,
            out_specs=pl.BlockSpec((tr, cols), lambda i, m: (i, 0))),
        out_shape=jax.ShapeDtypeStruct((r2, cols), F32), compiler_params=_params(1),
    )(mine, part, got, got, got, got)


def _adamw(w, own, other, m, v, c, name):
    r, cols = w.shape
    r2 = r // 2
    tr = _row_tile(r2, cols)
    nrt = r2 // tr
    c1 = 1.0 / (1.0 - ADAM_B1 ** ADAM_STEP)
    c2 = 1.0 / (1.0 - ADAM_B2 ** ADAM_STEP)

    def body(c_ref, w_ref, own_ref, other_ref, m_ref, v_ref, g_out, d_out, m_out, v_out):
        g_ = jnp.where(pl.program_id(0) == c_ref[0], own_ref[...], other_ref[...])
        m_new = ADAM_B1 * m_ref[...] + (1.0 - ADAM_B1) * g_
        v_new = ADAM_B2 * v_ref[...] + (1.0 - ADAM_B2) * (g_ * g_)
        g_out[...] = g_
        d_out[...] = -ADAM_LR * ((m_new * c1) / (jnp.sqrt(v_new * c2) + ADAM_EPS) + ADAM_WD * w_ref[...])
        m_out[...] = m_new
        v_out[...] = v_new

    full = pl.BlockSpec((tr, cols), lambda h, i, c_: (h * nrt + i, 0))
    half = pl.BlockSpec((tr, cols), lambda h, i, c_: (i, 0))
    out = jax.ShapeDtypeStruct((r, cols), F32)
    return pl.pallas_call(
        body, name=name,
        grid_spec=pltpu.PrefetchScalarGridSpec(num_scalar_prefetch=1, grid=(2, nrt), in_specs=[full, half, half, full, full],
                                               out_specs=[full] * 4),
        out_shape=[out] * 4, compiler_params=_params(2),
    )(c, w, own, other, m, v)


def _to_rows(flat, rows):
    return jnp.pad(flat, (0, rows * LANES - flat.shape[0])).reshape(rows, LANES)


def _pack_small(tree):
    return _to_rows(jnp.concatenate([tree[n].astype(F32).reshape(-1) for n in SMALL]), SMALL_ROWS)


def _unpack_small(packed, shapes):
    flat, out, at = packed.reshape(-1), {}, 0
    for n in SMALL:
        size = int(np.prod(shapes[n]))
        out[n] = flat[at:at + size].reshape(shapes[n])
        at += size
    return out


def _pack_small_by_chip(grads):
    pieces = []
    for n in SMALL:
        g = grads[n].astype(F32)
        if n in SMALL_SHARDED:
            pieces.append(_by_chip(g).reshape(N_CHIPS, -1))
        else:
            pieces.append(jnp.broadcast_to(g.reshape(1, -1), (N_CHIPS, g.size)))
    flat = jnp.concatenate(pieces, axis=1)
    return jnp.pad(flat, ((0, 0), (0, SMALL_ROWS * LANES - flat.shape[1]))).reshape(N_CHIPS, SMALL_ROWS, LANES)


def _bf16_shard(local, n):
    return local[n].reshape(local[n].shape[-2:]).astype(BF16)


def _all_chips(shard, slabs):
    x, y, _ = _position()
    is_mine = (lax.broadcasted_iota(jnp.int32, (N_CHIPS, 1, 1), 0) == 2 * x + y)
    return jnp.where(is_mine, shard[None], slabs.reshape((N_CHIPS,) + shard.shape))


def _assemble(names, shards, slabs):
    full = [_all_chips(s, g) for s, g in zip(shards, slabs)]
    return [_from_chips(f) if n in BIG_COL_SHARDED else f.reshape(N_CHIPS * f.shape[1], f.shape[2]) for n, f in zip(names, full)]


def _gather_first(local):
    shards = [_bf16_shard(local, "w_in"),
              _to_rows(jnp.concatenate([local[n].astype(F32).reshape(-1) for n in SMALL_SHARDED]), GATHER_SMALL_ROWS)]
    slabs = _gather_shards(shards)
    out = {"w_in": _assemble(["w_in"], shards[:1], slabs[:1])[0]}
    flat, at = _all_chips(shards[1], slabs[1]).reshape(N_CHIPS, -1), 0
    for n in SMALL_SHARDED:
        shape = local[n].shape[-2:]
        size = int(np.prod(shape))
        out[n] = _from_chips(flat[:, at:at + size].reshape((N_CHIPS,) + shape))
        at += size
    return out


def kernel(x, meta_tokens, norm1_gain, w_in, fox_b_f, q_norm_gain, k_norm_gain, hg_lb_logits, hg_out_gain, w_branch_a, w_branch_b, w_out, norm2_gain, w_up, conv_w, conv_b, w_down, loss_target, m_meta_tokens, m_norm1_gain, m_w_in, m_fox_b_f, m_q_norm_gain, m_k_norm_gain, m_hg_lb_logits, m_hg_out_gain, m_w_branch_a, m_w_branch_b, m_w_out, m_norm2_gain, m_w_up, m_conv_w, m_conv_b, m_w_down, v_meta_tokens, v_norm1_gain, v_w_in, v_fox_b_f, v_q_norm_gain, v_k_norm_gain, v_hg_lb_logits, v_hg_out_gain, v_w_branch_a, v_w_branch_b, v_w_out, v_norm2_gain, v_w_up, v_conv_w, v_conv_b, v_w_down):
    w_loc = dict(zip(WEIGHT_NAMES, (meta_tokens, norm1_gain, w_in, fox_b_f, q_norm_gain, k_norm_gain, hg_lb_logits, hg_out_gain,
                                    w_branch_a, w_branch_b, w_out, norm2_gain, w_up, conv_w, conv_b, w_down)))
    m_loc = dict(zip(WEIGHT_NAMES, (m_meta_tokens, m_norm1_gain, m_w_in, m_fox_b_f, m_q_norm_gain, m_k_norm_gain, m_hg_lb_logits,
                                    m_hg_out_gain, m_w_branch_a, m_w_branch_b, m_w_out, m_norm2_gain, m_w_up, m_conv_w, m_conv_b,
                                    m_w_down)))
    v_loc = dict(zip(WEIGHT_NAMES, (v_meta_tokens, v_norm1_gain, v_w_in, v_fox_b_f, v_q_norm_gain, v_k_norm_gain, v_hg_lb_logits,
                                    v_hg_out_gain, v_w_branch_a, v_w_branch_b, v_w_out, v_norm2_gain, v_w_up, v_conv_w, v_conv_b,
                                    v_w_down)))
    local_shapes = {n: tuple(w_loc[n].shape) for n in WEIGHT_NAMES}
    px, py, pc = _position()
    c, mine = _scalar(pc), _scalar(2 * px + py)

    weights = {n: w_loc[n].reshape(w_loc[n].shape[-2:]) for n in SMALL if n not in SMALL_SHARDED}
    weights.update(_gather_first(w_loc))

    lay = _Layout(x.shape[0], x.shape[1])
    loss, grad_x, grads, reduced_late = _local_step(x, loss_target, weights, [_bf16_shard(w_loc, n) for n in LATE], c, mine, lay)
    loss = lax.psum(loss, ("x", "y", "c"))

    names = ["w_in", "small"]
    by_chip = [grads["w_in"], _pack_small_by_chip(grads)]
    from_sibling = _sibling_exchange(by_chip, "reduce_sibling")
    parts = [_add_own_half(g, s, c, F32 if n == "small" else BF16, name=f"reduce_add2_{n}")
             for n, g, s in zip(names, by_chip, from_sibling)]
    from_chips = _chip_exchange(parts)
    own = [_add_chips(p, g, mine, name=f"reduce_add4_{n}") for n, p, g in zip(names, parts, from_chips)]
    names = list(BIG) + ["small"]
    own = [own[0]] + reduced_late + [own[1]]
    other = _sibling_send(own)

    two_d = lambda t: [t[n].reshape(t[n].shape[-2:]) for n in BIG] + [_pack_small(t)]
    results = [_adamw(w_, o_, t_, m_, v_, c, name=f"adamw_{n}")
               for n, w_, o_, t_, m_, v_ in zip(names, two_d(w_loc), own, other, two_d(m_loc), two_d(v_loc))]
    outs = []
    for kind in range(4):
        tree = {n: results[i][kind].reshape(local_shapes[n]) for i, n in enumerate(BIG)}
        tree.update(_unpack_small(results[-1][kind], local_shapes))
        outs += [tree[n] for n in WEIGHT_NAMES]
    return (loss, grad_x, *outs)
```

```python
import functools

import jax
import jax.numpy as jnp
import numpy as np
from jax import lax
from jax.experimental import pallas as pl
from jax.experimental.pallas import tpu as pltpu

F32 = jnp.float32
BF16 = jnp.bfloat16
MXU_DTYPE = BF16
HIGHEST = lax.Precision.HIGHEST

D_MODEL = 1024
N_META = 16
LEAD = 48
ROW0 = LEAD + N_META
FOX_HEADS, FOX_DIM, FOX_W = 8, 64, 512
HG_HEADS, HG_DIM, HG_W = 4, 128, 512
D_FF = 2816
FF2 = 2 * D_FF
EPS = 1e-6
SUB = 16
LANES = 128
N_CHIPS = 4
NEG = -1e30

ADAM_LR, ADAM_B1, ADAM_B2, ADAM_EPS, ADAM_WD, ADAM_STEP = 0.001, 0.9, 0.999, 1e-08, 0.01, 10

VMEM_LIMIT = 56 * 1024 * 1024

C_GA, C_GB = 0, 1
C_FQ, C_FK, C_FV, C_HQ, C_HI, C_HF, C_HG = 4, 5, 6, 7, 8, 9, 10
MAIN_COLS = 11 * 512


def _params(n_axes=1):
    return pltpu.CompilerParams(dimension_semantics=("arbitrary",) * n_axes, vmem_limit_bytes=VMEM_LIMIT)


def _pick(n, cands):
    for c in cands:
        if n % c == 0:
            return c
    raise ValueError(f"no tile for {n} among {cands}")


def _rowwise(fn, rows, consts, outs, reds, *, n_rows, tile, name, into=None):
    assert n_rows % tile == 0
    rows = [r if isinstance(r, tuple) else (r, r.shape[1], 0) for r in rows]
    nr, nc, no = len(rows), len(consts), len(outs)
    aliased = into is not None and not isinstance(into[0], int)
    n_in = nr + nc + (1 if aliased else 0)

    def body(*refs):
        i = pl.program_id(0)
        ins = [r[...] for r in refs[:nr + nc]]
        res = fn(i, *ins)
        res = res if isinstance(res, (tuple, list)) else (res,)
        for ref, v in zip(refs[n_in:n_in + no], res[:no]):
            ref[...] = v.astype(ref.dtype)
        red_refs = refs[n_in + no:]
        if red_refs:
            @pl.when(i == 0)
            def _():
                for ref in red_refs:
                    ref[...] = jnp.zeros_like(ref)
            for ref, v in zip(red_refs, res[no:]):
                ref[...] += v.astype(F32)

    in_specs = [pl.BlockSpec((tile, w), functools.partial(lambda i, j: (i, j), j=j)) for (_, w, j) in rows]
    in_specs += [pl.BlockSpec(c.shape, functools.partial(lambda i, nd: (0,) * nd, nd=c.ndim)) for c in consts]
    out_specs = [pl.BlockSpec((tile, w), lambda i: (i, 0)) for (w, _) in outs]
    out_specs += [pl.BlockSpec(s, functools.partial(lambda i, nd: (0,) * nd, nd=len(s))) for s in reds]
    out_shape = [jax.ShapeDtypeStruct((n_rows, w), dt) for (w, dt) in outs]
    out_shape += [jax.ShapeDtypeStruct(s, F32) for s in reds]
    args = [r[0] for r in rows] + list(consts)
    aliases = {}
    if into is not None:
        out_specs[0] = pl.BlockSpec((tile, outs[0][0]), functools.partial(lambda i, j: (i, j), j=into[1]))
        if aliased:
            in_specs.append(pl.BlockSpec(memory_space=pltpu.HBM))
            args.append(into[0])
            aliases = {n_in - 1: 0}
            out_shape[0] = jax.ShapeDtypeStruct(into[0].shape, into[0].dtype)
        else:
            out_shape[0] = jax.ShapeDtypeStruct((n_rows, into[0]), outs[0][1])
    return pl.pallas_call(
        body, name=name, grid=(n_rows // tile,), in_specs=in_specs, out_specs=out_specs, out_shape=out_shape,
        input_output_aliases=aliases, compiler_params=_params(1),
    )(*args)


def _matmul(a, b, *, trans_a=False, trans_b=False, out_dtype=F32, by_chip=False, name):
    if trans_a:
        k, m = a.shape
    else:
        m, k = a.shape
    n = b.shape[0] if trans_b else b.shape[1]
    assert (b.shape[1] if trans_b else b.shape[0]) == k
    if trans_a:
        tm = _pick(m, (1408, 1024, 512, 256, 128))
        tk = _pick(k, (2176, 1088, 1024, 768, 512, 256))
    else:
        tm = _pick(m, (1088, 512, 256, 128))
        tk = k if k <= 1024 else _pick(k, (2176, 1408, 1024, 512))
    nk = k // tk
    wide = (2816,) if nk == 1 and not trans_a else ()
    tn = n // N_CHIPS if by_chip else _pick(n, wide + (1408, 1024, 512, 256, 128))
    if not trans_a and tn <= 1408 and m % 2176 == 0:
        tm = 2176
    dims = (((0 if trans_a else 1,), (1 if trans_b else 0,)), ((), ()))

    def body(a_ref, b_ref, o_ref, acc_ref):
        out = o_ref.at[0] if by_chip else o_ref
        part = lax.dot_general(a_ref[...], b_ref[...], dims, preferred_element_type=F32)
        if nk == 1:
            out[...] = part.astype(out.dtype)
        else:
            kk = pl.program_id(2)

            @pl.when(kk == 0)
            def _():
                acc_ref[...] = part

            @pl.when(kk > 0)
            def _():
                acc_ref[...] += part

            @pl.when(kk == nk - 1)
            def _():
                out[...] = acc_ref[...].astype(out.dtype)

    a_spec = pl.BlockSpec((tk, tm), lambda i, j, kk: (kk, i)) if trans_a else pl.BlockSpec((tm, tk), lambda i, j, kk: (i, kk))
    b_spec = pl.BlockSpec((tn, tk), lambda i, j, kk: (j, kk)) if trans_b else pl.BlockSpec((tk, tn), lambda i, j, kk: (kk, j))
    if by_chip:
        out_spec, out_shape = pl.BlockSpec((1, tm, tn), lambda i, j, kk: (j, i, 0)), (N_CHIPS, m, tn)
    else:
        out_spec, out_shape = pl.BlockSpec((tm, tn), lambda i, j, kk: (i, j)), (m, n)
    return pl.pallas_call(
        body, name=name, grid=(m // tm, n // tn, nk), in_specs=[a_spec, b_spec], out_specs=out_spec,
        out_shape=jax.ShapeDtypeStruct(out_shape, out_dtype),
        scratch_shapes=[pltpu.VMEM((tm, tn) if nk > 1 else (8, LANES), F32)],
        compiler_params=_params(3),
    )(a, b)


def _sigmoid(x):
    return 1.0 / (1.0 + jnp.exp(-x))


def _silu(x):
    return x * _sigmoid(x)


def _log_sigmoid(x):
    return jnp.minimum(x, 0.0) - jnp.log(1.0 + jnp.exp(-jnp.abs(x)))


def _rms(x, gain):
    return x * lax.rsqrt(jnp.mean(x * x, axis=-1, keepdims=True) + EPS) * gain


def _group_matrix(width, group):
    g = (np.arange(width)[:, None] // group == np.arange(LANES)[None, :]).astype(np.float32)
    return jnp.asarray(g, MXU_DTYPE), jnp.asarray(g.T.copy(), MXU_DTYPE)


def _split_dot(x, mat):
    dt = mat.dtype
    hi = x.astype(dt)
    r1 = x - hi.astype(F32)
    mid = r1.astype(dt)
    lo = (r1 - mid.astype(F32)).astype(dt)
    dot = lambda a: jnp.dot(a, mat, preferred_element_type=F32)
    return dot(hi) + dot(mid) + dot(lo)


@jax.custom_vjp
def _group_sum(x, gmat, gmat_t):
    return _split_dot(x, gmat)


@jax.custom_vjp
def _group_spread(s, gmat, gmat_t):
    return _split_dot(s, gmat_t)


_group_sum.defvjp(lambda x, g, gt: (_split_dot(x, g), (g, gt)),
                  lambda res, ct: (_group_spread(ct, *res), jnp.zeros_like(res[0]), jnp.zeros_like(res[1])))
_group_spread.defvjp(lambda s, g, gt: (_split_dot(s, gt), (g, gt)),
                     lambda res, ct: (_group_sum(ct, *res), jnp.zeros_like(res[0]), jnp.zeros_like(res[1])))


def _group_rms(x, gain, gmat, gmat_t, group):
    rstd = lax.rsqrt(_group_sum(x * x, gmat, gmat_t) * (1.0 / group) + EPS)
    return x * _group_spread(rstd, gmat, gmat_t) * gain


def _head_rms(x, gain):
    outs = []
    for h in range(x.shape[1] // LANES):
        xs = x[:, h * LANES:(h + 1) * LANES]
        outs.append(xs * lax.rsqrt(jnp.mean(xs * xs, axis=-1, keepdims=True) + EPS) * gain)
    return jnp.concatenate(outs, axis=1)


class _Layout:
    def __init__(self, batch, seq):
        self.batch, self.seq = batch, seq
        self.l_real = N_META + seq
        self.lp = -(-(LEAD + self.l_real) // 256) * 256
        self.n = batch * self.lp
        self.tile = _pick(self.lp, (512, 256))

    def valid(self, i, tile):
        per = self.lp // tile
        r = lax.rem(i, per) * tile + lax.broadcasted_iota(jnp.int32, (tile, 1), 0)
        return (r >= LEAD) & (r < LEAD + self.l_real)


def _cumsum_rows(x, lay, *, reverse, name):
    t = lay.tile
    nt = lay.lp // t
    c = x.shape[1]

    def body(x_ref, o_ref, carry):
        j = pl.program_id(1)

        @pl.when(j == 0)
        def _():
            carry[...] = jnp.zeros_like(carry)

        r = lax.broadcasted_iota(jnp.int32, (t, t), 0)
        q = lax.broadcasted_iota(jnp.int32, (t, t), 1)
        tri = jnp.where((q >= r) if reverse else (q <= r), 1.0, 0.0).astype(F32)
        xs = x_ref[...]
        out = jnp.dot(tri, xs, precision=HIGHEST, preferred_element_type=F32) + carry[0:1, :]
        o_ref[...] = out
        carry[...] = jnp.broadcast_to(carry[0:1, :] + jnp.sum(xs, axis=0, keepdims=True), carry.shape)

    def idx(b, j):
        return (b * nt + (nt - 1 - j if reverse else j), 0)

    return pl.pallas_call(
        body, name=name, grid=(lay.batch, nt),
        in_specs=[pl.BlockSpec((t, c), idx)], out_specs=pl.BlockSpec((t, c), idx),
        out_shape=jax.ShapeDtypeStruct(x.shape, F32),
        scratch_shapes=[pltpu.VMEM((8, c), F32)],
        compiler_params=_params(2),
    )(x)


def _group_cumsum(x, tile, *, reverse):
    r = lax.rem(lax.broadcasted_iota(jnp.int32, (tile, 1), 0), SUB)
    s = 1
    while s < SUB:
        if reverse:
            x = x + jnp.where(r < SUB - s, pltpu.roll(x, tile - s, 0), 0.0)
        else:
            x = x + jnp.where(r >= s, pltpu.roll(x, s, 0), 0.0)
        s *= 2
    return x


AUG = 128
FOX_BK = 256
FOX_BQ = 256
FOX_SCALE = FOX_DIM ** -0.5
KT_ROWS = FOX_DIM + 16


def _aug_matrices():
    e1 = np.zeros((FOX_W, FOX_HEADS * AUG), np.float32)
    e2 = np.zeros((LANES, FOX_HEADS * AUG), np.float32)
    ones = np.zeros((1, FOX_HEADS * AUG), np.float32)
    for h in range(FOX_HEADS):
        for d in range(FOX_DIM):
            e1[h * FOX_DIM + d, h * AUG + d] = 1.0
        for j in range(3):
            e2[j * FOX_HEADS + h, h * AUG + FOX_DIM + j] = 1.0
            ones[0, h * AUG + FOX_DIM + j] = 1.0
    return jnp.asarray(e1, MXU_DTYPE), jnp.asarray(e2, MXU_DTYPE), jnp.asarray(ones)


def _fox_augment(q, k, cum, key_ok, e1, e2, ones):
    dt = q.dtype
    c = jnp.where(key_ok, -cum, NEG)
    hi = c.astype(dt)
    r1 = c - hi.astype(F32)
    mid = r1.astype(dt)
    lo = (r1 - mid.astype(F32)).astype(dt)
    lane = lax.broadcasted_iota(jnp.int32, c.shape, 1)
    shift = lambda a, by: pltpu.roll(a.astype(F32), by, 1)
    parts = jnp.where(lane < FOX_HEADS, hi.astype(F32),
                      jnp.where(lane < 2 * FOX_HEADS, shift(mid, FOX_HEADS),
                                jnp.where(lane < 3 * FOX_HEADS, shift(lo, 2 * FOX_HEADS), 0.0))).astype(dt)
    qs = (q.astype(F32) * FOX_SCALE).astype(dt)
    q_aug = jnp.dot(qs, e1, preferred_element_type=F32) + ones
    k_aug = jnp.dot(k, e1, preferred_element_type=F32) + jnp.dot(parts, e2, preferred_element_type=F32)
    return q_aug.astype(dt), k_aug.astype(dt)


def _fox_tile(k_blk, q_blk, k0, q0, masked):
    st = lax.dot_general(k_blk, q_blk, (((1,), (1,)), ((), ())), preferred_element_type=F32)
    if masked:
        keys = k0 + lax.broadcasted_iota(jnp.int32, st.shape, 0)
        qs = q0 + lax.broadcasted_iota(jnp.int32, st.shape, 1)
        st = jnp.where(keys <= qs, st, NEG)
    return st


def _fox_fwd_t(q_aug, k_aug, v_t, shards, lay):
    bk, bq = FOX_BK, FOX_BQ
    nq = lay.lp // bq
    pairs = FOX_HEADS // 2
    ng = len(shards)
    steps = lay.batch * pairs

    def body(*refs):
        q_ref, k_ref, vt_ref = refs[:3]
        ot_ref, lse_ref = refs[3 + ng:5 + ng]
        zeros_ref = refs[5 + 2 * ng]
        gather = _Gather(refs[3:3 + ng], refs[5 + ng:5 + 2 * ng], refs[6 + 2 * ng], refs[7 + 2 * ng])
        step = pl.program_id(0) * pairs + pl.program_id(1)
        pl.when(step == 0)(gather.start)
        pl.when(step == steps // 2)(gather.relay)
        heads = [(slice(hh * AUG, (hh + 1) * AUG), slice(hh * FOX_DIM, (hh + 1) * FOX_DIM)) for hh in range(2)]
        zeros_ref[...] = jnp.zeros_like(zeros_ref)

        def q_loop(qb, _):
            q0 = pl.multiple_of(qb * bq, bq)
            q_blks = [q_ref[pl.ds(q0, bq), lanes] for lanes, _ in heads]

            def scores(kb, h):
                k0 = pl.multiple_of(kb * bk, bk)
                return _fox_tile(k_ref[pl.ds(k0, bk), heads[h][0]], q_blks[h], k0, q0, False)

            def consume(kb, h, state, masked):
                m, l, acc, pend, st = state
                k0 = pl.multiple_of(kb * bk, bk)
                if masked:
                    keys = k0 + lax.broadcasted_iota(jnp.int32, st.shape, 0)
                    qs_ = q0 + lax.broadcasted_iota(jnp.int32, st.shape, 1)
                    st = jnp.where(keys <= qs_, st, NEG)
                m_new = jnp.maximum(m, jnp.max(st, axis=0, keepdims=True))
                alpha = jnp.exp(m - m_new)
                p = jnp.exp(st - m_new)
                l = alpha * l + jnp.sum(p, axis=0, keepdims=True)
                acc = alpha * (acc + pend)
                pend = jnp.dot(vt_ref[heads[h][1], pl.ds(k0, bk)], p.astype(vt_ref.dtype), preferred_element_type=F32)
                return m_new, l, acc, pend

            def k_step(kb, states):
                nxt = [scores(kb + 1, h) for h in range(2)]
                return tuple(consume(kb, h, states[h], False) + (nxt[h],) for h in range(2))

            states = tuple((jnp.full((1, bq), NEG, F32), jnp.zeros((1, bq), F32), zeros_ref[...], zeros_ref[...], scores(0, h))
                           for h in range(2))
            states = lax.fori_loop(0, qb, k_step, states)
            qs = q0 + lax.broadcasted_iota(jnp.int32, (1, bq), 1)
            ok = (qs >= LEAD) & (qs < LEAD + lay.l_real)
            for hh in range(2):
                m, l, acc, pend = consume(qb, hh, states[hh], True)
                ot_ref[heads[hh][1], pl.ds(q0, bq)] = jnp.where(ok, (acc + pend) / l, 0.0).astype(ot_ref.dtype)
                lse_ref[hh, :, pl.ds(q0, bq)] = m + jnp.log(l)
            return 0

        lax.fori_loop(0, nq, q_loop, 0)
        pl.when(step == steps - 1)(gather.finish)

    aug = pl.BlockSpec((lay.lp, 2 * AUG), lambda b, p: (b, p))
    tr = pl.BlockSpec((2 * FOX_DIM, lay.lp), lambda b, p: (p, b))
    outs = pl.pallas_call(
        body, name="fox_fwd", grid=(lay.batch, pairs),
        in_specs=[aug, aug, tr] + [HBM_SPEC] * ng,
        out_specs=[tr, pl.BlockSpec((2, 1, lay.lp), lambda b, p: (b * pairs + p, 0, 0))] + [HBM_SPEC] * ng,
        out_shape=[jax.ShapeDtypeStruct((FOX_W, lay.n), MXU_DTYPE),
                   jax.ShapeDtypeStruct((lay.batch * FOX_HEADS, 1, lay.lp), F32)] + _Gather.out_shapes(shards),
        scratch_shapes=[pltpu.VMEM((FOX_DIM, bq), F32)] + _Gather.semaphores(ng),
        compiler_params=_params(2),
    )(q_aug, k_aug, v_t, *shards)
    return outs[0], outs[1], outs[2:]


def _fox_bwd_t(q_aug, k_aug, v, do, k_t, o_t, do_t, lse, parts, lay):
    bk, bq = FOX_BK, FOX_BQ
    nq, nk = lay.lp // bq, lay.lp // bk
    pairs = FOX_HEADS // 2
    ne = len(parts)

    def body(*refs):
        q_ref, k_ref, v_ref, do_ref, kt_ref, ot_ref, dot_ref, lse_ref = refs[:8]
        dqt_ref, dk_ref, dv_ref = refs[8 + ne:11 + ne]
        delta = refs[11 + 2 * ne]
        exchange = _ChipExchange(refs[8:8 + ne], refs[11 + ne:11 + 2 * ne], refs[12 + 2 * ne], refs[13 + 2 * ne])
        step = pl.program_id(0) * pairs + pl.program_id(1)
        pl.when(step == 0)(exchange.start)
        dqt_ref[...] = jnp.zeros_like(dqt_ref)
        dk_ref[...] = jnp.zeros_like(dk_ref)
        dv_ref[...] = jnp.zeros_like(dv_ref)
        heads = [(hh, slice(hh * AUG, (hh + 1) * AUG), slice(hh * FOX_DIM, (hh + 1) * FOX_DIM),
                  slice(hh * KT_ROWS, (hh + 1) * KT_ROWS)) for hh in range(2)]

        def delta_loop(qb, _):
            q0 = pl.multiple_of(qb * bq, bq)
            for hh, _, cols, _ in heads:
                prod = ot_ref[cols, pl.ds(q0, bq)].astype(F32) * dot_ref[cols, pl.ds(q0, bq)].astype(F32)
                delta[hh, :, pl.ds(q0, bq)] = jnp.sum(prod, axis=0, keepdims=True)
            return 0

        lax.fori_loop(0, nq, delta_loop, 0)

        def k_loop(kb, _):
            k0 = pl.multiple_of(kb * bk, bk)

            def products(qb, h):
                q0 = pl.multiple_of(qb * bq, bq)
                _, lanes, cols, _ = heads[h]
                st = _fox_tile(k_ref[pl.ds(k0, bk), lanes], q_ref[pl.ds(q0, bq), lanes], k0, q0, False)
                dpt = lax.dot_general(v_ref[pl.ds(k0, bk), cols], do_ref[pl.ds(q0, bq), cols], (((1,), (1,)), ((), ())),
                                      preferred_element_type=F32)
                return st, dpt

            def consume(qb, h, st, dpt, masked):
                q0 = pl.multiple_of(qb * bq, bq)
                hh, lanes, cols, trows = heads[h]
                if masked:
                    keys = k0 + lax.broadcasted_iota(jnp.int32, st.shape, 0)
                    qs = q0 + lax.broadcasted_iota(jnp.int32, st.shape, 1)
                    st = jnp.where(keys <= qs, st, NEG)
                q_blk = q_ref[pl.ds(q0, bq), lanes]
                do_blk = do_ref[pl.ds(q0, bq), cols]
                pt = jnp.exp(st - lse_ref[hh, :, pl.ds(q0, bq)])
                dst = (pt * (dpt - delta[hh, :, pl.ds(q0, bq)])).astype(q_blk.dtype)
                dv_ref[pl.ds(k0, bk), cols] += jnp.dot(pt.astype(do_blk.dtype), do_blk, preferred_element_type=F32)
                dk_ref[pl.ds(k0, bk), lanes] += jnp.dot(dst, q_blk, preferred_element_type=F32)
                dqt_ref[trows, pl.ds(q0, bq)] += jnp.dot(kt_ref[trows, pl.ds(k0, bk)], dst, preferred_element_type=F32)

            after = lambda qb: jnp.minimum(qb + 1, nq - 1)
            cur = [products(kb, h) for h in range(2)]
            nxt = tuple(products(after(kb), h) for h in range(2))
            for h in range(2):
                consume(kb, h, *cur[h], True)

            def rest(qb, held):
                new = tuple(products(after(qb), h) for h in range(2))
                for h in range(2):
                    consume(qb, h, *held[h], False)
                return new

            lax.fori_loop(kb + 1, nq, rest, nxt)
            return 0

        lax.fori_loop(0, nk, k_loop, 0)
        pl.when(step == lay.batch * pairs - 1)(exchange.finish)

    aug = pl.BlockSpec((lay.lp, 2 * AUG), lambda b, p: (b, p))
    rows = pl.BlockSpec((lay.lp, 2 * FOX_DIM), lambda b, p: (b, p))
    tr = pl.BlockSpec((2 * FOX_DIM, lay.lp), lambda b, p: (p, b))
    tr_k = pl.BlockSpec((2 * KT_ROWS, lay.lp), lambda b, p: (p, b))
    outs = pl.pallas_call(
        body, name="fox_bwd", grid=(lay.batch, pairs),
        in_specs=[aug, aug, rows, rows, tr_k, tr, tr, pl.BlockSpec((2, 1, lay.lp), lambda b, p: (b * pairs + p, 0, 0))]
        + [HBM_SPEC] * ne,
        out_specs=[tr_k, aug, rows] + [HBM_SPEC] * ne,
        out_shape=[jax.ShapeDtypeStruct((FOX_HEADS * KT_ROWS, lay.n), F32), jax.ShapeDtypeStruct((lay.n, FOX_HEADS * AUG), F32),
                   jax.ShapeDtypeStruct((lay.n, FOX_W), F32)] + [jax.ShapeDtypeStruct(p.shape, p.dtype) for p in parts],
        scratch_shapes=[pltpu.VMEM((2, 1, lay.lp), F32)] + _ChipExchange.semaphores(ne),
        compiler_params=_params(2),
    )(q_aug, k_aug, v, do, k_t, o_t, do_t, lse, *parts)
    return outs[0], outs[1], outs[2], outs[3:]


def _hgrn_fwd(proj, kk, gl, lay):
    t = lay.tile
    nt = lay.lp // t
    nsc = t // SUB

    def body(q_ref, k_ref, g_ref, v_ref, o_ref, st_ref, state, sub_rows):
        @pl.when(pl.program_id(1) == 0)
        def _():
            state[...] = jnp.zeros_like(state)

        rowi = lax.broadcasted_iota(jnp.int32, (SUB, 1), 0)

        def sub(sc, _):
            r0 = pl.multiple_of(sc * SUB, SUB)
            sub_rows[0] = k_ref[pl.ds(r0, SUB), :]
            sub_rows[1] = g_ref[pl.ds(r0, SUB), :]
            sub_rows[2] = v_ref[pl.ds(r0, SUB), :]
            for h in range(HG_HEADS):
                lanes = slice(h * HG_DIM, (h + 1) * HG_DIM)
                q16 = q_ref[pl.ds(r0, SUB), lanes]
                k16 = sub_rows[0, :, lanes]
                g16 = sub_rows[1, :, lanes]
                v16 = sub_rows[2, :, lanes]
                g_end = sub_rows[1, SUB - 1:SUB, lanes]
                s_prev = state[h]
                st_ref[sc, h] = s_prev
                o = lax.dot_general((q16 * jnp.exp(g16)).astype(MXU_DTYPE), s_prev.astype(MXU_DTYPE),
                                    (((1,), (1,)), ((), ())), preferred_element_type=F32)
                for s in range(SUB):
                    ks = sub_rows[0, s:s + 1, lanes]
                    gs = sub_rows[1, s:s + 1, lanes]
                    vs = sub_rows[2, s:s + 1, lanes]
                    w = q16 * jnp.exp(jnp.minimum(g16 - gs, 0.0)) * ks
                    a = jnp.where(rowi >= s, jnp.sum(w, axis=1, keepdims=True), 0.0)
                    o = o + a * vs
                o_ref[pl.ds(r0, SUB), lanes] = o
                kt = k16 * jnp.exp(g_end - g16)
                upd = lax.dot_general(v16.astype(MXU_DTYPE), kt.astype(MXU_DTYPE), (((0,), (0,)), ((), ())),
                                      preferred_element_type=F32)
                state[h] = jnp.exp(g_end) * s_prev + upd
            return 0

        lax.fori_loop(0, nsc, sub, 0)

    rows = lambda col: pl.BlockSpec((t, HG_W), functools.partial(lambda b, i, col: (b * nt + i, col), col=col))
    return pl.pallas_call(
        body, name="hgrn_fwd", grid=(lay.batch, nt),
        in_specs=[rows(C_HQ), rows(0), rows(0), rows(C_HI)],
        out_specs=[rows(0), pl.BlockSpec((nsc, HG_HEADS, HG_DIM, HG_DIM), lambda b, i: (b * nt + i, 0, 0, 0))],
        out_shape=[jax.ShapeDtypeStruct((lay.n, HG_W), F32),
                   jax.ShapeDtypeStruct((lay.n // SUB, HG_HEADS, HG_DIM, HG_DIM), F32)],
        scratch_shapes=[pltpu.VMEM((HG_HEADS, HG_DIM, HG_DIM), F32), pltpu.VMEM((3, SUB, HG_W), F32)],
        compiler_params=_params(2),
    )(proj, kk, gl, proj)


def _hgrn_bwd(proj, kk, gl, do, states, lay):
    t = lay.tile
    nt = lay.lp // t
    nsc = t // SUB

    def body(q_ref, k_ref, g_ref, v_ref, do_ref, st_ref, dq_ref, dk_ref, dv_ref, dg_ref, dstate, sub_rows, row_acc):
        @pl.when(pl.program_id(1) == 0)
        def _():
            dstate[...] = jnp.zeros_like(dstate)

        rowi = lax.broadcasted_iota(jnp.int32, (SUB, 1), 0)

        def sub(it, _):
            sc = nsc - 1 - it
            r0 = pl.multiple_of(sc * SUB, SUB)
            sub_rows[0] = k_ref[pl.ds(r0, SUB), :]
            sub_rows[1] = g_ref[pl.ds(r0, SUB), :]
            sub_rows[2] = v_ref[pl.ds(r0, SUB), :]
            for h in range(HG_HEADS):
                lanes = slice(h * HG_DIM, (h + 1) * HG_DIM)
                q16 = q_ref[pl.ds(r0, SUB), lanes]
                k16 = sub_rows[0, :, lanes]
                g16 = sub_rows[1, :, lanes]
                v16 = sub_rows[2, :, lanes]
                do16 = do_ref[pl.ds(r0, SUB), lanes]
                g_end = sub_rows[1, SUB - 1:SUB, lanes]
                s_prev = st_ref[sc, h]
                ds_end = dstate[h]
                eg = jnp.exp(g16)
                ekt = jnp.exp(g_end - g16)
                e_end = jnp.exp(g_end)
                qt = q16 * eg
                kt = k16 * ekt
                ds_mx = ds_end.astype(MXU_DTYPE)
                dv = lax.dot_general(kt.astype(MXU_DTYPE), ds_mx, (((1,), (1,)), ((), ())), preferred_element_type=F32)
                dkt = jnp.dot(v16.astype(MXU_DTYPE), ds_mx, preferred_element_type=F32)
                dk = dkt * ekt
                ktdkt = kt * dkt
                dg_end = jnp.sum(ktdkt, axis=0, keepdims=True) + jnp.sum(s_prev * ds_end, axis=0, keepdims=True) * e_end
                dg = jnp.where(rowi == SUB - 1, dg_end, 0.0) - ktdkt
                dqt = jnp.dot(do16.astype(MXU_DTYPE), s_prev.astype(MXU_DTYPE), preferred_element_type=F32)
                dq = dqt * eg
                dg = dg + qt * dqt
                dstate[h] = e_end * ds_end + lax.dot_general(do16.astype(MXU_DTYPE), qt.astype(MXU_DTYPE),
                                                             (((0,), (0,)), ((), ())), preferred_element_type=F32)
                for s in range(SUB):
                    ks = sub_rows[0, s:s + 1, lanes]
                    gs = sub_rows[1, s:s + 1, lanes]
                    vs = sub_rows[2, s:s + 1, lanes]
                    live = rowi >= s
                    e = jnp.where(live, jnp.exp(jnp.minimum(g16 - gs, 0.0)), 0.0)
                    qe = q16 * e
                    a = jnp.sum(qe * ks, axis=1, keepdims=True)
                    da = jnp.where(live, jnp.sum(do16 * vs, axis=1, keepdims=True), 0.0)
                    t1 = da * qe
                    dk_row = jnp.sum(t1, axis=0, keepdims=True)
                    dq = dq + da * (e * ks)
                    dg = dg + t1 * ks
                    row_acc[0, s:s + 1, :] = jnp.sum(a * do16, axis=0, keepdims=True)
                    row_acc[1, s:s + 1, :] = dk_row
                    row_acc[2, s:s + 1, :] = ks * dk_row
                dq_ref[pl.ds(r0, SUB), lanes] = dq
                dk_ref[pl.ds(r0, SUB), lanes] = dk + row_acc[1]
                dv_ref[pl.ds(r0, SUB), lanes] = dv + row_acc[0]
                dg_ref[pl.ds(r0, SUB), lanes] = dg - row_acc[2]
            return 0

        lax.fori_loop(0, nsc, sub, 0)

    def rows(col):
        return pl.BlockSpec((t, HG_W), functools.partial(lambda b, i, col: (b * nt + nt - 1 - i, col), col=col))

    out = jax.ShapeDtypeStruct((lay.n, HG_W), F32)
    return pl.pallas_call(
        body, name="hgrn_bwd", grid=(lay.batch, nt),
        in_specs=[rows(C_HQ), rows(0), rows(0), rows(C_HI), rows(0),
                  pl.BlockSpec((nsc, HG_HEADS, HG_DIM, HG_DIM), lambda b, i: (b * nt + nt - 1 - i, 0, 0, 0))],
        out_specs=[rows(0)] * 4, out_shape=[out] * 4,
        scratch_shapes=[pltpu.VMEM((HG_HEADS, HG_DIM, HG_DIM), F32), pltpu.VMEM((3, SUB, HG_W), F32),
                        pltpu.VMEM((3, SUB, HG_DIM), F32)],
        compiler_params=_params(2),
    )(proj, kk, gl, proj, do, states)


CONV_COLS = 1408


def _shift_down(x, halo, tile, by):
    out = pltpu.roll(x, by, 0)
    rowi = lax.broadcasted_iota(jnp.int32, (8, 1), 0)
    top = out[0:8]
    for r in range(by):
        top = jnp.where(rowi == r, halo[8 - by + r:8 - by + r + 1, :], top)
    return jnp.concatenate([top, out[8:]], axis=0)


def _shift_up(x, halo, tile, by):
    out = pltpu.roll(x, tile - by, 0)
    rowi = lax.broadcasted_iota(jnp.int32, (8, 1), 0)
    bottom = out[tile - 8:]
    for r in range(by):
        bottom = jnp.where(rowi == 8 - by + r, halo[r:r + 1, :], bottom)
    return jnp.concatenate([out[:tile - 8], bottom], axis=0)


def _conv_specs(tile):
    ncb = D_FF // CONV_COLS
    per8 = tile // 8

    def tile_spec(off):
        return pl.BlockSpec((tile, CONV_COLS), functools.partial(lambda i, j, off: (i, j + off), off=off))

    def prev_spec(off):
        return pl.BlockSpec((8, CONV_COLS), functools.partial(lambda i, j, off: (jnp.maximum(i * per8 - 1, 0), j + off), off=off))

    def w_spec(off):
        return pl.BlockSpec((3, CONV_COLS), functools.partial(lambda i, j, off: (0, j + off), off=off))

    def b_spec(off):
        return pl.BlockSpec((1, CONV_COLS), functools.partial(lambda i, j, off: (0, j + off), off=off))

    return ncb, tile_spec, prev_spec, w_spec, b_spec


def _conv3(x, halo, w, b, tile):
    return w[0:1, :] * _shift_down(x, halo, tile, 2) + w[1:2, :] * _shift_down(x, halo, tile, 1) + w[2:3, :] * x + b


def _conv_act_fwd(u, conv_w, conv_b, lay):
    tile = lay.tile
    ncb, tile_spec, prev_spec, w_spec, b_spec = _conv_specs(tile)

    def body(ug, uv, pg, pv, wg, wv, bg, bv, o_ref):
        cg = _conv3(ug[...], pg, wg, bg[...], tile)
        cv = _conv3(uv[...], pv, wv, bv[...], tile)
        o_ref[...] = (_silu(cg) * cv).astype(o_ref.dtype)

    return pl.pallas_call(
        body, name="conv_act_fwd", grid=(lay.n // tile, ncb),
        in_specs=[tile_spec(0), tile_spec(ncb), prev_spec(0), prev_spec(ncb), w_spec(0), w_spec(ncb), b_spec(0), b_spec(ncb)],
        out_specs=pl.BlockSpec((tile, CONV_COLS), lambda i, j: (i, j)),
        out_shape=jax.ShapeDtypeStruct((lay.n, D_FF), MXU_DTYPE),
        compiler_params=_params(2),
    )(u, u, u, u, conv_w, conv_w, conv_b, conv_b)


def _conv_act_bwd(u, dact, conv_w, conv_b, lay):
    tile = lay.tile
    ncb, tile_spec, prev_spec, w_spec, b_spec = _conv_specs(tile)

    def body(ug, uv, pg, pv, wg, wv, bg, bv, da_ref, dg_ref, dv_ref, gwg, gwv, gbg, gbv):
        @pl.when(pl.program_id(1) == 0)
        def _():
            for r in (gwg, gwv, gbg, gbv):
                r[...] = jnp.zeros_like(r)

        xg, xv = ug[...], uv[...]
        cg = _conv3(xg, pg, wg, bg[...], tile)
        cv = _conv3(xv, pv, wv, bv[...], tile)
        da = da_ref[...].astype(F32)
        sg = _sigmoid(cg)
        dcv = da * (cg * sg)
        dcg = da * cv * (sg * (1.0 + cg * (1.0 - sg)))
        dg_ref[...] = dcg
        dv_ref[...] = dcv
        for x, halo, dc, gw, gb in ((xg, pg, dcg, gwg, gbg), (xv, pv, dcv, gwv, gbv)):
            gw[0, 0:1, :] += jnp.sum(dc * _shift_down(x, halo, tile, 2), axis=0, keepdims=True)
            gw[0, 1:2, :] += jnp.sum(dc * _shift_down(x, halo, tile, 1), axis=0, keepdims=True)
            gw[0, 2:3, :] += jnp.sum(dc * x, axis=0, keepdims=True)
            gb[0] += jnp.sum(dc, axis=0, keepdims=True)

    swap = lambda spec: pl.BlockSpec(spec.block_shape, functools.partial(lambda j, i, f: f(i, j), f=spec.index_map))
    col = lambda j, i: (i, j)
    red_w = pl.BlockSpec((1, 3, CONV_COLS), lambda j, i: (j, 0, 0))
    red_b = pl.BlockSpec((1, 1, CONV_COLS), lambda j, i: (j, 0, 0))
    outs = pl.pallas_call(
        body, name="conv_act_bwd", grid=(ncb, lay.n // tile),
        in_specs=[swap(s) for s in (tile_spec(0), tile_spec(ncb), prev_spec(0), prev_spec(ncb), w_spec(0), w_spec(ncb),
                                    b_spec(0), b_spec(ncb))] + [pl.BlockSpec((tile, CONV_COLS), col)],
        out_specs=[pl.BlockSpec((tile, CONV_COLS), col), pl.BlockSpec((tile, CONV_COLS), col), red_w, red_w, red_b, red_b],
        out_shape=[jax.ShapeDtypeStruct((lay.n, D_FF), F32), jax.ShapeDtypeStruct((lay.n, D_FF), F32),
                   jax.ShapeDtypeStruct((ncb, 3, CONV_COLS), F32), jax.ShapeDtypeStruct((ncb, 3, CONV_COLS), F32),
                   jax.ShapeDtypeStruct((ncb, 1, CONV_COLS), F32), jax.ShapeDtypeStruct((ncb, 1, CONV_COLS), F32)],
        compiler_params=_params(2),
    )(u, u, u, u, conv_w, conv_w, conv_b, conv_b, dact)
    dcg, dcv, gwg, gwv, gbg, gbv = outs
    unblock = lambda g: jnp.transpose(g, (1, 0, 2)).reshape(g.shape[1], D_FF)
    g_w = jnp.concatenate([unblock(gwg), unblock(gwv)], axis=1)
    g_b = jnp.concatenate([unblock(gbg), unblock(gbv)], axis=1)
    return dcg, dcv, g_w, g_b


def _conv_input_bwd(dcg, dcv, conv_w, lay):
    tile = lay.tile
    ncb = D_FF // CONV_COLS
    nblk8 = lay.n // 8
    per8 = tile // 8
    nxt = lambda i: jnp.minimum((i + 1) * per8, nblk8 - 1)

    def half(dc, off, into, name):
        def body(*refs):
            d, halo, w, o = refs[0], refs[1], refs[2], refs[-1]
            x = d[...]
            du = w[2:3, :] * x + w[1:2, :] * _shift_up(x, halo, tile, 1) + w[0:1, :] * _shift_up(x, halo, tile, 2)
            o[...] = jnp.where(lay.valid(pl.program_id(0), tile), du, 0.0).astype(o.dtype)

        in_specs = [pl.BlockSpec((tile, CONV_COLS), lambda i, j: (i, j)),
                    pl.BlockSpec((8, CONV_COLS), lambda i, j: (nxt(i), j)),
                    pl.BlockSpec((3, CONV_COLS), lambda i, j: (0, j + off))]
        args = [dc, dc, conv_w]
        if into is not None:
            in_specs.append(pl.BlockSpec(memory_space=pltpu.HBM))
            args.append(into)
        return pl.pallas_call(
            body, name=name, grid=(lay.n // tile, ncb), in_specs=in_specs,
            out_specs=pl.BlockSpec((tile, CONV_COLS), lambda i, j: (i, j + off)),
            out_shape=jax.ShapeDtypeStruct((lay.n, FF2), MXU_DTYPE),
            input_output_aliases={} if into is None else {3: 0},
            compiler_params=_params(2),
        )(*args)

    return half(dcv, ncb, half(dcg, 0, None, "conv_input_bwd_gate"), "conv_input_bwd_value")


def _loss_head(h1, mlp, target, lay):
    t, sub = 256, ROW0
    per = lay.lp // t
    nsub = t // sub
    nreal = lay.seq // sub

    def body(h_ref, m_ref, *rest):
        t_refs, (loss_ref, dy_ref, dyb_ref) = rest[:nsub], rest[nsub:]
        b, j = pl.program_id(0), pl.program_id(1)

        @pl.when((b == 0) & (j == 0))
        def _():
            loss_ref[...] = jnp.zeros_like(loss_ref)

        rows_ = j * t + lax.broadcasted_iota(jnp.int32, (t, 1), 0)
        real = (rows_ >= ROW0) & (rows_ < ROW0 + lay.seq)
        tgt_ = jnp.concatenate([r[...] for r in t_refs], axis=0)
        err = jnp.where(real, h_ref[...] + m_ref[...] - tgt_, 0.0)
        dy = err * (1.0 / D_MODEL)
        dy_ref[...] = dy
        dyb_ref[...] = dy.astype(dyb_ref.dtype)
        loss_ref[...] += 0.5 * jnp.sum(err * dy)

    rows = pl.BlockSpec((t, D_MODEL), lambda b, j: (b * per + j, 0))
    tgt = [pl.BlockSpec((sub, D_MODEL), functools.partial(
        lambda b, j, r: (b * nreal + jnp.clip(j * nsub + r - 1, 0, nreal - 1), 0), r=r)) for r in range(nsub)]
    return pl.pallas_call(
        body, name="loss_head", grid=(lay.batch, per),
        in_specs=[rows, rows] + tgt,
        out_specs=[pl.BlockSpec((8, LANES), lambda b, j: (0, 0)), rows, rows],
        out_shape=[jax.ShapeDtypeStruct((8, LANES), F32), jax.ShapeDtypeStruct((lay.n, D_MODEL), F32),
                   jax.ShapeDtypeStruct((lay.n, D_MODEL), MXU_DTYPE)],
        compiler_params=_params(2),
    )(h1, mlp, *([target] * nsub))


def _fox_prep(fq, fk, ff, gq, gk, bf, gmat, gmat_t, valid):
    q = _group_rms(fq, gq, gmat, gmat_t, FOX_DIM)
    k = _group_rms(fk, gk, gmat, gmat_t, FOX_DIM)
    logf = jnp.where(valid, _log_sigmoid(ff + bf), 0.0)
    return q, k, logf


def _hg_prep(hf, l0, l1):
    mx = jnp.maximum(l0, l1)
    e0, e1 = jnp.exp(l0 - mx), jnp.exp(l1 - mx)
    lb = e0 / (e0 + e1)
    lf = jnp.log(lb + (1.0 - lb) * _sigmoid(hf))
    kk = (1.0 - lb) * _sigmoid(-hf)
    return lf, kk


def _hg_post(o, hg, gain):
    return _head_rms(o, gain) * _silu(hg)


def _gate(ga, gb, ya, yb):
    return _sigmoid(ga) * ya + _sigmoid(gb) * yb


def _by_chip(g):
    return jnp.transpose(g.reshape(g.shape[0], N_CHIPS, g.shape[1] // N_CHIPS), (1, 0, 2))


def _from_chips(a):
    return jnp.transpose(a, (1, 0, 2)).reshape(a.shape[1], N_CHIPS * a.shape[2])


def _local_step(x, target, w, late_shards, c, mine, lay):
    n, tile = lay.n, lay.tile
    rw = functools.partial(_rowwise, n_rows=n, tile=tile)
    mx = lambda a: a.astype(MXU_DTYPE)

    w_in = w["w_in"]
    fq, fk, fv, ffw, hq, hf, hi, hg, ga, gb = jnp.split(w_in, list(np.cumsum([512, 512, 512, 8, 512, 512, 512, 512, 1024])), axis=1)
    w_main = mx(jnp.concatenate([ga, gb, fq, fk, fv, hq, hi, hf, hg], axis=1))
    w_ff = mx(jnp.pad(ffw, ((0, 0), (0, LANES - FOX_HEADS))))
    conv_w, conv_b = w["conv_w"].astype(F32), w["conv_b"].astype(F32)
    g1, g2 = w["norm1_gain"], w["norm2_gain"]
    gq, gk = jnp.tile(w["q_norm_gain"], (1, FOX_HEADS)), jnp.tile(w["k_norm_gain"], (1, FOX_HEADS))
    bf = jnp.pad(w["fox_b_f"], ((0, 0), (0, LANES - FOX_HEADS)))
    lb_logits, hg_gain = w["hg_lb_logits"], w["hg_out_gain"]
    gm64, gm64_t = _group_matrix(FOX_W, FOX_DIM)

    meta = jnp.broadcast_to(w["meta_tokens"].astype(F32)[None], (lay.batch, N_META, D_MODEL))
    h0 = jnp.concatenate([jnp.zeros((lay.batch, LEAD, D_MODEL), F32), meta, x,
                          jnp.zeros((lay.batch, lay.lp - LEAD - lay.l_real, D_MODEL), F32)], axis=1).reshape(n, D_MODEL)

    (xn,) = rw(lambda i, h, g: _rms(h, g), [h0], [g1], [(D_MODEL, MXU_DTYPE)], [], name="norm1")
    proj = _matmul(xn, w_main, name="proj_main")
    pff = _matmul(xn, w_ff, name="proj_ff")

    def fox_prep_fn(i, a, b_, v_, f_, gq_, gk_, bf_, m_, mt_):
        q_, k_, logf = _fox_prep(a, b_, f_, gq_, gk_, bf_, m_, mt_, lay.valid(i, tile))
        return q_, k_, v_, logf

    q, k, v, logf = rw(fox_prep_fn, [(proj, 512, C_FQ), (proj, 512, C_FK), (proj, 512, C_FV), pff], [gq, gk, bf, gm64, gm64_t],
                       [(512, MXU_DTYPE), (512, MXU_DTYPE), (512, MXU_DTYPE), (LANES, F32)], [], name="fox_prep")
    cum = _cumsum_rows(logf, lay, reverse=False, name="fox_cum")
    e1, e2, aug_ones = _aug_matrices()
    q_aug, k_aug = rw(lambda i, q_, k_, c_, e1_, e2_, on_: _fox_augment(q_, k_, c_, lay.valid(i, tile), e1_, e2_, on_),
                      [q, k, cum], [e1, e2, aug_ones], [(FOX_HEADS * AUG, MXU_DTYPE)] * 2, [], name="fox_aug")
    o_t, lse, late_slabs = _fox_fwd_t(q_aug, k_aug, v.T, late_shards, lay)
    w_a, w_b, w_out, w_up, w_down = [mx(a) for a in _assemble(LATE, late_shards, late_slabs)]

    def hg_prep_fn(i, hf_, l0, l1):
        lf, kk_ = _hg_prep(hf_, l0, l1)
        return kk_, _group_cumsum(lf, tile, reverse=False)

    lb0, lb1 = lb_logits[0:1], lb_logits[1:2]
    kk, gl = rw(hg_prep_fn, [(proj, 512, C_HF)], [lb0, lb1], [(512, F32), (512, F32)], [], name="hg_prep")
    o_hg, states = _hgrn_fwd(proj, kk, gl, lay)
    (oh,) = rw(lambda i, o, g_, gain: _hg_post(o, g_, gain), [o_hg, (proj, 512, C_HG)], [hg_gain], [(512, MXU_DTYPE)], [],
               name="hg_post")
    ya = _matmul(oh, w_a, out_dtype=MXU_DTYPE, name="branch_a")
    yb = _matmul(o_t, w_b, trans_a=True, out_dtype=MXU_DTYPE, name="branch_b")
    pga, pgb = (proj, 1024, C_GA), (proj, 1024, C_GB)
    gate_fn = lambda a, b_, c_, d_: _gate(a, b_, c_.astype(F32), d_.astype(F32))
    (merged,) = rw(lambda i, a, b_, c_, d_: gate_fn(a, b_, c_, d_), [pga, pgb, ya, yb], [], [(D_MODEL, MXU_DTYPE)], [], name="gate")
    mo = _matmul(merged, w_out, name="out_proj")
    h1, hn = rw(lambda i, h, m_, g: (h + m_, _rms(h + m_, g)), [h0, mo], [g2], [(D_MODEL, F32), (D_MODEL, MXU_DTYPE)], [],
                name="norm2")
    u = _matmul(hn, w_up, name="up_proj")
    act = _conv_act_fwd(u, conv_w, conv_b, lay)
    mlp = _matmul(act, w_down, name="down_proj")
    loss_blk, dy, dyb = _loss_head(h1, mlp, target.reshape(lay.batch * lay.seq, D_MODEL), lay)
    loss = loss_blk[0, 0]

    grads = {}
    dact = _matmul(dyb, w_down, trans_b=True, out_dtype=MXU_DTYPE, name="down_bwd_x")
    grads["w_down"] = _matmul(act, dyb, trans_a=True, name="down_bwd_w").reshape(N_CHIPS, D_FF // N_CHIPS, D_MODEL)
    dcg, dcv, grads["conv_w"], grads["conv_b"] = _conv_act_bwd(u, dact, conv_w, conv_b, lay)
    du = _conv_input_bwd(dcg, dcv, conv_w, lay)
    dhn = _matmul(du, w_up, trans_b=True, name="up_bwd_x")
    grads["w_up"] = _matmul(hn, du, trans_a=True, by_chip=True, name="up_bwd_w")

    def norm2_bwd(i, h, d_, dy_, g):
        _, vjp = jax.vjp(_rms, h, g)
        dh, dg = vjp(d_)
        return dh + dy_, dh + dy_, dg

    dh1, dh1b, grads["norm2_gain"] = rw(norm2_bwd, [h1, dhn, dy], [g2], [(D_MODEL, F32), (D_MODEL, MXU_DTYPE)], [(1, D_MODEL)],
                                        name="norm2_bwd")
    dmerged = _matmul(dh1b, w_out, trans_b=True, out_dtype=MXU_DTYPE, name="out_bwd_x")
    grads["w_out"] = _matmul(merged, dh1b, trans_a=True, name="out_bwd_w").reshape(N_CHIPS, D_MODEL // N_CHIPS, D_MODEL)

    def gate_bwd(i, a, b_, c_, d_, dm):
        _, vjp = jax.vjp(gate_fn, a, b_, c_, d_)
        da, db, dc, dd = vjp(dm.astype(F32))
        return jnp.concatenate([da, db], axis=1), dc, dd

    dproj, dya, dyb_ = rw(gate_bwd, [pga, pgb, ya, yb, dmerged], [], [(2 * D_MODEL, MXU_DTYPE)] + [(D_MODEL, MXU_DTYPE)] * 2, [],
                          name="gate_bwd", into=(MAIN_COLS, 0))
    doh = _matmul(dya, w_a, trans_b=True, out_dtype=MXU_DTYPE, name="branch_a_bwd_x")
    grads["w_branch_a"] = _matmul(oh, dya, trans_a=True, by_chip=True, name="branch_a_bwd_w")
    dofox = _matmul(dyb_, w_b, trans_b=True, out_dtype=MXU_DTYPE, name="branch_b_bwd_x")
    grads["w_branch_b"] = _matmul(o_t, dyb_, by_chip=True, name="branch_b_bwd_w")

    def hg_post_bwd(i, o, g_, d_, gain):
        _, vjp = jax.vjp(_hg_post, o, g_, gain)
        do_, dg_, dgain = vjp(d_.astype(F32))
        return dg_, do_, dgain

    dproj, do_hg, grads["hg_out_gain"] = rw(hg_post_bwd, [o_hg, (proj, 512, C_HG), doh], [hg_gain],
                                            [(512, MXU_DTYPE), (512, F32)], [(1, HG_DIM)], name="hg_post_bwd", into=(dproj, C_HG))
    dhq, dkk, dhi, dgl = _hgrn_bwd(proj, kk, gl, do_hg, states, lay)

    def hg_prep_bwd(i, hf_, dkk_, dgl_, l0, l1):
        _, vjp = jax.vjp(_hg_prep, hf_, l0, l1)
        return vjp((_group_cumsum(dgl_, tile, reverse=True), dkk_))

    dproj, g_lb0, g_lb1 = rw(hg_prep_bwd, [(proj, 512, C_HF), dkk, dgl], [lb0, lb1], [(512, MXU_DTYPE)], [(1, HG_W), (1, HG_W)],
                             name="hg_prep_bwd", into=(dproj, C_HF))
    grads["hg_lb_logits"] = jnp.concatenate([g_lb0, g_lb1], axis=0)

    k_t = (k.astype(F32) * FOX_SCALE).astype(MXU_DTYPE).T.reshape(FOX_HEADS, FOX_DIM, n)
    k_t = jnp.concatenate([k_t, jnp.ones((FOX_HEADS, KT_ROWS - FOX_DIM, n), MXU_DTYPE)], axis=1).reshape(FOX_HEADS * KT_ROWS, n)
    late_grads = [grads.pop(n_) for n_ in LATE]
    parts = [_add_own_half(g_, s_, c, BF16, name=f"reduce_add2_{n_}")
             for n_, g_, s_ in zip(LATE, late_grads, _sibling_exchange(late_grads, "reduce_sibling_late"))]
    dq_t, dk_aug, dv, from_chips = _fox_bwd_t(q_aug, k_aug, v, dofox, k_t, o_t, dofox.T, lse, parts, lay)
    reduced = [_add_chips(p_, g_, mine, name=f"reduce_add4_{n_}") for n_, p_, g_ in zip(LATE, parts, from_chips)]
    dq_t = dq_t.reshape(FOX_HEADS, KT_ROWS, n)
    dq = dq_t[:, :FOX_DIM].reshape(FOX_W, n).T
    dk_aug = dk_aug.reshape(n, FOX_HEADS, AUG)
    dk = dk_aug[:, :, :FOX_DIM].reshape(n, FOX_W)
    dcum = jnp.pad(dq_t[:, FOX_DIM].T - dk_aug[:, :, FOX_DIM], ((0, 0), (0, LANES - FOX_HEADS)))
    dlogf = _cumsum_rows(dcum, lay, reverse=True, name="fox_cum_bwd")

    def fox_prep_bwd(i, a, b_, f_, dq_, dk_, dl_, gq_, gk_, bf_, m_, mt_):
        valid = lay.valid(i, tile)
        _, vjp = jax.vjp(lambda a_, b__, f__, gq__, gk__, bf__: _fox_prep(a_, b__, f__, gq__, gk__, bf__, m_, mt_, valid),
                         a, b_, f_, gq_, gk_, bf_)
        da, db, df, dgq, dgk, dbf = vjp((dq_, dk_, dl_))
        return jnp.concatenate([da, db], axis=1), df, dgq, dgk, dbf

    dproj, dff, g_gq, g_gk, g_bf = rw(
        fox_prep_bwd, [(proj, 512, C_FQ), (proj, 512, C_FK), pff, dq, dk, dlogf], [gq, gk, bf, gm64, gm64_t],
        [(2 * FOX_W, MXU_DTYPE), (LANES, MXU_DTYPE)], [(1, FOX_W), (1, FOX_W), (1, LANES)], name="fox_prep_bwd",
        into=(dproj, C_FQ // 2))
    grads["q_norm_gain"] = g_gq.reshape(FOX_HEADS, FOX_DIM).sum(0, keepdims=True)
    grads["k_norm_gain"] = g_gk.reshape(FOX_HEADS, FOX_DIM).sum(0, keepdims=True)
    grads["fox_b_f"] = g_bf[:, :FOX_HEADS]

    (dproj,) = rw(lambda i, a, b_, c_: jnp.concatenate([a, b_, c_], axis=1), [dv, dhq, dhi], [], [(3 * 512, MXU_DTYPE)], [],
                  name="dproj_cast", into=(dproj, C_FV // 3))
    dxn = _matmul(dproj, w_main, trans_b=True, name="proj_bwd_x")
    dxn_ff = _matmul(dff, w_ff, trans_b=True, name="proj_ff_bwd_x")
    g_main = _matmul(xn, dproj, trans_a=True, name="proj_bwd_w")
    g_ff = _matmul(xn, dff, trans_a=True, name="proj_ff_bwd_w")[:, :FOX_HEADS]
    p = jnp.split(g_main, list(np.cumsum([1024, 1024] + [512] * 6)), axis=1)
    grads["w_in"] = _by_chip(jnp.concatenate([p[2], p[3], p[4], g_ff, p[5], p[7], p[6], p[8], p[0], p[1]], axis=1))

    per = lay.lp // tile

    def norm1_bwd(i, h, d1, d2, dh1_, g):
        _, vjp = jax.vjp(_rms, h, g)
        dh, dg = vjp(d1 + d2)
        dh = dh + dh1_
        dmeta = jnp.where(lax.rem(i, per) == 0, dh[LEAD:LEAD + N_META, :], 0.0)
        return dh, dg, dmeta

    dh0, grads["norm1_gain"], grads["meta_tokens"] = rw(norm1_bwd, [h0, dxn, dxn_ff, dh1], [g1], [(D_MODEL, F32)],
                                                       [(1, D_MODEL), (N_META, D_MODEL)], name="norm1_bwd")
    grad_x = dh0.reshape(lay.batch, lay.lp, D_MODEL)[:, ROW0:ROW0 + lay.seq]
    return loss, grad_x, grads, reduced


MESH = pl.DeviceIdType.MESH
HBM_SPEC = pl.BlockSpec(memory_space=pltpu.HBM)
WEIGHT_NAMES = ["meta_tokens", "norm1_gain", "w_in", "fox_b_f", "q_norm_gain", "k_norm_gain", "hg_lb_logits", "hg_out_gain",
                "w_branch_a", "w_branch_b", "w_out", "norm2_gain", "w_up", "conv_w", "conv_b", "w_down"]
BIG = ("w_in", "w_branch_a", "w_branch_b", "w_out", "w_up", "w_down")
BIG_COL_SHARDED = ("w_in", "w_branch_a", "w_branch_b", "w_up")
LATE = BIG[1:]
SMALL = tuple(n for n in WEIGHT_NAMES if n not in BIG)
SMALL_SHARDED = ("meta_tokens", "conv_w")
SMALL_ROWS = 144
GATHER_SMALL_ROWS = 80


def _position():
    return lax.axis_index("x"), lax.axis_index("y"), lax.axis_index("c")


def _other_chips(x, y):
    return [(1 - x, y), (x, 1 - y), (1 - x, 1 - y)]


def _scalar(v):
    return jnp.reshape(v, (1,)).astype(jnp.int32)


def _row_tile(rows, cols):
    width = -(-cols // LANES) * LANES * 4
    best = 8
    for d in range(8, rows + 1, 8):
        if rows % d == 0 and d * width <= (1 << 20):
            best = d
    return best


class _Gather:
    def __init__(self, xs, outs, send_sems, recv_sems):
        self.xs, self.outs, self.send_sems, self.recv_sems = xs, outs, send_sems, recv_sems
        self.na = len(xs)
        self.x, self.y, self.c = _position()
        self.me, self.sibling = (self.x, self.y, self.c), (self.x, self.y, 1 - self.c)
        self.chips = _other_chips(self.x, self.y)

    def _copy(self, a, k, block, to, own=False):
        dst = self.outs[a].at[4 * block[0] + 2 * block[1] + block[2]]
        src = dst
        if own:
            half = self.xs[a].shape[0] // 2
            src = self.xs[a].at[pl.ds(pl.multiple_of(self.c * half, 8), half), :]
        return pltpu.make_async_remote_copy(src_ref=src, dst_ref=dst, send_sem=self.send_sems.at[a, k],
                                            recv_sem=self.recv_sems.at[a, k], device_id=to, device_id_type=MESH)

    def _firsts(self):
        return [self._copy(a, j, self.me, (*chip, self.c), own=True) for a in range(self.na) for j, chip in enumerate(self.chips)]

    def _relays(self):
        return [self._copy(a, 3 + j, (*chip, self.c), self.sibling) for j, chip in enumerate(self.chips) for a in range(self.na)]

    def start(self):
        for cp in self._firsts():
            cp.start()

    def relay(self):
        for j, chip in enumerate(self.chips):
            for a in range(self.na):
                self._copy(a, j, (*chip, self.c), self.me).wait_recv()
                self._copy(a, 3 + j, (*chip, self.c), self.sibling).start()

    def finish(self):
        for a in range(self.na):
            for j, chip in enumerate(self.chips):
                self._copy(a, 3 + j, (*chip, 1 - self.c), self.me).wait_recv()
        for cp in self._firsts() + self._relays():
            cp.wait_send()

    @staticmethod
    def out_shapes(shards):
        return [jax.ShapeDtypeStruct((8, s.shape[0] // 2, s.shape[1]), s.dtype) for s in shards]

    @staticmethod
    def semaphores(na):
        return [pltpu.SemaphoreType.DMA((na, 6)), pltpu.SemaphoreType.DMA((na, 6))]


def _gather_shards(shards):
    na = len(shards)

    def body(*refs):
        g = _Gather(refs[:na], refs[na:2 * na], refs[2 * na], refs[2 * na + 1])
        g.start()
        g.relay()
        g.finish()

    return pl.pallas_call(
        body, name="gather_weights", out_shape=_Gather.out_shapes(shards),
        in_specs=[HBM_SPEC] * na, out_specs=[HBM_SPEC] * na, scratch_shapes=_Gather.semaphores(na),
    )(*shards)


def _sibling_exchange(gs, name):
    na = len(gs)
    halves = [g.shape[1] // 2 for g in gs]

    def body(*refs):
        srcs, gots, send_sems, recv_sems = refs[:na], refs[na:2 * na], refs[2 * na], refs[2 * na + 1]
        x, y, c = _position()
        copies = [pltpu.make_async_remote_copy(
            src_ref=srcs[a].at[:, pl.ds(pl.multiple_of((1 - c) * halves[a], 8), halves[a]), :], dst_ref=gots[a],
            send_sem=send_sems.at[a], recv_sem=recv_sems.at[a], device_id=(x, y, 1 - c), device_id_type=MESH) for a in range(na)]
        for cp in copies:
            cp.start()
        for cp in copies:
            cp.wait()

    return pl.pallas_call(
        body, name=name,
        out_shape=[jax.ShapeDtypeStruct((N_CHIPS, h, g.shape[2]), g.dtype) for h, g in zip(halves, gs)],
        in_specs=[HBM_SPEC] * na, out_specs=[HBM_SPEC] * na,
        scratch_shapes=[pltpu.SemaphoreType.DMA((na,)), pltpu.SemaphoreType.DMA((na,))],
    )(*gs)


class _ChipExchange:
    def __init__(self, srcs, gots, send_sems, recv_sems):
        self.srcs, self.gots, self.send_sems, self.recv_sems = srcs, gots, send_sems, recv_sems
        self.na = len(srcs)
        x, y, self.c = _position()
        self.mine = 2 * x + y
        self.chips = _other_chips(x, y)

    def _copy(self, a, j, arriving):
        cx, cy = self.chips[j]
        theirs = 2 * cx + cy
        src = self.srcs[a].at[self.mine if arriving else theirs]
        dst = self.gots[a].at[theirs if arriving else self.mine]
        return pltpu.make_async_remote_copy(src_ref=src, dst_ref=dst, send_sem=self.send_sems.at[a, j],
                                            recv_sem=self.recv_sems.at[a, j], device_id=(cx, cy, self.c), device_id_type=MESH)

    def start(self):
        for a in range(self.na):
            for j in range(3):
                self._copy(a, j, False).start()

    def finish(self):
        for a in range(self.na):
            for j in range(3):
                self._copy(a, j, True).wait_recv()
        for a in range(self.na):
            for j in range(3):
                self._copy(a, j, False).wait_send()

    @staticmethod
    def semaphores(na):
        return [pltpu.SemaphoreType.DMA((na, 3)), pltpu.SemaphoreType.DMA((na, 3))]


def _chip_exchange(parts):
    na = len(parts)

    def body(*refs):
        ex = _ChipExchange(refs[:na], refs[na:2 * na], refs[2 * na], refs[2 * na + 1])
        ex.start()
        ex.finish()

    return pl.pallas_call(
        body, name="reduce_chips", out_shape=[jax.ShapeDtypeStruct(p.shape, p.dtype) for p in parts],
        in_specs=[HBM_SPEC] * na, out_specs=[HBM_SPEC] * na, scratch_shapes=_ChipExchange.semaphores(na),
    )(*parts)


def _sibling_send(halves):
    na = len(halves)

    def body(*refs):
        srcs, gots, send_sems, recv_sems = refs[:na], refs[na:2 * na], refs[2 * na], refs[2 * na + 1]
        x, y, c = _position()
        copies = [pltpu.make_async_remote_copy(src_ref=srcs[a], dst_ref=gots[a], send_sem=send_sems.at[a], recv_sem=recv_sems.at[a],
                                               device_id=(x, y, 1 - c), device_id_type=MESH) for a in range(na)]
        for cp in copies:
            cp.start()
        for cp in copies:
            cp.wait()

    return pl.pallas_call(
        body, name="reduce_gather", out_shape=[jax.ShapeDtypeStruct(h.shape, h.dtype) for h in halves],
        in_specs=[HBM_SPEC] * na, out_specs=[HBM_SPEC] * na,
        scratch_shapes=[pltpu.SemaphoreType.DMA((na,)), pltpu.SemaphoreType.DMA((na,))],
    )(*halves)


def _add_own_half(g, got, c, dtype, name):
    _, r, cols = g.shape
    r2 = r // 2
    tr = _row_tile(r2, cols)
    nrt = r2 // tr

    def body(c_ref, g_ref, got_ref, o_ref):
        o_ref[...] = (g_ref[...] + got_ref[...]).astype(o_ref.dtype)

    blk = (1, tr, cols)
    return pl.pallas_call(
        body, name=name,
        grid_spec=pltpu.PrefetchScalarGridSpec(
            num_scalar_prefetch=1, grid=(N_CHIPS, nrt),
            in_specs=[pl.BlockSpec(blk, lambda j, i, c_: (j, c_[0] * nrt + i, 0)), pl.BlockSpec(blk, lambda j, i, c_: (j, i, 0))],
            out_specs=pl.BlockSpec(blk, lambda j, i, c_: (j, i, 0))),
        out_shape=jax.ShapeDtypeStruct((N_CHIPS, r2, cols), dtype), compiler_params=_params(2),
    )(c, g, got)


def _add_chips(part, got, mine, name):
    _, r2, cols = part.shape
    tr = _row_tile(r2, cols)

    def body(m_ref, p_ref, g0, g1, g2, g3, o_ref):
        t = [jnp.where(m_ref[0] == k, p_ref[0], g[0]).astype(F32) for k, g in enumerate((g0, g1, g2, g3))]
        o_ref[...] = ((t[0] + t[1]) + t[2]) + t[3]

    blk = (1, tr, cols)
    others = [pl.BlockSpec(blk, functools.partial(lambda i, m, k: (jnp.where(m[0] == k, (k + 1) % N_CHIPS, k), i, 0), k=k))
              for k in range(N_CHIPS)]
    return pl.pallas_call(
        body, name=name,
        grid_spec=pltpu.PrefetchScalarGridSpec(
            num_scalar_prefetch=1, grid=(r2 // tr,),
            in_specs=[pl.BlockSpec(blk, lambda i, m: (m[0], i, 0))] + others,
            out_specs=pl.BlockSpec((tr, cols), lambda i, m: (i, 0))),
        out_shape=jax.ShapeDtypeStruct((r2, cols), F32), compiler_params=_params(1),
    )(mine, part, got, got, got, got)


def _adamw(w, own, other, m, v, c, name):
    r, cols = w.shape
    r2 = r // 2
    tr = _row_tile(r2, cols)
    nrt = r2 // tr
    c1 = 1.0 / (1.0 - ADAM_B1 ** ADAM_STEP)
    c2 = 1.0 / (1.0 - ADAM_B2 ** ADAM_STEP)

    def body(c_ref, w_ref, own_ref, other_ref, m_ref, v_ref, g_out, d_out, m_out, v_out):
        g_ = jnp.where(pl.program_id(0) == c_ref[0], own_ref[...], other_ref[...])
        m_new = ADAM_B1 * m_ref[...] + (1.0 - ADAM_B1) * g_
        v_new = ADAM_B2 * v_ref[...] + (1.0 - ADAM_B2) * (g_ * g_)
        g_out[...] = g_
        d_out[...] = -ADAM_LR * ((m_new * c1) / (jnp.sqrt(v_new * c2) + ADAM_EPS) + ADAM_WD * w_ref[...])
        m_out[...] = m_new
        v_out[...] = v_new

    full = pl.BlockSpec((tr, cols), lambda h, i, c_: (h * nrt + i, 0))
    half = pl.BlockSpec((tr, cols), lambda h, i, c_: (i, 0))
    out = jax.ShapeDtypeStruct((r, cols), F32)
    return pl.pallas_call(
        body, name=name,
        grid_spec=pltpu.PrefetchScalarGridSpec(num_scalar_prefetch=1, grid=(2, nrt), in_specs=[full, half, half, full, full],
                                               out_specs=[full] * 4),
        out_shape=[out] * 4, compiler_params=_params(2),
    )(c, w, own, other, m, v)


def _to_rows(flat, rows):
    return jnp.pad(flat, (0, rows * LANES - flat.shape[0])).reshape(rows, LANES)


def _pack_small(tree):
    return _to_rows(jnp.concatenate([tree[n].astype(F32).reshape(-1) for n in SMALL]), SMALL_ROWS)


def _unpack_small(packed, shapes):
    flat, out, at = packed.reshape(-1), {}, 0
    for n in SMALL:
        size = int(np.prod(shapes[n]))
        out[n] = flat[at:at + size].reshape(shapes[n])
        at += size
    return out


def _pack_small_by_chip(grads):
    pieces = []
    for n in SMALL:
        g = grads[n].astype(F32)
        if n in SMALL_SHARDED:
            pieces.append(_by_chip(g).reshape(N_CHIPS, -1))
        else:
            pieces.append(jnp.broadcast_to(g.reshape(1, -1), (N_CHIPS, g.size)))
    flat = jnp.concatenate(pieces, axis=1)
    return jnp.pad(flat, ((0, 0), (0, SMALL_ROWS * LANES - flat.shape[1]))).reshape(N_CHIPS, SMALL_ROWS, LANES)


def _bf16_shard(local, n):
    return local[n].reshape(local[n].shape[-2:]).astype(BF16)


def _all_chips(shard, slabs):
    x, y, _ = _position()
    is_mine = (lax.broadcasted_iota(jnp.int32, (N_CHIPS, 1, 1), 0) == 2 * x + y)
    return jnp.where(is_mine, shard[None], slabs.reshape((N_CHIPS,) + shard.shape))


def _assemble(names, shards, slabs):
    full = [_all_chips(s, g) for s, g in zip(shards, slabs)]
    return [_from_chips(f) if n in BIG_COL_SHARDED else f.reshape(N_CHIPS * f.shape[1], f.shape[2]) for n, f in zip(names, full)]


def _gather_first(local):
    shards = [_bf16_shard(local, "w_in"),
              _to_rows(jnp.concatenate([local[n].astype(F32).reshape(-1) for n in SMALL_SHARDED]), GATHER_SMALL_ROWS)]
    slabs = _gather_shards(shards)
    out = {"w_in": _assemble(["w_in"], shards[:1], slabs[:1])[0]}
    flat, at = _all_chips(shards[1], slabs[1]).reshape(N_CHIPS, -1), 0
    for n in SMALL_SHARDED:
        shape = local[n].shape[-2:]
        size = int(np.prod(shape))
        out[n] = _from_chips(flat[:, at:at + size].reshape((N_CHIPS,) + shape))
        at += size
    return out


def kernel(x, meta_tokens, norm1_gain, w_in, fox_b_f, q_norm_gain, k_norm_gain, hg_lb_logits, hg_out_gain, w_branch_a, w_branch_b, w_out, norm2_gain, w_up, conv_w, conv_b, w_down, loss_target, m_meta_tokens, m_norm1_gain, m_w_in, m_fox_b_f, m_q_norm_gain, m_k_norm_gain, m_hg_lb_logits, m_hg_out_gain, m_w_branch_a, m_w_branch_b, m_w_out, m_norm2_gain, m_w_up, m_conv_w, m_conv_b, m_w_down, v_meta_tokens, v_norm1_gain, v_w_in, v_fox_b_f, v_q_norm_gain, v_k_norm_gain, v_hg_lb_logits, v_hg_out_gain, v_w_branch_a, v_w_branch_b, v_w_out, v_norm2_gain, v_w_up, v_conv_w, v_conv_b, v_w_down):
    w_loc = dict(zip(WEIGHT_NAMES, (meta_tokens, norm1_gain, w_in, fox_b_f, q_norm_gain, k_norm_gain, hg_lb_logits, hg_out_gain,
                                    w_branch_a, w_branch_b, w_out, norm2_gain, w_up, conv_w, conv_b, w_down)))
    m_loc = dict(zip(WEIGHT_NAMES, (m_meta_tokens, m_norm1_gain, m_w_in, m_fox_b_f, m_q_norm_gain, m_k_norm_gain, m_hg_lb_logits,
                                    m_hg_out_gain, m_w_branch_a, m_w_branch_b, m_w_out, m_norm2_gain, m_w_up, m_conv_w, m_conv_b,
                                    m_w_down)))
    v_loc = dict(zip(WEIGHT_NAMES, (v_meta_tokens, v_norm1_gain, v_w_in, v_fox_b_f, v_q_norm_gain, v_k_norm_gain, v_hg_lb_logits,
                                    v_hg_out_gain, v_w_branch_a, v_w_branch_b, v_w_out, v_norm2_gain, v_w_up, v_conv_w, v_conv_b,
                                    v_w_down)))
    local_shapes = {n: tuple(w_loc[n].shape) for n in WEIGHT_NAMES}
    px, py, pc = _position()
    c, mine = _scalar(pc), _scalar(2 * px + py)

    weights = {n: w_loc[n].reshape(w_loc[n].shape[-2:]) for n in SMALL if n not in SMALL_SHARDED}
    weights.update(_gather_first(w_loc))

    lay = _Layout(x.shape[0], x.shape[1])
    loss, grad_x, grads, reduced_late = _local_step(x, loss_target, weights, [_bf16_shard(w_loc, n) for n in LATE], c, mine, lay)
    loss = lax.psum(loss, ("x", "y", "c"))

    names = ["w_in", "small"]
    by_chip = [grads["w_in"], _pack_small_by_chip(grads)]
    from_sibling = _sibling_exchange(by_chip, "reduce_sibling")
    parts = [_add_own_half(g, s, c, F32 if n == "small" else BF16, name=f"reduce_add2_{n}")
             for n, g, s in zip(names, by_chip, from_sibling)]
    from_chips = _chip_exchange(parts)
    own = [_add_chips(p, g, mine, name=f"reduce_add4_{n}") for n, p, g in zip(names, parts, from_chips)]
    names = list(BIG) + ["small"]
    own = [own[0]] + reduced_late + [own[1]]
    other = _sibling_send(own)

    two_d = lambda t: [t[n].reshape(t[n].shape[-2:]) for n in BIG] + [_pack_small(t)]
    results = [_adamw(w_, o_, t_, m_, v_, c, name=f"adamw_{n}")
               for n, w_, o_, t_, m_, v_ in zip(names, two_d(w_loc), own, other, two_d(m_loc), two_d(v_loc))]
    outs = []
    for kind in range(4):
        tree = {n: results[i][kind].reshape(local_shapes[n]) for i, n in enumerate(BIG)}
        tree.update(_unpack_small(results[-1][kind], local_shapes))
        outs += [tree[n] for n in WEIGHT_NAMES]
    return (loss, grad_x, *outs)
```

```python
import functools

import jax
import jax.numpy as jnp
import numpy as np
from jax import lax
from jax.experimental import pallas as pl
from jax.experimental.pallas import tpu as pltpu

F32 = jnp.float32
BF16 = jnp.bfloat16
MXU_DTYPE = BF16
HIGHEST = lax.Precision.HIGHEST

D_MODEL = 1024
N_META = 16
LEAD = 48
ROW0 = LEAD + N_META
FOX_HEADS, FOX_DIM, FOX_W = 8, 64, 512
HG_HEADS, HG_DIM, HG_W = 4, 128, 512
D_FF = 2816
FF2 = 2 * D_FF
EPS = 1e-6
SUB = 16
LANES = 128
N_CHIPS = 4
NEG = -1e30

ADAM_LR, ADAM_B1, ADAM_B2, ADAM_EPS, ADAM_WD, ADAM_STEP = 0.001, 0.9, 0.999, 1e-08, 0.01, 10

VMEM_LIMIT = 56 * 1024 * 1024

C_GA, C_GB = 0, 1
C_FQ, C_FK, C_FV, C_HQ, C_HI, C_HF, C_HG = 4, 5, 6, 7, 8, 9, 10
MAIN_COLS = 11 * 512


def _params(n_axes=1):
    return pltpu.CompilerParams(dimension_semantics=("arbitrary",) * n_axes, vmem_limit_bytes=VMEM_LIMIT)


def _pick(n, cands):
    for c in cands:
        if n % c == 0:
            return c
    raise ValueError(f"no tile for {n} among {cands}")


def _rowwise(fn, rows, consts, outs, reds, *, n_rows, tile, name, into=None):
    assert n_rows % tile == 0
    rows = [r if isinstance(r, tuple) else (r, r.shape[1], 0) for r in rows]
    nr, nc, no = len(rows), len(consts), len(outs)
    aliased = into is not None and not isinstance(into[0], int)
    n_in = nr + nc + (1 if aliased else 0)

    def body(*refs):
        i = pl.program_id(0)
        ins = [r[...] for r in refs[:nr + nc]]
        res = fn(i, *ins)
        res = res if isinstance(res, (tuple, list)) else (res,)
        for ref, v in zip(refs[n_in:n_in + no], res[:no]):
            ref[...] = v.astype(ref.dtype)
        red_refs = refs[n_in + no:]
        if red_refs:
            @pl.when(i == 0)
            def _():
                for ref in red_refs:
                    ref[...] = jnp.zeros_like(ref)
            for ref, v in zip(red_refs, res[no:]):
                ref[...] += v.astype(F32)

    in_specs = [pl.BlockSpec((tile, w), functools.partial(lambda i, j: (i, j), j=j)) for (_, w, j) in rows]
    in_specs += [pl.BlockSpec(c.shape, functools.partial(lambda i, nd: (0,) * nd, nd=c.ndim)) for c in consts]
    out_specs = [pl.BlockSpec((tile, w), lambda i: (i, 0)) for (w, _) in outs]
    out_specs += [pl.BlockSpec(s, functools.partial(lambda i, nd: (0,) * nd, nd=len(s))) for s in reds]
    out_shape = [jax.ShapeDtypeStruct((n_rows, w), dt) for (w, dt) in outs]
    out_shape += [jax.ShapeDtypeStruct(s, F32) for s in reds]
    args = [r[0] for r in rows] + list(consts)
    aliases = {}
    if into is not None:
        out_specs[0] = pl.BlockSpec((tile, outs[0][0]), functools.partial(lambda i, j: (i, j), j=into[1]))
        if aliased:
            in_specs.append(pl.BlockSpec(memory_space=pltpu.HBM))
            args.append(into[0])
            aliases = {n_in - 1: 0}
            out_shape[0] = jax.ShapeDtypeStruct(into[0].shape, into[0].dtype)
        else:
            out_shape[0] = jax.ShapeDtypeStruct((n_rows, into[0]), outs[0][1])
    return pl.pallas_call(
        body, name=name, grid=(n_rows // tile,), in_specs=in_specs, out_specs=out_specs, out_shape=out_shape,
        input_output_aliases=aliases, compiler_params=_params(1),
    )(*args)


def _matmul(a, b, *, trans_a=False, trans_b=False, out_dtype=F32, by_chip=False, name):
    if trans_a:
        k, m = a.shape
    else:
        m, k = a.shape
    n = b.shape[0] if trans_b else b.shape[1]
    assert (b.shape[1] if trans_b else b.shape[0]) == k
    if trans_a:
        tm = _pick(m, (1408, 1024, 512, 256, 128))
        tk = _pick(k, (2176, 1088, 1024, 768, 512, 256))
    else:
        tm = _pick(m, (1088, 512, 256, 128))
        tk = k if k <= 1024 else _pick(k, (2176, 1408, 1024, 512))
    nk = k // tk
    wide = (2816,) if nk == 1 and not trans_a else ()
    tn = n // N_CHIPS if by_chip else _pick(n, wide + (1408, 1024, 512, 256, 128))
    dims = (((0 if trans_a else 1,), (1 if trans_b else 0,)), ((), ()))

    def body(a_ref, b_ref, o_ref, acc_ref):
        out = o_ref.at[0] if by_chip else o_ref
        part = lax.dot_general(a_ref[...], b_ref[...], dims, preferred_element_type=F32)
        if nk == 1:
            out[...] = part.astype(out.dtype)
        else:
            kk = pl.program_id(2)

            @pl.when(kk == 0)
            def _():
                acc_ref[...] = part

            @pl.when(kk > 0)
            def _():
                acc_ref[...] += part

            @pl.when(kk == nk - 1)
            def _():
                out[...] = acc_ref[...].astype(out.dtype)

    a_spec = pl.BlockSpec((tk, tm), lambda i, j, kk: (kk, i)) if trans_a else pl.BlockSpec((tm, tk), lambda i, j, kk: (i, kk))
    b_spec = pl.BlockSpec((tn, tk), lambda i, j, kk: (j, kk)) if trans_b else pl.BlockSpec((tk, tn), lambda i, j, kk: (kk, j))
    if by_chip:
        out_spec, out_shape = pl.BlockSpec((1, tm, tn), lambda i, j, kk: (j, i, 0)), (N_CHIPS, m, tn)
    else:
        out_spec, out_shape = pl.BlockSpec((tm, tn), lambda i, j, kk: (i, j)), (m, n)
    return pl.pallas_call(
        body, name=name, grid=(m // tm, n // tn, nk), in_specs=[a_spec, b_spec], out_specs=out_spec,
        out_shape=jax.ShapeDtypeStruct(out_shape, out_dtype),
        scratch_shapes=[pltpu.VMEM((tm, tn) if nk > 1 else (8, LANES), F32)],
        compiler_params=_params(3),
    )(a, b)


def _sigmoid(x):
    return 1.0 / (1.0 + jnp.exp(-x))


def _silu(x):
    return x * _sigmoid(x)


def _log_sigmoid(x):
    return jnp.minimum(x, 0.0) - jnp.log(1.0 + jnp.exp(-jnp.abs(x)))


def _rms(x, gain):
    return x * lax.rsqrt(jnp.mean(x * x, axis=-1, keepdims=True) + EPS) * gain


def _group_matrix(width, group):
    g = (np.arange(width)[:, None] // group == np.arange(LANES)[None, :]).astype(np.float32)
    return jnp.asarray(g, MXU_DTYPE), jnp.asarray(g.T.copy(), MXU_DTYPE)


def _split_dot(x, mat):
    dt = mat.dtype
    hi = x.astype(dt)
    r1 = x - hi.astype(F32)
    mid = r1.astype(dt)
    lo = (r1 - mid.astype(F32)).astype(dt)
    dot = lambda a: jnp.dot(a, mat, preferred_element_type=F32)
    return dot(hi) + dot(mid) + dot(lo)


@jax.custom_vjp
def _group_sum(x, gmat, gmat_t):
    return _split_dot(x, gmat)


@jax.custom_vjp
def _group_spread(s, gmat, gmat_t):
    return _split_dot(s, gmat_t)


_group_sum.defvjp(lambda x, g, gt: (_split_dot(x, g), (g, gt)),
                  lambda res, ct: (_group_spread(ct, *res), jnp.zeros_like(res[0]), jnp.zeros_like(res[1])))
_group_spread.defvjp(lambda s, g, gt: (_split_dot(s, gt), (g, gt)),
                     lambda res, ct: (_group_sum(ct, *res), jnp.zeros_like(res[0]), jnp.zeros_like(res[1])))


def _group_rms(x, gain, gmat, gmat_t, group):
    rstd = lax.rsqrt(_group_sum(x * x, gmat, gmat_t) * (1.0 / group) + EPS)
    return x * _group_spread(rstd, gmat, gmat_t) * gain


def _head_rms(x, gain):
    outs = []
    for h in range(x.shape[1] // LANES):
        xs = x[:, h * LANES:(h + 1) * LANES]
        outs.append(xs * lax.rsqrt(jnp.mean(xs * xs, axis=-1, keepdims=True) + EPS) * gain)
    return jnp.concatenate(outs, axis=1)


class _Layout:
    def __init__(self, batch, seq):
        self.batch, self.seq = batch, seq
        self.l_real = N_META + seq
        self.lp = -(-(LEAD + self.l_real) // 256) * 256
        self.n = batch * self.lp
        self.tile = _pick(self.lp, (512, 256))

    def valid(self, i, tile):
        per = self.lp // tile
        r = lax.rem(i, per) * tile + lax.broadcasted_iota(jnp.int32, (tile, 1), 0)
        return (r >= LEAD) & (r < LEAD + self.l_real)


def _cumsum_rows(x, lay, *, reverse, name):
    t = lay.tile
    nt = lay.lp // t
    c = x.shape[1]

    def body(x_ref, o_ref, carry):
        j = pl.program_id(1)

        @pl.when(j == 0)
        def _():
            carry[...] = jnp.zeros_like(carry)

        r = lax.broadcasted_iota(jnp.int32, (t, t), 0)
        q = lax.broadcasted_iota(jnp.int32, (t, t), 1)
        tri = jnp.where((q >= r) if reverse else (q <= r), 1.0, 0.0).astype(F32)
        xs = x_ref[...]
        out = jnp.dot(tri, xs, precision=HIGHEST, preferred_element_type=F32) + carry[0:1, :]
        o_ref[...] = out
        carry[...] = jnp.broadcast_to(carry[0:1, :] + jnp.sum(xs, axis=0, keepdims=True), carry.shape)

    def idx(b, j):
        return (b * nt + (nt - 1 - j if reverse else j), 0)

    return pl.pallas_call(
        body, name=name, grid=(lay.batch, nt),
        in_specs=[pl.BlockSpec((t, c), idx)], out_specs=pl.BlockSpec((t, c), idx),
        out_shape=jax.ShapeDtypeStruct(x.shape, F32),
        scratch_shapes=[pltpu.VMEM((8, c), F32)],
        compiler_params=_params(2),
    )(x)


def _group_cumsum(x, tile, *, reverse):
    r = lax.rem(lax.broadcasted_iota(jnp.int32, (tile, 1), 0), SUB)
    s = 1
    while s < SUB:
        if reverse:
            x = x + jnp.where(r < SUB - s, pltpu.roll(x, tile - s, 0), 0.0)
        else:
            x = x + jnp.where(r >= s, pltpu.roll(x, s, 0), 0.0)
        s *= 2
    return x


AUG = 128
FOX_BK = 256
FOX_BQ = 256
FOX_SCALE = FOX_DIM ** -0.5
KT_ROWS = FOX_DIM + 16


def _aug_matrices():
    e1 = np.zeros((FOX_W, FOX_HEADS * AUG), np.float32)
    e2 = np.zeros((LANES, FOX_HEADS * AUG), np.float32)
    ones = np.zeros((1, FOX_HEADS * AUG), np.float32)
    for h in range(FOX_HEADS):
        for d in range(FOX_DIM):
            e1[h * FOX_DIM + d, h * AUG + d] = 1.0
        for j in range(3):
            e2[j * FOX_HEADS + h, h * AUG + FOX_DIM + j] = 1.0
            ones[0, h * AUG + FOX_DIM + j] = 1.0
    return jnp.asarray(e1, MXU_DTYPE), jnp.asarray(e2, MXU_DTYPE), jnp.asarray(ones)


def _fox_augment(q, k, cum, key_ok, e1, e2, ones):
    dt = q.dtype
    c = jnp.where(key_ok, -cum, NEG)
    hi = c.astype(dt)
    r1 = c - hi.astype(F32)
    mid = r1.astype(dt)
    lo = (r1 - mid.astype(F32)).astype(dt)
    lane = lax.broadcasted_iota(jnp.int32, c.shape, 1)
    shift = lambda a, by: pltpu.roll(a.astype(F32), by, 1)
    parts = jnp.where(lane < FOX_HEADS, hi.astype(F32),
                      jnp.where(lane < 2 * FOX_HEADS, shift(mid, FOX_HEADS),
                                jnp.where(lane < 3 * FOX_HEADS, shift(lo, 2 * FOX_HEADS), 0.0))).astype(dt)
    qs = (q.astype(F32) * FOX_SCALE).astype(dt)
    q_aug = jnp.dot(qs, e1, preferred_element_type=F32) + ones
    k_aug = jnp.dot(k, e1, preferred_element_type=F32) + jnp.dot(parts, e2, preferred_element_type=F32)
    return q_aug.astype(dt), k_aug.astype(dt)


def _fox_tile(k_blk, q_blk, k0, q0, masked):
    st = lax.dot_general(k_blk, q_blk, (((1,), (1,)), ((), ())), preferred_element_type=F32)
    if masked:
        keys = k0 + lax.broadcasted_iota(jnp.int32, st.shape, 0)
        qs = q0 + lax.broadcasted_iota(jnp.int32, st.shape, 1)
        st = jnp.where(keys <= qs, st, NEG)
    return st


def _fox_fwd_t(q_aug, k_aug, v_t, shards, lay):
    bk, bq = FOX_BK, FOX_BQ
    nq = lay.lp // bq
    pairs = FOX_HEADS // 2
    ng = len(shards)
    steps = lay.batch * pairs

    def body(*refs):
        q_ref, k_ref, vt_ref = refs[:3]
        ot_ref, lse_ref = refs[3 + ng:5 + ng]
        zeros_ref = refs[5 + 2 * ng]
        gather = _Gather(refs[3:3 + ng], refs[5 + ng:5 + 2 * ng], refs[6 + 2 * ng], refs[7 + 2 * ng])
        step = pl.program_id(0) * pairs + pl.program_id(1)
        pl.when(step == 0)(gather.start)
        pl.when(step == steps // 2)(gather.relay)
        heads = [(slice(hh * AUG, (hh + 1) * AUG), slice(hh * FOX_DIM, (hh + 1) * FOX_DIM)) for hh in range(2)]
        zeros_ref[...] = jnp.zeros_like(zeros_ref)

        def q_loop(qb, _):
            q0 = pl.multiple_of(qb * bq, bq)
            q_blks = [q_ref[pl.ds(q0, bq), lanes] for lanes, _ in heads]

            def scores(kb, h):
                k0 = pl.multiple_of(kb * bk, bk)
                return _fox_tile(k_ref[pl.ds(k0, bk), heads[h][0]], q_blks[h], k0, q0, False)

            def consume(kb, h, state, masked):
                m, l, acc, pend, st = state
                k0 = pl.multiple_of(kb * bk, bk)
                if masked:
                    keys = k0 + lax.broadcasted_iota(jnp.int32, st.shape, 0)
                    qs_ = q0 + lax.broadcasted_iota(jnp.int32, st.shape, 1)
                    st = jnp.where(keys <= qs_, st, NEG)
                m_new = jnp.maximum(m, jnp.max(st, axis=0, keepdims=True))
                alpha = jnp.exp(m - m_new)
                p = jnp.exp(st - m_new)
                l = alpha * l + jnp.sum(p, axis=0, keepdims=True)
                acc = alpha * (acc + pend)
                pend = jnp.dot(vt_ref[heads[h][1], pl.ds(k0, bk)], p.astype(vt_ref.dtype), preferred_element_type=F32)
                return m_new, l, acc, pend

            def k_step(kb, states):
                nxt = [scores(kb + 1, h) for h in range(2)]
                return tuple(consume(kb, h, states[h], False) + (nxt[h],) for h in range(2))

            states = tuple((jnp.full((1, bq), NEG, F32), jnp.zeros((1, bq), F32), zeros_ref[...], zeros_ref[...], scores(0, h))
                           for h in range(2))
            states = lax.fori_loop(0, qb, k_step, states)
            qs = q0 + lax.broadcasted_iota(jnp.int32, (1, bq), 1)
            ok = (qs >= LEAD) & (qs < LEAD + lay.l_real)
            for hh in range(2):
                m, l, acc, pend = consume(qb, hh, states[hh], True)
                ot_ref[heads[hh][1], pl.ds(q0, bq)] = jnp.where(ok, (acc + pend) / l, 0.0).astype(ot_ref.dtype)
                lse_ref[hh, :, pl.ds(q0, bq)] = m + jnp.log(l)
            return 0

        lax.fori_loop(0, nq, q_loop, 0)
        pl.when(step == steps - 1)(gather.finish)

    aug = pl.BlockSpec((lay.lp, 2 * AUG), lambda b, p: (b, p))
    tr = pl.BlockSpec((2 * FOX_DIM, lay.lp), lambda b, p: (p, b))
    outs = pl.pallas_call(
        body, name="fox_fwd", grid=(lay.batch, pairs),
        in_specs=[aug, aug, tr] + [HBM_SPEC] * ng,
        out_specs=[tr, pl.BlockSpec((2, 1, lay.lp), lambda b, p: (b * pairs + p, 0, 0))] + [HBM_SPEC] * ng,
        out_shape=[jax.ShapeDtypeStruct((FOX_W, lay.n), MXU_DTYPE),
                   jax.ShapeDtypeStruct((lay.batch * FOX_HEADS, 1, lay.lp), F32)] + _Gather.out_shapes(shards),
        scratch_shapes=[pltpu.VMEM((FOX_DIM, bq), F32)] + _Gather.semaphores(ng),
        compiler_params=_params(2),
    )(q_aug, k_aug, v_t, *shards)
    return outs[0], outs[1], outs[2:]


def _fox_bwd_t(q_aug, k_aug, v, do, k_t, o_t, do_t, lse, parts, lay):
    bk, bq = FOX_BK, FOX_BQ
    nq, nk = lay.lp // bq, lay.lp // bk
    pairs = FOX_HEADS // 2
    ne = len(parts)

    def body(*refs):
        q_ref, k_ref, v_ref, do_ref, kt_ref, ot_ref, dot_ref, lse_ref = refs[:8]
        dqt_ref, dk_ref, dv_ref = refs[8 + ne:11 + ne]
        delta = refs[11 + 2 * ne]
        exchange = _ChipExchange(refs[8:8 + ne], refs[11 + ne:11 + 2 * ne], refs[12 + 2 * ne], refs[13 + 2 * ne])
        step = pl.program_id(0) * pairs + pl.program_id(1)
        pl.when(step == 0)(exchange.start)
        dqt_ref[...] = jnp.zeros_like(dqt_ref)
        dk_ref[...] = jnp.zeros_like(dk_ref)
        dv_ref[...] = jnp.zeros_like(dv_ref)
        heads = [(hh, slice(hh * AUG, (hh + 1) * AUG), slice(hh * FOX_DIM, (hh + 1) * FOX_DIM),
                  slice(hh * KT_ROWS, (hh + 1) * KT_ROWS)) for hh in range(2)]

        def delta_loop(qb, _):
            q0 = pl.multiple_of(qb * bq, bq)
            for hh, _, cols, _ in heads:
                prod = ot_ref[cols, pl.ds(q0, bq)].astype(F32) * dot_ref[cols, pl.ds(q0, bq)].astype(F32)
                delta[hh, :, pl.ds(q0, bq)] = jnp.sum(prod, axis=0, keepdims=True)
            return 0

        lax.fori_loop(0, nq, delta_loop, 0)

        def k_loop(kb, _):
            k0 = pl.multiple_of(kb * bk, bk)

            def products(qb, h):
                q0 = pl.multiple_of(qb * bq, bq)
                _, lanes, cols, _ = heads[h]
                st = _fox_tile(k_ref[pl.ds(k0, bk), lanes], q_ref[pl.ds(q0, bq), lanes], k0, q0, False)
                dpt = lax.dot_general(v_ref[pl.ds(k0, bk), cols], do_ref[pl.ds(q0, bq), cols], (((1,), (1,)), ((), ())),
                                      preferred_element_type=F32)
                return st, dpt

            def consume(qb, h, st, dpt, masked):
                q0 = pl.multiple_of(qb * bq, bq)
                hh, lanes, cols, trows = heads[h]
                if masked:
                    keys = k0 + lax.broadcasted_iota(jnp.int32, st.shape, 0)
                    qs = q0 + lax.broadcasted_iota(jnp.int32, st.shape, 1)
                    st = jnp.where(keys <= qs, st, NEG)
                q_blk = q_ref[pl.ds(q0, bq), lanes]
                do_blk = do_ref[pl.ds(q0, bq), cols]
                pt = jnp.exp(st - lse_ref[hh, :, pl.ds(q0, bq)])
                dst = (pt * (dpt - delta[hh, :, pl.ds(q0, bq)])).astype(q_blk.dtype)
                dv_ref[pl.ds(k0, bk), cols] += jnp.dot(pt.astype(do_blk.dtype), do_blk, preferred_element_type=F32)
                dk_ref[pl.ds(k0, bk), lanes] += jnp.dot(dst, q_blk, preferred_element_type=F32)
                dqt_ref[trows, pl.ds(q0, bq)] += jnp.dot(kt_ref[trows, pl.ds(k0, bk)], dst, preferred_element_type=F32)

            after = lambda qb: jnp.minimum(qb + 1, nq - 1)
            cur = [products(kb, h) for h in range(2)]
            nxt = tuple(products(after(kb), h) for h in range(2))
            for h in range(2):
                consume(kb, h, *cur[h], True)

            def rest(qb, held):
                new = tuple(products(after(qb), h) for h in range(2))
                for h in range(2):
                    consume(qb, h, *held[h], False)
                return new

            lax.fori_loop(kb + 1, nq, rest, nxt)
            return 0

        lax.fori_loop(0, nk, k_loop, 0)
        pl.when(step == lay.batch * pairs - 1)(exchange.finish)

    aug = pl.BlockSpec((lay.lp, 2 * AUG), lambda b, p: (b, p))
    rows = pl.BlockSpec((lay.lp, 2 * FOX_DIM), lambda b, p: (b, p))
    tr = pl.BlockSpec((2 * FOX_DIM, lay.lp), lambda b, p: (p, b))
    tr_k = pl.BlockSpec((2 * KT_ROWS, lay.lp), lambda b, p: (p, b))
    outs = pl.pallas_call(
        body, name="fox_bwd", grid=(lay.batch, pairs),
        in_specs=[aug, aug, rows, rows, tr_k, tr, tr, pl.BlockSpec((2, 1, lay.lp), lambda b, p: (b * pairs + p, 0, 0))]
        + [HBM_SPEC] * ne,
        out_specs=[tr_k, aug, rows] + [HBM_SPEC] * ne,
        out_shape=[jax.ShapeDtypeStruct((FOX_HEADS * KT_ROWS, lay.n), F32), jax.ShapeDtypeStruct((lay.n, FOX_HEADS * AUG), F32),
                   jax.ShapeDtypeStruct((lay.n, FOX_W), F32)] + [jax.ShapeDtypeStruct(p.shape, p.dtype) for p in parts],
        scratch_shapes=[pltpu.VMEM((2, 1, lay.lp), F32)] + _ChipExchange.semaphores(ne),
        compiler_params=_params(2),
    )(q_aug, k_aug, v, do, k_t, o_t, do_t, lse, *parts)
    return outs[0], outs[1], outs[2], outs[3:]


def _hgrn_fwd(proj, kk, gl, lay):
    t = lay.tile
    nt = lay.lp // t
    nsc = t // SUB

    def body(q_ref, k_ref, g_ref, v_ref, o_ref, st_ref, state, sub_rows):
        @pl.when(pl.program_id(1) == 0)
        def _():
            state[...] = jnp.zeros_like(state)

        rowi = lax.broadcasted_iota(jnp.int32, (SUB, 1), 0)

        def sub(sc, _):
            r0 = pl.multiple_of(sc * SUB, SUB)
            sub_rows[0] = k_ref[pl.ds(r0, SUB), :]
            sub_rows[1] = g_ref[pl.ds(r0, SUB), :]
            sub_rows[2] = v_ref[pl.ds(r0, SUB), :]
            for h in range(HG_HEADS):
                lanes = slice(h * HG_DIM, (h + 1) * HG_DIM)
                q16 = q_ref[pl.ds(r0, SUB), lanes]
                k16 = sub_rows[0, :, lanes]
                g16 = sub_rows[1, :, lanes]
                v16 = sub_rows[2, :, lanes]
                g_end = sub_rows[1, SUB - 1:SUB, lanes]
                s_prev = state[h]
                st_ref[sc, h] = s_prev
                o = lax.dot_general((q16 * jnp.exp(g16)).astype(MXU_DTYPE), s_prev.astype(MXU_DTYPE),
                                    (((1,), (1,)), ((), ())), preferred_element_type=F32)
                for s in range(SUB):
                    ks = sub_rows[0, s:s + 1, lanes]
                    gs = sub_rows[1, s:s + 1, lanes]
                    vs = sub_rows[2, s:s + 1, lanes]
                    w = q16 * jnp.exp(jnp.minimum(g16 - gs, 0.0)) * ks
                    a = jnp.where(rowi >= s, jnp.sum(w, axis=1, keepdims=True), 0.0)
                    o = o + a * vs
                o_ref[pl.ds(r0, SUB), lanes] = o
                kt = k16 * jnp.exp(g_end - g16)
                upd = lax.dot_general(v16.astype(MXU_DTYPE), kt.astype(MXU_DTYPE), (((0,), (0,)), ((), ())),
                                      preferred_element_type=F32)
                state[h] = jnp.exp(g_end) * s_prev + upd
            return 0

        lax.fori_loop(0, nsc, sub, 0)

    rows = lambda col: pl.BlockSpec((t, HG_W), functools.partial(lambda b, i, col: (b * nt + i, col), col=col))
    return pl.pallas_call(
        body, name="hgrn_fwd", grid=(lay.batch, nt),
        in_specs=[rows(C_HQ), rows(0), rows(0), rows(C_HI)],
        out_specs=[rows(0), pl.BlockSpec((nsc, HG_HEADS, HG_DIM, HG_DIM), lambda b, i: (b * nt + i, 0, 0, 0))],
        out_shape=[jax.ShapeDtypeStruct((lay.n, HG_W), F32),
                   jax.ShapeDtypeStruct((lay.n // SUB, HG_HEADS, HG_DIM, HG_DIM), F32)],
        scratch_shapes=[pltpu.VMEM((HG_HEADS, HG_DIM, HG_DIM), F32), pltpu.VMEM((3, SUB, HG_W), F32)],
        compiler_params=_params(2),
    )(proj, kk, gl, proj)


def _hgrn_bwd(proj, kk, gl, do, states, lay):
    t = lay.tile
    nt = lay.lp // t
    nsc = t // SUB

    def body(q_ref, k_ref, g_ref, v_ref, do_ref, st_ref, dq_ref, dk_ref, dv_ref, dg_ref, dstate, sub_rows, row_acc):
        @pl.when(pl.program_id(1) == 0)
        def _():
            dstate[...] = jnp.zeros_like(dstate)

        rowi = lax.broadcasted_iota(jnp.int32, (SUB, 1), 0)

        def sub(it, _):
            sc = nsc - 1 - it
            r0 = pl.multiple_of(sc * SUB, SUB)
            sub_rows[0] = k_ref[pl.ds(r0, SUB), :]
            sub_rows[1] = g_ref[pl.ds(r0, SUB), :]
            sub_rows[2] = v_ref[pl.ds(r0, SUB), :]
            for h in range(HG_HEADS):
                lanes = slice(h * HG_DIM, (h + 1) * HG_DIM)
                q16 = q_ref[pl.ds(r0, SUB), lanes]
                k16 = sub_rows[0, :, lanes]
                g16 = sub_rows[1, :, lanes]
                v16 = sub_rows[2, :, lanes]
                do16 = do_ref[pl.ds(r0, SUB), lanes]
                g_end = sub_rows[1, SUB - 1:SUB, lanes]
                s_prev = st_ref[sc, h]
                ds_end = dstate[h]
                eg = jnp.exp(g16)
                ekt = jnp.exp(g_end - g16)
                e_end = jnp.exp(g_end)
                qt = q16 * eg
                kt = k16 * ekt
                ds_mx = ds_end.astype(MXU_DTYPE)
                dv = lax.dot_general(kt.astype(MXU_DTYPE), ds_mx, (((1,), (1,)), ((), ())), preferred_element_type=F32)
                dkt = jnp.dot(v16.astype(MXU_DTYPE), ds_mx, preferred_element_type=F32)
                dk = dkt * ekt
                ktdkt = kt * dkt
                dg_end = jnp.sum(ktdkt, axis=0, keepdims=True) + jnp.sum(s_prev * ds_end, axis=0, keepdims=True) * e_end
                dg = jnp.where(rowi == SUB - 1, dg_end, 0.0) - ktdkt
                dqt = jnp.dot(do16.astype(MXU_DTYPE), s_prev.astype(MXU_DTYPE), preferred_element_type=F32)
                dq = dqt * eg
                dg = dg + qt * dqt
                dstate[h] = e_end * ds_end + lax.dot_general(do16.astype(MXU_DTYPE), qt.astype(MXU_DTYPE),
                                                             (((0,), (0,)), ((), ())), preferred_element_type=F32)
                for s in range(SUB):
                    ks = sub_rows[0, s:s + 1, lanes]
                    gs = sub_rows[1, s:s + 1, lanes]
                    vs = sub_rows[2, s:s + 1, lanes]
                    live = rowi >= s
                    e = jnp.where(live, jnp.exp(jnp.minimum(g16 - gs, 0.0)), 0.0)
                    qe = q16 * e
                    a = jnp.sum(qe * ks, axis=1, keepdims=True)
                    da = jnp.where(live, jnp.sum(do16 * vs, axis=1, keepdims=True), 0.0)
                    t1 = da * qe
                    dk_row = jnp.sum(t1, axis=0, keepdims=True)
                    dq = dq + da * (e * ks)
                    dg = dg + t1 * ks
                    row_acc[0, s:s + 1, :] = jnp.sum(a * do16, axis=0, keepdims=True)
                    row_acc[1, s:s + 1, :] = dk_row
                    row_acc[2, s:s + 1, :] = ks * dk_row
                dq_ref[pl.ds(r0, SUB), lanes] = dq
                dk_ref[pl.ds(r0, SUB), lanes] = dk + row_acc[1]
                dv_ref[pl.ds(r0, SUB), lanes] = dv + row_acc[0]
                dg_ref[pl.ds(r0, SUB), lanes] = dg - row_acc[2]
            return 0

        lax.fori_loop(0, nsc, sub, 0)

    def rows(col):
        return pl.BlockSpec((t, HG_W), functools.partial(lambda b, i, col: (b * nt + nt - 1 - i, col), col=col))

    out = jax.ShapeDtypeStruct((lay.n, HG_W), F32)
    return pl.pallas_call(
        body, name="hgrn_bwd", grid=(lay.batch, nt),
        in_specs=[rows(C_HQ), rows(0), rows(0), rows(C_HI), rows(0),
                  pl.BlockSpec((nsc, HG_HEADS, HG_DIM, HG_DIM), lambda b, i: (b * nt + nt - 1 - i, 0, 0, 0))],
        out_specs=[rows(0)] * 4, out_shape=[out] * 4,
        scratch_shapes=[pltpu.VMEM((HG_HEADS, HG_DIM, HG_DIM), F32), pltpu.VMEM((3, SUB, HG_W), F32),
                        pltpu.VMEM((3, SUB, HG_DIM), F32)],
        compiler_params=_params(2),
    )(proj, kk, gl, proj, do, states)


CONV_COLS = 1408


def _shift_down(x, halo, tile, by):
    out = pltpu.roll(x, by, 0)
    rowi = lax.broadcasted_iota(jnp.int32, (8, 1), 0)
    top = out[0:8]
    for r in range(by):
        top = jnp.where(rowi == r, halo[8 - by + r:8 - by + r + 1, :], top)
    return jnp.concatenate([top, out[8:]], axis=0)


def _shift_up(x, halo, tile, by):
    out = pltpu.roll(x, tile - by, 0)
    rowi = lax.broadcasted_iota(jnp.int32, (8, 1), 0)
    bottom = out[tile - 8:]
    for r in range(by):
        bottom = jnp.where(rowi == 8 - by + r, halo[r:r + 1, :], bottom)
    return jnp.concatenate([out[:tile - 8], bottom], axis=0)


def _conv_specs(tile):
    ncb = D_FF // CONV_COLS
    per8 = tile // 8

    def tile_spec(off):
        return pl.BlockSpec((tile, CONV_COLS), functools.partial(lambda i, j, off: (i, j + off), off=off))

    def prev_spec(off):
        return pl.BlockSpec((8, CONV_COLS), functools.partial(lambda i, j, off: (jnp.maximum(i * per8 - 1, 0), j + off), off=off))

    def w_spec(off):
        return pl.BlockSpec((3, CONV_COLS), functools.partial(lambda i, j, off: (0, j + off), off=off))

    def b_spec(off):
        return pl.BlockSpec((1, CONV_COLS), functools.partial(lambda i, j, off: (0, j + off), off=off))

    return ncb, tile_spec, prev_spec, w_spec, b_spec


def _conv3(x, halo, w, b, tile):
    return w[0:1, :] * _shift_down(x, halo, tile, 2) + w[1:2, :] * _shift_down(x, halo, tile, 1) + w[2:3, :] * x + b


def _conv_act_fwd(u, conv_w, conv_b, lay):
    tile = lay.tile
    ncb, tile_spec, prev_spec, w_spec, b_spec = _conv_specs(tile)

    def body(ug, uv, pg, pv, wg, wv, bg, bv, o_ref):
        cg = _conv3(ug[...], pg, wg, bg[...], tile)
        cv = _conv3(uv[...], pv, wv, bv[...], tile)
        o_ref[...] = (_silu(cg) * cv).astype(o_ref.dtype)

    return pl.pallas_call(
        body, name="conv_act_fwd", grid=(lay.n // tile, ncb),
        in_specs=[tile_spec(0), tile_spec(ncb), prev_spec(0), prev_spec(ncb), w_spec(0), w_spec(ncb), b_spec(0), b_spec(ncb)],
        out_specs=pl.BlockSpec((tile, CONV_COLS), lambda i, j: (i, j)),
        out_shape=jax.ShapeDtypeStruct((lay.n, D_FF), MXU_DTYPE),
        compiler_params=_params(2),
    )(u, u, u, u, conv_w, conv_w, conv_b, conv_b)


def _conv_act_bwd(u, dact, conv_w, conv_b, lay):
    tile = lay.tile
    ncb, tile_spec, prev_spec, w_spec, b_spec = _conv_specs(tile)

    def body(ug, uv, pg, pv, wg, wv, bg, bv, da_ref, dg_ref, dv_ref, gwg, gwv, gbg, gbv):
        @pl.when(pl.program_id(1) == 0)
        def _():
            for r in (gwg, gwv, gbg, gbv):
                r[...] = jnp.zeros_like(r)

        xg, xv = ug[...], uv[...]
        cg = _conv3(xg, pg, wg, bg[...], tile)
        cv = _conv3(xv, pv, wv, bv[...], tile)
        da = da_ref[...].astype(F32)
        sg = _sigmoid(cg)
        dcv = da * (cg * sg)
        dcg = da * cv * (sg * (1.0 + cg * (1.0 - sg)))
        dg_ref[...] = dcg
        dv_ref[...] = dcv
        for x, halo, dc, gw, gb in ((xg, pg, dcg, gwg, gbg), (xv, pv, dcv, gwv, gbv)):
            gw[0, 0:1, :] += jnp.sum(dc * _shift_down(x, halo, tile, 2), axis=0, keepdims=True)
            gw[0, 1:2, :] += jnp.sum(dc * _shift_down(x, halo, tile, 1), axis=0, keepdims=True)
            gw[0, 2:3, :] += jnp.sum(dc * x, axis=0, keepdims=True)
            gb[0] += jnp.sum(dc, axis=0, keepdims=True)

    swap = lambda spec: pl.BlockSpec(spec.block_shape, functools.partial(lambda j, i, f: f(i, j), f=spec.index_map))
    col = lambda j, i: (i, j)
    red_w = pl.BlockSpec((1, 3, CONV_COLS), lambda j, i: (j, 0, 0))
    red_b = pl.BlockSpec((1, 1, CONV_COLS), lambda j, i: (j, 0, 0))
    outs = pl.pallas_call(
        body, name="conv_act_bwd", grid=(ncb, lay.n // tile),
        in_specs=[swap(s) for s in (tile_spec(0), tile_spec(ncb), prev_spec(0), prev_spec(ncb), w_spec(0), w_spec(ncb),
                                    b_spec(0), b_spec(ncb))] + [pl.BlockSpec((tile, CONV_COLS), col)],
        out_specs=[pl.BlockSpec((tile, CONV_COLS), col), pl.BlockSpec((tile, CONV_COLS), col), red_w, red_w, red_b, red_b],
        out_shape=[jax.ShapeDtypeStruct((lay.n, D_FF), F32), jax.ShapeDtypeStruct((lay.n, D_FF), F32),
                   jax.ShapeDtypeStruct((ncb, 3, CONV_COLS), F32), jax.ShapeDtypeStruct((ncb, 3, CONV_COLS), F32),
                   jax.ShapeDtypeStruct((ncb, 1, CONV_COLS), F32), jax.ShapeDtypeStruct((ncb, 1, CONV_COLS), F32)],
        compiler_params=_params(2),
    )(u, u, u, u, conv_w, conv_w, conv_b, conv_b, dact)
    dcg, dcv, gwg, gwv, gbg, gbv = outs
    unblock = lambda g: jnp.transpose(g, (1, 0, 2)).reshape(g.shape[1], D_FF)
    g_w = jnp.concatenate([unblock(gwg), unblock(gwv)], axis=1)
    g_b = jnp.concatenate([unblock(gbg), unblock(gbv)], axis=1)
    return dcg, dcv, g_w, g_b


def _conv_input_bwd(dcg, dcv, conv_w, lay):
    tile = lay.tile
    ncb = D_FF // CONV_COLS
    nblk8 = lay.n // 8
    per8 = tile // 8
    nxt = lambda i: jnp.minimum((i + 1) * per8, nblk8 - 1)

    def half(dc, off, into, name):
        def body(*refs):
            d, halo, w, o = refs[0], refs[1], refs[2], refs[-1]
            x = d[...]
            du = w[2:3, :] * x + w[1:2, :] * _shift_up(x, halo, tile, 1) + w[0:1, :] * _shift_up(x, halo, tile, 2)
            o[...] = jnp.where(lay.valid(pl.program_id(0), tile), du, 0.0).astype(o.dtype)

        in_specs = [pl.BlockSpec((tile, CONV_COLS), lambda i, j: (i, j)),
                    pl.BlockSpec((8, CONV_COLS), lambda i, j: (nxt(i), j)),
                    pl.BlockSpec((3, CONV_COLS), lambda i, j: (0, j + off))]
        args = [dc, dc, conv_w]
        if into is not None:
            in_specs.append(pl.BlockSpec(memory_space=pltpu.HBM))
            args.append(into)
        return pl.pallas_call(
            body, name=name, grid=(lay.n // tile, ncb), in_specs=in_specs,
            out_specs=pl.BlockSpec((tile, CONV_COLS), lambda i, j: (i, j + off)),
            out_shape=jax.ShapeDtypeStruct((lay.n, FF2), MXU_DTYPE),
            input_output_aliases={} if into is None else {3: 0},
            compiler_params=_params(2),
        )(*args)

    return half(dcv, ncb, half(dcg, 0, None, "conv_input_bwd_gate"), "conv_input_bwd_value")


def _loss_head(h1, mlp, target, lay):
    t, sub = 256, ROW0
    per = lay.lp // t
    nsub = t // sub
    nreal = lay.seq // sub

    def body(h_ref, m_ref, *rest):
        t_refs, (loss_ref, dy_ref, dyb_ref) = rest[:nsub], rest[nsub:]
        b, j = pl.program_id(0), pl.program_id(1)

        @pl.when((b == 0) & (j == 0))
        def _():
            loss_ref[...] = jnp.zeros_like(loss_ref)

        rows_ = j * t + lax.broadcasted_iota(jnp.int32, (t, 1), 0)
        real = (rows_ >= ROW0) & (rows_ < ROW0 + lay.seq)
        tgt_ = jnp.concatenate([r[...] for r in t_refs], axis=0)
        err = jnp.where(real, h_ref[...] + m_ref[...] - tgt_, 0.0)
        dy = err * (1.0 / D_MODEL)
        dy_ref[...] = dy
        dyb_ref[...] = dy.astype(dyb_ref.dtype)
        loss_ref[...] += 0.5 * jnp.sum(err * dy)

    rows = pl.BlockSpec((t, D_MODEL), lambda b, j: (b * per + j, 0))
    tgt = [pl.BlockSpec((sub, D_MODEL), functools.partial(
        lambda b, j, r: (b * nreal + jnp.clip(j * nsub + r - 1, 0, nreal - 1), 0), r=r)) for r in range(nsub)]
    return pl.pallas_call(
        body, name="loss_head", grid=(lay.batch, per),
        in_specs=[rows, rows] + tgt,
        out_specs=[pl.BlockSpec((8, LANES), lambda b, j: (0, 0)), rows, rows],
        out_shape=[jax.ShapeDtypeStruct((8, LANES), F32), jax.ShapeDtypeStruct((lay.n, D_MODEL), F32),
                   jax.ShapeDtypeStruct((lay.n, D_MODEL), MXU_DTYPE)],
        compiler_params=_params(2),
    )(h1, mlp, *([target] * nsub))


def _fox_prep(fq, fk, ff, gq, gk, bf, gmat, gmat_t, valid):
    q = _group_rms(fq, gq, gmat, gmat_t, FOX_DIM)
    k = _group_rms(fk, gk, gmat, gmat_t, FOX_DIM)
    logf = jnp.where(valid, _log_sigmoid(ff + bf), 0.0)
    return q, k, logf


def _hg_prep(hf, l0, l1):
    mx = jnp.maximum(l0, l1)
    e0, e1 = jnp.exp(l0 - mx), jnp.exp(l1 - mx)
    lb = e0 / (e0 + e1)
    lf = jnp.log(lb + (1.0 - lb) * _sigmoid(hf))
    kk = (1.0 - lb) * _sigmoid(-hf)
    return lf, kk


def _hg_post(o, hg, gain):
    return _head_rms(o, gain) * _silu(hg)


def _gate(ga, gb, ya, yb):
    return _sigmoid(ga) * ya + _sigmoid(gb) * yb


def _by_chip(g):
    return jnp.transpose(g.reshape(g.shape[0], N_CHIPS, g.shape[1] // N_CHIPS), (1, 0, 2))


def _from_chips(a):
    return jnp.transpose(a, (1, 0, 2)).reshape(a.shape[1], N_CHIPS * a.shape[2])


def _local_step(x, target, w, late_shards, c, mine, lay):
    n, tile = lay.n, lay.tile
    rw = functools.partial(_rowwise, n_rows=n, tile=tile)
    mx = lambda a: a.astype(MXU_DTYPE)

    w_in = w["w_in"]
    fq, fk, fv, ffw, hq, hf, hi, hg, ga, gb = jnp.split(w_in, list(np.cumsum([512, 512, 512, 8, 512, 512, 512, 512, 1024])), axis=1)
    w_main = mx(jnp.concatenate([ga, gb, fq, fk, fv, hq, hi, hf, hg], axis=1))
    w_ff = mx(jnp.pad(ffw, ((0, 0), (0, LANES - FOX_HEADS))))
    conv_w, conv_b = w["conv_w"].astype(F32), w["conv_b"].astype(F32)
    g1, g2 = w["norm1_gain"], w["norm2_gain"]
    gq, gk = jnp.tile(w["q_norm_gain"], (1, FOX_HEADS)), jnp.tile(w["k_norm_gain"], (1, FOX_HEADS))
    bf = jnp.pad(w["fox_b_f"], ((0, 0), (0, LANES - FOX_HEADS)))
    lb_logits, hg_gain = w["hg_lb_logits"], w["hg_out_gain"]
    gm64, gm64_t = _group_matrix(FOX_W, FOX_DIM)

    meta = jnp.broadcast_to(w["meta_tokens"].astype(F32)[None], (lay.batch, N_META, D_MODEL))
    h0 = jnp.concatenate([jnp.zeros((lay.batch, LEAD, D_MODEL), F32), meta, x,
                          jnp.zeros((lay.batch, lay.lp - LEAD - lay.l_real, D_MODEL), F32)], axis=1).reshape(n, D_MODEL)

    (xn,) = rw(lambda i, h, g: _rms(h, g), [h0], [g1], [(D_MODEL, MXU_DTYPE)], [], name="norm1")
    proj = _matmul(xn, w_main, name="proj_main")
    pff = _matmul(xn, w_ff, name="proj_ff")

    def fox_prep_fn(i, a, b_, v_, f_, gq_, gk_, bf_, m_, mt_):
        q_, k_, logf = _fox_prep(a, b_, f_, gq_, gk_, bf_, m_, mt_, lay.valid(i, tile))
        return q_, k_, v_, logf

    q, k, v, logf = rw(fox_prep_fn, [(proj, 512, C_FQ), (proj, 512, C_FK), (proj, 512, C_FV), pff], [gq, gk, bf, gm64, gm64_t],
                       [(512, MXU_DTYPE), (512, MXU_DTYPE), (512, MXU_DTYPE), (LANES, F32)], [], name="fox_prep")
    cum = _cumsum_rows(logf, lay, reverse=False, name="fox_cum")
    e1, e2, aug_ones = _aug_matrices()
    q_aug, k_aug = rw(lambda i, q_, k_, c_, e1_, e2_, on_: _fox_augment(q_, k_, c_, lay.valid(i, tile), e1_, e2_, on_),
                      [q, k, cum], [e1, e2, aug_ones], [(FOX_HEADS * AUG, MXU_DTYPE)] * 2, [], name="fox_aug")
    o_t, lse, late_slabs = _fox_fwd_t(q_aug, k_aug, v.T, late_shards, lay)
    w_a, w_b, w_out, w_up, w_down = [mx(a) for a in _assemble(LATE, late_shards, late_slabs)]

    def hg_prep_fn(i, hf_, l0, l1):
        lf, kk_ = _hg_prep(hf_, l0, l1)
        return kk_, _group_cumsum(lf, tile, reverse=False)

    lb0, lb1 = lb_logits[0:1], lb_logits[1:2]
    kk, gl = rw(hg_prep_fn, [(proj, 512, C_HF)], [lb0, lb1], [(512, F32), (512, F32)], [], name="hg_prep")
    o_hg, states = _hgrn_fwd(proj, kk, gl, lay)
    (oh,) = rw(lambda i, o, g_, gain: _hg_post(o, g_, gain), [o_hg, (proj, 512, C_HG)], [hg_gain], [(512, MXU_DTYPE)], [],
               name="hg_post")
    ya = _matmul(oh, w_a, out_dtype=MXU_DTYPE, name="branch_a")
    yb = _matmul(o_t, w_b, trans_a=True, out_dtype=MXU_DTYPE, name="branch_b")
    pga, pgb = (proj, 1024, C_GA), (proj, 1024, C_GB)
    gate_fn = lambda a, b_, c_, d_: _gate(a, b_, c_.astype(F32), d_.astype(F32))
    (merged,) = rw(lambda i, a, b_, c_, d_: gate_fn(a, b_, c_, d_), [pga, pgb, ya, yb], [], [(D_MODEL, MXU_DTYPE)], [], name="gate")
    mo = _matmul(merged, w_out, name="out_proj")
    h1, hn = rw(lambda i, h, m_, g: (h + m_, _rms(h + m_, g)), [h0, mo], [g2], [(D_MODEL, F32), (D_MODEL, MXU_DTYPE)], [],
                name="norm2")
    u = _matmul(hn, w_up, name="up_proj")
    act = _conv_act_fwd(u, conv_w, conv_b, lay)
    mlp = _matmul(act, w_down, name="down_proj")
    loss_blk, dy, dyb = _loss_head(h1, mlp, target.reshape(lay.batch * lay.seq, D_MODEL), lay)
    loss = loss_blk[0, 0]

    grads = {}
    dact = _matmul(dyb, w_down, trans_b=True, out_dtype=MXU_DTYPE, name="down_bwd_x")
    grads["w_down"] = _matmul(act, dyb, trans_a=True, name="down_bwd_w").reshape(N_CHIPS, D_FF // N_CHIPS, D_MODEL)
    dcg, dcv, grads["conv_w"], grads["conv_b"] = _conv_act_bwd(u, dact, conv_w, conv_b, lay)
    du = _conv_input_bwd(dcg, dcv, conv_w, lay)
    dhn = _matmul(du, w_up, trans_b=True, out_dtype=MXU_DTYPE, name="up_bwd_x")
    grads["w_up"] = _matmul(hn, du, trans_a=True, by_chip=True, name="up_bwd_w")

    def norm2_bwd(i, h, d_, dy_, g):
        _, vjp = jax.vjp(_rms, h, g)
        dh, dg = vjp(d_.astype(F32))
        return dh + dy_, dh + dy_, dg

    dh1, dh1b, grads["norm2_gain"] = rw(norm2_bwd, [h1, dhn, dy], [g2], [(D_MODEL, F32), (D_MODEL, MXU_DTYPE)], [(1, D_MODEL)],
                                        name="norm2_bwd")
    dmerged = _matmul(dh1b, w_out, trans_b=True, out_dtype=MXU_DTYPE, name="out_bwd_x")
    grads["w_out"] = _matmul(merged, dh1b, trans_a=True, name="out_bwd_w").reshape(N_CHIPS, D_MODEL // N_CHIPS, D_MODEL)

    def gate_bwd(i, a, b_, c_, d_, dm):
        _, vjp = jax.vjp(gate_fn, a, b_, c_, d_)
        da, db, dc, dd = vjp(dm.astype(F32))
        return jnp.concatenate([da, db], axis=1), dc, dd

    dproj, dya, dyb_ = rw(gate_bwd, [pga, pgb, ya, yb, dmerged], [], [(2 * D_MODEL, MXU_DTYPE)] + [(D_MODEL, MXU_DTYPE)] * 2, [],
                          name="gate_bwd", into=(MAIN_COLS, 0))
    doh = _matmul(dya, w_a, trans_b=True, out_dtype=MXU_DTYPE, name="branch_a_bwd_x")
    grads["w_branch_a"] = _matmul(oh, dya, trans_a=True, by_chip=True, name="branch_a_bwd_w")
    dofox = _matmul(dyb_, w_b, trans_b=True, out_dtype=MXU_DTYPE, name="branch_b_bwd_x")
    grads["w_branch_b"] = _matmul(o_t, dyb_, by_chip=True, name="branch_b_bwd_w")

    def hg_post_bwd(i, o, g_, d_, gain):
        _, vjp = jax.vjp(_hg_post, o, g_, gain)
        do_, dg_, dgain = vjp(d_.astype(F32))
        return dg_, do_, dgain

    dproj, do_hg, grads["hg_out_gain"] = rw(hg_post_bwd, [o_hg, (proj, 512, C_HG), doh], [hg_gain],
                                            [(512, MXU_DTYPE), (512, F32)], [(1, HG_DIM)], name="hg_post_bwd", into=(dproj, C_HG))
    dhq, dkk, dhi, dgl = _hgrn_bwd(proj, kk, gl, do_hg, states, lay)

    def hg_prep_bwd(i, hf_, dkk_, dgl_, l0, l1):
        _, vjp = jax.vjp(_hg_prep, hf_, l0, l1)
        return vjp((_group_cumsum(dgl_, tile, reverse=True), dkk_))

    dproj, g_lb0, g_lb1 = rw(hg_prep_bwd, [(proj, 512, C_HF), dkk, dgl], [lb0, lb1], [(512, MXU_DTYPE)], [(1, HG_W), (1, HG_W)],
                             name="hg_prep_bwd", into=(dproj, C_HF))
    grads["hg_lb_logits"] = jnp.concatenate([g_lb0, g_lb1], axis=0)

    k_t = (k.astype(F32) * FOX_SCALE).astype(MXU_DTYPE).T.reshape(FOX_HEADS, FOX_DIM, n)
    k_t = jnp.concatenate([k_t, jnp.ones((FOX_HEADS, KT_ROWS - FOX_DIM, n), MXU_DTYPE)], axis=1).reshape(FOX_HEADS * KT_ROWS, n)
    late_grads = [grads.pop(n_) for n_ in LATE]
    parts = [_add_own_half(g_, s_, c, BF16, name=f"reduce_add2_{n_}")
             for n_, g_, s_ in zip(LATE, late_grads, _sibling_exchange(late_grads, "reduce_sibling_late"))]
    dq_t, dk_aug, dv, from_chips = _fox_bwd_t(q_aug, k_aug, v, dofox, k_t, o_t, dofox.T, lse, parts, lay)
    reduced = [_add_chips(p_, g_, mine, name=f"reduce_add4_{n_}") for n_, p_, g_ in zip(LATE, parts, from_chips)]
    dq_t = dq_t.reshape(FOX_HEADS, KT_ROWS, n)
    dq = dq_t[:, :FOX_DIM].reshape(FOX_W, n).T
    dk_aug = dk_aug.reshape(n, FOX_HEADS, AUG)
    dk = dk_aug[:, :, :FOX_DIM].reshape(n, FOX_W)
    dcum = jnp.pad(dq_t[:, FOX_DIM].T - dk_aug[:, :, FOX_DIM], ((0, 0), (0, LANES - FOX_HEADS)))
    dlogf = _cumsum_rows(dcum, lay, reverse=True, name="fox_cum_bwd")

    def fox_prep_bwd(i, a, b_, f_, dq_, dk_, dl_, gq_, gk_, bf_, m_, mt_):
        valid = lay.valid(i, tile)
        _, vjp = jax.vjp(lambda a_, b__, f__, gq__, gk__, bf__: _fox_prep(a_, b__, f__, gq__, gk__, bf__, m_, mt_, valid),
                         a, b_, f_, gq_, gk_, bf_)
        da, db, df, dgq, dgk, dbf = vjp((dq_, dk_, dl_))
        return jnp.concatenate([da, db], axis=1), df, dgq, dgk, dbf

    dproj, dff, g_gq, g_gk, g_bf = rw(
        fox_prep_bwd, [(proj, 512, C_FQ), (proj, 512, C_FK), pff, dq, dk, dlogf], [gq, gk, bf, gm64, gm64_t],
        [(2 * FOX_W, MXU_DTYPE), (LANES, MXU_DTYPE)], [(1, FOX_W), (1, FOX_W), (1, LANES)], name="fox_prep_bwd",
        into=(dproj, C_FQ // 2))
    grads["q_norm_gain"] = g_gq.reshape(FOX_HEADS, FOX_DIM).sum(0, keepdims=True)
    grads["k_norm_gain"] = g_gk.reshape(FOX_HEADS, FOX_DIM).sum(0, keepdims=True)
    grads["fox_b_f"] = g_bf[:, :FOX_HEADS]

    (dproj,) = rw(lambda i, a, b_, c_: jnp.concatenate([a, b_, c_], axis=1), [dv, dhq, dhi], [], [(3 * 512, MXU_DTYPE)], [],
                  name="dproj_cast", into=(dproj, C_FV // 3))
    dxn = _matmul(dproj, w_main, trans_b=True, out_dtype=MXU_DTYPE, name="proj_bwd_x")
    dxn_ff = _matmul(dff, w_ff, trans_b=True, out_dtype=MXU_DTYPE, name="proj_ff_bwd_x")
    g_main = _matmul(xn, dproj, trans_a=True, name="proj_bwd_w")
    g_ff = _matmul(xn, dff, trans_a=True, name="proj_ff_bwd_w")[:, :FOX_HEADS]
    p = jnp.split(g_main, list(np.cumsum([1024, 1024] + [512] * 6)), axis=1)
    grads["w_in"] = _by_chip(jnp.concatenate([p[2], p[3], p[4], g_ff, p[5], p[7], p[6], p[8], p[0], p[1]], axis=1))

    per = lay.lp // tile

    def norm1_bwd(i, h, d1, d2, dh1_, g):
        _, vjp = jax.vjp(_rms, h, g)
        dh, dg = vjp(d1.astype(F32) + d2.astype(F32))
        dh = dh + dh1_
        dmeta = jnp.where(lax.rem(i, per) == 0, dh[LEAD:LEAD + N_META, :], 0.0)
        return dh, dg, dmeta

    dh0, grads["norm1_gain"], grads["meta_tokens"] = rw(norm1_bwd, [h0, dxn, dxn_ff, dh1], [g1], [(D_MODEL, F32)],
                                                       [(1, D_MODEL), (N_META, D_MODEL)], name="norm1_bwd")
    grad_x = dh0.reshape(lay.batch, lay.lp, D_MODEL)[:, ROW0:ROW0 + lay.seq]
    return loss, grad_x, grads, reduced


MESH = pl.DeviceIdType.MESH
HBM_SPEC = pl.BlockSpec(memory_space=pltpu.HBM)
WEIGHT_NAMES = ["meta_tokens", "norm1_gain", "w_in", "fox_b_f", "q_norm_gain", "k_norm_gain", "hg_lb_logits", "hg_out_gain",
                "w_branch_a", "w_branch_b", "w_out", "norm2_gain", "w_up", "conv_w", "conv_b", "w_down"]
BIG = ("w_in", "w_branch_a", "w_branch_b", "w_out", "w_up", "w_down")
BIG_COL_SHARDED = ("w_in", "w_branch_a", "w_branch_b", "w_up")
LATE = BIG[1:]
SMALL = tuple(n for n in WEIGHT_NAMES if n not in BIG)
SMALL_SHARDED = ("meta_tokens", "conv_w")
SMALL_ROWS = 144
GATHER_SMALL_ROWS = 80


def _position():
    return lax.axis_index("x"), lax.axis_index("y"), lax.axis_index("c")


def _other_chips(x, y):
    return [(1 - x, y), (x, 1 - y), (1 - x, 1 - y)]


def _scalar(v):
    return jnp.reshape(v, (1,)).astype(jnp.int32)


def _row_tile(rows, cols):
    width = -(-cols // LANES) * LANES * 4
    best = 8
    for d in range(8, rows + 1, 8):
        if rows % d == 0 and d * width <= (1 << 20):
            best = d
    return best


class _Gather:
    def __init__(self, xs, outs, send_sems, recv_sems):
        self.xs, self.outs, self.send_sems, self.recv_sems = xs, outs, send_sems, recv_sems
        self.na = len(xs)
        self.x, self.y, self.c = _position()
        self.me, self.sibling = (self.x, self.y, self.c), (self.x, self.y, 1 - self.c)
        self.chips = _other_chips(self.x, self.y)

    def _copy(self, a, k, block, to, own=False):
        dst = self.outs[a].at[4 * block[0] + 2 * block[1] + block[2]]
        src = dst
        if own:
            half = self.xs[a].shape[0] // 2
            src = self.xs[a].at[pl.ds(pl.multiple_of(self.c * half, 8), half), :]
        return pltpu.make_async_remote_copy(src_ref=src, dst_ref=dst, send_sem=self.send_sems.at[a, k],
                                            recv_sem=self.recv_sems.at[a, k], device_id=to, device_id_type=MESH)

    def _firsts(self):
        return [self._copy(a, j, self.me, (*chip, self.c), own=True) for a in range(self.na) for j, chip in enumerate(self.chips)]

    def _relays(self):
        return [self._copy(a, 3 + j, (*chip, self.c), self.sibling) for j, chip in enumerate(self.chips) for a in range(self.na)]

    def start(self):
        for cp in self._firsts():
            cp.start()

    def relay(self):
        for j, chip in enumerate(self.chips):
            for a in range(self.na):
                self._copy(a, j, (*chip, self.c), self.me).wait_recv()
                self._copy(a, 3 + j, (*chip, self.c), self.sibling).start()

    def finish(self):
        for a in range(self.na):
            for j, chip in enumerate(self.chips):
                self._copy(a, 3 + j, (*chip, 1 - self.c), self.me).wait_recv()
        for cp in self._firsts() + self._relays():
            cp.wait_send()

    @staticmethod
    def out_shapes(shards):
        return [jax.ShapeDtypeStruct((8, s.shape[0] // 2, s.shape[1]), s.dtype) for s in shards]

    @staticmethod
    def semaphores(na):
        return [pltpu.SemaphoreType.DMA((na, 6)), pltpu.SemaphoreType.DMA((na, 6))]


def _gather_shards(shards):
    na = len(shards)

    def body(*refs):
        g = _Gather(refs[:na], refs[na:2 * na], refs[2 * na], refs[2 * na + 1])
        g.start()
        g.relay()
        g.finish()

    return pl.pallas_call(
        body, name="gather_weights", out_shape=_Gather.out_shapes(shards),
        in_specs=[HBM_SPEC] * na, out_specs=[HBM_SPEC] * na, scratch_shapes=_Gather.semaphores(na),
    )(*shards)


def _sibling_exchange(gs, name):
    na = len(gs)
    halves = [g.shape[1] // 2 for g in gs]

    def body(*refs):
        srcs, gots, send_sems, recv_sems = refs[:na], refs[na:2 * na], refs[2 * na], refs[2 * na + 1]
        x, y, c = _position()
        copies = [pltpu.make_async_remote_copy(
            src_ref=srcs[a].at[:, pl.ds(pl.multiple_of((1 - c) * halves[a], 8), halves[a]), :], dst_ref=gots[a],
            send_sem=send_sems.at[a], recv_sem=recv_sems.at[a], device_id=(x, y, 1 - c), device_id_type=MESH) for a in range(na)]
        for cp in copies:
            cp.start()
        for cp in copies:
            cp.wait()

    return pl.pallas_call(
        body, name=name,
        out_shape=[jax.ShapeDtypeStruct((N_CHIPS, h, g.shape[2]), g.dtype) for h, g in zip(halves, gs)],
        in_specs=[HBM_SPEC] * na, out_specs=[HBM_SPEC] * na,
        scratch_shapes=[pltpu.SemaphoreType.DMA((na,)), pltpu.SemaphoreType.DMA((na,))],
    )(*gs)


class _ChipExchange:
    def __init__(self, srcs, gots, send_sems, recv_sems):
        self.srcs, self.gots, self.send_sems, self.recv_sems = srcs, gots, send_sems, recv_sems
        self.na = len(srcs)
        x, y, self.c = _position()
        self.mine = 2 * x + y
        self.chips = _other_chips(x, y)

    def _copy(self, a, j, arriving):
        cx, cy = self.chips[j]
        theirs = 2 * cx + cy
        src = self.srcs[a].at[self.mine if arriving else theirs]
        dst = self.gots[a].at[theirs if arriving else self.mine]
        return pltpu.make_async_remote_copy(src_ref=src, dst_ref=dst, send_sem=self.send_sems.at[a, j],
                                            recv_sem=self.recv_sems.at[a, j], device_id=(cx, cy, self.c), device_id_type=MESH)

    def start(self):
        for a in range(self.na):
            for j in range(3):
                self._copy(a, j, False).start()

    def finish(self):
        for a in range(self.na):
            for j in range(3):
                self._copy(a, j, True).wait_recv()
        for a in range(self.na):
            for j in range(3):
                self._copy(a, j, False).wait_send()

    @staticmethod
    def semaphores(na):
        return [pltpu.SemaphoreType.DMA((na, 3)), pltpu.SemaphoreType.DMA((na, 3))]


def _chip_exchange(parts):
    na = len(parts)

    def body(*refs):
        ex = _ChipExchange(refs[:na], refs[na:2 * na], refs[2 * na], refs[2 * na + 1])
        ex.start()
        ex.finish()

    return pl.pallas_call(
        body, name="reduce_chips", out_shape=[jax.ShapeDtypeStruct(p.shape, p.dtype) for p in parts],
        in_specs=[HBM_SPEC] * na, out_specs=[HBM_SPEC] * na, scratch_shapes=_ChipExchange.semaphores(na),
    )(*parts)


def _sibling_send(halves):
    na = len(halves)

    def body(*refs):
        srcs, gots, send_sems, recv_sems = refs[:na], refs[na:2 * na], refs[2 * na], refs[2 * na + 1]
        x, y, c = _position()
        copies = [pltpu.make_async_remote_copy(src_ref=srcs[a], dst_ref=gots[a], send_sem=send_sems.at[a], recv_sem=recv_sems.at[a],
                                               device_id=(x, y, 1 - c), device_id_type=MESH) for a in range(na)]
        for cp in copies:
            cp.start()
        for cp in copies:
            cp.wait()

    return pl.pallas_call(
        body, name="reduce_gather", out_shape=[jax.ShapeDtypeStruct(h.shape, h.dtype) for h in halves],
        in_specs=[HBM_SPEC] * na, out_specs=[HBM_SPEC] * na,
        scratch_shapes=[pltpu.SemaphoreType.DMA((na,)), pltpu.SemaphoreType.DMA((na,))],
    )(*halves)


def _add_own_half(g, got, c, dtype, name):
    _, r, cols = g.shape
    r2 = r // 2
    tr = _row_tile(r2, cols)
    nrt = r2 // tr

    def body(c_ref, g_ref, got_ref, o_ref):
        o_ref[...] = (g_ref[...] + got_ref[...]).astype(o_ref.dtype)

    blk = (1, tr, cols)
    return pl.pallas_call(
        body, name=name,
        grid_spec=pltpu.PrefetchScalarGridSpec(
            num_scalar_prefetch=1, grid=(N_CHIPS, nrt),
            in_specs=[pl.BlockSpec(blk, lambda j, i, c_: (j, c_[0] * nrt + i, 0)), pl.BlockSpec(blk, lambda j, i, c_: (j, i, 0))],
            out_specs=pl.BlockSpec(blk, lambda j, i, c_: (j, i, 0))),
        out_shape=jax.ShapeDtypeStruct((N_CHIPS, r2, cols), dtype), compiler_params=_params(2),
    )(c, g, got)


def _add_chips(part, got, mine, name):
    _, r2, cols = part.shape
    tr = _row_tile(r2, cols)

    def body(m_ref, p_ref, g0, g1, g2, g3, o_ref):
        t = [jnp.where(m_ref[0] == k, p_ref[0], g[0]).astype(F32) for k, g in enumerate((g0, g1, g2, g3))]
        o_ref[...] = ((t[0] + t[1]) + t[2]) + t[3]

    blk = (1, tr, cols)
    others = [pl.BlockSpec(blk, functools.partial(lambda i, m, k: (jnp.where(m[0] == k, (k + 1) % N_CHIPS, k), i, 0), k=k))
              for k in range(N_CHIPS)]
    return pl.pallas_call(
        body, name=name,
        grid_spec=pltpu.PrefetchScalarGridSpec(
            num_scalar_prefetch=1, grid=(r2 // tr,),
            in_specs=[pl.BlockSpec(blk, lambda i, m: (m[0], i, 0))] + others,
            out_specs=pl.BlockSpec((tr, cols), lambda i, m: (i, 0))),
        out_shape=jax.ShapeDtypeStruct((r2, cols), F32), compiler_params=_params(1),
    )(mine, part, got, got, got, got)


def _adamw(w, own, other, m, v, c, name):
    r, cols = w.shape
    r2 = r // 2
    tr = _row_tile(r2, cols)
    nrt = r2 // tr
    c1 = 1.0 / (1.0 - ADAM_B1 ** ADAM_STEP)
    c2 = 1.0 / (1.0 - ADAM_B2 ** ADAM_STEP)

    def body(c_ref, w_ref, own_ref, other_ref, m_ref, v_ref, g_out, d_out, m_out, v_out):
        g_ = jnp.where(pl.program_id(0) == c_ref[0], own_ref[...], other_ref[...])
        m_new = ADAM_B1 * m_ref[...] + (1.0 - ADAM_B1) * g_
        v_new = ADAM_B2 * v_ref[...] + (1.0 - ADAM_B2) * (g_ * g_)
        g_out[...] = g_
        d_out[...] = -ADAM_LR * ((m_new * c1) / (jnp.sqrt(v_new * c2) + ADAM_EPS) + ADAM_WD * w_ref[...])
        m_out[...] = m_new
        v_out[...] = v_new

    full = pl.BlockSpec((tr, cols), lambda h, i, c_: (h * nrt + i, 0))
    half = pl.BlockSpec((tr, cols), lambda h, i, c_: (i, 0))
    out = jax.ShapeDtypeStruct((r, cols), F32)
    return pl.pallas_call(
        body, name=name,
        grid_spec=pltpu.PrefetchScalarGridSpec(num_scalar_prefetch=1, grid=(2, nrt), in_specs=[full, half, half, full, full],
                                               out_specs=[full] * 4),
        out_shape=[out] * 4, compiler_params=_params(2),
    )(c, w, own, other, m, v)


def _to_rows(flat, rows):
    return jnp.pad(flat, (0, rows * LANES - flat.shape[0])).reshape(rows, LANES)


def _pack_small(tree):
    return _to_rows(jnp.concatenate([tree[n].astype(F32).reshape(-1) for n in SMALL]), SMALL_ROWS)


def _unpack_small(packed, shapes):
    flat, out, at = packed.reshape(-1), {}, 0
    for n in SMALL:
        size = int(np.prod(shapes[n]))
        out[n] = flat[at:at + size].reshape(shapes[n])
        at += size
    return out


def _pack_small_by_chip(grads):
    pieces = []
    for n in SMALL:
        g = grads[n].astype(F32)
        if n in SMALL_SHARDED:
            pieces.append(_by_chip(g).reshape(N_CHIPS, -1))
        else:
            pieces.append(jnp.broadcast_to(g.reshape(1, -1), (N_CHIPS, g.size)))
    flat = jnp.concatenate(pieces, axis=1)
    return jnp.pad(flat, ((0, 0), (0, SMALL_ROWS * LANES - flat.shape[1]))).reshape(N_CHIPS, SMALL_ROWS, LANES)


def _bf16_shard(local, n):
    return local[n].reshape(local[n].shape[-2:]).astype(BF16)


def _all_chips(shard, slabs):
    x, y, _ = _position()
    is_mine = (lax.broadcasted_iota(jnp.int32, (N_CHIPS, 1, 1), 0) == 2 * x + y)
    return jnp.where(is_mine, shard[None], slabs.reshape((N_CHIPS,) + shard.shape))


def _assemble(names, shards, slabs):
    full = [_all_chips(s, g) for s, g in zip(shards, slabs)]
    return [_from_chips(f) if n in BIG_COL_SHARDED else f.reshape(N_CHIPS * f.shape[1], f.shape[2]) for n, f in zip(names, full)]


def _gather_first(local):
    shards = [_bf16_shard(local, "w_in"),
              _to_rows(jnp.concatenate([local[n].astype(F32).reshape(-1) for n in SMALL_SHARDED]), GATHER_SMALL_ROWS)]
    slabs = _gather_shards(shards)
    out = {"w_in": _assemble(["w_in"], shards[:1], slabs[:1])[0]}
    flat, at = _all_chips(shards[1], slabs[1]).reshape(N_CHIPS, -1), 0
    for n in SMALL_SHARDED:
        shape = local[n].shape[-2:]
        size = int(np.prod(shape))
        out[n] = _from_chips(flat[:, at:at + size].reshape((N_CHIPS,) + shape))
        at += size
    return out


def kernel(x, meta_tokens, norm1_gain, w_in, fox_b_f, q_norm_gain, k_norm_gain, hg_lb_logits, hg_out_gain, w_branch_a, w_branch_b, w_out, norm2_gain, w_up, conv_w, conv_b, w_down, loss_target, m_meta_tokens, m_norm1_gain, m_w_in, m_fox_b_f, m_q_norm_gain, m_k_norm_gain, m_hg_lb_logits, m_hg_out_gain, m_w_branch_a, m_w_branch_b, m_w_out, m_norm2_gain, m_w_up, m_conv_w, m_conv_b, m_w_down, v_meta_tokens, v_norm1_gain, v_w_in, v_fox_b_f, v_q_norm_gain, v_k_norm_gain, v_hg_lb_logits, v_hg_out_gain, v_w_branch_a, v_w_branch_b, v_w_out, v_norm2_gain, v_w_up, v_conv_w, v_conv_b, v_w_down):
    w_loc = dict(zip(WEIGHT_NAMES, (meta_tokens, norm1_gain, w_in, fox_b_f, q_norm_gain, k_norm_gain, hg_lb_logits, hg_out_gain,
                                    w_branch_a, w_branch_b, w_out, norm2_gain, w_up, conv_w, conv_b, w_down)))
    m_loc = dict(zip(WEIGHT_NAMES, (m_meta_tokens, m_norm1_gain, m_w_in, m_fox_b_f, m_q_norm_gain, m_k_norm_gain, m_hg_lb_logits,
                                    m_hg_out_gain, m_w_branch_a, m_w_branch_b, m_w_out, m_norm2_gain, m_w_up, m_conv_w, m_conv_b,
                                    m_w_down)))
    v_loc = dict(zip(WEIGHT_NAMES, (v_meta_tokens, v_norm1_gain, v_w_in, v_fox_b_f, v_q_norm_gain, v_k_norm_gain, v_hg_lb_logits,
                                    v_hg_out_gain, v_w_branch_a, v_w_branch_b, v_w_out, v_norm2_gain, v_w_up, v_conv_w, v_conv_b,
                                    v_w_down)))
    local_shapes = {n: tuple(w_loc[n].shape) for n in WEIGHT_NAMES}
    px, py, pc = _position()
    c, mine = _scalar(pc), _scalar(2 * px + py)

    weights = {n: w_loc[n].reshape(w_loc[n].shape[-2:]) for n in SMALL if n not in SMALL_SHARDED}
    weights.update(_gather_first(w_loc))

    lay = _Layout(x.shape[0], x.shape[1])
    loss, grad_x, grads, reduced_late = _local_step(x, loss_target, weights, [_bf16_shard(w_loc, n) for n in LATE], c, mine, lay)
    loss = lax.psum(loss, ("x", "y", "c"))

    names = ["w_in", "small"]
    by_chip = [grads["w_in"], _pack_small_by_chip(grads)]
    from_sibling = _sibling_exchange(by_chip, "reduce_sibling")
    parts = [_add_own_half(g, s, c, F32 if n == "small" else BF16, name=f"reduce_add2_{n}")
             for n, g, s in zip(names, by_chip, from_sibling)]
    from_chips = _chip_exchange(parts)
    own = [_add_chips(p, g, mine, name=f"reduce_add4_{n}") for n, p, g in zip(names, parts, from_chips)]
    names = list(BIG) + ["small"]
    own = [own[0]] + reduced_late + [own[1]]
    other = _sibling_send(own)

    two_d = lambda t: [t[n].reshape(t[n].shape[-2:]) for n in BIG] + [_pack_small(t)]
    results = [_adamw(w_, o_, t_, m_, v_, c, name=f"adamw_{n}")
               for n, w_, o_, t_, m_, v_ in zip(names, two_d(w_loc), own, other, two_d(m_loc), two_d(v_loc))]
    outs = []
    for kind in range(4):
        tree = {n: results[i][kind].reshape(local_shapes[n]) for i, n in enumerate(BIG)}
        tree.update(_unpack_small(results[-1][kind], local_shapes))
        outs += [tree[n] for n in WEIGHT_NAMES]
    return (loss, grad_x, *outs)
```
